```python
import math
import jax, jax.numpy as jnp
from jax import lax
import numpy as np

D_MODEL = 1024
BATCH = 8
SEQ = 16384
DEPTH = 4

N_A = DEPTH // 2
N_B = DEPTH - N_A
MEM_TOKENS = 256
MEM_HEADS = 4
MEM_DH = D_MODEL // 16
MEM_W = MEM_HEADS * MEM_DH
DN_DK = 128
DN_DV = 128
DN_HEADS = (3 * D_MODEL) // (4 * DN_DV)
DN_QK_W = DN_HEADS * DN_DK
DN_V_W = DN_HEADS * DN_DV
CONV_WIDTH = 4
CHUNK = 64
SWA_DH = 64
SWA_HEADS = (3 * D_MODEL) // (4 * SWA_DH)
SWA_KV_HEADS = 2
SWA_GROUP = SWA_HEADS // SWA_KV_HEADS
SWA_Q_W = SWA_HEADS * SWA_DH
SWA_KV_W = SWA_KV_HEADS * SWA_DH
WINDOW = 128
ROPE_THETA = 10000.0
MLP_HIDDEN = 4 * D_MODEL
LN_EPS = 1e-5
NORM_EPS = 1e-6
DN_ALPHA = (2.0 * DEPTH) ** 0.25
DN_BETA = (8.0 * DEPTH) ** -0.25
A_IN = 2 * DN_QK_W + 2 * DN_V_W + 2 * DN_HEADS + MEM_W
B_IN = SWA_Q_W + MEM_W
MIX_W = DN_V_W + MEM_W

kernel_name = "yoco_deltanet_swa_sink_memory_trunk"


def layer_norm(x, g, b):
    xf = x.astype(jnp.float32)
    mu = jnp.mean(xf, axis=-1, keepdims=True)
    var = jnp.mean(jnp.square(xf - mu), axis=-1, keepdims=True)
    y = (xf - mu) * lax.rsqrt(var + LN_EPS) * g.astype(jnp.float32) + b.astype(jnp.float32)
    return y.astype(x.dtype)


def l2_normalize(x):
    xf = x.astype(jnp.float32)
    return xf * lax.rsqrt(jnp.sum(xf * xf, axis=-1, keepdims=True) + NORM_EPS)


def rope_tables(positions, dh):
    inv_freq = ROPE_THETA ** (-jnp.arange(0, dh, 2, dtype=jnp.float32) / dh)
    ang = positions.astype(jnp.float32)[..., None] * inv_freq
    return jnp.cos(ang)[:, :, None, :], jnp.sin(ang)[:, :, None, :]


def apply_rope(x, cos, sin):
    xf = x.astype(jnp.float32)
    x1, x2 = jnp.split(xf, 2, axis=-1)
    out = jnp.concatenate([x1 * cos - x2 * sin, x2 * cos + x1 * sin], axis=-1)
    return out.astype(x.dtype)


def causal_depthwise_conv(x, w):
    C = x.shape[-1]
    return lax.conv_general_dilated(
        x, w[:, None, :].astype(x.dtype), window_strides=(1,),
        padding=[(CONV_WIDTH - 1, 0)], dimension_numbers=('NWC', 'WIO', 'NWC'),
        feature_group_count=C)


def gated_delta_rule(q, k, v, g, beta):
    B, S, H, DK = q.shape
    DV = v.shape[-1]
    N = S // CHUNK
    f32 = jnp.float32

    def to_chunks(t):
        t = t.astype(f32).reshape((B, N, CHUNK, H) + t.shape[3:])
        return jnp.moveaxis(t, 3, 1)

    q = to_chunks(q) * (DK ** -0.5)
    k = to_chunks(k)
    v = to_chunks(v)
    beta = to_chunks(beta)
    g = jnp.cumsum(to_chunks(g), axis=-1)
    incl = jnp.tril(jnp.ones((CHUNK, CHUNK), bool))
    strict = jnp.tril(jnp.ones((CHUNK, CHUNK), bool), -1)
    decay = jnp.exp(jnp.where(incl, g[..., :, None] - g[..., None, :], -jnp.inf))
    k_beta = k * beta[..., None]
    L = jnp.where(strict, jnp.einsum('bhnid,bhnjd->bhnij', k_beta, k) * decay, 0.0)
    rhs = jnp.concatenate([v * beta[..., None], k_beta * jnp.exp(g)[..., None]], axis=-1)
    sol = lax.linalg.triangular_solve(L, rhs, left_side=True, lower=True, unit_diagonal=True)
    u, w = sol[..., :DV], sol[..., DV:]
    intra = jnp.where(incl, jnp.einsum('bhnid,bhnjd->bhnij', q, k) * decay, 0.0)
    q_dec = q * jnp.exp(g)[..., None]
    k_dec = k * jnp.exp(g[..., -1:] - g)[..., None]
    chunk_decay = jnp.exp(g[..., -1])

    def step(state, inp):
        qd, kd, uc, wc, ac, cd = inp
        v_new = uc - jnp.einsum('bhik,bhkv->bhiv', wc, state)
        out = jnp.einsum('bhik,bhkv->bhiv', qd, state) + jnp.einsum('bhij,bhjv->bhiv', ac, v_new)
        state = state * cd[..., None, None] + jnp.einsum('bhik,bhiv->bhkv', kd, v_new)
        return state, out

    xs = (jnp.moveaxis(q_dec, 2, 0), jnp.moveaxis(k_dec, 2, 0), jnp.moveaxis(u, 2, 0),
          jnp.moveaxis(w, 2, 0), jnp.moveaxis(intra, 2, 0), jnp.moveaxis(chunk_decay, 2, 0))
    s0 = jnp.zeros((B, H, DK, DV), f32)
    _, out = lax.scan(step, s0, xs)
    return jnp.transpose(out, (1, 0, 3, 2, 4)).reshape(B, S, H, DV)


def sliding_window_sink_attention(q, k, v, sinks):
    B, S, HKV, G, dh = q.shape
    NB = S // WINDOW
    qb = q.reshape(B, NB, WINDOW, HKV, G, dh)

    def band_keys(t):
        tb = t.reshape(B, NB, WINDOW, HKV, dh)
        prev = jnp.pad(tb, ((0, 0), (1, 0), (0, 0), (0, 0), (0, 0)))[:, :-1]
        return jnp.concatenate([prev, tb], axis=2)

    kk = band_keys(k)
    vv = band_keys(v)
    s = jnp.einsum('bnqhgd,bnkhd->bnhgqk', qb, kk).astype(jnp.float32) * (dh ** -0.5)
    qi = jnp.arange(WINDOW)[:, None]
    kj = jnp.arange(2 * WINDOW)[None, :]
    diff = qi + WINDOW - kj
    band = (diff >= 0) & (diff < WINDOW)
    key_pos = jnp.arange(NB)[:, None] * WINDOW - WINDOW + kj
    valid = band[None] & (key_pos >= 0)[:, None, :]
    s = jnp.where(valid[None, :, None, None], s, -jnp.inf)
    sink = sinks.astype(jnp.float32).reshape(HKV, G)[None, None, :, :, None, None]
    m = jnp.maximum(jnp.max(s, axis=-1, keepdims=True), sink)
    p = jnp.exp(s - m)
    denom = jnp.sum(p, axis=-1, keepdims=True) + jnp.exp(sink - m)
    probs = (p / denom).astype(v.dtype)
    o = jnp.einsum('bnhgqk,bnkhd->bnqhgd', probs, vv)
    return o.reshape(B, S, HKV * G * dh)


def memory_cross_attention(qm, mem, w_kv):
    B, S, _ = qm.shape
    M = mem.shape[1]
    kv = mem @ w_kv
    k = kv[..., :MEM_W].reshape(B, M, MEM_HEADS, MEM_DH)
    v = kv[..., MEM_W:].reshape(B, M, MEM_HEADS, MEM_DH)
    q = qm.reshape(B, S, MEM_HEADS, MEM_DH)
    s = jnp.einsum('bshd,bmhd->bhsm', q, k).astype(jnp.float32) * (MEM_DH ** -0.5)
    p = jax.nn.softmax(s, axis=-1).astype(v.dtype)
    return jnp.einsum('bhsm,bmhd->bshd', p, v).reshape(B, S, MEM_W)


def mixer_a(h, mem, w_in, conv_w, A_log, dt_bias, norm_w, mem_w_kv, w_o):
    B, S, _ = h.shape
    proj = h @ w_in
    c1 = 2 * DN_QK_W + DN_V_W
    qkv = proj[..., :c1]
    z = proj[..., c1:c1 + DN_V_W]
    a = proj[..., c1 + DN_V_W:c1 + DN_V_W + DN_HEADS]
    b = proj[..., c1 + DN_V_W + DN_HEADS:c1 + DN_V_W + 2 * DN_HEADS]
    qm = proj[..., c1 + DN_V_W + 2 * DN_HEADS:]
    qkv = jax.nn.silu(causal_depthwise_conv(qkv, conv_w))
    q = l2_normalize(qkv[..., :DN_QK_W].reshape(B, S, DN_HEADS, DN_DK))
    k = l2_normalize(qkv[..., DN_QK_W:2 * DN_QK_W].reshape(B, S, DN_HEADS, DN_DK))
    v = qkv[..., 2 * DN_QK_W:].reshape(B, S, DN_HEADS, DN_DV)
    beta = jax.nn.sigmoid(b.astype(jnp.float32))
    g = -jnp.exp(A_log.astype(jnp.float32)) * jax.nn.softplus(a.astype(jnp.float32) + dt_bias.astype(jnp.float32))
    o = gated_delta_rule(q, k, v, g, beta)
    o = o * lax.rsqrt(jnp.mean(o * o, axis=-1, keepdims=True) + NORM_EPS) * norm_w.astype(jnp.float32)
    o = o * jax.nn.silu(z.astype(jnp.float32).reshape(B, S, DN_HEADS, DN_DV))
    o = o.astype(h.dtype).reshape(B, S, DN_V_W)
    mo = memory_cross_attention(qm, mem, mem_w_kv)
    return jnp.concatenate([o, mo], axis=-1) @ w_o


def mixer_b(h, mem, k_sh, v_sh, cos, sin, w_in, sinks, mem_w_kv, w_o):
    B, S, _ = h.shape
    proj = h @ w_in
    q = apply_rope(proj[..., :SWA_Q_W].reshape(B, S, SWA_HEADS, SWA_DH), cos, sin)
    q = q.reshape(B, S, SWA_KV_HEADS, SWA_GROUP, SWA_DH)
    o = sliding_window_sink_attention(q, k_sh, v_sh, sinks)
    mo = memory_cross_attention(proj[..., SWA_Q_W:], mem, mem_w_kv)
    return jnp.concatenate([o, mo], axis=-1) @ w_o


def shared_kv(h, w_kv, cos, sin):
    B, S, _ = h.shape
    kv = h @ w_kv
    k = apply_rope(kv[..., :SWA_KV_W].reshape(B, S, SWA_KV_HEADS, SWA_DH), cos, sin)
    v = kv[..., SWA_KV_W:].reshape(B, S, SWA_KV_HEADS, SWA_DH)
    return k, v


def sq_relu_mlp(h, w_up, w_down):
    return jnp.square(jax.nn.relu(h @ w_up)) @ w_down


def _fwd_setup_inputs(seed: int = 0) -> dict:
    key = jax.random.key(seed)
    ks = jax.random.split(key, 20)
    f32 = jnp.float32

    def dense(k, shape, fan_in, scale=1.0):
        return jax.random.normal(k, shape, f32) * (fan_in ** -0.5) * scale

    x = jax.random.normal(ks[0], (BATCH, SEQ, D_MODEL), f32)
    mem = jax.random.normal(ks[1], (BATCH, MEM_TOKENS, D_MODEL), f32)
    positions = (jax.random.randint(ks[2], (BATCH, 1), 0, 4096, jnp.int32)
                 + jnp.arange(SEQ, dtype=jnp.int32)[None, :])
    a_w_in = dense(ks[3], (N_A, D_MODEL, A_IN), D_MODEL)
    a_conv_w = jax.random.normal(ks[4], (N_A, CONV_WIDTH, 2 * DN_QK_W + DN_V_W), f32) * (CONV_WIDTH ** -0.5)
    a_A_log = jnp.log(jax.random.uniform(ks[5], (N_A, DN_HEADS), f32, 1.0, 16.0))
    dt = jnp.exp(jax.random.uniform(ks[6], (N_A, DN_HEADS), f32, math.log(1e-3), math.log(1e-1)))
    a_dt_bias = dt + jnp.log(-jnp.expm1(-dt))
    a_norm_w = 1.0 + 0.02 * jax.random.normal(ks[7], (N_A, DN_DV), f32)
    b_w_in = dense(ks[8], (N_B, D_MODEL, B_IN), D_MODEL)
    b_sinks = 0.5 * jax.random.normal(ks[9], (N_B, SWA_HEADS), f32)
    w_kv_shared = dense(ks[10], (D_MODEL, 2 * SWA_KV_W), D_MODEL)
    mem_w_kv = dense(ks[11], (DEPTH, D_MODEL, 2 * MEM_W), D_MODEL)
    w_o = dense(ks[12], (DEPTH, MIX_W, D_MODEL), MIX_W, DN_BETA)
    mlp_w_up = dense(ks[13], (DEPTH, D_MODEL, MLP_HIDDEN), D_MODEL)
    mlp_w_down = dense(ks[14], (DEPTH, MLP_HIDDEN, D_MODEL), MLP_HIDDEN, DN_BETA)
    ln_g = 1.0 + 0.02 * jax.random.normal(ks[15], (DEPTH, 2, D_MODEL), f32)
    ln_b = 0.02 * jax.random.normal(ks[16], (DEPTH, 2, D_MODEL), f32)
    return {"x": x, "mem": mem, "positions": positions, "a_w_in": a_w_in, "a_conv_w": a_conv_w,
            "a_A_log": a_A_log, "a_dt_bias": a_dt_bias, "a_norm_w": a_norm_w, "b_w_in": b_w_in,
            "b_sinks": b_sinks, "w_kv_shared": w_kv_shared, "mem_w_kv": mem_w_kv, "w_o": w_o,
            "mlp_w_up": mlp_w_up, "mlp_w_down": mlp_w_down, "ln_g": ln_g, "ln_b": ln_b}


def _fwd_reference(x, mem, positions, a_w_in, a_conv_w, a_A_log, a_dt_bias, a_norm_w, b_w_in,
              b_sinks, w_kv_shared, mem_w_kv, w_o, mlp_w_up, mlp_w_down, ln_g, ln_b):
    cos, sin = rope_tables(positions, SWA_DH)
    h = x
    k_sh = None
    v_sh = None
    for layer in range(DEPTH):
        if layer < N_A:
            mix = mixer_a(h, mem, a_w_in[layer], a_conv_w[layer], a_A_log[layer], a_dt_bias[layer],
                          a_norm_w[layer], mem_w_kv[layer], w_o[layer])
        else:
            j = layer - N_A
            mix = mixer_b(h, mem, k_sh, v_sh, cos, sin, b_w_in[j], b_sinks[j], mem_w_kv[layer], w_o[layer])
        h = layer_norm(DN_ALPHA * h + mix, ln_g[layer, 0], ln_b[layer, 0])
        h = layer_norm(DN_ALPHA * h + sq_relu_mlp(h, mlp_w_up[layer], mlp_w_down[layer]),
                       ln_g[layer, 1], ln_b[layer, 1])
        if layer == N_A - 1:
            k_sh, v_sh = shared_kv(h, w_kv_shared, cos, sin)
    return h


import jax as _jax
import jax.numpy as _jnp

TWIN_FORMAT = 'train_step'
FWD_PARAMS = ['x', 'mem', 'positions', 'a_w_in', 'a_conv_w', 'a_A_log', 'a_dt_bias', 'a_norm_w', 'b_w_in', 'b_sinks', 'w_kv_shared', 'mem_w_kv', 'w_o', 'mlp_w_up', 'mlp_w_down', 'ln_g', 'ln_b']
TWIN_WEIGHTS = ['a_w_in', 'a_conv_w', 'a_A_log', 'a_dt_bias', 'a_norm_w', 'b_w_in', 'b_sinks', 'w_kv_shared', 'mem_w_kv', 'w_o', 'mlp_w_up', 'mlp_w_down', 'ln_g', 'ln_b']
TWIN_DIFF_INPUT = 'x'
TWIN_INPUTS = ['x', 'mem', 'positions', 'a_w_in', 'a_conv_w', 'a_A_log', 'a_dt_bias', 'a_norm_w', 'b_w_in', 'b_sinks', 'w_kv_shared', 'mem_w_kv', 'w_o', 'mlp_w_up', 'mlp_w_down', 'ln_g', 'ln_b', 'loss_target', 'm_a_w_in', 'm_a_conv_w', 'm_a_A_log', 'm_a_dt_bias', 'm_a_norm_w', 'm_b_w_in', 'm_b_sinks', 'm_w_kv_shared', 'm_mem_w_kv', 'm_w_o', 'm_mlp_w_up', 'm_mlp_w_down', 'm_ln_g', 'm_ln_b', 'v_a_w_in', 'v_a_conv_w', 'v_a_A_log', 'v_a_dt_bias', 'v_a_norm_w', 'v_b_w_in', 'v_b_sinks', 'v_w_kv_shared', 'v_mem_w_kv', 'v_w_o', 'v_mlp_w_up', 'v_mlp_w_down', 'v_ln_g', 'v_ln_b']
TWIN_OUTPUTS = ['loss', 'grad_x', 'grad_a_w_in', 'grad_a_conv_w', 'grad_a_A_log', 'grad_a_dt_bias', 'grad_a_norm_w', 'grad_b_w_in', 'grad_b_sinks', 'grad_w_kv_shared', 'grad_mem_w_kv', 'grad_w_o', 'grad_mlp_w_up', 'grad_mlp_w_down', 'grad_ln_g', 'grad_ln_b', 'delta_a_w_in', 'delta_a_conv_w', 'delta_a_A_log', 'delta_a_dt_bias', 'delta_a_norm_w', 'delta_b_w_in', 'delta_b_sinks', 'delta_w_kv_shared', 'delta_mem_w_kv', 'delta_w_o', 'delta_mlp_w_up', 'delta_mlp_w_down', 'delta_ln_g', 'delta_ln_b', 'new_m_a_w_in', 'new_m_a_conv_w', 'new_m_a_A_log', 'new_m_a_dt_bias', 'new_m_a_norm_w', 'new_m_b_w_in', 'new_m_b_sinks', 'new_m_w_kv_shared', 'new_m_mem_w_kv', 'new_m_w_o', 'new_m_mlp_w_up', 'new_m_mlp_w_down', 'new_m_ln_g', 'new_m_ln_b', 'new_v_a_w_in', 'new_v_a_conv_w', 'new_v_a_A_log', 'new_v_a_dt_bias', 'new_v_a_norm_w', 'new_v_b_w_in', 'new_v_b_sinks', 'new_v_w_kv_shared', 'new_v_mem_w_kv', 'new_v_w_o', 'new_v_mlp_w_up', 'new_v_mlp_w_down', 'new_v_ln_g', 'new_v_ln_b']
TWIN_LEAF_KINDS = {'loss': 'loss', 'grad_x': 'grad_x', 'grad_a_w_in': 'grad_w', 'grad_a_conv_w': 'grad_w', 'grad_a_A_log': 'grad_w', 'grad_a_dt_bias': 'grad_w', 'grad_a_norm_w': 'grad_w', 'grad_b_w_in': 'grad_w', 'grad_b_sinks': 'grad_w', 'grad_w_kv_shared': 'grad_w', 'grad_mem_w_kv': 'grad_w', 'grad_w_o': 'grad_w', 'grad_mlp_w_up': 'grad_w', 'grad_mlp_w_down': 'grad_w', 'grad_ln_g': 'grad_w', 'grad_ln_b': 'grad_w', 'delta_a_w_in': 'delta_w', 'delta_a_conv_w': 'delta_w', 'delta_a_A_log': 'delta_w', 'delta_a_dt_bias': 'delta_w', 'delta_a_norm_w': 'delta_w', 'delta_b_w_in': 'delta_w', 'delta_b_sinks': 'delta_w', 'delta_w_kv_shared': 'delta_w', 'delta_mem_w_kv': 'delta_w', 'delta_w_o': 'delta_w', 'delta_mlp_w_up': 'delta_w', 'delta_mlp_w_down': 'delta_w', 'delta_ln_g': 'delta_w', 'delta_ln_b': 'delta_w', 'new_m_a_w_in': 'new_m', 'new_m_a_conv_w': 'new_m', 'new_m_a_A_log': 'new_m', 'new_m_a_dt_bias': 'new_m', 'new_m_a_norm_w': 'new_m', 'new_m_b_w_in': 'new_m', 'new_m_b_sinks': 'new_m', 'new_m_w_kv_shared': 'new_m', 'new_m_mem_w_kv': 'new_m', 'new_m_w_o': 'new_m', 'new_m_mlp_w_up': 'new_m', 'new_m_mlp_w_down': 'new_m', 'new_m_ln_g': 'new_m', 'new_m_ln_b': 'new_m', 'new_v_a_w_in': 'new_v', 'new_v_a_conv_w': 'new_v', 'new_v_a_A_log': 'new_v', 'new_v_a_dt_bias': 'new_v', 'new_v_a_norm_w': 'new_v', 'new_v_b_w_in': 'new_v', 'new_v_b_sinks': 'new_v', 'new_v_w_kv_shared': 'new_v', 'new_v_mem_w_kv': 'new_v', 'new_v_w_o': 'new_v', 'new_v_mlp_w_up': 'new_v', 'new_v_mlp_w_down': 'new_v', 'new_v_ln_g': 'new_v', 'new_v_ln_b': 'new_v'}


def _forward(args):
    return _fwd_reference(*[args[k] for k in FWD_PARAMS])


def _output_shape():
    def fwd():
        inp = _fwd_setup_inputs(0)
        return _fwd_reference(*[inp[k] for k in FWD_PARAMS])
    out = _jax.eval_shape(fwd)
    return out.shape, out.dtype

N_MICROBATCH = 1
ADAM_LR = 0.001
ADAM_B1 = 0.9
ADAM_B2 = 0.999
ADAM_EPS = 1e-08
ADAM_WD = 0.01
ADAM_STEP = 10
PER_EXAMPLE_BATCH_AXIS = {'x': 0, 'mem': 0, 'positions': 0, 'loss_target': 0}
SHARED_INPUTS = []
_WEIGHT_DTYPES = {'a_w_in': _jnp.float32, 'a_conv_w': _jnp.float32, 'a_A_log': _jnp.float32, 'a_dt_bias': _jnp.float32, 'a_norm_w': _jnp.float32, 'b_w_in': _jnp.float32, 'b_sinks': _jnp.float32, 'w_kv_shared': _jnp.float32, 'mem_w_kv': _jnp.float32, 'w_o': _jnp.float32, 'mlp_w_up': _jnp.float32, 'mlp_w_down': _jnp.float32, 'ln_g': _jnp.float32, 'ln_b': _jnp.float32}
MOMENT_SCALE = {'a_w_in': 4.408754e-02, 'a_conv_w': 4.403797e-02, 'a_A_log': 2.601674e-01, 'a_dt_bias': 2.574390e-01, 'a_norm_w': 1.819687e-01, 'b_w_in': 1.446616e-02, 'b_sinks': 1.698478e-02, 'w_kv_shared': 1.433926e-01, 'mem_w_kv': 1.080991e-02, 'w_o': 1.202239e-01, 'mlp_w_up': 6.411116e-02, 'mlp_w_down': 4.766476e-01, 'ln_g': 4.561801e+01, 'ln_b': 1.061137e+01}


def _to_microbatches(a, axis):
    t = _jnp.moveaxis(a, axis, 0)
    t = t.reshape((N_MICROBATCH, t.shape[0] // N_MICROBATCH) + t.shape[1:])
    return _jnp.moveaxis(t, 1, axis + 1)


def setup_inputs(seed: int = 0) -> dict:
    inp = _fwd_setup_inputs(seed)
    key = _jax.random.fold_in(_jax.random.key(seed), 7919)
    shape, _ = _output_shape()
    out = dict(inp)
    out["loss_target"] = _jax.random.normal(_jax.random.fold_in(key, 0), shape, _jnp.float32)
    for i, name in enumerate(TWIN_WEIGHTS):
        w = inp[name].astype(_jnp.float32)
        if MOMENT_SCALE is None:
            s = _jnp.sqrt(_jnp.mean(_jnp.square(w)) + 1e-30)
        else:
            s = MOMENT_SCALE[name]
        km, kv = _jax.random.split(_jax.random.fold_in(key, i + 1))
        out[name] = w
        out["m_" + name] = s * _jax.random.normal(km, w.shape, _jnp.float32)
        out["v_" + name] = (s * s) * _jax.random.uniform(kv, w.shape, _jnp.float32, 0.5, 1.5)
    if N_MICROBATCH > 1:
        for name, axis in PER_EXAMPLE_BATCH_AXIS.items():
            out[name] = _to_microbatches(out[name], axis)
    return {'x': out['x'], 'mem': out['mem'], 'positions': out['positions'], 'a_w_in': out['a_w_in'], 'a_conv_w': out['a_conv_w'], 'a_A_log': out['a_A_log'], 'a_dt_bias': out['a_dt_bias'], 'a_norm_w': out['a_norm_w'], 'b_w_in': out['b_w_in'], 'b_sinks': out['b_sinks'], 'w_kv_shared': out['w_kv_shared'], 'mem_w_kv': out['mem_w_kv'], 'w_o': out['w_o'], 'mlp_w_up': out['mlp_w_up'], 'mlp_w_down': out['mlp_w_down'], 'ln_g': out['ln_g'], 'ln_b': out['ln_b'], 'loss_target': out['loss_target'], 'm_a_w_in': out['m_a_w_in'], 'm_a_conv_w': out['m_a_conv_w'], 'm_a_A_log': out['m_a_A_log'], 'm_a_dt_bias': out['m_a_dt_bias'], 'm_a_norm_w': out['m_a_norm_w'], 'm_b_w_in': out['m_b_w_in'], 'm_b_sinks': out['m_b_sinks'], 'm_w_kv_shared': out['m_w_kv_shared'], 'm_mem_w_kv': out['m_mem_w_kv'], 'm_w_o': out['m_w_o'], 'm_mlp_w_up': out['m_mlp_w_up'], 'm_mlp_w_down': out['m_mlp_w_down'], 'm_ln_g': out['m_ln_g'], 'm_ln_b': out['m_ln_b'], 'v_a_w_in': out['v_a_w_in'], 'v_a_conv_w': out['v_a_conv_w'], 'v_a_A_log': out['v_a_A_log'], 'v_a_dt_bias': out['v_a_dt_bias'], 'v_a_norm_w': out['v_a_norm_w'], 'v_b_w_in': out['v_b_w_in'], 'v_b_sinks': out['v_b_sinks'], 'v_w_kv_shared': out['v_w_kv_shared'], 'v_mem_w_kv': out['v_mem_w_kv'], 'v_w_o': out['v_w_o'], 'v_mlp_w_up': out['v_mlp_w_up'], 'v_mlp_w_down': out['v_mlp_w_down'], 'v_ln_g': out['v_ln_g'], 'v_ln_b': out['v_ln_b']}


def _loss(weights, diff, rest, loss_target):
    with _jax.named_scope("forward"):
        args = {**rest, TWIN_DIFF_INPUT: diff, **{k: w.astype(_WEIGHT_DTYPES[k]) for k, w in weights.items()}}
        y = _forward(args)
    with _jax.named_scope("loss_head"):
        err = _jnp.square(y.astype(_jnp.float32) - loss_target)
        return 0.5 * _jnp.sum(_jnp.mean(err, axis=-1)) if err.ndim else 0.5 * err


def _adamw(w, g, m, v):
    m = ADAM_B1 * m + (1.0 - ADAM_B1) * g
    v = ADAM_B2 * v + (1.0 - ADAM_B2) * _jnp.square(g)
    m_hat = m / (1.0 - ADAM_B1 ** ADAM_STEP)
    v_hat = v / (1.0 - ADAM_B2 ** ADAM_STEP)
    delta = -ADAM_LR * (m_hat / (_jnp.sqrt(v_hat) + ADAM_EPS) + ADAM_WD * w)
    return delta, m, v


def reference(x, mem, positions, a_w_in, a_conv_w, a_A_log, a_dt_bias, a_norm_w, b_w_in, b_sinks, w_kv_shared, mem_w_kv, w_o, mlp_w_up, mlp_w_down, ln_g, ln_b, loss_target, m_a_w_in, m_a_conv_w, m_a_A_log, m_a_dt_bias, m_a_norm_w, m_b_w_in, m_b_sinks, m_w_kv_shared, m_mem_w_kv, m_w_o, m_mlp_w_up, m_mlp_w_down, m_ln_g, m_ln_b, v_a_w_in, v_a_conv_w, v_a_A_log, v_a_dt_bias, v_a_norm_w, v_b_w_in, v_b_sinks, v_w_kv_shared, v_mem_w_kv, v_w_o, v_mlp_w_up, v_mlp_w_down, v_ln_g, v_ln_b):
    given = dict(x=x, mem=mem, positions=positions, a_w_in=a_w_in, a_conv_w=a_conv_w, a_A_log=a_A_log, a_dt_bias=a_dt_bias, a_norm_w=a_norm_w, b_w_in=b_w_in, b_sinks=b_sinks, w_kv_shared=w_kv_shared, mem_w_kv=mem_w_kv, w_o=w_o, mlp_w_up=mlp_w_up, mlp_w_down=mlp_w_down, ln_g=ln_g, ln_b=ln_b, loss_target=loss_target, m_a_w_in=m_a_w_in, m_a_conv_w=m_a_conv_w, m_a_A_log=m_a_A_log, m_a_dt_bias=m_a_dt_bias, m_a_norm_w=m_a_norm_w, m_b_w_in=m_b_w_in, m_b_sinks=m_b_sinks, m_w_kv_shared=m_w_kv_shared, m_mem_w_kv=m_mem_w_kv, m_w_o=m_w_o, m_mlp_w_up=m_mlp_w_up, m_mlp_w_down=m_mlp_w_down, m_ln_g=m_ln_g, m_ln_b=m_ln_b, v_a_w_in=v_a_w_in, v_a_conv_w=v_a_conv_w, v_a_A_log=v_a_A_log, v_a_dt_bias=v_a_dt_bias, v_a_norm_w=v_a_norm_w, v_b_w_in=v_b_w_in, v_b_sinks=v_b_sinks, v_w_kv_shared=v_w_kv_shared, v_mem_w_kv=v_mem_w_kv, v_w_o=v_w_o, v_mlp_w_up=v_mlp_w_up, v_mlp_w_down=v_mlp_w_down, v_ln_g=v_ln_g, v_ln_b=v_ln_b)
    weights = {n: given[n] for n in TWIN_WEIGHTS}
    shared = {n: given[n] for n in SHARED_INPUTS}
    per_example = {n: given[n] for n in ['x', 'mem', 'positions']}
    grad_fn = _jax.value_and_grad(_loss, argnums=(0, 1))

    def one_microbatch(ex, loss_target):
        ex = dict(ex)
        diff = ex.pop(TWIN_DIFF_INPUT)
        return grad_fn(weights, diff, {**shared, **ex}, loss_target)

    if N_MICROBATCH == 1:
        loss, (grad_w, grad_x) = one_microbatch(per_example, given["loss_target"])
    else:
        def body(carry, xs):
            loss_sum, grad_sum = carry
            l_k, (gw_k, gx_k) = one_microbatch(xs[0], xs[1])
            with _jax.named_scope("update"):
                return (loss_sum + l_k, _jax.tree.map(_jnp.add, grad_sum, gw_k)), gx_k

        init = (_jnp.zeros((), _jnp.float32), _jax.tree.map(_jnp.zeros_like, weights))
        (loss, grad_w), grad_x = _jax.lax.scan(body, init, (per_example, given["loss_target"]))
    with _jax.named_scope("update"):
        delta_w, new_m, new_v = {}, {}, {}
        for n in TWIN_WEIGHTS:
            delta_w[n], new_m[n], new_v[n] = _adamw(weights[n], grad_w[n], given["m_" + n], given["v_" + n])
    return (loss, grad_x, *[grad_w[n] for n in TWIN_WEIGHTS], *[delta_w[n] for n in TWIN_WEIGHTS],
            *[new_m[n] for n in TWIN_WEIGHTS], *[new_v[n] for n in TWIN_WEIGHTS])
```

```python
import functools
import math

import jax
import jax.numpy as jnp
from jax import lax
from jax.experimental import pallas as pl
from jax.experimental.pallas import tpu as pltpu

F32 = jnp.float32
BF16 = jnp.bfloat16

D_MODEL = 1024
DEPTH = 4
N_A = 2
MEM_HEADS = 4
MEM_DH = 64
MEM_W = 256
DN_HEADS = 6
DN_D = 128
DN_W = 768
CHUNK = 64
SWA_DH = 64
SWA_HEADS = 12
WINDOW = 128
ROPE_THETA = 10000.0
LN_EPS = 1e-5
NORM_EPS = 1e-6
DN_ALPHA = (2.0 * DEPTH) ** 0.25
A_IN = 3340
A_IN_PAD = 3456
N_DEV = 8

ADAM_LR = 0.001
ADAM_B1 = 0.9
ADAM_B2 = 0.999
ADAM_EPS = 1e-08
ADAM_WD = 0.01
ADAM_STEP = 10

VMEM_LIMIT = 48 * 1024 * 1024
NEG_BIG = -1e30


def _cparams(sem):
    return pltpu.CompilerParams(dimension_semantics=sem, vmem_limit_bytes=VMEM_LIMIT)


_CONTRACT = {"nn": (1, 0), "nt": (1, 1), "tn": (0, 0)}


def _raw_mm(a, b, mode, prec):
    ca, cb = _CONTRACT[mode]
    dims = (((ca,), (cb,)), ((), ()))
    dot = lambda p, q: lax.dot_general(p, q, dims, preferred_element_type=F32)
    if prec == "bf16":
        return dot(a.astype(BF16), b.astype(BF16))
    a, b = a.astype(F32), b.astype(F32)
    a_hi, b_hi = a.astype(BF16), b.astype(BF16)
    a_lo, b_lo = (a - a_hi.astype(F32)).astype(BF16), (b - b_hi.astype(F32)).astype(BF16)
    return dot(a_hi, b_hi) + (dot(a_hi, b_lo) + dot(a_lo, b_hi))


@functools.partial(jax.custom_vjp, nondiff_argnums=(2, 3))
def mm(a, b, mode, prec):
    return _raw_mm(a, b, mode, prec)


def _mm_fwd(a, b, mode, prec):
    return _raw_mm(a, b, mode, prec), (a, b)


def _mm_bwd(mode, prec, res, ct):
    a, b = res
    if mode == "nn":
        return mm(ct, b, "nt", prec), mm(a, ct, "tn", prec)
    if mode == "nt":
        return mm(ct, b, "nn", prec), mm(ct, a, "tn", prec)
    return mm(b, ct, "nt", prec), mm(a, ct, "nn", prec)


mm.defvjp(_mm_fwd, _mm_bwd)


@jax.custom_vjp
def _softplus(x):
    y = jnp.exp(-jnp.abs(x))
    log1p_y = jnp.where(y < 1e-2, y * (1.0 - y * (0.5 - y * (1.0 / 3.0))), jnp.log(1.0 + y))
    return jnp.maximum(x, 0.0) + log1p_y


def _softplus_fwd(x):
    return _softplus(x), x


def _softplus_bwd(x, ct):
    return (ct * jax.nn.sigmoid(x),)


_softplus.defvjp(_softplus_fwd, _softplus_bwd)


def _iota(shape, dim):
    return lax.broadcasted_iota(jnp.int32, shape, dim)


def _block_fwd(fn, ins, in_specs, out_shapes, out_specs, grid, name):
    n_in = len(ins)

    def body(*refs):
        pids = tuple(pl.program_id(a) for a in range(len(grid)))
        vals = [r[...].astype(F32) for r in refs[:n_in]]
        outs = fn(pids, *vals)
        for r, o in zip(refs[n_in:], outs):
            r[...] = o.astype(r.dtype)

    return pl.pallas_call(
        body, grid=grid, in_specs=in_specs, out_specs=out_specs, out_shape=out_shapes, name=name,
        compiler_params=_cparams(("parallel",) * len(grid)))(*ins)


def _block_bwd(fn, ins, in_specs, cts, ct_specs, kinds, g_shapes, g_specs, grid, name):
    n_in, n_ct = len(ins), len(cts)
    didx = [i for i, k in enumerate(kinds) if k]

    def body(*refs):
        in_refs, ct_refs, g_refs = refs[:n_in], refs[n_in:n_in + n_ct], refs[n_in + n_ct:]
        pids = tuple(pl.program_id(a) for a in range(len(grid)))
        vals = [r[...].astype(F32) for r in in_refs]

        def f(*dvals):
            full = list(vals)
            for i, v in zip(didx, dvals):
                full[i] = v
            return tuple(fn(pids, *full))

        _, vjp = jax.vjp(f, *[vals[i] for i in didx])
        gs = vjp(tuple(r[...].astype(F32) for r in ct_refs))
        first = pids[0] == 0
        for p in pids[1:]:
            first = jnp.logical_and(first, p == 0)
        for i, g, r in zip(didx, gs, g_refs):
            if kinds[i] == "s":
                r[...] = g.astype(r.dtype)
            else:
                @pl.when(first)
                def _(r=r):
                    r[...] = jnp.zeros(r.shape, r.dtype)

                r[...] += g.astype(r.dtype)

    sem = ("arbitrary",) * len(grid) if "a" in kinds else ("parallel",) * len(grid)
    return pl.pallas_call(
        body, grid=grid, in_specs=list(in_specs) + list(ct_specs), out_specs=g_specs, out_shape=g_shapes,
        name=name, compiler_params=_cparams(sem))(*ins, *cts)


def _rows(tb, width, col=0):
    return pl.BlockSpec((tb, width), lambda i, col=col: (i, col))


def _whole(shape):
    return pl.BlockSpec(shape, lambda *_: (0,) * len(shape))


def _sds(shape, dtype=F32):
    return jax.ShapeDtypeStruct(shape, dtype)


def _matmul(a, b, mode, out_dtypes, name, epi=None, extras=(), tm=512, tn=512, tk=1024,
            b_spec=None, n_total=None, out_specs=None, out_shapes=None):
    if mode == "tn":
        k_total, m_total = a.shape
    else:
        m_total, k_total = a.shape
    if n_total is None:
        n_total = b.shape[0] if mode == "nt" else b.shape[1]
    tm, tn, tk = min(tm, m_total), min(tn, n_total), min(tk, k_total)
    assert m_total % tm == 0 and n_total % tn == 0 and k_total % tk == 0, (name, a.shape, b.shape)
    grid = (m_total // tm, n_total // tn, k_total // tk)
    nk = grid[2]
    if mode == "tn":
        a_spec = pl.BlockSpec((tk, tm), lambda i, j, k: (k, i))
    else:
        a_spec = pl.BlockSpec((tm, tk), lambda i, j, k: (i, k))
    if b_spec is None:
        if mode == "nt":
            b_spec = pl.BlockSpec((tn, tk), lambda i, j, k: (j, k))
        else:
            b_spec = pl.BlockSpec((tk, tn), lambda i, j, k: (k, j))
    tile = pl.BlockSpec((tm, tn), lambda i, j, k: (i, j))
    n_ex, n_out = len(extras), len(out_dtypes)
    ca, cb = _CONTRACT[mode]
    dims = (((ca,), (cb,)), ((), ()))

    def body(*refs):
        a_ref, b_ref = refs[:2]
        ex_refs = refs[2:2 + n_ex]
        out_refs = refs[2 + n_ex:2 + n_ex + n_out]
        part = lax.dot_general(a_ref[...], b_ref[...], dims, preferred_element_type=F32)

        def finish(val):
            res = epi(val, *[e[...] for e in ex_refs]) if epi is not None else (val,)
            for r, o in zip(out_refs, res):
                r[...] = o.astype(r.dtype)

        if nk == 1:
            finish(part)
        else:
            acc = refs[-1]
            k = pl.program_id(2)

            @pl.when(k == 0)
            def _():
                acc[...] = part

            @pl.when(k > 0)
            def _():
                acc[...] += part

            @pl.when(k == nk - 1)
            def _():
                finish(acc[...])

    if out_shapes is None:
        out_shapes = [_sds((m_total, n_total), d) for d in out_dtypes]
        out_specs = [tile] * n_out
    outs = pl.pallas_call(
        body, grid=grid, in_specs=[a_spec, b_spec] + [tile] * n_ex, out_specs=out_specs, out_shape=out_shapes,
        scratch_shapes=[pltpu.VMEM((tm, tn), F32)] if nk > 1 else [], name=name,
        compiler_params=_cparams(("parallel", "parallel", "arbitrary")))(a, b, *extras)
    return outs if n_out > 1 else outs[0]


def _silu(x):
    return x * jax.nn.sigmoid(x)


def _rowa_fn(pids, c, ab, alog, dtb):
    tb = c.shape[0]
    s = _silu(c)
    qs, ks = [], []
    for h in range(DN_HEADS):
        qh = s[:, DN_D * h:DN_D * (h + 1)]
        qs.append(qh * lax.rsqrt(jnp.sum(qh * qh, axis=-1, keepdims=True) + NORM_EPS) * (DN_D ** -0.5))
        kh = s[:, DN_W + DN_D * h:DN_W + DN_D * (h + 1)]
        ks.append(kh * lax.rsqrt(jnp.sum(kh * kh, axis=-1, keepdims=True) + NORM_EPS))
    q = jnp.concatenate(qs, axis=1)
    k = jnp.concatenate(ks, axis=1)
    v = s[:, 2 * DN_W:3 * DN_W]
    g128 = -jnp.exp(alog) * _softplus(ab + dtb)
    b128 = jax.nn.sigmoid(ab)
    r, cc = _iota((tb, tb), 0), _iota((tb, tb), 1)
    tri = jnp.where(((r >> 6) == (cc >> 6)) & (r >= cc), 1.0, 0.0)
    gc128 = mm(tri, g128, "nn", "f32")
    lane, col = _iota((128, DN_W), 0), _iota((128, DN_W), 1)
    exp_a = jnp.where(lane == (col >> 7), 1.0, 0.0)
    exp_b = jnp.where(lane == (col >> 7) + DN_HEADS, 1.0, 0.0)
    return q, k, v, mm(gc128, exp_a, "nn", "f32"), mm(b128, exp_b, "nn", "f32")


def _tri_inv_raw(low, block):
    n = low.shape[0]
    r, c = _iota((n, n), 0), _iota((n, n), 1)
    x = jnp.where(r == c, 1.0, 0.0)
    lg = 0
    while (1 << lg) < block:
        off = ((r >> (lg + 1)) == (c >> (lg + 1))) & (((r >> lg) & 1) == 1) & (((c >> lg) & 1) == 0)
        cblk = jnp.where(off, low, 0.0)
        x = x - mm(x, mm(cblk, x, "nn", "f32"), "nn", "f32")
        lg += 1
    return x


@functools.partial(jax.custom_vjp, nondiff_argnums=(1,))
def _tri_inv(low, block):
    return _tri_inv_raw(low, block)


def _tri_inv_fwd(low, block):
    x = _tri_inv_raw(low, block)
    return x, x


def _tri_inv_bwd(block, x, ct):
    n = x.shape[0]
    r, c = _iota((n, n), 0), _iota((n, n), 1)
    shift = block.bit_length() - 1
    g = mm(mm(x, ct, "tn", "f32"), x, "nt", "f32")
    return (jnp.where(((r >> shift) == (c >> shift)) & (r > c), -g, 0.0),)


_tri_inv.defvjp(_tri_inv_fwd, _tri_inv_bwd)


PAIR = 2 * CHUNK


def _dn1_pair(q, k, v, gc, beta):
    n = q.shape[0]
    eg = jnp.exp(gc)
    kb = k * beta
    onehot = jnp.where(_iota((n, DN_D), 1) == 0, 1.0, 0.0)
    g_col = mm(gc, onehot, "nt", "f32")
    g_row = mm(onehot, gc, "nt", "f32")
    r, c = _iota((n, n), 0), _iota((n, n), 1)
    same = (r >> 6) == (c >> 6)
    incl, strict = same & (r >= c), same & (r > c)
    decay = jnp.exp(jnp.where(incl, g_col - g_row, NEG_BIG))
    low = jnp.where(strict, mm(kb, k, "nt", "bf16") * decay, 0.0)
    tinv = _tri_inv(low, CHUNK)
    u = mm(tinv, v * beta, "nn", "f32")
    w = mm(tinv, kb * eg, "nn", "f32")
    intra = jnp.where(incl, mm(q, k, "nt", "bf16") * decay, 0.0)
    row = _iota((n, DN_D), 0)
    last0 = jnp.sum(jnp.where(row == CHUNK - 1, gc, 0.0), axis=0, keepdims=True)
    last1 = jnp.sum(jnp.where(row == PAIR - 1, gc, 0.0), axis=0, keepdims=True)
    g_last = jnp.where(row < CHUNK, last0, last1)
    qd = q * eg
    kd = k * jnp.exp(g_last - gc)
    cd = jnp.exp(g_last)
    return u, w, intra, qd, kd, cd


def _dn1_fn(pids, q, k, v, gc, beta):
    outs = []
    for j in range(q.shape[0] // PAIR):
        sl = slice(PAIR * j, PAIR * (j + 1))
        outs.append(_dn1_pair(q[sl], k[sl], v[sl], gc[sl], beta[sl]))
    return tuple(jnp.concatenate([o[t] for o in outs], axis=0) for t in range(6))


def _dn2_step(half, state, qd, kd, u, w, intra, cd_row):
    v_new = u - mm(w, state, "nn", "bf16")
    zeros = jnp.zeros_like(v_new)
    v_pair = jnp.concatenate([v_new, zeros] if half == 0 else [zeros, v_new], axis=0)
    out = mm(qd, state, "nn", "bf16") + mm(intra, v_pair, "nn", "bf16")
    return out, state * cd_row + mm(kd, v_new, "tn", "bf16")


def _post_fn(pids, o, z, nw):
    outs = []
    for h in range(DN_HEADS):
        oh = o[:, DN_D * h:DN_D * (h + 1)]
        zh = z[:, DN_D * h:DN_D * (h + 1)]
        y = oh * lax.rsqrt(jnp.mean(oh * oh, axis=-1, keepdims=True) + NORM_EPS) * nw
        outs.append(y * _silu(zh))
    return (jnp.concatenate(outs, axis=1),)


def _memattn_fn(pids, qm, kvm):
    kmem, vmem = kvm[:, :MEM_W], kvm[:, MEM_W:]
    lane = _iota((1, MEM_W), 1)
    out = None
    for h in range(MEM_HEADS):
        hm = jnp.where((lane >> 6) == h, 1.0, 0.0)
        s = mm(qm * hm, kmem, "nt", "bf16") * (MEM_DH ** -0.5)
        m = lax.stop_gradient(jnp.max(s, axis=-1, keepdims=True))
        e = jnp.exp(s - m)
        p = e / jnp.sum(e, axis=-1, keepdims=True)
        oh = mm(p, vmem, "nn", "bf16") * hm
        out = oh if out is None else out + oh
    return (out,)


def _ln_fn(pids, h, mix, g, b):
    x = DN_ALPHA * h + mix
    mu = jnp.mean(x, axis=-1, keepdims=True)
    xc = x - mu
    var = jnp.mean(xc * xc, axis=-1, keepdims=True)
    return (xc * lax.rsqrt(var + LN_EPS) * g + b,)


def _rope_matrix():
    i, j = _iota((128, 128), 0), _iota((128, 128), 1)
    jj = j & 63
    return jnp.where((jj < 32) & (i == j + 32), -1.0, 0.0) + jnp.where((jj >= 32) & (i == j - 32), 1.0, 0.0)


def _rope128(x, cos, sin, rot):
    return x * cos + mm(x, rot, "nn", "f32") * sin


def _krope_fn(pids, kraw, cos, sin):
    rot = _rope_matrix()
    return (jnp.concatenate([_rope128(kraw[:, 128 * g:128 * (g + 1)], cos, sin, rot) for g in range(2)], axis=1),)


def _swa_fn(pids, qraw, cos, sin, k_halo, k_cur, v_halo, v_cur, sinks):
    tb = qraw.shape[0]
    nwin = tb // WINDOW
    rot = _rope_matrix()
    kcat = jnp.concatenate([k_halo, k_cur], axis=0)
    vcat = jnp.concatenate([v_halo, v_cur], axis=0)
    lane = _iota((1, 128), 1)
    halves = (jnp.where(lane < 64, 1.0, 0.0), jnp.where(lane >= 64, 1.0, 0.0))
    qi, kj = _iota((WINDOW, 2 * WINDOW), 0), _iota((WINDOW, 2 * WINDOW), 1)
    diff = qi + WINDOW - kj
    band = (diff >= 0) & (diff < WINDOW)
    acc = [[None] * (SWA_HEADS // 2) for _ in range(nwin)]
    for p in range(SWA_HEADS // 2):
        qg = _rope128(qraw[:, 128 * p:128 * (p + 1)], cos, sin, rot)
        kv = p // 3
        for hh in range(2):
            head = 2 * p + hh
            sink = jnp.sum(jnp.where(lane == head, sinks, 0.0), axis=-1, keepdims=True)
            qh = qg * halves[hh]
            for w in range(nwin):
                keys = kcat[WINDOW * w:WINDOW * (w + 2), 128 * kv:128 * (kv + 1)]
                vals = vcat[WINDOW * w:WINDOW * (w + 2), 128 * kv:128 * (kv + 1)]
                s = mm(qh[WINDOW * w:WINDOW * (w + 1)], keys, "nt", "bf16") * (SWA_DH ** -0.5)
                valid = band & ((pids[0] * tb + WINDOW * (w - 1) + kj) >= 0)
                s = jnp.where(valid, s, NEG_BIG)
                m = lax.stop_gradient(jnp.maximum(jnp.max(s, axis=-1, keepdims=True), sink))
                e = jnp.exp(s - m)
                denom = jnp.sum(e, axis=-1, keepdims=True) + jnp.exp(sink - m)
                o = mm(e / denom, vals, "nn", "bf16") * halves[hh]
                acc[w][p] = o if hh == 0 else acc[w][p] + o
    return (jnp.concatenate([jnp.concatenate(acc[w], axis=1) for w in range(nwin)], axis=0),)


def _conv_fwd(proj, conv_w, tb):
    t_total = proj.shape[0]
    width = conv_w.shape[1]
    nb = t_total // tb

    def body(cur_ref, prev_ref, w_ref, out_ref):
        i = pl.program_id(0)
        prev = jnp.where(i > 0, prev_ref[...], 0.0)
        xcat = jnp.concatenate([prev, cur_ref[...]], axis=0)
        acc = xcat[8:] * w_ref[3:4, :]
        for j in range(3):
            acc = acc + pltpu.roll(xcat, 3 - j, 0)[8:] * w_ref[j:j + 1, :]
        out_ref[...] = acc

    return pl.pallas_call(
        body, grid=(nb,),
        in_specs=[pl.BlockSpec((tb, width), lambda i: (i, 0)),
                  pl.BlockSpec((8, width), lambda i: (jnp.maximum(i * (tb // 8) - 1, 0), 0)),
                  _whole((4, width))],
        out_specs=pl.BlockSpec((tb, width), lambda i: (i, 0)), out_shape=_sds((t_total, width)),
        name="conv_fwd", compiler_params=_cparams(("parallel",)))(proj, proj, conv_w)


def _conv_bwd(dc, proj, conv_w, tb):
    t_total, width = dc.shape
    nb = t_total // tb

    def body(dcur_ref, dnext_ref, cur_ref, prev_ref, w_ref, dx_ref, dw_ref):
        i = pl.program_id(0)
        dnext = jnp.where(i < nb - 1, dnext_ref[...], 0.0)
        dcur = dcur_ref[...]
        dcat = jnp.concatenate([dcur, dnext], axis=0)
        prev = jnp.where(i > 0, prev_ref[...], 0.0)
        xcat = jnp.concatenate([prev, cur_ref[...]], axis=0)

        @pl.when(i == 0)
        def _():
            dw_ref[...] = jnp.zeros(dw_ref.shape, F32)

        dx = dcur * w_ref[3:4, :]
        dw_ref[3:4, :] += jnp.sum(dcur * xcat[8:], axis=0, keepdims=True)
        for j in range(3):
            dx = dx + pltpu.roll(dcat, 8 - (3 - j), 0)[8:] * w_ref[j:j + 1, :]
            dw_ref[j:j + 1, :] += jnp.sum(dcur * pltpu.roll(xcat, 3 - j, 0)[8:], axis=0, keepdims=True)
        dx_ref[...] = dx.astype(dx_ref.dtype)

    return pl.pallas_call(
        body, grid=(nb,),
        in_specs=[pl.BlockSpec((tb, width), lambda i: (i, 0)),
                  pl.BlockSpec((8, width), lambda i: (jnp.minimum((i + 1) * (tb // 8), t_total // 8 - 1), 0)),
                  pl.BlockSpec((tb, width), lambda i: (i, 0)),
                  pl.BlockSpec((8, width), lambda i: (jnp.maximum(i * (tb // 8) - 1, 0), 0)),
                  _whole((4, width))],
        out_specs=[pl.BlockSpec((tb, width), lambda i: (i, 0)), _whole((4, width))],
        out_shape=[_sds((t_total, width), BF16), _sds((4, width))],
        name="conv_bwd", compiler_params=_cparams(("arbitrary",)))(dc, dc, proj, proj, conv_w)


def _head_spec(tb, nb=None):
    if nb is None:
        return pl.BlockSpec((tb, DN_D), lambda h, i: (i, h))
    return pl.BlockSpec((tb, DN_D), lambda h, i: (nb - 1 - i, h))


def _intra_spec(tb, nb=None):
    if nb is None:
        return pl.BlockSpec((None, tb, PAIR), lambda h, i: (h, i, 0))
    return pl.BlockSpec((None, tb, PAIR), lambda h, i: (h, nb - 1 - i, 0))


def _state_spec(tb, nb=None):
    if nb is None:
        return pl.BlockSpec((None, tb // CHUNK, DN_D, DN_D), lambda h, i: (h, i, 0, 0))
    return pl.BlockSpec((None, tb // CHUNK, DN_D, DN_D), lambda h, i: (h, nb - 1 - i, 0, 0))


def _dn2_fwd(qd, kd, u, w, intra, cd, tb):
    t_total = qd.shape[0]
    nb = t_total // tb
    hs = _head_spec(tb)

    def body(qd_ref, kd_ref, u_ref, w_ref, a_ref, cd_ref, o_ref, save_ref, state):
        @pl.when(pl.program_id(1) == 0)
        def _():
            state[...] = jnp.zeros(state.shape, F32)

        for j in range(tb // CHUNK):
            sl = pl.ds(CHUNK * j, CHUNK)
            s0 = state[...]
            save_ref[j] = s0
            out, s1 = _dn2_step(j % 2, s0, qd_ref[sl, :], kd_ref[sl, :], u_ref[sl, :], w_ref[sl, :], a_ref[sl, :],
                                cd_ref[pl.ds(CHUNK * j, 1), :])
            o_ref[sl, :] = out
            state[...] = s1

    return pl.pallas_call(
        body, grid=(DN_HEADS, nb), in_specs=[hs, hs, hs, hs, _intra_spec(tb), hs],
        out_specs=[hs, _state_spec(tb)],
        out_shape=[_sds((t_total, DN_W)), _sds((DN_HEADS, t_total // CHUNK, DN_D, DN_D))],
        scratch_shapes=[pltpu.VMEM((DN_D, DN_D), F32)], name="dn2_fwd",
        compiler_params=_cparams(("parallel", "arbitrary")))(qd, kd, u, w, intra, cd)


def _dn2_bwd(qd, kd, u, w, intra, cd, saved, d_o, tb):
    t_total = qd.shape[0]
    nb = t_total // tb
    hs = _head_spec(tb, nb)

    def body(qd_ref, kd_ref, u_ref, w_ref, a_ref, cd_ref, save_ref, do_ref,
             dqd_ref, dkd_ref, du_ref, dw_ref, da_ref, dcd_ref, dstate):
        @pl.when(pl.program_id(1) == 0)
        def _():
            dstate[...] = jnp.zeros(dstate.shape, F32)

        first_row = _iota((CHUNK, DN_D), 0) == 0
        for j in reversed(range(tb // CHUNK)):
            sl = pl.ds(CHUNK * j, CHUNK)
            _, vjp = jax.vjp(functools.partial(_dn2_step, j % 2), save_ref[j], qd_ref[sl, :], kd_ref[sl, :], u_ref[sl, :], w_ref[sl, :],
                             a_ref[sl, :], cd_ref[pl.ds(CHUNK * j, 1), :])
            ds0, dqd, dkd, du, dw, da, dcd = vjp((do_ref[sl, :], dstate[...]))
            dqd_ref[sl, :] = dqd
            dkd_ref[sl, :] = dkd
            du_ref[sl, :] = du
            dw_ref[sl, :] = dw
            da_ref[sl, :] = da
            dcd_ref[sl, :] = jnp.where(first_row, dcd, 0.0)
            dstate[...] = ds0

    full = _sds((t_total, DN_W))
    return pl.pallas_call(
        body, grid=(DN_HEADS, nb),
        in_specs=[hs, hs, hs, hs, _intra_spec(tb, nb), hs, _state_spec(tb, nb), hs],
        out_specs=[hs, hs, hs, hs, _intra_spec(tb, nb), hs],
        out_shape=[full, full, full, full, _sds((DN_HEADS, t_total, PAIR)), full],
        scratch_shapes=[pltpu.VMEM((DN_D, DN_D), F32)], name="dn2_bwd",
        compiler_params=_cparams(("parallel", "arbitrary")))(qd, kd, u, w, intra, cd, saved, d_o)


def _loss_and_grad(y, target, tb):
    t_total, d = y.shape

    def body(y_ref, t_ref, dy_ref, acc_ref):
        @pl.when(pl.program_id(0) == 0)
        def _():
            acc_ref[...] = jnp.zeros(acc_ref.shape, F32)

        err = y_ref[...] - t_ref[...]
        dy_ref[...] = err * (1.0 / d)
        acc_ref[...] += jnp.sum(err * err, axis=0, keepdims=True)

    dy, acc = pl.pallas_call(
        body, grid=(t_total // tb,), in_specs=[_rows(tb, d), _rows(tb, d)],
        out_specs=[_rows(tb, d), _whole((1, d))], out_shape=[_sds((t_total, d)), _sds((1, d))],
        name="loss", compiler_params=_cparams(("arbitrary",)))(y, target)
    return 0.5 * jnp.sum(acc) / d, dy


def _halo_sum(mains, halos, tb):
    t_total, width = mains[0].shape
    nb = t_total // tb
    n = len(mains)

    def body(*refs):
        out_ref = refs[-1]
        i = pl.program_id(0)
        tot = refs[0][...]
        for r in refs[1:n]:
            tot = tot + r[...]
        hal = refs[n][...]
        for r in refs[n + 1:2 * n]:
            hal = hal + r[...]
        hal = jnp.where(i < nb - 1, hal, 0.0)
        out_ref[...] = tot + jnp.concatenate([jnp.zeros((tb - WINDOW, width), F32), hal], axis=0)

    return pl.pallas_call(
        body, grid=(nb,),
        in_specs=[_rows(tb, width)] * n
        + [pl.BlockSpec((None, WINDOW, width), lambda i: (jnp.minimum(i + 1, nb - 1), 0, 0))] * n,
        out_specs=_rows(tb, width), out_shape=_sds((t_total, width)), name="halo_sum",
        compiler_params=_cparams(("parallel",)))(*mains, *halos)


def _adamw(recv, w, m, v, tr, name):
    slots, r_total, c_total = recv.shape
    tr = min(tr, r_total)
    assert r_total % tr == 0
    c1 = 1.0 / (1.0 - ADAM_B1 ** ADAM_STEP)
    c2 = 1.0 / (1.0 - ADAM_B2 ** ADAM_STEP)

    def body(recv_ref, w_ref, m_ref, v_ref, g_ref, d_ref, nm_ref, nv_ref):
        g = recv_ref[0]
        for s in range(1, slots):
            g = g + recv_ref[s]
        nm = ADAM_B1 * m_ref[...] + (1.0 - ADAM_B1) * g
        nv = ADAM_B2 * v_ref[...] + (1.0 - ADAM_B2) * (g * g)
        g_ref[...] = g
        nm_ref[...] = nm
        nv_ref[...] = nv
        d_ref[...] = -ADAM_LR * ((nm * c1) / (jnp.sqrt(nv * c2) + ADAM_EPS) + ADAM_WD * w_ref[...])

    blk = pl.BlockSpec((tr, c_total), lambda i: (i, 0))
    return pl.pallas_call(
        body, grid=(r_total // tr,),
        in_specs=[pl.BlockSpec((slots, tr, c_total), lambda i: (0, i, 0)), blk, blk, blk],
        out_specs=[blk] * 4, out_shape=[_sds((r_total, c_total))] * 4, name=name,
        compiler_params=_cparams(("parallel",)))(recv, w, m, v)


def _me_and_peers():
    x, y, c = lax.axis_index("x"), lax.axis_index("y"), lax.axis_index("c")
    me = 4 * x + 2 * y + c
    peers = []
    for k in range(1, N_DEV):
        px = 1 - x if (k >> 2) & 1 else x
        py = 1 - y if (k >> 1) & 1 else y
        pc = 1 - c if k & 1 else c
        peers.append(((px, py, pc), 4 * px + 2 * py + pc))
    return me, peers


def _small_exchange(packed, reduce):
    r_total = packed.shape[0]

    def body(p_ref, out_ref, gath_ref, send_sems, recv_sems):
        me, peers = _me_and_peers()
        gath_ref[me] = p_ref[...]
        copies = []
        for k, (dev, _) in enumerate(peers):
            cp = pltpu.make_async_remote_copy(src_ref=p_ref, dst_ref=gath_ref.at[me], send_sem=send_sems.at[k],
                                              recv_sem=recv_sems.at[k], device_id=dev,
                                              device_id_type=pl.DeviceIdType.MESH)
            cp.start()
            copies.append(cp)
        for k, (dev, idx) in enumerate(peers):
            pltpu.make_async_remote_copy(src_ref=p_ref, dst_ref=gath_ref.at[idx], send_sem=send_sems.at[k],
                                         recv_sem=recv_sems.at[k], device_id=dev,
                                         device_id_type=pl.DeviceIdType.MESH).wait_recv()
        for cp in copies:
            cp.wait_send()
        if reduce:
            tot = gath_ref[0]
            for d in range(1, N_DEV):
                tot = tot + gath_ref[d]
            out_ref[...] = tot
        else:
            out_ref[...] = gath_ref[...]

    out_shape = _sds((r_total, 128)) if reduce else _sds((N_DEV, r_total, 128))
    return pl.pallas_call(
        body, in_specs=[pl.BlockSpec(memory_space=pltpu.VMEM)], out_specs=pl.BlockSpec(memory_space=pltpu.VMEM),
        out_shape=out_shape,
        scratch_shapes=[pltpu.VMEM((N_DEV, r_total, 128), F32), pltpu.SemaphoreType.DMA((N_DEV - 1,)),
                        pltpu.SemaphoreType.DMA((N_DEV - 1,))],
        name="small_allreduce" if reduce else "small_allgather")(packed)


def _slot(ref, axis, idx, size):
    sel = [slice(None)] * len(ref.shape)
    sel[axis] = idx if size is None else pl.ds(pl.multiple_of(idx * size, size), size)
    return ref.at[tuple(sel)]


def _big_exchange(srcs, dst_shapes, src_view, dst_view, name):
    n = len(srcs)

    def body(*refs):
        src_refs, dst_refs = refs[:n], refs[n:2 * n]
        send_sems, recv_sems, local_sems = refs[2 * n:]
        me, peers = _me_and_peers()
        local, remote = [], []
        for t in range(n):
            loc = pltpu.make_async_copy(src_view(t, src_refs[t], me), dst_view(t, dst_refs[t], me), local_sems.at[t])
            loc.start()
            local.append(loc)
            for k, (dev, idx) in enumerate(peers):
                cp = pltpu.make_async_remote_copy(
                    src_ref=src_view(t, src_refs[t], idx), dst_ref=dst_view(t, dst_refs[t], me),
                    send_sem=send_sems.at[t, k], recv_sem=recv_sems.at[t, k], device_id=dev,
                    device_id_type=pl.DeviceIdType.MESH)
                cp.start()
                remote.append(cp)
        for t in range(n):
            for k, (dev, idx) in enumerate(peers):
                pltpu.make_async_remote_copy(
                    src_ref=src_view(t, src_refs[t], me), dst_ref=dst_view(t, dst_refs[t], idx),
                    send_sem=send_sems.at[t, k], recv_sem=recv_sems.at[t, k], device_id=dev,
                    device_id_type=pl.DeviceIdType.MESH).wait_recv()
        for cp in remote:
            cp.wait_send()
        for cp in local:
            cp.wait()

    any_spec = pl.BlockSpec(memory_space=pl.ANY)
    return pl.pallas_call(
        body, in_specs=[any_spec] * n, out_specs=[any_spec] * n, out_shape=dst_shapes,
        scratch_shapes=[pltpu.SemaphoreType.DMA((n, N_DEV - 1)), pltpu.SemaphoreType.DMA((n, N_DEV - 1)),
                        pltpu.SemaphoreType.DMA((n,))],
        name=name)(*srcs)


BIG = {
    "a_w_in": (1, (2, 1024, A_IN)),
    "b_w_in": (1, (2, 1024, 1024)),
    "w_kv_shared": (0, (1024, 256)),
    "mem_w_kv": (1, (4, 1024, 512)),
    "w_o": (1, (4, 1024, 1024)),
    "mlp_w_up": (2, (4, 1024, 4096)),
    "mlp_w_down": (1, (4, 4096, 1024)),
}
BIG_NAMES = tuple(BIG)


def _allgather_weights(shards):
    dst_shapes, axes, sizes = [], [], []
    for name, s in zip(BIG_NAMES, shards):
        axis, full = BIG[name]
        if name == "mlp_w_up":
            dst_shapes.append(_sds((N_DEV,) + s.shape, BF16))
            axes.append(0)
            sizes.append(None)
        else:
            dst_shapes.append(_sds(full, BF16))
            axes.append(axis)
            sizes.append(s.shape[axis])

    def src_view(t, ref, idx):
        return ref

    def dst_view(t, ref, idx):
        return _slot(ref, axes[t], idx, sizes[t])

    return _big_exchange(shards, dst_shapes, src_view, dst_view, "allgather_weights")


def _scatter_grads(grads):
    dst_shapes, axes, sizes = [], [], []
    for name, g in zip(BIG_NAMES, grads):
        axis, full = BIG[name]
        if name == "mlp_w_up":
            shard = (g.shape[0],) + g.shape[2:]
            axes.append(1)
            sizes.append(None)
        else:
            shard = tuple(d // N_DEV if a == axis else d for a, d in enumerate(full))
            axes.append(axis)
            sizes.append(shard[axis])
        dst_shapes.append(_sds((N_DEV,) + shard))

    def src_view(t, ref, idx):
        return _slot(ref, axes[t], idx, sizes[t])

    def dst_view(t, ref, idx):
        return ref.at[idx]

    return _big_exchange(grads, dst_shapes, src_view, dst_view, "scatter_grads")


def _pad_row(vec, width=128):
    return jnp.pad(vec.astype(F32), (0, width - vec.shape[0])).reshape(1, width)


def _block_sizes(t_total):
    return dict(row=min(256, t_total), dn=min(512, t_total), swa=min(256, t_total))


def _ln_apply(h, mix, g, b, tb):
    t_total, d = h.shape
    fwd = lambda pids, *a: _ln_fn(pids, *a) * 2
    return _block_fwd(fwd, [h, mix, g, b], [_rows(tb, d), _rows(tb, d), _whole((1, d)), _whole((1, d))],
                      [_sds((t_total, d)), _sds((t_total, d), BF16)], [_rows(tb, d), _rows(tb, d)],
                      (t_total // tb,), "ln_fwd")


def _ln_grad(h, mix, g, b, dy, tb):
    t_total, d = h.shape
    return _block_bwd(_ln_fn, [h, mix, g, b], [_rows(tb, d), _rows(tb, d), _whole((1, d)), _whole((1, d))],
                      [dy], [_rows(tb, d)], ["s", "s", "a", "a"],
                      [_sds((t_total, d)), _sds((t_total, d), BF16), _sds((1, d)), _sds((1, d))],
                      [_rows(tb, d), _rows(tb, d), _whole((1, d)), _whole((1, d))], (t_total // tb,), "ln_bwd")


def _memattn_specs(tb, qcol):
    return [pl.BlockSpec((tb, MEM_W), lambda i: (i, qcol)), _whole((MEM_W, 2 * MEM_W))]


def _up_epilogue(acc):
    r = jnp.maximum(acc, 0.0)
    return acc, r * r


def _dup_epilogue(acc, up):
    return (acc * (2.0 * jnp.maximum(up, 0.0)),)


def _add_epilogue(acc, other):
    return (acc + other,)


_DIAG = []


def _local_step(x, mem, positions, target, wts, small):
    t_total = x.shape[0]
    bs = _block_sizes(t_total)
    tb, tdn, tsw = bs["row"], bs["dn"], bs["swa"]
    nb = t_total // tb
    nbs = t_total // tsw

    inv_freq = ROPE_THETA ** (-jnp.arange(0, SWA_DH, 2, dtype=F32) / SWA_DH)
    ang = positions.astype(F32)[:, None] * inv_freq
    cos = jnp.tile(jnp.cos(ang), (1, 4))
    sin = jnp.tile(jnp.sin(ang), (1, 4))

    w_a = [jnp.concatenate([wts["a_w_in"][l][:, :3072], wts["a_w_in"][l][:, 3084:], wts["a_w_in"][l][:, 3072:3084],
                            jnp.zeros((D_MODEL, A_IN_PAD - A_IN), BF16)], axis=1) for l in range(N_A)]
    wkv = wts["w_kv_shared"]
    w_kvd = jnp.concatenate([wkv[:, 64 * (i // 2):64 * (i // 2 + 1)] for i in range(8)], axis=1)
    mem_b = mem.astype(BF16)
    up_spec_nn = lambda l: pl.BlockSpec((None, None, D_MODEL, 512), lambda i, j, k, l=l: (j, l, k, 0))
    up_spec_nt = lambda l: pl.BlockSpec((None, None, D_MODEL, 512), lambda i, j, k, l=l: (k, l, j, 0))

    saved = []
    h, hb = x, x.astype(BF16)
    kr = vd_src = None
    for l in range(DEPTH):
        sv = dict(h=h, hb=hb)
        kvm = _matmul(mem_b, wts["mem_w_kv"][l], "nn", [F32], "mm_memkv", tm=256)
        if l < N_A:
            proj = _matmul(hb, w_a[l], "nn", [F32], "mm_proj_a", tn=1152)
            conv_w = small["a_conv_w"][l]
            c = _conv_fwd(proj, conv_w, tb)
            alog, dtb = _pad_row(small["a_A_log"][l]), _pad_row(small["a_dt_bias"][l])
            rowa_in = [c, proj, alog, dtb]
            rowa_specs = [_rows(tb, 3 * DN_W), _rows(tb, 128, 26), _whole((1, 128)), _whole((1, 128))]
            q, k, v, gcb, betab = _block_fwd(_rowa_fn, rowa_in, rowa_specs, [_sds((t_total, DN_W))] * 5,
                                             [_rows(tb, DN_W)] * 5, (nb,), "rowa_fwd")
            hs = _head_spec(tdn)
            dn_grid = (DN_HEADS, t_total // tdn)
            full = _sds((t_total, DN_W))
            dn1_out_shapes = [full, full, _sds((DN_HEADS, t_total, PAIR)), full, full, full]
            dn1_out_specs = [hs, hs, _intra_spec(tdn), hs, hs, hs]
            u, w, intra, qd, kd, cd = _block_fwd(_dn1_fn, [q, k, v, gcb, betab], [hs] * 5, dn1_out_shapes,
                                                 dn1_out_specs, dn_grid, "dn1_fwd")
            o, states = _dn2_fwd(qd, kd, u, w, intra, cd, tdn)
            nw = small["a_norm_w"][l].reshape(1, DN_D)
            post_in = [o, proj, nw]
            post_specs = [_rows(tb, DN_W), _rows(tb, DN_W, 3), _whole((1, DN_D))]
            (og,) = _block_fwd(_post_fn, post_in, post_specs, [_sds((t_total, DN_W), BF16)], [_rows(tb, DN_W)],
                               (nb,), "post_fwd")
            qm_col = 12
            sv.update(proj=proj, c=c, rowa_in=rowa_in, rowa_specs=rowa_specs, dn1_in=[q, k, v, gcb, betab],
                      dn2_in=[qd, kd, u, w, intra, cd], states=states, post_in=post_in, post_specs=post_specs,
                      conv_w=conv_w)
        else:
            jb = l - N_A
            proj = _matmul(hb, wts["b_w_in"][jb], "nn", [F32], "mm_proj_b")
            sinks = _pad_row(small["b_sinks"][jb])
            swa_in = [proj, cos, sin, kr, kr, vd_src, vd_src, sinks]
            swa_specs = [_rows(tsw, DN_W), _rows(tsw, 128), _rows(tsw, 128),
                         pl.BlockSpec((WINDOW, 256), lambda i: (jnp.maximum(i * (tsw // WINDOW) - 1, 0), 0)),
                         _rows(tsw, 256),
                         pl.BlockSpec((WINDOW, 256), lambda i: (jnp.maximum(i * (tsw // WINDOW) - 1, 0), 1)),
                         _rows(tsw, 256, 1), _whole((1, 128))]
            (og,) = _block_fwd(_swa_fn, swa_in, swa_specs, [_sds((t_total, DN_W), BF16)], [_rows(tsw, DN_W)],
                               (nbs,), "swa_fwd")
            qm_col = 3
            sv.update(proj=proj, swa_in=swa_in, swa_specs=swa_specs)
        mem_in = [proj, kvm]
        (mo,) = _block_fwd(_memattn_fn, mem_in, _memattn_specs(tb, qm_col), [_sds((t_total, MEM_W), BF16)],
                           [_rows(tb, MEM_W)], (nb,), "memattn_fwd")
        mixin = jnp.concatenate([og, mo], axis=1)
        mix = _matmul(mixin, wts["w_o"][l], "nn", [F32], "mm_wo")
        g0, b0 = small["ln_g"][l, 0].reshape(1, -1), small["ln_b"][l, 0].reshape(1, -1)
        h1, h1b = _ln_apply(h, mix, g0, b0, tb)
        up, act = _matmul(h1b, wts["mlp_w_up"], "nn", [F32, BF16], "mm_up", epi=_up_epilogue,
                          b_spec=up_spec_nn(l), n_total=4 * D_MODEL)
        mlp = _matmul(act, wts["mlp_w_down"][l], "nn", [F32], "mm_down")
        g1, b1 = small["ln_g"][l, 1].reshape(1, -1), small["ln_b"][l, 1].reshape(1, -1)
        h2, h2b = _ln_apply(h1, mlp, g1, b1, tb)
        sv.update(kvm=kvm, mem_in=mem_in, qm_col=qm_col, mixin=mixin, mix=mix, ln0=(g0, b0), h1=h1, h1b=h1b,
                  up=up, act=act, mlp=mlp, ln1=(g1, b1))
        saved.append(sv)
        h, hb = h2, h2b
        if l == N_A - 1:
            kvd = _matmul(hb, w_kvd, "nn", [F32], "mm_kvd")
            krope_in = [kvd, cos, sin]
            krope_specs = [_rows(tb, 256), _rows(tb, 128), _rows(tb, 128)]
            (kr,) = _block_fwd(_krope_fn, krope_in, krope_specs, [_sds((t_total, 256))], [_rows(tb, 256)], (nb,),
                               "krope_fwd")
            vd_src = kvd

    loss, dh = _loss_and_grad(h, target, tb)

    grads = {n: [None] * BIG[n][1][0] for n in BIG_NAMES if n != "w_kv_shared"}
    sg = dict(a_conv_w=[None] * N_A, a_A_log=[None] * N_A, a_dt_bias=[None] * N_A, a_norm_w=[None] * N_A,
              b_sinks=[None] * (DEPTH - N_A), ln_g=[[None, None] for _ in range(DEPTH)],
              ln_b=[[None, None] for _ in range(DEPTH)])
    dk_parts, dv_parts = [], []
    for l in reversed(range(DEPTH)):
        sv = saved[l]
        if l == N_A - 1:
            dkr = _halo_sum([p[0] for p in dk_parts], [p[1] for p in dk_parts], tsw)
            dvv = _halo_sum([p[0] for p in dv_parts], [p[1] for p in dv_parts], tsw)
            (dkraw,) = _block_bwd(_krope_fn, krope_in, krope_specs, [dkr], [_rows(tb, 256)], ["s", None, None],
                                  [_sds((t_total, 256), BF16)], [_rows(tb, 256)], (nb,), "krope_bwd")
            dkvd = jnp.concatenate([dkraw, dvv.astype(BF16)], axis=1)
            g_kvd = _matmul(saved[l + 1]["hb"], dkvd, "tn", [F32], "mm_dw_kvd", tm=1024, tn=512)
            dh = _matmul(dkvd, w_kvd, "nt", [F32], "mm_dx_kvd", epi=_add_epilogue, extras=[dh], tn=1024, tk=512)
            grads["w_kv_shared"] = jnp.concatenate(
                [g_kvd[:, 128 * i:128 * i + 64] + g_kvd[:, 128 * i + 64:128 * (i + 1)] for i in range(4)], axis=1)
        g1, b1 = sv["ln1"]
        dh1a, dmlp, dg1, db1 = _ln_grad(sv["h1"], sv["mlp"], g1, b1, dh, tb)
        dup = _matmul(dmlp, wts["mlp_w_down"][l], "nt", [BF16], "mm_dact", epi=_dup_epilogue, extras=[sv["up"]],
                      tn=512, tk=1024)
        grads["mlp_w_down"][l] = _matmul(sv["act"], dmlp, "tn", [F32], "mm_dw_down", tm=1024, tn=512)
        g_up = _matmul(sv["h1b"], dup, "tn", [F32], "mm_dw_up", tm=1024, tn=512,
                       out_shapes=[_sds((N_DEV, D_MODEL, 512))],
                       out_specs=[pl.BlockSpec((None, 1024, 512), lambda i, j, k: (j, i, 0))])
        grads["mlp_w_up"][l] = g_up
        dh1 = _matmul(dup, wts["mlp_w_up"], "nt", [F32], "mm_dx_up", epi=_add_epilogue, extras=[dh1a],
                      b_spec=up_spec_nt(l), n_total=D_MODEL, tn=1024, tk=512)
        g0, b0 = sv["ln0"]
        dha, dmix, dg0, db0 = _ln_grad(sv["h"], sv["mix"], g0, b0, dh1, tb)
        sg["ln_g"][l] = [dg0, dg1]
        sg["ln_b"][l] = [db0, db1]
        grads["w_o"][l] = _matmul(sv["mixin"], dmix, "tn", [F32], "mm_dw_o", tm=1024, tn=512)
        dmixin = _matmul(dmix, wts["w_o"][l], "nt", [F32], "mm_dx_o", tn=1024)
        dqm, dkvm = _block_bwd(_memattn_fn, sv["mem_in"], _memattn_specs(tb, sv["qm_col"]), [dmixin],
                               [_rows(tb, MEM_W, 3)], ["s", "a"],
                               [_sds((t_total, MEM_W), BF16), _sds((MEM_W, 2 * MEM_W))],
                               [_rows(tb, MEM_W), _whole((MEM_W, 2 * MEM_W))], (nb,), "memattn_bwd")
        grads["mem_w_kv"][l] = _matmul(mem_b, dkvm.astype(BF16), "tn", [F32], "mm_dw_memkv", tm=1024, tn=512)
        if l < N_A:
            d_o, dz, dnw = _block_bwd(_post_fn, sv["post_in"], sv["post_specs"], [dmixin], [_rows(tb, DN_W)],
                                      ["s", "s", "a"],
                                      [_sds((t_total, DN_W)), _sds((t_total, DN_W), BF16), _sds((1, DN_D))],
                                      [_rows(tb, DN_W), _rows(tb, DN_W), _whole((1, DN_D))], (nb,), "post_bwd")
            sg["a_norm_w"][l] = dnw
            dqd, dkd, du, dw, da, dcd = _dn2_bwd(*sv["dn2_in"], sv["states"], d_o, tdn)
            hs = _head_spec(tdn)
            full = _sds((t_total, DN_W))
            dq, dk, dv, dgc, dbeta = _block_bwd(
                _dn1_fn, sv["dn1_in"], [hs] * 5, [du, dw, da, dqd, dkd, dcd],
                [hs, hs, _intra_spec(tdn), hs, hs, hs], ["s"] * 5, [full] * 5, [hs] * 5,
                (DN_HEADS, t_total // tdn), "dn1_bwd")
            if l == N_A - 1:
                _DIAG[:] = [1.0 - jnp.all(jnp.isfinite(t)).astype(F32)
                            for t in (d_o, dqd, dkd, du, dw, da, dcd, dq, dk, dv, dgc, dbeta)]
            dc, dab, dalog, ddtb = _block_bwd(
                _rowa_fn, sv["rowa_in"], sv["rowa_specs"], [dq, dk, dv, dgc, dbeta], [_rows(tb, DN_W)] * 5,
                ["s", "s", "a", "a"],
                [_sds((t_total, 3 * DN_W)), _sds((t_total, 128), BF16), _sds((1, 128)), _sds((1, 128))],
                [_rows(tb, 3 * DN_W), _rows(tb, 128), _whole((1, 128)), _whole((1, 128))], (nb,), "rowa_bwd")
            sg["a_A_log"][l] = dalog[0, :DN_HEADS]
            sg["a_dt_bias"][l] = ddtb[0, :DN_HEADS]
            dx, dconv = _conv_bwd(dc, sv["proj"], sv["conv_w"], tb)
            sg["a_conv_w"][l] = dconv
            dproj = jnp.concatenate([dx, dz, dqm, dab], axis=1)
            g_in = _matmul(sv["hb"], dproj, "tn", [F32], "mm_dw_a", tm=1024, tn=1152)
            grads["a_w_in"][l] = jnp.concatenate([g_in[:, :3072], g_in[:, 3328:3340], g_in[:, 3072:3328]], axis=1)
            dh = _matmul(dproj, w_a[l], "nt", [F32], "mm_dx_a", epi=_add_epilogue, extras=[dha], tn=1024, tk=1152)
            if l == N_A - 1:
                _DIAG.extend(1.0 - jnp.all(jnp.isfinite(t.astype(F32))).astype(F32) for t in (dc, dab, dx, dh))
        else:
            jb = l - N_A
            swa_kinds = ["s", None, None, "s", "s", "s", "s", "a"]
            halo_spec = pl.BlockSpec((None, WINDOW, 256), lambda i: (i, 0, 0))
            dq, dkh, dkc, dvh, dvc, dsink = _block_bwd(
                _swa_fn, sv["swa_in"], sv["swa_specs"], [dmixin], [_rows(tsw, DN_W)], swa_kinds,
                [_sds((t_total, DN_W), BF16), _sds((nbs, WINDOW, 256)), _sds((t_total, 256)),
                 _sds((nbs, WINDOW, 256)), _sds((t_total, 256)), _sds((1, 128))],
                [_rows(tsw, DN_W), halo_spec, _rows(tsw, 256), halo_spec, _rows(tsw, 256), _whole((1, 128))],
                (nbs,), "swa_bwd")
            sg["b_sinks"][jb] = dsink[0, :SWA_HEADS]
            dk_parts.append((dkc, dkh))
            dv_parts.append((dvc, dvh))
            dproj = jnp.concatenate([dq, dqm], axis=1)
            grads["b_w_in"][jb] = _matmul(sv["hb"], dproj, "tn", [F32], "mm_dw_b", tm=1024, tn=512)
            dh = _matmul(dproj, wts["b_w_in"][jb], "nt", [F32], "mm_dx_b", epi=_add_epilogue, extras=[dha], tn=1024)

    big = []
    for n in BIG_NAMES:
        if n == "w_kv_shared":
            big.append(grads[n])
        else:
            big.append(jnp.stack(grads[n], axis=0))
    small_grads = dict(
        a_conv_w=jnp.stack(sg["a_conv_w"]), a_A_log=jnp.stack(sg["a_A_log"]), a_dt_bias=jnp.stack(sg["a_dt_bias"]),
        a_norm_w=jnp.concatenate(sg["a_norm_w"], axis=0), b_sinks=jnp.stack(sg["b_sinks"]),
        ln_g=jnp.stack([jnp.concatenate(p, axis=0) for p in sg["ln_g"]]),
        ln_b=jnp.stack([jnp.concatenate(p, axis=0) for p in sg["ln_b"]]))
    return loss, dh, big, small_grads


def _pack(arrays, rows):
    flat = []
    for a in arrays:
        v = a.astype(F32).reshape(-1)
        flat.append(jnp.pad(v, (0, (-v.shape[0]) % 128)))
    flat = jnp.concatenate(flat)
    return jnp.pad(flat, (0, rows * 128 - flat.shape[0])).reshape(rows, 128)


def _unpack(slab, shapes):
    flat = slab.reshape(slab.shape[:-2] + (-1,))
    out, off = [], 0
    for s in shapes:
        n = math.prod(s)
        out.append(flat[..., off:off + n].reshape(slab.shape[:-2] + tuple(s)))
        off += n + (-n) % 128
    return out


def _rows_for(shapes):
    rows = sum((math.prod(s) + 127) // 128 for s in shapes)
    return rows + (-rows) % 8


SMALL_NAMES = ("a_conv_w", "a_A_log", "a_dt_bias", "a_norm_w", "b_sinks", "ln_g", "ln_b")
SMALL_SHARDED = {"a_conv_w": 2, "ln_g": 2, "ln_b": 2}
SMALL_FULL = {"a_conv_w": (2, 4, 2304), "a_A_log": (2, 6), "a_dt_bias": (2, 6), "a_norm_w": (2, 128),
              "b_sinks": (2, 12), "ln_g": (4, 2, 1024), "ln_b": (4, 2, 1024)}


def kernel(x, mem, positions, a_w_in, a_conv_w, a_A_log, a_dt_bias, a_norm_w, b_w_in, b_sinks, w_kv_shared, mem_w_kv, w_o, mlp_w_up, mlp_w_down, ln_g, ln_b, loss_target, m_a_w_in, m_a_conv_w, m_a_A_log, m_a_dt_bias, m_a_norm_w, m_b_w_in, m_b_sinks, m_w_kv_shared, m_mem_w_kv, m_w_o, m_mlp_w_up, m_mlp_w_down, m_ln_g, m_ln_b, v_a_w_in, v_a_conv_w, v_a_A_log, v_a_dt_bias, v_a_norm_w, v_b_w_in, v_b_sinks, v_w_kv_shared, v_mem_w_kv, v_w_o, v_mlp_w_up, v_mlp_w_down, v_ln_g, v_ln_b):
    params = dict(a_w_in=a_w_in, a_conv_w=a_conv_w, a_A_log=a_A_log, a_dt_bias=a_dt_bias, a_norm_w=a_norm_w,
                  b_w_in=b_w_in, b_sinks=b_sinks, w_kv_shared=w_kv_shared, mem_w_kv=mem_w_kv, w_o=w_o,
                  mlp_w_up=mlp_w_up, mlp_w_down=mlp_w_down, ln_g=ln_g, ln_b=ln_b)
    mom = dict(a_w_in=m_a_w_in, a_conv_w=m_a_conv_w, a_A_log=m_a_A_log, a_dt_bias=m_a_dt_bias, a_norm_w=m_a_norm_w,
               b_w_in=m_b_w_in, b_sinks=m_b_sinks, w_kv_shared=m_w_kv_shared, mem_w_kv=m_mem_w_kv, w_o=m_w_o,
               mlp_w_up=m_mlp_w_up, mlp_w_down=m_mlp_w_down, ln_g=m_ln_g, ln_b=m_ln_b)
    var = dict(a_w_in=v_a_w_in, a_conv_w=v_a_conv_w, a_A_log=v_a_A_log, a_dt_bias=v_a_dt_bias, a_norm_w=v_a_norm_w,
               b_w_in=v_b_w_in, b_sinks=v_b_sinks, w_kv_shared=v_w_kv_shared, mem_w_kv=v_mem_w_kv, w_o=v_w_o,
               mlp_w_up=v_mlp_w_up, mlp_w_down=v_mlp_w_down, ln_g=v_ln_g, ln_b=v_ln_b)
    me = 4 * lax.axis_index("x") + 2 * lax.axis_index("y") + lax.axis_index("c")

    full_b = _allgather_weights([params[n].astype(BF16) for n in BIG_NAMES])
    wts = dict(zip(BIG_NAMES, full_b))
    sharded_names = [n for n in SMALL_NAMES if n in SMALL_SHARDED]
    shard_shapes = [params[n].shape for n in sharded_names]
    gathered = _small_exchange(_pack([params[n] for n in sharded_names], _rows_for(shard_shapes)), reduce=False)
    small = {n: params[n] for n in SMALL_NAMES if n not in SMALL_SHARDED}
    for n, g in zip(sharded_names, _unpack(gathered, shard_shapes)):
        small[n] = jnp.moveaxis(g, 0, 2).reshape(SMALL_FULL[n])

    loss, dx, big_grads, small_grads = _local_step(x[0], mem[0], positions[0], loss_target[0], wts, small)
    loss = lax.psum(loss, ("x", "y", "c"))
    diag = sum(lax.pmax(b, ("x", "y", "c")) * 2.0 ** -(i + 1) for i, b in enumerate(_DIAG))
    loss = loss * (1.0 + diag)
    recv = _scatter_grads(big_grads)
    out = {}
    for n, r in zip(BIG_NAMES, recv):
        shp = params[n].shape
        rows = math.prod(shp[:-1])
        res = _adamw(r.reshape(N_DEV, rows, shp[-1]), params[n].reshape(rows, shp[-1]),
                     mom[n].reshape(rows, shp[-1]), var[n].reshape(rows, shp[-1]), 32, "adamw_" + n)
        out[n] = [t.reshape(shp) for t in res]
    full_shapes = [SMALL_FULL[n] for n in SMALL_NAMES]
    summed = _small_exchange(_pack([small_grads[n] for n in SMALL_NAMES], _rows_for(full_shapes)), reduce=True)
    local_g = []
    for n, g in zip(SMALL_NAMES, _unpack(summed, full_shapes)):
        if n in SMALL_SHARDED:
            size = params[n].shape[2]
            g = lax.dynamic_slice_in_dim(g, me * size, size, axis=2)
        local_g.append(g)
    local_shapes = [params[n].shape for n in SMALL_NAMES]
    rows = _rows_for(local_shapes)
    res = _adamw(_pack(local_g, rows)[None], _pack([params[n] for n in SMALL_NAMES], rows),
                 _pack([mom[n] for n in SMALL_NAMES], rows), _pack([var[n] for n in SMALL_NAMES], rows), rows,
                 "adamw_small")
    unpacked = [_unpack(t, local_shapes) for t in res]
    for i, n in enumerate(SMALL_NAMES):
        out[n] = [unpacked[k][i] for k in range(4)]

    order = ("a_w_in", "a_conv_w", "a_A_log", "a_dt_bias", "a_norm_w", "b_w_in", "b_sinks", "w_kv_shared",
             "mem_w_kv", "w_o", "mlp_w_up", "mlp_w_down", "ln_g", "ln_b")
    return (loss, dx[None], *[out[n][0] for n in order], *[out[n][1] for n in order],
            *[out[n][2] for n in order], *[out[n][3] for n in order])
```

```python
import functools
import math

import jax
import jax.numpy as jnp
from jax import lax
from jax.experimental import pallas as pl
from jax.experimental.pallas import tpu as pltpu

F32 = jnp.float32
BF16 = jnp.bfloat16

D_MODEL = 1024
DEPTH = 4
N_A = 2
MEM_HEADS = 4
MEM_DH = 64
MEM_W = 256
DN_HEADS = 6
DN_D = 128
DN_W = 768
CHUNK = 64
SWA_DH = 64
SWA_HEADS = 12
WINDOW = 128
ROPE_THETA = 10000.0
LN_EPS = 1e-5
NORM_EPS = 1e-6
DN_ALPHA = (2.0 * DEPTH) ** 0.25
A_IN = 3340
A_IN_PAD = 3456
N_DEV = 8

ADAM_LR = 0.001
ADAM_B1 = 0.9
ADAM_B2 = 0.999
ADAM_EPS = 1e-08
ADAM_WD = 0.01
ADAM_STEP = 10

VMEM_LIMIT = 48 * 1024 * 1024
NEG_BIG = -1e30


def _cparams(sem):
    return pltpu.CompilerParams(dimension_semantics=sem, vmem_limit_bytes=VMEM_LIMIT)


_CONTRACT = {"nn": (1, 0), "nt": (1, 1), "tn": (0, 0)}


def _raw_mm(a, b, mode, prec):
    ca, cb = _CONTRACT[mode]
    dims = (((ca,), (cb,)), ((), ()))
    dot = lambda p, q: lax.dot_general(p, q, dims, preferred_element_type=F32)
    if prec == "bf16":
        return dot(a.astype(BF16), b.astype(BF16))
    a, b = a.astype(F32), b.astype(F32)
    a_hi, b_hi = a.astype(BF16), b.astype(BF16)
    if prec == "sela":
        return dot(a_hi, b_hi) + dot(a_hi, (b - b_hi.astype(F32)).astype(BF16))
    a_lo = (a - a_hi.astype(F32)).astype(BF16)
    if prec == "selb":
        return dot(a_hi, b_hi) + dot(a_lo, b_hi)
    b_lo = (b - b_hi.astype(F32)).astype(BF16)
    return dot(a_hi, b_hi) + (dot(a_hi, b_lo) + dot(a_lo, b_hi))


@functools.partial(jax.custom_vjp, nondiff_argnums=(2, 3))
def mm(a, b, mode, prec):
    return _raw_mm(a, b, mode, prec)


def _mm_fwd(a, b, mode, prec):
    return _raw_mm(a, b, mode, prec), (a, b)


def _mm_bwd(mode, prec, res, ct):
    a, b = res
    if prec == "sela":
        pa, pb = "f32", {"nn": "sela", "nt": "selb", "tn": "sela"}[mode]
    elif prec == "selb":
        pa, pb = {"nn": "selb", "nt": "selb", "tn": "sela"}[mode], "f32"
    else:
        pa = pb = prec
    if mode == "nn":
        return mm(ct, b, "nt", pa), mm(a, ct, "tn", pb)
    if mode == "nt":
        return mm(ct, b, "nn", pa), mm(ct, a, "tn", pb)
    return mm(b, ct, "nt", pa), mm(a, ct, "nn", pb)


mm.defvjp(_mm_fwd, _mm_bwd)


@jax.custom_vjp
def _softplus(x):
    y = jnp.exp(-jnp.abs(x))
    log1p_y = jnp.where(y < 1e-2, y * (1.0 - y * (0.5 - y * (1.0 / 3.0))), jnp.log(1.0 + y))
    return jnp.maximum(x, 0.0) + log1p_y


def _softplus_fwd(x):
    return _softplus(x), x


def _softplus_bwd(x, ct):
    return (ct * jax.nn.sigmoid(x),)


_softplus.defvjp(_softplus_fwd, _softplus_bwd)


def _iota(shape, dim):
    return lax.broadcasted_iota(jnp.int32, shape, dim)


def _block_fwd(fn, ins, in_specs, out_shapes, out_specs, grid, name):
    n_in = len(ins)

    def body(*refs):
        pids = tuple(pl.program_id(a) for a in range(len(grid)))
        vals = [r[...].astype(F32) for r in refs[:n_in]]
        outs = fn(pids, *vals)
        for r, o in zip(refs[n_in:], outs):
            r[...] = o.astype(r.dtype)

    return pl.pallas_call(
        body, grid=grid, in_specs=in_specs, out_specs=out_specs, out_shape=out_shapes, name=name,
        compiler_params=_cparams(("parallel",) * len(grid)))(*ins)


def _block_bwd(fn, ins, in_specs, cts, ct_specs, kinds, g_shapes, g_specs, grid, name):
    n_in, n_ct = len(ins), len(cts)
    didx = [i for i, k in enumerate(kinds) if k]

    def body(*refs):
        in_refs, ct_refs, g_refs = refs[:n_in], refs[n_in:n_in + n_ct], refs[n_in + n_ct:]
        pids = tuple(pl.program_id(a) for a in range(len(grid)))
        vals = [r[...].astype(F32) for r in in_refs]

        def f(*dvals):
            full = list(vals)
            for i, v in zip(didx, dvals):
                full[i] = v
            return tuple(fn(pids, *full))

        _, vjp = jax.vjp(f, *[vals[i] for i in didx])
        gs = vjp(tuple(r[...].astype(F32) for r in ct_refs))
        first = pids[0] == 0
        for p in pids[1:]:
            first = jnp.logical_and(first, p == 0)
        for i, g, r in zip(didx, gs, g_refs):
            if kinds[i] == "s":
                r[...] = g.astype(r.dtype)
            else:
                @pl.when(first)
                def _(r=r):
                    r[...] = jnp.zeros(r.shape, r.dtype)

                r[...] += g.astype(r.dtype)

    sem = ("arbitrary",) * len(grid) if "a" in kinds else ("parallel",) * len(grid)
    return pl.pallas_call(
        body, grid=grid, in_specs=list(in_specs) + list(ct_specs), out_specs=g_specs, out_shape=g_shapes,
        name=name, compiler_params=_cparams(sem))(*ins, *cts)


def _rows(tb, width, col=0):
    return pl.BlockSpec((tb, width), lambda i, col=col: (i, col))


def _whole(shape):
    return pl.BlockSpec(shape, lambda *_: (0,) * len(shape))


def _sds(shape, dtype=F32):
    return jax.ShapeDtypeStruct(shape, dtype)


def _matmul(a, b, mode, out_dtypes, name, epi=None, extras=(), tm=1024, tn=1024, tk=1024,
            b_spec=None, n_total=None, out_specs=None, out_shapes=None):
    if mode == "tn":
        k_total, m_total = a.shape
    else:
        m_total, k_total = a.shape
    if n_total is None:
        n_total = b.shape[0] if mode == "nt" else b.shape[1]
    tm, tn, tk = min(tm, m_total), min(tn, n_total), min(tk, k_total)
    assert m_total % tm == 0 and n_total % tn == 0 and k_total % tk == 0, (name, a.shape, b.shape)
    grid = (m_total // tm, n_total // tn, k_total // tk)
    nk = grid[2]
    if mode == "tn":
        a_spec = pl.BlockSpec((tk, tm), lambda i, j, k: (k, i))
    else:
        a_spec = pl.BlockSpec((tm, tk), lambda i, j, k: (i, k))
    if b_spec is None:
        if mode == "nt":
            b_spec = pl.BlockSpec((tn, tk), lambda i, j, k: (j, k))
        else:
            b_spec = pl.BlockSpec((tk, tn), lambda i, j, k: (k, j))
    tile = pl.BlockSpec((tm, tn), lambda i, j, k: (i, j))
    n_ex, n_out = len(extras), len(out_dtypes)
    ca, cb = _CONTRACT[mode]
    dims = (((ca,), (cb,)), ((), ()))

    def body(*refs):
        a_ref, b_ref = refs[:2]
        ex_refs = refs[2:2 + n_ex]
        out_refs = refs[2 + n_ex:2 + n_ex + n_out]
        part = lax.dot_general(a_ref[...], b_ref[...], dims, preferred_element_type=F32)

        def finish(val):
            res = epi(val, *[e[...] for e in ex_refs]) if epi is not None else (val,)
            for r, o in zip(out_refs, res):
                r[...] = o.astype(r.dtype)

        if nk == 1:
            finish(part)
        else:
            acc = refs[-1]
            k = pl.program_id(2)

            @pl.when(k == 0)
            def _():
                acc[...] = part

            @pl.when(k > 0)
            def _():
                acc[...] += part

            @pl.when(k == nk - 1)
            def _():
                finish(acc[...])

    if out_shapes is None:
        out_shapes = [_sds((m_total, n_total), d) for d in out_dtypes]
        out_specs = [tile] * n_out
    outs = pl.pallas_call(
        body, grid=grid, in_specs=[a_spec, b_spec] + [tile] * n_ex, out_specs=out_specs, out_shape=out_shapes,
        scratch_shapes=[pltpu.VMEM((tm, tn), F32)] if nk > 1 else [], name=name,
        compiler_params=_cparams(("parallel", "parallel", "arbitrary")))(a, b, *extras)
    return outs if n_out > 1 else outs[0]


def _silu(x):
    return x * jax.nn.sigmoid(x)


def _rowa_fn(pids, c, ab, alog, dtb):
    tb = c.shape[0]
    s = _silu(c)
    qs, ks = [], []
    for h in range(DN_HEADS):
        qh = s[:, DN_D * h:DN_D * (h + 1)]
        qs.append(qh * lax.rsqrt(jnp.sum(qh * qh, axis=-1, keepdims=True) + NORM_EPS) * (DN_D ** -0.5))
        kh = s[:, DN_W + DN_D * h:DN_W + DN_D * (h + 1)]
        ks.append(kh * lax.rsqrt(jnp.sum(kh * kh, axis=-1, keepdims=True) + NORM_EPS))
    q = jnp.concatenate(qs, axis=1)
    k = jnp.concatenate(ks, axis=1)
    v = s[:, 2 * DN_W:3 * DN_W]
    g128 = -jnp.exp(alog) * _softplus(ab + dtb)
    b128 = jax.nn.sigmoid(ab)
    r, cc = _iota((tb, tb), 0), _iota((tb, tb), 1)
    tri = jnp.where(((r >> 6) == (cc >> 6)) & (r >= cc), 1.0, 0.0)
    gc128 = mm(tri, g128, "nn", "sela")
    lane, col = _iota((128, DN_W), 0), _iota((128, DN_W), 1)
    exp_a = jnp.where(lane == (col >> 7), 1.0, 0.0)
    exp_b = jnp.where(lane == (col >> 7) + DN_HEADS, 1.0, 0.0)
    return q, k, v, mm(gc128, exp_a, "nn", "selb"), mm(b128, exp_b, "nn", "selb")


def _tri_inv_raw(low, block):
    n = low.shape[0]
    r, c = _iota((n, n), 0), _iota((n, n), 1)
    lg = 0
    x = None
    while (1 << lg) < block:
        off = ((r >> (lg + 1)) == (c >> (lg + 1))) & (((r >> lg) & 1) == 1) & (((c >> lg) & 1) == 0)
        cblk = jnp.where(off, low, 0.0)
        if x is None:
            x = jnp.where(r == c, 1.0, 0.0) - cblk
        else:
            x = x - mm(x, mm(cblk, x, "nn", "f32"), "nn", "f32")
        lg += 1
    return x


def _tri_inv_cotangent(block, x, ct):
    n = x.shape[0]
    r, c = _iota((n, n), 0), _iota((n, n), 1)
    shift = block.bit_length() - 1
    g = mm(mm(x, ct, "tn", "f32"), x, "nt", "f32")
    return jnp.where(((r >> shift) == (c >> shift)) & (r > c), -g, 0.0)


@functools.partial(jax.custom_vjp, nondiff_argnums=(1,))
def _tri_inv(low, block):
    return _tri_inv_raw(low, block)


def _tri_inv_fwd(low, block):
    x = _tri_inv_raw(low, block)
    return x, x


def _tri_inv_bwd(block, x, ct):
    return (_tri_inv_cotangent(block, x, ct),)


_tri_inv.defvjp(_tri_inv_fwd, _tri_inv_bwd)


@functools.partial(jax.custom_vjp, nondiff_argnums=(2,))
def _tri_inv_known(low, known, block):
    return known


def _tri_inv_known_fwd(low, known, block):
    return known, known


def _tri_inv_known_bwd(block, x, ct):
    return _tri_inv_cotangent(block, x, ct), jnp.zeros_like(x)


_tri_inv_known.defvjp(_tri_inv_known_fwd, _tri_inv_known_bwd)


PAIR = 2 * CHUNK


def _dn1_pair(q, k, v, gc, beta, tinv_known=None):
    n = q.shape[0]
    assert n == PAIR == DN_D
    eg = jnp.exp(gc)
    kb = k * beta
    onehot = jnp.where(_iota((n, DN_D), 1) == 0, 1.0, 0.0)
    g_row = mm(onehot, gc, "nt", "sela")
    r, c = _iota((n, n), 0), _iota((n, n), 1)
    same = (r >> 6) == (c >> 6)
    incl, strict = same & (r >= c), same & (r > c)
    decay = jnp.exp(jnp.where(incl, gc - g_row, NEG_BIG))
    low = jnp.where(strict, mm(kb, k, "nt", "bf16") * decay, 0.0)
    tinv = _tri_inv(low, CHUNK) if tinv_known is None else _tri_inv_known(low, tinv_known, CHUNK)
    uw = mm(tinv, jnp.concatenate([v * beta, kb * eg], axis=1), "nn", "f32")
    u, w = uw[:, :DN_D], uw[:, DN_D:]
    intra = jnp.where(incl, mm(q, k, "nt", "bf16") * decay, 0.0)
    row = _iota((n, DN_D), 0)
    last0 = jnp.sum(jnp.where(row == CHUNK - 1, gc, 0.0), axis=0, keepdims=True)
    last1 = jnp.sum(jnp.where(row == PAIR - 1, gc, 0.0), axis=0, keepdims=True)
    g_last = jnp.where(row < CHUNK, last0, last1)
    qd = q * eg
    kd = k * jnp.exp(g_last - gc)
    cd = jnp.exp(g_last)
    return u, w, intra, qd, kd, cd, tinv


def _dn1_fn(pids, q, k, v, gc, beta):
    outs = []
    for j in range(q.shape[0] // PAIR):
        sl = slice(PAIR * j, PAIR * (j + 1))
        outs.append(_dn1_pair(q[sl], k[sl], v[sl], gc[sl], beta[sl]))
    return tuple(jnp.concatenate([o[t] for o in outs], axis=0) for t in range(7))


def _dn1_fn_known(pids, q, k, v, gc, beta, tinv):
    outs = []
    for j in range(q.shape[0] // PAIR):
        sl = slice(PAIR * j, PAIR * (j + 1))
        outs.append(_dn1_pair(q[sl], k[sl], v[sl], gc[sl], beta[sl], tinv[sl])[:6])
    return tuple(jnp.concatenate([o[t] for o in outs], axis=0) for t in range(6))


def _dn2_step(half, state, qd, kd, u, w, intra, cd_row):
    v_new = u - mm(w, state, "nn", "bf16")
    zeros = jnp.zeros_like(v_new)
    v_pair = jnp.concatenate([v_new, zeros] if half == 0 else [zeros, v_new], axis=0)
    out = mm(qd, state, "nn", "bf16") + mm(intra, v_pair, "nn", "bf16")
    return out, state * cd_row + mm(kd, v_new, "tn", "bf16")


def _post_fn(pids, o, z, nw):
    outs = []
    for h in range(DN_HEADS):
        oh = o[:, DN_D * h:DN_D * (h + 1)]
        zh = z[:, DN_D * h:DN_D * (h + 1)]
        y = oh * lax.rsqrt(jnp.mean(oh * oh, axis=-1, keepdims=True) + NORM_EPS) * nw
        outs.append(y * _silu(zh))
    return (jnp.concatenate(outs, axis=1),)


def _memattn_fn(pids, qm, kvm):
    kmem, vmem = kvm[:, :MEM_W], kvm[:, MEM_W:]
    lane = _iota((1, MEM_W), 1)
    out = None
    for h in range(MEM_HEADS):
        hm = jnp.where((lane >> 6) == h, 1.0, 0.0)
        s = mm(qm * hm, kmem, "nt", "bf16") * (MEM_DH ** -0.5)
        m = lax.stop_gradient(jnp.max(s, axis=-1, keepdims=True))
        e = jnp.exp(s - m)
        p = e / jnp.sum(e, axis=-1, keepdims=True)
        oh = mm(p, vmem, "nn", "bf16") * hm
        out = oh if out is None else out + oh
    return (out,)


def _ln_fn(pids, h, mix, g, b):
    x = DN_ALPHA * h + mix
    mu = jnp.mean(x, axis=-1, keepdims=True)
    xc = x - mu
    var = jnp.mean(xc * xc, axis=-1, keepdims=True)
    return (xc * lax.rsqrt(var + LN_EPS) * g + b,)


def _rope_matrix():
    i, j = _iota((128, 128), 0), _iota((128, 128), 1)
    jj = j & 63
    return jnp.where((jj < 32) & (i == j + 32), -1.0, 0.0) + jnp.where((jj >= 32) & (i == j - 32), 1.0, 0.0)


def _rope128(x, cos, sin, rot):
    return x * cos + mm(x, rot, "nn", "selb") * sin


def _krope_fn(pids, kraw, cos, sin):
    rot = _rope_matrix()
    return (jnp.concatenate([_rope128(kraw[:, 128 * g:128 * (g + 1)], cos, sin, rot) for g in range(2)], axis=1),)


def _swa_fn(pids, qraw, cos, sin, k_halo, k_cur, v_halo, v_cur, sinks):
    tb = qraw.shape[0]
    nwin = tb // WINDOW
    rot = _rope_matrix()
    kcat = jnp.concatenate([k_halo, k_cur], axis=0)
    vcat = jnp.concatenate([v_halo, v_cur], axis=0)
    lane = _iota((1, 128), 1)
    halves = (jnp.where(lane < 64, 1.0, 0.0), jnp.where(lane >= 64, 1.0, 0.0))
    qi, kj = _iota((WINDOW, 2 * WINDOW), 0), _iota((WINDOW, 2 * WINDOW), 1)
    diff = qi + WINDOW - kj
    band = (diff >= 0) & (diff < WINDOW)
    acc = [[None] * (SWA_HEADS // 2) for _ in range(nwin)]
    for p in range(SWA_HEADS // 2):
        qg = _rope128(qraw[:, 128 * p:128 * (p + 1)], cos, sin, rot)
        kv = p // 3
        for hh in range(2):
            head = 2 * p + hh
            sink = jnp.sum(jnp.where(lane == head, sinks, 0.0), axis=-1, keepdims=True)
            qh = qg * halves[hh]
            for w in range(nwin):
                keys = kcat[WINDOW * w:WINDOW * (w + 2), 128 * kv:128 * (kv + 1)]
                vals = vcat[WINDOW * w:WINDOW * (w + 2), 128 * kv:128 * (kv + 1)]
                s = mm(qh[WINDOW * w:WINDOW * (w + 1)], keys, "nt", "bf16") * (SWA_DH ** -0.5)
                valid = band & ((pids[0] * tb + WINDOW * (w - 1) + kj) >= 0)
                s = jnp.where(valid, s, NEG_BIG)
                m = lax.stop_gradient(jnp.maximum(jnp.max(s, axis=-1, keepdims=True), sink))
                e = jnp.exp(s - m)
                denom = jnp.sum(e, axis=-1, keepdims=True) + jnp.exp(sink - m)
                o = mm(e / denom, vals, "nn", "bf16") * halves[hh]
                acc[w][p] = o if hh == 0 else acc[w][p] + o
    return (jnp.concatenate([jnp.concatenate(acc[w], axis=1) for w in range(nwin)], axis=0),)


def _conv_fwd(proj, conv_w, tb):
    t_total = proj.shape[0]
    width = conv_w.shape[1]
    nb = t_total // tb

    def body(cur_ref, prev_ref, w_ref, out_ref):
        i = pl.program_id(0)
        prev = jnp.where(i > 0, prev_ref[...], 0.0)
        xcat = jnp.concatenate([prev, cur_ref[...]], axis=0)
        acc = xcat[8:] * w_ref[3:4, :]
        for j in range(3):
            acc = acc + pltpu.roll(xcat, 3 - j, 0)[8:] * w_ref[j:j + 1, :]
        out_ref[...] = acc

    return pl.pallas_call(
        body, grid=(nb,),
        in_specs=[pl.BlockSpec((tb, width), lambda i: (i, 0)),
                  pl.BlockSpec((8, width), lambda i: (jnp.maximum(i * (tb // 8) - 1, 0), 0)),
                  _whole((4, width))],
        out_specs=pl.BlockSpec((tb, width), lambda i: (i, 0)), out_shape=_sds((t_total, width)),
        name="conv_fwd", compiler_params=_cparams(("parallel",)))(proj, proj, conv_w)


def _conv_bwd(dc, proj, conv_w, tb):
    t_total, width = dc.shape
    nb = t_total // tb

    def body(dcur_ref, dnext_ref, cur_ref, prev_ref, w_ref, dx_ref, dw_ref):
        i = pl.program_id(0)
        dnext = jnp.where(i < nb - 1, dnext_ref[...], 0.0)
        dcur = dcur_ref[...]
        dcat = jnp.concatenate([dcur, dnext], axis=0)
        prev = jnp.where(i > 0, prev_ref[...], 0.0)
        xcat = jnp.concatenate([prev, cur_ref[...]], axis=0)

        @pl.when(i == 0)
        def _():
            dw_ref[...] = jnp.zeros(dw_ref.shape, F32)

        dx = dcur * w_ref[3:4, :]
        dw_ref[3:4, :] += jnp.sum(dcur * xcat[8:], axis=0, keepdims=True)
        for j in range(3):
            dx = dx + pltpu.roll(dcat, 8 - (3 - j), 0)[8:] * w_ref[j:j + 1, :]
            dw_ref[j:j + 1, :] += jnp.sum(dcur * pltpu.roll(xcat, 3 - j, 0)[8:], axis=0, keepdims=True)
        dx_ref[...] = dx.astype(dx_ref.dtype)

    return pl.pallas_call(
        body, grid=(nb,),
        in_specs=[pl.BlockSpec((tb, width), lambda i: (i, 0)),
                  pl.BlockSpec((8, width), lambda i: (jnp.minimum((i + 1) * (tb // 8), t_total // 8 - 1), 0)),
                  pl.BlockSpec((tb, width), lambda i: (i, 0)),
                  pl.BlockSpec((8, width), lambda i: (jnp.maximum(i * (tb // 8) - 1, 0), 0)),
                  _whole((4, width))],
        out_specs=[pl.BlockSpec((tb, width), lambda i: (i, 0)), _whole((4, width))],
        out_shape=[_sds((t_total, width), BF16), _sds((4, width))],
        name="conv_bwd", compiler_params=_cparams(("arbitrary",)))(dc, dc, proj, proj, conv_w)


def _head_spec(tb, nb=None):
    if nb is None:
        return pl.BlockSpec((tb, DN_D), lambda h, i: (i, h))
    return pl.BlockSpec((tb, DN_D), lambda h, i: (nb - 1 - i, h))


def _intra_spec(tb, nb=None):
    if nb is None:
        return pl.BlockSpec((None, tb, PAIR), lambda h, i: (h, i, 0))
    return pl.BlockSpec((None, tb, PAIR), lambda h, i: (h, nb - 1 - i, 0))


def _state_spec(tb, nb=None):
    if nb is None:
        return pl.BlockSpec((None, tb // CHUNK, DN_D, DN_D), lambda h, i: (h, i, 0, 0))
    return pl.BlockSpec((None, tb // CHUNK, DN_D, DN_D), lambda h, i: (h, nb - 1 - i, 0, 0))


def _dn2_fwd(qd, kd, u, w, intra, cd, tb):
    t_total = qd.shape[0]
    nb = t_total // tb
    hs = _head_spec(tb)

    def body(qd_ref, kd_ref, u_ref, w_ref, a_ref, cd_ref, o_ref, save_ref, state):
        @pl.when(pl.program_id(1) == 0)
        def _():
            state[...] = jnp.zeros(state.shape, F32)

        for j in range(tb // CHUNK):
            sl = pl.ds(CHUNK * j, CHUNK)
            s0 = state[...]
            save_ref[j] = s0
            out, s1 = _dn2_step(j % 2, s0, qd_ref[sl, :], kd_ref[sl, :], u_ref[sl, :], w_ref[sl, :], a_ref[sl, :],
                                cd_ref[pl.ds(CHUNK * j, 1), :])
            o_ref[sl, :] = out
            state[...] = s1

    return pl.pallas_call(
        body, grid=(DN_HEADS, nb), in_specs=[hs, hs, hs, hs, _intra_spec(tb), hs],
        out_specs=[hs, _state_spec(tb)],
        out_shape=[_sds((t_total, DN_W)), _sds((DN_HEADS, t_total // CHUNK, DN_D, DN_D))],
        scratch_shapes=[pltpu.VMEM((DN_D, DN_D), F32)], name="dn2_fwd",
        compiler_params=_cparams(("parallel", "arbitrary")))(qd, kd, u, w, intra, cd)


def _dn2_bwd(qd, kd, u, w, intra, cd, saved, d_o, tb):
    t_total = qd.shape[0]
    nb = t_total // tb
    hs = _head_spec(tb, nb)

    def body(qd_ref, kd_ref, u_ref, w_ref, a_ref, cd_ref, save_ref, do_ref,
             dqd_ref, dkd_ref, du_ref, dw_ref, da_ref, dcd_ref, dstate):
        @pl.when(pl.program_id(1) == 0)
        def _():
            dstate[...] = jnp.zeros(dstate.shape, F32)

        first_row = _iota((CHUNK, DN_D), 0) == 0
        for j in reversed(range(tb // CHUNK)):
            sl = pl.ds(CHUNK * j, CHUNK)
            _, vjp = jax.vjp(functools.partial(_dn2_step, j % 2), save_ref[j], qd_ref[sl, :], kd_ref[sl, :], u_ref[sl, :], w_ref[sl, :],
                             a_ref[sl, :], cd_ref[pl.ds(CHUNK * j, 1), :])
            ds0, dqd, dkd, du, dw, da, dcd = vjp((do_ref[sl, :], dstate[...]))
            dqd_ref[sl, :] = dqd
            dkd_ref[sl, :] = dkd
            du_ref[sl, :] = du
            dw_ref[sl, :] = dw
            da_ref[sl, :] = da
            dcd_ref[sl, :] = jnp.where(first_row, dcd, 0.0)
            dstate[...] = ds0

    full = _sds((t_total, DN_W))
    return pl.pallas_call(
        body, grid=(DN_HEADS, nb),
        in_specs=[hs, hs, hs, hs, _intra_spec(tb, nb), hs, _state_spec(tb, nb), hs],
        out_specs=[hs, hs, hs, hs, _intra_spec(tb, nb), hs],
        out_shape=[full, full, full, full, _sds((DN_HEADS, t_total, PAIR)), full],
        scratch_shapes=[pltpu.VMEM((DN_D, DN_D), F32)], name="dn2_bwd",
        compiler_params=_cparams(("parallel", "arbitrary")))(qd, kd, u, w, intra, cd, saved, d_o)


def _loss_and_grad(y, target, tb):
    t_total, d = y.shape

    def body(y_ref, t_ref, dy_ref, acc_ref):
        @pl.when(pl.program_id(0) == 0)
        def _():
            acc_ref[...] = jnp.zeros(acc_ref.shape, F32)

        err = y_ref[...] - t_ref[...]
        dy_ref[...] = err * (1.0 / d)
        acc_ref[...] += jnp.sum(err * err, axis=0, keepdims=True)

    dy, acc = pl.pallas_call(
        body, grid=(t_total // tb,), in_specs=[_rows(tb, d), _rows(tb, d)],
        out_specs=[_rows(tb, d), _whole((1, d))], out_shape=[_sds((t_total, d)), _sds((1, d))],
        name="loss", compiler_params=_cparams(("arbitrary",)))(y, target)
    return 0.5 * jnp.sum(acc) / d, dy


def _halo_sum(mains, halos, tb):
    t_total, width = mains[0].shape
    nb = t_total // tb
    n = len(mains)

    def body(*refs):
        out_ref = refs[-1]
        i = pl.program_id(0)
        tot = refs[0][...]
        for r in refs[1:n]:
            tot = tot + r[...]
        hal = refs[n][...]
        for r in refs[n + 1:2 * n]:
            hal = hal + r[...]
        hal = jnp.where(i < nb - 1, hal, 0.0)
        out_ref[...] = tot + jnp.concatenate([jnp.zeros((tb - WINDOW, width), F32), hal], axis=0)

    return pl.pallas_call(
        body, grid=(nb,),
        in_specs=[_rows(tb, width)] * n
        + [pl.BlockSpec((None, WINDOW, width), lambda i: (jnp.minimum(i + 1, nb - 1), 0, 0))] * n,
        out_specs=_rows(tb, width), out_shape=_sds((t_total, width)), name="halo_sum",
        compiler_params=_cparams(("parallel",)))(*mains, *halos)


def _adamw(recv, w, m, v, tr, name):
    slots, r_total, c_total = recv.shape
    tr = min(tr, r_total)
    assert r_total % tr == 0
    c1 = 1.0 / (1.0 - ADAM_B1 ** ADAM_STEP)
    c2 = 1.0 / (1.0 - ADAM_B2 ** ADAM_STEP)

    def body(recv_ref, w_ref, m_ref, v_ref, g_ref, d_ref, nm_ref, nv_ref):
        g = recv_ref[0]
        for s in range(1, slots):
            g = g + recv_ref[s]
        nm = ADAM_B1 * m_ref[...] + (1.0 - ADAM_B1) * g
        nv = ADAM_B2 * v_ref[...] + (1.0 - ADAM_B2) * (g * g)
        g_ref[...] = g
        nm_ref[...] = nm
        nv_ref[...] = nv
        d_ref[...] = -ADAM_LR * ((nm * c1) / (jnp.sqrt(nv * c2) + ADAM_EPS) + ADAM_WD * w_ref[...])

    blk = pl.BlockSpec((tr, c_total), lambda i: (i, 0))
    return pl.pallas_call(
        body, grid=(r_total // tr,),
        in_specs=[pl.BlockSpec((slots, tr, c_total), lambda i: (0, i, 0)), blk, blk, blk],
        out_specs=[blk] * 4, out_shape=[_sds((r_total, c_total))] * 4, name=name,
        compiler_params=_cparams(("parallel",)))(recv, w, m, v)


def _me_and_peers():
    x, y, c = lax.axis_index("x"), lax.axis_index("y"), lax.axis_index("c")
    me = 4 * x + 2 * y + c
    peers = []
    for k in range(1, N_DEV):
        px = 1 - x if (k >> 2) & 1 else x
        py = 1 - y if (k >> 1) & 1 else y
        pc = 1 - c if k & 1 else c
        peers.append(((px, py, pc), 4 * px + 2 * py + pc))
    return me, peers


def _small_exchange(packed, reduce):
    r_total = packed.shape[0]

    def body(p_ref, out_ref, gath_ref, send_sems, recv_sems):
        me, peers = _me_and_peers()
        gath_ref[me] = p_ref[...]
        copies = []
        for k, (dev, _) in enumerate(peers):
            cp = pltpu.make_async_remote_copy(src_ref=p_ref, dst_ref=gath_ref.at[me], send_sem=send_sems.at[k],
                                              recv_sem=recv_sems.at[k], device_id=dev,
                                              device_id_type=pl.DeviceIdType.MESH)
            cp.start()
            copies.append(cp)
        for k, (dev, idx) in enumerate(peers):
            pltpu.make_async_remote_copy(src_ref=p_ref, dst_ref=gath_ref.at[idx], send_sem=send_sems.at[k],
                                         recv_sem=recv_sems.at[k], device_id=dev,
                                         device_id_type=pl.DeviceIdType.MESH).wait_recv()
        for cp in copies:
            cp.wait_send()
        if reduce:
            tot = gath_ref[0]
            for d in range(1, N_DEV):
                tot = tot + gath_ref[d]
            out_ref[...] = tot
        else:
            out_ref[...] = gath_ref[...]

    out_shape = _sds((r_total, 128)) if reduce else _sds((N_DEV, r_total, 128))
    return pl.pallas_call(
        body, in_specs=[pl.BlockSpec(memory_space=pltpu.VMEM)], out_specs=pl.BlockSpec(memory_space=pltpu.VMEM),
        out_shape=out_shape,
        scratch_shapes=[pltpu.VMEM((N_DEV, r_total, 128), F32), pltpu.SemaphoreType.DMA((N_DEV - 1,)),
                        pltpu.SemaphoreType.DMA((N_DEV - 1,))],
        name="small_allreduce" if reduce else "small_allgather")(packed)


def _slot(ref, axis, idx, size):
    sel = [slice(None)] * len(ref.shape)
    sel[axis] = idx if size is None else pl.ds(pl.multiple_of(idx * size, size), size)
    return ref.at[tuple(sel)]


def _big_exchange(srcs, dst_shapes, src_view, dst_view, name):
    n = len(srcs)

    def body(*refs):
        src_refs, dst_refs = refs[:n], refs[n:2 * n]
        send_sems, recv_sems, local_sems = refs[2 * n:]
        me, peers = _me_and_peers()
        local, remote = [], []
        for t in range(n):
            loc = pltpu.make_async_copy(src_view(t, src_refs[t], me), dst_view(t, dst_refs[t], me), local_sems.at[t])
            loc.start()
            local.append(loc)
            for k, (dev, idx) in enumerate(peers):
                cp = pltpu.make_async_remote_copy(
                    src_ref=src_view(t, src_refs[t], idx), dst_ref=dst_view(t, dst_refs[t], me),
                    send_sem=send_sems.at[t, k], recv_sem=recv_sems.at[t, k], device_id=dev,
                    device_id_type=pl.DeviceIdType.MESH)
                cp.start()
                remote.append(cp)
        for t in range(n):
            for k, (dev, idx) in enumerate(peers):
                pltpu.make_async_remote_copy(
                    src_ref=src_view(t, src_refs[t], me), dst_ref=dst_view(t, dst_refs[t], idx),
                    send_sem=send_sems.at[t, k], recv_sem=recv_sems.at[t, k], device_id=dev,
                    device_id_type=pl.DeviceIdType.MESH).wait_recv()
        for cp in remote:
            cp.wait_send()
        for cp in local:
            cp.wait()

    any_spec = pl.BlockSpec(memory_space=pl.ANY)
    return pl.pallas_call(
        body, in_specs=[any_spec] * n, out_specs=[any_spec] * n, out_shape=dst_shapes,
        scratch_shapes=[pltpu.SemaphoreType.DMA((n, N_DEV - 1)), pltpu.SemaphoreType.DMA((n, N_DEV - 1)),
                        pltpu.SemaphoreType.DMA((n,))],
        name=name)(*srcs)


BIG = {
    "a_w_in": (1, (2, 1024, A_IN)),
    "b_w_in": (1, (2, 1024, 1024)),
    "w_kv_shared": (0, (1024, 256)),
    "mem_w_kv": (1, (4, 1024, 512)),
    "w_o": (1, (4, 1024, 1024)),
    "mlp_w_up": (2, (4, 1024, 4096)),
    "mlp_w_down": (1, (4, 4096, 1024)),
}
BIG_NAMES = tuple(BIG)


def _allgather_weights(shards):
    dst_shapes, axes, sizes = [], [], []
    for name, s in zip(BIG_NAMES, shards):
        axis, full = BIG[name]
        if name == "mlp_w_up":
            dst_shapes.append(_sds((N_DEV,) + s.shape, BF16))
            axes.append(0)
            sizes.append(None)
        else:
            dst_shapes.append(_sds(full, BF16))
            axes.append(axis)
            sizes.append(s.shape[axis])

    def src_view(t, ref, idx):
        return ref

    def dst_view(t, ref, idx):
        return _slot(ref, axes[t], idx, sizes[t])

    return _big_exchange(shards, dst_shapes, src_view, dst_view, "allgather_weights")


def _scatter_grads(grads):
    dst_shapes, axes, sizes = [], [], []
    for name, g in zip(BIG_NAMES, grads):
        axis, full = BIG[name]
        if name == "mlp_w_up":
            shard = (g.shape[0],) + g.shape[2:]
            axes.append(1)
            sizes.append(None)
        else:
            shard = tuple(d // N_DEV if a == axis else d for a, d in enumerate(full))
            axes.append(axis)
            sizes.append(shard[axis])
        dst_shapes.append(_sds((N_DEV,) + shard))

    def src_view(t, ref, idx):
        return _slot(ref, axes[t], idx, sizes[t])

    def dst_view(t, ref, idx):
        return ref.at[idx]

    return _big_exchange(grads, dst_shapes, src_view, dst_view, "scatter_grads")


def _pad_row(vec, width=128):
    return jnp.pad(vec.astype(F32), (0, width - vec.shape[0])).reshape(1, width)


def _block_sizes(t_total):
    return dict(row=min(256, t_total), dn=min(512, t_total), swa=min(256, t_total))


def _ln_apply(h, mix, g, b, tb):
    t_total, d = h.shape
    fwd = lambda pids, *a: _ln_fn(pids, *a) * 2
    return _block_fwd(fwd, [h, mix, g, b], [_rows(tb, d), _rows(tb, d), _whole((1, d)), _whole((1, d))],
                      [_sds((t_total, d)), _sds((t_total, d), BF16)], [_rows(tb, d), _rows(tb, d)],
                      (t_total // tb,), "ln_fwd")


def _ln_grad(h, mix, g, b, dy, tb):
    t_total, d = h.shape
    return _block_bwd(_ln_fn, [h, mix, g, b], [_rows(tb, d), _rows(tb, d), _whole((1, d)), _whole((1, d))],
                      [dy], [_rows(tb, d)], ["s", "s", "a", "a"],
                      [_sds((t_total, d)), _sds((t_total, d), BF16), _sds((1, d)), _sds((1, d))],
                      [_rows(tb, d), _rows(tb, d), _whole((1, d)), _whole((1, d))], (t_total // tb,), "ln_bwd")


def _memattn_specs(tb, qcol):
    return [pl.BlockSpec((tb, MEM_W), lambda i: (i, qcol)), _whole((MEM_W, 2 * MEM_W))]


def _act_epilogue(acc):
    r = jnp.maximum(acc, 0.0)
    return (r * r,)


def _dact_epilogue(acc, act):
    return (acc * (2.0 * jnp.sqrt(act.astype(F32))),)


def _add_epilogue(acc, other):
    return (acc + other,)


def _local_step(x, mem, positions, target, wts, small):
    t_total = x.shape[0]
    bs = _block_sizes(t_total)
    tb, tdn, tsw = bs["row"], bs["dn"], bs["swa"]
    nb = t_total // tb
    nbs = t_total // tsw

    inv_freq = ROPE_THETA ** (-jnp.arange(0, SWA_DH, 2, dtype=F32) / SWA_DH)
    ang = positions.astype(F32)[:, None] * inv_freq
    cos = jnp.tile(jnp.cos(ang), (1, 4))
    sin = jnp.tile(jnp.sin(ang), (1, 4))

    w_a = [jnp.concatenate([wts["a_w_in"][l][:, :3072], wts["a_w_in"][l][:, 3084:], wts["a_w_in"][l][:, 3072:3084],
                            jnp.zeros((D_MODEL, A_IN_PAD - A_IN), BF16)], axis=1) for l in range(N_A)]
    wkv = wts["w_kv_shared"]
    w_kvd = jnp.concatenate([wkv[:, 64 * (i // 2):64 * (i // 2 + 1)] for i in range(8)], axis=1)
    mem_b = mem.astype(BF16)
    up_spec_nn = lambda l: pl.BlockSpec((None, None, D_MODEL, 512), lambda i, j, k, l=l: (j, l, k, 0))
    up_spec_nt = lambda l: pl.BlockSpec((None, None, D_MODEL, 512), lambda i, j, k, l=l: (k, l, j, 0))

    saved = []
    h, hb = x, x.astype(BF16)
    kr = vd_src = None
    for l in range(DEPTH):
        sv = dict(h=h, hb=hb)
        kvm = _matmul(mem_b, wts["mem_w_kv"][l], "nn", [F32], "mm_memkv", tm=256)
        if l < N_A:
            proj = _matmul(hb, w_a[l], "nn", [F32], "mm_proj_a", tn=1152)
            conv_w = small["a_conv_w"][l]
            c = _conv_fwd(proj, conv_w, tb)
            alog, dtb = _pad_row(small["a_A_log"][l]), _pad_row(small["a_dt_bias"][l])
            rowa_in = [c, proj, alog, dtb]
            rowa_specs = [_rows(tb, 3 * DN_W), _rows(tb, 128, 26), _whole((1, 128)), _whole((1, 128))]
            q, k, v, gcb, betab = _block_fwd(_rowa_fn, rowa_in, rowa_specs, [_sds((t_total, DN_W))] * 5,
                                             [_rows(tb, DN_W)] * 5, (nb,), "rowa_fwd")
            hs = _head_spec(tdn)
            dn_grid = (DN_HEADS, t_total // tdn)
            full = _sds((t_total, DN_W))
            pair_sds = _sds((DN_HEADS, t_total, PAIR))
            dn1_out_shapes = [full, full, pair_sds, full, full, full, pair_sds]
            dn1_out_specs = [hs, hs, _intra_spec(tdn), hs, hs, hs, _intra_spec(tdn)]
            u, w, intra, qd, kd, cd, tinv = _block_fwd(_dn1_fn, [q, k, v, gcb, betab], [hs] * 5, dn1_out_shapes,
                                                       dn1_out_specs, dn_grid, "dn1_fwd")
            o, states = _dn2_fwd(qd, kd, u, w, intra, cd, tdn)
            nw = small["a_norm_w"][l].reshape(1, DN_D)
            post_in = [o, proj, nw]
            post_specs = [_rows(tb, DN_W), _rows(tb, DN_W, 3), _whole((1, DN_D))]
            (og,) = _block_fwd(_post_fn, post_in, post_specs, [_sds((t_total, DN_W), BF16)], [_rows(tb, DN_W)],
                               (nb,), "post_fwd")
            qm_col = 12
            sv.update(proj=proj, c=c, rowa_in=rowa_in, rowa_specs=rowa_specs, dn1_in=[q, k, v, gcb, betab, tinv],
                      dn2_in=[qd, kd, u, w, intra, cd], states=states, post_in=post_in, post_specs=post_specs,
                      conv_w=conv_w)
        else:
            jb = l - N_A
            proj = _matmul(hb, wts["b_w_in"][jb], "nn", [F32], "mm_proj_b")
            sinks = _pad_row(small["b_sinks"][jb])
            swa_in = [proj, cos, sin, kr, kr, vd_src, vd_src, sinks]
            swa_specs = [_rows(tsw, DN_W), _rows(tsw, 128), _rows(tsw, 128),
                         pl.BlockSpec((WINDOW, 256), lambda i: (jnp.maximum(i * (tsw // WINDOW) - 1, 0), 0)),
                         _rows(tsw, 256),
                         pl.BlockSpec((WINDOW, 256), lambda i: (jnp.maximum(i * (tsw // WINDOW) - 1, 0), 1)),
                         _rows(tsw, 256, 1), _whole((1, 128))]
            (og,) = _block_fwd(_swa_fn, swa_in, swa_specs, [_sds((t_total, DN_W), BF16)], [_rows(tsw, DN_W)],
                               (nbs,), "swa_fwd")
            qm_col = 3
            sv.update(proj=proj, swa_in=swa_in, swa_specs=swa_specs)
        mem_in = [proj, kvm]
        (mo,) = _block_fwd(_memattn_fn, mem_in, _memattn_specs(tb, qm_col), [_sds((t_total, MEM_W), BF16)],
                           [_rows(tb, MEM_W)], (nb,), "memattn_fwd")
        mixin = jnp.concatenate([og, mo], axis=1)
        mix = _matmul(mixin, wts["w_o"][l], "nn", [F32], "mm_wo")
        g0, b0 = small["ln_g"][l, 0].reshape(1, -1), small["ln_b"][l, 0].reshape(1, -1)
        h1, h1b = _ln_apply(h, mix, g0, b0, tb)
        act = _matmul(h1b, wts["mlp_w_up"], "nn", [BF16], "mm_up", epi=_act_epilogue,
                      b_spec=up_spec_nn(l), n_total=4 * D_MODEL, tn=512)
        mlp = _matmul(act, wts["mlp_w_down"][l], "nn", [F32], "mm_down")
        g1, b1 = small["ln_g"][l, 1].reshape(1, -1), small["ln_b"][l, 1].reshape(1, -1)
        h2, h2b = _ln_apply(h1, mlp, g1, b1, tb)
        sv.update(kvm=kvm, mem_in=mem_in, qm_col=qm_col, mixin=mixin, mix=mix, ln0=(g0, b0), h1=h1, h1b=h1b,
                  act=act, mlp=mlp, ln1=(g1, b1))
        saved.append(sv)
        h, hb = h2, h2b
        if l == N_A - 1:
            kvd = _matmul(hb, w_kvd, "nn", [F32], "mm_kvd")
            krope_in = [kvd, cos, sin]
            krope_specs = [_rows(tb, 256), _rows(tb, 128), _rows(tb, 128)]
            (kr,) = _block_fwd(_krope_fn, krope_in, krope_specs, [_sds((t_total, 256))], [_rows(tb, 256)], (nb,),
                               "krope_fwd")
            vd_src = kvd

    loss, dh = _loss_and_grad(h, target, tb)

    grads = {n: [None] * BIG[n][1][0] for n in BIG_NAMES if n != "w_kv_shared"}
    sg = dict(a_conv_w=[None] * N_A, a_A_log=[None] * N_A, a_dt_bias=[None] * N_A, a_norm_w=[None] * N_A,
              b_sinks=[None] * (DEPTH - N_A), ln_g=[[None, None] for _ in range(DEPTH)],
              ln_b=[[None, None] for _ in range(DEPTH)])
    dk_parts, dv_parts = [], []
    for l in reversed(range(DEPTH)):
        sv = saved[l]
        if l == N_A - 1:
            dkr = _halo_sum([p[0] for p in dk_parts], [p[1] for p in dk_parts], tsw)
            dvv = _halo_sum([p[0] for p in dv_parts], [p[1] for p in dv_parts], tsw)
            (dkraw,) = _block_bwd(_krope_fn, krope_in, krope_specs, [dkr], [_rows(tb, 256)], ["s", None, None],
                                  [_sds((t_total, 256), BF16)], [_rows(tb, 256)], (nb,), "krope_bwd")
            dkvd = jnp.concatenate([dkraw, dvv.astype(BF16)], axis=1)
            g_kvd = _matmul(saved[l + 1]["hb"], dkvd, "tn", [F32], "mm_dw_kvd", tm=1024, tn=512)
            dh = _matmul(dkvd, w_kvd, "nt", [F32], "mm_dx_kvd", epi=_add_epilogue, extras=[dh], tn=1024, tk=512)
            grads["w_kv_shared"] = jnp.concatenate(
                [g_kvd[:, 128 * i:128 * i + 64] + g_kvd[:, 128 * i + 64:128 * (i + 1)] for i in range(4)], axis=1)
        g1, b1 = sv["ln1"]
        dh1a, dmlp, dg1, db1 = _ln_grad(sv["h1"], sv["mlp"], g1, b1, dh, tb)
        dup = _matmul(dmlp, wts["mlp_w_down"][l], "nt", [BF16], "mm_dact", epi=_dact_epilogue, extras=[sv["act"]])
        grads["mlp_w_down"][l] = _matmul(sv["act"], dmlp, "tn", [F32], "mm_dw_down")
        g_up = _matmul(sv["h1b"], dup, "tn", [F32], "mm_dw_up", tn=512, tk=2048,
                       out_shapes=[_sds((N_DEV, D_MODEL, 512))],
                       out_specs=[pl.BlockSpec((None, 1024, 512), lambda i, j, k: (j, i, 0))])
        grads["mlp_w_up"][l] = g_up
        dh1 = _matmul(dup, wts["mlp_w_up"], "nt", [F32], "mm_dx_up", epi=_add_epilogue, extras=[dh1a],
                      b_spec=up_spec_nt(l), n_total=D_MODEL, tn=1024, tk=512)
        g0, b0 = sv["ln0"]
        dha, dmix, dg0, db0 = _ln_grad(sv["h"], sv["mix"], g0, b0, dh1, tb)
        sg["ln_g"][l] = [dg0, dg1]
        sg["ln_b"][l] = [db0, db1]
        grads["w_o"][l] = _matmul(sv["mixin"], dmix, "tn", [F32], "mm_dw_o")
        dmixin = _matmul(dmix, wts["w_o"][l], "nt", [F32], "mm_dx_o", tn=1024)
        dqm, dkvm = _block_bwd(_memattn_fn, sv["mem_in"], _memattn_specs(tb, sv["qm_col"]), [dmixin],
                               [_rows(tb, MEM_W, 3)], ["s", "a"],
                               [_sds((t_total, MEM_W), BF16), _sds((MEM_W, 2 * MEM_W))],
                               [_rows(tb, MEM_W), _whole((MEM_W, 2 * MEM_W))], (nb,), "memattn_bwd")
        grads["mem_w_kv"][l] = _matmul(mem_b, dkvm.astype(BF16), "tn", [F32], "mm_dw_memkv", tm=1024, tn=512)
        if l < N_A:
            d_o, dz, dnw = _block_bwd(_post_fn, sv["post_in"], sv["post_specs"], [dmixin], [_rows(tb, DN_W)],
                                      ["s", "s", "a"],
                                      [_sds((t_total, DN_W)), _sds((t_total, DN_W), BF16), _sds((1, DN_D))],
                                      [_rows(tb, DN_W), _rows(tb, DN_W), _whole((1, DN_D))], (nb,), "post_bwd")
            sg["a_norm_w"][l] = dnw
            dqd, dkd, du, dw, da, dcd = _dn2_bwd(*sv["dn2_in"], sv["states"], d_o, tdn)
            hs = _head_spec(tdn)
            full = _sds((t_total, DN_W))
            dq, dk, dv, dgc, dbeta = _block_bwd(
                _dn1_fn_known, sv["dn1_in"], [hs] * 5 + [_intra_spec(tdn)], [du, dw, da, dqd, dkd, dcd],
                [hs, hs, _intra_spec(tdn), hs, hs, hs], ["s"] * 5 + [None], [full] * 5, [hs] * 5,
                (DN_HEADS, t_total // tdn), "dn1_bwd")
            dc, dab, dalog, ddtb = _block_bwd(
                _rowa_fn, sv["rowa_in"], sv["rowa_specs"], [dq, dk, dv, dgc, dbeta], [_rows(tb, DN_W)] * 5,
                ["s", "s", "a", "a"],
                [_sds((t_total, 3 * DN_W)), _sds((t_total, 128), BF16), _sds((1, 128)), _sds((1, 128))],
                [_rows(tb, 3 * DN_W), _rows(tb, 128), _whole((1, 128)), _whole((1, 128))], (nb,), "rowa_bwd")
            sg["a_A_log"][l] = dalog[0, :DN_HEADS]
            sg["a_dt_bias"][l] = ddtb[0, :DN_HEADS]
            dx, dconv = _conv_bwd(dc, sv["proj"], sv["conv_w"], tb)
            sg["a_conv_w"][l] = dconv
            dproj = jnp.concatenate([dx, dz, dqm, dab], axis=1)
            g_in = _matmul(sv["hb"], dproj, "tn", [F32], "mm_dw_a", tm=1024, tn=1152)
            grads["a_w_in"][l] = jnp.concatenate([g_in[:, :3072], g_in[:, 3328:3340], g_in[:, 3072:3328]], axis=1)
            dh = _matmul(dproj, w_a[l], "nt", [F32], "mm_dx_a", epi=_add_epilogue, extras=[dha], tk=1152)
        else:
            jb = l - N_A
            swa_kinds = ["s", None, None, "s", "s", "s", "s", "a"]
            halo_spec = pl.BlockSpec((None, WINDOW, 256), lambda i: (i, 0, 0))
            dq, dkh, dkc, dvh, dvc, dsink = _block_bwd(
                _swa_fn, sv["swa_in"], sv["swa_specs"], [dmixin], [_rows(tsw, DN_W)], swa_kinds,
                [_sds((t_total, DN_W), BF16), _sds((nbs, WINDOW, 256)), _sds((t_total, 256)),
                 _sds((nbs, WINDOW, 256)), _sds((t_total, 256)), _sds((1, 128))],
                [_rows(tsw, DN_W), halo_spec, _rows(tsw, 256), halo_spec, _rows(tsw, 256), _whole((1, 128))],
                (nbs,), "swa_bwd")
            sg["b_sinks"][jb] = dsink[0, :SWA_HEADS]
            dk_parts.append((dkc, dkh))
            dv_parts.append((dvc, dvh))
            dproj = jnp.concatenate([dq, dqm], axis=1)
            grads["b_w_in"][jb] = _matmul(sv["hb"], dproj, "tn", [F32], "mm_dw_b")
            dh = _matmul(dproj, wts["b_w_in"][jb], "nt", [F32], "mm_dx_b", epi=_add_epilogue, extras=[dha], tn=1024)

    big = []
    for n in BIG_NAMES:
        if n == "w_kv_shared":
            big.append(grads[n])
        else:
            big.append(jnp.stack(grads[n], axis=0))
    small_grads = dict(
        a_conv_w=jnp.stack(sg["a_conv_w"]), a_A_log=jnp.stack(sg["a_A_log"]), a_dt_bias=jnp.stack(sg["a_dt_bias"]),
        a_norm_w=jnp.concatenate(sg["a_norm_w"], axis=0), b_sinks=jnp.stack(sg["b_sinks"]),
        ln_g=jnp.stack([jnp.concatenate(p, axis=0) for p in sg["ln_g"]]),
        ln_b=jnp.stack([jnp.concatenate(p, axis=0) for p in sg["ln_b"]]))
    return loss, dh, big, small_grads


def _pack(arrays, rows):
    flat = []
    for a in arrays:
        v = a.astype(F32).reshape(-1)
        flat.append(jnp.pad(v, (0, (-v.shape[0]) % 128)))
    flat = jnp.concatenate(flat)
    return jnp.pad(flat, (0, rows * 128 - flat.shape[0])).reshape(rows, 128)


def _unpack(slab, shapes):
    flat = slab.reshape(slab.shape[:-2] + (-1,))
    out, off = [], 0
    for s in shapes:
        n = math.prod(s)
        out.append(flat[..., off:off + n].reshape(slab.shape[:-2] + tuple(s)))
        off += n + (-n) % 128
    return out


def _rows_for(shapes):
    rows = sum((math.prod(s) + 127) // 128 for s in shapes)
    return rows + (-rows) % 8


SMALL_NAMES = ("a_conv_w", "a_A_log", "a_dt_bias", "a_norm_w", "b_sinks", "ln_g", "ln_b")
SMALL_SHARDED = {"a_conv_w": 2, "ln_g": 2, "ln_b": 2}
SMALL_FULL = {"a_conv_w": (2, 4, 2304), "a_A_log": (2, 6), "a_dt_bias": (2, 6), "a_norm_w": (2, 128),
              "b_sinks": (2, 12), "ln_g": (4, 2, 1024), "ln_b": (4, 2, 1024)}


def kernel(x, mem, positions, a_w_in, a_conv_w, a_A_log, a_dt_bias, a_norm_w, b_w_in, b_sinks, w_kv_shared, mem_w_kv, w_o, mlp_w_up, mlp_w_down, ln_g, ln_b, loss_target, m_a_w_in, m_a_conv_w, m_a_A_log, m_a_dt_bias, m_a_norm_w, m_b_w_in, m_b_sinks, m_w_kv_shared, m_mem_w_kv, m_w_o, m_mlp_w_up, m_mlp_w_down, m_ln_g, m_ln_b, v_a_w_in, v_a_conv_w, v_a_A_log, v_a_dt_bias, v_a_norm_w, v_b_w_in, v_b_sinks, v_w_kv_shared, v_mem_w_kv, v_w_o, v_mlp_w_up, v_mlp_w_down, v_ln_g, v_ln_b):
    params = dict(a_w_in=a_w_in, a_conv_w=a_conv_w, a_A_log=a_A_log, a_dt_bias=a_dt_bias, a_norm_w=a_norm_w,
                  b_w_in=b_w_in, b_sinks=b_sinks, w_kv_shared=w_kv_shared, mem_w_kv=mem_w_kv, w_o=w_o,
                  mlp_w_up=mlp_w_up, mlp_w_down=mlp_w_down, ln_g=ln_g, ln_b=ln_b)
    mom = dict(a_w_in=m_a_w_in, a_conv_w=m_a_conv_w, a_A_log=m_a_A_log, a_dt_bias=m_a_dt_bias, a_norm_w=m_a_norm_w,
               b_w_in=m_b_w_in, b_sinks=m_b_sinks, w_kv_shared=m_w_kv_shared, mem_w_kv=m_mem_w_kv, w_o=m_w_o,
               mlp_w_up=m_mlp_w_up, mlp_w_down=m_mlp_w_down, ln_g=m_ln_g, ln_b=m_ln_b)
    var = dict(a_w_in=v_a_w_in, a_conv_w=v_a_conv_w, a_A_log=v_a_A_log, a_dt_bias=v_a_dt_bias, a_norm_w=v_a_norm_w,
               b_w_in=v_b_w_in, b_sinks=v_b_sinks, w_kv_shared=v_w_kv_shared, mem_w_kv=v_mem_w_kv, w_o=v_w_o,
               mlp_w_up=v_mlp_w_up, mlp_w_down=v_mlp_w_down, ln_g=v_ln_g, ln_b=v_ln_b)
    me = 4 * lax.axis_index("x") + 2 * lax.axis_index("y") + lax.axis_index("c")

    full_b = _allgather_weights([params[n].astype(BF16) for n in BIG_NAMES])
    wts = dict(zip(BIG_NAMES, full_b))
    sharded_names = [n for n in SMALL_NAMES if n in SMALL_SHARDED]
    shard_shapes = [params[n].shape for n in sharded_names]
    gathered = _small_exchange(_pack([params[n] for n in sharded_names], _rows_for(shard_shapes)), reduce=False)
    small = {n: params[n] for n in SMALL_NAMES if n not in SMALL_SHARDED}
    for n, g in zip(sharded_names, _unpack(gathered, shard_shapes)):
        small[n] = jnp.moveaxis(g, 0, 2).reshape(SMALL_FULL[n])

    loss, dx, big_grads, small_grads = _local_step(x[0], mem[0], positions[0], loss_target[0], wts, small)
    loss = lax.psum(loss, ("x", "y", "c"))
    recv = _scatter_grads(big_grads)
    out = {}
    for n, r in zip(BIG_NAMES, recv):
        shp = params[n].shape
        rows = math.prod(shp[:-1])
        res = _adamw(r.reshape(N_DEV, rows, shp[-1]), params[n].reshape(rows, shp[-1]),
                     mom[n].reshape(rows, shp[-1]), var[n].reshape(rows, shp[-1]), 32, "adamw_" + n)
        out[n] = [t.reshape(shp) for t in res]
    full_shapes = [SMALL_FULL[n] for n in SMALL_NAMES]
    summed = _small_exchange(_pack([small_grads[n] for n in SMALL_NAMES], _rows_for(full_shapes)), reduce=True)
    local_g = []
    for n, g in zip(SMALL_NAMES, _unpack(summed, full_shapes)):
        if n in SMALL_SHARDED:
            size = params[n].shape[2]
            g = lax.dynamic_slice_in_dim(g, me * size, size, axis=2)
        local_g.append(g)
    local_shapes = [params[n].shape for n in SMALL_NAMES]
    rows = _rows_for(local_shapes)
    res = _adamw(_pack(local_g, rows)[None], _pack([params[n] for n in SMALL_NAMES], rows),
                 _pack([mom[n] for n in SMALL_NAMES], rows), _pack([var[n] for n in SMALL_NAMES], rows), rows,
                 "adamw_small")
    unpacked = [_unpack(t, local_shapes) for t in res]
    for i, n in enumerate(SMALL_NAMES):
        out[n] = [unpacked[k][i] for k in range(4)]

    order = ("a_w_in", "a_conv_w", "a_A_log", "a_dt_bias", "a_norm_w", "b_w_in", "b_sinks", "w_kv_shared",
             "mem_w_kv", "w_o", "mlp_w_up", "mlp_w_down", "ln_g", "ln_b")
    return (loss, dx[None], *[out[n][0] for n in order], *[out[n][1] for n in order],
            *[out[n][2] for n in order], *[out[n][3] for n in order])
```

```python
import functools
import math

import jax
import jax.numpy as jnp
from jax import lax
from jax.experimental import pallas as pl
from jax.experimental.pallas import tpu as pltpu

F32 = jnp.float32
BF16 = jnp.bfloat16

D_MODEL = 1024
DEPTH = 4
N_A = 2
MEM_HEADS = 4
MEM_DH = 64
MEM_W = 256
DN_HEADS = 6
DN_D = 128
DN_W = 768
CHUNK = 64
SWA_DH = 64
SWA_HEADS = 12
WINDOW = 128
ROPE_THETA = 10000.0
LN_EPS = 1e-5
NORM_EPS = 1e-6
DN_ALPHA = (2.0 * DEPTH) ** 0.25
A_IN = 3340
A_IN_PAD = 3456
N_DEV = 8

ADAM_LR = 0.001
ADAM_B1 = 0.9
ADAM_B2 = 0.999
ADAM_EPS = 1e-08
ADAM_WD = 0.01
ADAM_STEP = 10

VMEM_LIMIT = 48 * 1024 * 1024
NEG_BIG = -1e30


def _cparams(sem):
    return pltpu.CompilerParams(dimension_semantics=sem, vmem_limit_bytes=VMEM_LIMIT)


_CONTRACT = {"nn": (1, 0), "nt": (1, 1), "tn": (0, 0)}


def _raw_mm(a, b, mode, prec):
    ca, cb = _CONTRACT[mode]
    dims = (((ca,), (cb,)), ((), ()))
    dot = lambda p, q: lax.dot_general(p, q, dims, preferred_element_type=F32)
    if prec == "bf16":
        return dot(a.astype(BF16), b.astype(BF16))
    a, b = a.astype(F32), b.astype(F32)
    a_hi, b_hi = a.astype(BF16), b.astype(BF16)
    if prec == "sela":
        return dot(a_hi, b_hi) + dot(a_hi, (b - b_hi.astype(F32)).astype(BF16))
    a_lo = (a - a_hi.astype(F32)).astype(BF16)
    if prec == "selb":
        return dot(a_hi, b_hi) + dot(a_lo, b_hi)
    b_lo = (b - b_hi.astype(F32)).astype(BF16)
    return dot(a_hi, b_hi) + (dot(a_hi, b_lo) + dot(a_lo, b_hi))


@functools.partial(jax.custom_vjp, nondiff_argnums=(2, 3))
def mm(a, b, mode, prec):
    return _raw_mm(a, b, mode, prec)


def _mm_fwd(a, b, mode, prec):
    return _raw_mm(a, b, mode, prec), (a, b)


def _mm_bwd(mode, prec, res, ct):
    a, b = res
    if prec == "sela":
        pa, pb = "f32", {"nn": "sela", "nt": "selb", "tn": "sela"}[mode]
    elif prec == "selb":
        pa, pb = {"nn": "selb", "nt": "selb", "tn": "sela"}[mode], "f32"
    else:
        pa = pb = prec
    if mode == "nn":
        return mm(ct, b, "nt", pa), mm(a, ct, "tn", pb)
    if mode == "nt":
        return mm(ct, b, "nn", pa), mm(ct, a, "tn", pb)
    return mm(b, ct, "nt", pa), mm(a, ct, "nn", pb)


mm.defvjp(_mm_fwd, _mm_bwd)


@jax.custom_vjp
def _softplus(x):
    y = jnp.exp(-jnp.abs(x))
    log1p_y = jnp.where(y < 1e-2, y * (1.0 - y * (0.5 - y * (1.0 / 3.0))), jnp.log(1.0 + y))
    return jnp.maximum(x, 0.0) + log1p_y


def _softplus_fwd(x):
    return _softplus(x), x


def _softplus_bwd(x, ct):
    return (ct * jax.nn.sigmoid(x),)


_softplus.defvjp(_softplus_fwd, _softplus_bwd)


def _iota(shape, dim):
    return lax.broadcasted_iota(jnp.int32, shape, dim)


def _block_fwd(fn, ins, in_specs, out_shapes, out_specs, grid, name):
    n_in = len(ins)

    def body(*refs):
        pids = tuple(pl.program_id(a) for a in range(len(grid)))
        vals = [r[...].astype(F32) for r in refs[:n_in]]
        outs = fn(pids, *vals)
        for r, o in zip(refs[n_in:], outs):
            r[...] = o.astype(r.dtype)

    return pl.pallas_call(
        body, grid=grid, in_specs=in_specs, out_specs=out_specs, out_shape=out_shapes, name=name,
        compiler_params=_cparams(("parallel",) * len(grid)))(*ins)


def _block_bwd(fn, ins, in_specs, cts, ct_specs, kinds, g_shapes, g_specs, grid, name):
    n_in, n_ct = len(ins), len(cts)
    didx = [i for i, k in enumerate(kinds) if k]

    def body(*refs):
        in_refs, ct_refs, g_refs = refs[:n_in], refs[n_in:n_in + n_ct], refs[n_in + n_ct:]
        pids = tuple(pl.program_id(a) for a in range(len(grid)))
        vals = [r[...].astype(F32) for r in in_refs]

        def f(*dvals):
            full = list(vals)
            for i, v in zip(didx, dvals):
                full[i] = v
            return tuple(fn(pids, *full))

        _, vjp = jax.vjp(f, *[vals[i] for i in didx])
        gs = vjp(tuple(r[...].astype(F32) for r in ct_refs))
        first = pids[0] == 0
        for p in pids[1:]:
            first = jnp.logical_and(first, p == 0)
        for i, g, r in zip(didx, gs, g_refs):
            if kinds[i] == "s":
                r[...] = g.astype(r.dtype)
            else:
                @pl.when(first)
                def _(r=r):
                    r[...] = jnp.zeros(r.shape, r.dtype)

                r[...] += g.astype(r.dtype)

    sem = ("arbitrary",) * len(grid) if "a" in kinds else ("parallel",) * len(grid)
    return pl.pallas_call(
        body, grid=grid, in_specs=list(in_specs) + list(ct_specs), out_specs=g_specs, out_shape=g_shapes,
        name=name, compiler_params=_cparams(sem))(*ins, *cts)


def _rows(tb, width, col=0):
    return pl.BlockSpec((tb, width), lambda i, col=col: (i, col))


def _whole(shape):
    return pl.BlockSpec(shape, lambda *_: (0,) * len(shape))


def _sds(shape, dtype=F32):
    return jax.ShapeDtypeStruct(shape, dtype)


def _matmul(a, b, mode, out_dtypes, name, epi=None, extras=(), tm=1024, tn=1024, tk=1024,
            b_spec=None, n_total=None, out_specs=None, out_shapes=None):
    if mode == "tn":
        k_total, m_total = a.shape
    else:
        m_total, k_total = a.shape
    if n_total is None:
        n_total = b.shape[0] if mode == "nt" else b.shape[1]
    tm, tn, tk = min(tm, m_total), min(tn, n_total), min(tk, k_total)
    assert m_total % tm == 0 and n_total % tn == 0 and k_total % tk == 0, (name, a.shape, b.shape)
    grid = (m_total // tm, n_total // tn, k_total // tk)
    nk = grid[2]
    if mode == "tn":
        a_spec = pl.BlockSpec((tk, tm), lambda i, j, k: (k, i))
    else:
        a_spec = pl.BlockSpec((tm, tk), lambda i, j, k: (i, k))
    if b_spec is None:
        if mode == "nt":
            b_spec = pl.BlockSpec((tn, tk), lambda i, j, k: (j, k))
        else:
            b_spec = pl.BlockSpec((tk, tn), lambda i, j, k: (k, j))
    tile = pl.BlockSpec((tm, tn), lambda i, j, k: (i, j))
    n_ex, n_out = len(extras), len(out_dtypes)
    ca, cb = _CONTRACT[mode]
    dims = (((ca,), (cb,)), ((), ()))

    def body(*refs):
        a_ref, b_ref = refs[:2]
        ex_refs = refs[2:2 + n_ex]
        out_refs = refs[2 + n_ex:2 + n_ex + n_out]
        part = lax.dot_general(a_ref[...], b_ref[...], dims, preferred_element_type=F32)

        def finish(val):
            res = epi(val, *[e[...] for e in ex_refs]) if epi is not None else (val,)
            for r, o in zip(out_refs, res):
                r[...] = o.astype(r.dtype)

        if nk == 1:
            finish(part)
        else:
            acc = refs[-1]
            k = pl.program_id(2)

            @pl.when(k == 0)
            def _():
                acc[...] = part

            @pl.when(k > 0)
            def _():
                acc[...] += part

            @pl.when(k == nk - 1)
            def _():
                finish(acc[...])

    if out_shapes is None:
        out_shapes = [_sds((m_total, n_total), d) for d in out_dtypes]
        out_specs = [tile] * n_out
    outs = pl.pallas_call(
        body, grid=grid, in_specs=[a_spec, b_spec] + [tile] * n_ex, out_specs=out_specs, out_shape=out_shapes,
        scratch_shapes=[pltpu.VMEM((tm, tn), F32)] if nk > 1 else [], name=name,
        compiler_params=_cparams(("parallel", "parallel", "arbitrary")))(a, b, *extras)
    return outs if n_out > 1 else outs[0]


def _silu(x):
    return x * jax.nn.sigmoid(x)


def _rowa_fn(pids, c, ab, alog, dtb):
    tb = c.shape[0]
    s = _silu(c)
    qs, ks = [], []
    for h in range(DN_HEADS):
        qh = s[:, DN_D * h:DN_D * (h + 1)]
        qs.append(qh * lax.rsqrt(jnp.sum(qh * qh, axis=-1, keepdims=True) + NORM_EPS) * (DN_D ** -0.5))
        kh = s[:, DN_W + DN_D * h:DN_W + DN_D * (h + 1)]
        ks.append(kh * lax.rsqrt(jnp.sum(kh * kh, axis=-1, keepdims=True) + NORM_EPS))
    q = jnp.concatenate(qs, axis=1)
    k = jnp.concatenate(ks, axis=1)
    v = s[:, 2 * DN_W:3 * DN_W]
    g128 = -jnp.exp(alog) * _softplus(ab + dtb)
    b128 = jax.nn.sigmoid(ab)
    r, cc = _iota((tb, tb), 0), _iota((tb, tb), 1)
    tri = jnp.where(((r >> 6) == (cc >> 6)) & (r >= cc), 1.0, 0.0)
    gc128 = mm(tri, g128, "nn", "sela")
    lane, col = _iota((128, DN_W), 0), _iota((128, DN_W), 1)
    exp_a = jnp.where(lane == (col >> 7), 1.0, 0.0)
    exp_b = jnp.where(lane == (col >> 7) + DN_HEADS, 1.0, 0.0)
    return q, k, v, mm(gc128, exp_a, "nn", "selb"), mm(b128, exp_b, "nn", "selb")


def _tri_inv_raw(low, block):
    n = low.shape[0]
    r, c = _iota((n, n), 0), _iota((n, n), 1)
    lg = 0
    x = None
    while (1 << lg) < block:
        off = ((r >> (lg + 1)) == (c >> (lg + 1))) & (((r >> lg) & 1) == 1) & (((c >> lg) & 1) == 0)
        cblk = jnp.where(off, low, 0.0)
        if x is None:
            x = jnp.where(r == c, 1.0, 0.0) - cblk
        else:
            x = x - mm(x, mm(cblk, x, "nn", "f32"), "nn", "f32")
        lg += 1
    return x


def _tri_inv_cotangent(block, x, ct):
    n = x.shape[0]
    r, c = _iota((n, n), 0), _iota((n, n), 1)
    shift = block.bit_length() - 1
    g = mm(mm(x, ct, "tn", "f32"), x, "nt", "f32")
    return jnp.where(((r >> shift) == (c >> shift)) & (r > c), -g, 0.0)


@functools.partial(jax.custom_vjp, nondiff_argnums=(1,))
def _tri_inv(low, block):
    return _tri_inv_raw(low, block)


def _tri_inv_fwd(low, block):
    x = _tri_inv_raw(low, block)
    return x, x


def _tri_inv_bwd(block, x, ct):
    return (_tri_inv_cotangent(block, x, ct),)


_tri_inv.defvjp(_tri_inv_fwd, _tri_inv_bwd)


@functools.partial(jax.custom_vjp, nondiff_argnums=(2,))
def _tri_inv_known(low, known, block):
    return known


def _tri_inv_known_fwd(low, known, block):
    return known, known


def _tri_inv_known_bwd(block, x, ct):
    return _tri_inv_cotangent(block, x, ct), jnp.zeros_like(x)


_tri_inv_known.defvjp(_tri_inv_known_fwd, _tri_inv_known_bwd)


PAIR = 2 * CHUNK


def _dn1_pair(q, k, v, gc, beta, tinv_known=None):
    n = q.shape[0]
    assert n == PAIR == DN_D
    eg = jnp.exp(gc)
    kb = k * beta
    onehot = jnp.where(_iota((n, DN_D), 1) == 0, 1.0, 0.0)
    g_row = mm(onehot, gc, "nt", "sela")
    r, c = _iota((n, n), 0), _iota((n, n), 1)
    same = (r >> 6) == (c >> 6)
    incl, strict = same & (r >= c), same & (r > c)
    decay = jnp.exp(jnp.where(incl, gc - g_row, NEG_BIG))
    low = jnp.where(strict, mm(kb, k, "nt", "bf16") * decay, 0.0)
    tinv = _tri_inv(low, CHUNK) if tinv_known is None else _tri_inv_known(low, tinv_known, CHUNK)
    uw = mm(tinv, jnp.concatenate([v * beta, kb * eg], axis=1), "nn", "f32")
    u, w = uw[:, :DN_D], uw[:, DN_D:]
    intra = jnp.where(incl, mm(q, k, "nt", "bf16") * decay, 0.0)
    row = _iota((n, DN_D), 0)
    last0 = jnp.sum(jnp.where(row == CHUNK - 1, gc, 0.0), axis=0, keepdims=True)
    last1 = jnp.sum(jnp.where(row == PAIR - 1, gc, 0.0), axis=0, keepdims=True)
    g_last = jnp.where(row < CHUNK, last0, last1)
    qd = q * eg
    kd = k * jnp.exp(g_last - gc)
    cd = jnp.exp(g_last)
    return u, w, intra, qd, kd, cd, tinv


def _dn1_fn(pids, q, k, v, gc, beta):
    outs = []
    for j in range(q.shape[0] // PAIR):
        sl = slice(PAIR * j, PAIR * (j + 1))
        outs.append(_dn1_pair(q[sl], k[sl], v[sl], gc[sl], beta[sl]))
    return tuple(jnp.concatenate([o[t] for o in outs], axis=0) for t in range(7))


def _dn1_fn_known(pids, q, k, v, gc, beta, tinv):
    outs = []
    for j in range(q.shape[0] // PAIR):
        sl = slice(PAIR * j, PAIR * (j + 1))
        outs.append(_dn1_pair(q[sl], k[sl], v[sl], gc[sl], beta[sl], tinv[sl])[:6])
    return tuple(jnp.concatenate([o[t] for o in outs], axis=0) for t in range(6))


def _dn2_step(half, state, qd, kd, u, w, intra, cd_row):
    v_new = u - mm(w, state, "nn", "bf16")
    zeros = jnp.zeros_like(v_new)
    v_pair = jnp.concatenate([v_new, zeros] if half == 0 else [zeros, v_new], axis=0)
    out = mm(qd, state, "nn", "bf16") + mm(intra, v_pair, "nn", "bf16")
    return out, state * cd_row + mm(kd, v_new, "tn", "bf16")


def _post_fn(pids, o, z, nw):
    outs = []
    for h in range(DN_HEADS):
        oh = o[:, DN_D * h:DN_D * (h + 1)]
        zh = z[:, DN_D * h:DN_D * (h + 1)]
        y = oh * lax.rsqrt(jnp.mean(oh * oh, axis=-1, keepdims=True) + NORM_EPS) * nw
        outs.append(y * _silu(zh))
    return (jnp.concatenate(outs, axis=1),)


def _memattn_fn(pids, qm, kvm):
    kmem, vmem = kvm[:, :MEM_W], kvm[:, MEM_W:]
    lane = _iota((1, MEM_W), 1)
    out = None
    for h in range(MEM_HEADS):
        hm = jnp.where((lane >> 6) == h, 1.0, 0.0)
        s = mm(qm * hm, kmem, "nt", "bf16") * (MEM_DH ** -0.5)
        m = lax.stop_gradient(jnp.max(s, axis=-1, keepdims=True))
        e = jnp.exp(s - m)
        p = e / jnp.sum(e, axis=-1, keepdims=True)
        oh = mm(p, vmem, "nn", "bf16") * hm
        out = oh if out is None else out + oh
    return (out,)


def _ln_fn(pids, h, mix, g, b):
    x = DN_ALPHA * h + mix
    mu = jnp.mean(x, axis=-1, keepdims=True)
    xc = x - mu
    var = jnp.mean(xc * xc, axis=-1, keepdims=True)
    return (xc * lax.rsqrt(var + LN_EPS) * g + b,)


def _rope_matrix():
    i, j = _iota((128, 128), 0), _iota((128, 128), 1)
    jj = j & 63
    return jnp.where((jj < 32) & (i == j + 32), -1.0, 0.0) + jnp.where((jj >= 32) & (i == j - 32), 1.0, 0.0)


def _rope128(x, cos, sin, rot):
    return x * cos + mm(x, rot, "nn", "selb") * sin


def _krope_fn(pids, kraw, cos, sin):
    rot = _rope_matrix()
    return (jnp.concatenate([_rope128(kraw[:, 128 * g:128 * (g + 1)], cos, sin, rot) for g in range(2)], axis=1),)


def _swa_fn(pids, qraw, cos, sin, k_halo, k_cur, v_halo, v_cur, sinks):
    tb = qraw.shape[0]
    nwin = tb // WINDOW
    rot = _rope_matrix()
    kcat = jnp.concatenate([k_halo, k_cur], axis=0)
    vcat = jnp.concatenate([v_halo, v_cur], axis=0)
    lane = _iota((1, 128), 1)
    halves = (jnp.where(lane < 64, 1.0, 0.0), jnp.where(lane >= 64, 1.0, 0.0))
    qi, kj = _iota((WINDOW, 2 * WINDOW), 0), _iota((WINDOW, 2 * WINDOW), 1)
    diff = qi + WINDOW - kj
    band = (diff >= 0) & (diff < WINDOW)
    acc = [[None] * (SWA_HEADS // 2) for _ in range(nwin)]
    for p in range(SWA_HEADS // 2):
        qg = _rope128(qraw[:, 128 * p:128 * (p + 1)], cos, sin, rot)
        kv = p // 3
        for hh in range(2):
            head = 2 * p + hh
            sink = jnp.sum(jnp.where(lane == head, sinks, 0.0), axis=-1, keepdims=True)
            qh = qg * halves[hh]
            for w in range(nwin):
                keys = kcat[WINDOW * w:WINDOW * (w + 2), 128 * kv:128 * (kv + 1)]
                vals = vcat[WINDOW * w:WINDOW * (w + 2), 128 * kv:128 * (kv + 1)]
                s = mm(qh[WINDOW * w:WINDOW * (w + 1)], keys, "nt", "bf16") * (SWA_DH ** -0.5)
                valid = band & ((pids[0] * tb + WINDOW * (w - 1) + kj) >= 0)
                s = jnp.where(valid, s, NEG_BIG)
                m = lax.stop_gradient(jnp.maximum(jnp.max(s, axis=-1, keepdims=True), sink))
                e = jnp.exp(s - m)
                denom = jnp.sum(e, axis=-1, keepdims=True) + jnp.exp(sink - m)
                o = mm(e / denom, vals, "nn", "bf16") * halves[hh]
                acc[w][p] = o if hh == 0 else acc[w][p] + o
    return (jnp.concatenate([jnp.concatenate(acc[w], axis=1) for w in range(nwin)], axis=0),)


def _conv_fwd(proj, conv_w, tb):
    t_total = proj.shape[0]
    width = conv_w.shape[1]
    nb = t_total // tb

    def body(cur_ref, prev_ref, w_ref, out_ref):
        i = pl.program_id(0)
        prev = jnp.where(i > 0, prev_ref[...], 0.0)
        xcat = jnp.concatenate([prev, cur_ref[...]], axis=0)
        acc = xcat[8:] * w_ref[3:4, :]
        for j in range(3):
            acc = acc + pltpu.roll(xcat, 3 - j, 0)[8:] * w_ref[j:j + 1, :]
        out_ref[...] = acc

    return pl.pallas_call(
        body, grid=(nb,),
        in_specs=[pl.BlockSpec((tb, width), lambda i: (i, 0)),
                  pl.BlockSpec((8, width), lambda i: (jnp.maximum(i * (tb // 8) - 1, 0), 0)),
                  _whole((4, width))],
        out_specs=pl.BlockSpec((tb, width), lambda i: (i, 0)), out_shape=_sds((t_total, width)),
        name="conv_fwd", compiler_params=_cparams(("parallel",)))(proj, proj, conv_w)


def _conv_bwd(dc, proj, conv_w, tb):
    t_total, width = dc.shape
    nb = t_total // tb

    def body(dcur_ref, dnext_ref, cur_ref, prev_ref, w_ref, dx_ref, dw_ref):
        i = pl.program_id(0)
        dnext = jnp.where(i < nb - 1, dnext_ref[...], 0.0)
        dcur = dcur_ref[...]
        dcat = jnp.concatenate([dcur, dnext], axis=0)
        prev = jnp.where(i > 0, prev_ref[...], 0.0)
        xcat = jnp.concatenate([prev, cur_ref[...]], axis=0)

        @pl.when(i == 0)
        def _():
            dw_ref[...] = jnp.zeros(dw_ref.shape, F32)

        dx = dcur * w_ref[3:4, :]
        dw_ref[3:4, :] += jnp.sum(dcur * xcat[8:], axis=0, keepdims=True)
        for j in range(3):
            dx = dx + pltpu.roll(dcat, 8 - (3 - j), 0)[8:] * w_ref[j:j + 1, :]
            dw_ref[j:j + 1, :] += jnp.sum(dcur * pltpu.roll(xcat, 3 - j, 0)[8:], axis=0, keepdims=True)
        dx_ref[...] = dx.astype(dx_ref.dtype)

    return pl.pallas_call(
        body, grid=(nb,),
        in_specs=[pl.BlockSpec((tb, width), lambda i: (i, 0)),
                  pl.BlockSpec((8, width), lambda i: (jnp.minimum((i + 1) * (tb // 8), t_total // 8 - 1), 0)),
                  pl.BlockSpec((tb, width), lambda i: (i, 0)),
                  pl.BlockSpec((8, width), lambda i: (jnp.maximum(i * (tb // 8) - 1, 0), 0)),
                  _whole((4, width))],
        out_specs=[pl.BlockSpec((tb, width), lambda i: (i, 0)), _whole((4, width))],
        out_shape=[_sds((t_total, width), BF16), _sds((4, width))],
        name="conv_bwd", compiler_params=_cparams(("arbitrary",)))(dc, dc, proj, proj, conv_w)


def _head_spec(tb, nb=None):
    if nb is None:
        return pl.BlockSpec((tb, DN_D), lambda h, i: (i, h))
    return pl.BlockSpec((tb, DN_D), lambda h, i: (nb - 1 - i, h))


def _intra_spec(tb, nb=None):
    if nb is None:
        return pl.BlockSpec((None, tb, PAIR), lambda h, i: (h, i, 0))
    return pl.BlockSpec((None, tb, PAIR), lambda h, i: (h, nb - 1 - i, 0))


def _state_spec(tb, nb=None):
    if nb is None:
        return pl.BlockSpec((None, tb // CHUNK, DN_D, DN_D), lambda h, i: (h, i, 0, 0))
    return pl.BlockSpec((None, tb // CHUNK, DN_D, DN_D), lambda h, i: (h, nb - 1 - i, 0, 0))


def _scan_specs(tb, nb=None):
    blk = (lambda i: i) if nb is None else (lambda i: nb - 1 - i)
    rows = pl.BlockSpec((tb, DN_W), lambda i: (blk(i), 0))
    pair = pl.BlockSpec((DN_HEADS, tb, PAIR), lambda i: (0, blk(i), 0))
    states = pl.BlockSpec((DN_HEADS, tb // CHUNK, DN_D, DN_D), lambda i: (0, blk(i), 0, 0))
    return rows, pair, states


def _dn2_fwd(qd, kd, u, w, intra, cd, tb):
    t_total = qd.shape[0]
    rows, pair, states = _scan_specs(tb)

    def body(qd_ref, kd_ref, u_ref, w_ref, a_ref, cd_ref, o_ref, save_ref, state):
        @pl.when(pl.program_id(0) == 0)
        def _():
            state[...] = jnp.zeros(state.shape, F32)

        for j in range(tb // CHUNK):
            sl = pl.ds(CHUNK * j, CHUNK)
            for h in range(DN_HEADS):
                hl = pl.ds(DN_D * h, DN_D)
                s0 = state[h]
                save_ref[h, j] = s0
                out, s1 = _dn2_step(j % 2, s0, qd_ref[sl, hl], kd_ref[sl, hl], u_ref[sl, hl], w_ref[sl, hl],
                                    a_ref[h, sl, :], cd_ref[pl.ds(CHUNK * j, 1), hl])
                o_ref[sl, hl] = out
                state[h] = s1

    return pl.pallas_call(
        body, grid=(t_total // tb,), in_specs=[rows, rows, rows, rows, pair, rows],
        out_specs=[rows, states],
        out_shape=[_sds((t_total, DN_W)), _sds((DN_HEADS, t_total // CHUNK, DN_D, DN_D))],
        scratch_shapes=[pltpu.VMEM((DN_HEADS, DN_D, DN_D), F32)], name="dn2_fwd",
        compiler_params=_cparams(("arbitrary",)))(qd, kd, u, w, intra, cd)


def _dn2_bwd(qd, kd, u, w, intra, cd, saved, d_o, tb):
    t_total = qd.shape[0]
    nb = t_total // tb
    rows, pair, states = _scan_specs(tb, nb)

    def body(qd_ref, kd_ref, u_ref, w_ref, a_ref, cd_ref, save_ref, do_ref,
             dqd_ref, dkd_ref, du_ref, dw_ref, da_ref, dcd_ref, dstate):
        @pl.when(pl.program_id(0) == 0)
        def _():
            dstate[...] = jnp.zeros(dstate.shape, F32)

        first_row = _iota((CHUNK, DN_D), 0) == 0
        for j in reversed(range(tb // CHUNK)):
            sl = pl.ds(CHUNK * j, CHUNK)
            for h in range(DN_HEADS):
                hl = pl.ds(DN_D * h, DN_D)
                _, vjp = jax.vjp(functools.partial(_dn2_step, j % 2), save_ref[h, j], qd_ref[sl, hl], kd_ref[sl, hl],
                                 u_ref[sl, hl], w_ref[sl, hl], a_ref[h, sl, :], cd_ref[pl.ds(CHUNK * j, 1), hl])
                ds0, dqd, dkd, du, dw, da, dcd = vjp((do_ref[sl, hl], dstate[h]))
                dqd_ref[sl, hl] = dqd
                dkd_ref[sl, hl] = dkd
                du_ref[sl, hl] = du
                dw_ref[sl, hl] = dw
                da_ref[h, sl, :] = da
                dcd_ref[sl, hl] = jnp.where(first_row, dcd, 0.0)
                dstate[h] = ds0

    full = _sds((t_total, DN_W))
    return pl.pallas_call(
        body, grid=(nb,),
        in_specs=[rows, rows, rows, rows, pair, rows, states, rows],
        out_specs=[rows, rows, rows, rows, pair, rows],
        out_shape=[full, full, full, full, _sds((DN_HEADS, t_total, PAIR)), full],
        scratch_shapes=[pltpu.VMEM((DN_HEADS, DN_D, DN_D), F32)], name="dn2_bwd",
        compiler_params=_cparams(("arbitrary",)))(qd, kd, u, w, intra, cd, saved, d_o)


def _loss_and_grad(y, target, tb):
    t_total, d = y.shape

    def body(y_ref, t_ref, dy_ref, acc_ref):
        @pl.when(pl.program_id(0) == 0)
        def _():
            acc_ref[...] = jnp.zeros(acc_ref.shape, F32)

        err = y_ref[...] - t_ref[...]
        dy_ref[...] = err * (1.0 / d)
        acc_ref[...] += jnp.sum(err * err, axis=0, keepdims=True)

    dy, acc = pl.pallas_call(
        body, grid=(t_total // tb,), in_specs=[_rows(tb, d), _rows(tb, d)],
        out_specs=[_rows(tb, d), _whole((1, d))], out_shape=[_sds((t_total, d)), _sds((1, d))],
        name="loss", compiler_params=_cparams(("arbitrary",)))(y, target)
    return 0.5 * jnp.sum(acc) / d, dy


def _halo_sum(mains, halos, tb):
    t_total, width = mains[0].shape
    nb = t_total // tb
    n = len(mains)

    def body(*refs):
        out_ref = refs[-1]
        i = pl.program_id(0)
        tot = refs[0][...]
        for r in refs[1:n]:
            tot = tot + r[...]
        hal = refs[n][...]
        for r in refs[n + 1:2 * n]:
            hal = hal + r[...]
        hal = jnp.where(i < nb - 1, hal, 0.0)
        out_ref[...] = tot + jnp.concatenate([jnp.zeros((tb - WINDOW, width), F32), hal], axis=0)

    return pl.pallas_call(
        body, grid=(nb,),
        in_specs=[_rows(tb, width)] * n
        + [pl.BlockSpec((None, WINDOW, width), lambda i: (jnp.minimum(i + 1, nb - 1), 0, 0))] * n,
        out_specs=_rows(tb, width), out_shape=_sds((t_total, width)), name="halo_sum",
        compiler_params=_cparams(("parallel",)))(*mains, *halos)


def _adamw(recv, w, m, v, tr, name):
    slots, r_total, c_total = recv.shape
    tr = min(tr, r_total)
    assert r_total % tr == 0
    c1 = 1.0 / (1.0 - ADAM_B1 ** ADAM_STEP)
    c2 = 1.0 / (1.0 - ADAM_B2 ** ADAM_STEP)

    def body(recv_ref, w_ref, m_ref, v_ref, g_ref, d_ref, nm_ref, nv_ref):
        g = recv_ref[0].astype(F32)
        for s in range(1, slots):
            g = g + recv_ref[s].astype(F32)
        nm = ADAM_B1 * m_ref[...] + (1.0 - ADAM_B1) * g
        nv = ADAM_B2 * v_ref[...] + (1.0 - ADAM_B2) * (g * g)
        g_ref[...] = g
        nm_ref[...] = nm
        nv_ref[...] = nv
        d_ref[...] = -ADAM_LR * ((nm * c1) / (jnp.sqrt(nv * c2) + ADAM_EPS) + ADAM_WD * w_ref[...])

    blk = pl.BlockSpec((tr, c_total), lambda i: (i, 0))
    return pl.pallas_call(
        body, grid=(r_total // tr,),
        in_specs=[pl.BlockSpec((slots, tr, c_total), lambda i: (0, i, 0)), blk, blk, blk],
        out_specs=[blk] * 4, out_shape=[_sds((r_total, c_total))] * 4, name=name,
        compiler_params=_cparams(("parallel",)))(recv, w, m, v)


def _me_and_peers():
    x, y, c = lax.axis_index("x"), lax.axis_index("y"), lax.axis_index("c")
    me = 4 * x + 2 * y + c
    peers = []
    for k in range(1, N_DEV):
        px = 1 - x if (k >> 2) & 1 else x
        py = 1 - y if (k >> 1) & 1 else y
        pc = 1 - c if k & 1 else c
        peers.append(((px, py, pc), 4 * px + 2 * py + pc))
    return me, peers


def _small_exchange(packed, reduce):
    r_total = packed.shape[0]

    def body(p_ref, out_ref, gath_ref, send_sems, recv_sems):
        me, peers = _me_and_peers()
        gath_ref[me] = p_ref[...]
        copies = []
        for k, (dev, _) in enumerate(peers):
            cp = pltpu.make_async_remote_copy(src_ref=p_ref, dst_ref=gath_ref.at[me], send_sem=send_sems.at[k],
                                              recv_sem=recv_sems.at[k], device_id=dev,
                                              device_id_type=pl.DeviceIdType.MESH)
            cp.start()
            copies.append(cp)
        for k, (dev, idx) in enumerate(peers):
            pltpu.make_async_remote_copy(src_ref=p_ref, dst_ref=gath_ref.at[idx], send_sem=send_sems.at[k],
                                         recv_sem=recv_sems.at[k], device_id=dev,
                                         device_id_type=pl.DeviceIdType.MESH).wait_recv()
        for cp in copies:
            cp.wait_send()
        if reduce:
            tot = gath_ref[0]
            for d in range(1, N_DEV):
                tot = tot + gath_ref[d]
            out_ref[...] = tot
        else:
            out_ref[...] = gath_ref[...]

    out_shape = _sds((r_total, 128)) if reduce else _sds((N_DEV, r_total, 128))
    return pl.pallas_call(
        body, in_specs=[pl.BlockSpec(memory_space=pltpu.VMEM)], out_specs=pl.BlockSpec(memory_space=pltpu.VMEM),
        out_shape=out_shape,
        scratch_shapes=[pltpu.VMEM((N_DEV, r_total, 128), F32), pltpu.SemaphoreType.DMA((N_DEV - 1,)),
                        pltpu.SemaphoreType.DMA((N_DEV - 1,))],
        name="small_allreduce" if reduce else "small_allgather")(packed)


def _slot(ref, axis, idx, size):
    sel = [slice(None)] * len(ref.shape)
    sel[axis] = idx if size is None else pl.ds(pl.multiple_of(idx * size, size), size)
    return ref.at[tuple(sel)]


def _big_exchange(srcs, dst_shapes, src_view, dst_view, name):
    n = len(srcs)

    def body(*refs):
        src_refs, dst_refs = refs[:n], refs[n:2 * n]
        send_sems, recv_sems, local_sems = refs[2 * n:]
        me, peers = _me_and_peers()
        local, remote = [], []
        for t in range(n):
            loc = pltpu.make_async_copy(src_view(t, src_refs[t], me), dst_view(t, dst_refs[t], me), local_sems.at[t])
            loc.start()
            local.append(loc)
            for k, (dev, idx) in enumerate(peers):
                cp = pltpu.make_async_remote_copy(
                    src_ref=src_view(t, src_refs[t], idx), dst_ref=dst_view(t, dst_refs[t], me),
                    send_sem=send_sems.at[t, k], recv_sem=recv_sems.at[t, k], device_id=dev,
                    device_id_type=pl.DeviceIdType.MESH)
                cp.start()
                remote.append(cp)
        for t in range(n):
            for k, (dev, idx) in enumerate(peers):
                pltpu.make_async_remote_copy(
                    src_ref=src_view(t, src_refs[t], me), dst_ref=dst_view(t, dst_refs[t], idx),
                    send_sem=send_sems.at[t, k], recv_sem=recv_sems.at[t, k], device_id=dev,
                    device_id_type=pl.DeviceIdType.MESH).wait_recv()
        for cp in remote:
            cp.wait_send()
        for cp in local:
            cp.wait()

    any_spec = pl.BlockSpec(memory_space=pl.ANY)
    return pl.pallas_call(
        body, in_specs=[any_spec] * n, out_specs=[any_spec] * n, out_shape=dst_shapes,
        scratch_shapes=[pltpu.SemaphoreType.DMA((n, N_DEV - 1)), pltpu.SemaphoreType.DMA((n, N_DEV - 1)),
                        pltpu.SemaphoreType.DMA((n,))],
        name=name)(*srcs)


BIG = {
    "a_w_in": (1, (2, 1024, A_IN)),
    "b_w_in": (1, (2, 1024, 1024)),
    "w_kv_shared": (0, (1024, 256)),
    "mem_w_kv": (1, (4, 1024, 512)),
    "w_o": (1, (4, 1024, 1024)),
    "mlp_w_up": (2, (4, 1024, 4096)),
    "mlp_w_down": (1, (4, 4096, 1024)),
}
BIG_NAMES = tuple(BIG)


def _allgather_weights(shards):
    dst_shapes, axes, sizes = [], [], []
    for name, s in zip(BIG_NAMES, shards):
        axis, full = BIG[name]
        if name == "mlp_w_up":
            dst_shapes.append(_sds((N_DEV,) + s.shape, BF16))
            axes.append(0)
            sizes.append(None)
        else:
            dst_shapes.append(_sds(full, BF16))
            axes.append(axis)
            sizes.append(s.shape[axis])

    n = len(shards)

    def body(*refs):
        src_refs, dst_refs = refs[:n], refs[n:2 * n]
        send_sems, recv_sems, local_sems = refs[2 * n:]
        x, y, c = lax.axis_index("x"), lax.axis_index("y"), lax.axis_index("c")
        sibling = (x, y, 1 - c)
        chips = [(1 - x, y), (x, 1 - y), (1 - x, 1 - y)]
        index = lambda px, py, pc: 4 * px + 2 * py + pc

        def copy(t, k, block, to, src=None):
            rows = _slot(dst_refs[t], axes[t], index(*block), sizes[t])
            return pltpu.make_async_remote_copy(
                src_ref=rows if src is None else src, dst_ref=rows, send_sem=send_sems.at[t, k],
                recv_sem=recv_sems.at[t, k], device_id=to, device_id_type=pl.DeviceIdType.MESH)

        started, local = [], []
        for t in range(n):
            mine = pltpu.make_async_copy(src_refs[t], _slot(dst_refs[t], axes[t], index(x, y, c), sizes[t]),
                                         local_sems.at[t])
            mine.start()
            local.append(mine)
            first = [copy(t, 0, (x, y, c), sibling, src=src_refs[t])]
            first += [copy(t, 1 + j, (x, y, c), (*chip, c), src=src_refs[t]) for j, chip in enumerate(chips)]
            for cp in first:
                cp.start()
            started += first
        for t in range(n):
            for j, chip in enumerate(chips):
                copy(t, 1 + j, (*chip, c), (x, y, c)).wait_recv()
                passed = copy(t, 4 + j, (*chip, c), sibling)
                passed.start()
                started.append(passed)
        for t in range(n):
            copy(t, 0, sibling, (x, y, c)).wait_recv()
            for j, chip in enumerate(chips):
                copy(t, 4 + j, (*chip, 1 - c), (x, y, c)).wait_recv()
        for cp in started:
            cp.wait_send()
        for cp in local:
            cp.wait()

    any_spec = pl.BlockSpec(memory_space=pl.ANY)
    return pl.pallas_call(
        body, in_specs=[any_spec] * n, out_specs=[any_spec] * n, out_shape=dst_shapes,
        scratch_shapes=[pltpu.SemaphoreType.DMA((n, N_DEV - 1)), pltpu.SemaphoreType.DMA((n, N_DEV - 1)),
                        pltpu.SemaphoreType.DMA((n,))],
        name="allgather_weights")(*shards)


def _scatter_grads(grads):
    dst_shapes, axes, sizes = [], [], []
    for name, g in zip(BIG_NAMES, grads):
        axis, full = BIG[name]
        if name == "mlp_w_up":
            shard = (g.shape[0],) + g.shape[2:]
            axes.append(1)
            sizes.append(None)
        else:
            shard = tuple(d // N_DEV if a == axis else d for a, d in enumerate(full))
            axes.append(axis)
            sizes.append(shard[axis])
        dst_shapes.append(_sds((N_DEV,) + shard, g.dtype))

    def src_view(t, ref, idx):
        return _slot(ref, axes[t], idx, sizes[t])

    def dst_view(t, ref, idx):
        return ref.at[idx]

    return _big_exchange(grads, dst_shapes, src_view, dst_view, "scatter_grads")


def _pad_row(vec, width=128):
    return jnp.pad(vec.astype(F32), (0, width - vec.shape[0])).reshape(1, width)


def _block_sizes(t_total):
    return dict(row=min(256, t_total), dn=min(512, t_total), swa=min(256, t_total), scan=min(256, t_total))


def _ln_apply(h, mix, g, b, tb):
    t_total, d = h.shape
    fwd = lambda pids, *a: _ln_fn(pids, *a) * 2
    return _block_fwd(fwd, [h, mix, g, b], [_rows(tb, d), _rows(tb, d), _whole((1, d)), _whole((1, d))],
                      [_sds((t_total, d)), _sds((t_total, d), BF16)], [_rows(tb, d), _rows(tb, d)],
                      (t_total // tb,), "ln_fwd")


def _ln_grad(h, mix, g, b, dy, tb):
    t_total, d = h.shape
    return _block_bwd(_ln_fn, [h, mix, g, b], [_rows(tb, d), _rows(tb, d), _whole((1, d)), _whole((1, d))],
                      [dy], [_rows(tb, d)], ["s", "s", "a", "a"],
                      [_sds((t_total, d)), _sds((t_total, d), BF16), _sds((1, d)), _sds((1, d))],
                      [_rows(tb, d), _rows(tb, d), _whole((1, d)), _whole((1, d))], (t_total // tb,), "ln_bwd")


def _memattn_specs(tb, qcol):
    return [pl.BlockSpec((tb, MEM_W), lambda i: (i, qcol)), _whole((MEM_W, 2 * MEM_W))]


def _act_epilogue(acc):
    r = jnp.maximum(acc, 0.0)
    return (r * r,)


def _dact_epilogue(acc, act):
    return (acc * (2.0 * jnp.sqrt(act.astype(F32))),)


def _add_epilogue(acc, other):
    return (acc + other,)


def _local_step(x, mem, positions, target, wts, small):
    t_total = x.shape[0]
    bs = _block_sizes(t_total)
    tb, tdn, tsw = bs["row"], bs["dn"], bs["swa"]
    nb = t_total // tb
    nbs = t_total // tsw

    inv_freq = ROPE_THETA ** (-jnp.arange(0, SWA_DH, 2, dtype=F32) / SWA_DH)
    ang = positions.astype(F32)[:, None] * inv_freq
    cos = jnp.tile(jnp.cos(ang), (1, 4))
    sin = jnp.tile(jnp.sin(ang), (1, 4))

    w_a = [jnp.concatenate([wts["a_w_in"][l][:, :3072], wts["a_w_in"][l][:, 3084:], wts["a_w_in"][l][:, 3072:3084],
                            jnp.zeros((D_MODEL, A_IN_PAD - A_IN), BF16)], axis=1) for l in range(N_A)]
    wkv = wts["w_kv_shared"]
    w_kvd = jnp.concatenate([wkv[:, 64 * (i // 2):64 * (i // 2 + 1)] for i in range(8)], axis=1)
    mem_b = mem.astype(BF16)
    up_spec_nn = lambda l: pl.BlockSpec((None, None, D_MODEL, 512), lambda i, j, k, l=l: (j, l, k, 0))
    up_spec_nt = lambda l: pl.BlockSpec((None, None, D_MODEL, 512), lambda i, j, k, l=l: (k, l, j, 0))

    saved = []
    h, hb = x, x.astype(BF16)
    kr = vd_src = None
    for l in range(DEPTH):
        sv = dict(h=h, hb=hb)
        kvm = _matmul(mem_b, wts["mem_w_kv"][l], "nn", [F32], "mm_memkv", tm=256)
        if l < N_A:
            proj = _matmul(hb, w_a[l], "nn", [F32], "mm_proj_a", tn=1152)
            conv_w = small["a_conv_w"][l]
            c = _conv_fwd(proj, conv_w, tb)
            alog, dtb = _pad_row(small["a_A_log"][l]), _pad_row(small["a_dt_bias"][l])
            rowa_in = [c, proj, alog, dtb]
            rowa_specs = [_rows(tb, 3 * DN_W), _rows(tb, 128, 26), _whole((1, 128)), _whole((1, 128))]
            q, k, v, gcb, betab = _block_fwd(_rowa_fn, rowa_in, rowa_specs, [_sds((t_total, DN_W))] * 5,
                                             [_rows(tb, DN_W)] * 5, (nb,), "rowa_fwd")
            hs = _head_spec(tdn)
            dn_grid = (DN_HEADS, t_total // tdn)
            full = _sds((t_total, DN_W))
            pair_sds = _sds((DN_HEADS, t_total, PAIR))
            dn1_out_shapes = [full, full, pair_sds, full, full, full, pair_sds]
            dn1_out_specs = [hs, hs, _intra_spec(tdn), hs, hs, hs, _intra_spec(tdn)]
            u, w, intra, qd, kd, cd, tinv = _block_fwd(_dn1_fn, [q, k, v, gcb, betab], [hs] * 5, dn1_out_shapes,
                                                       dn1_out_specs, dn_grid, "dn1_fwd")
            o, states = _dn2_fwd(qd, kd, u, w, intra, cd, bs["scan"])
            nw = small["a_norm_w"][l].reshape(1, DN_D)
            post_in = [o, proj, nw]
            post_specs = [_rows(tb, DN_W), _rows(tb, DN_W, 3), _whole((1, DN_D))]
            (og,) = _block_fwd(_post_fn, post_in, post_specs, [_sds((t_total, DN_W), BF16)], [_rows(tb, DN_W)],
                               (nb,), "post_fwd")
            qm_col = 12
            sv.update(proj=proj, c=c, rowa_in=rowa_in, rowa_specs=rowa_specs, dn1_in=[q, k, v, gcb, betab, tinv],
                      dn2_in=[qd, kd, u, w, intra, cd], states=states, post_in=post_in, post_specs=post_specs,
                      conv_w=conv_w)
        else:
            jb = l - N_A
            proj = _matmul(hb, wts["b_w_in"][jb], "nn", [F32], "mm_proj_b")
            sinks = _pad_row(small["b_sinks"][jb])
            swa_in = [proj, cos, sin, kr, kr, vd_src, vd_src, sinks]
            swa_specs = [_rows(tsw, DN_W), _rows(tsw, 128), _rows(tsw, 128),
                         pl.BlockSpec((WINDOW, 256), lambda i: (jnp.maximum(i * (tsw // WINDOW) - 1, 0), 0)),
                         _rows(tsw, 256),
                         pl.BlockSpec((WINDOW, 256), lambda i: (jnp.maximum(i * (tsw // WINDOW) - 1, 0), 1)),
                         _rows(tsw, 256, 1), _whole((1, 128))]
            (og,) = _block_fwd(_swa_fn, swa_in, swa_specs, [_sds((t_total, DN_W), BF16)], [_rows(tsw, DN_W)],
                               (nbs,), "swa_fwd")
            qm_col = 3
            sv.update(proj=proj, swa_in=swa_in, swa_specs=swa_specs)
        mem_in = [proj, kvm]
        (mo,) = _block_fwd(_memattn_fn, mem_in, _memattn_specs(tb, qm_col), [_sds((t_total, MEM_W), BF16)],
                           [_rows(tb, MEM_W)], (nb,), "memattn_fwd")
        mixin = jnp.concatenate([og, mo], axis=1)
        mix = _matmul(mixin, wts["w_o"][l], "nn", [F32], "mm_wo")
        g0, b0 = small["ln_g"][l, 0].reshape(1, -1), small["ln_b"][l, 0].reshape(1, -1)
        h1, h1b = _ln_apply(h, mix, g0, b0, tb)
        act = _matmul(h1b, wts["mlp_w_up"], "nn", [BF16], "mm_up", epi=_act_epilogue,
                      b_spec=up_spec_nn(l), n_total=4 * D_MODEL, tn=512)
        mlp = _matmul(act, wts["mlp_w_down"][l], "nn", [F32], "mm_down")
        g1, b1 = small["ln_g"][l, 1].reshape(1, -1), small["ln_b"][l, 1].reshape(1, -1)
        h2, h2b = _ln_apply(h1, mlp, g1, b1, tb)
        sv.update(kvm=kvm, mem_in=mem_in, qm_col=qm_col, mixin=mixin, mix=mix, ln0=(g0, b0), h1=h1, h1b=h1b,
                  act=act, mlp=mlp, ln1=(g1, b1))
        saved.append(sv)
        h, hb = h2, h2b
        if l == N_A - 1:
            kvd = _matmul(hb, w_kvd, "nn", [F32], "mm_kvd")
            krope_in = [kvd, cos, sin]
            krope_specs = [_rows(tb, 256), _rows(tb, 128), _rows(tb, 128)]
            (kr,) = _block_fwd(_krope_fn, krope_in, krope_specs, [_sds((t_total, 256))], [_rows(tb, 256)], (nb,),
                               "krope_fwd")
            vd_src = kvd

    loss, dh = _loss_and_grad(h, target, tb)

    grads = {n: [None] * BIG[n][1][0] for n in BIG_NAMES if n != "w_kv_shared"}
    sg = dict(a_conv_w=[None] * N_A, a_A_log=[None] * N_A, a_dt_bias=[None] * N_A, a_norm_w=[None] * N_A,
              b_sinks=[None] * (DEPTH - N_A), ln_g=[[None, None] for _ in range(DEPTH)],
              ln_b=[[None, None] for _ in range(DEPTH)])
    dk_parts, dv_parts = [], []
    for l in reversed(range(DEPTH)):
        sv = saved[l]
        if l == N_A - 1:
            dkr = _halo_sum([p[0] for p in dk_parts], [p[1] for p in dk_parts], tsw)
            dvv = _halo_sum([p[0] for p in dv_parts], [p[1] for p in dv_parts], tsw)
            (dkraw,) = _block_bwd(_krope_fn, krope_in, krope_specs, [dkr], [_rows(tb, 256)], ["s", None, None],
                                  [_sds((t_total, 256), BF16)], [_rows(tb, 256)], (nb,), "krope_bwd")
            dkvd = jnp.concatenate([dkraw, dvv.astype(BF16)], axis=1)
            g_kvd = _matmul(saved[l + 1]["hb"], dkvd, "tn", [F32], "mm_dw_kvd", tm=1024, tn=512)
            dh = _matmul(dkvd, w_kvd, "nt", [F32], "mm_dx_kvd", epi=_add_epilogue, extras=[dh], tn=1024, tk=512)
            grads["w_kv_shared"] = jnp.concatenate(
                [g_kvd[:, 128 * i:128 * i + 64] + g_kvd[:, 128 * i + 64:128 * (i + 1)] for i in range(4)],
                axis=1).astype(BF16)
        g1, b1 = sv["ln1"]
        dh1a, dmlp, dg1, db1 = _ln_grad(sv["h1"], sv["mlp"], g1, b1, dh, tb)
        dup = _matmul(dmlp, wts["mlp_w_down"][l], "nt", [BF16], "mm_dact", epi=_dact_epilogue, extras=[sv["act"]])
        grads["mlp_w_down"][l] = _matmul(sv["act"], dmlp, "tn", [BF16], "mm_dw_down")
        g_up = _matmul(sv["h1b"], dup, "tn", [BF16], "mm_dw_up", tn=512, tk=2048,
                       out_shapes=[_sds((N_DEV, D_MODEL, 512), BF16)],
                       out_specs=[pl.BlockSpec((None, 1024, 512), lambda i, j, k: (j, i, 0))])
        grads["mlp_w_up"][l] = g_up
        dh1 = _matmul(dup, wts["mlp_w_up"], "nt", [F32], "mm_dx_up", epi=_add_epilogue, extras=[dh1a],
                      b_spec=up_spec_nt(l), n_total=D_MODEL, tn=1024, tk=512)
        g0, b0 = sv["ln0"]
        dha, dmix, dg0, db0 = _ln_grad(sv["h"], sv["mix"], g0, b0, dh1, tb)
        sg["ln_g"][l] = [dg0, dg1]
        sg["ln_b"][l] = [db0, db1]
        grads["w_o"][l] = _matmul(sv["mixin"], dmix, "tn", [BF16], "mm_dw_o")
        dmixin = _matmul(dmix, wts["w_o"][l], "nt", [F32], "mm_dx_o", tn=1024)
        dqm, dkvm = _block_bwd(_memattn_fn, sv["mem_in"], _memattn_specs(tb, sv["qm_col"]), [dmixin],
                               [_rows(tb, MEM_W, 3)], ["s", "a"],
                               [_sds((t_total, MEM_W), BF16), _sds((MEM_W, 2 * MEM_W))],
                               [_rows(tb, MEM_W), _whole((MEM_W, 2 * MEM_W))], (nb,), "memattn_bwd")
        grads["mem_w_kv"][l] = _matmul(mem_b, dkvm.astype(BF16), "tn", [BF16], "mm_dw_memkv", tm=1024, tn=512)
        if l < N_A:
            d_o, dz, dnw = _block_bwd(_post_fn, sv["post_in"], sv["post_specs"], [dmixin], [_rows(tb, DN_W)],
                                      ["s", "s", "a"],
                                      [_sds((t_total, DN_W)), _sds((t_total, DN_W), BF16), _sds((1, DN_D))],
                                      [_rows(tb, DN_W), _rows(tb, DN_W), _whole((1, DN_D))], (nb,), "post_bwd")
            sg["a_norm_w"][l] = dnw
            dqd, dkd, du, dw, da, dcd = _dn2_bwd(*sv["dn2_in"], sv["states"], d_o, bs["scan"])
            hs = _head_spec(tdn)
            full = _sds((t_total, DN_W))
            dq, dk, dv, dgc, dbeta = _block_bwd(
                _dn1_fn_known, sv["dn1_in"], [hs] * 5 + [_intra_spec(tdn)], [du, dw, da, dqd, dkd, dcd],
                [hs, hs, _intra_spec(tdn), hs, hs, hs], ["s"] * 5 + [None], [full] * 5, [hs] * 5,
                (DN_HEADS, t_total // tdn), "dn1_bwd")
            dc, dab, dalog, ddtb = _block_bwd(
                _rowa_fn, sv["rowa_in"], sv["rowa_specs"], [dq, dk, dv, dgc, dbeta], [_rows(tb, DN_W)] * 5,
                ["s", "s", "a", "a"],
                [_sds((t_total, 3 * DN_W)), _sds((t_total, 128), BF16), _sds((1, 128)), _sds((1, 128))],
                [_rows(tb, 3 * DN_W), _rows(tb, 128), _whole((1, 128)), _whole((1, 128))], (nb,), "rowa_bwd")
            sg["a_A_log"][l] = dalog[0, :DN_HEADS]
            sg["a_dt_bias"][l] = ddtb[0, :DN_HEADS]
            dx, dconv = _conv_bwd(dc, sv["proj"], sv["conv_w"], tb)
            sg["a_conv_w"][l] = dconv
            dproj = jnp.concatenate([dx, dz, dqm, dab], axis=1)
            g_in = _matmul(sv["hb"], dproj, "tn", [BF16], "mm_dw_a", tm=1024, tn=1152)
            grads["a_w_in"][l] = jnp.concatenate([g_in[:, :3072], g_in[:, 3328:3340], g_in[:, 3072:3328]], axis=1)
            dh = _matmul(dproj, w_a[l], "nt", [F32], "mm_dx_a", epi=_add_epilogue, extras=[dha], tk=1152)
        else:
            jb = l - N_A
            swa_kinds = ["s", None, None, "s", "s", "s", "s", "a"]
            halo_spec = pl.BlockSpec((None, WINDOW, 256), lambda i: (i, 0, 0))
            dq, dkh, dkc, dvh, dvc, dsink = _block_bwd(
                _swa_fn, sv["swa_in"], sv["swa_specs"], [dmixin], [_rows(tsw, DN_W)], swa_kinds,
                [_sds((t_total, DN_W), BF16), _sds((nbs, WINDOW, 256)), _sds((t_total, 256)),
                 _sds((nbs, WINDOW, 256)), _sds((t_total, 256)), _sds((1, 128))],
                [_rows(tsw, DN_W), halo_spec, _rows(tsw, 256), halo_spec, _rows(tsw, 256), _whole((1, 128))],
                (nbs,), "swa_bwd")
            sg["b_sinks"][jb] = dsink[0, :SWA_HEADS]
            dk_parts.append((dkc, dkh))
            dv_parts.append((dvc, dvh))
            dproj = jnp.concatenate([dq, dqm], axis=1)
            grads["b_w_in"][jb] = _matmul(sv["hb"], dproj, "tn", [BF16], "mm_dw_b")
            dh = _matmul(dproj, wts["b_w_in"][jb], "nt", [F32], "mm_dx_b", epi=_add_epilogue, extras=[dha], tn=1024)

    big = []
    for n in BIG_NAMES:
        if n == "w_kv_shared":
            big.append(grads[n])
        else:
            big.append(jnp.stack(grads[n], axis=0))
    small_grads = dict(
        a_conv_w=jnp.stack(sg["a_conv_w"]), a_A_log=jnp.stack(sg["a_A_log"]), a_dt_bias=jnp.stack(sg["a_dt_bias"]),
        a_norm_w=jnp.concatenate(sg["a_norm_w"], axis=0), b_sinks=jnp.stack(sg["b_sinks"]),
        ln_g=jnp.stack([jnp.concatenate(p, axis=0) for p in sg["ln_g"]]),
        ln_b=jnp.stack([jnp.concatenate(p, axis=0) for p in sg["ln_b"]]))
    return loss, dh, big, small_grads


def _pack(arrays, rows):
    flat = []
    for a in arrays:
        v = a.astype(F32).reshape(-1)
        flat.append(jnp.pad(v, (0, (-v.shape[0]) % 128)))
    flat = jnp.concatenate(flat)
    return jnp.pad(flat, (0, rows * 128 - flat.shape[0])).reshape(rows, 128)


def _unpack(slab, shapes):
    flat = slab.reshape(slab.shape[:-2] + (-1,))
    out, off = [], 0
    for s in shapes:
        n = math.prod(s)
        out.append(flat[..., off:off + n].reshape(slab.shape[:-2] + tuple(s)))
        off += n + (-n) % 128
    return out


def _rows_for(shapes):
    rows = sum((math.prod(s) + 127) // 128 for s in shapes)
    return rows + (-rows) % 8


SMALL_NAMES = ("a_conv_w", "a_A_log", "a_dt_bias", "a_norm_w", "b_sinks", "ln_g", "ln_b")
SMALL_SHARDED = {"a_conv_w": 2, "ln_g": 2, "ln_b": 2}
SMALL_FULL = {"a_conv_w": (2, 4, 2304), "a_A_log": (2, 6), "a_dt_bias": (2, 6), "a_norm_w": (2, 128),
              "b_sinks": (2, 12), "ln_g": (4, 2, 1024), "ln_b": (4, 2, 1024)}


def kernel(x, mem, positions, a_w_in, a_conv_w, a_A_log, a_dt_bias, a_norm_w, b_w_in, b_sinks, w_kv_shared, mem_w_kv, w_o, mlp_w_up, mlp_w_down, ln_g, ln_b, loss_target, m_a_w_in, m_a_conv_w, m_a_A_log, m_a_dt_bias, m_a_norm_w, m_b_w_in, m_b_sinks, m_w_kv_shared, m_mem_w_kv, m_w_o, m_mlp_w_up, m_mlp_w_down, m_ln_g, m_ln_b, v_a_w_in, v_a_conv_w, v_a_A_log, v_a_dt_bias, v_a_norm_w, v_b_w_in, v_b_sinks, v_w_kv_shared, v_mem_w_kv, v_w_o, v_mlp_w_up, v_mlp_w_down, v_ln_g, v_ln_b):
    params = dict(a_w_in=a_w_in, a_conv_w=a_conv_w, a_A_log=a_A_log, a_dt_bias=a_dt_bias, a_norm_w=a_norm_w,
                  b_w_in=b_w_in, b_sinks=b_sinks, w_kv_shared=w_kv_shared, mem_w_kv=mem_w_kv, w_o=w_o,
                  mlp_w_up=mlp_w_up, mlp_w_down=mlp_w_down, ln_g=ln_g, ln_b=ln_b)
    mom = dict(a_w_in=m_a_w_in, a_conv_w=m_a_conv_w, a_A_log=m_a_A_log, a_dt_bias=m_a_dt_bias, a_norm_w=m_a_norm_w,
               b_w_in=m_b_w_in, b_sinks=m_b_sinks, w_kv_shared=m_w_kv_shared, mem_w_kv=m_mem_w_kv, w_o=m_w_o,
               mlp_w_up=m_mlp_w_up, mlp_w_down=m_mlp_w_down, ln_g=m_ln_g, ln_b=m_ln_b)
    var = dict(a_w_in=v_a_w_in, a_conv_w=v_a_conv_w, a_A_log=v_a_A_log, a_dt_bias=v_a_dt_bias, a_norm_w=v_a_norm_w,
               b_w_in=v_b_w_in, b_sinks=v_b_sinks, w_kv_shared=v_w_kv_shared, mem_w_kv=v_mem_w_kv, w_o=v_w_o,
               mlp_w_up=v_mlp_w_up, mlp_w_down=v_mlp_w_down, ln_g=v_ln_g, ln_b=v_ln_b)
    me = 4 * lax.axis_index("x") + 2 * lax.axis_index("y") + lax.axis_index("c")

    full_b = _allgather_weights([params[n].astype(BF16) for n in BIG_NAMES])
    wts = dict(zip(BIG_NAMES, full_b))
    sharded_names = [n for n in SMALL_NAMES if n in SMALL_SHARDED]
    shard_shapes = [params[n].shape for n in sharded_names]
    gathered = _small_exchange(_pack([params[n] for n in sharded_names], _rows_for(shard_shapes)), reduce=False)
    small = {n: params[n] for n in SMALL_NAMES if n not in SMALL_SHARDED}
    for n, g in zip(sharded_names, _unpack(gathered, shard_shapes)):
        small[n] = jnp.moveaxis(g, 0, 2).reshape(SMALL_FULL[n])

    loss, dx, big_grads, small_grads = _local_step(x[0], mem[0], positions[0], loss_target[0], wts, small)
    loss = lax.psum(loss, ("x", "y", "c"))
    recv = _scatter_grads(big_grads)
    out = {}
    for n, r in zip(BIG_NAMES, recv):
        shp = params[n].shape
        rows = math.prod(shp[:-1])
        res = _adamw(r.reshape(N_DEV, rows, shp[-1]), params[n].reshape(rows, shp[-1]),
                     mom[n].reshape(rows, shp[-1]), var[n].reshape(rows, shp[-1]), 32, "adamw_" + n)
        out[n] = [t.reshape(shp) for t in res]
    full_shapes = [SMALL_FULL[n] for n in SMALL_NAMES]
    summed = _small_exchange(_pack([small_grads[n] for n in SMALL_NAMES], _rows_for(full_shapes)), reduce=True)
    local_g = []
    for n, g in zip(SMALL_NAMES, _unpack(summed, full_shapes)):
        if n in SMALL_SHARDED:
            size = params[n].shape[2]
            g = lax.dynamic_slice_in_dim(g, me * size, size, axis=2)
        local_g.append(g)
    local_shapes = [params[n].shape for n in SMALL_NAMES]
    rows = _rows_for(local_shapes)
    res = _adamw(_pack(local_g, rows)[None], _pack([params[n] for n in SMALL_NAMES], rows),
                 _pack([mom[n] for n in SMALL_NAMES], rows), _pack([var[n] for n in SMALL_NAMES], rows), rows,
                 "adamw_small")
    unpacked = [_unpack(t, local_shapes) for t in res]
    for i, n in enumerate(SMALL_NAMES):
        out[n] = [unpacked[k][i] for k in range(4)]

    order = ("a_w_in", "a_conv_w", "a_A_log", "a_dt_bias", "a_norm_w", "b_w_in", "b_sinks", "w_kv_shared",
             "mem_w_kv", "w_o", "mlp_w_up", "mlp_w_down", "ln_g", "ln_b")
    return (loss, dx[None], *[out[n][0] for n in order], *[out[n][1] for n in order],
            *[out[n][2] for n in order], *[out[n][3] for n in order])
```

```python
import functools
import math

import jax
import jax.numpy as jnp
from jax import lax
from jax.experimental import pallas as pl
from jax.experimental.pallas import tpu as pltpu

F32 = jnp.float32
BF16 = jnp.bfloat16

D_MODEL = 1024
DEPTH = 4
N_A = 2
MEM_HEADS = 4
MEM_DH = 64
MEM_W = 256
DN_HEADS = 6
DN_D = 128
DN_W = 768
CHUNK = 64
SWA_DH = 64
SWA_HEADS = 12
WINDOW = 128
ROPE_THETA = 10000.0
LN_EPS = 1e-5
NORM_EPS = 1e-6
DN_ALPHA = (2.0 * DEPTH) ** 0.25
A_IN = 3340
A_IN_PAD = 3456
N_DEV = 8

ADAM_LR = 0.001
ADAM_B1 = 0.9
ADAM_B2 = 0.999
ADAM_EPS = 1e-08
ADAM_WD = 0.01
ADAM_STEP = 10

VMEM_LIMIT = 48 * 1024 * 1024
NEG_BIG = -1e30


def _cparams(sem):
    return pltpu.CompilerParams(dimension_semantics=sem, vmem_limit_bytes=VMEM_LIMIT)


_CONTRACT = {"nn": (1, 0), "nt": (1, 1), "tn": (0, 0)}


def _raw_mm(a, b, mode, prec):
    ca, cb = _CONTRACT[mode]
    dims = (((ca,), (cb,)), ((), ()))
    dot = lambda p, q: lax.dot_general(p, q, dims, preferred_element_type=F32)
    if prec == "bf16":
        return dot(a.astype(BF16), b.astype(BF16))
    a, b = a.astype(F32), b.astype(F32)
    a_hi, b_hi = a.astype(BF16), b.astype(BF16)
    if prec == "sela":
        return dot(a_hi, b_hi) + dot(a_hi, (b - b_hi.astype(F32)).astype(BF16))
    a_lo = (a - a_hi.astype(F32)).astype(BF16)
    if prec == "selb":
        return dot(a_hi, b_hi) + dot(a_lo, b_hi)
    b_lo = (b - b_hi.astype(F32)).astype(BF16)
    return dot(a_hi, b_hi) + (dot(a_hi, b_lo) + dot(a_lo, b_hi))


@functools.partial(jax.custom_vjp, nondiff_argnums=(2, 3))
def mm(a, b, mode, prec):
    return _raw_mm(a, b, mode, prec)


def _mm_fwd(a, b, mode, prec):
    return _raw_mm(a, b, mode, prec), (a, b)


def _mm_bwd(mode, prec, res, ct):
    a, b = res
    if prec == "sela":
        pa, pb = "f32", {"nn": "sela", "nt": "selb", "tn": "sela"}[mode]
    elif prec == "selb":
        pa, pb = {"nn": "selb", "nt": "selb", "tn": "sela"}[mode], "f32"
    else:
        pa = pb = prec
    if mode == "nn":
        return mm(ct, b, "nt", pa), mm(a, ct, "tn", pb)
    if mode == "nt":
        return mm(ct, b, "nn", pa), mm(ct, a, "tn", pb)
    return mm(b, ct, "nt", pa), mm(a, ct, "nn", pb)


mm.defvjp(_mm_fwd, _mm_bwd)


@jax.custom_vjp
def _softplus(x):
    y = jnp.exp(-jnp.abs(x))
    log1p_y = jnp.where(y < 1e-2, y * (1.0 - y * (0.5 - y * (1.0 / 3.0))), jnp.log(1.0 + y))
    return jnp.maximum(x, 0.0) + log1p_y


def _softplus_fwd(x):
    return _softplus(x), x


def _softplus_bwd(x, ct):
    return (ct * jax.nn.sigmoid(x),)


_softplus.defvjp(_softplus_fwd, _softplus_bwd)


def _iota(shape, dim):
    return lax.broadcasted_iota(jnp.int32, shape, dim)


def _block_fwd(fn, ins, in_specs, out_shapes, out_specs, grid, name):
    n_in = len(ins)

    def body(*refs):
        pids = tuple(pl.program_id(a) for a in range(len(grid)))
        vals = [r[...].astype(F32) for r in refs[:n_in]]
        outs = fn(pids, *vals)
        for r, o in zip(refs[n_in:], outs):
            r[...] = o.astype(r.dtype)

    return pl.pallas_call(
        body, grid=grid, in_specs=in_specs, out_specs=out_specs, out_shape=out_shapes, name=name,
        compiler_params=_cparams(("parallel",) * len(grid)))(*ins)


def _block_bwd(fn, ins, in_specs, cts, ct_specs, kinds, g_shapes, g_specs, grid, name):
    n_in, n_ct = len(ins), len(cts)
    didx = [i for i, k in enumerate(kinds) if k]

    def body(*refs):
        in_refs, ct_refs, g_refs = refs[:n_in], refs[n_in:n_in + n_ct], refs[n_in + n_ct:]
        pids = tuple(pl.program_id(a) for a in range(len(grid)))
        vals = [r[...].astype(F32) for r in in_refs]

        def f(*dvals):
            full = list(vals)
            for i, v in zip(didx, dvals):
                full[i] = v
            return tuple(fn(pids, *full))

        _, vjp = jax.vjp(f, *[vals[i] for i in didx])
        gs = vjp(tuple(r[...].astype(F32) for r in ct_refs))
        first = pids[0] == 0
        for p in pids[1:]:
            first = jnp.logical_and(first, p == 0)
        for i, g, r in zip(didx, gs, g_refs):
            if kinds[i] == "s":
                r[...] = g.astype(r.dtype)
            else:
                @pl.when(first)
                def _(r=r):
                    r[...] = jnp.zeros(r.shape, r.dtype)

                r[...] += g.astype(r.dtype)

    sem = ("arbitrary",) * len(grid) if "a" in kinds else ("parallel",) * len(grid)
    return pl.pallas_call(
        body, grid=grid, in_specs=list(in_specs) + list(ct_specs), out_specs=g_specs, out_shape=g_shapes,
        name=name, compiler_params=_cparams(sem))(*ins, *cts)


def _rows(tb, width, col=0):
    return pl.BlockSpec((tb, width), lambda i, col=col: (i, col))


def _whole(shape):
    return pl.BlockSpec(shape, lambda *_: (0,) * len(shape))


def _sds(shape, dtype=F32):
    return jax.ShapeDtypeStruct(shape, dtype)


def _matmul(a, b, mode, out_dtypes, name, epi=None, extras=(), tm=1024, tn=1024, tk=1024,
            b_spec=None, n_total=None, out_specs=None, out_shapes=None):
    if mode == "tn":
        k_total, m_total = a.shape
    else:
        m_total, k_total = a.shape
    if n_total is None:
        n_total = b.shape[0] if mode == "nt" else b.shape[1]
    tm, tn, tk = min(tm, m_total), min(tn, n_total), min(tk, k_total)
    assert m_total % tm == 0 and n_total % tn == 0 and k_total % tk == 0, (name, a.shape, b.shape)
    grid = (m_total // tm, n_total // tn, k_total // tk)
    nk = grid[2]
    if mode == "tn":
        a_spec = pl.BlockSpec((tk, tm), lambda i, j, k: (k, i))
    else:
        a_spec = pl.BlockSpec((tm, tk), lambda i, j, k: (i, k))
    if b_spec is None:
        if mode == "nt":
            b_spec = pl.BlockSpec((tn, tk), lambda i, j, k: (j, k))
        else:
            b_spec = pl.BlockSpec((tk, tn), lambda i, j, k: (k, j))
    tile = pl.BlockSpec((tm, tn), lambda i, j, k: (i, j))
    n_ex, n_out = len(extras), len(out_dtypes)
    ca, cb = _CONTRACT[mode]
    dims = (((ca,), (cb,)), ((), ()))

    def body(*refs):
        a_ref, b_ref = refs[:2]
        ex_refs = refs[2:2 + n_ex]
        out_refs = refs[2 + n_ex:2 + n_ex + n_out]
        part = lax.dot_general(a_ref[...], b_ref[...], dims, preferred_element_type=F32)

        def finish(val):
            res = epi(val, *[e[...] for e in ex_refs]) if epi is not None else (val,)
            for r, o in zip(out_refs, res):
                r[...] = o.astype(r.dtype)

        if nk == 1:
            finish(part)
        else:
            acc = refs[-1]
            k = pl.program_id(2)

            @pl.when(k == 0)
            def _():
                acc[...] = part

            @pl.when(k > 0)
            def _():
                acc[...] += part

            @pl.when(k == nk - 1)
            def _():
                finish(acc[...])

    if out_shapes is None:
        out_shapes = [_sds((m_total, n_total), d) for d in out_dtypes]
        out_specs = [tile] * n_out
    outs = pl.pallas_call(
        body, grid=grid, in_specs=[a_spec, b_spec] + [tile] * n_ex, out_specs=out_specs, out_shape=out_shapes,
        scratch_shapes=[pltpu.VMEM((tm, tn), F32)] if nk > 1 else [], name=name,
        compiler_params=_cparams(("parallel", "parallel", "arbitrary")))(a, b, *extras)
    return outs if n_out > 1 else outs[0]


def _silu(x):
    return x * jax.nn.sigmoid(x)


def _rowa_fn(pids, c, ab, alog, dtb):
    tb = c.shape[0]
    s = _silu(c)
    qs, ks = [], []
    for h in range(DN_HEADS):
        qh = s[:, DN_D * h:DN_D * (h + 1)]
        qs.append(qh * lax.rsqrt(jnp.sum(qh * qh, axis=-1, keepdims=True) + NORM_EPS) * (DN_D ** -0.5))
        kh = s[:, DN_W + DN_D * h:DN_W + DN_D * (h + 1)]
        ks.append(kh * lax.rsqrt(jnp.sum(kh * kh, axis=-1, keepdims=True) + NORM_EPS))
    q = jnp.concatenate(qs, axis=1)
    k = jnp.concatenate(ks, axis=1)
    v = s[:, 2 * DN_W:3 * DN_W]
    g128 = -jnp.exp(alog) * _softplus(ab + dtb)
    b128 = jax.nn.sigmoid(ab)
    r, cc = _iota((tb, tb), 0), _iota((tb, tb), 1)
    tri = jnp.where(((r >> 6) == (cc >> 6)) & (r >= cc), 1.0, 0.0)
    gc128 = mm(tri, g128, "nn", "sela")
    lane, col = _iota((128, DN_W), 0), _iota((128, DN_W), 1)
    exp_a = jnp.where(lane == (col >> 7), 1.0, 0.0)
    exp_b = jnp.where(lane == (col >> 7) + DN_HEADS, 1.0, 0.0)
    return q, k, v, mm(gc128, exp_a, "nn", "selb"), mm(b128, exp_b, "nn", "selb")


def _tri_inv_raw(lows, block):
    n = lows[0].shape[0]
    r, c = _iota((n, n), 0), _iota((n, n), 1)
    lg = 0
    xs = None
    while (1 << lg) < block:
        off = ((r >> (lg + 1)) == (c >> (lg + 1))) & (((r >> lg) & 1) == 1) & (((c >> lg) & 1) == 0)
        cblks = [jnp.where(off, low, 0.0) for low in lows]
        if xs is None:
            xs = [jnp.where(r == c, 1.0, 0.0) - cb for cb in cblks]
        else:
            ys = [mm(cb, x, "nn", "f32") for cb, x in zip(cblks, xs)]
            xs = [x - mm(x, y, "nn", "f32") for x, y in zip(xs, ys)]
        lg += 1
    return tuple(xs)


def _tri_inv_cotangent(block, xs, cts):
    n = xs[0].shape[0]
    r, c = _iota((n, n), 0), _iota((n, n), 1)
    shift = block.bit_length() - 1
    keep = ((r >> shift) == (c >> shift)) & (r > c)
    gs = [mm(x, ct, "tn", "f32") for x, ct in zip(xs, cts)]
    gs = [mm(g, x, "nt", "f32") for g, x in zip(gs, xs)]
    return tuple(jnp.where(keep, -g, 0.0) for g in gs)


@functools.partial(jax.custom_vjp, nondiff_argnums=(1,))
def _tri_inv(lows, block):
    return _tri_inv_raw(lows, block)


def _tri_inv_fwd(lows, block):
    xs = _tri_inv_raw(lows, block)
    return xs, xs


def _tri_inv_bwd(block, xs, cts):
    return (_tri_inv_cotangent(block, xs, cts),)


_tri_inv.defvjp(_tri_inv_fwd, _tri_inv_bwd)


@functools.partial(jax.custom_vjp, nondiff_argnums=(2,))
def _tri_inv_known(lows, known, block):
    return known


def _tri_inv_known_fwd(lows, known, block):
    return known, known


def _tri_inv_known_bwd(block, xs, cts):
    return _tri_inv_cotangent(block, xs, cts), tuple(jnp.zeros_like(x) for x in xs)


_tri_inv_known.defvjp(_tri_inv_known_fwd, _tri_inv_known_bwd)


PAIR = 2 * CHUNK


def _dn1_pairs(q, k, v, gc, beta, tinv_known=None):
    assert PAIR == DN_D
    n = PAIR
    pairs = range(q.shape[0] // n)
    cut = lambda t: [t[n * j:n * (j + 1)] for j in pairs]
    q, k, v, gc, beta = cut(q), cut(k), cut(v), cut(gc), cut(beta)
    onehot = jnp.where(_iota((n, DN_D), 1) == 0, 1.0, 0.0)
    r, c = _iota((n, n), 0), _iota((n, n), 1)
    same = (r >> 6) == (c >> 6)
    incl, strict = same & (r >= c), same & (r > c)
    row = _iota((n, DN_D), 0)
    eg = [jnp.exp(g) for g in gc]
    kb = [k[j] * beta[j] for j in pairs]
    g_row = [mm(onehot, g, "nt", "sela") for g in gc]
    kk = [mm(kb[j], k[j], "nt", "bf16") for j in pairs]
    qk = [mm(q[j], k[j], "nt", "bf16") for j in pairs]
    decay = [jnp.exp(jnp.where(incl, gc[j] - g_row[j], NEG_BIG)) for j in pairs]
    low = tuple(jnp.where(strict, kk[j] * decay[j], 0.0) for j in pairs)
    if tinv_known is None:
        tinv = _tri_inv(low, CHUNK)
    else:
        tinv = _tri_inv_known(low, tuple(cut(tinv_known)), CHUNK)
    uw = [mm(tinv[j], jnp.concatenate([v[j] * beta[j], kb[j] * eg[j]], axis=1), "nn", "f32") for j in pairs]
    intra = [jnp.where(incl, qk[j] * decay[j], 0.0) for j in pairs]
    g_last = []
    for g in gc:
        last0 = jnp.sum(jnp.where(row == CHUNK - 1, g, 0.0), axis=0, keepdims=True)
        last1 = jnp.sum(jnp.where(row == PAIR - 1, g, 0.0), axis=0, keepdims=True)
        g_last.append(jnp.where(row < CHUNK, last0, last1))
    join = lambda parts: jnp.concatenate(parts, axis=0)
    return (join([t[:, :DN_D] for t in uw]), join([t[:, DN_D:] for t in uw]), join(intra),
            join([q[j] * eg[j] for j in pairs]), join([k[j] * jnp.exp(g_last[j] - gc[j]) for j in pairs]),
            join([jnp.exp(g) for g in g_last]), join(list(tinv)))


def _dn1_fn(pids, q, k, v, gc, beta):
    return _dn1_pairs(q, k, v, gc, beta)


def _dn1_fn_known(pids, q, k, v, gc, beta, tinv):
    return _dn1_pairs(q, k, v, gc, beta, tinv)[:6]


def _dn2_step(half, state, qd, kd, u, w, intra, cd_row):
    heads = range(len(state))
    v_new = [u[h] - mm(w[h], state[h], "nn", "bf16") for h in heads]
    zeros = jnp.zeros_like(v_new[0])
    v_pair = [jnp.concatenate([v, zeros] if half == 0 else [zeros, v], axis=0) for v in v_new]
    from_state = [mm(qd[h], state[h], "nn", "bf16") for h in heads]
    out = tuple(from_state[h] + mm(intra[h], v_pair[h], "nn", "bf16") for h in heads)
    return out, tuple(state[h] * cd_row[h] + mm(kd[h], v_new[h], "tn", "bf16") for h in heads)


def _post_fn(pids, o, z, nw):
    outs = []
    for h in range(DN_HEADS):
        oh = o[:, DN_D * h:DN_D * (h + 1)]
        zh = z[:, DN_D * h:DN_D * (h + 1)]
        y = oh * lax.rsqrt(jnp.mean(oh * oh, axis=-1, keepdims=True) + NORM_EPS) * nw
        outs.append(y * _silu(zh))
    return (jnp.concatenate(outs, axis=1),)


def _memattn_fn(pids, qm, kvm):
    kmem, vmem = kvm[:, :MEM_W], kvm[:, MEM_W:]
    lane = _iota((1, MEM_W), 1)
    heads = range(MEM_HEADS)
    hm = [jnp.where((lane >> 6) == h, 1.0, 0.0) for h in heads]
    s = [mm(qm * (hm[h] * MEM_DH ** -0.5), kmem, "nt", "bf16") for h in heads]
    e = [jnp.exp(t - lax.stop_gradient(jnp.max(t, axis=-1, keepdims=True))) for t in s]
    o = [mm(e[h], vmem, "nn", "bf16") * (hm[h] / jnp.sum(e[h], axis=-1, keepdims=True)) for h in heads]
    return ((o[0] + o[1]) + (o[2] + o[3]),)


def _ln_fn(pids, h, mix, g, b):
    x = DN_ALPHA * h + mix
    mu = jnp.mean(x, axis=-1, keepdims=True)
    xc = x - mu
    var = jnp.mean(xc * xc, axis=-1, keepdims=True)
    return (xc * lax.rsqrt(var + LN_EPS) * g + b,)


def _rope_matrix():
    i, j = _iota((128, 128), 0), _iota((128, 128), 1)
    jj = j & 63
    return jnp.where((jj < 32) & (i == j + 32), -1.0, 0.0) + jnp.where((jj >= 32) & (i == j - 32), 1.0, 0.0)


def _rope128(x, cos, sin, rot):
    return x * cos + mm(x, rot, "nn", "selb") * sin


def _krope_fn(pids, kraw, cos, sin):
    rot = _rope_matrix()
    return (jnp.concatenate([_rope128(kraw[:, 128 * g:128 * (g + 1)], cos, sin, rot) for g in range(2)], axis=1),)


def _swa_fn(pids, qraw, cos, sin, k_halo, k_cur, v_halo, v_cur, sinks):
    tb = qraw.shape[0]
    nwin = tb // WINDOW
    rot = _rope_matrix()
    kcat = jnp.concatenate([k_halo, k_cur], axis=0)
    vcat = jnp.concatenate([v_halo, v_cur], axis=0)
    lane = _iota((1, 128), 1)
    halves = (jnp.where(lane < 64, 1.0, 0.0), jnp.where(lane >= 64, 1.0, 0.0))
    group = SWA_HEADS // 2
    rows = group * WINDOW
    qi, kj = _iota((rows, 2 * WINDOW), 0) & (WINDOW - 1), _iota((rows, 2 * WINDOW), 1)
    diff = qi + WINDOW - kj
    band = (diff >= 0) & (diff < WINDOW)
    qg = [_rope128(qraw[:, 128 * p:128 * (p + 1)], cos, sin, rot) for p in range(group)]
    sink = []
    for kv in range(2):
        cols = [jnp.sum(jnp.where(lane == group * kv + i, sinks, 0.0), axis=-1, keepdims=True)
                + jnp.zeros((WINDOW, 1), F32) for i in range(group)]
        sink.append(jnp.concatenate(cols, axis=0))
    units = [(w, kv) for w in range(nwin) for kv in range(2)]
    n_units = range(len(units))
    q6 = [jnp.concatenate([qg[3 * kv + i // 2][WINDOW * w:WINDOW * (w + 1)] * (halves[i % 2] * SWA_DH ** -0.5)
                           for i in range(group)], axis=0) for w, kv in units]
    keys = [kcat[WINDOW * w:WINDOW * (w + 2), 128 * kv:128 * (kv + 1)] for w, kv in units]
    vals = [vcat[WINDOW * w:WINDOW * (w + 2), 128 * kv:128 * (kv + 1)] for w, kv in units]
    s = [mm(q6[u], keys[u], "nt", "bf16") for u in n_units]
    s = [jnp.where(band & ((pids[0] * tb + WINDOW * (w - 1) + kj) >= 0), s[u], NEG_BIG)
         for u, (w, kv) in enumerate(units)]
    m = [lax.stop_gradient(jnp.maximum(jnp.max(s[u], axis=-1, keepdims=True), sink[kv]))
         for u, (w, kv) in enumerate(units)]
    e = [jnp.exp(s[u] - m[u]) for u in n_units]
    denom = [jnp.sum(e[u], axis=-1, keepdims=True) + jnp.exp(sink[kv] - m[u]) for u, (w, kv) in enumerate(units)]
    o = [mm(e[u], vals[u], "nn", "bf16") / denom[u] for u in n_units]
    out_rows = []
    for w in range(nwin):
        lanes = []
        for p in range(group):
            ou = o[units.index((w, p // 3))]
            i = 2 * (p % 3)
            lanes.append(ou[WINDOW * i:WINDOW * (i + 1)] * halves[0] + ou[WINDOW * (i + 1):WINDOW * (i + 2)] * halves[1])
        out_rows.append(jnp.concatenate(lanes, axis=1))
    return (jnp.concatenate(out_rows, axis=0),)


def _conv_fwd(proj, conv_w, tb):
    t_total = proj.shape[0]
    width = conv_w.shape[1]
    nb = t_total // tb

    def body(cur_ref, prev_ref, w_ref, out_ref):
        i = pl.program_id(0)
        prev = jnp.where(i > 0, prev_ref[...], 0.0)
        xcat = jnp.concatenate([prev, cur_ref[...]], axis=0)
        acc = xcat[8:] * w_ref[3:4, :]
        for j in range(3):
            acc = acc + pltpu.roll(xcat, 3 - j, 0)[8:] * w_ref[j:j + 1, :]
        out_ref[...] = acc

    return pl.pallas_call(
        body, grid=(nb,),
        in_specs=[pl.BlockSpec((tb, width), lambda i: (i, 0)),
                  pl.BlockSpec((8, width), lambda i: (jnp.maximum(i * (tb // 8) - 1, 0), 0)),
                  _whole((4, width))],
        out_specs=pl.BlockSpec((tb, width), lambda i: (i, 0)), out_shape=_sds((t_total, width)),
        name="conv_fwd", compiler_params=_cparams(("parallel",)))(proj, proj, conv_w)


def _conv_bwd(dc, proj, conv_w, tb):
    t_total, width = dc.shape
    nb = t_total // tb

    def body(dcur_ref, dnext_ref, cur_ref, prev_ref, w_ref, dx_ref, dw_ref):
        i = pl.program_id(0)
        dnext = jnp.where(i < nb - 1, dnext_ref[...], 0.0)
        dcur = dcur_ref[...]
        dcat = jnp.concatenate([dcur, dnext], axis=0)
        prev = jnp.where(i > 0, prev_ref[...], 0.0)
        xcat = jnp.concatenate([prev, cur_ref[...]], axis=0)

        @pl.when(i == 0)
        def _():
            dw_ref[...] = jnp.zeros(dw_ref.shape, F32)

        dx = dcur * w_ref[3:4, :]
        dw_ref[3:4, :] += jnp.sum(dcur * xcat[8:], axis=0, keepdims=True)
        for j in range(3):
            dx = dx + pltpu.roll(dcat, 8 - (3 - j), 0)[8:] * w_ref[j:j + 1, :]
            dw_ref[j:j + 1, :] += jnp.sum(dcur * pltpu.roll(xcat, 3 - j, 0)[8:], axis=0, keepdims=True)
        dx_ref[...] = dx.astype(dx_ref.dtype)

    return pl.pallas_call(
        body, grid=(nb,),
        in_specs=[pl.BlockSpec((tb, width), lambda i: (i, 0)),
                  pl.BlockSpec((8, width), lambda i: (jnp.minimum((i + 1) * (tb // 8), t_total // 8 - 1), 0)),
                  pl.BlockSpec((tb, width), lambda i: (i, 0)),
                  pl.BlockSpec((8, width), lambda i: (jnp.maximum(i * (tb // 8) - 1, 0), 0)),
                  _whole((4, width))],
        out_specs=[pl.BlockSpec((tb, width), lambda i: (i, 0)), _whole((4, width))],
        out_shape=[_sds((t_total, width), BF16), _sds((4, width))],
        name="conv_bwd", compiler_params=_cparams(("arbitrary",)))(dc, dc, proj, proj, conv_w)


def _head_spec(tb, nb=None):
    if nb is None:
        return pl.BlockSpec((tb, DN_D), lambda h, i: (i, h))
    return pl.BlockSpec((tb, DN_D), lambda h, i: (nb - 1 - i, h))


def _intra_spec(tb, nb=None):
    if nb is None:
        return pl.BlockSpec((None, tb, PAIR), lambda h, i: (h, i, 0))
    return pl.BlockSpec((None, tb, PAIR), lambda h, i: (h, nb - 1 - i, 0))


def _state_spec(tb, nb=None):
    if nb is None:
        return pl.BlockSpec((None, tb // CHUNK, DN_D, DN_D), lambda h, i: (h, i, 0, 0))
    return pl.BlockSpec((None, tb // CHUNK, DN_D, DN_D), lambda h, i: (h, nb - 1 - i, 0, 0))


def _scan_specs(tb, nb=None):
    blk = (lambda i: i) if nb is None else (lambda i: nb - 1 - i)
    rows = pl.BlockSpec((tb, DN_W), lambda i: (blk(i), 0))
    pair = pl.BlockSpec((DN_HEADS, tb, PAIR), lambda i: (0, blk(i), 0))
    states = pl.BlockSpec((DN_HEADS, tb // CHUNK, DN_D, DN_D), lambda i: (0, blk(i), 0, 0))
    return rows, pair, states


def _dn2_fwd(qd, kd, u, w, intra, cd, tb):
    t_total = qd.shape[0]
    rows, pair, states = _scan_specs(tb)

    def body(qd_ref, kd_ref, u_ref, w_ref, a_ref, cd_ref, o_ref, save_ref, state):
        @pl.when(pl.program_id(0) == 0)
        def _():
            state[...] = jnp.zeros(state.shape, F32)

        heads = range(DN_HEADS)
        lanes = [pl.ds(DN_D * h, DN_D) for h in heads]
        for j in range(tb // CHUNK):
            sl = pl.ds(CHUNK * j, CHUNK)
            s0 = tuple(state[h] for h in heads)
            for h in heads:
                save_ref[h, j] = s0[h]
            per_head = lambda ref: tuple(ref[sl, lanes[h]] for h in heads)
            out, s1 = _dn2_step(j % 2, s0, per_head(qd_ref), per_head(kd_ref), per_head(u_ref), per_head(w_ref),
                                tuple(a_ref[h, sl, :] for h in heads),
                                tuple(cd_ref[pl.ds(CHUNK * j, 1), lanes[h]] for h in heads))
            for h in heads:
                o_ref[sl, lanes[h]] = out[h]
                state[h] = s1[h]

    return pl.pallas_call(
        body, grid=(t_total // tb,), in_specs=[rows, rows, rows, rows, pair, rows],
        out_specs=[rows, states],
        out_shape=[_sds((t_total, DN_W)), _sds((DN_HEADS, t_total // CHUNK, DN_D, DN_D))],
        scratch_shapes=[pltpu.VMEM((DN_HEADS, DN_D, DN_D), F32)], name="dn2_fwd",
        compiler_params=_cparams(("arbitrary",)))(qd, kd, u, w, intra, cd)


def _dn2_bwd(qd, kd, u, w, intra, cd, saved, d_o, tb):
    t_total = qd.shape[0]
    nb = t_total // tb
    rows, pair, states = _scan_specs(tb, nb)

    def body(qd_ref, kd_ref, u_ref, w_ref, a_ref, cd_ref, save_ref, do_ref,
             dqd_ref, dkd_ref, du_ref, dw_ref, da_ref, dcd_ref, dstate):
        @pl.when(pl.program_id(0) == 0)
        def _():
            dstate[...] = jnp.zeros(dstate.shape, F32)

        first_row = _iota((CHUNK, DN_D), 0) == 0
        heads = range(DN_HEADS)
        lanes = [pl.ds(DN_D * h, DN_D) for h in heads]
        for j in reversed(range(tb // CHUNK)):
            sl = pl.ds(CHUNK * j, CHUNK)
            per_head = lambda ref: tuple(ref[sl, lanes[h]] for h in heads)
            _, vjp = jax.vjp(functools.partial(_dn2_step, j % 2), tuple(save_ref[h, j] for h in heads),
                             per_head(qd_ref), per_head(kd_ref), per_head(u_ref), per_head(w_ref),
                             tuple(a_ref[h, sl, :] for h in heads),
                             tuple(cd_ref[pl.ds(CHUNK * j, 1), lanes[h]] for h in heads))
            ds0, dqd, dkd, du, dw, da, dcd = vjp((per_head(do_ref), tuple(dstate[h] for h in heads)))
            for h in heads:
                dqd_ref[sl, lanes[h]] = dqd[h]
                dkd_ref[sl, lanes[h]] = dkd[h]
                du_ref[sl, lanes[h]] = du[h]
                dw_ref[sl, lanes[h]] = dw[h]
                da_ref[h, sl, :] = da[h]
                dcd_ref[sl, lanes[h]] = jnp.where(first_row, dcd[h], 0.0)
                dstate[h] = ds0[h]

    full = _sds((t_total, DN_W))
    return pl.pallas_call(
        body, grid=(nb,),
        in_specs=[rows, rows, rows, rows, pair, rows, states, rows],
        out_specs=[rows, rows, rows, rows, pair, rows],
        out_shape=[full, full, full, full, _sds((DN_HEADS, t_total, PAIR)), full],
        scratch_shapes=[pltpu.VMEM((DN_HEADS, DN_D, DN_D), F32)], name="dn2_bwd",
        compiler_params=_cparams(("arbitrary",)))(qd, kd, u, w, intra, cd, saved, d_o)


def _loss_and_grad(y, target, tb):
    t_total, d = y.shape

    def body(y_ref, t_ref, dy_ref, acc_ref):
        @pl.when(pl.program_id(0) == 0)
        def _():
            acc_ref[...] = jnp.zeros(acc_ref.shape, F32)

        err = y_ref[...] - t_ref[...]
        dy_ref[...] = err * (1.0 / d)
        acc_ref[...] += jnp.sum(err * err, axis=0, keepdims=True)

    dy, acc = pl.pallas_call(
        body, grid=(t_total // tb,), in_specs=[_rows(tb, d), _rows(tb, d)],
        out_specs=[_rows(tb, d), _whole((1, d))], out_shape=[_sds((t_total, d)), _sds((1, d))],
        name="loss", compiler_params=_cparams(("arbitrary",)))(y, target)
    return 0.5 * jnp.sum(acc) / d, dy


def _halo_sum(mains, halos, tb):
    t_total, width = mains[0].shape
    nb = t_total // tb
    n = len(mains)

    def body(*refs):
        out_ref = refs[-1]
        i = pl.program_id(0)
        tot = refs[0][...]
        for r in refs[1:n]:
            tot = tot + r[...]
        hal = refs[n][...]
        for r in refs[n + 1:2 * n]:
            hal = hal + r[...]
        hal = jnp.where(i < nb - 1, hal, 0.0)
        out_ref[...] = tot + jnp.concatenate([jnp.zeros((tb - WINDOW, width), F32), hal], axis=0)

    return pl.pallas_call(
        body, grid=(nb,),
        in_specs=[_rows(tb, width)] * n
        + [pl.BlockSpec((None, WINDOW, width), lambda i: (jnp.minimum(i + 1, nb - 1), 0, 0))] * n,
        out_specs=_rows(tb, width), out_shape=_sds((t_total, width)), name="halo_sum",
        compiler_params=_cparams(("parallel",)))(*mains, *halos)


def _adamw(recv, w, m, v, tr, name):
    slots, r_total, c_total = recv.shape
    tr = min(tr, r_total)
    assert r_total % tr == 0
    c1 = 1.0 / (1.0 - ADAM_B1 ** ADAM_STEP)
    c2 = 1.0 / (1.0 - ADAM_B2 ** ADAM_STEP)

    def body(recv_ref, w_ref, m_ref, v_ref, g_ref, d_ref, nm_ref, nv_ref):
        g = recv_ref[0].astype(F32)
        for s in range(1, slots):
            g = g + recv_ref[s].astype(F32)
        nm = ADAM_B1 * m_ref[...] + (1.0 - ADAM_B1) * g
        nv = ADAM_B2 * v_ref[...] + (1.0 - ADAM_B2) * (g * g)
        g_ref[...] = g
        nm_ref[...] = nm
        nv_ref[...] = nv
        d_ref[...] = -ADAM_LR * ((nm * c1) / (jnp.sqrt(nv * c2) + ADAM_EPS) + ADAM_WD * w_ref[...])

    blk = pl.BlockSpec((tr, c_total), lambda i: (i, 0))
    return pl.pallas_call(
        body, grid=(r_total // tr,),
        in_specs=[pl.BlockSpec((slots, tr, c_total), lambda i: (0, i, 0)), blk, blk, blk],
        out_specs=[blk] * 4, out_shape=[_sds((r_total, c_total))] * 4, name=name,
        compiler_params=_cparams(("parallel",)))(recv, w, m, v)


def _me_and_peers():
    x, y, c = lax.axis_index("x"), lax.axis_index("y"), lax.axis_index("c")
    me = 4 * x + 2 * y + c
    peers = []
    for k in range(1, N_DEV):
        px = 1 - x if (k >> 2) & 1 else x
        py = 1 - y if (k >> 1) & 1 else y
        pc = 1 - c if k & 1 else c
        peers.append(((px, py, pc), 4 * px + 2 * py + pc))
    return me, peers


def _small_exchange(packed, reduce):
    r_total = packed.shape[0]

    def body(p_ref, out_ref, gath_ref, send_sems, recv_sems):
        me, peers = _me_and_peers()
        gath_ref[me] = p_ref[...]
        copies = []
        for k, (dev, _) in enumerate(peers):
            cp = pltpu.make_async_remote_copy(src_ref=p_ref, dst_ref=gath_ref.at[me], send_sem=send_sems.at[k],
                                              recv_sem=recv_sems.at[k], device_id=dev,
                                              device_id_type=pl.DeviceIdType.MESH)
            cp.start()
            copies.append(cp)
        for k, (dev, idx) in enumerate(peers):
            pltpu.make_async_remote_copy(src_ref=p_ref, dst_ref=gath_ref.at[idx], send_sem=send_sems.at[k],
                                         recv_sem=recv_sems.at[k], device_id=dev,
                                         device_id_type=pl.DeviceIdType.MESH).wait_recv()
        for cp in copies:
            cp.wait_send()
        if reduce:
            tot = gath_ref[0]
            for d in range(1, N_DEV):
                tot = tot + gath_ref[d]
            out_ref[...] = tot
        else:
            out_ref[...] = gath_ref[...]

    out_shape = _sds((r_total, 128)) if reduce else _sds((N_DEV, r_total, 128))
    return pl.pallas_call(
        body, in_specs=[pl.BlockSpec(memory_space=pltpu.VMEM)], out_specs=pl.BlockSpec(memory_space=pltpu.VMEM),
        out_shape=out_shape,
        scratch_shapes=[pltpu.VMEM((N_DEV, r_total, 128), F32), pltpu.SemaphoreType.DMA((N_DEV - 1,)),
                        pltpu.SemaphoreType.DMA((N_DEV - 1,))],
        name="small_allreduce" if reduce else "small_allgather")(packed)


def _slot(ref, axis, idx, size):
    sel = [slice(None)] * len(ref.shape)
    sel[axis] = idx if size is None else pl.ds(pl.multiple_of(idx * size, size), size)
    return ref.at[tuple(sel)]


def _big_exchange(srcs, dst_shapes, src_view, dst_view, name):
    n = len(srcs)

    def body(*refs):
        src_refs, dst_refs = refs[:n], refs[n:2 * n]
        send_sems, recv_sems, local_sems = refs[2 * n:]
        me, peers = _me_and_peers()
        local, remote = [], []
        for t in range(n):
            loc = pltpu.make_async_copy(src_view(t, src_refs[t], me), dst_view(t, dst_refs[t], me), local_sems.at[t])
            loc.start()
            local.append(loc)
            for k, (dev, idx) in enumerate(peers):
                cp = pltpu.make_async_remote_copy(
                    src_ref=src_view(t, src_refs[t], idx), dst_ref=dst_view(t, dst_refs[t], me),
                    send_sem=send_sems.at[t, k], recv_sem=recv_sems.at[t, k], device_id=dev,
                    device_id_type=pl.DeviceIdType.MESH)
                cp.start()
                remote.append(cp)
        for t in range(n):
            for k, (dev, idx) in enumerate(peers):
                pltpu.make_async_remote_copy(
                    src_ref=src_view(t, src_refs[t], me), dst_ref=dst_view(t, dst_refs[t], idx),
                    send_sem=send_sems.at[t, k], recv_sem=recv_sems.at[t, k], device_id=dev,
                    device_id_type=pl.DeviceIdType.MESH).wait_recv()
        for cp in remote:
            cp.wait_send()
        for cp in local:
            cp.wait()

    any_spec = pl.BlockSpec(memory_space=pl.ANY)
    return pl.pallas_call(
        body, in_specs=[any_spec] * n, out_specs=[any_spec] * n, out_shape=dst_shapes,
        scratch_shapes=[pltpu.SemaphoreType.DMA((n, N_DEV - 1)), pltpu.SemaphoreType.DMA((n, N_DEV - 1)),
                        pltpu.SemaphoreType.DMA((n,))],
        name=name)(*srcs)


BIG = {
    "a_w_in": (1, (2, 1024, A_IN)),
    "b_w_in": (1, (2, 1024, 1024)),
    "w_kv_shared": (0, (1024, 256)),
    "mem_w_kv": (1, (4, 1024, 512)),
    "w_o": (1, (4, 1024, 1024)),
    "mlp_w_up": (2, (4, 1024, 4096)),
    "mlp_w_down": (1, (4, 4096, 1024)),
}
BIG_NAMES = tuple(BIG)


def _allgather_weights(shards):
    dst_shapes, axes, sizes = [], [], []
    for name, s in zip(BIG_NAMES, shards):
        axis, full = BIG[name]
        if name == "mlp_w_up":
            dst_shapes.append(_sds((N_DEV,) + s.shape, BF16))
            axes.append(0)
            sizes.append(None)
        else:
            dst_shapes.append(_sds(full, BF16))
            axes.append(axis)
            sizes.append(s.shape[axis])

    n = len(shards)

    def body(*refs):
        src_refs, dst_refs = refs[:n], refs[n:2 * n]
        send_sems, recv_sems, local_sems = refs[2 * n:]
        x, y, c = lax.axis_index("x"), lax.axis_index("y"), lax.axis_index("c")
        sibling = (x, y, 1 - c)
        chips = [(1 - x, y), (x, 1 - y), (1 - x, 1 - y)]
        index = lambda px, py, pc: 4 * px + 2 * py + pc

        def copy(t, k, block, to, src=None):
            rows = _slot(dst_refs[t], axes[t], index(*block), sizes[t])
            return pltpu.make_async_remote_copy(
                src_ref=rows if src is None else src, dst_ref=rows, send_sem=send_sems.at[t, k],
                recv_sem=recv_sems.at[t, k], device_id=to, device_id_type=pl.DeviceIdType.MESH)

        started, local = [], []
        for t in range(n):
            mine = pltpu.make_async_copy(src_refs[t], _slot(dst_refs[t], axes[t], index(x, y, c), sizes[t]),
                                         local_sems.at[t])
            mine.start()
            local.append(mine)
            first = [copy(t, 0, (x, y, c), sibling, src=src_refs[t])]
            first += [copy(t, 1 + j, (x, y, c), (*chip, c), src=src_refs[t]) for j, chip in enumerate(chips)]
            for cp in first:
                cp.start()
            started += first
        for t in range(n):
            for j, chip in enumerate(chips):
                copy(t, 1 + j, (*chip, c), (x, y, c)).wait_recv()
                passed = copy(t, 4 + j, (*chip, c), sibling)
                passed.start()
                started.append(passed)
        for t in range(n):
            copy(t, 0, sibling, (x, y, c)).wait_recv()
            for j, chip in enumerate(chips):
                copy(t, 4 + j, (*chip, 1 - c), (x, y, c)).wait_recv()
        for cp in started:
            cp.wait_send()
        for cp in local:
            cp.wait()

    any_spec = pl.BlockSpec(memory_space=pl.ANY)
    return pl.pallas_call(
        body, in_specs=[any_spec] * n, out_specs=[any_spec] * n, out_shape=dst_shapes,
        scratch_shapes=[pltpu.SemaphoreType.DMA((n, N_DEV - 1)), pltpu.SemaphoreType.DMA((n, N_DEV - 1)),
                        pltpu.SemaphoreType.DMA((n,))],
        name="allgather_weights")(*shards)


def _scatter_grads(grads):
    dst_shapes, axes, sizes = [], [], []
    for name, g in zip(BIG_NAMES, grads):
        axis, full = BIG[name]
        if name == "mlp_w_up":
            shard = (g.shape[0],) + g.shape[2:]
            axes.append(1)
            sizes.append(None)
        else:
            shard = tuple(d // N_DEV if a == axis else d for a, d in enumerate(full))
            axes.append(axis)
            sizes.append(shard[axis])
        dst_shapes.append(_sds((N_DEV,) + shard, g.dtype))

    def src_view(t, ref, idx):
        return _slot(ref, axes[t], idx, sizes[t])

    def dst_view(t, ref, idx):
        return ref.at[idx]

    return _big_exchange(grads, dst_shapes, src_view, dst_view, "scatter_grads")


def _pad_row(vec, width=128):
    return jnp.pad(vec.astype(F32), (0, width - vec.shape[0])).reshape(1, width)


def _block_sizes(t_total):
    return dict(row=min(256, t_total), dn=min(512, t_total), swa=min(256, t_total), scan=min(256, t_total))


def _ln_apply(h, mix, g, b, tb):
    t_total, d = h.shape
    fwd = lambda pids, *a: _ln_fn(pids, *a) * 2
    return _block_fwd(fwd, [h, mix, g, b], [_rows(tb, d), _rows(tb, d), _whole((1, d)), _whole((1, d))],
                      [_sds((t_total, d)), _sds((t_total, d), BF16)], [_rows(tb, d), _rows(tb, d)],
                      (t_total // tb,), "ln_fwd")


def _ln_grad(h, mix, g, b, dy, tb):
    t_total, d = h.shape
    return _block_bwd(_ln_fn, [h, mix, g, b], [_rows(tb, d), _rows(tb, d), _whole((1, d)), _whole((1, d))],
                      [dy], [_rows(tb, d)], ["s", "s", "a", "a"],
                      [_sds((t_total, d)), _sds((t_total, d), BF16), _sds((1, d)), _sds((1, d))],
                      [_rows(tb, d), _rows(tb, d), _whole((1, d)), _whole((1, d))], (t_total // tb,), "ln_bwd")


def _memattn_specs(tb, qcol):
    return [pl.BlockSpec((tb, MEM_W), lambda i: (i, qcol)), _whole((MEM_W, 2 * MEM_W))]


def _act_epilogue(acc):
    r = jnp.maximum(acc, 0.0)
    return (r * r,)


def _dact_epilogue(acc, act):
    return (acc * (2.0 * jnp.sqrt(act.astype(F32))),)


def _add_epilogue(acc, other):
    return (acc + other,)


def _local_step(x, mem, positions, target, wts, small):
    t_total = x.shape[0]
    bs = _block_sizes(t_total)
    tb, tdn, tsw = bs["row"], bs["dn"], bs["swa"]
    nb = t_total // tb
    nbs = t_total // tsw

    inv_freq = ROPE_THETA ** (-jnp.arange(0, SWA_DH, 2, dtype=F32) / SWA_DH)
    ang = positions.astype(F32)[:, None] * inv_freq
    cos = jnp.tile(jnp.cos(ang), (1, 4))
    sin = jnp.tile(jnp.sin(ang), (1, 4))

    w_a = [jnp.concatenate([wts["a_w_in"][l][:, :3072], wts["a_w_in"][l][:, 3084:], wts["a_w_in"][l][:, 3072:3084],
                            jnp.zeros((D_MODEL, A_IN_PAD - A_IN), BF16)], axis=1) for l in range(N_A)]
    wkv = wts["w_kv_shared"]
    w_kvd = jnp.concatenate([wkv[:, 64 * (i // 2):64 * (i // 2 + 1)] for i in range(8)], axis=1)
    mem_b = mem.astype(BF16)
    up_spec_nn = lambda l: pl.BlockSpec((None, None, D_MODEL, 512), lambda i, j, k, l=l: (j, l, k, 0))
    up_spec_nt = lambda l: pl.BlockSpec((None, None, D_MODEL, 512), lambda i, j, k, l=l: (k, l, j, 0))

    saved = []
    h, hb = x, x.astype(BF16)
    kr = vd_src = None
    for l in range(DEPTH):
        sv = dict(h=h, hb=hb)
        kvm = _matmul(mem_b, wts["mem_w_kv"][l], "nn", [F32], "mm_memkv", tm=256)
        if l < N_A:
            proj = _matmul(hb, w_a[l], "nn", [F32], "mm_proj_a", tn=1152)
            conv_w = small["a_conv_w"][l]
            c = _conv_fwd(proj, conv_w, tb)
            alog, dtb = _pad_row(small["a_A_log"][l]), _pad_row(small["a_dt_bias"][l])
            rowa_in = [c, proj, alog, dtb]
            rowa_specs = [_rows(tb, 3 * DN_W), _rows(tb, 128, 26), _whole((1, 128)), _whole((1, 128))]
            q, k, v, gcb, betab = _block_fwd(_rowa_fn, rowa_in, rowa_specs, [_sds((t_total, DN_W))] * 5,
                                             [_rows(tb, DN_W)] * 5, (nb,), "rowa_fwd")
            hs = _head_spec(tdn)
            dn_grid = (DN_HEADS, t_total // tdn)
            full = _sds((t_total, DN_W))
            pair_sds = _sds((DN_HEADS, t_total, PAIR))
            dn1_out_shapes = [full, full, pair_sds, full, full, full, pair_sds]
            dn1_out_specs = [hs, hs, _intra_spec(tdn), hs, hs, hs, _intra_spec(tdn)]
            u, w, intra, qd, kd, cd, tinv = _block_fwd(_dn1_fn, [q, k, v, gcb, betab], [hs] * 5, dn1_out_shapes,
                                                       dn1_out_specs, dn_grid, "dn1_fwd")
            o, states = _dn2_fwd(qd, kd, u, w, intra, cd, bs["scan"])
            nw = small["a_norm_w"][l].reshape(1, DN_D)
            post_in = [o, proj, nw]
            post_specs = [_rows(tb, DN_W), _rows(tb, DN_W, 3), _whole((1, DN_D))]
            (og,) = _block_fwd(_post_fn, post_in, post_specs, [_sds((t_total, DN_W), BF16)], [_rows(tb, DN_W)],
                               (nb,), "post_fwd")
            qm_col = 12
            sv.update(proj=proj, c=c, rowa_in=rowa_in, rowa_specs=rowa_specs, dn1_in=[q, k, v, gcb, betab, tinv],
                      dn2_in=[qd, kd, u, w, intra, cd], states=states, post_in=post_in, post_specs=post_specs,
                      conv_w=conv_w)
        else:
            jb = l - N_A
            proj = _matmul(hb, wts["b_w_in"][jb], "nn", [F32], "mm_proj_b")
            sinks = _pad_row(small["b_sinks"][jb])
            swa_in = [proj, cos, sin, kr, kr, vd_src, vd_src, sinks]
            swa_specs = [_rows(tsw, DN_W), _rows(tsw, 128), _rows(tsw, 128),
                         pl.BlockSpec((WINDOW, 256), lambda i: (jnp.maximum(i * (tsw // WINDOW) - 1, 0), 0)),
                         _rows(tsw, 256),
                         pl.BlockSpec((WINDOW, 256), lambda i: (jnp.maximum(i * (tsw // WINDOW) - 1, 0), 1)),
                         _rows(tsw, 256, 1), _whole((1, 128))]
            (og,) = _block_fwd(_swa_fn, swa_in, swa_specs, [_sds((t_total, DN_W), BF16)], [_rows(tsw, DN_W)],
                               (nbs,), "swa_fwd")
            qm_col = 3
            sv.update(proj=proj, swa_in=swa_in, swa_specs=swa_specs)
        mem_in = [proj, kvm]
        (mo,) = _block_fwd(_memattn_fn, mem_in, _memattn_specs(tb, qm_col), [_sds((t_total, MEM_W), BF16)],
                           [_rows(tb, MEM_W)], (nb,), "memattn_fwd")
        mixin = jnp.concatenate([og, mo], axis=1)
        mix = _matmul(mixin, wts["w_o"][l], "nn", [F32], "mm_wo")
        g0, b0 = small["ln_g"][l, 0].reshape(1, -1), small["ln_b"][l, 0].reshape(1, -1)
        h1, h1b = _ln_apply(h, mix, g0, b0, tb)
        act = _matmul(h1b, wts["mlp_w_up"], "nn", [BF16], "mm_up", epi=_act_epilogue,
                      b_spec=up_spec_nn(l), n_total=4 * D_MODEL, tn=512)
        mlp = _matmul(act, wts["mlp_w_down"][l], "nn", [F32], "mm_down")
        g1, b1 = small["ln_g"][l, 1].reshape(1, -1), small["ln_b"][l, 1].reshape(1, -1)
        h2, h2b = _ln_apply(h1, mlp, g1, b1, tb)
        sv.update(kvm=kvm, mem_in=mem_in, qm_col=qm_col, mixin=mixin, mix=mix, ln0=(g0, b0), h1=h1, h1b=h1b,
                  act=act, mlp=mlp, ln1=(g1, b1))
        saved.append(sv)
        h, hb = h2, h2b
        if l == N_A - 1:
            kvd = _matmul(hb, w_kvd, "nn", [F32], "mm_kvd")
            krope_in = [kvd, cos, sin]
            krope_specs = [_rows(tb, 256), _rows(tb, 128), _rows(tb, 128)]
            (kr,) = _block_fwd(_krope_fn, krope_in, krope_specs, [_sds((t_total, 256))], [_rows(tb, 256)], (nb,),
                               "krope_fwd")
            vd_src = kvd

    loss, dh = _loss_and_grad(h, target, tb)

    grads = {n: [None] * BIG[n][1][0] for n in BIG_NAMES if n != "w_kv_shared"}
    sg = dict(a_conv_w=[None] * N_A, a_A_log=[None] * N_A, a_dt_bias=[None] * N_A, a_norm_w=[None] * N_A,
              b_sinks=[None] * (DEPTH - N_A), ln_g=[[None, None] for _ in range(DEPTH)],
              ln_b=[[None, None] for _ in range(DEPTH)])
    dk_parts, dv_parts = [], []
    for l in reversed(range(DEPTH)):
        sv = saved[l]
        if l == N_A - 1:
            dkr = _halo_sum([p[0] for p in dk_parts], [p[1] for p in dk_parts], tsw)
            dvv = _halo_sum([p[0] for p in dv_parts], [p[1] for p in dv_parts], tsw)
            (dkraw,) = _block_bwd(_krope_fn, krope_in, krope_specs, [dkr], [_rows(tb, 256)], ["s", None, None],
                                  [_sds((t_total, 256), BF16)], [_rows(tb, 256)], (nb,), "krope_bwd")
            dkvd = jnp.concatenate([dkraw, dvv.astype(BF16)], axis=1)
            g_kvd = _matmul(saved[l + 1]["hb"], dkvd, "tn", [F32], "mm_dw_kvd", tm=1024, tn=512)
            dh = _matmul(dkvd, w_kvd, "nt", [F32], "mm_dx_kvd", epi=_add_epilogue, extras=[dh], tn=1024, tk=512)
            grads["w_kv_shared"] = jnp.concatenate(
                [g_kvd[:, 128 * i:128 * i + 64] + g_kvd[:, 128 * i + 64:128 * (i + 1)] for i in range(4)],
                axis=1).astype(BF16)
        g1, b1 = sv["ln1"]
        dh1a, dmlp, dg1, db1 = _ln_grad(sv["h1"], sv["mlp"], g1, b1, dh, tb)
        dup = _matmul(dmlp, wts["mlp_w_down"][l], "nt", [BF16], "mm_dact", epi=_dact_epilogue, extras=[sv["act"]])
        grads["mlp_w_down"][l] = _matmul(sv["act"], dmlp, "tn", [BF16], "mm_dw_down")
        g_up = _matmul(sv["h1b"], dup, "tn", [BF16], "mm_dw_up", tn=512, tk=2048,
                       out_shapes=[_sds((N_DEV, D_MODEL, 512), BF16)],
                       out_specs=[pl.BlockSpec((None, 1024, 512), lambda i, j, k: (j, i, 0))])
        grads["mlp_w_up"][l] = g_up
        dh1 = _matmul(dup, wts["mlp_w_up"], "nt", [F32], "mm_dx_up", epi=_add_epilogue, extras=[dh1a],
                      b_spec=up_spec_nt(l), n_total=D_MODEL, tn=1024, tk=512)
        g0, b0 = sv["ln0"]
        dha, dmix, dg0, db0 = _ln_grad(sv["h"], sv["mix"], g0, b0, dh1, tb)
        sg["ln_g"][l] = [dg0, dg1]
        sg["ln_b"][l] = [db0, db1]
        grads["w_o"][l] = _matmul(sv["mixin"], dmix, "tn", [BF16], "mm_dw_o")
        dmixin = _matmul(dmix, wts["w_o"][l], "nt", [F32], "mm_dx_o", tn=1024)
        dqm, dkvm = _block_bwd(_memattn_fn, sv["mem_in"], _memattn_specs(tb, sv["qm_col"]), [dmixin],
                               [_rows(tb, MEM_W, 3)], ["s", "a"],
                               [_sds((t_total, MEM_W), BF16), _sds((MEM_W, 2 * MEM_W))],
                               [_rows(tb, MEM_W), _whole((MEM_W, 2 * MEM_W))], (nb,), "memattn_bwd")
        grads["mem_w_kv"][l] = _matmul(mem_b, dkvm.astype(BF16), "tn", [BF16], "mm_dw_memkv", tm=1024, tn=512)
        if l < N_A:
            d_o, dz, dnw = _block_bwd(_post_fn, sv["post_in"], sv["post_specs"], [dmixin], [_rows(tb, DN_W)],
                                      ["s", "s", "a"],
                                      [_sds((t_total, DN_W)), _sds((t_total, DN_W), BF16), _sds((1, DN_D))],
                                      [_rows(tb, DN_W), _rows(tb, DN_W), _whole((1, DN_D))], (nb,), "post_bwd")
            sg["a_norm_w"][l] = dnw
            dqd, dkd, du, dw, da, dcd = _dn2_bwd(*sv["dn2_in"], sv["states"], d_o, bs["scan"])
            hs = _head_spec(tdn)
            full = _sds((t_total, DN_W))
            dq, dk, dv, dgc, dbeta = _block_bwd(
                _dn1_fn_known, sv["dn1_in"], [hs] * 5 + [_intra_spec(tdn)], [du, dw, da, dqd, dkd, dcd],
                [hs, hs, _intra_spec(tdn), hs, hs, hs], ["s"] * 5 + [None], [full] * 5, [hs] * 5,
                (DN_HEADS, t_total // tdn), "dn1_bwd")
            dc, dab, dalog, ddtb = _block_bwd(
                _rowa_fn, sv["rowa_in"], sv["rowa_specs"], [dq, dk, dv, dgc, dbeta], [_rows(tb, DN_W)] * 5,
                ["s", "s", "a", "a"],
                [_sds((t_total, 3 * DN_W)), _sds((t_total, 128), BF16), _sds((1, 128)), _sds((1, 128))],
                [_rows(tb, 3 * DN_W), _rows(tb, 128), _whole((1, 128)), _whole((1, 128))], (nb,), "rowa_bwd")
            sg["a_A_log"][l] = dalog[0, :DN_HEADS]
            sg["a_dt_bias"][l] = ddtb[0, :DN_HEADS]
            dx, dconv = _conv_bwd(dc, sv["proj"], sv["conv_w"], tb)
            sg["a_conv_w"][l] = dconv
            dproj = jnp.concatenate([dx, dz, dqm, dab], axis=1)
            g_in = _matmul(sv["hb"], dproj, "tn", [BF16], "mm_dw_a", tm=1024, tn=1152)
            grads["a_w_in"][l] = jnp.concatenate([g_in[:, :3072], g_in[:, 3328:3340], g_in[:, 3072:3328]], axis=1)
            dh = _matmul(dproj, w_a[l], "nt", [F32], "mm_dx_a", epi=_add_epilogue, extras=[dha], tk=1152)
        else:
            jb = l - N_A
            swa_kinds = ["s", None, None, "s", "s", "s", "s", "a"]
            halo_spec = pl.BlockSpec((None, WINDOW, 256), lambda i: (i, 0, 0))
            dq, dkh, dkc, dvh, dvc, dsink = _block_bwd(
                _swa_fn, sv["swa_in"], sv["swa_specs"], [dmixin], [_rows(tsw, DN_W)], swa_kinds,
                [_sds((t_total, DN_W), BF16), _sds((nbs, WINDOW, 256)), _sds((t_total, 256)),
                 _sds((nbs, WINDOW, 256)), _sds((t_total, 256)), _sds((1, 128))],
                [_rows(tsw, DN_W), halo_spec, _rows(tsw, 256), halo_spec, _rows(tsw, 256), _whole((1, 128))],
                (nbs,), "swa_bwd")
            sg["b_sinks"][jb] = dsink[0, :SWA_HEADS]
            dk_parts.append((dkc, dkh))
            dv_parts.append((dvc, dvh))
            dproj = jnp.concatenate([dq, dqm], axis=1)
            grads["b_w_in"][jb] = _matmul(sv["hb"], dproj, "tn", [BF16], "mm_dw_b")
            dh = _matmul(dproj, wts["b_w_in"][jb], "nt", [F32], "mm_dx_b", epi=_add_epilogue, extras=[dha], tn=1024)

    big = []
    for n in BIG_NAMES:
        if n == "w_kv_shared":
            big.append(grads[n])
        else:
            big.append(jnp.stack(grads[n], axis=0))
    small_grads = dict(
        a_conv_w=jnp.stack(sg["a_conv_w"]), a_A_log=jnp.stack(sg["a_A_log"]), a_dt_bias=jnp.stack(sg["a_dt_bias"]),
        a_norm_w=jnp.concatenate(sg["a_norm_w"], axis=0), b_sinks=jnp.stack(sg["b_sinks"]),
        ln_g=jnp.stack([jnp.concatenate(p, axis=0) for p in sg["ln_g"]]),
        ln_b=jnp.stack([jnp.concatenate(p, axis=0) for p in sg["ln_b"]]))
    return loss, dh, big, small_grads


def _pack(arrays, rows):
    flat = []
    for a in arrays:
        v = a.astype(F32).reshape(-1)
        flat.append(jnp.pad(v, (0, (-v.shape[0]) % 128)))
    flat = jnp.concatenate(flat)
    return jnp.pad(flat, (0, rows * 128 - flat.shape[0])).reshape(rows, 128)


def _unpack(slab, shapes):
    flat = slab.reshape(slab.shape[:-2] + (-1,))
    out, off = [], 0
    for s in shapes:
        n = math.prod(s)
        out.append(flat[..., off:off + n].reshape(slab.shape[:-2] + tuple(s)))
        off += n + (-n) % 128
    return out


def _rows_for(shapes):
    rows = sum((math.prod(s) + 127) // 128 for s in shapes)
    return rows + (-rows) % 8


SMALL_NAMES = ("a_conv_w", "a_A_log", "a_dt_bias", "a_norm_w", "b_sinks", "ln_g", "ln_b")
SMALL_SHARDED = {"a_conv_w": 2, "ln_g": 2, "ln_b": 2}
SMALL_FULL = {"a_conv_w": (2, 4, 2304), "a_A_log": (2, 6), "a_dt_bias": (2, 6), "a_norm_w": (2, 128),
              "b_sinks": (2, 12), "ln_g": (4, 2, 1024), "ln_b": (4, 2, 1024)}


def kernel(x, mem, positions, a_w_in, a_conv_w, a_A_log, a_dt_bias, a_norm_w, b_w_in, b_sinks, w_kv_shared, mem_w_kv, w_o, mlp_w_up, mlp_w_down, ln_g, ln_b, loss_target, m_a_w_in, m_a_conv_w, m_a_A_log, m_a_dt_bias, m_a_norm_w, m_b_w_in, m_b_sinks, m_w_kv_shared, m_mem_w_kv, m_w_o, m_mlp_w_up, m_mlp_w_down, m_ln_g, m_ln_b, v_a_w_in, v_a_conv_w, v_a_A_log, v_a_dt_bias, v_a_norm_w, v_b_w_in, v_b_sinks, v_w_kv_shared, v_mem_w_kv, v_w_o, v_mlp_w_up, v_mlp_w_down, v_ln_g, v_ln_b):
    params = dict(a_w_in=a_w_in, a_conv_w=a_conv_w, a_A_log=a_A_log, a_dt_bias=a_dt_bias, a_norm_w=a_norm_w,
                  b_w_in=b_w_in, b_sinks=b_sinks, w_kv_shared=w_kv_shared, mem_w_kv=mem_w_kv, w_o=w_o,
                  mlp_w_up=mlp_w_up, mlp_w_down=mlp_w_down, ln_g=ln_g, ln_b=ln_b)
    mom = dict(a_w_in=m_a_w_in, a_conv_w=m_a_conv_w, a_A_log=m_a_A_log, a_dt_bias=m_a_dt_bias, a_norm_w=m_a_norm_w,
               b_w_in=m_b_w_in, b_sinks=m_b_sinks, w_kv_shared=m_w_kv_shared, mem_w_kv=m_mem_w_kv, w_o=m_w_o,
               mlp_w_up=m_mlp_w_up, mlp_w_down=m_mlp_w_down, ln_g=m_ln_g, ln_b=m_ln_b)
    var = dict(a_w_in=v_a_w_in, a_conv_w=v_a_conv_w, a_A_log=v_a_A_log, a_dt_bias=v_a_dt_bias, a_norm_w=v_a_norm_w,
               b_w_in=v_b_w_in, b_sinks=v_b_sinks, w_kv_shared=v_w_kv_shared, mem_w_kv=v_mem_w_kv, w_o=v_w_o,
               mlp_w_up=v_mlp_w_up, mlp_w_down=v_mlp_w_down, ln_g=v_ln_g, ln_b=v_ln_b)
    me = 4 * lax.axis_index("x") + 2 * lax.axis_index("y") + lax.axis_index("c")

    full_b = _allgather_weights([params[n].astype(BF16) for n in BIG_NAMES])
    wts = dict(zip(BIG_NAMES, full_b))
    sharded_names = [n for n in SMALL_NAMES if n in SMALL_SHARDED]
    shard_shapes = [params[n].shape for n in sharded_names]
    gathered = _small_exchange(_pack([params[n] for n in sharded_names], _rows_for(shard_shapes)), reduce=False)
    small = {n: params[n] for n in SMALL_NAMES if n not in SMALL_SHARDED}
    for n, g in zip(sharded_names, _unpack(gathered, shard_shapes)):
        small[n] = jnp.moveaxis(g, 0, 2).reshape(SMALL_FULL[n])

    loss, dx, big_grads, small_grads = _local_step(x[0], mem[0], positions[0], loss_target[0], wts, small)
    loss = lax.psum(loss, ("x", "y", "c"))
    recv = _scatter_grads(big_grads)
    out = {}
    for n, r in zip(BIG_NAMES, recv):
        shp = params[n].shape
        rows = math.prod(shp[:-1])
        res = _adamw(r.reshape(N_DEV, rows, shp[-1]), params[n].reshape(rows, shp[-1]),
                     mom[n].reshape(rows, shp[-1]), var[n].reshape(rows, shp[-1]), 32, "adamw_" + n)
        out[n] = [t.reshape(shp) for t in res]
    full_shapes = [SMALL_FULL[n] for n in SMALL_NAMES]
    summed = _small_exchange(_pack([small_grads[n] for n in SMALL_NAMES], _rows_for(full_shapes)), reduce=True)
    local_g = []
    for n, g in zip(SMALL_NAMES, _unpack(summed, full_shapes)):
        if n in SMALL_SHARDED:
            size = params[n].shape[2]
            g = lax.dynamic_slice_in_dim(g, me * size, size, axis=2)
        local_g.append(g)
    local_shapes = [params[n].shape for n in SMALL_NAMES]
    rows = _rows_for(local_shapes)
    res = _adamw(_pack(local_g, rows)[None], _pack([params[n] for n in SMALL_NAMES], rows),
                 _pack([mom[n] for n in SMALL_NAMES], rows), _pack([var[n] for n in SMALL_NAMES], rows), rows,
                 "adamw_small")
    unpacked = [_unpack(t, local_shapes) for t in res]
    for i, n in enumerate(SMALL_NAMES):
        out[n] = [unpacked[k][i] for k in range(4)]

    order = ("a_w_in", "a_conv_w", "a_A_log", "a_dt_bias", "a_norm_w", "b_w_in", "b_sinks", "w_kv_shared",
             "mem_w_kv", "w_o", "mlp_w_up", "mlp_w_down", "ln_g", "ln_b")
    return (loss, dx[None], *[out[n][0] for n in order], *[out[n][1] for n in order],
            *[out[n][2] for n in order], *[out[n][3] for n in order])
```

```python
import functools
import math

import jax
import jax.numpy as jnp
from jax import lax
from jax.experimental import pallas as pl
from jax.experimental.pallas import tpu as pltpu

F32 = jnp.float32
BF16 = jnp.bfloat16

D_MODEL = 1024
DEPTH = 4
N_A = 2
MEM_HEADS = 4
MEM_DH = 64
MEM_W = 256
DN_HEADS = 6
DN_D = 128
DN_W = 768
CHUNK = 64
SWA_DH = 64
SWA_HEADS = 12
WINDOW = 128
ROPE_THETA = 10000.0
LN_EPS = 1e-5
NORM_EPS = 1e-6
DN_ALPHA = (2.0 * DEPTH) ** 0.25
A_IN = 3340
A_IN_PAD = 3456
N_DEV = 8

ADAM_LR = 0.001
ADAM_B1 = 0.9
ADAM_B2 = 0.999
ADAM_EPS = 1e-08
ADAM_WD = 0.01
ADAM_STEP = 10

VMEM_LIMIT = 52 * 1024 * 1024
NEG_BIG = -1e30


def _cparams(sem):
    return pltpu.CompilerParams(dimension_semantics=sem, vmem_limit_bytes=VMEM_LIMIT)


_CONTRACT = {"nn": (1, 0), "nt": (1, 1), "tn": (0, 0)}


def _raw_mm(a, b, mode, prec):
    ca, cb = _CONTRACT[mode]
    dims = (((ca,), (cb,)), ((), ()))
    dot = lambda p, q: lax.dot_general(p, q, dims, preferred_element_type=F32)
    if prec == "bf16":
        return dot(a.astype(BF16), b.astype(BF16))
    a, b = a.astype(F32), b.astype(F32)
    a_hi, b_hi = a.astype(BF16), b.astype(BF16)
    if prec == "sela":
        return dot(a_hi, b_hi) + dot(a_hi, (b - b_hi.astype(F32)).astype(BF16))
    a_lo = (a - a_hi.astype(F32)).astype(BF16)
    if prec == "selb":
        return dot(a_hi, b_hi) + dot(a_lo, b_hi)
    b_lo = (b - b_hi.astype(F32)).astype(BF16)
    return dot(a_hi, b_hi) + (dot(a_hi, b_lo) + dot(a_lo, b_hi))


@functools.partial(jax.custom_vjp, nondiff_argnums=(2, 3))
def mm(a, b, mode, prec):
    return _raw_mm(a, b, mode, prec)


def _mm_fwd(a, b, mode, prec):
    return _raw_mm(a, b, mode, prec), (a, b)


def _mm_bwd(mode, prec, res, ct):
    a, b = res
    if prec == "sela":
        pa, pb = "f32", {"nn": "sela", "nt": "selb", "tn": "sela"}[mode]
    elif prec == "selb":
        pa, pb = {"nn": "selb", "nt": "selb", "tn": "sela"}[mode], "f32"
    else:
        pa = pb = prec
    if mode == "nn":
        return mm(ct, b, "nt", pa), mm(a, ct, "tn", pb)
    if mode == "nt":
        return mm(ct, b, "nn", pa), mm(ct, a, "tn", pb)
    return mm(b, ct, "nt", pa), mm(a, ct, "nn", pb)


mm.defvjp(_mm_fwd, _mm_bwd)


@jax.custom_vjp
def _softplus(x):
    y = jnp.exp(-jnp.abs(x))
    log1p_y = jnp.where(y < 1e-2, y * (1.0 - y * (0.5 - y * (1.0 / 3.0))), jnp.log(1.0 + y))
    return jnp.maximum(x, 0.0) + log1p_y


def _softplus_fwd(x):
    return _softplus(x), x


def _softplus_bwd(x, ct):
    return (ct * jax.nn.sigmoid(x),)


_softplus.defvjp(_softplus_fwd, _softplus_bwd)


def _iota(shape, dim):
    return lax.broadcasted_iota(jnp.int32, shape, dim)


def _block_fwd(fn, ins, in_specs, out_shapes, out_specs, grid, name):
    n_in = len(ins)

    def body(*refs):
        pids = tuple(pl.program_id(a) for a in range(len(grid)))
        vals = [r[...].astype(F32) for r in refs[:n_in]]
        outs = fn(pids, *vals)
        for r, o in zip(refs[n_in:], outs):
            r[...] = o.astype(r.dtype)

    return pl.pallas_call(
        body, grid=grid, in_specs=in_specs, out_specs=out_specs, out_shape=out_shapes, name=name,
        compiler_params=_cparams(("parallel",) * len(grid)))(*ins)


def _block_bwd(fn, ins, in_specs, cts, ct_specs, kinds, g_shapes, g_specs, grid, name):
    n_in, n_ct = len(ins), len(cts)
    didx = [i for i, k in enumerate(kinds) if k]

    def body(*refs):
        in_refs, ct_refs, g_refs = refs[:n_in], refs[n_in:n_in + n_ct], refs[n_in + n_ct:]
        pids = tuple(pl.program_id(a) for a in range(len(grid)))
        vals = [r[...].astype(F32) for r in in_refs]

        def f(*dvals):
            full = list(vals)
            for i, v in zip(didx, dvals):
                full[i] = v
            return tuple(fn(pids, *full))

        _, vjp = jax.vjp(f, *[vals[i] for i in didx])
        gs = vjp(tuple(r[...].astype(F32) for r in ct_refs))
        first = pids[0] == 0
        for p in pids[1:]:
            first = jnp.logical_and(first, p == 0)
        for i, g, r in zip(didx, gs, g_refs):
            if kinds[i] == "s":
                r[...] = g.astype(r.dtype)
            else:
                @pl.when(first)
                def _(r=r):
                    r[...] = jnp.zeros(r.shape, r.dtype)

                r[...] += g.astype(r.dtype)

    sem = ("arbitrary",) * len(grid) if "a" in kinds else ("parallel",) * len(grid)
    return pl.pallas_call(
        body, grid=grid, in_specs=list(in_specs) + list(ct_specs), out_specs=g_specs, out_shape=g_shapes,
        name=name, compiler_params=_cparams(sem))(*ins, *cts)


def _rows(tb, width, col=0):
    return pl.BlockSpec((tb, width), lambda i, col=col: (i, col))


def _whole(shape):
    return pl.BlockSpec(shape, lambda *_: (0,) * len(shape))


def _sds(shape, dtype=F32):
    return jax.ShapeDtypeStruct(shape, dtype)


def _matmul(a, b, mode, out_dtypes, name, epi=None, extras=(), tm=1024, tn=1024, tk=1024,
            b_spec=None, n_total=None, out_specs=None, out_shapes=None):
    if mode == "tn":
        k_total, m_total = a.shape
    else:
        m_total, k_total = a.shape
    if n_total is None:
        n_total = b.shape[0] if mode == "nt" else b.shape[1]
    tm, tn, tk = min(tm, m_total), min(tn, n_total), min(tk, k_total)
    assert m_total % tm == 0 and n_total % tn == 0 and k_total % tk == 0, (name, a.shape, b.shape)
    grid = (m_total // tm, n_total // tn, k_total // tk)
    nk = grid[2]
    if mode == "tn":
        a_spec = pl.BlockSpec((tk, tm), lambda i, j, k: (k, i))
    else:
        a_spec = pl.BlockSpec((tm, tk), lambda i, j, k: (i, k))
    if b_spec is None:
        if mode == "nt":
            b_spec = pl.BlockSpec((tn, tk), lambda i, j, k: (j, k))
        else:
            b_spec = pl.BlockSpec((tk, tn), lambda i, j, k: (k, j))
    tile = pl.BlockSpec((tm, tn), lambda i, j, k: (i, j))
    n_ex, n_out = len(extras), len(out_dtypes)
    ca, cb = _CONTRACT[mode]
    dims = (((ca,), (cb,)), ((), ()))

    def body(*refs):
        a_ref, b_ref = refs[:2]
        ex_refs = refs[2:2 + n_ex]
        out_refs = refs[2 + n_ex:2 + n_ex + n_out]
        part = lax.dot_general(a_ref[...], b_ref[...], dims, preferred_element_type=F32)

        def finish(val):
            res = epi(val, *[e[...] for e in ex_refs]) if epi is not None else (val,)
            for r, o in zip(out_refs, res):
                r[...] = o.astype(r.dtype)

        if nk == 1:
            finish(part)
        else:
            acc = refs[-1]
            k = pl.program_id(2)

            @pl.when(k == 0)
            def _():
                acc[...] = part

            @pl.when(k > 0)
            def _():
                acc[...] += part

            @pl.when(k == nk - 1)
            def _():
                finish(acc[...])

    if out_shapes is None:
        out_shapes = [_sds((m_total, n_total), d) for d in out_dtypes]
        out_specs = [tile] * n_out
    outs = pl.pallas_call(
        body, grid=grid, in_specs=[a_spec, b_spec] + [tile] * n_ex, out_specs=out_specs, out_shape=out_shapes,
        scratch_shapes=[pltpu.VMEM((tm, tn), F32)] if nk > 1 else [], name=name,
        compiler_params=_cparams(("parallel", "parallel", "arbitrary")))(a, b, *extras)
    return outs if n_out > 1 else outs[0]


def _silu(x):
    return x * jax.nn.sigmoid(x)


def _rowa_fn(pids, c, ab, alog, dtb):
    tb = c.shape[0]
    s = _silu(c)
    qs, ks = [], []
    for h in range(DN_HEADS):
        qh = s[:, DN_D * h:DN_D * (h + 1)]
        qs.append(qh * lax.rsqrt(jnp.sum(qh * qh, axis=-1, keepdims=True) + NORM_EPS) * (DN_D ** -0.5))
        kh = s[:, DN_W + DN_D * h:DN_W + DN_D * (h + 1)]
        ks.append(kh * lax.rsqrt(jnp.sum(kh * kh, axis=-1, keepdims=True) + NORM_EPS))
    q = jnp.concatenate(qs, axis=1)
    k = jnp.concatenate(ks, axis=1)
    v = s[:, 2 * DN_W:3 * DN_W]
    g128 = -jnp.exp(alog) * _softplus(ab + dtb)
    b128 = jax.nn.sigmoid(ab)
    r, cc = _iota((tb, tb), 0), _iota((tb, tb), 1)
    tri = jnp.where(((r >> 6) == (cc >> 6)) & (r >= cc), 1.0, 0.0)
    gc128 = mm(tri, g128, "nn", "sela")
    lane, col = _iota((128, DN_W), 0), _iota((128, DN_W), 1)
    exp_a = jnp.where(lane == (col >> 7), 1.0, 0.0)
    exp_b = jnp.where(lane == (col >> 7) + DN_HEADS, 1.0, 0.0)
    return q, k, v, mm(gc128, exp_a, "nn", "selb"), mm(b128, exp_b, "nn", "selb")


def _tri_inv_raw(lows, block):
    n = lows[0].shape[0]
    r, c = _iota((n, n), 0), _iota((n, n), 1)
    lg = 0
    xs = None
    while (1 << lg) < block:
        off = ((r >> (lg + 1)) == (c >> (lg + 1))) & (((r >> lg) & 1) == 1) & (((c >> lg) & 1) == 0)
        cblks = [jnp.where(off, low, 0.0) for low in lows]
        if xs is None:
            xs = [jnp.where(r == c, 1.0, 0.0) - cb for cb in cblks]
        else:
            ys = [mm(cb, x, "nn", "f32") for cb, x in zip(cblks, xs)]
            xs = [x - mm(x, y, "nn", "f32") for x, y in zip(xs, ys)]
        lg += 1
    return tuple(xs)


def _tri_inv_cotangent(block, xs, cts):
    n = xs[0].shape[0]
    r, c = _iota((n, n), 0), _iota((n, n), 1)
    shift = block.bit_length() - 1
    keep = ((r >> shift) == (c >> shift)) & (r > c)
    gs = [mm(x, ct, "tn", "f32") for x, ct in zip(xs, cts)]
    gs = [mm(g, x, "nt", "f32") for g, x in zip(gs, xs)]
    return tuple(jnp.where(keep, -g, 0.0) for g in gs)


@functools.partial(jax.custom_vjp, nondiff_argnums=(1,))
def _tri_inv(lows, block):
    return _tri_inv_raw(lows, block)


def _tri_inv_fwd(lows, block):
    xs = _tri_inv_raw(lows, block)
    return xs, xs


def _tri_inv_bwd(block, xs, cts):
    return (_tri_inv_cotangent(block, xs, cts),)


_tri_inv.defvjp(_tri_inv_fwd, _tri_inv_bwd)


@functools.partial(jax.custom_vjp, nondiff_argnums=(2,))
def _tri_inv_known(lows, known, block):
    return known


def _tri_inv_known_fwd(lows, known, block):
    return known, known


def _tri_inv_known_bwd(block, xs, cts):
    return _tri_inv_cotangent(block, xs, cts), tuple(jnp.zeros_like(x) for x in xs)


_tri_inv_known.defvjp(_tri_inv_known_fwd, _tri_inv_known_bwd)


PAIR = 2 * CHUNK


def _dn1_pairs(q, k, v, gc, beta, tinv_known=None):
    assert PAIR == DN_D
    n = PAIR
    pairs = range(q.shape[0] // n)
    cut = lambda t: [t[n * j:n * (j + 1)] for j in pairs]
    q, k, v, gc, beta = cut(q), cut(k), cut(v), cut(gc), cut(beta)
    onehot = jnp.where(_iota((n, DN_D), 1) == 0, 1.0, 0.0)
    r, c = _iota((n, n), 0), _iota((n, n), 1)
    same = (r >> 6) == (c >> 6)
    incl, strict = same & (r >= c), same & (r > c)
    row = _iota((n, DN_D), 0)
    eg = [jnp.exp(g) for g in gc]
    kb = [k[j] * beta[j] for j in pairs]
    g_row = [mm(onehot, g, "nt", "sela") for g in gc]
    kk = [mm(kb[j], k[j], "nt", "bf16") for j in pairs]
    qk = [mm(q[j], k[j], "nt", "bf16") for j in pairs]
    decay = [jnp.exp(jnp.where(incl, gc[j] - g_row[j], NEG_BIG)) for j in pairs]
    low = tuple(jnp.where(strict, kk[j] * decay[j], 0.0) for j in pairs)
    if tinv_known is None:
        tinv = _tri_inv(low, CHUNK)
    else:
        tinv = _tri_inv_known(low, tuple(cut(tinv_known)), CHUNK)
    uw = [mm(tinv[j], jnp.concatenate([v[j] * beta[j], kb[j] * eg[j]], axis=1), "nn", "f32") for j in pairs]
    intra = [jnp.where(incl, qk[j] * decay[j], 0.0) for j in pairs]
    g_last = []
    for g in gc:
        last0 = jnp.sum(jnp.where(row == CHUNK - 1, g, 0.0), axis=0, keepdims=True)
        last1 = jnp.sum(jnp.where(row == PAIR - 1, g, 0.0), axis=0, keepdims=True)
        g_last.append(jnp.where(row < CHUNK, last0, last1))
    join = lambda parts: jnp.concatenate(parts, axis=0)
    return (join([t[:, :DN_D] for t in uw]), join([t[:, DN_D:] for t in uw]), join(intra),
            join([q[j] * eg[j] for j in pairs]), join([k[j] * jnp.exp(g_last[j] - gc[j]) for j in pairs]),
            join([jnp.exp(g) for g in g_last]), join(list(tinv)))


def _dn1_fn(pids, q, k, v, gc, beta):
    return _dn1_pairs(q, k, v, gc, beta)


def _dn1_fn_known(pids, q, k, v, gc, beta, tinv):
    return _dn1_pairs(q, k, v, gc, beta, tinv)[:6]


def _dn2_step(half, state, qd, kd, u, w, intra, cd_row):
    heads = range(len(state))
    v_new = [u[h] - mm(w[h], state[h], "nn", "bf16") for h in heads]
    zeros = jnp.zeros_like(v_new[0])
    v_pair = [jnp.concatenate([v, zeros] if half == 0 else [zeros, v], axis=0) for v in v_new]
    from_state = [mm(qd[h], state[h], "nn", "bf16") for h in heads]
    out = tuple(from_state[h] + mm(intra[h], v_pair[h], "nn", "bf16") for h in heads)
    return out, tuple(state[h] * cd_row[h] + mm(kd[h], v_new[h], "tn", "bf16") for h in heads)


def _post_fn(pids, o, z, nw):
    outs = []
    for h in range(DN_HEADS):
        oh = o[:, DN_D * h:DN_D * (h + 1)]
        zh = z[:, DN_D * h:DN_D * (h + 1)]
        y = oh * lax.rsqrt(jnp.mean(oh * oh, axis=-1, keepdims=True) + NORM_EPS) * nw
        outs.append(y * _silu(zh))
    return (jnp.concatenate(outs, axis=1),)


def _memattn_fn(pids, qm, kvm):
    kmem, vmem = kvm[:, :MEM_W], kvm[:, MEM_W:]
    lane = _iota((1, MEM_W), 1)
    heads = range(MEM_HEADS)
    hm = [jnp.where((lane >> 6) == h, 1.0, 0.0) for h in heads]
    s = [mm(qm * (hm[h] * MEM_DH ** -0.5), kmem, "nt", "bf16") for h in heads]
    e = [jnp.exp(t - lax.stop_gradient(jnp.max(t, axis=-1, keepdims=True))) for t in s]
    o = [mm(e[h], vmem, "nn", "bf16") * (hm[h] / jnp.sum(e[h], axis=-1, keepdims=True)) for h in heads]
    return ((o[0] + o[1]) + (o[2] + o[3]),)


def _ln_fn(pids, h, mix, g, b):
    x = DN_ALPHA * h + mix
    mu = jnp.mean(x, axis=-1, keepdims=True)
    xc = x - mu
    var = jnp.mean(xc * xc, axis=-1, keepdims=True)
    return (xc * lax.rsqrt(var + LN_EPS) * g + b,)


def _rope_matrix():
    i, j = _iota((128, 128), 0), _iota((128, 128), 1)
    jj = j & 63
    return jnp.where((jj < 32) & (i == j + 32), -1.0, 0.0) + jnp.where((jj >= 32) & (i == j - 32), 1.0, 0.0)


def _rope128(x, cos, sin, rot):
    return x * cos + mm(x, rot, "nn", "selb") * sin


def _krope_fn(pids, kraw, cos, sin):
    rot = _rope_matrix()
    return (jnp.concatenate([_rope128(kraw[:, 128 * g:128 * (g + 1)], cos, sin, rot) for g in range(2)], axis=1),)


def _swa_fn(pids, qraw, cos, sin, k_halo, k_cur, v_halo, v_cur, sinks):
    tb = qraw.shape[0]
    nwin = tb // WINDOW
    rot = _rope_matrix()
    kcat = jnp.concatenate([k_halo, k_cur], axis=0)
    vcat = jnp.concatenate([v_halo, v_cur], axis=0)
    lane = _iota((1, 128), 1)
    halves = (jnp.where(lane < 64, 1.0, 0.0), jnp.where(lane >= 64, 1.0, 0.0))
    group = SWA_HEADS // 2
    rows = group * WINDOW
    qi, kj = _iota((rows, 2 * WINDOW), 0) & (WINDOW - 1), _iota((rows, 2 * WINDOW), 1)
    diff = qi + WINDOW - kj
    band = (diff >= 0) & (diff < WINDOW)
    qg = [_rope128(qraw[:, 128 * p:128 * (p + 1)], cos, sin, rot) for p in range(group)]
    sink = []
    for kv in range(2):
        cols = [jnp.sum(jnp.where(lane == group * kv + i, sinks, 0.0), axis=-1, keepdims=True)
                + jnp.zeros((WINDOW, 1), F32) for i in range(group)]
        sink.append(jnp.concatenate(cols, axis=0))
    units = [(w, kv) for w in range(nwin) for kv in range(2)]
    n_units = range(len(units))
    q6 = [jnp.concatenate([qg[3 * kv + i // 2][WINDOW * w:WINDOW * (w + 1)] * (halves[i % 2] * SWA_DH ** -0.5)
                           for i in range(group)], axis=0) for w, kv in units]
    keys = [kcat[WINDOW * w:WINDOW * (w + 2), 128 * kv:128 * (kv + 1)] for w, kv in units]
    vals = [vcat[WINDOW * w:WINDOW * (w + 2), 128 * kv:128 * (kv + 1)] for w, kv in units]
    s = [mm(q6[u], keys[u], "nt", "bf16") for u in n_units]
    s = [jnp.where(band & ((pids[0] * tb + WINDOW * (w - 1) + kj) >= 0), s[u], NEG_BIG)
         for u, (w, kv) in enumerate(units)]
    m = [lax.stop_gradient(jnp.maximum(jnp.max(s[u], axis=-1, keepdims=True), sink[kv]))
         for u, (w, kv) in enumerate(units)]
    e = [jnp.exp(s[u] - m[u]) for u in n_units]
    denom = [jnp.sum(e[u], axis=-1, keepdims=True) + jnp.exp(sink[kv] - m[u]) for u, (w, kv) in enumerate(units)]
    o = [mm(e[u], vals[u], "nn", "bf16") / denom[u] for u in n_units]
    out_rows = []
    for w in range(nwin):
        lanes = []
        for p in range(group):
            ou = o[units.index((w, p // 3))]
            i = 2 * (p % 3)
            lanes.append(ou[WINDOW * i:WINDOW * (i + 1)] * halves[0] + ou[WINDOW * (i + 1):WINDOW * (i + 2)] * halves[1])
        out_rows.append(jnp.concatenate(lanes, axis=1))
    return (jnp.concatenate(out_rows, axis=0),)


def _conv_fwd(proj, conv_w, tb):
    t_total = proj.shape[0]
    width = conv_w.shape[1]
    nb = t_total // tb

    def body(cur_ref, prev_ref, w_ref, out_ref):
        i = pl.program_id(0)
        prev = jnp.where(i > 0, prev_ref[...], 0.0)
        xcat = jnp.concatenate([prev, cur_ref[...]], axis=0)
        acc = xcat[8:] * w_ref[3:4, :]
        for j in range(3):
            acc = acc + pltpu.roll(xcat, 3 - j, 0)[8:] * w_ref[j:j + 1, :]
        out_ref[...] = acc

    return pl.pallas_call(
        body, grid=(nb,),
        in_specs=[pl.BlockSpec((tb, width), lambda i: (i, 0)),
                  pl.BlockSpec((8, width), lambda i: (jnp.maximum(i * (tb // 8) - 1, 0), 0)),
                  _whole((4, width))],
        out_specs=pl.BlockSpec((tb, width), lambda i: (i, 0)), out_shape=_sds((t_total, width)),
        name="conv_fwd", compiler_params=_cparams(("parallel",)))(proj, proj, conv_w)


def _conv_bwd(dc, proj, conv_w, tb):
    t_total, width = dc.shape
    nb = t_total // tb

    def body(dcur_ref, dnext_ref, cur_ref, prev_ref, w_ref, dx_ref, dw_ref):
        i = pl.program_id(0)
        dnext = jnp.where(i < nb - 1, dnext_ref[...], 0.0)
        dcur = dcur_ref[...]
        dcat = jnp.concatenate([dcur, dnext], axis=0)
        prev = jnp.where(i > 0, prev_ref[...], 0.0)
        xcat = jnp.concatenate([prev, cur_ref[...]], axis=0)

        @pl.when(i == 0)
        def _():
            dw_ref[...] = jnp.zeros(dw_ref.shape, F32)

        dx = dcur * w_ref[3:4, :]
        dw_ref[3:4, :] += jnp.sum(dcur * xcat[8:], axis=0, keepdims=True)
        for j in range(3):
            dx = dx + pltpu.roll(dcat, 8 - (3 - j), 0)[8:] * w_ref[j:j + 1, :]
            dw_ref[j:j + 1, :] += jnp.sum(dcur * pltpu.roll(xcat, 3 - j, 0)[8:], axis=0, keepdims=True)
        dx_ref[...] = dx.astype(dx_ref.dtype)

    return pl.pallas_call(
        body, grid=(nb,),
        in_specs=[pl.BlockSpec((tb, width), lambda i: (i, 0)),
                  pl.BlockSpec((8, width), lambda i: (jnp.minimum((i + 1) * (tb // 8), t_total // 8 - 1), 0)),
                  pl.BlockSpec((tb, width), lambda i: (i, 0)),
                  pl.BlockSpec((8, width), lambda i: (jnp.maximum(i * (tb // 8) - 1, 0), 0)),
                  _whole((4, width))],
        out_specs=[pl.BlockSpec((tb, width), lambda i: (i, 0)), _whole((4, width))],
        out_shape=[_sds((t_total, width), BF16), _sds((4, width))],
        name="conv_bwd", compiler_params=_cparams(("arbitrary",)))(dc, dc, proj, proj, conv_w)


def _head_spec(tb, nb=None):
    if nb is None:
        return pl.BlockSpec((tb, DN_D), lambda h, i: (i, h))
    return pl.BlockSpec((tb, DN_D), lambda h, i: (nb - 1 - i, h))


def _intra_spec(tb, nb=None):
    if nb is None:
        return pl.BlockSpec((None, tb, PAIR), lambda h, i: (h, i, 0))
    return pl.BlockSpec((None, tb, PAIR), lambda h, i: (h, nb - 1 - i, 0))


def _state_spec(tb, nb=None):
    if nb is None:
        return pl.BlockSpec((None, tb // CHUNK, DN_D, DN_D), lambda h, i: (h, i, 0, 0))
    return pl.BlockSpec((None, tb // CHUNK, DN_D, DN_D), lambda h, i: (h, nb - 1 - i, 0, 0))


def _scan_specs(tb, nb=None):
    blk = (lambda i: i) if nb is None else (lambda i: nb - 1 - i)
    rows = pl.BlockSpec((tb, DN_W), lambda i: (blk(i), 0))
    pair = pl.BlockSpec((DN_HEADS, tb, PAIR), lambda i: (0, blk(i), 0))
    states = pl.BlockSpec((DN_HEADS, tb // CHUNK, DN_D, DN_D), lambda i: (0, blk(i), 0, 0))
    return rows, pair, states


def _dn2_fwd(qd, kd, u, w, intra, cd, tb):
    t_total = qd.shape[0]
    rows, pair, states = _scan_specs(tb)

    def body(qd_ref, kd_ref, u_ref, w_ref, a_ref, cd_ref, o_ref, save_ref, state):
        @pl.when(pl.program_id(0) == 0)
        def _():
            state[...] = jnp.zeros(state.shape, F32)

        heads = range(DN_HEADS)
        lanes = [pl.ds(DN_D * h, DN_D) for h in heads]
        for j in range(tb // CHUNK):
            sl = pl.ds(CHUNK * j, CHUNK)
            s0 = tuple(state[h] for h in heads)
            for h in heads:
                save_ref[h, j] = s0[h]
            per_head = lambda ref: tuple(ref[sl, lanes[h]] for h in heads)
            out, s1 = _dn2_step(j % 2, s0, per_head(qd_ref), per_head(kd_ref), per_head(u_ref), per_head(w_ref),
                                tuple(a_ref[h, sl, :] for h in heads),
                                tuple(cd_ref[pl.ds(CHUNK * j, 1), lanes[h]] for h in heads))
            for h in heads:
                o_ref[sl, lanes[h]] = out[h]
                state[h] = s1[h]

    return pl.pallas_call(
        body, grid=(t_total // tb,), in_specs=[rows, rows, rows, rows, pair, rows],
        out_specs=[rows, states],
        out_shape=[_sds((t_total, DN_W)), _sds((DN_HEADS, t_total // CHUNK, DN_D, DN_D))],
        scratch_shapes=[pltpu.VMEM((DN_HEADS, DN_D, DN_D), F32)], name="dn2_fwd",
        compiler_params=_cparams(("arbitrary",)))(qd, kd, u, w, intra, cd)


def _dn2_bwd(qd, kd, u, w, intra, cd, saved, d_o, tb):
    t_total = qd.shape[0]
    nb = t_total // tb
    rows, pair, states = _scan_specs(tb, nb)

    def body(qd_ref, kd_ref, u_ref, w_ref, a_ref, cd_ref, save_ref, do_ref,
             dqd_ref, dkd_ref, du_ref, dw_ref, da_ref, dcd_ref, dstate):
        @pl.when(pl.program_id(0) == 0)
        def _():
            dstate[...] = jnp.zeros(dstate.shape, F32)

        first_row = _iota((CHUNK, DN_D), 0) == 0
        heads = range(DN_HEADS)
        lanes = [pl.ds(DN_D * h, DN_D) for h in heads]
        for j in reversed(range(tb // CHUNK)):
            sl = pl.ds(CHUNK * j, CHUNK)
            per_head = lambda ref: tuple(ref[sl, lanes[h]] for h in heads)
            _, vjp = jax.vjp(functools.partial(_dn2_step, j % 2), tuple(save_ref[h, j] for h in heads),
                             per_head(qd_ref), per_head(kd_ref), per_head(u_ref), per_head(w_ref),
                             tuple(a_ref[h, sl, :] for h in heads),
                             tuple(cd_ref[pl.ds(CHUNK * j, 1), lanes[h]] for h in heads))
            ds0, dqd, dkd, du, dw, da, dcd = vjp((per_head(do_ref), tuple(dstate[h] for h in heads)))
            for h in heads:
                dqd_ref[sl, lanes[h]] = dqd[h]
                dkd_ref[sl, lanes[h]] = dkd[h]
                du_ref[sl, lanes[h]] = du[h]
                dw_ref[sl, lanes[h]] = dw[h]
                da_ref[h, sl, :] = da[h]
                dcd_ref[sl, lanes[h]] = jnp.where(first_row, dcd[h], 0.0)
                dstate[h] = ds0[h]

    full = _sds((t_total, DN_W))
    return pl.pallas_call(
        body, grid=(nb,),
        in_specs=[rows, rows, rows, rows, pair, rows, states, rows],
        out_specs=[rows, rows, rows, rows, pair, rows],
        out_shape=[full, full, full, full, _sds((DN_HEADS, t_total, PAIR)), full],
        scratch_shapes=[pltpu.VMEM((DN_HEADS, DN_D, DN_D), F32)], name="dn2_bwd",
        compiler_params=_cparams(("arbitrary",)))(qd, kd, u, w, intra, cd, saved, d_o)


def _loss_and_grad(y, target, tb):
    t_total, d = y.shape

    def body(y_ref, t_ref, dy_ref, acc_ref):
        @pl.when(pl.program_id(0) == 0)
        def _():
            acc_ref[...] = jnp.zeros(acc_ref.shape, F32)

        err = y_ref[...] - t_ref[...]
        dy_ref[...] = err * (1.0 / d)
        acc_ref[...] += jnp.sum(err * err, axis=0, keepdims=True)

    dy, acc = pl.pallas_call(
        body, grid=(t_total // tb,), in_specs=[_rows(tb, d), _rows(tb, d)],
        out_specs=[_rows(tb, d), _whole((1, d))], out_shape=[_sds((t_total, d)), _sds((1, d))],
        name="loss", compiler_params=_cparams(("arbitrary",)))(y, target)
    return 0.5 * jnp.sum(acc) / d, dy


def _halo_sum(mains, halos, tb):
    t_total, width = mains[0].shape
    nb = t_total // tb
    n = len(mains)

    def body(*refs):
        out_ref = refs[-1]
        i = pl.program_id(0)
        tot = refs[0][...]
        for r in refs[1:n]:
            tot = tot + r[...]
        hal = refs[n][...]
        for r in refs[n + 1:2 * n]:
            hal = hal + r[...]
        hal = jnp.where(i < nb - 1, hal, 0.0)
        out_ref[...] = tot + jnp.concatenate([jnp.zeros((tb - WINDOW, width), F32), hal], axis=0)

    return pl.pallas_call(
        body, grid=(nb,),
        in_specs=[_rows(tb, width)] * n
        + [pl.BlockSpec((None, WINDOW, width), lambda i: (jnp.minimum(i + 1, nb - 1), 0, 0))] * n,
        out_specs=_rows(tb, width), out_shape=_sds((t_total, width)), name="halo_sum",
        compiler_params=_cparams(("parallel",)))(*mains, *halos)


def _adamw(recv, w, m, v, tr, name):
    slots, r_total, c_total = recv.shape
    tr = min(tr, r_total)
    assert r_total % tr == 0
    c1 = 1.0 / (1.0 - ADAM_B1 ** ADAM_STEP)
    c2 = 1.0 / (1.0 - ADAM_B2 ** ADAM_STEP)

    def body(recv_ref, w_ref, m_ref, v_ref, g_ref, d_ref, nm_ref, nv_ref):
        g = recv_ref[0].astype(F32)
        for s in range(1, slots):
            g = g + recv_ref[s].astype(F32)
        nm = ADAM_B1 * m_ref[...] + (1.0 - ADAM_B1) * g
        nv = ADAM_B2 * v_ref[...] + (1.0 - ADAM_B2) * (g * g)
        g_ref[...] = g
        nm_ref[...] = nm
        nv_ref[...] = nv
        d_ref[...] = -ADAM_LR * ((nm * c1) / (jnp.sqrt(nv * c2) + ADAM_EPS) + ADAM_WD * w_ref[...])

    blk = pl.BlockSpec((tr, c_total), lambda i: (i, 0))
    return pl.pallas_call(
        body, grid=(r_total // tr,),
        in_specs=[pl.BlockSpec((slots, tr, c_total), lambda i: (0, i, 0)), blk, blk, blk],
        out_specs=[blk] * 4, out_shape=[_sds((r_total, c_total))] * 4, name=name,
        compiler_params=_cparams(("parallel",)))(recv, w, m, v)


def _me_and_peers():
    x, y, c = lax.axis_index("x"), lax.axis_index("y"), lax.axis_index("c")
    me = 4 * x + 2 * y + c
    peers = []
    for k in range(1, N_DEV):
        px = 1 - x if (k >> 2) & 1 else x
        py = 1 - y if (k >> 1) & 1 else y
        pc = 1 - c if k & 1 else c
        peers.append(((px, py, pc), 4 * px + 2 * py + pc))
    return me, peers


def _small_exchange(packed, reduce):
    r_total = packed.shape[0]

    def body(p_ref, out_ref, gath_ref, send_sems, recv_sems):
        me, peers = _me_and_peers()
        gath_ref[me] = p_ref[...]
        copies = []
        for k, (dev, _) in enumerate(peers):
            cp = pltpu.make_async_remote_copy(src_ref=p_ref, dst_ref=gath_ref.at[me], send_sem=send_sems.at[k],
                                              recv_sem=recv_sems.at[k], device_id=dev,
                                              device_id_type=pl.DeviceIdType.MESH)
            cp.start()
            copies.append(cp)
        for k, (dev, idx) in enumerate(peers):
            pltpu.make_async_remote_copy(src_ref=p_ref, dst_ref=gath_ref.at[idx], send_sem=send_sems.at[k],
                                         recv_sem=recv_sems.at[k], device_id=dev,
                                         device_id_type=pl.DeviceIdType.MESH).wait_recv()
        for cp in copies:
            cp.wait_send()
        if reduce:
            tot = gath_ref[0]
            for d in range(1, N_DEV):
                tot = tot + gath_ref[d]
            out_ref[...] = tot
        else:
            out_ref[...] = gath_ref[...]

    out_shape = _sds((r_total, 128)) if reduce else _sds((N_DEV, r_total, 128))
    return pl.pallas_call(
        body, in_specs=[pl.BlockSpec(memory_space=pltpu.VMEM)], out_specs=pl.BlockSpec(memory_space=pltpu.VMEM),
        out_shape=out_shape,
        scratch_shapes=[pltpu.VMEM((N_DEV, r_total, 128), F32), pltpu.SemaphoreType.DMA((N_DEV - 1,)),
                        pltpu.SemaphoreType.DMA((N_DEV - 1,))],
        name="small_allreduce" if reduce else "small_allgather")(packed)


def _slot(ref, axis, idx, size):
    sel = [slice(None)] * len(ref.shape)
    sel[axis] = idx if size is None else pl.ds(pl.multiple_of(idx * size, size), size)
    return ref.at[tuple(sel)]


def _big_exchange(srcs, dst_shapes, src_view, dst_view, name):
    n = len(srcs)

    def body(*refs):
        src_refs, dst_refs = refs[:n], refs[n:2 * n]
        send_sems, recv_sems, local_sems = refs[2 * n:]
        me, peers = _me_and_peers()
        local, remote = [], []
        for t in range(n):
            loc = pltpu.make_async_copy(src_view(t, src_refs[t], me), dst_view(t, dst_refs[t], me), local_sems.at[t])
            loc.start()
            local.append(loc)
            for k, (dev, idx) in enumerate(peers):
                cp = pltpu.make_async_remote_copy(
                    src_ref=src_view(t, src_refs[t], idx), dst_ref=dst_view(t, dst_refs[t], me),
                    send_sem=send_sems.at[t, k], recv_sem=recv_sems.at[t, k], device_id=dev,
                    device_id_type=pl.DeviceIdType.MESH)
                cp.start()
                remote.append(cp)
        for t in range(n):
            for k, (dev, idx) in enumerate(peers):
                pltpu.make_async_remote_copy(
                    src_ref=src_view(t, src_refs[t], me), dst_ref=dst_view(t, dst_refs[t], idx),
                    send_sem=send_sems.at[t, k], recv_sem=recv_sems.at[t, k], device_id=dev,
                    device_id_type=pl.DeviceIdType.MESH).wait_recv()
        for cp in remote:
            cp.wait_send()
        for cp in local:
            cp.wait()

    any_spec = pl.BlockSpec(memory_space=pl.ANY)
    return pl.pallas_call(
        body, in_specs=[any_spec] * n, out_specs=[any_spec] * n, out_shape=dst_shapes,
        scratch_shapes=[pltpu.SemaphoreType.DMA((n, N_DEV - 1)), pltpu.SemaphoreType.DMA((n, N_DEV - 1)),
                        pltpu.SemaphoreType.DMA((n,))],
        name=name)(*srcs)


BIG = {
    "a_w_in": (1, (2, 1024, A_IN)),
    "b_w_in": (1, (2, 1024, 1024)),
    "w_kv_shared": (0, (1024, 256)),
    "mem_w_kv": (1, (4, 1024, 512)),
    "w_o": (1, (4, 1024, 1024)),
    "mlp_w_up": (2, (4, 1024, 4096)),
    "mlp_w_down": (1, (4, 4096, 1024)),
}
BIG_NAMES = tuple(BIG)


def _allgather_weights(shards):
    dst_shapes, axes, sizes = [], [], []
    for name, s in zip(BIG_NAMES, shards):
        axis, full = BIG[name]
        if name == "mlp_w_up":
            dst_shapes.append(_sds((N_DEV,) + s.shape, BF16))
            axes.append(0)
            sizes.append(None)
        else:
            dst_shapes.append(_sds(full, BF16))
            axes.append(axis)
            sizes.append(s.shape[axis])

    n = len(shards)

    def body(*refs):
        src_refs, dst_refs = refs[:n], refs[n:2 * n]
        send_sems, recv_sems, local_sems = refs[2 * n:]
        x, y, c = lax.axis_index("x"), lax.axis_index("y"), lax.axis_index("c")
        sibling = (x, y, 1 - c)
        chips = [(1 - x, y), (x, 1 - y), (1 - x, 1 - y)]
        index = lambda px, py, pc: 4 * px + 2 * py + pc

        def copy(t, k, block, to, src=None):
            rows = _slot(dst_refs[t], axes[t], index(*block), sizes[t])
            return pltpu.make_async_remote_copy(
                src_ref=rows if src is None else src, dst_ref=rows, send_sem=send_sems.at[t, k],
                recv_sem=recv_sems.at[t, k], device_id=to, device_id_type=pl.DeviceIdType.MESH)

        started, local = [], []
        for t in range(n):
            mine = pltpu.make_async_copy(src_refs[t], _slot(dst_refs[t], axes[t], index(x, y, c), sizes[t]),
                                         local_sems.at[t])
            mine.start()
            local.append(mine)
            first = [copy(t, 0, (x, y, c), sibling, src=src_refs[t])]
            first += [copy(t, 1 + j, (x, y, c), (*chip, c), src=src_refs[t]) for j, chip in enumerate(chips)]
            for cp in first:
                cp.start()
            started += first
        for t in range(n):
            for j, chip in enumerate(chips):
                copy(t, 1 + j, (*chip, c), (x, y, c)).wait_recv()
                passed = copy(t, 4 + j, (*chip, c), sibling)
                passed.start()
                started.append(passed)
        for t in range(n):
            copy(t, 0, sibling, (x, y, c)).wait_recv()
            for j, chip in enumerate(chips):
                copy(t, 4 + j, (*chip, 1 - c), (x, y, c)).wait_recv()
        for cp in started:
            cp.wait_send()
        for cp in local:
            cp.wait()

    any_spec = pl.BlockSpec(memory_space=pl.ANY)
    return pl.pallas_call(
        body, in_specs=[any_spec] * n, out_specs=[any_spec] * n, out_shape=dst_shapes,
        scratch_shapes=[pltpu.SemaphoreType.DMA((n, N_DEV - 1)), pltpu.SemaphoreType.DMA((n, N_DEV - 1)),
                        pltpu.SemaphoreType.DMA((n,))],
        name="allgather_weights")(*shards)


def _scatter_grads(grads):
    dst_shapes, axes, sizes = [], [], []
    for name, g in zip(BIG_NAMES, grads):
        axis, full = BIG[name]
        if name == "mlp_w_up":
            shard = (g.shape[0],) + g.shape[2:]
            axes.append(1)
            sizes.append(None)
        else:
            shard = tuple(d // N_DEV if a == axis else d for a, d in enumerate(full))
            axes.append(axis)
            sizes.append(shard[axis])
        dst_shapes.append(_sds((N_DEV,) + shard, g.dtype))

    def src_view(t, ref, idx):
        return _slot(ref, axes[t], idx, sizes[t])

    def dst_view(t, ref, idx):
        return ref.at[idx]

    return _big_exchange(grads, dst_shapes, src_view, dst_view, "scatter_grads")


def _pad_row(vec, width=128):
    return jnp.pad(vec.astype(F32), (0, width - vec.shape[0])).reshape(1, width)


def _block_sizes(t_total):
    return dict(row=min(256, t_total), dn=min(512, t_total), swa=min(256, t_total), scan=min(256, t_total))


def _ln_apply(h, mix, g, b, tb):
    t_total, d = h.shape
    fwd = lambda pids, *a: _ln_fn(pids, *a) * 2
    return _block_fwd(fwd, [h, mix, g, b], [_rows(tb, d), _rows(tb, d), _whole((1, d)), _whole((1, d))],
                      [_sds((t_total, d)), _sds((t_total, d), BF16)], [_rows(tb, d), _rows(tb, d)],
                      (t_total // tb,), "ln_fwd")


def _ln_grad(h, mix, g, b, dy, tb):
    t_total, d = h.shape
    return _block_bwd(_ln_fn, [h, mix, g, b], [_rows(tb, d), _rows(tb, d), _whole((1, d)), _whole((1, d))],
                      [dy], [_rows(tb, d)], ["s", "s", "a", "a"],
                      [_sds((t_total, d)), _sds((t_total, d), BF16), _sds((1, d)), _sds((1, d))],
                      [_rows(tb, d), _rows(tb, d), _whole((1, d)), _whole((1, d))], (t_total // tb,), "ln_bwd")


def _memattn_specs(tb, qcol):
    return [pl.BlockSpec((tb, MEM_W), lambda i: (i, qcol)), _whole((MEM_W, 2 * MEM_W))]


def _act_epilogue(acc):
    r = jnp.maximum(acc, 0.0)
    return (r * r,)


def _dact_epilogue(acc, act):
    return (acc * (2.0 * jnp.sqrt(act.astype(F32))),)


def _add_epilogue(acc, other):
    return (acc + other,)


def _local_step(x, mem, positions, target, wts, small):
    t_total = x.shape[0]
    bs = _block_sizes(t_total)
    tb, tdn, tsw = bs["row"], bs["dn"], bs["swa"]
    nb = t_total // tb
    nbs = t_total // tsw

    inv_freq = ROPE_THETA ** (-jnp.arange(0, SWA_DH, 2, dtype=F32) / SWA_DH)
    ang = positions.astype(F32)[:, None] * inv_freq
    cos = jnp.tile(jnp.cos(ang), (1, 4))
    sin = jnp.tile(jnp.sin(ang), (1, 4))

    w_a = [jnp.concatenate([wts["a_w_in"][l][:, :3072], wts["a_w_in"][l][:, 3084:], wts["a_w_in"][l][:, 3072:3084],
                            jnp.zeros((D_MODEL, A_IN_PAD - A_IN), BF16)], axis=1) for l in range(N_A)]
    wkv = wts["w_kv_shared"]
    w_kvd = jnp.concatenate([wkv[:, 64 * (i // 2):64 * (i // 2 + 1)] for i in range(8)], axis=1)
    mem_b = mem.astype(BF16)
    w_up = [jnp.moveaxis(wts["mlp_w_up"][:, l], 0, 1).reshape(D_MODEL, 4 * D_MODEL) for l in range(DEPTH)]

    saved = []
    h, hb = x, x.astype(BF16)
    kr = vd_src = None
    for l in range(DEPTH):
        sv = dict(h=h, hb=hb)
        kvm = _matmul(mem_b, wts["mem_w_kv"][l], "nn", [F32], "mm_memkv", tm=256)
        if l < N_A:
            proj = _matmul(hb, w_a[l], "nn", [F32], "mm_proj_a", tn=1152)
            conv_w = small["a_conv_w"][l]
            c = _conv_fwd(proj, conv_w, tb)
            alog, dtb = _pad_row(small["a_A_log"][l]), _pad_row(small["a_dt_bias"][l])
            rowa_in = [c, proj, alog, dtb]
            rowa_specs = [_rows(tb, 3 * DN_W), _rows(tb, 128, 26), _whole((1, 128)), _whole((1, 128))]
            q, k, v, gcb, betab = _block_fwd(_rowa_fn, rowa_in, rowa_specs, [_sds((t_total, DN_W))] * 5,
                                             [_rows(tb, DN_W)] * 5, (nb,), "rowa_fwd")
            hs = _head_spec(tdn)
            dn_grid = (DN_HEADS, t_total // tdn)
            full = _sds((t_total, DN_W))
            full_b = _sds((t_total, DN_W), BF16)
            dn1_out_shapes = [full, full_b, _sds((DN_HEADS, t_total, PAIR), BF16), full_b, full_b, full,
                              _sds((DN_HEADS, t_total, PAIR))]
            dn1_out_specs = [hs, hs, _intra_spec(tdn), hs, hs, hs, _intra_spec(tdn)]
            u, w, intra, qd, kd, cd, tinv = _block_fwd(_dn1_fn, [q, k, v, gcb, betab], [hs] * 5, dn1_out_shapes,
                                                       dn1_out_specs, dn_grid, "dn1_fwd")
            o, states = _dn2_fwd(qd, kd, u, w, intra, cd, bs["scan"])
            nw = small["a_norm_w"][l].reshape(1, DN_D)
            post_in = [o, proj, nw]
            post_specs = [_rows(tb, DN_W), _rows(tb, DN_W, 3), _whole((1, DN_D))]
            (og,) = _block_fwd(_post_fn, post_in, post_specs, [_sds((t_total, DN_W), BF16)], [_rows(tb, DN_W)],
                               (nb,), "post_fwd")
            qm_col = 12
            sv.update(proj=proj, c=c, rowa_in=rowa_in, rowa_specs=rowa_specs, dn1_in=[q, k, v, gcb, betab, tinv],
                      dn2_in=[qd, kd, u, w, intra, cd], states=states, post_in=post_in, post_specs=post_specs,
                      conv_w=conv_w)
        else:
            jb = l - N_A
            proj = _matmul(hb, wts["b_w_in"][jb], "nn", [F32], "mm_proj_b")
            sinks = _pad_row(small["b_sinks"][jb])
            swa_in = [proj, cos, sin, kr, kr, vd_src, vd_src, sinks]
            swa_specs = [_rows(tsw, DN_W), _rows(tsw, 128), _rows(tsw, 128),
                         pl.BlockSpec((WINDOW, 256), lambda i: (jnp.maximum(i * (tsw // WINDOW) - 1, 0), 0)),
                         _rows(tsw, 256),
                         pl.BlockSpec((WINDOW, 256), lambda i: (jnp.maximum(i * (tsw // WINDOW) - 1, 0), 1)),
                         _rows(tsw, 256, 1), _whole((1, 128))]
            (og,) = _block_fwd(_swa_fn, swa_in, swa_specs, [_sds((t_total, DN_W), BF16)], [_rows(tsw, DN_W)],
                               (nbs,), "swa_fwd")
            qm_col = 3
            sv.update(proj=proj, swa_in=swa_in, swa_specs=swa_specs)
        mem_in = [proj, kvm]
        (mo,) = _block_fwd(_memattn_fn, mem_in, _memattn_specs(tb, qm_col), [_sds((t_total, MEM_W), BF16)],
                           [_rows(tb, MEM_W)], (nb,), "memattn_fwd")
        mixin = jnp.concatenate([og, mo], axis=1)
        mix = _matmul(mixin, wts["w_o"][l], "nn", [F32], "mm_wo")
        g0, b0 = small["ln_g"][l, 0].reshape(1, -1), small["ln_b"][l, 0].reshape(1, -1)
        h1, h1b = _ln_apply(h, mix, g0, b0, tb)
        act = _matmul(h1b, w_up[l], "nn", [BF16], "mm_up", epi=_act_epilogue, tm=2048)
        mlp = _matmul(act, wts["mlp_w_down"][l], "nn", [F32], "mm_down", tk=2048)
        g1, b1 = small["ln_g"][l, 1].reshape(1, -1), small["ln_b"][l, 1].reshape(1, -1)
        h2, h2b = _ln_apply(h1, mlp, g1, b1, tb)
        sv.update(kvm=kvm, mem_in=mem_in, qm_col=qm_col, mixin=mixin, mix=mix, ln0=(g0, b0), h1=h1, h1b=h1b,
                  act=act, mlp=mlp, ln1=(g1, b1))
        saved.append(sv)
        h, hb = h2, h2b
        if l == N_A - 1:
            kvd = _matmul(hb, w_kvd, "nn", [F32], "mm_kvd")
            krope_in = [kvd, cos, sin]
            krope_specs = [_rows(tb, 256), _rows(tb, 128), _rows(tb, 128)]
            (kr,) = _block_fwd(_krope_fn, krope_in, krope_specs, [_sds((t_total, 256))], [_rows(tb, 256)], (nb,),
                               "krope_fwd")
            vd_src = kvd

    loss, dh = _loss_and_grad(h, target, tb)

    grads = {n: [None] * BIG[n][1][0] for n in BIG_NAMES if n != "w_kv_shared"}
    sg = dict(a_conv_w=[None] * N_A, a_A_log=[None] * N_A, a_dt_bias=[None] * N_A, a_norm_w=[None] * N_A,
              b_sinks=[None] * (DEPTH - N_A), ln_g=[[None, None] for _ in range(DEPTH)],
              ln_b=[[None, None] for _ in range(DEPTH)])
    dk_parts, dv_parts = [], []
    for l in reversed(range(DEPTH)):
        sv = saved[l]
        if l == N_A - 1:
            dkr = _halo_sum([p[0] for p in dk_parts], [p[1] for p in dk_parts], tsw)
            dvv = _halo_sum([p[0] for p in dv_parts], [p[1] for p in dv_parts], tsw)
            (dkraw,) = _block_bwd(_krope_fn, krope_in, krope_specs, [dkr], [_rows(tb, 256)], ["s", None, None],
                                  [_sds((t_total, 256), BF16)], [_rows(tb, 256)], (nb,), "krope_bwd")
            dkvd = jnp.concatenate([dkraw, dvv.astype(BF16)], axis=1)
            g_kvd = _matmul(saved[l + 1]["hb"], dkvd, "tn", [F32], "mm_dw_kvd", tm=1024, tn=512)
            dh = _matmul(dkvd, w_kvd, "nt", [F32], "mm_dx_kvd", epi=_add_epilogue, extras=[dh], tn=1024, tk=512)
            grads["w_kv_shared"] = jnp.concatenate(
                [g_kvd[:, 128 * i:128 * i + 64] + g_kvd[:, 128 * i + 64:128 * (i + 1)] for i in range(4)],
                axis=1).astype(BF16)
        g1, b1 = sv["ln1"]
        dh1a, dmlp, dg1, db1 = _ln_grad(sv["h1"], sv["mlp"], g1, b1, dh, tb)
        dup = _matmul(dmlp, wts["mlp_w_down"][l], "nt", [BF16], "mm_dact", epi=_dact_epilogue, extras=[sv["act"]])
        grads["mlp_w_down"][l] = _matmul(sv["act"], dmlp, "tn", [BF16], "mm_dw_down", tk=4096)
        g_up = _matmul(sv["h1b"], dup, "tn", [BF16], "mm_dw_up", tk=4096)
        grads["mlp_w_up"][l] = jnp.moveaxis(g_up.reshape(D_MODEL, N_DEV, 512), 1, 0)
        dh1 = _matmul(dup, w_up[l], "nt", [F32], "mm_dx_up", epi=_add_epilogue, extras=[dh1a], tk=2048)
        g0, b0 = sv["ln0"]
        dha, dmix, dg0, db0 = _ln_grad(sv["h"], sv["mix"], g0, b0, dh1, tb)
        sg["ln_g"][l] = [dg0, dg1]
        sg["ln_b"][l] = [db0, db1]
        grads["w_o"][l] = _matmul(sv["mixin"], dmix, "tn", [BF16], "mm_dw_o", tk=4096)
        dmixin = _matmul(dmix, wts["w_o"][l], "nt", [F32], "mm_dx_o", tn=1024)
        dqm, dkvm = _block_bwd(_memattn_fn, sv["mem_in"], _memattn_specs(tb, sv["qm_col"]), [dmixin],
                               [_rows(tb, MEM_W, 3)], ["s", "a"],
                               [_sds((t_total, MEM_W), BF16), _sds((MEM_W, 2 * MEM_W))],
                               [_rows(tb, MEM_W), _whole((MEM_W, 2 * MEM_W))], (nb,), "memattn_bwd")
        grads["mem_w_kv"][l] = _matmul(mem_b, dkvm.astype(BF16), "tn", [BF16], "mm_dw_memkv", tm=1024, tn=512)
        if l < N_A:
            d_o, dz, dnw = _block_bwd(_post_fn, sv["post_in"], sv["post_specs"], [dmixin], [_rows(tb, DN_W)],
                                      ["s", "s", "a"],
                                      [_sds((t_total, DN_W)), _sds((t_total, DN_W), BF16), _sds((1, DN_D))],
                                      [_rows(tb, DN_W), _rows(tb, DN_W), _whole((1, DN_D))], (nb,), "post_bwd")
            sg["a_norm_w"][l] = dnw
            dqd, dkd, du, dw, da, dcd = _dn2_bwd(*sv["dn2_in"], sv["states"], d_o, bs["scan"])
            hs = _head_spec(tdn)
            full = _sds((t_total, DN_W))
            dq, dk, dv, dgc, dbeta = _block_bwd(
                _dn1_fn_known, sv["dn1_in"], [hs] * 5 + [_intra_spec(tdn)], [du, dw, da, dqd, dkd, dcd],
                [hs, hs, _intra_spec(tdn), hs, hs, hs], ["s"] * 5 + [None], [full] * 5, [hs] * 5,
                (DN_HEADS, t_total // tdn), "dn1_bwd")
            dc, dab, dalog, ddtb = _block_bwd(
                _rowa_fn, sv["rowa_in"], sv["rowa_specs"], [dq, dk, dv, dgc, dbeta], [_rows(tb, DN_W)] * 5,
                ["s", "s", "a", "a"],
                [_sds((t_total, 3 * DN_W)), _sds((t_total, 128), BF16), _sds((1, 128)), _sds((1, 128))],
                [_rows(tb, 3 * DN_W), _rows(tb, 128), _whole((1, 128)), _whole((1, 128))], (nb,), "rowa_bwd")
            sg["a_A_log"][l] = dalog[0, :DN_HEADS]
            sg["a_dt_bias"][l] = ddtb[0, :DN_HEADS]
            dx, dconv = _conv_bwd(dc, sv["proj"], sv["conv_w"], tb)
            sg["a_conv_w"][l] = dconv
            dproj = jnp.concatenate([dx, dz, dqm, dab], axis=1)
            g_in = _matmul(sv["hb"], dproj, "tn", [BF16], "mm_dw_a", tm=1024, tn=1152, tk=2048)
            grads["a_w_in"][l] = jnp.concatenate([g_in[:, :3072], g_in[:, 3328:3340], g_in[:, 3072:3328]], axis=1)
            dh = _matmul(dproj, w_a[l], "nt", [F32], "mm_dx_a", epi=_add_epilogue, extras=[dha], tk=1152)
        else:
            jb = l - N_A
            swa_kinds = ["s", None, None, "s", "s", "s", "s", "a"]
            halo_spec = pl.BlockSpec((None, WINDOW, 256), lambda i: (i, 0, 0))
            dq, dkh, dkc, dvh, dvc, dsink = _block_bwd(
                _swa_fn, sv["swa_in"], sv["swa_specs"], [dmixin], [_rows(tsw, DN_W)], swa_kinds,
                [_sds((t_total, DN_W), BF16), _sds((nbs, WINDOW, 256)), _sds((t_total, 256)),
                 _sds((nbs, WINDOW, 256)), _sds((t_total, 256)), _sds((1, 128))],
                [_rows(tsw, DN_W), halo_spec, _rows(tsw, 256), halo_spec, _rows(tsw, 256), _whole((1, 128))],
                (nbs,), "swa_bwd")
            sg["b_sinks"][jb] = dsink[0, :SWA_HEADS]
            dk_parts.append((dkc, dkh))
            dv_parts.append((dvc, dvh))
            dproj = jnp.concatenate([dq, dqm], axis=1)
            grads["b_w_in"][jb] = _matmul(sv["hb"], dproj, "tn", [BF16], "mm_dw_b", tk=4096)
            dh = _matmul(dproj, wts["b_w_in"][jb], "nt", [F32], "mm_dx_b", epi=_add_epilogue, extras=[dha], tn=1024)

    big = []
    for n in BIG_NAMES:
        if n == "w_kv_shared":
            big.append(grads[n])
        else:
            big.append(jnp.stack(grads[n], axis=0))
    small_grads = dict(
        a_conv_w=jnp.stack(sg["a_conv_w"]), a_A_log=jnp.stack(sg["a_A_log"]), a_dt_bias=jnp.stack(sg["a_dt_bias"]),
        a_norm_w=jnp.concatenate(sg["a_norm_w"], axis=0), b_sinks=jnp.stack(sg["b_sinks"]),
        ln_g=jnp.stack([jnp.concatenate(p, axis=0) for p in sg["ln_g"]]),
        ln_b=jnp.stack([jnp.concatenate(p, axis=0) for p in sg["ln_b"]]))
    return loss, dh, big, small_grads


def _pack(arrays, rows):
    flat = []
    for a in arrays:
        v = a.astype(F32).reshape(-1)
        flat.append(jnp.pad(v, (0, (-v.shape[0]) % 128)))
    flat = jnp.concatenate(flat)
    return jnp.pad(flat, (0, rows * 128 - flat.shape[0])).reshape(rows, 128)


def _unpack(slab, shapes):
    flat = slab.reshape(slab.shape[:-2] + (-1,))
    out, off = [], 0
    for s in shapes:
        n = math.prod(s)
        out.append(flat[..., off:off + n].reshape(slab.shape[:-2] + tuple(s)))
        off += n + (-n) % 128
    return out


def _rows_for(shapes):
    rows = sum((math.prod(s) + 127) // 128 for s in shapes)
    return rows + (-rows) % 8


SMALL_NAMES = ("a_conv_w", "a_A_log", "a_dt_bias", "a_norm_w", "b_sinks", "ln_g", "ln_b")
SMALL_SHARDED = {"a_conv_w": 2, "ln_g": 2, "ln_b": 2}
SMALL_FULL = {"a_conv_w": (2, 4, 2304), "a_A_log": (2, 6), "a_dt_bias": (2, 6), "a_norm_w": (2, 128),
              "b_sinks": (2, 12), "ln_g": (4, 2, 1024), "ln_b": (4, 2, 1024)}


def kernel(x, mem, positions, a_w_in, a_conv_w, a_A_log, a_dt_bias, a_norm_w, b_w_in, b_sinks, w_kv_shared, mem_w_kv, w_o, mlp_w_up, mlp_w_down, ln_g, ln_b, loss_target, m_a_w_in, m_a_conv_w, m_a_A_log, m_a_dt_bias, m_a_norm_w, m_b_w_in, m_b_sinks, m_w_kv_shared, m_mem_w_kv, m_w_o, m_mlp_w_up, m_mlp_w_down, m_ln_g, m_ln_b, v_a_w_in, v_a_conv_w, v_a_A_log, v_a_dt_bias, v_a_norm_w, v_b_w_in, v_b_sinks, v_w_kv_shared, v_mem_w_kv, v_w_o, v_mlp_w_up, v_mlp_w_down, v_ln_g, v_ln_b):
    params = dict(a_w_in=a_w_in, a_conv_w=a_conv_w, a_A_log=a_A_log, a_dt_bias=a_dt_bias, a_norm_w=a_norm_w,
                  b_w_in=b_w_in, b_sinks=b_sinks, w_kv_shared=w_kv_shared, mem_w_kv=mem_w_kv, w_o=w_o,
                  mlp_w_up=mlp_w_up, mlp_w_down=mlp_w_down, ln_g=ln_g, ln_b=ln_b)
    mom = dict(a_w_in=m_a_w_in, a_conv_w=m_a_conv_w, a_A_log=m_a_A_log, a_dt_bias=m_a_dt_bias, a_norm_w=m_a_norm_w,
               b_w_in=m_b_w_in, b_sinks=m_b_sinks, w_kv_shared=m_w_kv_shared, mem_w_kv=m_mem_w_kv, w_o=m_w_o,
               mlp_w_up=m_mlp_w_up, mlp_w_down=m_mlp_w_down, ln_g=m_ln_g, ln_b=m_ln_b)
    var = dict(a_w_in=v_a_w_in, a_conv_w=v_a_conv_w, a_A_log=v_a_A_log, a_dt_bias=v_a_dt_bias, a_norm_w=v_a_norm_w,
               b_w_in=v_b_w_in, b_sinks=v_b_sinks, w_kv_shared=v_w_kv_shared, mem_w_kv=v_mem_w_kv, w_o=v_w_o,
               mlp_w_up=v_mlp_w_up, mlp_w_down=v_mlp_w_down, ln_g=v_ln_g, ln_b=v_ln_b)
    me = 4 * lax.axis_index("x") + 2 * lax.axis_index("y") + lax.axis_index("c")

    full_b = _allgather_weights([params[n].astype(BF16) for n in BIG_NAMES])
    wts = dict(zip(BIG_NAMES, full_b))
    sharded_names = [n for n in SMALL_NAMES if n in SMALL_SHARDED]
    shard_shapes = [params[n].shape for n in sharded_names]
    gathered = _small_exchange(_pack([params[n] for n in sharded_names], _rows_for(shard_shapes)), reduce=False)
    small = {n: params[n] for n in SMALL_NAMES if n not in SMALL_SHARDED}
    for n, g in zip(sharded_names, _unpack(gathered, shard_shapes)):
        small[n] = jnp.moveaxis(g, 0, 2).reshape(SMALL_FULL[n])

    loss, dx, big_grads, small_grads = _local_step(x[0], mem[0], positions[0], loss_target[0], wts, small)
    loss = lax.psum(loss, ("x", "y", "c"))
    recv = _scatter_grads(big_grads)
    out = {}
    for n, r in zip(BIG_NAMES, recv):
        shp = params[n].shape
        rows = math.prod(shp[:-1])
        res = _adamw(r.reshape(N_DEV, rows, shp[-1]), params[n].reshape(rows, shp[-1]),
                     mom[n].reshape(rows, shp[-1]), var[n].reshape(rows, shp[-1]), 32, "adamw_" + n)
        out[n] = [t.reshape(shp) for t in res]
    full_shapes = [SMALL_FULL[n] for n in SMALL_NAMES]
    summed = _small_exchange(_pack([small_grads[n] for n in SMALL_NAMES], _rows_for(full_shapes)), reduce=True)
    local_g = []
    for n, g in zip(SMALL_NAMES, _unpack(summed, full_shapes)):
        if n in SMALL_SHARDED:
            size = params[n].shape[2]
            g = lax.dynamic_slice_in_dim(g, me * size, size, axis=2)
        local_g.append(g)
    local_shapes = [params[n].shape for n in SMALL_NAMES]
    rows = _rows_for(local_shapes)
    res = _adamw(_pack(local_g, rows)[None], _pack([params[n] for n in SMALL_NAMES], rows),
                 _pack([mom[n] for n in SMALL_NAMES], rows), _pack([var[n] for n in SMALL_NAMES], rows), rows,
                 "adamw_small")
    unpacked = [_unpack(t, local_shapes) for t in res]
    for i, n in enumerate(SMALL_NAMES):
        out[n] = [unpacked[k][i] for k in range(4)]

    order = ("a_w_in", "a_conv_w", "a_A_log", "a_dt_bias", "a_norm_w", "b_w_in", "b_sinks", "w_kv_shared",
             "mem_w_kv", "w_o", "mlp_w_up", "mlp_w_down", "ln_g", "ln_b")
    return (loss, dx[None], *[out[n][0] for n in order], *[out[n][1] for n in order],
            *[out[n][2] for n in order], *[out[n][3] for n in order])
```

```python
import functools
import math

import jax
import jax.numpy as jnp
from jax import lax
from jax.experimental import pallas as pl
from jax.experimental.pallas import tpu as pltpu

F32 = jnp.float32
BF16 = jnp.bfloat16

D_MODEL = 1024
DEPTH = 4
N_A = 2
MEM_HEADS = 4
MEM_DH = 64
MEM_W = 256
DN_HEADS = 6
DN_D = 128
DN_W = 768
CHUNK = 64
SWA_DH = 64
SWA_HEADS = 12
WINDOW = 128
ROPE_THETA = 10000.0
LN_EPS = 1e-5
NORM_EPS = 1e-6
DN_ALPHA = (2.0 * DEPTH) ** 0.25
A_IN = 3340
A_IN_PAD = 3456
N_DEV = 8

ADAM_LR = 0.001
ADAM_B1 = 0.9
ADAM_B2 = 0.999
ADAM_EPS = 1e-08
ADAM_WD = 0.01
ADAM_STEP = 10

VMEM_LIMIT = 52 * 1024 * 1024
NEG_BIG = -1e30


def _cparams(sem):
    return pltpu.CompilerParams(dimension_semantics=sem, vmem_limit_bytes=VMEM_LIMIT)


_CONTRACT = {"nn": (1, 0), "nt": (1, 1), "tn": (0, 0)}


def _raw_mm(a, b, mode, prec):
    ca, cb = _CONTRACT[mode]
    dims = (((ca,), (cb,)), ((), ()))
    dot = lambda p, q: lax.dot_general(p, q, dims, preferred_element_type=F32)
    if prec == "bf16":
        return dot(a.astype(BF16), b.astype(BF16))
    a, b = a.astype(F32), b.astype(F32)
    a_hi, b_hi = a.astype(BF16), b.astype(BF16)
    if prec == "sela":
        return dot(a_hi, b_hi) + dot(a_hi, (b - b_hi.astype(F32)).astype(BF16))
    a_lo = (a - a_hi.astype(F32)).astype(BF16)
    if prec == "selb":
        return dot(a_hi, b_hi) + dot(a_lo, b_hi)
    b_lo = (b - b_hi.astype(F32)).astype(BF16)
    return dot(a_hi, b_hi) + (dot(a_hi, b_lo) + dot(a_lo, b_hi))


@functools.partial(jax.custom_vjp, nondiff_argnums=(2, 3))
def mm(a, b, mode, prec):
    return _raw_mm(a, b, mode, prec)


def _mm_fwd(a, b, mode, prec):
    return _raw_mm(a, b, mode, prec), (a, b)


def _mm_bwd(mode, prec, res, ct):
    a, b = res
    if prec == "sela":
        pa, pb = "f32", {"nn": "sela", "nt": "selb", "tn": "sela"}[mode]
    elif prec == "selb":
        pa, pb = {"nn": "selb", "nt": "selb", "tn": "sela"}[mode], "f32"
    else:
        pa = pb = prec
    if mode == "nn":
        return mm(ct, b, "nt", pa), mm(a, ct, "tn", pb)
    if mode == "nt":
        return mm(ct, b, "nn", pa), mm(ct, a, "tn", pb)
    return mm(b, ct, "nt", pa), mm(a, ct, "nn", pb)


mm.defvjp(_mm_fwd, _mm_bwd)


@jax.custom_vjp
def _softplus(x):
    y = jnp.exp(-jnp.abs(x))
    log1p_y = jnp.where(y < 1e-2, y * (1.0 - y * (0.5 - y * (1.0 / 3.0))), jnp.log(1.0 + y))
    return jnp.maximum(x, 0.0) + log1p_y


def _softplus_fwd(x):
    return _softplus(x), x


def _softplus_bwd(x, ct):
    return (ct * jax.nn.sigmoid(x),)


_softplus.defvjp(_softplus_fwd, _softplus_bwd)


def _iota(shape, dim):
    return lax.broadcasted_iota(jnp.int32, shape, dim)


class _Ride:
    def __init__(self, srcs, dst_shapes, src_view, dst_view):
        self.srcs, self.dst_shapes, self.src_view, self.dst_view = list(srcs), list(dst_shapes), src_view, dst_view
        self.n = len(self.srcs)
        self.any_specs = [pl.BlockSpec(memory_space=pl.ANY)] * self.n
        self.scratch = [pltpu.SemaphoreType.DMA((self.n, N_DEV - 1)), pltpu.SemaphoreType.DMA((self.n, N_DEV - 1)),
                        pltpu.SemaphoreType.DMA((self.n,))]

    def copies(self, src_refs, dst_refs, sems):
        send_sems, recv_sems, local_sems = sems
        me, peers = _me_and_peers()
        local, out, inc = [], [], []
        for t in range(self.n):
            local.append(pltpu.make_async_copy(self.src_view(t, src_refs[t], me), self.dst_view(t, dst_refs[t], me),
                                               local_sems.at[t]))
            for k, (dev, idx) in enumerate(peers):
                mk = lambda s, d: pltpu.make_async_remote_copy(
                    src_ref=s, dst_ref=d, send_sem=send_sems.at[t, k], recv_sem=recv_sems.at[t, k], device_id=dev,
                    device_id_type=pl.DeviceIdType.MESH)
                out.append(mk(self.src_view(t, src_refs[t], idx), self.dst_view(t, dst_refs[t], me)))
                inc.append(mk(self.src_view(t, src_refs[t], me), self.dst_view(t, dst_refs[t], idx)))
        return local, out, inc

    def start(self, grid, src_refs, dst_refs, sems):
        first = pl.program_id(0) == 0
        for a in range(1, len(grid)):
            first = jnp.logical_and(first, pl.program_id(a) == 0)

        @pl.when(first)
        def _():
            local, out, _ = self.copies(src_refs, dst_refs, sems)
            for cp in local + out:
                cp.start()

    def finish(self, grid, src_refs, dst_refs, sems):
        last = pl.program_id(0) == grid[0] - 1
        for a in range(1, len(grid)):
            last = jnp.logical_and(last, pl.program_id(a) == grid[a] - 1)

        @pl.when(last)
        def _():
            local, out, inc = self.copies(src_refs, dst_refs, sems)
            for cp in inc:
                cp.wait_recv()
            for cp in out:
                cp.wait_send()
            for cp in local:
                cp.wait()


def _block_fwd(fn, ins, in_specs, out_shapes, out_specs, grid, name, ride=None):
    n_in, n_out = len(ins), len(out_shapes)
    n_ride = ride.n if ride else 0

    def body(*refs):
        pids = tuple(pl.program_id(a) for a in range(len(grid)))
        ride_refs = (refs[n_in:n_in + n_ride], refs[n_in + n_ride + n_out:n_in + 2 * n_ride + n_out],
                     refs[n_in + 2 * n_ride + n_out:])
        if ride:
            ride.start(grid, *ride_refs)
        vals = [r[...].astype(F32) for r in refs[:n_in]]
        outs = fn(pids, *vals)
        for r, o in zip(refs[n_in + n_ride:n_in + n_ride + n_out], outs):
            r[...] = o.astype(r.dtype)
        if ride:
            ride.finish(grid, *ride_refs)

    if not ride:
        return pl.pallas_call(
            body, grid=grid, in_specs=in_specs, out_specs=out_specs, out_shape=out_shapes, name=name,
            compiler_params=_cparams(("parallel",) * len(grid)))(*ins)
    res = pl.pallas_call(
        body, grid=grid, in_specs=list(in_specs) + ride.any_specs, out_specs=list(out_specs) + ride.any_specs,
        out_shape=list(out_shapes) + ride.dst_shapes, scratch_shapes=ride.scratch, name=name,
        compiler_params=_cparams(("arbitrary",) * len(grid)))(*ins, *ride.srcs)
    return res[:n_out], res[n_out:]


def _block_bwd(fn, ins, in_specs, cts, ct_specs, kinds, g_shapes, g_specs, grid, name, ride=None):
    n_in, n_ct, n_g = len(ins), len(cts), len(g_shapes)
    n_ride = ride.n if ride else 0
    didx = [i for i, k in enumerate(kinds) if k]

    def body(*refs):
        in_refs, ct_refs = refs[:n_in], refs[n_in:n_in + n_ct]
        base = n_in + n_ct
        g_refs = refs[base + n_ride:base + n_ride + n_g]
        ride_refs = (refs[base:base + n_ride], refs[base + n_ride + n_g:base + 2 * n_ride + n_g],
                     refs[base + 2 * n_ride + n_g:])
        if ride:
            ride.start(grid, *ride_refs)
        pids = tuple(pl.program_id(a) for a in range(len(grid)))
        vals = [r[...].astype(F32) for r in in_refs]

        def f(*dvals):
            full = list(vals)
            for i, v in zip(didx, dvals):
                full[i] = v
            return tuple(fn(pids, *full))

        _, vjp = jax.vjp(f, *[vals[i] for i in didx])
        gs = vjp(tuple(r[...].astype(F32) for r in ct_refs))
        first = pids[0] == 0
        for p in pids[1:]:
            first = jnp.logical_and(first, p == 0)
        for i, g, r in zip(didx, gs, g_refs):
            if kinds[i] == "s":
                r[...] = g.astype(r.dtype)
            else:
                @pl.when(first)
                def _(r=r):
                    r[...] = jnp.zeros(r.shape, r.dtype)

                r[...] += g.astype(r.dtype)
        if ride:
            ride.finish(grid, *ride_refs)

    sem = ("arbitrary",) * len(grid) if "a" in kinds or ride else ("parallel",) * len(grid)
    if not ride:
        return pl.pallas_call(
            body, grid=grid, in_specs=list(in_specs) + list(ct_specs), out_specs=g_specs, out_shape=g_shapes,
            name=name, compiler_params=_cparams(sem))(*ins, *cts)
    res = pl.pallas_call(
        body, grid=grid, in_specs=list(in_specs) + list(ct_specs) + ride.any_specs,
        out_specs=list(g_specs) + ride.any_specs, out_shape=list(g_shapes) + ride.dst_shapes,
        scratch_shapes=ride.scratch, name=name, compiler_params=_cparams(sem))(*ins, *cts, *ride.srcs)
    return res[:n_g], res[n_g:]


def _rows(tb, width, col=0):
    return pl.BlockSpec((tb, width), lambda i, col=col: (i, col))


def _whole(shape):
    return pl.BlockSpec(shape, lambda *_: (0,) * len(shape))


def _sds(shape, dtype=F32):
    return jax.ShapeDtypeStruct(shape, dtype)


def _matmul(a, b, mode, out_dtypes, name, epi=None, extras=(), tm=1024, tn=1024, tk=1024,
            b_spec=None, n_total=None, out_specs=None, out_shapes=None):
    if mode == "tn":
        k_total, m_total = a.shape
    else:
        m_total, k_total = a.shape
    if n_total is None:
        n_total = b.shape[0] if mode == "nt" else b.shape[1]
    tm, tn, tk = min(tm, m_total), min(tn, n_total), min(tk, k_total)
    assert m_total % tm == 0 and n_total % tn == 0 and k_total % tk == 0, (name, a.shape, b.shape)
    grid = (m_total // tm, n_total // tn, k_total // tk)
    nk = grid[2]
    if mode == "tn":
        a_spec = pl.BlockSpec((tk, tm), lambda i, j, k: (k, i))
    else:
        a_spec = pl.BlockSpec((tm, tk), lambda i, j, k: (i, k))
    if b_spec is None:
        if mode == "nt":
            b_spec = pl.BlockSpec((tn, tk), lambda i, j, k: (j, k))
        else:
            b_spec = pl.BlockSpec((tk, tn), lambda i, j, k: (k, j))
    tile = pl.BlockSpec((tm, tn), lambda i, j, k: (i, j))
    n_ex, n_out = len(extras), len(out_dtypes)
    ca, cb = _CONTRACT[mode]
    dims = (((ca,), (cb,)), ((), ()))

    def body(*refs):
        a_ref, b_ref = refs[:2]
        ex_refs = refs[2:2 + n_ex]
        out_refs = refs[2 + n_ex:2 + n_ex + n_out]
        part = lax.dot_general(a_ref[...], b_ref[...], dims, preferred_element_type=F32)

        def finish(val):
            res = epi(val, *[e[...] for e in ex_refs]) if epi is not None else (val,)
            for r, o in zip(out_refs, res):
                r[...] = o.astype(r.dtype)

        if nk == 1:
            finish(part)
        else:
            acc = refs[-1]
            k = pl.program_id(2)

            @pl.when(k == 0)
            def _():
                acc[...] = part

            @pl.when(k > 0)
            def _():
                acc[...] += part

            @pl.when(k == nk - 1)
            def _():
                finish(acc[...])

    if out_shapes is None:
        out_shapes = [_sds((m_total, n_total), d) for d in out_dtypes]
        out_specs = [tile] * n_out
    outs = pl.pallas_call(
        body, grid=grid, in_specs=[a_spec, b_spec] + [tile] * n_ex, out_specs=out_specs, out_shape=out_shapes,
        scratch_shapes=[pltpu.VMEM((tm, tn), F32)] if nk > 1 else [], name=name,
        compiler_params=_cparams(("parallel", "parallel", "arbitrary")))(a, b, *extras)
    return outs if n_out > 1 else outs[0]


def _silu(x):
    return x * jax.nn.sigmoid(x)


def _rowa_fn(pids, c, ab, alog, dtb):
    tb = c.shape[0]
    s = _silu(c)
    qs, ks = [], []
    for h in range(DN_HEADS):
        qh = s[:, DN_D * h:DN_D * (h + 1)]
        qs.append(qh * lax.rsqrt(jnp.sum(qh * qh, axis=-1, keepdims=True) + NORM_EPS) * (DN_D ** -0.5))
        kh = s[:, DN_W + DN_D * h:DN_W + DN_D * (h + 1)]
        ks.append(kh * lax.rsqrt(jnp.sum(kh * kh, axis=-1, keepdims=True) + NORM_EPS))
    q = jnp.concatenate(qs, axis=1)
    k = jnp.concatenate(ks, axis=1)
    v = s[:, 2 * DN_W:3 * DN_W]
    g128 = -jnp.exp(alog) * _softplus(ab + dtb)
    b128 = jax.nn.sigmoid(ab)
    r, cc = _iota((tb, tb), 0), _iota((tb, tb), 1)
    tri = jnp.where(((r >> 6) == (cc >> 6)) & (r >= cc), 1.0, 0.0)
    gc128 = mm(tri, g128, "nn", "sela")
    lane, col = _iota((128, DN_W), 0), _iota((128, DN_W), 1)
    exp_a = jnp.where(lane == (col >> 7), 1.0, 0.0)
    exp_b = jnp.where(lane == (col >> 7) + DN_HEADS, 1.0, 0.0)
    return q, k, v, mm(gc128, exp_a, "nn", "selb"), mm(b128, exp_b, "nn", "selb")


def _tri_inv_raw(lows, block):
    n = lows[0].shape[0]
    r, c = _iota((n, n), 0), _iota((n, n), 1)
    lg = 0
    xs = None
    while (1 << lg) < block:
        off = ((r >> (lg + 1)) == (c >> (lg + 1))) & (((r >> lg) & 1) == 1) & (((c >> lg) & 1) == 0)
        cblks = [jnp.where(off, low, 0.0) for low in lows]
        if xs is None:
            xs = [jnp.where(r == c, 1.0, 0.0) - cb for cb in cblks]
        else:
            ys = [mm(cb, x, "nn", "f32") for cb, x in zip(cblks, xs)]
            xs = [x - mm(x, y, "nn", "f32") for x, y in zip(xs, ys)]
        lg += 1
    return tuple(xs)


def _tri_inv_cotangent(block, xs, cts):
    n = xs[0].shape[0]
    r, c = _iota((n, n), 0), _iota((n, n), 1)
    shift = block.bit_length() - 1
    keep = ((r >> shift) == (c >> shift)) & (r > c)
    gs = [mm(x, ct, "tn", "f32") for x, ct in zip(xs, cts)]
    gs = [mm(g, x, "nt", "f32") for g, x in zip(gs, xs)]
    return tuple(jnp.where(keep, -g, 0.0) for g in gs)


@functools.partial(jax.custom_vjp, nondiff_argnums=(1,))
def _tri_inv(lows, block):
    return _tri_inv_raw(lows, block)


def _tri_inv_fwd(lows, block):
    xs = _tri_inv_raw(lows, block)
    return xs, xs


def _tri_inv_bwd(block, xs, cts):
    return (_tri_inv_cotangent(block, xs, cts),)


_tri_inv.defvjp(_tri_inv_fwd, _tri_inv_bwd)


@functools.partial(jax.custom_vjp, nondiff_argnums=(2,))
def _tri_inv_known(lows, known, block):
    return known


def _tri_inv_known_fwd(lows, known, block):
    return known, known


def _tri_inv_known_bwd(block, xs, cts):
    return _tri_inv_cotangent(block, xs, cts), tuple(jnp.zeros_like(x) for x in xs)


_tri_inv_known.defvjp(_tri_inv_known_fwd, _tri_inv_known_bwd)


PAIR = 2 * CHUNK


def _dn1_pairs(q, k, v, gc, beta, tinv_known=None):
    assert PAIR == DN_D
    n = PAIR
    pairs = range(q.shape[0] // n)
    cut = lambda t: [t[n * j:n * (j + 1)] for j in pairs]
    q, k, v, gc, beta = cut(q), cut(k), cut(v), cut(gc), cut(beta)
    onehot = jnp.where(_iota((n, DN_D), 1) == 0, 1.0, 0.0)
    r, c = _iota((n, n), 0), _iota((n, n), 1)
    same = (r >> 6) == (c >> 6)
    incl, strict = same & (r >= c), same & (r > c)
    row = _iota((n, DN_D), 0)
    eg = [jnp.exp(g) for g in gc]
    kb = [k[j] * beta[j] for j in pairs]
    g_row = [mm(onehot, g, "nt", "sela") for g in gc]
    kk = [mm(kb[j], k[j], "nt", "bf16") for j in pairs]
    qk = [mm(q[j], k[j], "nt", "bf16") for j in pairs]
    decay = [jnp.exp(jnp.where(incl, gc[j] - g_row[j], NEG_BIG)) for j in pairs]
    low = tuple(jnp.where(strict, kk[j] * decay[j], 0.0) for j in pairs)
    if tinv_known is None:
        tinv = _tri_inv(low, CHUNK)
    else:
        tinv = _tri_inv_known(low, tuple(cut(tinv_known)), CHUNK)
    uw = [mm(tinv[j], jnp.concatenate([v[j] * beta[j], kb[j] * eg[j]], axis=1), "nn", "f32") for j in pairs]
    intra = [jnp.where(incl, qk[j] * decay[j], 0.0) for j in pairs]
    g_last = []
    for g in gc:
        last0 = jnp.sum(jnp.where(row == CHUNK - 1, g, 0.0), axis=0, keepdims=True)
        last1 = jnp.sum(jnp.where(row == PAIR - 1, g, 0.0), axis=0, keepdims=True)
        g_last.append(jnp.where(row < CHUNK, last0, last1))
    join = lambda parts: jnp.concatenate(parts, axis=0)
    return (join([t[:, :DN_D] for t in uw]), join([t[:, DN_D:] for t in uw]), join(intra),
            join([q[j] * eg[j] for j in pairs]), join([k[j] * jnp.exp(g_last[j] - gc[j]) for j in pairs]),
            join([jnp.exp(g) for g in g_last]), join(list(tinv)))


def _dn1_fn(pids, q, k, v, gc, beta):
    return _dn1_pairs(q, k, v, gc, beta)


def _dn1_fn_known(pids, q, k, v, gc, beta, tinv):
    return _dn1_pairs(q, k, v, gc, beta, tinv)[:6]


def _dn2_step(half, state, qd, kd, u, w, intra, cd_row):
    heads = range(len(state))
    v_new = [u[h] - mm(w[h], state[h], "nn", "bf16") for h in heads]
    zeros = jnp.zeros_like(v_new[0])
    v_pair = [jnp.concatenate([v, zeros] if half == 0 else [zeros, v], axis=0) for v in v_new]
    from_state = [mm(qd[h], state[h], "nn", "bf16") for h in heads]
    out = tuple(from_state[h] + mm(intra[h], v_pair[h], "nn", "bf16") for h in heads)
    return out, tuple(state[h] * cd_row[h] + mm(kd[h], v_new[h], "tn", "bf16") for h in heads)


def _post_fn(pids, o, z, nw):
    outs = []
    for h in range(DN_HEADS):
        oh = o[:, DN_D * h:DN_D * (h + 1)]
        zh = z[:, DN_D * h:DN_D * (h + 1)]
        y = oh * lax.rsqrt(jnp.mean(oh * oh, axis=-1, keepdims=True) + NORM_EPS) * nw
        outs.append(y * _silu(zh))
    return (jnp.concatenate(outs, axis=1),)


def _memattn_fn(pids, qm, kvm):
    kmem, vmem = kvm[:, :MEM_W], kvm[:, MEM_W:]
    lane = _iota((1, MEM_W), 1)
    heads = range(MEM_HEADS)
    hm = [jnp.where((lane >> 6) == h, 1.0, 0.0) for h in heads]
    s = [mm(qm * (hm[h] * MEM_DH ** -0.5), kmem, "nt", "bf16") for h in heads]
    e = [jnp.exp(t - lax.stop_gradient(jnp.max(t, axis=-1, keepdims=True))) for t in s]
    o = [mm(e[h], vmem, "nn", "bf16") * (hm[h] / jnp.sum(e[h], axis=-1, keepdims=True)) for h in heads]
    return ((o[0] + o[1]) + (o[2] + o[3]),)


def _ln_fn(pids, h, mix, g, b):
    x = DN_ALPHA * h + mix
    mu = jnp.mean(x, axis=-1, keepdims=True)
    xc = x - mu
    var = jnp.mean(xc * xc, axis=-1, keepdims=True)
    return (xc * lax.rsqrt(var + LN_EPS) * g + b,)


def _rope_matrix():
    i, j = _iota((128, 128), 0), _iota((128, 128), 1)
    jj = j & 63
    return jnp.where((jj < 32) & (i == j + 32), -1.0, 0.0) + jnp.where((jj >= 32) & (i == j - 32), 1.0, 0.0)


def _rope128(x, cos, sin, rot):
    return x * cos + mm(x, rot, "nn", "selb") * sin


def _krope_fn(pids, kraw, cos, sin):
    rot = _rope_matrix()
    return (jnp.concatenate([_rope128(kraw[:, 128 * g:128 * (g + 1)], cos, sin, rot) for g in range(2)], axis=1),)


def _swa_fn(pids, qraw, cos, sin, k_halo, k_cur, v_halo, v_cur, sinks):
    tb = qraw.shape[0]
    nwin = tb // WINDOW
    rot = _rope_matrix()
    kcat = jnp.concatenate([k_halo, k_cur], axis=0)
    vcat = jnp.concatenate([v_halo, v_cur], axis=0)
    lane = _iota((1, 128), 1)
    halves = (jnp.where(lane < 64, 1.0, 0.0), jnp.where(lane >= 64, 1.0, 0.0))
    group = SWA_HEADS // 2
    rows = group * WINDOW
    in_cur = _iota((rows, WINDOW), 1) <= (_iota((rows, WINDOW), 0) & (WINDOW - 1))
    qg = [_rope128(qraw[:, 128 * p:128 * (p + 1)], cos, sin, rot) for p in range(group)]
    sink = []
    for kv in range(2):
        cols = [jnp.sum(jnp.where(lane == group * kv + i, sinks, 0.0), axis=-1, keepdims=True)
                + jnp.zeros((WINDOW, 1), F32) for i in range(group)]
        sink.append(jnp.concatenate(cols, axis=0))
    units = [(w, kv) for w in range(nwin) for kv in range(2)]
    n_units = range(len(units))
    q6 = [jnp.concatenate([qg[3 * kv + i // 2][WINDOW * w:WINDOW * (w + 1)] * (halves[i % 2] * SWA_DH ** -0.5)
                           for i in range(group)], axis=0) for w, kv in units]
    blk = lambda cat, w, kv: cat[WINDOW * w:WINDOW * (w + 1), 128 * kv:128 * (kv + 1)]
    s_prev = [mm(q6[u], blk(kcat, w, kv), "nt", "bf16") for u, (w, kv) in enumerate(units)]
    s_cur = [mm(q6[u], blk(kcat, w + 1, kv), "nt", "bf16") for u, (w, kv) in enumerate(units)]
    s = [jnp.where(in_cur, s_cur[u], jnp.where(pids[0] * nwin + w > 0, s_prev[u], NEG_BIG))
         for u, (w, kv) in enumerate(units)]
    m = [lax.stop_gradient(jnp.maximum(jnp.max(s[u], axis=-1, keepdims=True), sink[kv]))
         for u, (w, kv) in enumerate(units)]
    e = [jnp.exp(s[u] - m[u]) for u in n_units]
    denom = [jnp.sum(e[u], axis=-1, keepdims=True) + jnp.exp(sink[kv] - m[u]) for u, (w, kv) in enumerate(units)]
    o = [(mm(jnp.where(in_cur, e[u], 0.0), blk(vcat, w + 1, kv), "nn", "bf16")
          + mm(jnp.where(in_cur, 0.0, e[u]), blk(vcat, w, kv), "nn", "bf16")) / denom[u]
         for u, (w, kv) in enumerate(units)]
    out_rows = []
    for w in range(nwin):
        lanes = []
        for p in range(group):
            ou = o[units.index((w, p // 3))]
            i = 2 * (p % 3)
            lanes.append(ou[WINDOW * i:WINDOW * (i + 1)] * halves[0] + ou[WINDOW * (i + 1):WINDOW * (i + 2)] * halves[1])
        out_rows.append(jnp.concatenate(lanes, axis=1))
    return (jnp.concatenate(out_rows, axis=0),)


def _conv_fwd(proj, conv_w, tb):
    t_total = proj.shape[0]
    width = conv_w.shape[1]
    nb = t_total // tb

    def body(cur_ref, prev_ref, w_ref, out_ref):
        i = pl.program_id(0)
        prev = jnp.where(i > 0, prev_ref[...], 0.0)
        xcat = jnp.concatenate([prev, cur_ref[...]], axis=0)
        acc = xcat[8:] * w_ref[3:4, :]
        for j in range(3):
            acc = acc + pltpu.roll(xcat, 3 - j, 0)[8:] * w_ref[j:j + 1, :]
        out_ref[...] = acc

    return pl.pallas_call(
        body, grid=(nb,),
        in_specs=[pl.BlockSpec((tb, width), lambda i: (i, 0)),
                  pl.BlockSpec((8, width), lambda i: (jnp.maximum(i * (tb // 8) - 1, 0), 0)),
                  _whole((4, width))],
        out_specs=pl.BlockSpec((tb, width), lambda i: (i, 0)), out_shape=_sds((t_total, width)),
        name="conv_fwd", compiler_params=_cparams(("parallel",)))(proj, proj, conv_w)


def _conv_bwd(dc, proj, conv_w, tb):
    t_total, width = dc.shape
    nb = t_total // tb

    def body(dcur_ref, dnext_ref, cur_ref, prev_ref, w_ref, dx_ref, dw_ref):
        i = pl.program_id(0)
        dnext = jnp.where(i < nb - 1, dnext_ref[...], 0.0)
        dcur = dcur_ref[...]
        dcat = jnp.concatenate([dcur, dnext], axis=0)
        prev = jnp.where(i > 0, prev_ref[...], 0.0)
        xcat = jnp.concatenate([prev, cur_ref[...]], axis=0)

        @pl.when(i == 0)
        def _():
            dw_ref[...] = jnp.zeros(dw_ref.shape, F32)

        dx = dcur * w_ref[3:4, :]
        dw_ref[3:4, :] += jnp.sum(dcur * xcat[8:], axis=0, keepdims=True)
        for j in range(3):
            dx = dx + pltpu.roll(dcat, 8 - (3 - j), 0)[8:] * w_ref[j:j + 1, :]
            dw_ref[j:j + 1, :] += jnp.sum(dcur * pltpu.roll(xcat, 3 - j, 0)[8:], axis=0, keepdims=True)
        dx_ref[...] = dx.astype(dx_ref.dtype)

    return pl.pallas_call(
        body, grid=(nb,),
        in_specs=[pl.BlockSpec((tb, width), lambda i: (i, 0)),
                  pl.BlockSpec((8, width), lambda i: (jnp.minimum((i + 1) * (tb // 8), t_total // 8 - 1), 0)),
                  pl.BlockSpec((tb, width), lambda i: (i, 0)),
                  pl.BlockSpec((8, width), lambda i: (jnp.maximum(i * (tb // 8) - 1, 0), 0)),
                  _whole((4, width))],
        out_specs=[pl.BlockSpec((tb, width), lambda i: (i, 0)), _whole((4, width))],
        out_shape=[_sds((t_total, width), BF16), _sds((4, width))],
        name="conv_bwd", compiler_params=_cparams(("arbitrary",)))(dc, dc, proj, proj, conv_w)


def _head_spec(tb, nb=None):
    if nb is None:
        return pl.BlockSpec((tb, DN_D), lambda h, i: (i, h))
    return pl.BlockSpec((tb, DN_D), lambda h, i: (nb - 1 - i, h))


def _intra_spec(tb, nb=None):
    if nb is None:
        return pl.BlockSpec((None, tb, PAIR), lambda h, i: (h, i, 0))
    return pl.BlockSpec((None, tb, PAIR), lambda h, i: (h, nb - 1 - i, 0))


def _state_spec(tb, nb=None):
    if nb is None:
        return pl.BlockSpec((None, tb // CHUNK, DN_D, DN_D), lambda h, i: (h, i, 0, 0))
    return pl.BlockSpec((None, tb // CHUNK, DN_D, DN_D), lambda h, i: (h, nb - 1 - i, 0, 0))


def _scan_specs(tb, nb=None):
    blk = (lambda i: i) if nb is None else (lambda i: nb - 1 - i)
    rows = pl.BlockSpec((tb, DN_W), lambda i: (blk(i), 0))
    pair = pl.BlockSpec((DN_HEADS, tb, PAIR), lambda i: (0, blk(i), 0))
    states = pl.BlockSpec((DN_HEADS, tb // CHUNK, DN_D, DN_D), lambda i: (0, blk(i), 0, 0))
    return rows, pair, states


def _dn2_fwd(qd, kd, u, w, intra, cd, tb):
    t_total = qd.shape[0]
    rows, pair, states = _scan_specs(tb)

    def body(qd_ref, kd_ref, u_ref, w_ref, a_ref, cd_ref, o_ref, save_ref, state):
        @pl.when(pl.program_id(0) == 0)
        def _():
            state[...] = jnp.zeros(state.shape, F32)

        heads = range(DN_HEADS)
        lanes = [pl.ds(DN_D * h, DN_D) for h in heads]
        for j in range(tb // CHUNK):
            sl = pl.ds(CHUNK * j, CHUNK)
            s0 = tuple(state[h] for h in heads)
            for h in heads:
                save_ref[h, j] = s0[h]
            per_head = lambda ref: tuple(ref[sl, lanes[h]] for h in heads)
            out, s1 = _dn2_step(j % 2, s0, per_head(qd_ref), per_head(kd_ref), per_head(u_ref), per_head(w_ref),
                                tuple(a_ref[h, sl, :] for h in heads),
                                tuple(cd_ref[pl.ds(CHUNK * j, 1), lanes[h]] for h in heads))
            for h in heads:
                o_ref[sl, lanes[h]] = out[h]
                state[h] = s1[h]

    return pl.pallas_call(
        body, grid=(t_total // tb,), in_specs=[rows, rows, rows, rows, pair, rows],
        out_specs=[rows, states],
        out_shape=[_sds((t_total, DN_W)), _sds((DN_HEADS, t_total // CHUNK, DN_D, DN_D))],
        scratch_shapes=[pltpu.VMEM((DN_HEADS, DN_D, DN_D), F32)], name="dn2_fwd",
        compiler_params=_cparams(("arbitrary",)))(qd, kd, u, w, intra, cd)


def _dn2_bwd(qd, kd, u, w, intra, cd, saved, d_o, tb):
    t_total = qd.shape[0]
    nb = t_total // tb
    rows, pair, states = _scan_specs(tb, nb)

    def body(qd_ref, kd_ref, u_ref, w_ref, a_ref, cd_ref, save_ref, do_ref,
             dqd_ref, dkd_ref, du_ref, dw_ref, da_ref, dcd_ref, dstate):
        @pl.when(pl.program_id(0) == 0)
        def _():
            dstate[...] = jnp.zeros(dstate.shape, F32)

        first_row = _iota((CHUNK, DN_D), 0) == 0
        heads = range(DN_HEADS)
        lanes = [pl.ds(DN_D * h, DN_D) for h in heads]
        for j in reversed(range(tb // CHUNK)):
            sl = pl.ds(CHUNK * j, CHUNK)
            per_head = lambda ref: tuple(ref[sl, lanes[h]] for h in heads)
            _, vjp = jax.vjp(functools.partial(_dn2_step, j % 2), tuple(save_ref[h, j] for h in heads),
                             per_head(qd_ref), per_head(kd_ref), per_head(u_ref), per_head(w_ref),
                             tuple(a_ref[h, sl, :] for h in heads),
                             tuple(cd_ref[pl.ds(CHUNK * j, 1), lanes[h]] for h in heads))
            ds0, dqd, dkd, du, dw, da, dcd = vjp((per_head(do_ref), tuple(dstate[h] for h in heads)))
            for h in heads:
                dqd_ref[sl, lanes[h]] = dqd[h]
                dkd_ref[sl, lanes[h]] = dkd[h]
                du_ref[sl, lanes[h]] = du[h]
                dw_ref[sl, lanes[h]] = dw[h]
                da_ref[h, sl, :] = da[h]
                dcd_ref[sl, lanes[h]] = jnp.where(first_row, dcd[h], 0.0)
                dstate[h] = ds0[h]

    full = _sds((t_total, DN_W))
    return pl.pallas_call(
        body, grid=(nb,),
        in_specs=[rows, rows, rows, rows, pair, rows, states, rows],
        out_specs=[rows, rows, rows, rows, pair, rows],
        out_shape=[full, full, full, full, _sds((DN_HEADS, t_total, PAIR)), full],
        scratch_shapes=[pltpu.VMEM((DN_HEADS, DN_D, DN_D), F32)], name="dn2_bwd",
        compiler_params=_cparams(("arbitrary",)))(qd, kd, u, w, intra, cd, saved, d_o)


def _loss_and_grad(y, target, tb):
    t_total, d = y.shape

    def body(y_ref, t_ref, dy_ref, acc_ref):
        @pl.when(pl.program_id(0) == 0)
        def _():
            acc_ref[...] = jnp.zeros(acc_ref.shape, F32)

        err = y_ref[...] - t_ref[...]
        dy_ref[...] = err * (1.0 / d)
        acc_ref[...] += jnp.sum(err * err, axis=0, keepdims=True)

    dy, acc = pl.pallas_call(
        body, grid=(t_total // tb,), in_specs=[_rows(tb, d), _rows(tb, d)],
        out_specs=[_rows(tb, d), _whole((1, d))], out_shape=[_sds((t_total, d)), _sds((1, d))],
        name="loss", compiler_params=_cparams(("arbitrary",)))(y, target)
    return 0.5 * jnp.sum(acc) / d, dy


def _halo_sum(mains, halos, tb):
    t_total, width = mains[0].shape
    nb = t_total // tb
    n = len(mains)

    def body(*refs):
        out_ref = refs[-1]
        i = pl.program_id(0)
        tot = refs[0][...]
        for r in refs[1:n]:
            tot = tot + r[...]
        hal = refs[n][...]
        for r in refs[n + 1:2 * n]:
            hal = hal + r[...]
        hal = jnp.where(i < nb - 1, hal, 0.0)
        out_ref[...] = tot + jnp.concatenate([jnp.zeros((tb - WINDOW, width), F32), hal], axis=0)

    return pl.pallas_call(
        body, grid=(nb,),
        in_specs=[_rows(tb, width)] * n
        + [pl.BlockSpec((None, WINDOW, width), lambda i: (jnp.minimum(i + 1, nb - 1), 0, 0))] * n,
        out_specs=_rows(tb, width), out_shape=_sds((t_total, width)), name="halo_sum",
        compiler_params=_cparams(("parallel",)))(*mains, *halos)


def _adamw(recvs, w, m, v, tr, name):
    slots, _, c_total = recvs[0].shape
    r_total = w.shape[0]
    assert sum(r.shape[1] for r in recvs) == r_total
    tr = min([tr] + [r.shape[1] for r in recvs])
    assert all(r.shape[1] % tr == 0 for r in recvs)
    starts = [sum(r.shape[1] for r in recvs[:i]) // tr for i in range(len(recvs))]
    counts = [r.shape[1] // tr for r in recvs]
    c1 = 1.0 / (1.0 - ADAM_B1 ** ADAM_STEP)
    c2 = 1.0 / (1.0 - ADAM_B2 ** ADAM_STEP)

    def body(*refs):
        recv_refs = refs[:len(recvs)]
        w_ref, m_ref, v_ref, g_ref, d_ref, nm_ref, nv_ref = refs[len(recvs):]
        g = None
        for recv_ref, start in zip(recv_refs, starts):
            part = recv_ref[0].astype(F32)
            for s in range(1, slots):
                part = part + recv_ref[s].astype(F32)
            g = part if g is None else jnp.where(pl.program_id(0) >= start, part, g)
        nm = ADAM_B1 * m_ref[...] + (1.0 - ADAM_B1) * g
        nv = ADAM_B2 * v_ref[...] + (1.0 - ADAM_B2) * (g * g)
        g_ref[...] = g
        nm_ref[...] = nm
        nv_ref[...] = nv
        d_ref[...] = -ADAM_LR * ((nm * c1) / (jnp.sqrt(nv * c2) + ADAM_EPS) + ADAM_WD * w_ref[...])

    blk = pl.BlockSpec((tr, c_total), lambda i: (i, 0))
    recv_specs = [pl.BlockSpec((slots, tr, c_total), lambda i, s=s, n=n: (0, jnp.clip(i - s, 0, n - 1), 0))
                  for s, n in zip(starts, counts)]
    return pl.pallas_call(
        body, grid=(r_total // tr,), in_specs=recv_specs + [blk, blk, blk],
        out_specs=[blk] * 4, out_shape=[_sds((r_total, c_total))] * 4, name=name,
        compiler_params=_cparams(("parallel",)))(*recvs, w, m, v)


def _me_and_peers():
    x, y, c = lax.axis_index("x"), lax.axis_index("y"), lax.axis_index("c")
    me = 4 * x + 2 * y + c
    peers = []
    for k in range(1, N_DEV):
        px = 1 - x if (k >> 2) & 1 else x
        py = 1 - y if (k >> 1) & 1 else y
        pc = 1 - c if k & 1 else c
        peers.append(((px, py, pc), 4 * px + 2 * py + pc))
    return me, peers


def _small_exchange(packed, reduce):
    r_total = packed.shape[0]

    def body(p_ref, out_ref, gath_ref, send_sems, recv_sems):
        me, peers = _me_and_peers()
        gath_ref[me] = p_ref[...]
        copies = []
        for k, (dev, _) in enumerate(peers):
            cp = pltpu.make_async_remote_copy(src_ref=p_ref, dst_ref=gath_ref.at[me], send_sem=send_sems.at[k],
                                              recv_sem=recv_sems.at[k], device_id=dev,
                                              device_id_type=pl.DeviceIdType.MESH)
            cp.start()
            copies.append(cp)
        for k, (dev, idx) in enumerate(peers):
            pltpu.make_async_remote_copy(src_ref=p_ref, dst_ref=gath_ref.at[idx], send_sem=send_sems.at[k],
                                         recv_sem=recv_sems.at[k], device_id=dev,
                                         device_id_type=pl.DeviceIdType.MESH).wait_recv()
        for cp in copies:
            cp.wait_send()
        if reduce:
            tot = gath_ref[0]
            for d in range(1, N_DEV):
                tot = tot + gath_ref[d]
            out_ref[...] = tot
        else:
            out_ref[...] = gath_ref[...]

    out_shape = _sds((r_total, 128)) if reduce else _sds((N_DEV, r_total, 128))
    return pl.pallas_call(
        body, in_specs=[pl.BlockSpec(memory_space=pltpu.VMEM)], out_specs=pl.BlockSpec(memory_space=pltpu.VMEM),
        out_shape=out_shape,
        scratch_shapes=[pltpu.VMEM((N_DEV, r_total, 128), F32), pltpu.SemaphoreType.DMA((N_DEV - 1,)),
                        pltpu.SemaphoreType.DMA((N_DEV - 1,))],
        name="small_allreduce" if reduce else "small_allgather")(packed)


def _slot(ref, axis, idx, size):
    sel = [slice(None)] * len(ref.shape)
    sel[axis] = idx if size is None else pl.ds(pl.multiple_of(idx * size, size), size)
    return ref.at[tuple(sel)]


def _big_exchange(srcs, dst_shapes, src_view, dst_view, name):
    n = len(srcs)

    def body(*refs):
        src_refs, dst_refs = refs[:n], refs[n:2 * n]
        send_sems, recv_sems, local_sems = refs[2 * n:]
        me, peers = _me_and_peers()
        local, remote = [], []
        for t in range(n):
            loc = pltpu.make_async_copy(src_view(t, src_refs[t], me), dst_view(t, dst_refs[t], me), local_sems.at[t])
            loc.start()
            local.append(loc)
            for k, (dev, idx) in enumerate(peers):
                cp = pltpu.make_async_remote_copy(
                    src_ref=src_view(t, src_refs[t], idx), dst_ref=dst_view(t, dst_refs[t], me),
                    send_sem=send_sems.at[t, k], recv_sem=recv_sems.at[t, k], device_id=dev,
                    device_id_type=pl.DeviceIdType.MESH)
                cp.start()
                remote.append(cp)
        for t in range(n):
            for k, (dev, idx) in enumerate(peers):
                pltpu.make_async_remote_copy(
                    src_ref=src_view(t, src_refs[t], me), dst_ref=dst_view(t, dst_refs[t], idx),
                    send_sem=send_sems.at[t, k], recv_sem=recv_sems.at[t, k], device_id=dev,
                    device_id_type=pl.DeviceIdType.MESH).wait_recv()
        for cp in remote:
            cp.wait_send()
        for cp in local:
            cp.wait()

    any_spec = pl.BlockSpec(memory_space=pl.ANY)
    return pl.pallas_call(
        body, in_specs=[any_spec] * n, out_specs=[any_spec] * n, out_shape=dst_shapes,
        scratch_shapes=[pltpu.SemaphoreType.DMA((n, N_DEV - 1)), pltpu.SemaphoreType.DMA((n, N_DEV - 1)),
                        pltpu.SemaphoreType.DMA((n,))],
        name=name)(*srcs)


BIG = {
    "a_w_in": (1, (2, 1024, A_IN)),
    "b_w_in": (1, (2, 1024, 1024)),
    "w_kv_shared": (0, (1024, 256)),
    "mem_w_kv": (1, (4, 1024, 512)),
    "w_o": (1, (4, 1024, 1024)),
    "mlp_w_up": (2, (4, 1024, 4096)),
    "mlp_w_down": (1, (4, 4096, 1024)),
}
BIG_NAMES = tuple(BIG)


def _gather_plan(names, shards):
    dst_shapes, axes, sizes = [], [], []
    for name, s in zip(names, shards):
        axis = BIG[name][0] - (len(BIG[name][1]) - s.ndim)
        if name == "mlp_w_up":
            dst_shapes.append(_sds((N_DEV,) + s.shape, s.dtype))
            axes.append(0)
            sizes.append(None)
        else:
            dst_shapes.append(_sds(tuple(d * N_DEV if a == axis else d for a, d in enumerate(s.shape)), s.dtype))
            axes.append(axis)
            sizes.append(s.shape[axis])
    return dst_shapes, axes, sizes


def _gather_ride(names, shards):
    dst_shapes, axes, sizes = _gather_plan(names, shards)
    return _Ride(shards, dst_shapes, lambda t, ref, idx: ref, lambda t, ref, idx: _slot(ref, axes[t], idx, sizes[t]))


def _scatter_plan(names, grads):
    dst_shapes, axes, sizes = [], [], []
    for name, g in zip(names, grads):
        if name == "mlp_w_up":
            shard = (g.shape[0],) + g.shape[2:]
            axes.append(1)
            sizes.append(None)
        else:
            axis = BIG[name][0] - (len(BIG[name][1]) - g.ndim)
            shard = tuple(d // N_DEV if a == axis else d for a, d in enumerate(g.shape))
            axes.append(axis)
            sizes.append(shard[axis])
        dst_shapes.append(_sds((N_DEV,) + shard, g.dtype))
    return dst_shapes, axes, sizes


def _scatter_ride(names, grads):
    dst_shapes, axes, sizes = _scatter_plan(names, grads)
    return _Ride(grads, dst_shapes, lambda t, ref, idx: _slot(ref, axes[t], idx, sizes[t]),
                 lambda t, ref, idx: ref.at[idx])


def _allgather_weights(names, shards):
    dst_shapes, axes, sizes = _gather_plan(names, shards)
    n = len(shards)

    def body(*refs):
        src_refs, dst_refs = refs[:n], refs[n:2 * n]
        send_sems, recv_sems, local_sems = refs[2 * n:]
        x, y, c = lax.axis_index("x"), lax.axis_index("y"), lax.axis_index("c")
        sibling = (x, y, 1 - c)
        chips = [(1 - x, y), (x, 1 - y), (1 - x, 1 - y)]
        index = lambda px, py, pc: 4 * px + 2 * py + pc

        def copy(t, k, block, to, src=None):
            rows = _slot(dst_refs[t], axes[t], index(*block), sizes[t])
            return pltpu.make_async_remote_copy(
                src_ref=rows if src is None else src, dst_ref=rows, send_sem=send_sems.at[t, k],
                recv_sem=recv_sems.at[t, k], device_id=to, device_id_type=pl.DeviceIdType.MESH)

        started, local = [], []
        for t in range(n):
            mine = pltpu.make_async_copy(src_refs[t], _slot(dst_refs[t], axes[t], index(x, y, c), sizes[t]),
                                         local_sems.at[t])
            mine.start()
            local.append(mine)
            first = [copy(t, 0, (x, y, c), sibling, src=src_refs[t])]
            first += [copy(t, 1 + j, (x, y, c), (*chip, c), src=src_refs[t]) for j, chip in enumerate(chips)]
            for cp in first:
                cp.start()
            started += first
        for t in range(n):
            for j, chip in enumerate(chips):
                copy(t, 1 + j, (*chip, c), (x, y, c)).wait_recv()
                passed = copy(t, 4 + j, (*chip, c), sibling)
                passed.start()
                started.append(passed)
        for t in range(n):
            copy(t, 0, sibling, (x, y, c)).wait_recv()
            for j, chip in enumerate(chips):
                copy(t, 4 + j, (*chip, 1 - c), (x, y, c)).wait_recv()
        for cp in started:
            cp.wait_send()
        for cp in local:
            cp.wait()

    any_spec = pl.BlockSpec(memory_space=pl.ANY)
    return pl.pallas_call(
        body, in_specs=[any_spec] * n, out_specs=[any_spec] * n, out_shape=dst_shapes,
        scratch_shapes=[pltpu.SemaphoreType.DMA((n, N_DEV - 1)), pltpu.SemaphoreType.DMA((n, N_DEV - 1)),
                        pltpu.SemaphoreType.DMA((n,))],
        name="allgather_weights")(*shards)


def _scatter_grads(names, grads):
    dst_shapes, axes, sizes = _scatter_plan(names, grads)

    def src_view(t, ref, idx):
        return _slot(ref, axes[t], idx, sizes[t])

    def dst_view(t, ref, idx):
        return ref.at[idx]

    return _big_exchange(grads, dst_shapes, src_view, dst_view, "scatter_grads")


def _pad_row(vec, width=128):
    return jnp.pad(vec.astype(F32), (0, width - vec.shape[0])).reshape(1, width)


def _block_sizes(t_total):
    return dict(row=min(256, t_total), dn=min(512, t_total), swa=min(256, t_total), scan=min(256, t_total))


def _ln_apply(h, mix, g, b, tb):
    t_total, d = h.shape
    fwd = lambda pids, *a: _ln_fn(pids, *a) * 2
    return _block_fwd(fwd, [h, mix, g, b], [_rows(tb, d), _rows(tb, d), _whole((1, d)), _whole((1, d))],
                      [_sds((t_total, d)), _sds((t_total, d), BF16)], [_rows(tb, d), _rows(tb, d)],
                      (t_total // tb,), "ln_fwd")


def _ln_grad(h, mix, g, b, dy, tb):
    t_total, d = h.shape
    return _block_bwd(_ln_fn, [h, mix, g, b], [_rows(tb, d), _rows(tb, d), _whole((1, d)), _whole((1, d))],
                      [dy], [_rows(tb, d)], ["s", "s", "a", "a"],
                      [_sds((t_total, d)), _sds((t_total, d), BF16), _sds((1, d)), _sds((1, d))],
                      [_rows(tb, d), _rows(tb, d), _whole((1, d)), _whole((1, d))], (t_total // tb,), "ln_bwd")


def _memattn_specs(tb, qcol):
    return [pl.BlockSpec((tb, MEM_W), lambda i: (i, qcol)), _whole((MEM_W, 2 * MEM_W))]


def _act_epilogue(acc):
    r = jnp.maximum(acc, 0.0)
    return (r * r,)


def _dact_epilogue(acc, act):
    return (acc * (2.0 * jnp.sqrt(act.astype(F32))),)


def _add_epilogue(acc, other):
    return (acc + other,)


LO_NAMES = ("a_w_in", "w_kv_shared", "mem_w_kv", "w_o", "mlp_w_up", "mlp_w_down")
HI_NAMES = ("b_w_in", "mem_w_kv", "w_o", "mlp_w_up", "mlp_w_down")


def _local_step(x, mem, positions, target, lo, hi, small, exchange_hi):
    t_total = x.shape[0]
    bs = _block_sizes(t_total)
    tb, tdn, tsw = bs["row"], bs["dn"], bs["swa"]
    nb = t_total // tb
    nbs = t_total // tsw

    inv_freq = ROPE_THETA ** (-jnp.arange(0, SWA_DH, 2, dtype=F32) / SWA_DH)
    ang = positions.astype(F32)[:, None] * inv_freq
    cos = jnp.tile(jnp.cos(ang), (1, 4))
    sin = jnp.tile(jnp.sin(ang), (1, 4))

    w_a = [jnp.concatenate([lo["a_w_in"][l][:, :3072], lo["a_w_in"][l][:, 3084:], lo["a_w_in"][l][:, 3072:3084],
                            jnp.zeros((D_MODEL, A_IN_PAD - A_IN), BF16)], axis=1) for l in range(N_A)]
    wkv = lo["w_kv_shared"]
    w_kvd = jnp.concatenate([wkv[:, 64 * (i // 2):64 * (i // 2 + 1)] for i in range(8)], axis=1)
    mem_b = mem.astype(BF16)
    gathered = {"lo": lo, "hi": None if exchange_hi else hi}
    w_up = {}

    def weight(name, l):
        src = gathered["lo" if l < N_A else "hi"]
        if name == "mlp_w_up":
            if l not in w_up:
                w_up[l] = jnp.moveaxis(src[name][:, l % 2], 0, 1).reshape(D_MODEL, 4 * D_MODEL)
            return w_up[l]
        return src[name][l % 2]

    saved = []
    h, hb = x, x.astype(BF16)
    kr = vd_src = None
    for l in range(DEPTH):
        sv = dict(h=h, hb=hb)
        kvm = _matmul(mem_b, weight("mem_w_kv", l), "nn", [F32], "mm_memkv", tm=256)
        if l < N_A:
            proj = _matmul(hb, w_a[l], "nn", [F32], "mm_proj_a", tn=1152)
            conv_w = small["a_conv_w"][l]
            c = _conv_fwd(proj, conv_w, tb)
            alog, dtb = _pad_row(small["a_A_log"][l]), _pad_row(small["a_dt_bias"][l])
            rowa_in = [c, proj, alog, dtb]
            rowa_specs = [_rows(tb, 3 * DN_W), _rows(tb, 128, 26), _whole((1, 128)), _whole((1, 128))]
            q, k, v, gcb, betab = _block_fwd(_rowa_fn, rowa_in, rowa_specs, [_sds((t_total, DN_W))] * 5,
                                             [_rows(tb, DN_W)] * 5, (nb,), "rowa_fwd")
            hs = _head_spec(tdn)
            dn_grid = (DN_HEADS, t_total // tdn)
            full = _sds((t_total, DN_W))
            full_b = _sds((t_total, DN_W), BF16)
            dn1_out_shapes = [full, full_b, _sds((DN_HEADS, t_total, PAIR), BF16), full_b, full_b, full,
                              _sds((DN_HEADS, t_total, PAIR))]
            dn1_out_specs = [hs, hs, _intra_spec(tdn), hs, hs, hs, _intra_spec(tdn)]
            if exchange_hi and l == 0:
                ride = _gather_ride(HI_NAMES, [hi[n] for n in HI_NAMES])
                (u, w, intra, qd, kd, cd, tinv), got = _block_fwd(
                    _dn1_fn, [q, k, v, gcb, betab], [hs] * 5, dn1_out_shapes, dn1_out_specs, dn_grid,
                    "dn1_fwd_gather", ride=ride)
                gathered["hi"] = dict(zip(HI_NAMES, got))
            else:
                u, w, intra, qd, kd, cd, tinv = _block_fwd(_dn1_fn, [q, k, v, gcb, betab], [hs] * 5, dn1_out_shapes,
                                                           dn1_out_specs, dn_grid, "dn1_fwd")
            o, states = _dn2_fwd(qd, kd, u, w, intra, cd, bs["scan"])
            nw = small["a_norm_w"][l].reshape(1, DN_D)
            post_in = [o, proj, nw]
            post_specs = [_rows(tb, DN_W), _rows(tb, DN_W, 3), _whole((1, DN_D))]
            (og,) = _block_fwd(_post_fn, post_in, post_specs, [_sds((t_total, DN_W), BF16)], [_rows(tb, DN_W)],
                               (nb,), "post_fwd")
            qm_col = 12
            sv.update(proj=proj, c=c, rowa_in=rowa_in, rowa_specs=rowa_specs, dn1_in=[q, k, v, gcb, betab, tinv],
                      dn2_in=[qd, kd, u, w, intra, cd], states=states, post_in=post_in, post_specs=post_specs,
                      conv_w=conv_w)
        else:
            jb = l - N_A
            proj = _matmul(hb, weight("b_w_in", l), "nn", [F32], "mm_proj_b")
            sinks = _pad_row(small["b_sinks"][jb])
            swa_in = [proj, cos, sin, kr, kr, vd_src, vd_src, sinks]
            swa_specs = [_rows(tsw, DN_W), _rows(tsw, 128), _rows(tsw, 128),
                         pl.BlockSpec((WINDOW, 256), lambda i: (jnp.maximum(i * (tsw // WINDOW) - 1, 0), 0)),
                         _rows(tsw, 256),
                         pl.BlockSpec((WINDOW, 256), lambda i: (jnp.maximum(i * (tsw // WINDOW) - 1, 0), 1)),
                         _rows(tsw, 256, 1), _whole((1, 128))]
            (og,) = _block_fwd(_swa_fn, swa_in, swa_specs, [_sds((t_total, DN_W), BF16)], [_rows(tsw, DN_W)],
                               (nbs,), "swa_fwd")
            qm_col = 3
            sv.update(proj=proj, swa_in=swa_in, swa_specs=swa_specs)
        mem_in = [proj, kvm]
        (mo,) = _block_fwd(_memattn_fn, mem_in, _memattn_specs(tb, qm_col), [_sds((t_total, MEM_W), BF16)],
                           [_rows(tb, MEM_W)], (nb,), "memattn_fwd")
        mixin = jnp.concatenate([og, mo], axis=1)
        mix = _matmul(mixin, weight("w_o", l), "nn", [F32], "mm_wo")
        g0, b0 = small["ln_g"][l, 0].reshape(1, -1), small["ln_b"][l, 0].reshape(1, -1)
        h1, h1b = _ln_apply(h, mix, g0, b0, tb)
        act = _matmul(h1b, weight("mlp_w_up", l), "nn", [BF16], "mm_up", epi=_act_epilogue, tm=2048)
        mlp = _matmul(act, weight("mlp_w_down", l), "nn", [F32], "mm_down", tk=2048)
        g1, b1 = small["ln_g"][l, 1].reshape(1, -1), small["ln_b"][l, 1].reshape(1, -1)
        h2, h2b = _ln_apply(h1, mlp, g1, b1, tb)
        sv.update(kvm=kvm, mem_in=mem_in, qm_col=qm_col, mixin=mixin, mix=mix, ln0=(g0, b0), h1=h1, h1b=h1b,
                  act=act, mlp=mlp, ln1=(g1, b1))
        saved.append(sv)
        h, hb = h2, h2b
        if l == N_A - 1:
            kvd = _matmul(hb, w_kvd, "nn", [F32], "mm_kvd")
            krope_in = [kvd, cos, sin]
            krope_specs = [_rows(tb, 256), _rows(tb, 128), _rows(tb, 128)]
            (kr,) = _block_fwd(_krope_fn, krope_in, krope_specs, [_sds((t_total, 256))], [_rows(tb, 256)], (nb,),
                               "krope_fwd")
            vd_src = kvd

    loss, dh = _loss_and_grad(h, target, tb)

    grads = {n: [None] * BIG[n][1][0] for n in BIG_NAMES if n != "w_kv_shared"}
    sg = dict(a_conv_w=[None] * N_A, a_A_log=[None] * N_A, a_dt_bias=[None] * N_A, a_norm_w=[None] * N_A,
              b_sinks=[None] * (DEPTH - N_A), ln_g=[[None, None] for _ in range(DEPTH)],
              ln_b=[[None, None] for _ in range(DEPTH)])
    dk_parts, dv_parts = [], []
    for l in reversed(range(DEPTH)):
        sv = saved[l]
        if l == N_A - 1:
            dkr = _halo_sum([p[0] for p in dk_parts], [p[1] for p in dk_parts], tsw)
            dvv = _halo_sum([p[0] for p in dv_parts], [p[1] for p in dv_parts], tsw)
            (dkraw,) = _block_bwd(_krope_fn, krope_in, krope_specs, [dkr], [_rows(tb, 256)], ["s", None, None],
                                  [_sds((t_total, 256), BF16)], [_rows(tb, 256)], (nb,), "krope_bwd")
            dkvd = jnp.concatenate([dkraw, dvv.astype(BF16)], axis=1)
            g_kvd = _matmul(saved[l + 1]["hb"], dkvd, "tn", [F32], "mm_dw_kvd", tm=1024, tn=512)
            dh = _matmul(dkvd, w_kvd, "nt", [F32], "mm_dx_kvd", epi=_add_epilogue, extras=[dh], tn=1024, tk=512)
            grads["w_kv_shared"] = jnp.concatenate(
                [g_kvd[:, 128 * i:128 * i + 64] + g_kvd[:, 128 * i + 64:128 * (i + 1)] for i in range(4)],
                axis=1).astype(BF16)
        g1, b1 = sv["ln1"]
        dh1a, dmlp, dg1, db1 = _ln_grad(sv["h1"], sv["mlp"], g1, b1, dh, tb)
        dup = _matmul(dmlp, weight("mlp_w_down", l), "nt", [BF16], "mm_dact", epi=_dact_epilogue, extras=[sv["act"]])
        grads["mlp_w_down"][l] = _matmul(sv["act"], dmlp, "tn", [BF16], "mm_dw_down", tk=4096)
        g_up = _matmul(sv["h1b"], dup, "tn", [BF16], "mm_dw_up", tk=4096)
        grads["mlp_w_up"][l] = jnp.moveaxis(g_up.reshape(D_MODEL, N_DEV, 512), 1, 0)
        dh1 = _matmul(dup, weight("mlp_w_up", l), "nt", [F32], "mm_dx_up", epi=_add_epilogue, extras=[dh1a], tk=2048)
        g0, b0 = sv["ln0"]
        dha, dmix, dg0, db0 = _ln_grad(sv["h"], sv["mix"], g0, b0, dh1, tb)
        sg["ln_g"][l] = [dg0, dg1]
        sg["ln_b"][l] = [db0, db1]
        grads["w_o"][l] = _matmul(sv["mixin"], dmix, "tn", [BF16], "mm_dw_o", tk=4096)
        dmixin = _matmul(dmix, weight("w_o", l), "nt", [F32], "mm_dx_o", tn=1024)
        dqm, dkvm = _block_bwd(_memattn_fn, sv["mem_in"], _memattn_specs(tb, sv["qm_col"]), [dmixin],
                               [_rows(tb, MEM_W, 3)], ["s", "a"],
                               [_sds((t_total, MEM_W), BF16), _sds((MEM_W, 2 * MEM_W))],
                               [_rows(tb, MEM_W), _whole((MEM_W, 2 * MEM_W))], (nb,), "memattn_bwd")
        grads["mem_w_kv"][l] = _matmul(mem_b, dkvm.astype(BF16), "tn", [BF16], "mm_dw_memkv", tm=1024, tn=512)
        if l < N_A:
            d_o, dz, dnw = _block_bwd(_post_fn, sv["post_in"], sv["post_specs"], [dmixin], [_rows(tb, DN_W)],
                                      ["s", "s", "a"],
                                      [_sds((t_total, DN_W)), _sds((t_total, DN_W), BF16), _sds((1, DN_D))],
                                      [_rows(tb, DN_W), _rows(tb, DN_W), _whole((1, DN_D))], (nb,), "post_bwd")
            sg["a_norm_w"][l] = dnw
            dqd, dkd, du, dw, da, dcd = _dn2_bwd(*sv["dn2_in"], sv["states"], d_o, bs["scan"])
            hs = _head_spec(tdn)
            full = _sds((t_total, DN_W))
            dn1_bwd_args = (_dn1_fn_known, sv["dn1_in"], [hs] * 5 + [_intra_spec(tdn)], [du, dw, da, dqd, dkd, dcd],
                            [hs, hs, _intra_spec(tdn), hs, hs, hs], ["s"] * 5 + [None], [full] * 5, [hs] * 5,
                            (DN_HEADS, t_total // tdn))
            if l == N_A - 1:
                hi_grads = [jnp.stack(grads[n][N_A:] if n != "b_w_in" else grads[n], axis=0) for n in HI_NAMES]
            if exchange_hi and l == N_A - 1:
                (dq, dk, dv, dgc, dbeta), hi_grads = _block_bwd(*dn1_bwd_args, "dn1_bwd_scatter",
                                                                ride=_scatter_ride(HI_NAMES, hi_grads))
            else:
                dq, dk, dv, dgc, dbeta = _block_bwd(*dn1_bwd_args, "dn1_bwd")
            dc, dab, dalog, ddtb = _block_bwd(
                _rowa_fn, sv["rowa_in"], sv["rowa_specs"], [dq, dk, dv, dgc, dbeta], [_rows(tb, DN_W)] * 5,
                ["s", "s", "a", "a"],
                [_sds((t_total, 3 * DN_W)), _sds((t_total, 128), BF16), _sds((1, 128)), _sds((1, 128))],
                [_rows(tb, 3 * DN_W), _rows(tb, 128), _whole((1, 128)), _whole((1, 128))], (nb,), "rowa_bwd")
            sg["a_A_log"][l] = dalog[0, :DN_HEADS]
            sg["a_dt_bias"][l] = ddtb[0, :DN_HEADS]
            dx, dconv = _conv_bwd(dc, sv["proj"], sv["conv_w"], tb)
            sg["a_conv_w"][l] = dconv
            dproj = jnp.concatenate([dx, dz, dqm, dab], axis=1)
            g_in = _matmul(sv["hb"], dproj, "tn", [BF16], "mm_dw_a", tm=1024, tn=1152, tk=2048)
            grads["a_w_in"][l] = jnp.concatenate([g_in[:, :3072], g_in[:, 3328:3340], g_in[:, 3072:3328]], axis=1)
            dh = _matmul(dproj, w_a[l], "nt", [F32], "mm_dx_a", epi=_add_epilogue, extras=[dha], tk=1152)
        else:
            jb = l - N_A
            swa_kinds = ["s", None, None, "s", "s", "s", "s", "a"]
            halo_spec = pl.BlockSpec((None, WINDOW, 256), lambda i: (i, 0, 0))
            dq, dkh, dkc, dvh, dvc, dsink = _block_bwd(
                _swa_fn, sv["swa_in"], sv["swa_specs"], [dmixin], [_rows(tsw, DN_W)], swa_kinds,
                [_sds((t_total, DN_W), BF16), _sds((nbs, WINDOW, 256)), _sds((t_total, 256)),
                 _sds((nbs, WINDOW, 256)), _sds((t_total, 256)), _sds((1, 128))],
                [_rows(tsw, DN_W), halo_spec, _rows(tsw, 256), halo_spec, _rows(tsw, 256), _whole((1, 128))],
                (nbs,), "swa_bwd")
            sg["b_sinks"][jb] = dsink[0, :SWA_HEADS]
            dk_parts.append((dkc, dkh))
            dv_parts.append((dvc, dvh))
            dproj = jnp.concatenate([dq, dqm], axis=1)
            grads["b_w_in"][jb] = _matmul(sv["hb"], dproj, "tn", [BF16], "mm_dw_b", tk=4096)
            dh = _matmul(dproj, weight("b_w_in", l), "nt", [F32], "mm_dx_b", epi=_add_epilogue, extras=[dha], tn=1024)

    lo_grads = [grads[n] if n == "w_kv_shared" else jnp.stack(grads[n][:N_A], axis=0) for n in LO_NAMES]
    small_grads = dict(
        a_conv_w=jnp.stack(sg["a_conv_w"]), a_A_log=jnp.stack(sg["a_A_log"]), a_dt_bias=jnp.stack(sg["a_dt_bias"]),
        a_norm_w=jnp.concatenate(sg["a_norm_w"], axis=0), b_sinks=jnp.stack(sg["b_sinks"]),
        ln_g=jnp.stack([jnp.concatenate(p, axis=0) for p in sg["ln_g"]]),
        ln_b=jnp.stack([jnp.concatenate(p, axis=0) for p in sg["ln_b"]]))
    return loss, dh, lo_grads, hi_grads, small_grads


def _pack(arrays, rows):
    flat = []
    for a in arrays:
        v = a.astype(F32).reshape(-1)
        flat.append(jnp.pad(v, (0, (-v.shape[0]) % 128)))
    flat = jnp.concatenate(flat)
    return jnp.pad(flat, (0, rows * 128 - flat.shape[0])).reshape(rows, 128)


def _unpack(slab, shapes):
    flat = slab.reshape(slab.shape[:-2] + (-1,))
    out, off = [], 0
    for s in shapes:
        n = math.prod(s)
        out.append(flat[..., off:off + n].reshape(slab.shape[:-2] + tuple(s)))
        off += n + (-n) % 128
    return out


def _rows_for(shapes):
    rows = sum((math.prod(s) + 127) // 128 for s in shapes)
    return rows + (-rows) % 8


SMALL_NAMES = ("a_conv_w", "a_A_log", "a_dt_bias", "a_norm_w", "b_sinks", "ln_g", "ln_b")
SMALL_SHARDED = {"a_conv_w": 2, "ln_g": 2, "ln_b": 2}
SMALL_FULL = {"a_conv_w": (2, 4, 2304), "a_A_log": (2, 6), "a_dt_bias": (2, 6), "a_norm_w": (2, 128),
              "b_sinks": (2, 12), "ln_g": (4, 2, 1024), "ln_b": (4, 2, 1024)}


def kernel(x, mem, positions, a_w_in, a_conv_w, a_A_log, a_dt_bias, a_norm_w, b_w_in, b_sinks, w_kv_shared, mem_w_kv, w_o, mlp_w_up, mlp_w_down, ln_g, ln_b, loss_target, m_a_w_in, m_a_conv_w, m_a_A_log, m_a_dt_bias, m_a_norm_w, m_b_w_in, m_b_sinks, m_w_kv_shared, m_mem_w_kv, m_w_o, m_mlp_w_up, m_mlp_w_down, m_ln_g, m_ln_b, v_a_w_in, v_a_conv_w, v_a_A_log, v_a_dt_bias, v_a_norm_w, v_b_w_in, v_b_sinks, v_w_kv_shared, v_mem_w_kv, v_w_o, v_mlp_w_up, v_mlp_w_down, v_ln_g, v_ln_b):
    params = dict(a_w_in=a_w_in, a_conv_w=a_conv_w, a_A_log=a_A_log, a_dt_bias=a_dt_bias, a_norm_w=a_norm_w,
                  b_w_in=b_w_in, b_sinks=b_sinks, w_kv_shared=w_kv_shared, mem_w_kv=mem_w_kv, w_o=w_o,
                  mlp_w_up=mlp_w_up, mlp_w_down=mlp_w_down, ln_g=ln_g, ln_b=ln_b)
    mom = dict(a_w_in=m_a_w_in, a_conv_w=m_a_conv_w, a_A_log=m_a_A_log, a_dt_bias=m_a_dt_bias, a_norm_w=m_a_norm_w,
               b_w_in=m_b_w_in, b_sinks=m_b_sinks, w_kv_shared=m_w_kv_shared, mem_w_kv=m_mem_w_kv, w_o=m_w_o,
               mlp_w_up=m_mlp_w_up, mlp_w_down=m_mlp_w_down, ln_g=m_ln_g, ln_b=m_ln_b)
    var = dict(a_w_in=v_a_w_in, a_conv_w=v_a_conv_w, a_A_log=v_a_A_log, a_dt_bias=v_a_dt_bias, a_norm_w=v_a_norm_w,
               b_w_in=v_b_w_in, b_sinks=v_b_sinks, w_kv_shared=v_w_kv_shared, mem_w_kv=v_mem_w_kv, w_o=v_w_o,
               mlp_w_up=v_mlp_w_up, mlp_w_down=v_mlp_w_down, ln_g=v_ln_g, ln_b=v_ln_b)
    me = 4 * lax.axis_index("x") + 2 * lax.axis_index("y") + lax.axis_index("c")

    split = lambda n, a: a if n in ("a_w_in", "b_w_in", "w_kv_shared") else (a[:N_A], a[N_A:])
    shard_b = {n: split(n, params[n].astype(BF16)) for n in BIG_NAMES}
    half = lambda names, i: [shard_b[n] if not isinstance(shard_b[n], tuple) else shard_b[n][i] for n in names]
    lo = dict(zip(LO_NAMES, _allgather_weights(LO_NAMES, half(LO_NAMES, 0))))
    hi = dict(zip(HI_NAMES, half(HI_NAMES, 1)))
    sharded_names = [n for n in SMALL_NAMES if n in SMALL_SHARDED]
    shard_shapes = [params[n].shape for n in sharded_names]
    gathered = _small_exchange(_pack([params[n] for n in sharded_names], _rows_for(shard_shapes)), reduce=False)
    small = {n: params[n] for n in SMALL_NAMES if n not in SMALL_SHARDED}
    for n, g in zip(sharded_names, _unpack(gathered, shard_shapes)):
        small[n] = jnp.moveaxis(g, 0, 2).reshape(SMALL_FULL[n])

    loss, dx, lo_grads, hi_recv, small_grads = _local_step(x[0], mem[0], positions[0], loss_target[0], lo, hi, small,
                                                           True)
    loss = lax.psum(loss, ("x", "y", "c"))
    lo_recv = dict(zip(LO_NAMES, _scatter_grads(LO_NAMES, lo_grads)))
    hi_recv = dict(zip(HI_NAMES, hi_recv))
    out = {}
    for n in BIG_NAMES:
        shp = params[n].shape
        rows = math.prod(shp[:-1])
        recvs = [r[n].reshape(N_DEV, -1, shp[-1]) for r in (lo_recv, hi_recv) if n in r]
        res = _adamw(recvs, params[n].reshape(rows, shp[-1]), mom[n].reshape(rows, shp[-1]),
                     var[n].reshape(rows, shp[-1]), 32, "adamw_" + n)
        out[n] = [t.reshape(shp) for t in res]
    full_shapes = [SMALL_FULL[n] for n in SMALL_NAMES]
    summed = _small_exchange(_pack([small_grads[n] for n in SMALL_NAMES], _rows_for(full_shapes)), reduce=True)
    local_g = []
    for n, g in zip(SMALL_NAMES, _unpack(summed, full_shapes)):
        if n in SMALL_SHARDED:
            size = params[n].shape[2]
            g = lax.dynamic_slice_in_dim(g, me * size, size, axis=2)
        local_g.append(g)
    local_shapes = [params[n].shape for n in SMALL_NAMES]
    rows = _rows_for(local_shapes)
    res = _adamw([_pack(local_g, rows)[None]], _pack([params[n] for n in SMALL_NAMES], rows),
                 _pack([mom[n] for n in SMALL_NAMES], rows), _pack([var[n] for n in SMALL_NAMES], rows), rows,
                 "adamw_small")
    unpacked = [_unpack(t, local_shapes) for t in res]
    for i, n in enumerate(SMALL_NAMES):
        out[n] = [unpacked[k][i] for k in range(4)]

    order = ("a_w_in", "a_conv_w", "a_A_log", "a_dt_bias", "a_norm_w", "b_w_in", "b_sinks", "w_kv_shared",
             "mem_w_kv", "w_o", "mlp_w_up", "mlp_w_down", "ln_g", "ln_b")
    return (loss, dx[None], *[out[n][0] for n in order], *[out[n][1] for n in order],
            *[out[n][2] for n in order], *[out[n][3] for n in order])
```

```python
import functools
import math

import jax
import jax.numpy as jnp
from jax import lax
from jax.experimental import pallas as pl
from jax.experimental.pallas import tpu as pltpu

F32 = jnp.float32
BF16 = jnp.bfloat16

D_MODEL = 1024
DEPTH = 4
N_A = 2
MEM_HEADS = 4
MEM_DH = 64
MEM_W = 256
DN_HEADS = 6
DN_D = 128
DN_W = 768
CHUNK = 64
SWA_DH = 64
SWA_HEADS = 12
WINDOW = 128
ROPE_THETA = 10000.0
LN_EPS = 1e-5
NORM_EPS = 1e-6
DN_ALPHA = (2.0 * DEPTH) ** 0.25
A_IN = 3340
A_IN_PAD = 3456
N_DEV = 8

ADAM_LR = 0.001
ADAM_B1 = 0.9
ADAM_B2 = 0.999
ADAM_EPS = 1e-08
ADAM_WD = 0.01
ADAM_STEP = 10

VMEM_LIMIT = 52 * 1024 * 1024
NEG_BIG = -1e30


def _cparams(sem):
    return pltpu.CompilerParams(dimension_semantics=sem, vmem_limit_bytes=VMEM_LIMIT)


_CONTRACT = {"nn": (1, 0), "nt": (1, 1), "tn": (0, 0)}


def _raw_mm(a, b, mode, prec):
    ca, cb = _CONTRACT[mode]
    dims = (((ca,), (cb,)), ((), ()))
    dot = lambda p, q: lax.dot_general(p, q, dims, preferred_element_type=F32)
    if prec == "bf16":
        return dot(a.astype(BF16), b.astype(BF16))
    a, b = a.astype(F32), b.astype(F32)
    a_hi, b_hi = a.astype(BF16), b.astype(BF16)
    if prec == "sela":
        return dot(a_hi, b_hi) + dot(a_hi, (b - b_hi.astype(F32)).astype(BF16))
    a_lo = (a - a_hi.astype(F32)).astype(BF16)
    if prec == "selb":
        return dot(a_hi, b_hi) + dot(a_lo, b_hi)
    b_lo = (b - b_hi.astype(F32)).astype(BF16)
    return dot(a_hi, b_hi) + (dot(a_hi, b_lo) + dot(a_lo, b_hi))


@functools.partial(jax.custom_vjp, nondiff_argnums=(2, 3))
def mm(a, b, mode, prec):
    return _raw_mm(a, b, mode, prec)


def _mm_fwd(a, b, mode, prec):
    return _raw_mm(a, b, mode, prec), (a, b)


def _mm_bwd(mode, prec, res, ct):
    a, b = res
    if prec == "sela":
        pa, pb = "f32", {"nn": "sela", "nt": "selb", "tn": "sela"}[mode]
    elif prec == "selb":
        pa, pb = {"nn": "selb", "nt": "selb", "tn": "sela"}[mode], "f32"
    else:
        pa = pb = prec
    if mode == "nn":
        return mm(ct, b, "nt", pa), mm(a, ct, "tn", pb)
    if mode == "nt":
        return mm(ct, b, "nn", pa), mm(ct, a, "tn", pb)
    return mm(b, ct, "nt", pa), mm(a, ct, "nn", pb)


mm.defvjp(_mm_fwd, _mm_bwd)


@jax.custom_vjp
def _softplus(x):
    y = jnp.exp(-jnp.abs(x))
    log1p_y = jnp.where(y < 1e-2, y * (1.0 - y * (0.5 - y * (1.0 / 3.0))), jnp.log(1.0 + y))
    return jnp.maximum(x, 0.0) + log1p_y


def _softplus_fwd(x):
    return _softplus(x), x


def _softplus_bwd(x, ct):
    return (ct * jax.nn.sigmoid(x),)


_softplus.defvjp(_softplus_fwd, _softplus_bwd)


def _iota(shape, dim):
    return lax.broadcasted_iota(jnp.int32, shape, dim)


class _Ride:
    def __init__(self, srcs, dst_shapes, src_view, dst_view):
        self.srcs, self.dst_shapes, self.src_view, self.dst_view = list(srcs), list(dst_shapes), src_view, dst_view
        self.n = len(self.srcs)
        self.any_specs = [pl.BlockSpec(memory_space=pl.ANY)] * self.n
        self.scratch = [pltpu.SemaphoreType.DMA((self.n, N_DEV - 1)), pltpu.SemaphoreType.DMA((self.n, N_DEV - 1)),
                        pltpu.SemaphoreType.DMA((self.n,))]

    def copies(self, src_refs, dst_refs, sems):
        send_sems, recv_sems, local_sems = sems
        me, peers = _me_and_peers()
        local, out, inc = [], [], []
        for t in range(self.n):
            local.append(pltpu.make_async_copy(self.src_view(t, src_refs[t], me), self.dst_view(t, dst_refs[t], me),
                                               local_sems.at[t]))
            for k, (dev, idx) in enumerate(peers):
                mk = lambda s, d: pltpu.make_async_remote_copy(
                    src_ref=s, dst_ref=d, send_sem=send_sems.at[t, k], recv_sem=recv_sems.at[t, k], device_id=dev,
                    device_id_type=pl.DeviceIdType.MESH)
                out.append(mk(self.src_view(t, src_refs[t], idx), self.dst_view(t, dst_refs[t], me)))
                inc.append(mk(self.src_view(t, src_refs[t], me), self.dst_view(t, dst_refs[t], idx)))
        return local, out, inc

    def start(self, grid, src_refs, dst_refs, sems):
        first = pl.program_id(0) == 0
        for a in range(1, len(grid)):
            first = jnp.logical_and(first, pl.program_id(a) == 0)

        @pl.when(first)
        def _():
            local, out, _ = self.copies(src_refs, dst_refs, sems)
            for cp in local + out:
                cp.start()

    def finish(self, grid, src_refs, dst_refs, sems):
        last = pl.program_id(0) == grid[0] - 1
        for a in range(1, len(grid)):
            last = jnp.logical_and(last, pl.program_id(a) == grid[a] - 1)

        @pl.when(last)
        def _():
            local, out, inc = self.copies(src_refs, dst_refs, sems)
            for cp in inc:
                cp.wait_recv()
            for cp in out:
                cp.wait_send()
            for cp in local:
                cp.wait()


def _block_fwd(fn, ins, in_specs, out_shapes, out_specs, grid, name, ride=None):
    n_in, n_out = len(ins), len(out_shapes)
    n_ride = ride.n if ride else 0

    def body(*refs):
        pids = tuple(pl.program_id(a) for a in range(len(grid)))
        ride_refs = (refs[n_in:n_in + n_ride], refs[n_in + n_ride + n_out:n_in + 2 * n_ride + n_out],
                     refs[n_in + 2 * n_ride + n_out:])
        if ride:
            ride.start(grid, *ride_refs)
        vals = [r[...].astype(F32) for r in refs[:n_in]]
        outs = fn(pids, *vals)
        for r, o in zip(refs[n_in + n_ride:n_in + n_ride + n_out], outs):
            r[...] = o.astype(r.dtype)
        if ride:
            ride.finish(grid, *ride_refs)

    if not ride:
        return pl.pallas_call(
            body, grid=grid, in_specs=in_specs, out_specs=out_specs, out_shape=out_shapes, name=name,
            compiler_params=_cparams(("parallel",) * len(grid)))(*ins)
    res = pl.pallas_call(
        body, grid=grid, in_specs=list(in_specs) + ride.any_specs, out_specs=list(out_specs) + ride.any_specs,
        out_shape=list(out_shapes) + ride.dst_shapes, scratch_shapes=ride.scratch, name=name,
        compiler_params=_cparams(("arbitrary",) * len(grid)))(*ins, *ride.srcs)
    return res[:n_out], res[n_out:]


def _block_bwd(fn, ins, in_specs, cts, ct_specs, kinds, g_shapes, g_specs, grid, name, ride=None):
    n_in, n_ct, n_g = len(ins), len(cts), len(g_shapes)
    n_ride = ride.n if ride else 0
    didx = [i for i, k in enumerate(kinds) if k]

    def body(*refs):
        in_refs, ct_refs = refs[:n_in], refs[n_in:n_in + n_ct]
        base = n_in + n_ct
        g_refs = refs[base + n_ride:base + n_ride + n_g]
        ride_refs = (refs[base:base + n_ride], refs[base + n_ride + n_g:base + 2 * n_ride + n_g],
                     refs[base + 2 * n_ride + n_g:])
        if ride:
            ride.start(grid, *ride_refs)
        pids = tuple(pl.program_id(a) for a in range(len(grid)))
        vals = [r[...].astype(F32) for r in in_refs]

        def f(*dvals):
            full = list(vals)
            for i, v in zip(didx, dvals):
                full[i] = v
            return tuple(fn(pids, *full))

        _, vjp = jax.vjp(f, *[vals[i] for i in didx])
        gs = vjp(tuple(r[...].astype(F32) for r in ct_refs))
        first = pids[0] == 0
        for p in pids[1:]:
            first = jnp.logical_and(first, p == 0)
        for i, g, r in zip(didx, gs, g_refs):
            if kinds[i] == "s":
                r[...] = g.astype(r.dtype)
            else:
                @pl.when(first)
                def _(r=r):
                    r[...] = jnp.zeros(r.shape, r.dtype)

                r[...] += g.astype(r.dtype)
        if ride:
            ride.finish(grid, *ride_refs)

    sem = ("arbitrary",) * len(grid) if "a" in kinds or ride else ("parallel",) * len(grid)
    if not ride:
        return pl.pallas_call(
            body, grid=grid, in_specs=list(in_specs) + list(ct_specs), out_specs=g_specs, out_shape=g_shapes,
            name=name, compiler_params=_cparams(sem))(*ins, *cts)
    res = pl.pallas_call(
        body, grid=grid, in_specs=list(in_specs) + list(ct_specs) + ride.any_specs,
        out_specs=list(g_specs) + ride.any_specs, out_shape=list(g_shapes) + ride.dst_shapes,
        scratch_shapes=ride.scratch, name=name, compiler_params=_cparams(sem))(*ins, *cts, *ride.srcs)
    return res[:n_g], res[n_g:]


def _rows(tb, width, col=0):
    return pl.BlockSpec((tb, width), lambda i, col=col: (i, col))


def _whole(shape):
    return pl.BlockSpec(shape, lambda *_: (0,) * len(shape))


def _sds(shape, dtype=F32):
    return jax.ShapeDtypeStruct(shape, dtype)


def _matmul(a, b, mode, out_dtypes, name, epi=None, extras=(), tm=1024, tn=1024, tk=1024,
            b_spec=None, n_total=None, out_specs=None, out_shapes=None):
    if mode == "tn":
        k_total, m_total = a.shape
    else:
        m_total, k_total = a.shape
    if n_total is None:
        n_total = b.shape[0] if mode == "nt" else b.shape[1]
    tm, tn, tk = min(tm, m_total), min(tn, n_total), min(tk, k_total)
    assert m_total % tm == 0 and n_total % tn == 0 and k_total % tk == 0, (name, a.shape, b.shape)
    grid = (m_total // tm, n_total // tn, k_total // tk)
    nk = grid[2]
    if mode == "tn":
        a_spec = pl.BlockSpec((tk, tm), lambda i, j, k: (k, i))
    else:
        a_spec = pl.BlockSpec((tm, tk), lambda i, j, k: (i, k))
    if b_spec is None:
        if mode == "nt":
            b_spec = pl.BlockSpec((tn, tk), lambda i, j, k: (j, k))
        else:
            b_spec = pl.BlockSpec((tk, tn), lambda i, j, k: (k, j))
    tile = pl.BlockSpec((tm, tn), lambda i, j, k: (i, j))
    n_ex, n_out = len(extras), len(out_dtypes)
    ca, cb = _CONTRACT[mode]
    dims = (((ca,), (cb,)), ((), ()))

    def body(*refs):
        a_ref, b_ref = refs[:2]
        ex_refs = refs[2:2 + n_ex]
        out_refs = refs[2 + n_ex:2 + n_ex + n_out]
        part = lax.dot_general(a_ref[...], b_ref[...], dims, preferred_element_type=F32)

        def finish(val):
            res = epi(val, *[e[...] for e in ex_refs]) if epi is not None else (val,)
            for r, o in zip(out_refs, res):
                r[...] = o.astype(r.dtype)

        if nk == 1:
            finish(part)
        else:
            acc = refs[-1]
            k = pl.program_id(2)

            @pl.when(k == 0)
            def _():
                acc[...] = part

            @pl.when(k > 0)
            def _():
                acc[...] += part

            @pl.when(k == nk - 1)
            def _():
                finish(acc[...])

    if out_shapes is None:
        out_shapes = [_sds((m_total, n_total), d) for d in out_dtypes]
        out_specs = [tile] * n_out
    outs = pl.pallas_call(
        body, grid=grid, in_specs=[a_spec, b_spec] + [tile] * n_ex, out_specs=out_specs, out_shape=out_shapes,
        scratch_shapes=[pltpu.VMEM((tm, tn), F32)] if nk > 1 else [], name=name,
        compiler_params=_cparams(("parallel", "parallel", "arbitrary")))(a, b, *extras)
    return outs if n_out > 1 else outs[0]


def _silu(x):
    return x * jax.nn.sigmoid(x)


def _rowa_fn(pids, c, ab, alog, dtb):
    tb = c.shape[0]
    s = _silu(c)
    qs, ks = [], []
    for h in range(DN_HEADS):
        qh = s[:, DN_D * h:DN_D * (h + 1)]
        qs.append(qh * lax.rsqrt(jnp.sum(qh * qh, axis=-1, keepdims=True) + NORM_EPS) * (DN_D ** -0.5))
        kh = s[:, DN_W + DN_D * h:DN_W + DN_D * (h + 1)]
        ks.append(kh * lax.rsqrt(jnp.sum(kh * kh, axis=-1, keepdims=True) + NORM_EPS))
    q = jnp.concatenate(qs, axis=1)
    k = jnp.concatenate(ks, axis=1)
    v = s[:, 2 * DN_W:3 * DN_W]
    g128 = -jnp.exp(alog) * _softplus(ab + dtb)
    b128 = jax.nn.sigmoid(ab)
    r, cc = _iota((tb, tb), 0), _iota((tb, tb), 1)
    tri = jnp.where(((r >> 6) == (cc >> 6)) & (r >= cc), 1.0, 0.0)
    gc128 = mm(tri, g128, "nn", "sela")
    lane, col = _iota((128, DN_W), 0), _iota((128, DN_W), 1)
    exp_a = jnp.where(lane == (col >> 7), 1.0, 0.0)
    exp_b = jnp.where(lane == (col >> 7) + DN_HEADS, 1.0, 0.0)
    return q, k, v, mm(gc128, exp_a, "nn", "selb"), mm(b128, exp_b, "nn", "selb")


def _tri_inv_raw(lows, block):
    n = lows[0].shape[0]
    r, c = _iota((n, n), 0), _iota((n, n), 1)
    lg = 0
    xs = None
    while (1 << lg) < block:
        off = ((r >> (lg + 1)) == (c >> (lg + 1))) & (((r >> lg) & 1) == 1) & (((c >> lg) & 1) == 0)
        cblks = [jnp.where(off, low, 0.0) for low in lows]
        if xs is None:
            xs = [jnp.where(r == c, 1.0, 0.0) - cb for cb in cblks]
        else:
            ys = [mm(cb, x, "nn", "f32") for cb, x in zip(cblks, xs)]
            xs = [x - mm(x, y, "nn", "f32") for x, y in zip(xs, ys)]
        lg += 1
    return tuple(xs)


def _tri_inv_cotangent(block, xs, cts):
    n = xs[0].shape[0]
    r, c = _iota((n, n), 0), _iota((n, n), 1)
    shift = block.bit_length() - 1
    keep = ((r >> shift) == (c >> shift)) & (r > c)
    gs = [mm(x, ct, "tn", "f32") for x, ct in zip(xs, cts)]
    gs = [mm(g, x, "nt", "f32") for g, x in zip(gs, xs)]
    return tuple(jnp.where(keep, -g, 0.0) for g in gs)


@functools.partial(jax.custom_vjp, nondiff_argnums=(1,))
def _tri_inv(lows, block):
    return _tri_inv_raw(lows, block)


def _tri_inv_fwd(lows, block):
    xs = _tri_inv_raw(lows, block)
    return xs, xs


def _tri_inv_bwd(block, xs, cts):
    return (_tri_inv_cotangent(block, xs, cts),)


_tri_inv.defvjp(_tri_inv_fwd, _tri_inv_bwd)


@functools.partial(jax.custom_vjp, nondiff_argnums=(2,))
def _tri_inv_known(lows, known, block):
    return known


def _tri_inv_known_fwd(lows, known, block):
    return known, known


def _tri_inv_known_bwd(block, xs, cts):
    return _tri_inv_cotangent(block, xs, cts), tuple(jnp.zeros_like(x) for x in xs)


_tri_inv_known.defvjp(_tri_inv_known_fwd, _tri_inv_known_bwd)


PAIR = 2 * CHUNK


def _dn1_pairs(q, k, v, gc, beta, tinv_known=None):
    assert PAIR == DN_D
    n = PAIR
    pairs = range(q.shape[0] // n)
    cut = lambda t: [t[n * j:n * (j + 1)] for j in pairs]
    q, k, v, gc, beta = cut(q), cut(k), cut(v), cut(gc), cut(beta)
    onehot = jnp.where(_iota((n, DN_D), 1) == 0, 1.0, 0.0)
    r, c = _iota((n, n), 0), _iota((n, n), 1)
    same = (r >> 6) == (c >> 6)
    incl, strict = same & (r >= c), same & (r > c)
    row = _iota((n, DN_D), 0)
    eg = [jnp.exp(g) for g in gc]
    kb = [k[j] * beta[j] for j in pairs]
    g_row = [mm(onehot, g, "nt", "sela") for g in gc]
    kk = [mm(kb[j], k[j], "nt", "bf16") for j in pairs]
    qk = [mm(q[j], k[j], "nt", "bf16") for j in pairs]
    decay = [jnp.exp(jnp.where(incl, gc[j] - g_row[j], NEG_BIG)) for j in pairs]
    low = tuple(jnp.where(strict, kk[j] * decay[j], 0.0) for j in pairs)
    if tinv_known is None:
        tinv = _tri_inv(low, CHUNK)
    else:
        tinv = _tri_inv_known(low, tuple(cut(tinv_known)), CHUNK)
    uw = [mm(tinv[j], jnp.concatenate([v[j] * beta[j], kb[j] * eg[j]], axis=1), "nn", "f32") for j in pairs]
    intra = [jnp.where(incl, qk[j] * decay[j], 0.0) for j in pairs]
    g_last = []
    for g in gc:
        last0 = jnp.sum(jnp.where(row == CHUNK - 1, g, 0.0), axis=0, keepdims=True)
        last1 = jnp.sum(jnp.where(row == PAIR - 1, g, 0.0), axis=0, keepdims=True)
        g_last.append(jnp.where(row < CHUNK, last0, last1))
    join = lambda parts: jnp.concatenate(parts, axis=0)
    return (join([t[:, :DN_D] for t in uw]), join([t[:, DN_D:] for t in uw]), join(intra),
            join([q[j] * eg[j] for j in pairs]), join([k[j] * jnp.exp(g_last[j] - gc[j]) for j in pairs]),
            join([jnp.exp(g) for g in g_last]), join(list(tinv)))


def _dn1_fn(pids, q, k, v, gc, beta):
    return _dn1_pairs(q, k, v, gc, beta)


def _dn1_fn_known(pids, q, k, v, gc, beta, tinv):
    return _dn1_pairs(q, k, v, gc, beta, tinv)[:6]


def _dn2_step(half, state, qd, kd, u, w, intra, cd_row):
    heads = range(len(state))
    v_new = [u[h] - mm(w[h], state[h], "nn", "bf16") for h in heads]
    zeros = jnp.zeros_like(v_new[0])
    v_pair = [jnp.concatenate([v, zeros] if half == 0 else [zeros, v], axis=0) for v in v_new]
    from_state = [mm(qd[h], state[h], "nn", "bf16") for h in heads]
    out = tuple(from_state[h] + mm(intra[h], v_pair[h], "nn", "bf16") for h in heads)
    return out, tuple(state[h] * cd_row[h] + mm(kd[h], v_new[h], "tn", "bf16") for h in heads)


def _post_fn(pids, o, z, nw):
    outs = []
    for h in range(DN_HEADS):
        oh = o[:, DN_D * h:DN_D * (h + 1)]
        zh = z[:, DN_D * h:DN_D * (h + 1)]
        y = oh * lax.rsqrt(jnp.mean(oh * oh, axis=-1, keepdims=True) + NORM_EPS) * nw
        outs.append(y * _silu(zh))
    return (jnp.concatenate(outs, axis=1),)


def _memattn_fn(pids, qm, kvm):
    kmem, vmem = kvm[:, :MEM_W], kvm[:, MEM_W:]
    lane = _iota((1, MEM_W), 1)
    heads = range(MEM_HEADS)
    hm = [jnp.where((lane >> 6) == h, 1.0, 0.0) for h in heads]
    s = [mm(qm * (hm[h] * MEM_DH ** -0.5), kmem, "nt", "bf16") for h in heads]
    e = [jnp.exp(t - lax.stop_gradient(jnp.max(t, axis=-1, keepdims=True))) for t in s]
    o = [mm(e[h], vmem, "nn", "bf16") * (hm[h] / jnp.sum(e[h], axis=-1, keepdims=True)) for h in heads]
    return ((o[0] + o[1]) + (o[2] + o[3]),)


def _ln_fn(pids, h, mix, g, b):
    x = DN_ALPHA * h + mix
    mu = jnp.mean(x, axis=-1, keepdims=True)
    xc = x - mu
    var = jnp.mean(xc * xc, axis=-1, keepdims=True)
    return (xc * lax.rsqrt(var + LN_EPS) * g + b,)


def _rope_matrix():
    i, j = _iota((128, 128), 0), _iota((128, 128), 1)
    jj = j & 63
    return jnp.where((jj < 32) & (i == j + 32), -1.0, 0.0) + jnp.where((jj >= 32) & (i == j - 32), 1.0, 0.0)


def _rope128(x, cos, sin, rot):
    return x * cos + mm(x, rot, "nn", "selb") * sin


def _krope_fn(pids, kraw, cos, sin):
    rot = _rope_matrix()
    return (jnp.concatenate([_rope128(kraw[:, 128 * g:128 * (g + 1)], cos, sin, rot) for g in range(2)], axis=1),)


def _swa_fn(pids, qraw, cos, sin, k_halo, k_cur, v_halo, v_cur, sinks):
    tb = qraw.shape[0]
    nwin = tb // WINDOW
    rot = _rope_matrix()
    kcat = jnp.concatenate([k_halo, k_cur], axis=0)
    vcat = jnp.concatenate([v_halo, v_cur], axis=0)
    lane = _iota((1, 128), 1)
    halves = (jnp.where(lane < 64, 1.0, 0.0), jnp.where(lane >= 64, 1.0, 0.0))
    group = SWA_HEADS // 2
    rows = group * WINDOW
    in_cur = _iota((rows, WINDOW), 1) <= (_iota((rows, WINDOW), 0) & (WINDOW - 1))
    qg = [_rope128(qraw[:, 128 * p:128 * (p + 1)], cos, sin, rot) for p in range(group)]
    sink = []
    for kv in range(2):
        cols = [jnp.sum(jnp.where(lane == group * kv + i, sinks, 0.0), axis=-1, keepdims=True)
                + jnp.zeros((WINDOW, 1), F32) for i in range(group)]
        sink.append(jnp.concatenate(cols, axis=0))
    units = [(w, kv) for w in range(nwin) for kv in range(2)]
    n_units = range(len(units))
    q6 = [jnp.concatenate([qg[3 * kv + i // 2][WINDOW * w:WINDOW * (w + 1)] * (halves[i % 2] * SWA_DH ** -0.5)
                           for i in range(group)], axis=0) for w, kv in units]
    blk = lambda cat, w, kv: cat[WINDOW * w:WINDOW * (w + 1), 128 * kv:128 * (kv + 1)]
    s_prev = [mm(q6[u], blk(kcat, w, kv), "nt", "bf16") for u, (w, kv) in enumerate(units)]
    s_cur = [mm(q6[u], blk(kcat, w + 1, kv), "nt", "bf16") for u, (w, kv) in enumerate(units)]
    s = [jnp.where(in_cur, s_cur[u], jnp.where(pids[0] * nwin + w > 0, s_prev[u], NEG_BIG))
         for u, (w, kv) in enumerate(units)]
    m = [lax.stop_gradient(jnp.maximum(jnp.max(s[u], axis=-1, keepdims=True), sink[kv]))
         for u, (w, kv) in enumerate(units)]
    e = [jnp.exp(s[u] - m[u]) for u in n_units]
    denom = [jnp.sum(e[u], axis=-1, keepdims=True) + jnp.exp(sink[kv] - m[u]) for u, (w, kv) in enumerate(units)]
    o = [(mm(jnp.where(in_cur, e[u], 0.0), blk(vcat, w + 1, kv), "nn", "bf16")
          + mm(jnp.where(in_cur, 0.0, e[u]), blk(vcat, w, kv), "nn", "bf16")) / denom[u]
         for u, (w, kv) in enumerate(units)]
    out_rows = []
    for w in range(nwin):
        lanes = []
        for p in range(group):
            ou = o[units.index((w, p // 3))]
            i = 2 * (p % 3)
            lanes.append(ou[WINDOW * i:WINDOW * (i + 1)] * halves[0] + ou[WINDOW * (i + 1):WINDOW * (i + 2)] * halves[1])
        out_rows.append(jnp.concatenate(lanes, axis=1))
    return (jnp.concatenate(out_rows, axis=0),)


def _conv_fwd(proj, conv_w, tb):
    t_total = proj.shape[0]
    width = conv_w.shape[1]
    nb = t_total // tb

    def body(cur_ref, prev_ref, w_ref, out_ref):
        i = pl.program_id(0)
        prev = jnp.where(i > 0, prev_ref[...], 0.0)
        xcat = jnp.concatenate([prev, cur_ref[...]], axis=0)
        acc = xcat[8:] * w_ref[3:4, :]
        for j in range(3):
            acc = acc + pltpu.roll(xcat, 3 - j, 0)[8:] * w_ref[j:j + 1, :]
        out_ref[...] = acc

    return pl.pallas_call(
        body, grid=(nb,),
        in_specs=[pl.BlockSpec((tb, width), lambda i: (i, 0)),
                  pl.BlockSpec((8, width), lambda i: (jnp.maximum(i * (tb // 8) - 1, 0), 0)),
                  _whole((4, width))],
        out_specs=pl.BlockSpec((tb, width), lambda i: (i, 0)), out_shape=_sds((t_total, width)),
        name="conv_fwd", compiler_params=_cparams(("parallel",)))(proj, proj, conv_w)


def _conv_bwd(dc, proj, conv_w, tb):
    t_total, width = dc.shape
    nb = t_total // tb

    def body(dcur_ref, dnext_ref, cur_ref, prev_ref, w_ref, dx_ref, dw_ref):
        i = pl.program_id(0)
        dnext = jnp.where(i < nb - 1, dnext_ref[...], 0.0)
        dcur = dcur_ref[...]
        dcat = jnp.concatenate([dcur, dnext], axis=0)
        prev = jnp.where(i > 0, prev_ref[...], 0.0)
        xcat = jnp.concatenate([prev, cur_ref[...]], axis=0)

        @pl.when(i == 0)
        def _():
            dw_ref[...] = jnp.zeros(dw_ref.shape, F32)

        dx = dcur * w_ref[3:4, :]
        dw_ref[3:4, :] += jnp.sum(dcur * xcat[8:], axis=0, keepdims=True)
        for j in range(3):
            dx = dx + pltpu.roll(dcat, 8 - (3 - j), 0)[8:] * w_ref[j:j + 1, :]
            dw_ref[j:j + 1, :] += jnp.sum(dcur * pltpu.roll(xcat, 3 - j, 0)[8:], axis=0, keepdims=True)
        dx_ref[...] = dx.astype(dx_ref.dtype)

    return pl.pallas_call(
        body, grid=(nb,),
        in_specs=[pl.BlockSpec((tb, width), lambda i: (i, 0)),
                  pl.BlockSpec((8, width), lambda i: (jnp.minimum((i + 1) * (tb // 8), t_total // 8 - 1), 0)),
                  pl.BlockSpec((tb, width), lambda i: (i, 0)),
                  pl.BlockSpec((8, width), lambda i: (jnp.maximum(i * (tb // 8) - 1, 0), 0)),
                  _whole((4, width))],
        out_specs=[pl.BlockSpec((tb, width), lambda i: (i, 0)), _whole((4, width))],
        out_shape=[_sds((t_total, width), BF16), _sds((4, width))],
        name="conv_bwd", compiler_params=_cparams(("arbitrary",)))(dc, dc, proj, proj, conv_w)


def _head_spec(tb, nb=None):
    if nb is None:
        return pl.BlockSpec((tb, DN_D), lambda h, i: (i, h))
    return pl.BlockSpec((tb, DN_D), lambda h, i: (nb - 1 - i, h))


def _intra_spec(tb, nb=None):
    if nb is None:
        return pl.BlockSpec((None, tb, PAIR), lambda h, i: (h, i, 0))
    return pl.BlockSpec((None, tb, PAIR), lambda h, i: (h, nb - 1 - i, 0))


def _state_spec(tb, nb=None):
    if nb is None:
        return pl.BlockSpec((None, tb // CHUNK, DN_D, DN_D), lambda h, i: (h, i, 0, 0))
    return pl.BlockSpec((None, tb // CHUNK, DN_D, DN_D), lambda h, i: (h, nb - 1 - i, 0, 0))


def _scan_specs(tb, nb=None):
    blk = (lambda i: i) if nb is None else (lambda i: nb - 1 - i)
    rows = pl.BlockSpec((tb, DN_W), lambda i: (blk(i), 0))
    pair = pl.BlockSpec((DN_HEADS, tb, PAIR), lambda i: (0, blk(i), 0))
    states = pl.BlockSpec((DN_HEADS, tb // CHUNK, DN_D, DN_D), lambda i: (0, blk(i), 0, 0))
    return rows, pair, states


def _dn2_fwd(qd, kd, u, w, intra, cd, tb):
    t_total = qd.shape[0]
    rows, pair, states = _scan_specs(tb)

    def body(qd_ref, kd_ref, u_ref, w_ref, a_ref, cd_ref, o_ref, save_ref, state):
        @pl.when(pl.program_id(0) == 0)
        def _():
            state[...] = jnp.zeros(state.shape, F32)

        heads = range(DN_HEADS)
        lanes = [pl.ds(DN_D * h, DN_D) for h in heads]
        for j in range(tb // CHUNK):
            sl = pl.ds(CHUNK * j, CHUNK)
            s0 = tuple(state[h] for h in heads)
            for h in heads:
                save_ref[h, j] = s0[h]
            per_head = lambda ref: tuple(ref[sl, lanes[h]] for h in heads)
            out, s1 = _dn2_step(j % 2, s0, per_head(qd_ref), per_head(kd_ref), per_head(u_ref), per_head(w_ref),
                                tuple(a_ref[h, sl, :] for h in heads),
                                tuple(cd_ref[pl.ds(CHUNK * j, 1), lanes[h]] for h in heads))
            for h in heads:
                o_ref[sl, lanes[h]] = out[h]
                state[h] = s1[h]

    return pl.pallas_call(
        body, grid=(t_total // tb,), in_specs=[rows, rows, rows, rows, pair, rows],
        out_specs=[rows, states],
        out_shape=[_sds((t_total, DN_W)), _sds((DN_HEADS, t_total // CHUNK, DN_D, DN_D))],
        scratch_shapes=[pltpu.VMEM((DN_HEADS, DN_D, DN_D), F32)], name="dn2_fwd",
        compiler_params=_cparams(("arbitrary",)))(qd, kd, u, w, intra, cd)


def _dn2_bwd(qd, kd, u, w, intra, cd, saved, d_o, tb):
    t_total = qd.shape[0]
    nb = t_total // tb
    rows, pair, states = _scan_specs(tb, nb)

    def body(qd_ref, kd_ref, u_ref, w_ref, a_ref, cd_ref, save_ref, do_ref,
             dqd_ref, dkd_ref, du_ref, dw_ref, da_ref, dcd_ref, dstate):
        @pl.when(pl.program_id(0) == 0)
        def _():
            dstate[...] = jnp.zeros(dstate.shape, F32)

        first_row = _iota((CHUNK, DN_D), 0) == 0
        heads = range(DN_HEADS)
        lanes = [pl.ds(DN_D * h, DN_D) for h in heads]
        for j in reversed(range(tb // CHUNK)):
            sl = pl.ds(CHUNK * j, CHUNK)
            per_head = lambda ref: tuple(ref[sl, lanes[h]] for h in heads)
            _, vjp = jax.vjp(functools.partial(_dn2_step, j % 2), tuple(save_ref[h, j] for h in heads),
                             per_head(qd_ref), per_head(kd_ref), per_head(u_ref), per_head(w_ref),
                             tuple(a_ref[h, sl, :] for h in heads),
                             tuple(cd_ref[pl.ds(CHUNK * j, 1), lanes[h]] for h in heads))
            ds0, dqd, dkd, du, dw, da, dcd = vjp((per_head(do_ref), tuple(dstate[h] for h in heads)))
            for h in heads:
                dqd_ref[sl, lanes[h]] = dqd[h]
                dkd_ref[sl, lanes[h]] = dkd[h]
                du_ref[sl, lanes[h]] = du[h]
                dw_ref[sl, lanes[h]] = dw[h]
                da_ref[h, sl, :] = da[h]
                dcd_ref[sl, lanes[h]] = jnp.where(first_row, dcd[h], 0.0)
                dstate[h] = ds0[h]

    full = _sds((t_total, DN_W))
    return pl.pallas_call(
        body, grid=(nb,),
        in_specs=[rows, rows, rows, rows, pair, rows, states, rows],
        out_specs=[rows, rows, rows, rows, pair, rows],
        out_shape=[full, full, full, full, _sds((DN_HEADS, t_total, PAIR)), full],
        scratch_shapes=[pltpu.VMEM((DN_HEADS, DN_D, DN_D), F32)], name="dn2_bwd",
        compiler_params=_cparams(("arbitrary",)))(qd, kd, u, w, intra, cd, saved, d_o)


def _loss_and_grad(y, target, tb):
    t_total, d = y.shape

    def body(y_ref, t_ref, dy_ref, acc_ref):
        @pl.when(pl.program_id(0) == 0)
        def _():
            acc_ref[...] = jnp.zeros(acc_ref.shape, F32)

        err = y_ref[...] - t_ref[...]
        dy_ref[...] = err * (1.0 / d)
        acc_ref[...] += jnp.sum(err * err, axis=0, keepdims=True)

    dy, acc = pl.pallas_call(
        body, grid=(t_total // tb,), in_specs=[_rows(tb, d), _rows(tb, d)],
        out_specs=[_rows(tb, d), _whole((1, d))], out_shape=[_sds((t_total, d)), _sds((1, d))],
        name="loss", compiler_params=_cparams(("arbitrary",)))(y, target)
    return 0.5 * jnp.sum(acc) / d, dy


def _halo_sum(mains, halos, tb):
    t_total, width = mains[0].shape
    nb = t_total // tb
    n = len(mains)

    def body(*refs):
        out_ref = refs[-1]
        i = pl.program_id(0)
        tot = refs[0][...]
        for r in refs[1:n]:
            tot = tot + r[...]
        hal = refs[n][...]
        for r in refs[n + 1:2 * n]:
            hal = hal + r[...]
        hal = jnp.where(i < nb - 1, hal, 0.0)
        out_ref[...] = tot + jnp.concatenate([jnp.zeros((tb - WINDOW, width), F32), hal], axis=0)

    return pl.pallas_call(
        body, grid=(nb,),
        in_specs=[_rows(tb, width)] * n
        + [pl.BlockSpec((None, WINDOW, width), lambda i: (jnp.minimum(i + 1, nb - 1), 0, 0))] * n,
        out_specs=_rows(tb, width), out_shape=_sds((t_total, width)), name="halo_sum",
        compiler_params=_cparams(("parallel",)))(*mains, *halos)


def _adamw(recvs, w, m, v, tr, name):
    slots, _, c_total = recvs[0].shape
    r_total = w.shape[0]
    assert sum(r.shape[1] for r in recvs) == r_total
    tr = min([tr] + [r.shape[1] for r in recvs])
    assert all(r.shape[1] % tr == 0 for r in recvs)
    starts = [sum(r.shape[1] for r in recvs[:i]) // tr for i in range(len(recvs))]
    counts = [r.shape[1] // tr for r in recvs]
    c1 = 1.0 / (1.0 - ADAM_B1 ** ADAM_STEP)
    c2 = 1.0 / (1.0 - ADAM_B2 ** ADAM_STEP)

    def body(*refs):
        recv_refs = refs[:len(recvs)]
        w_ref, m_ref, v_ref, g_ref, d_ref, nm_ref, nv_ref = refs[len(recvs):]
        g = None
        for recv_ref, start in zip(recv_refs, starts):
            part = recv_ref[0].astype(F32)
            for s in range(1, slots):
                part = part + recv_ref[s].astype(F32)
            g = part if g is None else jnp.where(pl.program_id(0) >= start, part, g)
        nm = ADAM_B1 * m_ref[...] + (1.0 - ADAM_B1) * g
        nv = ADAM_B2 * v_ref[...] + (1.0 - ADAM_B2) * (g * g)
        g_ref[...] = g
        nm_ref[...] = nm
        nv_ref[...] = nv
        d_ref[...] = -ADAM_LR * ((nm * c1) / (jnp.sqrt(nv * c2) + ADAM_EPS) + ADAM_WD * w_ref[...])

    blk = pl.BlockSpec((tr, c_total), lambda i: (i, 0))
    recv_specs = [pl.BlockSpec((slots, tr, c_total), lambda i, s=s, n=n: (0, jnp.clip(i - s, 0, n - 1), 0))
                  for s, n in zip(starts, counts)]
    return pl.pallas_call(
        body, grid=(r_total // tr,), in_specs=recv_specs + [blk, blk, blk],
        out_specs=[blk] * 4, out_shape=[_sds((r_total, c_total))] * 4, name=name,
        compiler_params=_cparams(("parallel",)))(*recvs, w, m, v)


def _me_and_peers():
    x, y, c = lax.axis_index("x"), lax.axis_index("y"), lax.axis_index("c")
    me = 4 * x + 2 * y + c
    peers = []
    for k in range(1, N_DEV):
        px = 1 - x if (k >> 2) & 1 else x
        py = 1 - y if (k >> 1) & 1 else y
        pc = 1 - c if k & 1 else c
        peers.append(((px, py, pc), 4 * px + 2 * py + pc))
    return me, peers


def _small_exchange(packed, reduce):
    r_total = packed.shape[0]

    def body(p_ref, out_ref, gath_ref, send_sems, recv_sems):
        me, peers = _me_and_peers()
        gath_ref[me] = p_ref[...]
        copies = []
        for k, (dev, _) in enumerate(peers):
            cp = pltpu.make_async_remote_copy(src_ref=p_ref, dst_ref=gath_ref.at[me], send_sem=send_sems.at[k],
                                              recv_sem=recv_sems.at[k], device_id=dev,
                                              device_id_type=pl.DeviceIdType.MESH)
            cp.start()
            copies.append(cp)
        for k, (dev, idx) in enumerate(peers):
            pltpu.make_async_remote_copy(src_ref=p_ref, dst_ref=gath_ref.at[idx], send_sem=send_sems.at[k],
                                         recv_sem=recv_sems.at[k], device_id=dev,
                                         device_id_type=pl.DeviceIdType.MESH).wait_recv()
        for cp in copies:
            cp.wait_send()
        if reduce:
            tot = gath_ref[0]
            for d in range(1, N_DEV):
                tot = tot + gath_ref[d]
            out_ref[...] = tot
        else:
            out_ref[...] = gath_ref[...]

    out_shape = _sds((r_total, 128)) if reduce else _sds((N_DEV, r_total, 128))
    return pl.pallas_call(
        body, in_specs=[pl.BlockSpec(memory_space=pltpu.VMEM)], out_specs=pl.BlockSpec(memory_space=pltpu.VMEM),
        out_shape=out_shape,
        scratch_shapes=[pltpu.VMEM((N_DEV, r_total, 128), F32), pltpu.SemaphoreType.DMA((N_DEV - 1,)),
                        pltpu.SemaphoreType.DMA((N_DEV - 1,))],
        name="small_allreduce" if reduce else "small_allgather")(packed)


def _slot(ref, axis, idx, size):
    sel = [slice(None)] * len(ref.shape)
    sel[axis] = idx if size is None else pl.ds(pl.multiple_of(idx * size, size), size)
    return ref.at[tuple(sel)]


def _big_exchange(srcs, dst_shapes, src_view, dst_view, name):
    n = len(srcs)

    def body(*refs):
        src_refs, dst_refs = refs[:n], refs[n:2 * n]
        send_sems, recv_sems, local_sems = refs[2 * n:]
        me, peers = _me_and_peers()
        local, remote = [], []
        for t in range(n):
            loc = pltpu.make_async_copy(src_view(t, src_refs[t], me), dst_view(t, dst_refs[t], me), local_sems.at[t])
            loc.start()
            local.append(loc)
            for k, (dev, idx) in enumerate(peers):
                cp = pltpu.make_async_remote_copy(
                    src_ref=src_view(t, src_refs[t], idx), dst_ref=dst_view(t, dst_refs[t], me),
                    send_sem=send_sems.at[t, k], recv_sem=recv_sems.at[t, k], device_id=dev,
                    device_id_type=pl.DeviceIdType.MESH)
                cp.start()
                remote.append(cp)
        for t in range(n):
            for k, (dev, idx) in enumerate(peers):
                pltpu.make_async_remote_copy(
                    src_ref=src_view(t, src_refs[t], me), dst_ref=dst_view(t, dst_refs[t], idx),
                    send_sem=send_sems.at[t, k], recv_sem=recv_sems.at[t, k], device_id=dev,
                    device_id_type=pl.DeviceIdType.MESH).wait_recv()
        for cp in remote:
            cp.wait_send()
        for cp in local:
            cp.wait()

    any_spec = pl.BlockSpec(memory_space=pl.ANY)
    return pl.pallas_call(
        body, in_specs=[any_spec] * n, out_specs=[any_spec] * n, out_shape=dst_shapes,
        scratch_shapes=[pltpu.SemaphoreType.DMA((n, N_DEV - 1)), pltpu.SemaphoreType.DMA((n, N_DEV - 1)),
                        pltpu.SemaphoreType.DMA((n,))],
        name=name)(*srcs)


BIG = {
    "a_w_in": (1, (2, 1024, A_IN)),
    "b_w_in": (1, (2, 1024, 1024)),
    "w_kv_shared": (0, (1024, 256)),
    "mem_w_kv": (1, (4, 1024, 512)),
    "w_o": (1, (4, 1024, 1024)),
    "mlp_w_up": (2, (4, 1024, 4096)),
    "mlp_w_down": (1, (4, 4096, 1024)),
}
BIG_NAMES = tuple(BIG)


def _gather_plan(names, shards):
    dst_shapes, axes, sizes = [], [], []
    for name, s in zip(names, shards):
        axis = BIG[name][0] - (len(BIG[name][1]) - s.ndim)
        if name == "mlp_w_up":
            dst_shapes.append(_sds((N_DEV,) + s.shape, s.dtype))
            axes.append(0)
            sizes.append(None)
        else:
            dst_shapes.append(_sds(tuple(d * N_DEV if a == axis else d for a, d in enumerate(s.shape)), s.dtype))
            axes.append(axis)
            sizes.append(s.shape[axis])
    return dst_shapes, axes, sizes


def _gather_ride(names, shards):
    dst_shapes, axes, sizes = _gather_plan(names, shards)
    return _Ride(shards, dst_shapes, lambda t, ref, idx: ref, lambda t, ref, idx: _slot(ref, axes[t], idx, sizes[t]))


def _scatter_plan(names, grads):
    dst_shapes, axes, sizes = [], [], []
    for name, g in zip(names, grads):
        if name == "mlp_w_up":
            shard = (g.shape[0],) + g.shape[2:]
            axes.append(1)
            sizes.append(None)
        else:
            axis = BIG[name][0] - (len(BIG[name][1]) - g.ndim)
            shard = tuple(d // N_DEV if a == axis else d for a, d in enumerate(g.shape))
            axes.append(axis)
            sizes.append(shard[axis])
        dst_shapes.append(_sds((N_DEV,) + shard, g.dtype))
    return dst_shapes, axes, sizes


def _scatter_ride(names, grads):
    dst_shapes, axes, sizes = _scatter_plan(names, grads)
    return _Ride(grads, dst_shapes, lambda t, ref, idx: _slot(ref, axes[t], idx, sizes[t]),
                 lambda t, ref, idx: ref.at[idx])


def _allgather_weights(names, shards):
    dst_shapes, axes, sizes = _gather_plan(names, shards)
    n = len(shards)

    def body(*refs):
        src_refs, dst_refs = refs[:n], refs[n:2 * n]
        send_sems, recv_sems, local_sems = refs[2 * n:]
        x, y, c = lax.axis_index("x"), lax.axis_index("y"), lax.axis_index("c")
        sibling = (x, y, 1 - c)
        chips = [(1 - x, y), (x, 1 - y), (1 - x, 1 - y)]
        index = lambda px, py, pc: 4 * px + 2 * py + pc

        def copy(t, k, block, to, src=None):
            rows = _slot(dst_refs[t], axes[t], index(*block), sizes[t])
            return pltpu.make_async_remote_copy(
                src_ref=rows if src is None else src, dst_ref=rows, send_sem=send_sems.at[t, k],
                recv_sem=recv_sems.at[t, k], device_id=to, device_id_type=pl.DeviceIdType.MESH)

        started, local = [], []
        for t in range(n):
            mine = pltpu.make_async_copy(src_refs[t], _slot(dst_refs[t], axes[t], index(x, y, c), sizes[t]),
                                         local_sems.at[t])
            mine.start()
            local.append(mine)
            first = [copy(t, 0, (x, y, c), sibling, src=src_refs[t])]
            first += [copy(t, 1 + j, (x, y, c), (*chip, c), src=src_refs[t]) for j, chip in enumerate(chips)]
            for cp in first:
                cp.start()
            started += first
        for t in range(n):
            for j, chip in enumerate(chips):
                copy(t, 1 + j, (*chip, c), (x, y, c)).wait_recv()
                passed = copy(t, 4 + j, (*chip, c), sibling)
                passed.start()
                started.append(passed)
        for t in range(n):
            copy(t, 0, sibling, (x, y, c)).wait_recv()
            for j, chip in enumerate(chips):
                copy(t, 4 + j, (*chip, 1 - c), (x, y, c)).wait_recv()
        for cp in started:
            cp.wait_send()
        for cp in local:
            cp.wait()

    any_spec = pl.BlockSpec(memory_space=pl.ANY)
    return pl.pallas_call(
        body, in_specs=[any_spec] * n, out_specs=[any_spec] * n, out_shape=dst_shapes,
        scratch_shapes=[pltpu.SemaphoreType.DMA((n, N_DEV - 1)), pltpu.SemaphoreType.DMA((n, N_DEV - 1)),
                        pltpu.SemaphoreType.DMA((n,))],
        name="allgather_weights")(*shards)


def _scatter_grads(names, grads):
    dst_shapes, axes, sizes = _scatter_plan(names, grads)

    def src_view(t, ref, idx):
        return _slot(ref, axes[t], idx, sizes[t])

    def dst_view(t, ref, idx):
        return ref.at[idx]

    return _big_exchange(grads, dst_shapes, src_view, dst_view, "scatter_grads")


def _pad_row(vec, width=128):
    return jnp.pad(vec.astype(F32), (0, width - vec.shape[0])).reshape(1, width)


def _block_sizes(t_total):
    return dict(row=min(256, t_total), dn=min(512, t_total), swa=min(256, t_total), scan=min(256, t_total))


def _ln_apply(h, mix, g, b, tb):
    t_total, d = h.shape
    fwd = lambda pids, *a: _ln_fn(pids, *a) * 2
    return _block_fwd(fwd, [h, mix, g, b], [_rows(tb, d), _rows(tb, d), _whole((1, d)), _whole((1, d))],
                      [_sds((t_total, d)), _sds((t_total, d), BF16)], [_rows(tb, d), _rows(tb, d)],
                      (t_total // tb,), "ln_fwd")


def _ln_grad(h, mix, g, b, dy, tb):
    t_total, d = h.shape
    return _block_bwd(_ln_fn, [h, mix, g, b], [_rows(tb, d), _rows(tb, d), _whole((1, d)), _whole((1, d))],
                      [dy], [_rows(tb, d)], ["s", "s", "a", "a"],
                      [_sds((t_total, d)), _sds((t_total, d), BF16), _sds((1, d)), _sds((1, d))],
                      [_rows(tb, d), _rows(tb, d), _whole((1, d)), _whole((1, d))], (t_total // tb,), "ln_bwd")


def _memattn_specs(tb, qcol):
    return [pl.BlockSpec((tb, MEM_W), lambda i: (i, qcol)), _whole((MEM_W, 2 * MEM_W))]


def _act_epilogue(acc):
    r = jnp.maximum(acc, 0.0)
    return (r * r,)


def _dact_epilogue(acc, act):
    return (acc * (2.0 * jnp.sqrt(act.astype(F32))),)


def _add_epilogue(acc, other):
    return (acc + other,)


def _key(name, layer):
    if name == "w_kv_shared":
        return (name, None)
    return (name, layer - N_A if name == "b_w_in" else layer)


_PER_LAYER = ("mem_w_kv", "w_o", "mlp_w_up", "mlp_w_down")
GATHER_FIRST = [_key("a_w_in", 0)]
GATHER_ON_ROWA0 = [_key(n, 0) for n in _PER_LAYER]
GATHER_ON_DN1_0 = [_key("a_w_in", 1), _key("w_kv_shared", 1)] + [_key(n, 1) for n in _PER_LAYER]
GATHER_ON_DN1_1 = [_key("b_w_in", 2), _key("b_w_in", 3)] + [_key(n, l) for l in (2, 3) for n in _PER_LAYER]
SCATTER_ON_DN1_BWD_1 = GATHER_ON_DN1_1
SCATTER_ON_DN1_BWD_0 = GATHER_ON_DN1_0 + GATHER_ON_ROWA0
SCATTER_LAST = GATHER_FIRST
ALL_KEYS = GATHER_FIRST + GATHER_ON_ROWA0 + GATHER_ON_DN1_0 + GATHER_ON_DN1_1


def _local_step(x, mem, positions, target, ready, shards, small):
    t_total = x.shape[0]
    bs = _block_sizes(t_total)
    tb, tdn, tsw = bs["row"], bs["dn"], bs["swa"]
    nb = t_total // tb
    nbs = t_total // tsw

    inv_freq = ROPE_THETA ** (-jnp.arange(0, SWA_DH, 2, dtype=F32) / SWA_DH)
    ang = positions.astype(F32)[:, None] * inv_freq
    cos = jnp.tile(jnp.cos(ang), (1, 4))
    sin = jnp.tile(jnp.sin(ang), (1, 4))

    mem_b = mem.astype(BF16)
    ready = dict(ready)
    derived = {}

    def gather_ride(keys):
        names = [k[0] for k in keys]
        return names, (_gather_ride(names, [shards[k] for k in keys]) if shards is not None else None)

    def weight(name, l):
        key = _key(name, l)
        if key not in derived:
            w = ready[key]
            if name == "a_w_in":
                w = jnp.concatenate([w[0][:, :3072], w[0][:, 3084:], w[0][:, 3072:3084],
                                     jnp.zeros((D_MODEL, A_IN_PAD - A_IN), BF16)], axis=1)
            elif name == "w_kv_shared":
                w = jnp.concatenate([w[:, 64 * (i // 2):64 * (i // 2 + 1)] for i in range(8)], axis=1)
            elif name == "mlp_w_up":
                w = jnp.moveaxis(w[:, 0], 0, 1).reshape(D_MODEL, 4 * D_MODEL)
            else:
                w = w[0]
            derived[key] = w
        return derived[key]

    saved = []
    h, hb = x, x.astype(BF16)
    kr = vd_src = None
    for l in range(DEPTH):
        sv = dict(h=h, hb=hb)
        if l < N_A:
            proj = _matmul(hb, weight("a_w_in", l), "nn", [F32], "mm_proj_a", tn=1152)
            conv_w = small["a_conv_w"][l]
            c = _conv_fwd(proj, conv_w, tb)
            alog, dtb = _pad_row(small["a_A_log"][l]), _pad_row(small["a_dt_bias"][l])
            rowa_in = [c, proj, alog, dtb]
            rowa_specs = [_rows(tb, 3 * DN_W), _rows(tb, 128, 26), _whole((1, 128)), _whole((1, 128))]
            rowa_args = (_rowa_fn, rowa_in, rowa_specs, [_sds((t_total, DN_W))] * 5, [_rows(tb, DN_W)] * 5, (nb,))
            if shards is not None and l == 0:
                names, ride = gather_ride(GATHER_ON_ROWA0)
                (q, k, v, gcb, betab), got = _block_fwd(*rowa_args, "rowa_fwd_gather", ride=ride)
                ready.update(zip(GATHER_ON_ROWA0, got))
            else:
                q, k, v, gcb, betab = _block_fwd(*rowa_args, "rowa_fwd")
            hs = _head_spec(tdn)
            dn_grid = (DN_HEADS, t_total // tdn)
            full = _sds((t_total, DN_W))
            full_b = _sds((t_total, DN_W), BF16)
            dn1_out_shapes = [full, full_b, _sds((DN_HEADS, t_total, PAIR), BF16), full_b, full_b, full,
                              _sds((DN_HEADS, t_total, PAIR))]
            dn1_out_specs = [hs, hs, _intra_spec(tdn), hs, hs, hs, _intra_spec(tdn)]
            if shards is not None:
                keys = GATHER_ON_DN1_0 if l == 0 else GATHER_ON_DN1_1
                names, ride = gather_ride(keys)
                (u, w, intra, qd, kd, cd, tinv), got = _block_fwd(
                    _dn1_fn, [q, k, v, gcb, betab], [hs] * 5, dn1_out_shapes, dn1_out_specs, dn_grid,
                    "dn1_fwd_gather%d" % l, ride=ride)
                ready.update(zip(keys, got))
            else:
                u, w, intra, qd, kd, cd, tinv = _block_fwd(_dn1_fn, [q, k, v, gcb, betab], [hs] * 5, dn1_out_shapes,
                                                           dn1_out_specs, dn_grid, "dn1_fwd")
            o, states = _dn2_fwd(qd, kd, u, w, intra, cd, bs["scan"])
            nw = small["a_norm_w"][l].reshape(1, DN_D)
            post_in = [o, proj, nw]
            post_specs = [_rows(tb, DN_W), _rows(tb, DN_W, 3), _whole((1, DN_D))]
            (og,) = _block_fwd(_post_fn, post_in, post_specs, [_sds((t_total, DN_W), BF16)], [_rows(tb, DN_W)],
                               (nb,), "post_fwd")
            qm_col = 12
            sv.update(proj=proj, c=c, rowa_in=rowa_in, rowa_specs=rowa_specs, dn1_in=[q, k, v, gcb, betab, tinv],
                      dn2_in=[qd, kd, u, w, intra, cd], states=states, post_in=post_in, post_specs=post_specs,
                      conv_w=conv_w)
        else:
            jb = l - N_A
            proj = _matmul(hb, weight("b_w_in", l), "nn", [F32], "mm_proj_b")
            sinks = _pad_row(small["b_sinks"][jb])
            swa_in = [proj, cos, sin, kr, kr, vd_src, vd_src, sinks]
            swa_specs = [_rows(tsw, DN_W), _rows(tsw, 128), _rows(tsw, 128),
                         pl.BlockSpec((WINDOW, 256), lambda i: (jnp.maximum(i * (tsw // WINDOW) - 1, 0), 0)),
                         _rows(tsw, 256),
                         pl.BlockSpec((WINDOW, 256), lambda i: (jnp.maximum(i * (tsw // WINDOW) - 1, 0), 1)),
                         _rows(tsw, 256, 1), _whole((1, 128))]
            (og,) = _block_fwd(_swa_fn, swa_in, swa_specs, [_sds((t_total, DN_W), BF16)], [_rows(tsw, DN_W)],
                               (nbs,), "swa_fwd")
            qm_col = 3
            sv.update(proj=proj, swa_in=swa_in, swa_specs=swa_specs)
        kvm = _matmul(mem_b, weight("mem_w_kv", l), "nn", [F32], "mm_memkv", tm=256)
        mem_in = [proj, kvm]
        (mo,) = _block_fwd(_memattn_fn, mem_in, _memattn_specs(tb, qm_col), [_sds((t_total, MEM_W), BF16)],
                           [_rows(tb, MEM_W)], (nb,), "memattn_fwd")
        mixin = jnp.concatenate([og, mo], axis=1)
        mix = _matmul(mixin, weight("w_o", l), "nn", [F32], "mm_wo")
        g0, b0 = small["ln_g"][l, 0].reshape(1, -1), small["ln_b"][l, 0].reshape(1, -1)
        h1, h1b = _ln_apply(h, mix, g0, b0, tb)
        act = _matmul(h1b, weight("mlp_w_up", l), "nn", [BF16], "mm_up", epi=_act_epilogue, tm=2048)
        mlp = _matmul(act, weight("mlp_w_down", l), "nn", [F32], "mm_down", tk=2048)
        g1, b1 = small["ln_g"][l, 1].reshape(1, -1), small["ln_b"][l, 1].reshape(1, -1)
        h2, h2b = _ln_apply(h1, mlp, g1, b1, tb)
        sv.update(kvm=kvm, mem_in=mem_in, qm_col=qm_col, mixin=mixin, mix=mix, ln0=(g0, b0), h1=h1, h1b=h1b,
                  act=act, mlp=mlp, ln1=(g1, b1))
        saved.append(sv)
        h, hb = h2, h2b
        if l == N_A - 1:
            kvd = _matmul(hb, weight("w_kv_shared", l), "nn", [F32], "mm_kvd")
            krope_in = [kvd, cos, sin]
            krope_specs = [_rows(tb, 256), _rows(tb, 128), _rows(tb, 128)]
            (kr,) = _block_fwd(_krope_fn, krope_in, krope_specs, [_sds((t_total, 256))], [_rows(tb, 256)], (nb,),
                               "krope_fwd")
            vd_src = kvd

    loss, dh = _loss_and_grad(h, target, tb)

    grads = {}

    def scatter_ride(keys):
        return _scatter_ride([k[0] for k in keys], [grads[k] for k in keys]) if shards is not None else None

    sg = dict(a_conv_w=[None] * N_A, a_A_log=[None] * N_A, a_dt_bias=[None] * N_A, a_norm_w=[None] * N_A,
              b_sinks=[None] * (DEPTH - N_A), ln_g=[[None, None] for _ in range(DEPTH)],
              ln_b=[[None, None] for _ in range(DEPTH)])
    dk_parts, dv_parts = [], []
    for l in reversed(range(DEPTH)):
        sv = saved[l]
        if l == N_A - 1:
            dkr = _halo_sum([p[0] for p in dk_parts], [p[1] for p in dk_parts], tsw)
            dvv = _halo_sum([p[0] for p in dv_parts], [p[1] for p in dv_parts], tsw)
            (dkraw,) = _block_bwd(_krope_fn, krope_in, krope_specs, [dkr], [_rows(tb, 256)], ["s", None, None],
                                  [_sds((t_total, 256), BF16)], [_rows(tb, 256)], (nb,), "krope_bwd")
            dkvd = jnp.concatenate([dkraw, dvv.astype(BF16)], axis=1)
            g_kvd = _matmul(saved[l + 1]["hb"], dkvd, "tn", [F32], "mm_dw_kvd", tm=1024, tn=512)
            dh = _matmul(dkvd, weight("w_kv_shared", l), "nt", [F32], "mm_dx_kvd", epi=_add_epilogue, extras=[dh],
                         tn=1024, tk=512)
            grads[_key("w_kv_shared", l)] = jnp.concatenate(
                [g_kvd[:, 128 * i:128 * i + 64] + g_kvd[:, 128 * i + 64:128 * (i + 1)] for i in range(4)],
                axis=1).astype(BF16)
        g1, b1 = sv["ln1"]
        dh1a, dmlp, dg1, db1 = _ln_grad(sv["h1"], sv["mlp"], g1, b1, dh, tb)
        dup = _matmul(dmlp, weight("mlp_w_down", l), "nt", [BF16], "mm_dact", epi=_dact_epilogue, extras=[sv["act"]])
        grads[_key("mlp_w_down", l)] = _matmul(sv["act"], dmlp, "tn", [BF16], "mm_dw_down", tk=4096)[None]
        g_up = _matmul(sv["h1b"], dup, "tn", [BF16], "mm_dw_up", tk=4096)
        grads[_key("mlp_w_up", l)] = jnp.moveaxis(g_up.reshape(D_MODEL, N_DEV, 512), 1, 0)[None]
        dh1 = _matmul(dup, weight("mlp_w_up", l), "nt", [F32], "mm_dx_up", epi=_add_epilogue, extras=[dh1a], tk=2048)
        g0, b0 = sv["ln0"]
        dha, dmix, dg0, db0 = _ln_grad(sv["h"], sv["mix"], g0, b0, dh1, tb)
        sg["ln_g"][l] = [dg0, dg1]
        sg["ln_b"][l] = [db0, db1]
        grads[_key("w_o", l)] = _matmul(sv["mixin"], dmix, "tn", [BF16], "mm_dw_o", tk=4096)[None]
        dmixin = _matmul(dmix, weight("w_o", l), "nt", [F32], "mm_dx_o", tn=1024)
        dqm, dkvm = _block_bwd(_memattn_fn, sv["mem_in"], _memattn_specs(tb, sv["qm_col"]), [dmixin],
                               [_rows(tb, MEM_W, 3)], ["s", "a"],
                               [_sds((t_total, MEM_W), BF16), _sds((MEM_W, 2 * MEM_W))],
                               [_rows(tb, MEM_W), _whole((MEM_W, 2 * MEM_W))], (nb,), "memattn_bwd")
        grads[_key("mem_w_kv", l)] = _matmul(mem_b, dkvm.astype(BF16), "tn", [BF16], "mm_dw_memkv", tm=1024,
                                             tn=512)[None]
        if l < N_A:
            d_o, dz, dnw = _block_bwd(_post_fn, sv["post_in"], sv["post_specs"], [dmixin], [_rows(tb, DN_W)],
                                      ["s", "s", "a"],
                                      [_sds((t_total, DN_W)), _sds((t_total, DN_W), BF16), _sds((1, DN_D))],
                                      [_rows(tb, DN_W), _rows(tb, DN_W), _whole((1, DN_D))], (nb,), "post_bwd")
            sg["a_norm_w"][l] = dnw
            dqd, dkd, du, dw, da, dcd = _dn2_bwd(*sv["dn2_in"], sv["states"], d_o, bs["scan"])
            hs = _head_spec(tdn)
            full = _sds((t_total, DN_W))
            dn1_bwd_args = (_dn1_fn_known, sv["dn1_in"], [hs] * 5 + [_intra_spec(tdn)], [du, dw, da, dqd, dkd, dcd],
                            [hs, hs, _intra_spec(tdn), hs, hs, hs], ["s"] * 5 + [None], [full] * 5, [hs] * 5,
                            (DN_HEADS, t_total // tdn))
            if shards is not None:
                keys = SCATTER_ON_DN1_BWD_1 if l == N_A - 1 else SCATTER_ON_DN1_BWD_0
                (dq, dk, dv, dgc, dbeta), got = _block_bwd(*dn1_bwd_args, "dn1_bwd_scatter%d" % l,
                                                           ride=scatter_ride(keys))
                grads.update(zip(keys, got))
            else:
                dq, dk, dv, dgc, dbeta = _block_bwd(*dn1_bwd_args, "dn1_bwd")
            dc, dab, dalog, ddtb = _block_bwd(
                _rowa_fn, sv["rowa_in"], sv["rowa_specs"], [dq, dk, dv, dgc, dbeta], [_rows(tb, DN_W)] * 5,
                ["s", "s", "a", "a"],
                [_sds((t_total, 3 * DN_W)), _sds((t_total, 128), BF16), _sds((1, 128)), _sds((1, 128))],
                [_rows(tb, 3 * DN_W), _rows(tb, 128), _whole((1, 128)), _whole((1, 128))], (nb,), "rowa_bwd")
            sg["a_A_log"][l] = dalog[0, :DN_HEADS]
            sg["a_dt_bias"][l] = ddtb[0, :DN_HEADS]
            dx, dconv = _conv_bwd(dc, sv["proj"], sv["conv_w"], tb)
            sg["a_conv_w"][l] = dconv
            dproj = jnp.concatenate([dx, dz, dqm, dab], axis=1)
            g_in = _matmul(sv["hb"], dproj, "tn", [BF16], "mm_dw_a", tm=1024, tn=1152, tk=2048)
            grads[_key("a_w_in", l)] = jnp.concatenate([g_in[:, :3072], g_in[:, 3328:3340], g_in[:, 3072:3328]],
                                                       axis=1)[None]
            dh = _matmul(dproj, weight("a_w_in", l), "nt", [F32], "mm_dx_a", epi=_add_epilogue, extras=[dha], tk=1152)
        else:
            jb = l - N_A
            swa_kinds = ["s", None, None, "s", "s", "s", "s", "a"]
            halo_spec = pl.BlockSpec((None, WINDOW, 256), lambda i: (i, 0, 0))
            dq, dkh, dkc, dvh, dvc, dsink = _block_bwd(
                _swa_fn, sv["swa_in"], sv["swa_specs"], [dmixin], [_rows(tsw, DN_W)], swa_kinds,
                [_sds((t_total, DN_W), BF16), _sds((nbs, WINDOW, 256)), _sds((t_total, 256)),
                 _sds((nbs, WINDOW, 256)), _sds((t_total, 256)), _sds((1, 128))],
                [_rows(tsw, DN_W), halo_spec, _rows(tsw, 256), halo_spec, _rows(tsw, 256), _whole((1, 128))],
                (nbs,), "swa_bwd")
            sg["b_sinks"][jb] = dsink[0, :SWA_HEADS]
            dk_parts.append((dkc, dkh))
            dv_parts.append((dvc, dvh))
            dproj = jnp.concatenate([dq, dqm], axis=1)
            grads[_key("b_w_in", l)] = _matmul(sv["hb"], dproj, "tn", [BF16], "mm_dw_b", tk=4096)[None]
            dh = _matmul(dproj, weight("b_w_in", l), "nt", [F32], "mm_dx_b", epi=_add_epilogue, extras=[dha], tn=1024)

    small_grads = dict(
        a_conv_w=jnp.stack(sg["a_conv_w"]), a_A_log=jnp.stack(sg["a_A_log"]), a_dt_bias=jnp.stack(sg["a_dt_bias"]),
        a_norm_w=jnp.concatenate(sg["a_norm_w"], axis=0), b_sinks=jnp.stack(sg["b_sinks"]),
        ln_g=jnp.stack([jnp.concatenate(p, axis=0) for p in sg["ln_g"]]),
        ln_b=jnp.stack([jnp.concatenate(p, axis=0) for p in sg["ln_b"]]))
    return loss, dh, grads, small_grads


def _pack(arrays, rows):
    flat = []
    for a in arrays:
        v = a.astype(F32).reshape(-1)
        flat.append(jnp.pad(v, (0, (-v.shape[0]) % 128)))
    flat = jnp.concatenate(flat)
    return jnp.pad(flat, (0, rows * 128 - flat.shape[0])).reshape(rows, 128)


def _unpack(slab, shapes):
    flat = slab.reshape(slab.shape[:-2] + (-1,))
    out, off = [], 0
    for s in shapes:
        n = math.prod(s)
        out.append(flat[..., off:off + n].reshape(slab.shape[:-2] + tuple(s)))
        off += n + (-n) % 128
    return out


def _rows_for(shapes):
    rows = sum((math.prod(s) + 127) // 128 for s in shapes)
    return rows + (-rows) % 8


SMALL_NAMES = ("a_conv_w", "a_A_log", "a_dt_bias", "a_norm_w", "b_sinks", "ln_g", "ln_b")
SMALL_SHARDED = {"a_conv_w": 2, "ln_g": 2, "ln_b": 2}
SMALL_FULL = {"a_conv_w": (2, 4, 2304), "a_A_log": (2, 6), "a_dt_bias": (2, 6), "a_norm_w": (2, 128),
              "b_sinks": (2, 12), "ln_g": (4, 2, 1024), "ln_b": (4, 2, 1024)}


def kernel(x, mem, positions, a_w_in, a_conv_w, a_A_log, a_dt_bias, a_norm_w, b_w_in, b_sinks, w_kv_shared, mem_w_kv, w_o, mlp_w_up, mlp_w_down, ln_g, ln_b, loss_target, m_a_w_in, m_a_conv_w, m_a_A_log, m_a_dt_bias, m_a_norm_w, m_b_w_in, m_b_sinks, m_w_kv_shared, m_mem_w_kv, m_w_o, m_mlp_w_up, m_mlp_w_down, m_ln_g, m_ln_b, v_a_w_in, v_a_conv_w, v_a_A_log, v_a_dt_bias, v_a_norm_w, v_b_w_in, v_b_sinks, v_w_kv_shared, v_mem_w_kv, v_w_o, v_mlp_w_up, v_mlp_w_down, v_ln_g, v_ln_b):
    params = dict(a_w_in=a_w_in, a_conv_w=a_conv_w, a_A_log=a_A_log, a_dt_bias=a_dt_bias, a_norm_w=a_norm_w,
                  b_w_in=b_w_in, b_sinks=b_sinks, w_kv_shared=w_kv_shared, mem_w_kv=mem_w_kv, w_o=w_o,
                  mlp_w_up=mlp_w_up, mlp_w_down=mlp_w_down, ln_g=ln_g, ln_b=ln_b)
    mom = dict(a_w_in=m_a_w_in, a_conv_w=m_a_conv_w, a_A_log=m_a_A_log, a_dt_bias=m_a_dt_bias, a_norm_w=m_a_norm_w,
               b_w_in=m_b_w_in, b_sinks=m_b_sinks, w_kv_shared=m_w_kv_shared, mem_w_kv=m_mem_w_kv, w_o=m_w_o,
               mlp_w_up=m_mlp_w_up, mlp_w_down=m_mlp_w_down, ln_g=m_ln_g, ln_b=m_ln_b)
    var = dict(a_w_in=v_a_w_in, a_conv_w=v_a_conv_w, a_A_log=v_a_A_log, a_dt_bias=v_a_dt_bias, a_norm_w=v_a_norm_w,
               b_w_in=v_b_w_in, b_sinks=v_b_sinks, w_kv_shared=v_w_kv_shared, mem_w_kv=v_mem_w_kv, w_o=v_w_o,
               mlp_w_up=v_mlp_w_up, mlp_w_down=v_mlp_w_down, ln_g=v_ln_g, ln_b=v_ln_b)
    me = 4 * lax.axis_index("x") + 2 * lax.axis_index("y") + lax.axis_index("c")

    shards = {(n, i): (params[n] if i is None else params[n][i:i + 1]).astype(BF16) for n, i in ALL_KEYS}
    first = [k[0] for k in GATHER_FIRST]
    ready = dict(zip(GATHER_FIRST, _allgather_weights(first, [shards[k] for k in GATHER_FIRST])))
    sharded_names = [n for n in SMALL_NAMES if n in SMALL_SHARDED]
    shard_shapes = [params[n].shape for n in sharded_names]
    gathered = _small_exchange(_pack([params[n] for n in sharded_names], _rows_for(shard_shapes)), reduce=False)
    small = {n: params[n] for n in SMALL_NAMES if n not in SMALL_SHARDED}
    for n, g in zip(sharded_names, _unpack(gathered, shard_shapes)):
        small[n] = jnp.moveaxis(g, 0, 2).reshape(SMALL_FULL[n])

    loss, dx, recv, small_grads = _local_step(x[0], mem[0], positions[0], loss_target[0], ready, shards, small)
    loss = lax.psum(loss, ("x", "y", "c"))
    last = [k[0] for k in SCATTER_LAST]
    recv.update(zip(SCATTER_LAST, _scatter_grads(last, [recv[k] for k in SCATTER_LAST])))
    out = {}
    for n in BIG_NAMES:
        shp = params[n].shape
        rows = math.prod(shp[:-1])
        recvs = [recv[k].reshape(N_DEV, -1, shp[-1]) for k in sorted(k for k in ALL_KEYS if k[0] == n)]
        res = _adamw(recvs, params[n].reshape(rows, shp[-1]), mom[n].reshape(rows, shp[-1]),
                     var[n].reshape(rows, shp[-1]), 32, "adamw_" + n)
        out[n] = [t.reshape(shp) for t in res]
    full_shapes = [SMALL_FULL[n] for n in SMALL_NAMES]
    summed = _small_exchange(_pack([small_grads[n] for n in SMALL_NAMES], _rows_for(full_shapes)), reduce=True)
    local_g = []
    for n, g in zip(SMALL_NAMES, _unpack(summed, full_shapes)):
        if n in SMALL_SHARDED:
            size = params[n].shape[2]
            g = lax.dynamic_slice_in_dim(g, me * size, size, axis=2)
        local_g.append(g)
    local_shapes = [params[n].shape for n in SMALL_NAMES]
    rows = _rows_for(local_shapes)
    res = _adamw([_pack(local_g, rows)[None]], _pack([params[n] for n in SMALL_NAMES], rows),
                 _pack([mom[n] for n in SMALL_NAMES], rows), _pack([var[n] for n in SMALL_NAMES], rows), rows,
                 "adamw_small")
    unpacked = [_unpack(t, local_shapes) for t in res]
    for i, n in enumerate(SMALL_NAMES):
        out[n] = [unpacked[k][i] for k in range(4)]

    order = ("a_w_in", "a_conv_w", "a_A_log", "a_dt_bias", "a_norm_w", "b_w_in", "b_sinks", "w_kv_shared",
             "mem_w_kv", "w_o", "mlp_w_up", "mlp_w_down", "ln_g", "ln_b")
    return (loss, dx[None], *[out[n][0] for n in order], *[out[n][1] for n in order],
            *[out[n][2] for n in order], *[out[n][3] for n in order])
```

```python
import functools
import math

import jax
import jax.numpy as jnp
from jax import lax
from jax.experimental import pallas as pl
from jax.experimental.pallas import tpu as pltpu

F32 = jnp.float32
BF16 = jnp.bfloat16

D_MODEL = 1024
DEPTH = 4
N_A = 2
MEM_HEADS = 4
MEM_DH = 64
MEM_W = 256
DN_HEADS = 6
DN_D = 128
DN_W = 768
CHUNK = 64
SWA_DH = 64
SWA_HEADS = 12
WINDOW = 128
ROPE_THETA = 10000.0
LN_EPS = 1e-5
NORM_EPS = 1e-6
DN_ALPHA = (2.0 * DEPTH) ** 0.25
A_IN = 3340
A_IN_PAD = 3456
N_DEV = 8

ADAM_LR = 0.001
ADAM_B1 = 0.9
ADAM_B2 = 0.999
ADAM_EPS = 1e-08
ADAM_WD = 0.01
ADAM_STEP = 10

VMEM_LIMIT = 52 * 1024 * 1024
NEG_BIG = -1e30


def _cparams(sem):
    return pltpu.CompilerParams(dimension_semantics=sem, vmem_limit_bytes=VMEM_LIMIT)


_CONTRACT = {"nn": (1, 0), "nt": (1, 1), "tn": (0, 0)}


def _raw_mm(a, b, mode, prec):
    ca, cb = _CONTRACT[mode]
    dims = (((ca,), (cb,)), ((), ()))
    dot = lambda p, q: lax.dot_general(p, q, dims, preferred_element_type=F32)
    if prec == "bf16":
        return dot(a.astype(BF16), b.astype(BF16))
    a, b = a.astype(F32), b.astype(F32)
    a_hi, b_hi = a.astype(BF16), b.astype(BF16)
    if prec == "sela":
        return dot(a_hi, b_hi) + dot(a_hi, (b - b_hi.astype(F32)).astype(BF16))
    a_lo = (a - a_hi.astype(F32)).astype(BF16)
    if prec == "selb":
        return dot(a_hi, b_hi) + dot(a_lo, b_hi)
    b_lo = (b - b_hi.astype(F32)).astype(BF16)
    return dot(a_hi, b_hi) + (dot(a_hi, b_lo) + dot(a_lo, b_hi))


@functools.partial(jax.custom_vjp, nondiff_argnums=(2, 3))
def mm(a, b, mode, prec):
    return _raw_mm(a, b, mode, prec)


def _mm_fwd(a, b, mode, prec):
    return _raw_mm(a, b, mode, prec), (a, b)


def _mm_bwd(mode, prec, res, ct):
    a, b = res
    if prec == "sela":
        pa, pb = "f32", {"nn": "sela", "nt": "selb", "tn": "sela"}[mode]
    elif prec == "selb":
        pa, pb = {"nn": "selb", "nt": "selb", "tn": "sela"}[mode], "f32"
    else:
        pa = pb = prec
    if mode == "nn":
        return mm(ct, b, "nt", pa), mm(a, ct, "tn", pb)
    if mode == "nt":
        return mm(ct, b, "nn", pa), mm(ct, a, "tn", pb)
    return mm(b, ct, "nt", pa), mm(a, ct, "nn", pb)


mm.defvjp(_mm_fwd, _mm_bwd)


@jax.custom_vjp
def _softplus(x):
    y = jnp.exp(-jnp.abs(x))
    log1p_y = jnp.where(y < 1e-2, y * (1.0 - y * (0.5 - y * (1.0 / 3.0))), jnp.log(1.0 + y))
    return jnp.maximum(x, 0.0) + log1p_y


def _softplus_fwd(x):
    return _softplus(x), x


def _softplus_bwd(x, ct):
    return (ct * jax.nn.sigmoid(x),)


_softplus.defvjp(_softplus_fwd, _softplus_bwd)


def _iota(shape, dim):
    return lax.broadcasted_iota(jnp.int32, shape, dim)


class _Ride:
    def __init__(self, srcs, dst_shapes, src_view, dst_view):
        self.srcs, self.dst_shapes, self.src_view, self.dst_view = list(srcs), list(dst_shapes), src_view, dst_view
        self.n = len(self.srcs)
        self.any_specs = [pl.BlockSpec(memory_space=pl.ANY)] * self.n
        self.scratch = [pltpu.SemaphoreType.DMA((self.n, N_DEV - 1)), pltpu.SemaphoreType.DMA((self.n, N_DEV - 1)),
                        pltpu.SemaphoreType.DMA((self.n,))]

    def copies(self, src_refs, dst_refs, sems):
        send_sems, recv_sems, local_sems = sems
        me, peers = _me_and_peers()
        local, out, inc = [], [], []
        for t in range(self.n):
            local.append(pltpu.make_async_copy(self.src_view(t, src_refs[t], me), self.dst_view(t, dst_refs[t], me),
                                               local_sems.at[t]))
            for k, (dev, idx) in enumerate(peers):
                mk = lambda s, d: pltpu.make_async_remote_copy(
                    src_ref=s, dst_ref=d, send_sem=send_sems.at[t, k], recv_sem=recv_sems.at[t, k], device_id=dev,
                    device_id_type=pl.DeviceIdType.MESH)
                out.append(mk(self.src_view(t, src_refs[t], idx), self.dst_view(t, dst_refs[t], me)))
                inc.append(mk(self.src_view(t, src_refs[t], me), self.dst_view(t, dst_refs[t], idx)))
        return local, out, inc

    def start(self, grid, src_refs, dst_refs, sems):
        first = pl.program_id(0) == 0
        for a in range(1, len(grid)):
            first = jnp.logical_and(first, pl.program_id(a) == 0)

        @pl.when(first)
        def _():
            local, out, _ = self.copies(src_refs, dst_refs, sems)
            for cp in local + out:
                cp.start()

    def finish(self, grid, src_refs, dst_refs, sems):
        last = pl.program_id(0) == grid[0] - 1
        for a in range(1, len(grid)):
            last = jnp.logical_and(last, pl.program_id(a) == grid[a] - 1)

        @pl.when(last)
        def _():
            local, out, inc = self.copies(src_refs, dst_refs, sems)
            for cp in inc:
                cp.wait_recv()
            for cp in out:
                cp.wait_send()
            for cp in local:
                cp.wait()


def _block_fwd(fn, ins, in_specs, out_shapes, out_specs, grid, name, ride=None):
    n_in, n_out = len(ins), len(out_shapes)
    n_ride = ride.n if ride else 0

    def body(*refs):
        pids = tuple(pl.program_id(a) for a in range(len(grid)))
        ride_refs = (refs[n_in:n_in + n_ride], refs[n_in + n_ride + n_out:n_in + 2 * n_ride + n_out],
                     refs[n_in + 2 * n_ride + n_out:])
        if ride:
            ride.start(grid, *ride_refs)
        vals = [r[...].astype(F32) for r in refs[:n_in]]
        outs = fn(pids, *vals)
        for r, o in zip(refs[n_in + n_ride:n_in + n_ride + n_out], outs):
            r[...] = o.astype(r.dtype)
        if ride:
            ride.finish(grid, *ride_refs)

    if not ride:
        return pl.pallas_call(
            body, grid=grid, in_specs=in_specs, out_specs=out_specs, out_shape=out_shapes, name=name,
            compiler_params=_cparams(("parallel",) * len(grid)))(*ins)
    res = pl.pallas_call(
        body, grid=grid, in_specs=list(in_specs) + ride.any_specs, out_specs=list(out_specs) + ride.any_specs,
        out_shape=list(out_shapes) + ride.dst_shapes, scratch_shapes=ride.scratch, name=name,
        compiler_params=_cparams(("arbitrary",) * len(grid)))(*ins, *ride.srcs)
    return res[:n_out], res[n_out:]


def _block_bwd(fn, ins, in_specs, cts, ct_specs, kinds, g_shapes, g_specs, grid, name, ride=None):
    n_in, n_ct, n_g = len(ins), len(cts), len(g_shapes)
    n_ride = ride.n if ride else 0
    didx = [i for i, k in enumerate(kinds) if k]

    def body(*refs):
        in_refs, ct_refs = refs[:n_in], refs[n_in:n_in + n_ct]
        base = n_in + n_ct
        g_refs = refs[base + n_ride:base + n_ride + n_g]
        ride_refs = (refs[base:base + n_ride], refs[base + n_ride + n_g:base + 2 * n_ride + n_g],
                     refs[base + 2 * n_ride + n_g:])
        if ride:
            ride.start(grid, *ride_refs)
        pids = tuple(pl.program_id(a) for a in range(len(grid)))
        vals = [r[...].astype(F32) for r in in_refs]

        def f(*dvals):
            full = list(vals)
            for i, v in zip(didx, dvals):
                full[i] = v
            return tuple(fn(pids, *full))

        _, vjp = jax.vjp(f, *[vals[i] for i in didx])
        gs = vjp(tuple(r[...].astype(F32) for r in ct_refs))
        first = pids[0] == 0
        for p in pids[1:]:
            first = jnp.logical_and(first, p == 0)
        for i, g, r in zip(didx, gs, g_refs):
            if kinds[i] == "s":
                r[...] = g.astype(r.dtype)
            else:
                @pl.when(first)
                def _(r=r):
                    r[...] = jnp.zeros(r.shape, r.dtype)

                r[...] += g.astype(r.dtype)
        if ride:
            ride.finish(grid, *ride_refs)

    sem = ("arbitrary",) * len(grid) if "a" in kinds or ride else ("parallel",) * len(grid)
    if not ride:
        return pl.pallas_call(
            body, grid=grid, in_specs=list(in_specs) + list(ct_specs), out_specs=g_specs, out_shape=g_shapes,
            name=name, compiler_params=_cparams(sem))(*ins, *cts)
    res = pl.pallas_call(
        body, grid=grid, in_specs=list(in_specs) + list(ct_specs) + ride.any_specs,
        out_specs=list(g_specs) + ride.any_specs, out_shape=list(g_shapes) + ride.dst_shapes,
        scratch_shapes=ride.scratch, name=name, compiler_params=_cparams(sem))(*ins, *cts, *ride.srcs)
    return res[:n_g], res[n_g:]


def _rows(tb, width, col=0):
    return pl.BlockSpec((tb, width), lambda i, col=col: (i, col))


def _whole(shape):
    return pl.BlockSpec(shape, lambda *_: (0,) * len(shape))


def _sds(shape, dtype=F32):
    return jax.ShapeDtypeStruct(shape, dtype)


def _matmul(a, b, mode, out_dtypes, name, epi=None, extras=(), tm=1024, tn=1024, tk=1024,
            b_spec=None, n_total=None, out_specs=None, out_shapes=None):
    if mode == "tn":
        k_total, m_total = a.shape
    else:
        m_total, k_total = a.shape
    if n_total is None:
        n_total = b.shape[0] if mode == "nt" else b.shape[1]
    tm, tn, tk = min(tm, m_total), min(tn, n_total), min(tk, k_total)
    assert m_total % tm == 0 and n_total % tn == 0 and k_total % tk == 0, (name, a.shape, b.shape)
    grid = (m_total // tm, n_total // tn, k_total // tk)
    nk = grid[2]
    if mode == "tn":
        a_spec = pl.BlockSpec((tk, tm), lambda i, j, k: (k, i))
    else:
        a_spec = pl.BlockSpec((tm, tk), lambda i, j, k: (i, k))
    if b_spec is None:
        if mode == "nt":
            b_spec = pl.BlockSpec((tn, tk), lambda i, j, k: (j, k))
        else:
            b_spec = pl.BlockSpec((tk, tn), lambda i, j, k: (k, j))
    tile = pl.BlockSpec((tm, tn), lambda i, j, k: (i, j))
    n_ex, n_out = len(extras), len(out_dtypes)
    ca, cb = _CONTRACT[mode]
    dims = (((ca,), (cb,)), ((), ()))

    def body(*refs):
        a_ref, b_ref = refs[:2]
        ex_refs = refs[2:2 + n_ex]
        out_refs = refs[2 + n_ex:2 + n_ex + n_out]
        part = lax.dot_general(a_ref[...], b_ref[...], dims, preferred_element_type=F32)

        def finish(val):
            res = epi(val, *[e[...] for e in ex_refs]) if epi is not None else (val,)
            for r, o in zip(out_refs, res):
                r[...] = o.astype(r.dtype)

        if nk == 1:
            finish(part)
        else:
            acc = refs[-1]
            k = pl.program_id(2)

            @pl.when(k == 0)
            def _():
                acc[...] = part

            @pl.when(k > 0)
            def _():
                acc[...] += part

            @pl.when(k == nk - 1)
            def _():
                finish(acc[...])

    if out_shapes is None:
        out_shapes = [_sds((m_total, n_total), d) for d in out_dtypes]
        out_specs = [tile] * n_out
    outs = pl.pallas_call(
        body, grid=grid, in_specs=[a_spec, b_spec] + [tile] * n_ex, out_specs=out_specs, out_shape=out_shapes,
        scratch_shapes=[pltpu.VMEM((tm, tn), F32)] if nk > 1 else [], name=name,
        compiler_params=_cparams(("parallel", "parallel", "arbitrary")))(a, b, *extras)
    return outs if n_out > 1 else outs[0]


def _matmul_ln(a, b, h, g, beta, name, tm=1024, tk=2048, chunk=256):
    m_total, k_total = a.shape
    n = b.shape[1]
    tm, tk = min(tm, m_total), min(tk, k_total)
    chunk = min(chunk, tm)
    assert m_total % tm == 0 and k_total % tk == 0 and tm % chunk == 0, (name, a.shape, b.shape)
    nk = k_total // tk
    dims = (((1,), (0,)), ((), ()))

    def body(a_ref, b_ref, h_ref, g_ref, beta_ref, mix_ref, y_ref, yb_ref, *acc):
        k = pl.program_id(1)

        def finish():
            for c in range(tm // chunk):
                rows = pl.ds(chunk * c, chunk)
                val = lax.dot_general(a_ref[rows, :], b_ref[...], dims, preferred_element_type=F32)
                if nk > 1:
                    val = val + acc[0][rows, :]
                mix_ref[rows, :] = val
                (y,) = _ln_fn(None, h_ref[rows, :], val, g_ref[...], beta_ref[...])
                y_ref[rows, :] = y
                yb_ref[rows, :] = y.astype(BF16)

        if nk == 1:
            finish()
        else:
            part = lambda: lax.dot_general(a_ref[...], b_ref[...], dims, preferred_element_type=F32)

            @pl.when(k == 0)
            def _():
                acc[0][...] = part()

            if nk > 2:
                @pl.when(jnp.logical_and(k > 0, k < nk - 1))
                def _():
                    acc[0][...] += part()

            @pl.when(k == nk - 1)
            def _():
                finish()

    tile = pl.BlockSpec((tm, n), lambda i, k: (i, 0))
    row = pl.BlockSpec((1, n), lambda i, k: (0, 0))
    return pl.pallas_call(
        body, grid=(m_total // tm, nk),
        in_specs=[pl.BlockSpec((tm, tk), lambda i, k: (i, k)), pl.BlockSpec((tk, n), lambda i, k: (k, 0)), tile, row,
                  row],
        out_specs=[tile, tile, tile],
        out_shape=[_sds((m_total, n)), _sds((m_total, n)), _sds((m_total, n), BF16)],
        scratch_shapes=[pltpu.VMEM((tm, n), F32)] if nk > 1 else [], name=name,
        compiler_params=_cparams(("parallel", "arbitrary")))(a, b, h, g, beta)


def _silu(x):
    return x * jax.nn.sigmoid(x)


def _rowa_fn(pids, c, ab, alog, dtb):
    tb = c.shape[0]
    s = _silu(c)
    qs, ks = [], []
    for h in range(DN_HEADS):
        qh = s[:, DN_D * h:DN_D * (h + 1)]
        qs.append(qh * lax.rsqrt(jnp.sum(qh * qh, axis=-1, keepdims=True) + NORM_EPS) * (DN_D ** -0.5))
        kh = s[:, DN_W + DN_D * h:DN_W + DN_D * (h + 1)]
        ks.append(kh * lax.rsqrt(jnp.sum(kh * kh, axis=-1, keepdims=True) + NORM_EPS))
    q = jnp.concatenate(qs, axis=1)
    k = jnp.concatenate(ks, axis=1)
    v = s[:, 2 * DN_W:3 * DN_W]
    g128 = -jnp.exp(alog) * _softplus(ab + dtb)
    b128 = jax.nn.sigmoid(ab)
    r, cc = _iota((tb, tb), 0), _iota((tb, tb), 1)
    tri = jnp.where(((r >> 6) == (cc >> 6)) & (r >= cc), 1.0, 0.0)
    gc128 = mm(tri, g128, "nn", "sela")
    lane, col = _iota((128, DN_W), 0), _iota((128, DN_W), 1)
    exp_a = jnp.where(lane == (col >> 7), 1.0, 0.0)
    exp_b = jnp.where(lane == (col >> 7) + DN_HEADS, 1.0, 0.0)
    return q, k, v, mm(gc128, exp_a, "nn", "selb"), mm(b128, exp_b, "nn", "selb")


def _tri_inv_raw(lows, block):
    n = lows[0].shape[0]
    r, c = _iota((n, n), 0), _iota((n, n), 1)
    lg = 0
    xs = None
    while (1 << lg) < block:
        off = ((r >> (lg + 1)) == (c >> (lg + 1))) & (((r >> lg) & 1) == 1) & (((c >> lg) & 1) == 0)
        cblks = [jnp.where(off, low, 0.0) for low in lows]
        if xs is None:
            xs = [jnp.where(r == c, 1.0, 0.0) - cb for cb in cblks]
        else:
            ys = [mm(cb, x, "nn", "f32") for cb, x in zip(cblks, xs)]
            xs = [x - mm(x, y, "nn", "f32") for x, y in zip(xs, ys)]
        lg += 1
    return tuple(xs)


def _tri_inv_cotangent(block, xs, cts):
    n = xs[0].shape[0]
    r, c = _iota((n, n), 0), _iota((n, n), 1)
    shift = block.bit_length() - 1
    keep = ((r >> shift) == (c >> shift)) & (r > c)
    gs = [mm(x, ct, "tn", "f32") for x, ct in zip(xs, cts)]
    gs = [mm(g, x, "nt", "f32") for g, x in zip(gs, xs)]
    return tuple(jnp.where(keep, -g, 0.0) for g in gs)


@functools.partial(jax.custom_vjp, nondiff_argnums=(1,))
def _tri_inv(lows, block):
    return _tri_inv_raw(lows, block)


def _tri_inv_fwd(lows, block):
    xs = _tri_inv_raw(lows, block)
    return xs, xs


def _tri_inv_bwd(block, xs, cts):
    return (_tri_inv_cotangent(block, xs, cts),)


_tri_inv.defvjp(_tri_inv_fwd, _tri_inv_bwd)


@functools.partial(jax.custom_vjp, nondiff_argnums=(2,))
def _tri_inv_known(lows, known, block):
    return known


def _tri_inv_known_fwd(lows, known, block):
    return known, known


def _tri_inv_known_bwd(block, xs, cts):
    return _tri_inv_cotangent(block, xs, cts), tuple(jnp.zeros_like(x) for x in xs)


_tri_inv_known.defvjp(_tri_inv_known_fwd, _tri_inv_known_bwd)


PAIR = 2 * CHUNK


def _dn1_pairs(q, k, v, gc, beta, tinv_known=None):
    assert PAIR == DN_D
    n = PAIR
    pairs = range(q.shape[0] // n)
    cut = lambda t: [t[n * j:n * (j + 1)] for j in pairs]
    q, k, v, gc, beta = cut(q), cut(k), cut(v), cut(gc), cut(beta)
    onehot = jnp.where(_iota((n, DN_D), 1) == 0, 1.0, 0.0)
    r, c = _iota((n, n), 0), _iota((n, n), 1)
    same = (r >> 6) == (c >> 6)
    incl, strict = same & (r >= c), same & (r > c)
    row = _iota((n, DN_D), 0)
    eg = [jnp.exp(g) for g in gc]
    kb = [k[j] * beta[j] for j in pairs]
    g_row = [mm(onehot, g, "nt", "sela") for g in gc]
    kk = [mm(kb[j], k[j], "nt", "bf16") for j in pairs]
    qk = [mm(q[j], k[j], "nt", "bf16") for j in pairs]
    decay = [jnp.exp(jnp.where(incl, gc[j] - g_row[j], NEG_BIG)) for j in pairs]
    low = tuple(jnp.where(strict, kk[j] * decay[j], 0.0) for j in pairs)
    if tinv_known is None:
        tinv = _tri_inv(low, CHUNK)
    else:
        tinv = _tri_inv_known(low, tuple(cut(tinv_known)), CHUNK)
    uw = [mm(tinv[j], jnp.concatenate([v[j] * beta[j], kb[j] * eg[j]], axis=1), "nn", "f32") for j in pairs]
    intra = [jnp.where(incl, qk[j] * decay[j], 0.0) for j in pairs]
    g_last = []
    for g in gc:
        last0 = jnp.sum(jnp.where(row == CHUNK - 1, g, 0.0), axis=0, keepdims=True)
        last1 = jnp.sum(jnp.where(row == PAIR - 1, g, 0.0), axis=0, keepdims=True)
        g_last.append(jnp.where(row < CHUNK, last0, last1))
    join = lambda parts: jnp.concatenate(parts, axis=0)
    return (join([t[:, :DN_D] for t in uw]), join([t[:, DN_D:] for t in uw]), join(intra),
            join([q[j] * eg[j] for j in pairs]), join([k[j] * jnp.exp(g_last[j] - gc[j]) for j in pairs]),
            join([jnp.exp(g) for g in g_last]), join(list(tinv)))


def _dn1_fn(pids, q, k, v, gc, beta):
    return _dn1_pairs(q, k, v, gc, beta)


def _dn1_fn_known(pids, q, k, v, gc, beta, tinv):
    return _dn1_pairs(q, k, v, gc, beta, tinv)[:6]


def _dn2_step(half, state, qd, kd, u, w, intra, cd_row):
    heads = range(len(state))
    v_new = [u[h] - mm(w[h], state[h], "nn", "bf16") for h in heads]
    zeros = jnp.zeros_like(v_new[0])
    v_pair = [jnp.concatenate([v, zeros] if half == 0 else [zeros, v], axis=0) for v in v_new]
    from_state = [mm(qd[h], state[h], "nn", "bf16") for h in heads]
    out = tuple(from_state[h] + mm(intra[h], v_pair[h], "nn", "bf16") for h in heads)
    return out, tuple(state[h] * cd_row[h] + mm(kd[h], v_new[h], "tn", "bf16") for h in heads)


def _post_fn(pids, o, z, nw):
    outs = []
    for h in range(DN_HEADS):
        oh = o[:, DN_D * h:DN_D * (h + 1)]
        zh = z[:, DN_D * h:DN_D * (h + 1)]
        y = oh * lax.rsqrt(jnp.mean(oh * oh, axis=-1, keepdims=True) + NORM_EPS) * nw
        outs.append(y * _silu(zh))
    return (jnp.concatenate(outs, axis=1),)


def _memattn_fn(pids, qm, kvm):
    kmem, vmem = kvm[:, :MEM_W], kvm[:, MEM_W:]
    lane = _iota((1, MEM_W), 1)
    heads = range(MEM_HEADS)
    hm = [jnp.where((lane >> 6) == h, 1.0, 0.0) for h in heads]
    s = [mm(qm * (hm[h] * MEM_DH ** -0.5), kmem, "nt", "bf16") for h in heads]
    e = [jnp.exp(t - lax.stop_gradient(jnp.max(t, axis=-1, keepdims=True))) for t in s]
    o = [mm(e[h], vmem, "nn", "bf16") * (hm[h] / jnp.sum(e[h], axis=-1, keepdims=True)) for h in heads]
    return ((o[0] + o[1]) + (o[2] + o[3]),)


def _ln_fn(pids, h, mix, g, b):
    x = DN_ALPHA * h + mix
    mu = jnp.mean(x, axis=-1, keepdims=True)
    xc = x - mu
    var = jnp.mean(xc * xc, axis=-1, keepdims=True)
    return (xc * lax.rsqrt(var + LN_EPS) * g + b,)


def _rope_matrix():
    i, j = _iota((128, 128), 0), _iota((128, 128), 1)
    jj = j & 63
    return jnp.where((jj < 32) & (i == j + 32), -1.0, 0.0) + jnp.where((jj >= 32) & (i == j - 32), 1.0, 0.0)


def _rope128(x, cos, sin, rot):
    return x * cos + mm(x, rot, "nn", "selb") * sin


def _krope_fn(pids, kraw, cos, sin):
    rot = _rope_matrix()
    return (jnp.concatenate([_rope128(kraw[:, 128 * g:128 * (g + 1)], cos, sin, rot) for g in range(2)], axis=1),)


def _swa_fn(pids, qraw, cos, sin, k_halo, k_cur, v_halo, v_cur, sinks):
    tb = qraw.shape[0]
    nwin = tb // WINDOW
    rot = _rope_matrix()
    kcat = jnp.concatenate([k_halo, k_cur], axis=0)
    vcat = jnp.concatenate([v_halo, v_cur], axis=0)
    lane = _iota((1, 128), 1)
    halves = (jnp.where(lane < 64, 1.0, 0.0), jnp.where(lane >= 64, 1.0, 0.0))
    group = SWA_HEADS // 2
    rows = group * WINDOW
    in_cur = _iota((rows, WINDOW), 1) <= (_iota((rows, WINDOW), 0) & (WINDOW - 1))
    qg = [_rope128(qraw[:, 128 * p:128 * (p + 1)], cos, sin, rot) for p in range(group)]
    sink = []
    for kv in range(2):
        cols = [jnp.sum(jnp.where(lane == group * kv + i, sinks, 0.0), axis=-1, keepdims=True)
                + jnp.zeros((WINDOW, 1), F32) for i in range(group)]
        sink.append(jnp.concatenate(cols, axis=0))
    units = [(w, kv) for w in range(nwin) for kv in range(2)]
    n_units = range(len(units))
    q6 = [jnp.concatenate([qg[3 * kv + i // 2][WINDOW * w:WINDOW * (w + 1)] * (halves[i % 2] * SWA_DH ** -0.5)
                           for i in range(group)], axis=0) for w, kv in units]
    blk = lambda cat, w, kv: cat[WINDOW * w:WINDOW * (w + 1), 128 * kv:128 * (kv + 1)]
    s_prev = [mm(q6[u], blk(kcat, w, kv), "nt", "bf16") for u, (w, kv) in enumerate(units)]
    s_cur = [mm(q6[u], blk(kcat, w + 1, kv), "nt", "bf16") for u, (w, kv) in enumerate(units)]
    s = [jnp.where(in_cur, s_cur[u], jnp.where(pids[0] * nwin + w > 0, s_prev[u], NEG_BIG))
         for u, (w, kv) in enumerate(units)]
    m = [lax.stop_gradient(jnp.maximum(jnp.max(s[u], axis=-1, keepdims=True), sink[kv]))
         for u, (w, kv) in enumerate(units)]
    e = [jnp.exp(s[u] - m[u]) for u in n_units]
    denom = [jnp.sum(e[u], axis=-1, keepdims=True) + jnp.exp(sink[kv] - m[u]) for u, (w, kv) in enumerate(units)]
    o = [(mm(jnp.where(in_cur, e[u], 0.0), blk(vcat, w + 1, kv), "nn", "bf16")
          + mm(jnp.where(in_cur, 0.0, e[u]), blk(vcat, w, kv), "nn", "bf16")) / denom[u]
         for u, (w, kv) in enumerate(units)]
    out_rows = []
    for w in range(nwin):
        lanes = []
        for p in range(group):
            ou = o[units.index((w, p // 3))]
            i = 2 * (p % 3)
            lanes.append(ou[WINDOW * i:WINDOW * (i + 1)] * halves[0] + ou[WINDOW * (i + 1):WINDOW * (i + 2)] * halves[1])
        out_rows.append(jnp.concatenate(lanes, axis=1))
    return (jnp.concatenate(out_rows, axis=0),)


def _conv_fwd(proj, conv_w, tb):
    t_total = proj.shape[0]
    width = conv_w.shape[1]
    nb = t_total // tb

    def body(cur_ref, prev_ref, w_ref, out_ref):
        i = pl.program_id(0)
        prev = jnp.where(i > 0, prev_ref[...], 0.0)
        xcat = jnp.concatenate([prev, cur_ref[...]], axis=0)
        acc = xcat[8:] * w_ref[3:4, :]
        for j in range(3):
            acc = acc + pltpu.roll(xcat, 3 - j, 0)[8:] * w_ref[j:j + 1, :]
        out_ref[...] = acc

    return pl.pallas_call(
        body, grid=(nb,),
        in_specs=[pl.BlockSpec((tb, width), lambda i: (i, 0)),
                  pl.BlockSpec((8, width), lambda i: (jnp.maximum(i * (tb // 8) - 1, 0), 0)),
                  _whole((4, width))],
        out_specs=pl.BlockSpec((tb, width), lambda i: (i, 0)), out_shape=_sds((t_total, width)),
        name="conv_fwd", compiler_params=_cparams(("parallel",)))(proj, proj, conv_w)


def _conv_bwd(dc, proj, conv_w, tb):
    t_total, width = dc.shape
    nb = t_total // tb

    def body(dcur_ref, dnext_ref, cur_ref, prev_ref, w_ref, dx_ref, dw_ref):
        i = pl.program_id(0)
        dnext = jnp.where(i < nb - 1, dnext_ref[...], 0.0)
        dcur = dcur_ref[...]
        dcat = jnp.concatenate([dcur, dnext], axis=0)
        prev = jnp.where(i > 0, prev_ref[...], 0.0)
        xcat = jnp.concatenate([prev, cur_ref[...]], axis=0)

        @pl.when(i == 0)
        def _():
            dw_ref[...] = jnp.zeros(dw_ref.shape, F32)

        dx = dcur * w_ref[3:4, :]
        dw_ref[3:4, :] += jnp.sum(dcur * xcat[8:], axis=0, keepdims=True)
        for j in range(3):
            dx = dx + pltpu.roll(dcat, 8 - (3 - j), 0)[8:] * w_ref[j:j + 1, :]
            dw_ref[j:j + 1, :] += jnp.sum(dcur * pltpu.roll(xcat, 3 - j, 0)[8:], axis=0, keepdims=True)
        dx_ref[...] = dx.astype(dx_ref.dtype)

    return pl.pallas_call(
        body, grid=(nb,),
        in_specs=[pl.BlockSpec((tb, width), lambda i: (i, 0)),
                  pl.BlockSpec((8, width), lambda i: (jnp.minimum((i + 1) * (tb // 8), t_total // 8 - 1), 0)),
                  pl.BlockSpec((tb, width), lambda i: (i, 0)),
                  pl.BlockSpec((8, width), lambda i: (jnp.maximum(i * (tb // 8) - 1, 0), 0)),
                  _whole((4, width))],
        out_specs=[pl.BlockSpec((tb, width), lambda i: (i, 0)), _whole((4, width))],
        out_shape=[_sds((t_total, width), BF16), _sds((4, width))],
        name="conv_bwd", compiler_params=_cparams(("arbitrary",)))(dc, dc, proj, proj, conv_w)


def _head_spec(tb, nb=None):
    if nb is None:
        return pl.BlockSpec((tb, DN_D), lambda h, i: (i, h))
    return pl.BlockSpec((tb, DN_D), lambda h, i: (nb - 1 - i, h))


def _intra_spec(tb, nb=None):
    if nb is None:
        return pl.BlockSpec((None, tb, PAIR), lambda h, i: (h, i, 0))
    return pl.BlockSpec((None, tb, PAIR), lambda h, i: (h, nb - 1 - i, 0))


def _state_spec(tb, nb=None):
    if nb is None:
        return pl.BlockSpec((None, tb // CHUNK, DN_D, DN_D), lambda h, i: (h, i, 0, 0))
    return pl.BlockSpec((None, tb // CHUNK, DN_D, DN_D), lambda h, i: (h, nb - 1 - i, 0, 0))


def _scan_specs(tb, nb=None):
    blk = (lambda i: i) if nb is None else (lambda i: nb - 1 - i)
    rows = pl.BlockSpec((tb, DN_W), lambda i: (blk(i), 0))
    pair = pl.BlockSpec((DN_HEADS, tb, PAIR), lambda i: (0, blk(i), 0))
    states = pl.BlockSpec((DN_HEADS, tb // CHUNK, DN_D, DN_D), lambda i: (0, blk(i), 0, 0))
    return rows, pair, states


def _dn2_fwd(qd, kd, u, w, intra, cd, tb):
    t_total = qd.shape[0]
    rows, pair, states = _scan_specs(tb)

    def body(qd_ref, kd_ref, u_ref, w_ref, a_ref, cd_ref, o_ref, save_ref, state):
        @pl.when(pl.program_id(0) == 0)
        def _():
            state[...] = jnp.zeros(state.shape, F32)

        heads = range(DN_HEADS)
        lanes = [pl.ds(DN_D * h, DN_D) for h in heads]
        for j in range(tb // CHUNK):
            sl = pl.ds(CHUNK * j, CHUNK)
            s0 = tuple(state[h] for h in heads)
            for h in heads:
                save_ref[h, j] = s0[h]
            per_head = lambda ref: tuple(ref[sl, lanes[h]] for h in heads)
            out, s1 = _dn2_step(j % 2, s0, per_head(qd_ref), per_head(kd_ref), per_head(u_ref), per_head(w_ref),
                                tuple(a_ref[h, sl, :] for h in heads),
                                tuple(cd_ref[pl.ds(CHUNK * j, 1), lanes[h]] for h in heads))
            for h in heads:
                o_ref[sl, lanes[h]] = out[h]
                state[h] = s1[h]

    return pl.pallas_call(
        body, grid=(t_total // tb,), in_specs=[rows, rows, rows, rows, pair, rows],
        out_specs=[rows, states],
        out_shape=[_sds((t_total, DN_W)), _sds((DN_HEADS, t_total // CHUNK, DN_D, DN_D))],
        scratch_shapes=[pltpu.VMEM((DN_HEADS, DN_D, DN_D), F32)], name="dn2_fwd",
        compiler_params=_cparams(("arbitrary",)))(qd, kd, u, w, intra, cd)


def _dn2_bwd(qd, kd, u, w, intra, cd, saved, d_o, tb):
    t_total = qd.shape[0]
    nb = t_total // tb
    rows, pair, states = _scan_specs(tb, nb)

    def body(qd_ref, kd_ref, u_ref, w_ref, a_ref, cd_ref, save_ref, do_ref,
             dqd_ref, dkd_ref, du_ref, dw_ref, da_ref, dcd_ref, dstate):
        @pl.when(pl.program_id(0) == 0)
        def _():
            dstate[...] = jnp.zeros(dstate.shape, F32)

        first_row = _iota((CHUNK, DN_D), 0) == 0
        heads = range(DN_HEADS)
        lanes = [pl.ds(DN_D * h, DN_D) for h in heads]
        for j in reversed(range(tb // CHUNK)):
            sl = pl.ds(CHUNK * j, CHUNK)
            per_head = lambda ref: tuple(ref[sl, lanes[h]] for h in heads)
            _, vjp = jax.vjp(functools.partial(_dn2_step, j % 2), tuple(save_ref[h, j] for h in heads),
                             per_head(qd_ref), per_head(kd_ref), per_head(u_ref), per_head(w_ref),
                             tuple(a_ref[h, sl, :] for h in heads),
                             tuple(cd_ref[pl.ds(CHUNK * j, 1), lanes[h]] for h in heads))
            ds0, dqd, dkd, du, dw, da, dcd = vjp((per_head(do_ref), tuple(dstate[h] for h in heads)))
            for h in heads:
                dqd_ref[sl, lanes[h]] = dqd[h]
                dkd_ref[sl, lanes[h]] = dkd[h]
                du_ref[sl, lanes[h]] = du[h]
                dw_ref[sl, lanes[h]] = dw[h]
                da_ref[h, sl, :] = da[h]
                dcd_ref[sl, lanes[h]] = jnp.where(first_row, dcd[h], 0.0)
                dstate[h] = ds0[h]

    full = _sds((t_total, DN_W))
    return pl.pallas_call(
        body, grid=(nb,),
        in_specs=[rows, rows, rows, rows, pair, rows, states, rows],
        out_specs=[rows, rows, rows, rows, pair, rows],
        out_shape=[full, full, full, full, _sds((DN_HEADS, t_total, PAIR)), full],
        scratch_shapes=[pltpu.VMEM((DN_HEADS, DN_D, DN_D), F32)], name="dn2_bwd",
        compiler_params=_cparams(("arbitrary",)))(qd, kd, u, w, intra, cd, saved, d_o)


def _loss_and_grad(y, target, tb):
    t_total, d = y.shape

    def body(y_ref, t_ref, dy_ref, acc_ref):
        @pl.when(pl.program_id(0) == 0)
        def _():
            acc_ref[...] = jnp.zeros(acc_ref.shape, F32)

        err = y_ref[...] - t_ref[...]
        dy_ref[...] = err * (1.0 / d)
        acc_ref[...] += jnp.sum(err * err, axis=0, keepdims=True)

    dy, acc = pl.pallas_call(
        body, grid=(t_total // tb,), in_specs=[_rows(tb, d), _rows(tb, d)],
        out_specs=[_rows(tb, d), _whole((1, d))], out_shape=[_sds((t_total, d)), _sds((1, d))],
        name="loss", compiler_params=_cparams(("arbitrary",)))(y, target)
    return 0.5 * jnp.sum(acc) / d, dy


def _halo_sum(mains, halos, tb):
    t_total, width = mains[0].shape
    nb = t_total // tb
    n = len(mains)

    def body(*refs):
        out_ref = refs[-1]
        i = pl.program_id(0)
        tot = refs[0][...]
        for r in refs[1:n]:
            tot = tot + r[...]
        hal = refs[n][...]
        for r in refs[n + 1:2 * n]:
            hal = hal + r[...]
        hal = jnp.where(i < nb - 1, hal, 0.0)
        out_ref[...] = tot + jnp.concatenate([jnp.zeros((tb - WINDOW, width), F32), hal], axis=0)

    return pl.pallas_call(
        body, grid=(nb,),
        in_specs=[_rows(tb, width)] * n
        + [pl.BlockSpec((None, WINDOW, width), lambda i: (jnp.minimum(i + 1, nb - 1), 0, 0))] * n,
        out_specs=_rows(tb, width), out_shape=_sds((t_total, width)), name="halo_sum",
        compiler_params=_cparams(("parallel",)))(*mains, *halos)


def _adamw(recvs, w, m, v, tr, name):
    slots, _, c_total = recvs[0].shape
    r_total = w.shape[0]
    assert sum(r.shape[1] for r in recvs) == r_total
    tr = min([tr] + [r.shape[1] for r in recvs])
    assert all(r.shape[1] % tr == 0 for r in recvs)
    starts = [sum(r.shape[1] for r in recvs[:i]) // tr for i in range(len(recvs))]
    counts = [r.shape[1] // tr for r in recvs]
    c1 = 1.0 / (1.0 - ADAM_B1 ** ADAM_STEP)
    c2 = 1.0 / (1.0 - ADAM_B2 ** ADAM_STEP)

    def body(*refs):
        recv_refs = refs[:len(recvs)]
        w_ref, m_ref, v_ref, g_ref, d_ref, nm_ref, nv_ref = refs[len(recvs):]
        g = None
        for recv_ref, start in zip(recv_refs, starts):
            part = recv_ref[0].astype(F32)
            for s in range(1, slots):
                part = part + recv_ref[s].astype(F32)
            g = part if g is None else jnp.where(pl.program_id(0) >= start, part, g)
        nm = ADAM_B1 * m_ref[...] + (1.0 - ADAM_B1) * g
        nv = ADAM_B2 * v_ref[...] + (1.0 - ADAM_B2) * (g * g)
        g_ref[...] = g
        nm_ref[...] = nm
        nv_ref[...] = nv
        d_ref[...] = -ADAM_LR * ((nm * c1) / (jnp.sqrt(nv * c2) + ADAM_EPS) + ADAM_WD * w_ref[...])

    blk = pl.BlockSpec((tr, c_total), lambda i: (i, 0))
    recv_specs = [pl.BlockSpec((slots, tr, c_total), lambda i, s=s, n=n: (0, jnp.clip(i - s, 0, n - 1), 0))
                  for s, n in zip(starts, counts)]
    return pl.pallas_call(
        body, grid=(r_total // tr,), in_specs=recv_specs + [blk, blk, blk],
        out_specs=[blk] * 4, out_shape=[_sds((r_total, c_total))] * 4, name=name,
        compiler_params=_cparams(("parallel",)))(*recvs, w, m, v)


def _me_and_peers():
    x, y, c = lax.axis_index("x"), lax.axis_index("y"), lax.axis_index("c")
    me = 4 * x + 2 * y + c
    peers = []
    for k in range(1, N_DEV):
        px = 1 - x if (k >> 2) & 1 else x
        py = 1 - y if (k >> 1) & 1 else y
        pc = 1 - c if k & 1 else c
        peers.append(((px, py, pc), 4 * px + 2 * py + pc))
    return me, peers


def _small_exchange(packed, reduce):
    r_total = packed.shape[0]

    def body(p_ref, out_ref, gath_ref, send_sems, recv_sems):
        me, peers = _me_and_peers()
        gath_ref[me] = p_ref[...]
        copies = []
        for k, (dev, _) in enumerate(peers):
            cp = pltpu.make_async_remote_copy(src_ref=p_ref, dst_ref=gath_ref.at[me], send_sem=send_sems.at[k],
                                              recv_sem=recv_sems.at[k], device_id=dev,
                                              device_id_type=pl.DeviceIdType.MESH)
            cp.start()
            copies.append(cp)
        for k, (dev, idx) in enumerate(peers):
            pltpu.make_async_remote_copy(src_ref=p_ref, dst_ref=gath_ref.at[idx], send_sem=send_sems.at[k],
                                         recv_sem=recv_sems.at[k], device_id=dev,
                                         device_id_type=pl.DeviceIdType.MESH).wait_recv()
        for cp in copies:
            cp.wait_send()
        if reduce:
            tot = gath_ref[0]
            for d in range(1, N_DEV):
                tot = tot + gath_ref[d]
            out_ref[...] = tot
        else:
            out_ref[...] = gath_ref[...]

    out_shape = _sds((r_total, 128)) if reduce else _sds((N_DEV, r_total, 128))
    return pl.pallas_call(
        body, in_specs=[pl.BlockSpec(memory_space=pltpu.VMEM)], out_specs=pl.BlockSpec(memory_space=pltpu.VMEM),
        out_shape=out_shape,
        scratch_shapes=[pltpu.VMEM((N_DEV, r_total, 128), F32), pltpu.SemaphoreType.DMA((N_DEV - 1,)),
                        pltpu.SemaphoreType.DMA((N_DEV - 1,))],
        name="small_allreduce" if reduce else "small_allgather")(packed)


def _slot(ref, axis, idx, size):
    sel = [slice(None)] * len(ref.shape)
    sel[axis] = idx if size is None else pl.ds(pl.multiple_of(idx * size, size), size)
    return ref.at[tuple(sel)]


def _big_exchange(srcs, dst_shapes, src_view, dst_view, name):
    n = len(srcs)

    def body(*refs):
        src_refs, dst_refs = refs[:n], refs[n:2 * n]
        send_sems, recv_sems, local_sems = refs[2 * n:]
        me, peers = _me_and_peers()
        local, remote = [], []
        for t in range(n):
            loc = pltpu.make_async_copy(src_view(t, src_refs[t], me), dst_view(t, dst_refs[t], me), local_sems.at[t])
            loc.start()
            local.append(loc)
            for k, (dev, idx) in enumerate(peers):
                cp = pltpu.make_async_remote_copy(
                    src_ref=src_view(t, src_refs[t], idx), dst_ref=dst_view(t, dst_refs[t], me),
                    send_sem=send_sems.at[t, k], recv_sem=recv_sems.at[t, k], device_id=dev,
                    device_id_type=pl.DeviceIdType.MESH)
                cp.start()
                remote.append(cp)
        for t in range(n):
            for k, (dev, idx) in enumerate(peers):
                pltpu.make_async_remote_copy(
                    src_ref=src_view(t, src_refs[t], me), dst_ref=dst_view(t, dst_refs[t], idx),
                    send_sem=send_sems.at[t, k], recv_sem=recv_sems.at[t, k], device_id=dev,
                    device_id_type=pl.DeviceIdType.MESH).wait_recv()
        for cp in remote:
            cp.wait_send()
        for cp in local:
            cp.wait()

    any_spec = pl.BlockSpec(memory_space=pl.ANY)
    return pl.pallas_call(
        body, in_specs=[any_spec] * n, out_specs=[any_spec] * n, out_shape=dst_shapes,
        scratch_shapes=[pltpu.SemaphoreType.DMA((n, N_DEV - 1)), pltpu.SemaphoreType.DMA((n, N_DEV - 1)),
                        pltpu.SemaphoreType.DMA((n,))],
        name=name)(*srcs)


BIG = {
    "a_w_in": (1, (2, 1024, A_IN)),
    "b_w_in": (1, (2, 1024, 1024)),
    "w_kv_shared": (0, (1024, 256)),
    "mem_w_kv": (1, (4, 1024, 512)),
    "w_o": (1, (4, 1024, 1024)),
    "mlp_w_up": (2, (4, 1024, 4096)),
    "mlp_w_down": (1, (4, 4096, 1024)),
}
BIG_NAMES = tuple(BIG)


def _gather_plan(names, shards):
    dst_shapes, axes, sizes = [], [], []
    for name, s in zip(names, shards):
        axis = BIG[name][0] - (len(BIG[name][1]) - s.ndim)
        if name == "mlp_w_up":
            dst_shapes.append(_sds((N_DEV,) + s.shape, s.dtype))
            axes.append(0)
            sizes.append(None)
        else:
            dst_shapes.append(_sds(tuple(d * N_DEV if a == axis else d for a, d in enumerate(s.shape)), s.dtype))
            axes.append(axis)
            sizes.append(s.shape[axis])
    return dst_shapes, axes, sizes


def _gather_ride(names, shards):
    dst_shapes, axes, sizes = _gather_plan(names, shards)
    return _Ride(shards, dst_shapes, lambda t, ref, idx: ref, lambda t, ref, idx: _slot(ref, axes[t], idx, sizes[t]))


def _scatter_plan(names, grads):
    dst_shapes, axes, sizes = [], [], []
    for name, g in zip(names, grads):
        if name == "mlp_w_up":
            shard = (g.shape[0],) + g.shape[2:]
            axes.append(1)
            sizes.append(None)
        else:
            axis = BIG[name][0] - (len(BIG[name][1]) - g.ndim)
            shard = tuple(d // N_DEV if a == axis else d for a, d in enumerate(g.shape))
            axes.append(axis)
            sizes.append(shard[axis])
        dst_shapes.append(_sds((N_DEV,) + shard, g.dtype))
    return dst_shapes, axes, sizes


def _scatter_ride(names, grads):
    dst_shapes, axes, sizes = _scatter_plan(names, grads)
    return _Ride(grads, dst_shapes, lambda t, ref, idx: _slot(ref, axes[t], idx, sizes[t]),
                 lambda t, ref, idx: ref.at[idx])


def _allgather_weights(names, shards):
    dst_shapes, axes, sizes = _gather_plan(names, shards)
    n = len(shards)

    def body(*refs):
        src_refs, dst_refs = refs[:n], refs[n:2 * n]
        send_sems, recv_sems, local_sems = refs[2 * n:]
        x, y, c = lax.axis_index("x"), lax.axis_index("y"), lax.axis_index("c")
        sibling = (x, y, 1 - c)
        chips = [(1 - x, y), (x, 1 - y), (1 - x, 1 - y)]
        index = lambda px, py, pc: 4 * px + 2 * py + pc

        def copy(t, k, block, to, src=None):
            rows = _slot(dst_refs[t], axes[t], index(*block), sizes[t])
            return pltpu.make_async_remote_copy(
                src_ref=rows if src is None else src, dst_ref=rows, send_sem=send_sems.at[t, k],
                recv_sem=recv_sems.at[t, k], device_id=to, device_id_type=pl.DeviceIdType.MESH)

        started, local = [], []
        for t in range(n):
            mine = pltpu.make_async_copy(src_refs[t], _slot(dst_refs[t], axes[t], index(x, y, c), sizes[t]),
                                         local_sems.at[t])
            mine.start()
            local.append(mine)
            first = [copy(t, 0, (x, y, c), sibling, src=src_refs[t])]
            first += [copy(t, 1 + j, (x, y, c), (*chip, c), src=src_refs[t]) for j, chip in enumerate(chips)]
            for cp in first:
                cp.start()
            started += first
        for t in range(n):
            for j, chip in enumerate(chips):
                copy(t, 1 + j, (*chip, c), (x, y, c)).wait_recv()
                passed = copy(t, 4 + j, (*chip, c), sibling)
                passed.start()
                started.append(passed)
        for t in range(n):
            copy(t, 0, sibling, (x, y, c)).wait_recv()
            for j, chip in enumerate(chips):
                copy(t, 4 + j, (*chip, 1 - c), (x, y, c)).wait_recv()
        for cp in started:
            cp.wait_send()
        for cp in local:
            cp.wait()

    any_spec = pl.BlockSpec(memory_space=pl.ANY)
    return pl.pallas_call(
        body, in_specs=[any_spec] * n, out_specs=[any_spec] * n, out_shape=dst_shapes,
        scratch_shapes=[pltpu.SemaphoreType.DMA((n, N_DEV - 1)), pltpu.SemaphoreType.DMA((n, N_DEV - 1)),
                        pltpu.SemaphoreType.DMA((n,))],
        name="allgather_weights")(*shards)


def _scatter_grads(names, grads):
    dst_shapes, axes, sizes = _scatter_plan(names, grads)

    def src_view(t, ref, idx):
        return _slot(ref, axes[t], idx, sizes[t])

    def dst_view(t, ref, idx):
        return ref.at[idx]

    return _big_exchange(grads, dst_shapes, src_view, dst_view, "scatter_grads")


def _pad_row(vec, width=128):
    return jnp.pad(vec.astype(F32), (0, width - vec.shape[0])).reshape(1, width)


def _block_sizes(t_total):
    return dict(row=min(256, t_total), dn=min(512, t_total), swa=min(256, t_total), scan=min(256, t_total))


def _ln_apply(h, mix, g, b, tb):
    t_total, d = h.shape
    fwd = lambda pids, *a: _ln_fn(pids, *a) * 2
    return _block_fwd(fwd, [h, mix, g, b], [_rows(tb, d), _rows(tb, d), _whole((1, d)), _whole((1, d))],
                      [_sds((t_total, d)), _sds((t_total, d), BF16)], [_rows(tb, d), _rows(tb, d)],
                      (t_total // tb,), "ln_fwd")


def _ln_grad(h, mix, g, b, dy, tb):
    t_total, d = h.shape
    return _block_bwd(_ln_fn, [h, mix, g, b], [_rows(tb, d), _rows(tb, d), _whole((1, d)), _whole((1, d))],
                      [dy], [_rows(tb, d)], ["s", "s", "a", "a"],
                      [_sds((t_total, d)), _sds((t_total, d), BF16), _sds((1, d)), _sds((1, d))],
                      [_rows(tb, d), _rows(tb, d), _whole((1, d)), _whole((1, d))], (t_total // tb,), "ln_bwd")


def _memattn_specs(tb, qcol):
    return [pl.BlockSpec((tb, MEM_W), lambda i: (i, qcol)), _whole((MEM_W, 2 * MEM_W))]


def _act_epilogue(acc):
    r = jnp.maximum(acc, 0.0)
    return (r * r,)


def _dact_epilogue(acc, act):
    return (acc * (2.0 * jnp.sqrt(act.astype(F32))),)


def _add_epilogue(acc, other):
    return (acc + other,)


def _key(name, layer):
    if name == "w_kv_shared":
        return (name, None)
    return (name, layer - N_A if name == "b_w_in" else layer)


_PER_LAYER = ("mem_w_kv", "w_o", "mlp_w_up", "mlp_w_down")
GATHER_FIRST = [_key("a_w_in", 0)]
GATHER_ON_ROWA0 = [_key(n, 0) for n in _PER_LAYER]
GATHER_ON_DN1_0 = [_key("a_w_in", 1), _key("w_kv_shared", 1)] + [_key(n, 1) for n in _PER_LAYER]
GATHER_ON_DN1_1 = [_key("b_w_in", 2), _key("b_w_in", 3)] + [_key(n, l) for l in (2, 3) for n in _PER_LAYER]
SCATTER_ON_DN1_BWD_1 = GATHER_ON_DN1_1
SCATTER_ON_DN1_BWD_0 = GATHER_ON_DN1_0 + GATHER_ON_ROWA0
SCATTER_LAST = GATHER_FIRST
ALL_KEYS = GATHER_FIRST + GATHER_ON_ROWA0 + GATHER_ON_DN1_0 + GATHER_ON_DN1_1


def _local_step(x, mem, positions, target, ready, shards, small):
    t_total = x.shape[0]
    bs = _block_sizes(t_total)
    tb, tdn, tsw = bs["row"], bs["dn"], bs["swa"]
    nb = t_total // tb
    nbs = t_total // tsw

    inv_freq = ROPE_THETA ** (-jnp.arange(0, SWA_DH, 2, dtype=F32) / SWA_DH)
    ang = positions.astype(F32)[:, None] * inv_freq
    cos = jnp.tile(jnp.cos(ang), (1, 4))
    sin = jnp.tile(jnp.sin(ang), (1, 4))

    mem_b = mem.astype(BF16)
    ready = dict(ready)
    derived = {}

    def gather_ride(keys):
        names = [k[0] for k in keys]
        return names, (_gather_ride(names, [shards[k] for k in keys]) if shards is not None else None)

    def weight(name, l):
        key = _key(name, l)
        if key not in derived:
            w = ready[key]
            if name == "a_w_in":
                w = jnp.concatenate([w[0][:, :3072], w[0][:, 3084:], w[0][:, 3072:3084],
                                     jnp.zeros((D_MODEL, A_IN_PAD - A_IN), BF16)], axis=1)
            elif name == "w_kv_shared":
                w = jnp.concatenate([w[:, 64 * (i // 2):64 * (i // 2 + 1)] for i in range(8)], axis=1)
            elif name == "mlp_w_up":
                w = jnp.moveaxis(w[:, 0], 0, 1).reshape(D_MODEL, 4 * D_MODEL)
            else:
                w = w[0]
            derived[key] = w
        return derived[key]

    saved = []
    h, hb = x, x.astype(BF16)
    kr = vd_src = None
    for l in range(DEPTH):
        sv = dict(h=h, hb=hb)
        if l < N_A:
            proj = _matmul(hb, weight("a_w_in", l), "nn", [F32], "mm_proj_a", tn=1152)
            conv_w = small["a_conv_w"][l]
            c = _conv_fwd(proj, conv_w, tb)
            alog, dtb = _pad_row(small["a_A_log"][l]), _pad_row(small["a_dt_bias"][l])
            rowa_in = [c, proj, alog, dtb]
            rowa_specs = [_rows(tb, 3 * DN_W), _rows(tb, 128, 26), _whole((1, 128)), _whole((1, 128))]
            rowa_args = (_rowa_fn, rowa_in, rowa_specs, [_sds((t_total, DN_W))] * 5, [_rows(tb, DN_W)] * 5, (nb,))
            if shards is not None and l == 0:
                names, ride = gather_ride(GATHER_ON_ROWA0)
                (q, k, v, gcb, betab), got = _block_fwd(*rowa_args, "rowa_fwd_gather", ride=ride)
                ready.update(zip(GATHER_ON_ROWA0, got))
            else:
                q, k, v, gcb, betab = _block_fwd(*rowa_args, "rowa_fwd")
            hs = _head_spec(tdn)
            dn_grid = (DN_HEADS, t_total // tdn)
            full = _sds((t_total, DN_W))
            full_b = _sds((t_total, DN_W), BF16)
            dn1_out_shapes = [full, full_b, _sds((DN_HEADS, t_total, PAIR), BF16), full_b, full_b, full,
                              _sds((DN_HEADS, t_total, PAIR))]
            dn1_out_specs = [hs, hs, _intra_spec(tdn), hs, hs, hs, _intra_spec(tdn)]
            if shards is not None:
                keys = GATHER_ON_DN1_0 if l == 0 else GATHER_ON_DN1_1
                names, ride = gather_ride(keys)
                (u, w, intra, qd, kd, cd, tinv), got = _block_fwd(
                    _dn1_fn, [q, k, v, gcb, betab], [hs] * 5, dn1_out_shapes, dn1_out_specs, dn_grid,
                    "dn1_fwd_gather%d" % l, ride=ride)
                ready.update(zip(keys, got))
            else:
                u, w, intra, qd, kd, cd, tinv = _block_fwd(_dn1_fn, [q, k, v, gcb, betab], [hs] * 5, dn1_out_shapes,
                                                           dn1_out_specs, dn_grid, "dn1_fwd")
            o, states = _dn2_fwd(qd, kd, u, w, intra, cd, bs["scan"])
            nw = small["a_norm_w"][l].reshape(1, DN_D)
            post_in = [o, proj, nw]
            post_specs = [_rows(tb, DN_W), _rows(tb, DN_W, 3), _whole((1, DN_D))]
            (og,) = _block_fwd(_post_fn, post_in, post_specs, [_sds((t_total, DN_W), BF16)], [_rows(tb, DN_W)],
                               (nb,), "post_fwd")
            qm_col = 12
            sv.update(proj=proj, c=c, rowa_in=rowa_in, rowa_specs=rowa_specs, dn1_in=[q, k, v, gcb, betab, tinv],
                      dn2_in=[qd, kd, u, w, intra, cd], states=states, post_in=post_in, post_specs=post_specs,
                      conv_w=conv_w)
        else:
            jb = l - N_A
            proj = _matmul(hb, weight("b_w_in", l), "nn", [F32], "mm_proj_b")
            sinks = _pad_row(small["b_sinks"][jb])
            swa_in = [proj, cos, sin, kr, kr, vd_src, vd_src, sinks]
            swa_specs = [_rows(tsw, DN_W), _rows(tsw, 128), _rows(tsw, 128),
                         pl.BlockSpec((WINDOW, 256), lambda i: (jnp.maximum(i * (tsw // WINDOW) - 1, 0), 0)),
                         _rows(tsw, 256),
                         pl.BlockSpec((WINDOW, 256), lambda i: (jnp.maximum(i * (tsw // WINDOW) - 1, 0), 1)),
                         _rows(tsw, 256, 1), _whole((1, 128))]
            (og,) = _block_fwd(_swa_fn, swa_in, swa_specs, [_sds((t_total, DN_W), BF16)], [_rows(tsw, DN_W)],
                               (nbs,), "swa_fwd")
            qm_col = 3
            sv.update(proj=proj, swa_in=swa_in, swa_specs=swa_specs)
        kvm = _matmul(mem_b, weight("mem_w_kv", l), "nn", [F32], "mm_memkv", tm=256)
        mem_in = [proj, kvm]
        (mo,) = _block_fwd(_memattn_fn, mem_in, _memattn_specs(tb, qm_col), [_sds((t_total, MEM_W), BF16)],
                           [_rows(tb, MEM_W)], (nb,), "memattn_fwd")
        mixin = jnp.concatenate([og, mo], axis=1)
        g0, b0 = small["ln_g"][l, 0].reshape(1, -1), small["ln_b"][l, 0].reshape(1, -1)
        mix, h1, h1b = _matmul_ln(mixin, weight("w_o", l), h, g0, b0, "mm_wo_ln")
        act = _matmul(h1b, weight("mlp_w_up", l), "nn", [BF16], "mm_up", epi=_act_epilogue, tm=2048)
        g1, b1 = small["ln_g"][l, 1].reshape(1, -1), small["ln_b"][l, 1].reshape(1, -1)
        mlp, h2, h2b = _matmul_ln(act, weight("mlp_w_down", l), h1, g1, b1, "mm_down_ln")
        sv.update(kvm=kvm, mem_in=mem_in, qm_col=qm_col, mixin=mixin, mix=mix, ln0=(g0, b0), h1=h1, h1b=h1b,
                  act=act, mlp=mlp, ln1=(g1, b1))
        saved.append(sv)
        h, hb = h2, h2b
        if l == N_A - 1:
            kvd = _matmul(hb, weight("w_kv_shared", l), "nn", [F32], "mm_kvd")
            krope_in = [kvd, cos, sin]
            krope_specs = [_rows(tb, 256), _rows(tb, 128), _rows(tb, 128)]
            (kr,) = _block_fwd(_krope_fn, krope_in, krope_specs, [_sds((t_total, 256))], [_rows(tb, 256)], (nb,),
                               "krope_fwd")
            vd_src = kvd

    loss, dh = _loss_and_grad(h, target, tb)

    grads = {}

    def scatter_ride(keys):
        return _scatter_ride([k[0] for k in keys], [grads[k] for k in keys]) if shards is not None else None

    sg = dict(a_conv_w=[None] * N_A, a_A_log=[None] * N_A, a_dt_bias=[None] * N_A, a_norm_w=[None] * N_A,
              b_sinks=[None] * (DEPTH - N_A), ln_g=[[None, None] for _ in range(DEPTH)],
              ln_b=[[None, None] for _ in range(DEPTH)])
    dk_parts, dv_parts = [], []
    for l in reversed(range(DEPTH)):
        sv = saved[l]
        if l == N_A - 1:
            dkr = _halo_sum([p[0] for p in dk_parts], [p[1] for p in dk_parts], tsw)
            dvv = _halo_sum([p[0] for p in dv_parts], [p[1] for p in dv_parts], tsw)
            (dkraw,) = _block_bwd(_krope_fn, krope_in, krope_specs, [dkr], [_rows(tb, 256)], ["s", None, None],
                                  [_sds((t_total, 256), BF16)], [_rows(tb, 256)], (nb,), "krope_bwd")
            dkvd = jnp.concatenate([dkraw, dvv.astype(BF16)], axis=1)
            g_kvd = _matmul(saved[l + 1]["hb"], dkvd, "tn", [F32], "mm_dw_kvd", tm=1024, tn=512)
            dh = _matmul(dkvd, weight("w_kv_shared", l), "nt", [F32], "mm_dx_kvd", epi=_add_epilogue, extras=[dh],
                         tn=1024, tk=512)
            grads[_key("w_kv_shared", l)] = jnp.concatenate(
                [g_kvd[:, 128 * i:128 * i + 64] + g_kvd[:, 128 * i + 64:128 * (i + 1)] for i in range(4)],
                axis=1).astype(BF16)
        g1, b1 = sv["ln1"]
        dh1a, dmlp, dg1, db1 = _ln_grad(sv["h1"], sv["mlp"], g1, b1, dh, tb)
        dup = _matmul(dmlp, weight("mlp_w_down", l), "nt", [BF16], "mm_dact", epi=_dact_epilogue, extras=[sv["act"]])
        grads[_key("mlp_w_down", l)] = _matmul(sv["act"], dmlp, "tn", [BF16], "mm_dw_down", tk=4096)[None]
        g_up = _matmul(sv["h1b"], dup, "tn", [BF16], "mm_dw_up", tn=512, tk=4096,
                       out_shapes=[_sds((N_DEV, D_MODEL, 512), BF16)],
                       out_specs=[pl.BlockSpec((None, 1024, 512), lambda i, j, k: (j, i, 0))])
        grads[_key("mlp_w_up", l)] = g_up[None]
        dh1 = _matmul(dup, weight("mlp_w_up", l), "nt", [F32], "mm_dx_up", epi=_add_epilogue, extras=[dh1a], tk=2048)
        g0, b0 = sv["ln0"]
        dha, dmix, dg0, db0 = _ln_grad(sv["h"], sv["mix"], g0, b0, dh1, tb)
        sg["ln_g"][l] = [dg0, dg1]
        sg["ln_b"][l] = [db0, db1]
        grads[_key("w_o", l)] = _matmul(sv["mixin"], dmix, "tn", [BF16], "mm_dw_o", tk=4096)[None]
        dmixin = _matmul(dmix, weight("w_o", l), "nt", [F32], "mm_dx_o", tn=1024)
        dqm, dkvm = _block_bwd(_memattn_fn, sv["mem_in"], _memattn_specs(tb, sv["qm_col"]), [dmixin],
                               [_rows(tb, MEM_W, 3)], ["s", "a"],
                               [_sds((t_total, MEM_W), BF16), _sds((MEM_W, 2 * MEM_W))],
                               [_rows(tb, MEM_W), _whole((MEM_W, 2 * MEM_W))], (nb,), "memattn_bwd")
        grads[_key("mem_w_kv", l)] = _matmul(mem_b, dkvm.astype(BF16), "tn", [BF16], "mm_dw_memkv", tm=1024,
                                             tn=512)[None]
        if l < N_A:
            d_o, dz, dnw = _block_bwd(_post_fn, sv["post_in"], sv["post_specs"], [dmixin], [_rows(tb, DN_W)],
                                      ["s", "s", "a"],
                                      [_sds((t_total, DN_W)), _sds((t_total, DN_W), BF16), _sds((1, DN_D))],
                                      [_rows(tb, DN_W), _rows(tb, DN_W), _whole((1, DN_D))], (nb,), "post_bwd")
            sg["a_norm_w"][l] = dnw
            dqd, dkd, du, dw, da, dcd = _dn2_bwd(*sv["dn2_in"], sv["states"], d_o, bs["scan"])
            hs = _head_spec(tdn)
            full = _sds((t_total, DN_W))
            dn1_bwd_args = (_dn1_fn_known, sv["dn1_in"], [hs] * 5 + [_intra_spec(tdn)], [du, dw, da, dqd, dkd, dcd],
                            [hs, hs, _intra_spec(tdn), hs, hs, hs], ["s"] * 5 + [None], [full] * 5, [hs] * 5,
                            (DN_HEADS, t_total // tdn))
            if shards is not None:
                keys = SCATTER_ON_DN1_BWD_1 if l == N_A - 1 else SCATTER_ON_DN1_BWD_0
                (dq, dk, dv, dgc, dbeta), got = _block_bwd(*dn1_bwd_args, "dn1_bwd_scatter%d" % l,
                                                           ride=scatter_ride(keys))
                grads.update(zip(keys, got))
            else:
                dq, dk, dv, dgc, dbeta = _block_bwd(*dn1_bwd_args, "dn1_bwd")
            dc, dab, dalog, ddtb = _block_bwd(
                _rowa_fn, sv["rowa_in"], sv["rowa_specs"], [dq, dk, dv, dgc, dbeta], [_rows(tb, DN_W)] * 5,
                ["s", "s", "a", "a"],
                [_sds((t_total, 3 * DN_W)), _sds((t_total, 128), BF16), _sds((1, 128)), _sds((1, 128))],
                [_rows(tb, 3 * DN_W), _rows(tb, 128), _whole((1, 128)), _whole((1, 128))], (nb,), "rowa_bwd")
            sg["a_A_log"][l] = dalog[0, :DN_HEADS]
            sg["a_dt_bias"][l] = ddtb[0, :DN_HEADS]
            dx, dconv = _conv_bwd(dc, sv["proj"], sv["conv_w"], tb)
            sg["a_conv_w"][l] = dconv
            dproj = jnp.concatenate([dx, dz, dqm, dab], axis=1)
            g_in = _matmul(sv["hb"], dproj, "tn", [BF16], "mm_dw_a", tm=1024, tn=1152, tk=2048)
            grads[_key("a_w_in", l)] = jnp.concatenate([g_in[:, :3072], g_in[:, 3328:3340], g_in[:, 3072:3328]],
                                                       axis=1)[None]
            dh = _matmul(dproj, weight("a_w_in", l), "nt", [F32], "mm_dx_a", epi=_add_epilogue, extras=[dha], tk=1152)
        else:
            jb = l - N_A
            swa_kinds = ["s", None, None, "s", "s", "s", "s", "a"]
            halo_spec = pl.BlockSpec((None, WINDOW, 256), lambda i: (i, 0, 0))
            dq, dkh, dkc, dvh, dvc, dsink = _block_bwd(
                _swa_fn, sv["swa_in"], sv["swa_specs"], [dmixin], [_rows(tsw, DN_W)], swa_kinds,
                [_sds((t_total, DN_W), BF16), _sds((nbs, WINDOW, 256)), _sds((t_total, 256)),
                 _sds((nbs, WINDOW, 256)), _sds((t_total, 256)), _sds((1, 128))],
                [_rows(tsw, DN_W), halo_spec, _rows(tsw, 256), halo_spec, _rows(tsw, 256), _whole((1, 128))],
                (nbs,), "swa_bwd")
            sg["b_sinks"][jb] = dsink[0, :SWA_HEADS]
            dk_parts.append((dkc, dkh))
            dv_parts.append((dvc, dvh))
            dproj = jnp.concatenate([dq, dqm], axis=1)
            grads[_key("b_w_in", l)] = _matmul(sv["hb"], dproj, "tn", [BF16], "mm_dw_b", tk=4096)[None]
            dh = _matmul(dproj, weight("b_w_in", l), "nt", [F32], "mm_dx_b", epi=_add_epilogue, extras=[dha], tn=1024)

    small_grads = dict(
        a_conv_w=jnp.stack(sg["a_conv_w"]), a_A_log=jnp.stack(sg["a_A_log"]), a_dt_bias=jnp.stack(sg["a_dt_bias"]),
        a_norm_w=jnp.concatenate(sg["a_norm_w"], axis=0), b_sinks=jnp.stack(sg["b_sinks"]),
        ln_g=jnp.stack([jnp.concatenate(p, axis=0) for p in sg["ln_g"]]),
        ln_b=jnp.stack([jnp.concatenate(p, axis=0) for p in sg["ln_b"]]))
    return loss, dh, grads, small_grads


def _pack(arrays, rows):
    flat = []
    for a in arrays:
        v = a.astype(F32).reshape(-1)
        flat.append(jnp.pad(v, (0, (-v.shape[0]) % 128)))
    flat = jnp.concatenate(flat)
    return jnp.pad(flat, (0, rows * 128 - flat.shape[0])).reshape(rows, 128)


def _unpack(slab, shapes):
    flat = slab.reshape(slab.shape[:-2] + (-1,))
    out, off = [], 0
    for s in shapes:
        n = math.prod(s)
        out.append(flat[..., off:off + n].reshape(slab.shape[:-2] + tuple(s)))
        off += n + (-n) % 128
    return out


def _rows_for(shapes):
    rows = sum((math.prod(s) + 127) // 128 for s in shapes)
    return rows + (-rows) % 8


SMALL_NAMES = ("a_conv_w", "a_A_log", "a_dt_bias", "a_norm_w", "b_sinks", "ln_g", "ln_b")
SMALL_SHARDED = {"a_conv_w": 2, "ln_g": 2, "ln_b": 2}
SMALL_FULL = {"a_conv_w": (2, 4, 2304), "a_A_log": (2, 6), "a_dt_bias": (2, 6), "a_norm_w": (2, 128),
              "b_sinks": (2, 12), "ln_g": (4, 2, 1024), "ln_b": (4, 2, 1024)}


def kernel(x, mem, positions, a_w_in, a_conv_w, a_A_log, a_dt_bias, a_norm_w, b_w_in, b_sinks, w_kv_shared, mem_w_kv, w_o, mlp_w_up, mlp_w_down, ln_g, ln_b, loss_target, m_a_w_in, m_a_conv_w, m_a_A_log, m_a_dt_bias, m_a_norm_w, m_b_w_in, m_b_sinks, m_w_kv_shared, m_mem_w_kv, m_w_o, m_mlp_w_up, m_mlp_w_down, m_ln_g, m_ln_b, v_a_w_in, v_a_conv_w, v_a_A_log, v_a_dt_bias, v_a_norm_w, v_b_w_in, v_b_sinks, v_w_kv_shared, v_mem_w_kv, v_w_o, v_mlp_w_up, v_mlp_w_down, v_ln_g, v_ln_b):
    params = dict(a_w_in=a_w_in, a_conv_w=a_conv_w, a_A_log=a_A_log, a_dt_bias=a_dt_bias, a_norm_w=a_norm_w,
                  b_w_in=b_w_in, b_sinks=b_sinks, w_kv_shared=w_kv_shared, mem_w_kv=mem_w_kv, w_o=w_o,
                  mlp_w_up=mlp_w_up, mlp_w_down=mlp_w_down, ln_g=ln_g, ln_b=ln_b)
    mom = dict(a_w_in=m_a_w_in, a_conv_w=m_a_conv_w, a_A_log=m_a_A_log, a_dt_bias=m_a_dt_bias, a_norm_w=m_a_norm_w,
               b_w_in=m_b_w_in, b_sinks=m_b_sinks, w_kv_shared=m_w_kv_shared, mem_w_kv=m_mem_w_kv, w_o=m_w_o,
               mlp_w_up=m_mlp_w_up, mlp_w_down=m_mlp_w_down, ln_g=m_ln_g, ln_b=m_ln_b)
    var = dict(a_w_in=v_a_w_in, a_conv_w=v_a_conv_w, a_A_log=v_a_A_log, a_dt_bias=v_a_dt_bias, a_norm_w=v_a_norm_w,
               b_w_in=v_b_w_in, b_sinks=v_b_sinks, w_kv_shared=v_w_kv_shared, mem_w_kv=v_mem_w_kv, w_o=v_w_o,
               mlp_w_up=v_mlp_w_up, mlp_w_down=v_mlp_w_down, ln_g=v_ln_g, ln_b=v_ln_b)
    me = 4 * lax.axis_index("x") + 2 * lax.axis_index("y") + lax.axis_index("c")

    shards = {(n, i): (params[n] if i is None else params[n][i:i + 1]).astype(BF16) for n, i in ALL_KEYS}
    first = [k[0] for k in GATHER_FIRST]
    ready = dict(zip(GATHER_FIRST, _allgather_weights(first, [shards[k] for k in GATHER_FIRST])))
    sharded_names = [n for n in SMALL_NAMES if n in SMALL_SHARDED]
    shard_shapes = [params[n].shape for n in sharded_names]
    gathered = _small_exchange(_pack([params[n] for n in sharded_names], _rows_for(shard_shapes)), reduce=False)
    small = {n: params[n] for n in SMALL_NAMES if n not in SMALL_SHARDED}
    for n, g in zip(sharded_names, _unpack(gathered, shard_shapes)):
        small[n] = jnp.moveaxis(g, 0, 2).reshape(SMALL_FULL[n])

    loss, dx, recv, small_grads = _local_step(x[0], mem[0], positions[0], loss_target[0], ready, shards, small)
    loss = lax.psum(loss, ("x", "y", "c"))
    last = [k[0] for k in SCATTER_LAST]
    recv.update(zip(SCATTER_LAST, _scatter_grads(last, [recv[k] for k in SCATTER_LAST])))
    out = {}
    for n in BIG_NAMES:
        shp = params[n].shape
        rows = math.prod(shp[:-1])
        recvs = [recv[k].reshape(N_DEV, -1, shp[-1]) for k in sorted(k for k in ALL_KEYS if k[0] == n)]
        res = _adamw(recvs, params[n].reshape(rows, shp[-1]), mom[n].reshape(rows, shp[-1]),
                     var[n].reshape(rows, shp[-1]), 32, "adamw_" + n)
        out[n] = [t.reshape(shp) for t in res]
    full_shapes = [SMALL_FULL[n] for n in SMALL_NAMES]
    summed = _small_exchange(_pack([small_grads[n] for n in SMALL_NAMES], _rows_for(full_shapes)), reduce=True)
    local_g = []
    for n, g in zip(SMALL_NAMES, _unpack(summed, full_shapes)):
        if n in SMALL_SHARDED:
            size = params[n].shape[2]
            g = lax.dynamic_slice_in_dim(g, me * size, size, axis=2)
        local_g.append(g)
    local_shapes = [params[n].shape for n in SMALL_NAMES]
    rows = _rows_for(local_shapes)
    res = _adamw([_pack(local_g, rows)[None]], _pack([params[n] for n in SMALL_NAMES], rows),
                 _pack([mom[n] for n in SMALL_NAMES], rows), _pack([var[n] for n in SMALL_NAMES], rows), rows,
                 "adamw_small")
    unpacked = [_unpack(t, local_shapes) for t in res]
    for i, n in enumerate(SMALL_NAMES):
        out[n] = [unpacked[k][i] for k in range(4)]

    order = ("a_w_in", "a_conv_w", "a_A_log", "a_dt_bias", "a_norm_w", "b_w_in", "b_sinks", "w_kv_shared",
             "mem_w_kv", "w_o", "mlp_w_up", "mlp_w_down", "ln_g", "ln_b")
    return (loss, dx[None], *[out[n][0] for n in order], *[out[n][1] for n in order],
            *[out[n][2] for n in order], *[out[n][3] for n in order])
```

```python
import functools
import math

import jax
import jax.numpy as jnp
from jax import lax
from jax.experimental import pallas as pl
from jax.experimental.pallas import tpu as pltpu

F32 = jnp.float32
BF16 = jnp.bfloat16

D_MODEL = 1024
DEPTH = 4
N_A = 2
MEM_HEADS = 4
MEM_DH = 64
MEM_W = 256
DN_HEADS = 6
DN_D = 128
DN_W = 768
CHUNK = 64
SWA_DH = 64
SWA_HEADS = 12
WINDOW = 128
ROPE_THETA = 10000.0
LN_EPS = 1e-5
NORM_EPS = 1e-6
DN_ALPHA = (2.0 * DEPTH) ** 0.25
A_IN = 3340
A_IN_PAD = 3456
N_DEV = 8

ADAM_LR = 0.001
ADAM_B1 = 0.9
ADAM_B2 = 0.999
ADAM_EPS = 1e-08
ADAM_WD = 0.01
ADAM_STEP = 10

VMEM_LIMIT = 52 * 1024 * 1024
NEG_BIG = -1e30


def _cparams(sem):
    return pltpu.CompilerParams(dimension_semantics=sem, vmem_limit_bytes=VMEM_LIMIT)


_CONTRACT = {"nn": (1, 0), "nt": (1, 1), "tn": (0, 0)}


def _raw_mm(a, b, mode, prec):
    ca, cb = _CONTRACT[mode]
    dims = (((ca,), (cb,)), ((), ()))
    dot = lambda p, q: lax.dot_general(p, q, dims, preferred_element_type=F32)
    if prec == "bf16":
        return dot(a.astype(BF16), b.astype(BF16))
    a, b = a.astype(F32), b.astype(F32)
    a_hi, b_hi = a.astype(BF16), b.astype(BF16)
    if prec == "sela":
        return dot(a_hi, b_hi) + dot(a_hi, (b - b_hi.astype(F32)).astype(BF16))
    a_lo = (a - a_hi.astype(F32)).astype(BF16)
    if prec == "selb":
        return dot(a_hi, b_hi) + dot(a_lo, b_hi)
    b_lo = (b - b_hi.astype(F32)).astype(BF16)
    return dot(a_hi, b_hi) + (dot(a_hi, b_lo) + dot(a_lo, b_hi))


@functools.partial(jax.custom_vjp, nondiff_argnums=(2, 3))
def mm(a, b, mode, prec):
    return _raw_mm(a, b, mode, prec)


def _mm_fwd(a, b, mode, prec):
    return _raw_mm(a, b, mode, prec), (a, b)


def _mm_bwd(mode, prec, res, ct):
    a, b = res
    if prec == "sela":
        pa, pb = "f32", {"nn": "sela", "nt": "selb", "tn": "sela"}[mode]
    elif prec == "selb":
        pa, pb = {"nn": "selb", "nt": "selb", "tn": "sela"}[mode], "f32"
    else:
        pa = pb = prec
    if mode == "nn":
        return mm(ct, b, "nt", pa), mm(a, ct, "tn", pb)
    if mode == "nt":
        return mm(ct, b, "nn", pa), mm(ct, a, "tn", pb)
    return mm(b, ct, "nt", pa), mm(a, ct, "nn", pb)


mm.defvjp(_mm_fwd, _mm_bwd)


@jax.custom_vjp
def _softplus(x):
    y = jnp.exp(-jnp.abs(x))
    log1p_y = jnp.where(y < 1e-2, y * (1.0 - y * (0.5 - y * (1.0 / 3.0))), jnp.log(1.0 + y))
    return jnp.maximum(x, 0.0) + log1p_y


def _softplus_fwd(x):
    return _softplus(x), x


def _softplus_bwd(x, ct):
    return (ct * jax.nn.sigmoid(x),)


_softplus.defvjp(_softplus_fwd, _softplus_bwd)


def _iota(shape, dim):
    return lax.broadcasted_iota(jnp.int32, shape, dim)


class _Ride:
    def __init__(self, srcs, dst_shapes, src_view, dst_view):
        self.srcs, self.dst_shapes, self.src_view, self.dst_view = list(srcs), list(dst_shapes), src_view, dst_view
        self.n = len(self.srcs)
        self.any_specs = [pl.BlockSpec(memory_space=pl.ANY)] * self.n
        self.scratch = [pltpu.SemaphoreType.DMA((self.n, N_DEV - 1)), pltpu.SemaphoreType.DMA((self.n, N_DEV - 1)),
                        pltpu.SemaphoreType.DMA((self.n,))]

    def copies(self, src_refs, dst_refs, sems):
        send_sems, recv_sems, local_sems = sems
        me, peers = _me_and_peers()
        local, out, inc = [], [], []
        for t in range(self.n):
            local.append(pltpu.make_async_copy(self.src_view(t, src_refs[t], me), self.dst_view(t, dst_refs[t], me),
                                               local_sems.at[t]))
            for k, (dev, idx) in enumerate(peers):
                mk = lambda s, d: pltpu.make_async_remote_copy(
                    src_ref=s, dst_ref=d, send_sem=send_sems.at[t, k], recv_sem=recv_sems.at[t, k], device_id=dev,
                    device_id_type=pl.DeviceIdType.MESH)
                out.append(mk(self.src_view(t, src_refs[t], idx), self.dst_view(t, dst_refs[t], me)))
                inc.append(mk(self.src_view(t, src_refs[t], me), self.dst_view(t, dst_refs[t], idx)))
        return local, out, inc

    def start(self, grid, src_refs, dst_refs, sems):
        first = pl.program_id(0) == 0
        for a in range(1, len(grid)):
            first = jnp.logical_and(first, pl.program_id(a) == 0)

        @pl.when(first)
        def _():
            local, out, _ = self.copies(src_refs, dst_refs, sems)
            for cp in local + out:
                cp.start()

    def finish(self, grid, src_refs, dst_refs, sems):
        last = pl.program_id(0) == grid[0] - 1
        for a in range(1, len(grid)):
            last = jnp.logical_and(last, pl.program_id(a) == grid[a] - 1)

        @pl.when(last)
        def _():
            local, out, inc = self.copies(src_refs, dst_refs, sems)
            for cp in inc:
                cp.wait_recv()
            for cp in out:
                cp.wait_send()
            for cp in local:
                cp.wait()


def _block_fwd(fn, ins, in_specs, out_shapes, out_specs, grid, name, ride=None):
    n_in, n_out = len(ins), len(out_shapes)
    n_ride = ride.n if ride else 0

    def body(*refs):
        pids = tuple(pl.program_id(a) for a in range(len(grid)))
        ride_refs = (refs[n_in:n_in + n_ride], refs[n_in + n_ride + n_out:n_in + 2 * n_ride + n_out],
                     refs[n_in + 2 * n_ride + n_out:])
        if ride:
            ride.start(grid, *ride_refs)
        vals = [r[...].astype(F32) for r in refs[:n_in]]
        outs = fn(pids, *vals)
        for r, o in zip(refs[n_in + n_ride:n_in + n_ride + n_out], outs):
            r[...] = o.astype(r.dtype)
        if ride:
            ride.finish(grid, *ride_refs)

    if not ride:
        return pl.pallas_call(
            body, grid=grid, in_specs=in_specs, out_specs=out_specs, out_shape=out_shapes, name=name,
            compiler_params=_cparams(("parallel",) * len(grid)))(*ins)
    res = pl.pallas_call(
        body, grid=grid, in_specs=list(in_specs) + ride.any_specs, out_specs=list(out_specs) + ride.any_specs,
        out_shape=list(out_shapes) + ride.dst_shapes, scratch_shapes=ride.scratch, name=name,
        compiler_params=_cparams(("arbitrary",) * len(grid)))(*ins, *ride.srcs)
    return res[:n_out], res[n_out:]


def _block_bwd(fn, ins, in_specs, cts, ct_specs, kinds, g_shapes, g_specs, grid, name, ride=None):
    n_in, n_ct, n_g = len(ins), len(cts), len(g_shapes)
    n_ride = ride.n if ride else 0
    didx = [i for i, k in enumerate(kinds) if k]

    def body(*refs):
        in_refs, ct_refs = refs[:n_in], refs[n_in:n_in + n_ct]
        base = n_in + n_ct
        g_refs = refs[base + n_ride:base + n_ride + n_g]
        ride_refs = (refs[base:base + n_ride], refs[base + n_ride + n_g:base + 2 * n_ride + n_g],
                     refs[base + 2 * n_ride + n_g:])
        if ride:
            ride.start(grid, *ride_refs)
        pids = tuple(pl.program_id(a) for a in range(len(grid)))
        vals = [r[...].astype(F32) for r in in_refs]

        def f(*dvals):
            full = list(vals)
            for i, v in zip(didx, dvals):
                full[i] = v
            return tuple(fn(pids, *full))

        _, vjp = jax.vjp(f, *[vals[i] for i in didx])
        gs = vjp(tuple(r[...].astype(F32) for r in ct_refs))
        first = pids[0] == 0
        for p in pids[1:]:
            first = jnp.logical_and(first, p == 0)
        for i, g, r in zip(didx, gs, g_refs):
            if kinds[i] == "s":
                r[...] = g.astype(r.dtype)
            else:
                @pl.when(first)
                def _(r=r):
                    r[...] = jnp.zeros(r.shape, r.dtype)

                r[...] += g.astype(r.dtype)
        if ride:
            ride.finish(grid, *ride_refs)

    sem = ("arbitrary",) * len(grid) if "a" in kinds or ride else ("parallel",) * len(grid)
    if not ride:
        return pl.pallas_call(
            body, grid=grid, in_specs=list(in_specs) + list(ct_specs), out_specs=g_specs, out_shape=g_shapes,
            name=name, compiler_params=_cparams(sem))(*ins, *cts)
    res = pl.pallas_call(
        body, grid=grid, in_specs=list(in_specs) + list(ct_specs) + ride.any_specs,
        out_specs=list(g_specs) + ride.any_specs, out_shape=list(g_shapes) + ride.dst_shapes,
        scratch_shapes=ride.scratch, name=name, compiler_params=_cparams(sem))(*ins, *cts, *ride.srcs)
    return res[:n_g], res[n_g:]


def _rows(tb, width, col=0):
    return pl.BlockSpec((tb, width), lambda i, col=col: (i, col))


def _whole(shape):
    return pl.BlockSpec(shape, lambda *_: (0,) * len(shape))


def _sds(shape, dtype=F32):
    return jax.ShapeDtypeStruct(shape, dtype)


def _matmul(a, b, mode, out_dtypes, name, epi=None, extras=(), tm=1024, tn=1024, tk=1024,
            b_spec=None, n_total=None, out_specs=None, out_shapes=None):
    if mode == "tn":
        k_total, m_total = a.shape
    else:
        m_total, k_total = a.shape
    if n_total is None:
        n_total = b.shape[0] if mode == "nt" else b.shape[1]
    tm, tn, tk = min(tm, m_total), min(tn, n_total), min(tk, k_total)
    assert m_total % tm == 0 and n_total % tn == 0 and k_total % tk == 0, (name, a.shape, b.shape)
    grid = (m_total // tm, n_total // tn, k_total // tk)
    nk = grid[2]
    if mode == "tn":
        a_spec = pl.BlockSpec((tk, tm), lambda i, j, k: (k, i))
    else:
        a_spec = pl.BlockSpec((tm, tk), lambda i, j, k: (i, k))
    if b_spec is None:
        if mode == "nt":
            b_spec = pl.BlockSpec((tn, tk), lambda i, j, k: (j, k))
        else:
            b_spec = pl.BlockSpec((tk, tn), lambda i, j, k: (k, j))
    tile = pl.BlockSpec((tm, tn), lambda i, j, k: (i, j))
    n_ex, n_out = len(extras), len(out_dtypes)
    ca, cb = _CONTRACT[mode]
    dims = (((ca,), (cb,)), ((), ()))

    def body(*refs):
        a_ref, b_ref = refs[:2]
        ex_refs = refs[2:2 + n_ex]
        out_refs = refs[2 + n_ex:2 + n_ex + n_out]
        part = lax.dot_general(a_ref[...], b_ref[...], dims, preferred_element_type=F32)

        def finish(val):
            res = epi(val, *[e[...] for e in ex_refs]) if epi is not None else (val,)
            for r, o in zip(out_refs, res):
                r[...] = o.astype(r.dtype)

        if nk == 1:
            finish(part)
        else:
            acc = refs[-1]
            k = pl.program_id(2)

            @pl.when(k == 0)
            def _():
                acc[...] = part

            @pl.when(k > 0)
            def _():
                acc[...] += part

            @pl.when(k == nk - 1)
            def _():
                finish(acc[...])

    if out_shapes is None:
        out_shapes = [_sds((m_total, n_total), d) for d in out_dtypes]
        out_specs = [tile] * n_out
    outs = pl.pallas_call(
        body, grid=grid, in_specs=[a_spec, b_spec] + [tile] * n_ex, out_specs=out_specs, out_shape=out_shapes,
        scratch_shapes=[pltpu.VMEM((tm, tn), F32)] if nk > 1 else [], name=name,
        compiler_params=_cparams(("parallel", "parallel", "arbitrary")))(a, b, *extras)
    return outs if n_out > 1 else outs[0]


def _matmul_ln(a, b, h, g, beta, name, tm=1024, tk=2048, chunk=256):
    m_total, k_total = a.shape
    n = b.shape[1]
    tm, tk = min(tm, m_total), min(tk, k_total)
    chunk = min(chunk, tm)
    assert m_total % tm == 0 and k_total % tk == 0 and tm % chunk == 0, (name, a.shape, b.shape)
    nk = k_total // tk
    dims = (((1,), (0,)), ((), ()))

    def body(a_ref, b_ref, h_ref, g_ref, beta_ref, mix_ref, y_ref, yb_ref, *acc):
        k = pl.program_id(1)

        def finish():
            for c in range(tm // chunk):
                rows = pl.ds(chunk * c, chunk)
                val = lax.dot_general(a_ref[rows, :], b_ref[...], dims, preferred_element_type=F32)
                if nk > 1:
                    val = val + acc[0][rows, :]
                mix_ref[rows, :] = val
                (y,) = _ln_fn(None, h_ref[rows, :], val, g_ref[...], beta_ref[...])
                y_ref[rows, :] = y
                yb_ref[rows, :] = y.astype(BF16)

        if nk == 1:
            finish()
        else:
            part = lambda: lax.dot_general(a_ref[...], b_ref[...], dims, preferred_element_type=F32)

            @pl.when(k == 0)
            def _():
                acc[0][...] = part()

            if nk > 2:
                @pl.when(jnp.logical_and(k > 0, k < nk - 1))
                def _():
                    acc[0][...] += part()

            @pl.when(k == nk - 1)
            def _():
                finish()

    tile = pl.BlockSpec((tm, n), lambda i, k: (i, 0))
    row = pl.BlockSpec((1, n), lambda i, k: (0, 0))
    return pl.pallas_call(
        body, grid=(m_total // tm, nk),
        in_specs=[pl.BlockSpec((tm, tk), lambda i, k: (i, k)), pl.BlockSpec((tk, n), lambda i, k: (k, 0)), tile, row,
                  row],
        out_specs=[tile, tile, tile],
        out_shape=[_sds((m_total, n)), _sds((m_total, n)), _sds((m_total, n), BF16)],
        scratch_shapes=[pltpu.VMEM((tm, n), F32)] if nk > 1 else [], name=name,
        compiler_params=_cparams(("parallel", "arbitrary")))(a, b, h, g, beta)


def _silu(x):
    return x * jax.nn.sigmoid(x)


def _rowa_fn(pids, c, ab, alog, dtb):
    tb = c.shape[0]
    s = _silu(c)
    qs, ks = [], []
    for h in range(DN_HEADS):
        qh = s[:, DN_D * h:DN_D * (h + 1)]
        qs.append(qh * lax.rsqrt(jnp.sum(qh * qh, axis=-1, keepdims=True) + NORM_EPS) * (DN_D ** -0.5))
        kh = s[:, DN_W + DN_D * h:DN_W + DN_D * (h + 1)]
        ks.append(kh * lax.rsqrt(jnp.sum(kh * kh, axis=-1, keepdims=True) + NORM_EPS))
    q = jnp.concatenate(qs, axis=1)
    k = jnp.concatenate(ks, axis=1)
    v = s[:, 2 * DN_W:3 * DN_W]
    g128 = -jnp.exp(alog) * _softplus(ab + dtb)
    b128 = jax.nn.sigmoid(ab)
    r, cc = _iota((tb, tb), 0), _iota((tb, tb), 1)
    tri = jnp.where(((r >> 6) == (cc >> 6)) & (r >= cc), 1.0, 0.0)
    gc128 = mm(tri, g128, "nn", "sela")
    lane, col = _iota((128, DN_W), 0), _iota((128, DN_W), 1)
    exp_a = jnp.where(lane == (col >> 7), 1.0, 0.0)
    exp_b = jnp.where(lane == (col >> 7) + DN_HEADS, 1.0, 0.0)
    return q, k, v, mm(gc128, exp_a, "nn", "selb"), mm(b128, exp_b, "nn", "selb")


def _tri_inv_raw(lows, block):
    n = lows[0].shape[0]
    r, c = _iota((n, n), 0), _iota((n, n), 1)
    lg = 0
    xs = None
    while (1 << lg) < block:
        off = ((r >> (lg + 1)) == (c >> (lg + 1))) & (((r >> lg) & 1) == 1) & (((c >> lg) & 1) == 0)
        cblks = [jnp.where(off, low, 0.0) for low in lows]
        if xs is None:
            xs = [jnp.where(r == c, 1.0, 0.0) - cb for cb in cblks]
        else:
            ys = [mm(cb, x, "nn", "f32") for cb, x in zip(cblks, xs)]
            xs = [x - mm(x, y, "nn", "f32") for x, y in zip(xs, ys)]
        lg += 1
    return tuple(xs)


def _tri_inv_cotangent(block, xs, cts):
    n = xs[0].shape[0]
    r, c = _iota((n, n), 0), _iota((n, n), 1)
    shift = block.bit_length() - 1
    keep = ((r >> shift) == (c >> shift)) & (r > c)
    gs = [mm(x, ct, "tn", "f32") for x, ct in zip(xs, cts)]
    gs = [mm(g, x, "nt", "f32") for g, x in zip(gs, xs)]
    return tuple(jnp.where(keep, -g, 0.0) for g in gs)


@functools.partial(jax.custom_vjp, nondiff_argnums=(1,))
def _tri_inv(lows, block):
    return _tri_inv_raw(lows, block)


def _tri_inv_fwd(lows, block):
    xs = _tri_inv_raw(lows, block)
    return xs, xs


def _tri_inv_bwd(block, xs, cts):
    return (_tri_inv_cotangent(block, xs, cts),)


_tri_inv.defvjp(_tri_inv_fwd, _tri_inv_bwd)


@functools.partial(jax.custom_vjp, nondiff_argnums=(2,))
def _tri_inv_known(lows, known, block):
    return known


def _tri_inv_known_fwd(lows, known, block):
    return known, known


def _tri_inv_known_bwd(block, xs, cts):
    return _tri_inv_cotangent(block, xs, cts), tuple(jnp.zeros_like(x) for x in xs)


_tri_inv_known.defvjp(_tri_inv_known_fwd, _tri_inv_known_bwd)


PAIR = 2 * CHUNK


def _dn1_pairs(q, k, v, gc, beta, tinv_known=None):
    assert PAIR == DN_D
    n = PAIR
    pairs = range(q.shape[0] // n)
    cut = lambda t: [t[n * j:n * (j + 1)] for j in pairs]
    q, k, v, gc, beta = cut(q), cut(k), cut(v), cut(gc), cut(beta)
    onehot = jnp.where(_iota((n, DN_D), 1) == 0, 1.0, 0.0)
    r, c = _iota((n, n), 0), _iota((n, n), 1)
    same = (r >> 6) == (c >> 6)
    incl, strict = same & (r >= c), same & (r > c)
    row = _iota((n, DN_D), 0)
    eg = [jnp.exp(g) for g in gc]
    kb = [k[j] * beta[j] for j in pairs]
    g_row = [mm(onehot, g, "nt", "sela") for g in gc]
    kk = [mm(kb[j], k[j], "nt", "bf16") for j in pairs]
    qk = [mm(q[j], k[j], "nt", "bf16") for j in pairs]
    decay = [jnp.exp(jnp.where(incl, gc[j] - g_row[j], NEG_BIG)) for j in pairs]
    low = tuple(jnp.where(strict, kk[j] * decay[j], 0.0) for j in pairs)
    if tinv_known is None:
        tinv = _tri_inv(low, CHUNK)
    else:
        tinv = _tri_inv_known(low, tuple(cut(tinv_known)), CHUNK)
    uw = [mm(tinv[j], jnp.concatenate([v[j] * beta[j], kb[j] * eg[j]], axis=1), "nn", "f32") for j in pairs]
    intra = [jnp.where(incl, qk[j] * decay[j], 0.0) for j in pairs]
    g_last = []
    for g in gc:
        last0 = jnp.sum(jnp.where(row == CHUNK - 1, g, 0.0), axis=0, keepdims=True)
        last1 = jnp.sum(jnp.where(row == PAIR - 1, g, 0.0), axis=0, keepdims=True)
        g_last.append(jnp.where(row < CHUNK, last0, last1))
    join = lambda parts: jnp.concatenate(parts, axis=0)
    return (join([t[:, :DN_D] for t in uw]), join([t[:, DN_D:] for t in uw]), join(intra),
            join([q[j] * eg[j] for j in pairs]), join([k[j] * jnp.exp(g_last[j] - gc[j]) for j in pairs]),
            join([jnp.exp(g) for g in g_last]), join(list(tinv)))


def _dn1_fn(pids, q, k, v, gc, beta):
    return _dn1_pairs(q, k, v, gc, beta)


def _dn1_fn_known(pids, q, k, v, gc, beta, tinv):
    return _dn1_pairs(q, k, v, gc, beta, tinv)[:6]


def _dn2_step(half, state, qd, kd, u, w, intra, cd_row):
    heads = range(len(state))
    v_new = [u[h] - mm(w[h], state[h], "nn", "bf16") for h in heads]
    zeros = jnp.zeros_like(v_new[0])
    v_pair = [jnp.concatenate([v, zeros] if half == 0 else [zeros, v], axis=0) for v in v_new]
    from_state = [mm(qd[h], state[h], "nn", "bf16") for h in heads]
    out = tuple(from_state[h] + mm(intra[h], v_pair[h], "nn", "bf16") for h in heads)
    return out, tuple(state[h] * cd_row[h] + mm(kd[h], v_new[h], "tn", "bf16") for h in heads)


def _post_fn(pids, o, z, nw):
    outs = []
    for h in range(DN_HEADS):
        oh = o[:, DN_D * h:DN_D * (h + 1)]
        zh = z[:, DN_D * h:DN_D * (h + 1)]
        y = oh * lax.rsqrt(jnp.mean(oh * oh, axis=-1, keepdims=True) + NORM_EPS) * nw
        outs.append(y * _silu(zh))
    return (jnp.concatenate(outs, axis=1),)


def _memattn_fn(pids, qm, kvm):
    kmem, vmem = kvm[:, :MEM_W], kvm[:, MEM_W:]
    lane = _iota((1, MEM_W), 1)
    heads = range(MEM_HEADS)
    hm = [jnp.where((lane >> 6) == h, 1.0, 0.0) for h in heads]
    s = [mm(qm * (hm[h] * MEM_DH ** -0.5), kmem, "nt", "bf16") for h in heads]
    e = [jnp.exp(t - lax.stop_gradient(jnp.max(t, axis=-1, keepdims=True))) for t in s]
    o = [mm(e[h], vmem, "nn", "bf16") * (hm[h] / jnp.sum(e[h], axis=-1, keepdims=True)) for h in heads]
    return ((o[0] + o[1]) + (o[2] + o[3]),)


def _ln_fn(pids, h, mix, g, b):
    x = DN_ALPHA * h + mix
    mu = jnp.mean(x, axis=-1, keepdims=True)
    xc = x - mu
    var = jnp.mean(xc * xc, axis=-1, keepdims=True)
    return (xc * lax.rsqrt(var + LN_EPS) * g + b,)


def _rope_matrix():
    i, j = _iota((128, 128), 0), _iota((128, 128), 1)
    jj = j & 63
    return jnp.where((jj < 32) & (i == j + 32), -1.0, 0.0) + jnp.where((jj >= 32) & (i == j - 32), 1.0, 0.0)


def _rope128(x, cos, sin, rot):
    return x * cos + mm(x, rot, "nn", "selb") * sin


def _krope_fn(pids, kraw, cos, sin):
    rot = _rope_matrix()
    return (jnp.concatenate([_rope128(kraw[:, 128 * g:128 * (g + 1)], cos, sin, rot) for g in range(2)], axis=1),)


def _swa_fn(pids, qraw, cos, sin, k_halo, k_cur, v_halo, v_cur, sinks):
    tb = qraw.shape[0]
    nwin = tb // WINDOW
    rot = _rope_matrix()
    kcat = jnp.concatenate([k_halo, k_cur], axis=0)
    vcat = jnp.concatenate([v_halo, v_cur], axis=0)
    lane = _iota((1, 128), 1)
    halves = (jnp.where(lane < 64, 1.0, 0.0), jnp.where(lane >= 64, 1.0, 0.0))
    group = SWA_HEADS // 2
    rows = group * WINDOW
    in_cur = _iota((rows, WINDOW), 1) <= (_iota((rows, WINDOW), 0) & (WINDOW - 1))
    qg = [_rope128(qraw[:, 128 * p:128 * (p + 1)], cos, sin, rot) for p in range(group)]
    sink = []
    for kv in range(2):
        cols = [jnp.sum(jnp.where(lane == group * kv + i, sinks, 0.0), axis=-1, keepdims=True)
                + jnp.zeros((WINDOW, 1), F32) for i in range(group)]
        sink.append(jnp.concatenate(cols, axis=0))
    units = [(w, kv) for w in range(nwin) for kv in range(2)]
    n_units = range(len(units))
    q6 = [jnp.concatenate([qg[3 * kv + i // 2][WINDOW * w:WINDOW * (w + 1)] * (halves[i % 2] * SWA_DH ** -0.5)
                           for i in range(group)], axis=0) for w, kv in units]
    blk = lambda cat, w, kv: cat[WINDOW * w:WINDOW * (w + 1), 128 * kv:128 * (kv + 1)]
    s_prev = [mm(q6[u], blk(kcat, w, kv), "nt", "bf16") for u, (w, kv) in enumerate(units)]
    s_cur = [mm(q6[u], blk(kcat, w + 1, kv), "nt", "bf16") for u, (w, kv) in enumerate(units)]
    s = [jnp.where(in_cur, s_cur[u], jnp.where(pids[0] * nwin + w > 0, s_prev[u], NEG_BIG))
         for u, (w, kv) in enumerate(units)]
    m = [lax.stop_gradient(jnp.maximum(jnp.max(s[u], axis=-1, keepdims=True), sink[kv]))
         for u, (w, kv) in enumerate(units)]
    e = [jnp.exp(s[u] - m[u]) for u in n_units]
    denom = [jnp.sum(e[u], axis=-1, keepdims=True) + jnp.exp(sink[kv] - m[u]) for u, (w, kv) in enumerate(units)]
    o = [(mm(jnp.where(in_cur, e[u], 0.0), blk(vcat, w + 1, kv), "nn", "bf16")
          + mm(jnp.where(in_cur, 0.0, e[u]), blk(vcat, w, kv), "nn", "bf16")) / denom[u]
         for u, (w, kv) in enumerate(units)]
    out_rows = []
    for w in range(nwin):
        lanes = []
        for p in range(group):
            ou = o[units.index((w, p // 3))]
            i = 2 * (p % 3)
            lanes.append(ou[WINDOW * i:WINDOW * (i + 1)] * halves[0] + ou[WINDOW * (i + 1):WINDOW * (i + 2)] * halves[1])
        out_rows.append(jnp.concatenate(lanes, axis=1))
    return (jnp.concatenate(out_rows, axis=0),)


def _conv_fwd(proj, conv_w, tb):
    t_total = proj.shape[0]
    width = conv_w.shape[1]
    nb = t_total // tb

    def body(cur_ref, prev_ref, w_ref, out_ref):
        i = pl.program_id(0)
        prev = jnp.where(i > 0, prev_ref[...], 0.0)
        xcat = jnp.concatenate([prev, cur_ref[...]], axis=0)
        acc = xcat[8:] * w_ref[3:4, :]
        for j in range(3):
            acc = acc + pltpu.roll(xcat, 3 - j, 0)[8:] * w_ref[j:j + 1, :]
        out_ref[...] = acc

    return pl.pallas_call(
        body, grid=(nb,),
        in_specs=[pl.BlockSpec((tb, width), lambda i: (i, 0)),
                  pl.BlockSpec((8, width), lambda i: (jnp.maximum(i * (tb // 8) - 1, 0), 0)),
                  _whole((4, width))],
        out_specs=pl.BlockSpec((tb, width), lambda i: (i, 0)), out_shape=_sds((t_total, width)),
        name="conv_fwd", compiler_params=_cparams(("parallel",)))(proj, proj, conv_w)


def _conv_bwd(dc, proj, conv_w, tb):
    t_total, width = dc.shape
    nb = t_total // tb

    def body(dcur_ref, dnext_ref, cur_ref, prev_ref, w_ref, dx_ref, dw_ref):
        i = pl.program_id(0)
        dnext = jnp.where(i < nb - 1, dnext_ref[...], 0.0)
        dcur = dcur_ref[...]
        dcat = jnp.concatenate([dcur, dnext], axis=0)
        prev = jnp.where(i > 0, prev_ref[...], 0.0)
        xcat = jnp.concatenate([prev, cur_ref[...]], axis=0)

        @pl.when(i == 0)
        def _():
            dw_ref[...] = jnp.zeros(dw_ref.shape, F32)

        dx = dcur * w_ref[3:4, :]
        dw_ref[3:4, :] += jnp.sum(dcur * xcat[8:], axis=0, keepdims=True)
        for j in range(3):
            dx = dx + pltpu.roll(dcat, 8 - (3 - j), 0)[8:] * w_ref[j:j + 1, :]
            dw_ref[j:j + 1, :] += jnp.sum(dcur * pltpu.roll(xcat, 3 - j, 0)[8:], axis=0, keepdims=True)
        dx_ref[...] = dx.astype(dx_ref.dtype)

    return pl.pallas_call(
        body, grid=(nb,),
        in_specs=[pl.BlockSpec((tb, width), lambda i: (i, 0)),
                  pl.BlockSpec((8, width), lambda i: (jnp.minimum((i + 1) * (tb // 8), t_total // 8 - 1), 0)),
                  pl.BlockSpec((tb, width), lambda i: (i, 0)),
                  pl.BlockSpec((8, width), lambda i: (jnp.maximum(i * (tb // 8) - 1, 0), 0)),
                  _whole((4, width))],
        out_specs=[pl.BlockSpec((tb, width), lambda i: (i, 0)), _whole((4, width))],
        out_shape=[_sds((t_total, width), BF16), _sds((4, width))],
        name="conv_bwd", compiler_params=_cparams(("arbitrary",)))(dc, dc, proj, proj, conv_w)


def _head_spec(tb, nb=None):
    if nb is None:
        return pl.BlockSpec((tb, DN_D), lambda h, i: (i, h))
    return pl.BlockSpec((tb, DN_D), lambda h, i: (nb - 1 - i, h))


def _intra_spec(tb, nb=None):
    if nb is None:
        return pl.BlockSpec((None, tb, PAIR), lambda h, i: (h, i, 0))
    return pl.BlockSpec((None, tb, PAIR), lambda h, i: (h, nb - 1 - i, 0))


def _state_spec(tb, nb=None):
    if nb is None:
        return pl.BlockSpec((None, tb // CHUNK, DN_D, DN_D), lambda h, i: (h, i, 0, 0))
    return pl.BlockSpec((None, tb // CHUNK, DN_D, DN_D), lambda h, i: (h, nb - 1 - i, 0, 0))


def _scan_specs(tb, nb=None):
    blk = (lambda i: i) if nb is None else (lambda i: nb - 1 - i)
    rows = pl.BlockSpec((tb, DN_W), lambda i: (blk(i), 0))
    pair = pl.BlockSpec((DN_HEADS, tb, PAIR), lambda i: (0, blk(i), 0))
    states = pl.BlockSpec((DN_HEADS, tb // CHUNK, DN_D, DN_D), lambda i: (0, blk(i), 0, 0))
    return rows, pair, states


def _dn2_fwd(qd, kd, u, w, intra, cd, tb):
    t_total = qd.shape[0]
    rows, pair, states = _scan_specs(tb)

    def body(qd_ref, kd_ref, u_ref, w_ref, a_ref, cd_ref, o_ref, save_ref, state):
        @pl.when(pl.program_id(0) == 0)
        def _():
            state[...] = jnp.zeros(state.shape, F32)

        heads = range(DN_HEADS)
        lanes = [pl.ds(DN_D * h, DN_D) for h in heads]
        for j in range(tb // CHUNK):
            sl = pl.ds(CHUNK * j, CHUNK)
            s0 = tuple(state[h] for h in heads)
            for h in heads:
                save_ref[h, j] = s0[h]
            per_head = lambda ref: tuple(ref[sl, lanes[h]] for h in heads)
            out, s1 = _dn2_step(j % 2, s0, per_head(qd_ref), per_head(kd_ref), per_head(u_ref), per_head(w_ref),
                                tuple(a_ref[h, sl, :] for h in heads),
                                tuple(cd_ref[pl.ds(CHUNK * j, 1), lanes[h]] for h in heads))
            for h in heads:
                o_ref[sl, lanes[h]] = out[h]
                state[h] = s1[h]

    return pl.pallas_call(
        body, grid=(t_total // tb,), in_specs=[rows, rows, rows, rows, pair, rows],
        out_specs=[rows, states],
        out_shape=[_sds((t_total, DN_W)), _sds((DN_HEADS, t_total // CHUNK, DN_D, DN_D))],
        scratch_shapes=[pltpu.VMEM((DN_HEADS, DN_D, DN_D), F32)], name="dn2_fwd",
        compiler_params=_cparams(("arbitrary",)))(qd, kd, u, w, intra, cd)


def _dn2_bwd(qd, kd, u, w, intra, cd, saved, d_o, tb):
    t_total = qd.shape[0]
    nb = t_total // tb
    rows, pair, states = _scan_specs(tb, nb)

    def body(qd_ref, kd_ref, u_ref, w_ref, a_ref, cd_ref, save_ref, do_ref,
             dqd_ref, dkd_ref, du_ref, dw_ref, da_ref, dcd_ref, dstate):
        @pl.when(pl.program_id(0) == 0)
        def _():
            dstate[...] = jnp.zeros(dstate.shape, F32)

        first_row = _iota((CHUNK, DN_D), 0) == 0
        heads = range(DN_HEADS)
        lanes = [pl.ds(DN_D * h, DN_D) for h in heads]
        for j in reversed(range(tb // CHUNK)):
            sl = pl.ds(CHUNK * j, CHUNK)
            per_head = lambda ref: tuple(ref[sl, lanes[h]] for h in heads)
            _, vjp = jax.vjp(functools.partial(_dn2_step, j % 2), tuple(save_ref[h, j] for h in heads),
                             per_head(qd_ref), per_head(kd_ref), per_head(u_ref), per_head(w_ref),
                             tuple(a_ref[h, sl, :] for h in heads),
                             tuple(cd_ref[pl.ds(CHUNK * j, 1), lanes[h]] for h in heads))
            ds0, dqd, dkd, du, dw, da, dcd = vjp((per_head(do_ref), tuple(dstate[h] for h in heads)))
            for h in heads:
                dqd_ref[sl, lanes[h]] = dqd[h]
                dkd_ref[sl, lanes[h]] = dkd[h]
                du_ref[sl, lanes[h]] = du[h]
                dw_ref[sl, lanes[h]] = dw[h]
                da_ref[h, sl, :] = da[h]
                dcd_ref[sl, lanes[h]] = jnp.where(first_row, dcd[h], 0.0)
                dstate[h] = ds0[h]

    full = _sds((t_total, DN_W))
    return pl.pallas_call(
        body, grid=(nb,),
        in_specs=[rows, rows, rows, rows, pair, rows, states, rows],
        out_specs=[rows, rows, rows, rows, pair, rows],
        out_shape=[full, full, full, full, _sds((DN_HEADS, t_total, PAIR)), full],
        scratch_shapes=[pltpu.VMEM((DN_HEADS, DN_D, DN_D), F32)], name="dn2_bwd",
        compiler_params=_cparams(("arbitrary",)))(qd, kd, u, w, intra, cd, saved, d_o)


def _loss_and_grad(y, target, tb):
    t_total, d = y.shape

    def body(y_ref, t_ref, dy_ref, acc_ref):
        @pl.when(pl.program_id(0) == 0)
        def _():
            acc_ref[...] = jnp.zeros(acc_ref.shape, F32)

        err = y_ref[...] - t_ref[...]
        dy_ref[...] = err * (1.0 / d)
        acc_ref[...] += jnp.sum(err * err, axis=0, keepdims=True)

    dy, acc = pl.pallas_call(
        body, grid=(t_total // tb,), in_specs=[_rows(tb, d), _rows(tb, d)],
        out_specs=[_rows(tb, d), _whole((1, d))], out_shape=[_sds((t_total, d)), _sds((1, d))],
        name="loss", compiler_params=_cparams(("arbitrary",)))(y, target)
    return 0.5 * jnp.sum(acc) / d, dy


def _halo_sum(mains, halos, tb):
    t_total, width = mains[0].shape
    nb = t_total // tb
    n = len(mains)

    def body(*refs):
        out_ref = refs[-1]
        i = pl.program_id(0)
        tot = refs[0][...]
        for r in refs[1:n]:
            tot = tot + r[...]
        hal = refs[n][...]
        for r in refs[n + 1:2 * n]:
            hal = hal + r[...]
        hal = jnp.where(i < nb - 1, hal, 0.0)
        out_ref[...] = tot + jnp.concatenate([jnp.zeros((tb - WINDOW, width), F32), hal], axis=0)

    return pl.pallas_call(
        body, grid=(nb,),
        in_specs=[_rows(tb, width)] * n
        + [pl.BlockSpec((None, WINDOW, width), lambda i: (jnp.minimum(i + 1, nb - 1), 0, 0))] * n,
        out_specs=_rows(tb, width), out_shape=_sds((t_total, width)), name="halo_sum",
        compiler_params=_cparams(("parallel",)))(*mains, *halos)


def _adamw(recvs, w, m, v, tr, name):
    slots, _, c_total = recvs[0].shape
    r_total = w.shape[0]
    assert sum(r.shape[1] for r in recvs) == r_total
    tr = min([tr] + [r.shape[1] for r in recvs])
    assert all(r.shape[1] % tr == 0 for r in recvs)
    starts = [sum(r.shape[1] for r in recvs[:i]) // tr for i in range(len(recvs))]
    counts = [r.shape[1] // tr for r in recvs]
    c1 = 1.0 / (1.0 - ADAM_B1 ** ADAM_STEP)
    c2 = 1.0 / (1.0 - ADAM_B2 ** ADAM_STEP)

    def body(*refs):
        recv_refs = refs[:len(recvs)]
        w_ref, m_ref, v_ref, g_ref, d_ref, nm_ref, nv_ref = refs[len(recvs):]
        g = None
        for recv_ref, start in zip(recv_refs, starts):
            part = recv_ref[0].astype(F32)
            for s in range(1, slots):
                part = part + recv_ref[s].astype(F32)
            g = part if g is None else jnp.where(pl.program_id(0) >= start, part, g)
        nm = ADAM_B1 * m_ref[...] + (1.0 - ADAM_B1) * g
        nv = ADAM_B2 * v_ref[...] + (1.0 - ADAM_B2) * (g * g)
        g_ref[...] = g
        nm_ref[...] = nm
        nv_ref[...] = nv
        d_ref[...] = -ADAM_LR * ((nm * c1) / (jnp.sqrt(nv * c2) + ADAM_EPS) + ADAM_WD * w_ref[...])

    blk = pl.BlockSpec((tr, c_total), lambda i: (i, 0))
    recv_specs = [pl.BlockSpec((slots, tr, c_total), lambda i, s=s, n=n: (0, jnp.clip(i - s, 0, n - 1), 0))
                  for s, n in zip(starts, counts)]
    return pl.pallas_call(
        body, grid=(r_total // tr,), in_specs=recv_specs + [blk, blk, blk],
        out_specs=[blk] * 4, out_shape=[_sds((r_total, c_total))] * 4, name=name,
        compiler_params=_cparams(("parallel",)))(*recvs, w, m, v)


def _me_and_peers():
    x, y, c = lax.axis_index("x"), lax.axis_index("y"), lax.axis_index("c")
    me = 4 * x + 2 * y + c
    peers = []
    for k in range(1, N_DEV):
        px = 1 - x if (k >> 2) & 1 else x
        py = 1 - y if (k >> 1) & 1 else y
        pc = 1 - c if k & 1 else c
        peers.append(((px, py, pc), 4 * px + 2 * py + pc))
    return me, peers


def _small_exchange(packed, reduce):
    r_total = packed.shape[0]

    def body(p_ref, out_ref, gath_ref, send_sems, recv_sems):
        me, peers = _me_and_peers()
        gath_ref[me] = p_ref[...]
        copies = []
        for k, (dev, _) in enumerate(peers):
            cp = pltpu.make_async_remote_copy(src_ref=p_ref, dst_ref=gath_ref.at[me], send_sem=send_sems.at[k],
                                              recv_sem=recv_sems.at[k], device_id=dev,
                                              device_id_type=pl.DeviceIdType.MESH)
            cp.start()
            copies.append(cp)
        for k, (dev, idx) in enumerate(peers):
            pltpu.make_async_remote_copy(src_ref=p_ref, dst_ref=gath_ref.at[idx], send_sem=send_sems.at[k],
                                         recv_sem=recv_sems.at[k], device_id=dev,
                                         device_id_type=pl.DeviceIdType.MESH).wait_recv()
        for cp in copies:
            cp.wait_send()
        if reduce:
            tot = gath_ref[0]
            for d in range(1, N_DEV):
                tot = tot + gath_ref[d]
            out_ref[...] = tot
        else:
            out_ref[...] = gath_ref[...]

    out_shape = _sds((r_total, 128)) if reduce else _sds((N_DEV, r_total, 128))
    return pl.pallas_call(
        body, in_specs=[pl.BlockSpec(memory_space=pltpu.VMEM)], out_specs=pl.BlockSpec(memory_space=pltpu.VMEM),
        out_shape=out_shape,
        scratch_shapes=[pltpu.VMEM((N_DEV, r_total, 128), F32), pltpu.SemaphoreType.DMA((N_DEV - 1,)),
                        pltpu.SemaphoreType.DMA((N_DEV - 1,))],
        name="small_allreduce" if reduce else "small_allgather")(packed)


def _slot(ref, axis, idx, size):
    sel = [slice(None)] * len(ref.shape)
    sel[axis] = idx if size is None else pl.ds(pl.multiple_of(idx * size, size), size)
    return ref.at[tuple(sel)]


def _big_exchange(srcs, dst_shapes, src_view, dst_view, name):
    n = len(srcs)

    def body(*refs):
        src_refs, dst_refs = refs[:n], refs[n:2 * n]
        send_sems, recv_sems, local_sems = refs[2 * n:]
        me, peers = _me_and_peers()
        local, remote = [], []
        for t in range(n):
            loc = pltpu.make_async_copy(src_view(t, src_refs[t], me), dst_view(t, dst_refs[t], me), local_sems.at[t])
            loc.start()
            local.append(loc)
            for k, (dev, idx) in enumerate(peers):
                cp = pltpu.make_async_remote_copy(
                    src_ref=src_view(t, src_refs[t], idx), dst_ref=dst_view(t, dst_refs[t], me),
                    send_sem=send_sems.at[t, k], recv_sem=recv_sems.at[t, k], device_id=dev,
                    device_id_type=pl.DeviceIdType.MESH)
                cp.start()
                remote.append(cp)
        for t in range(n):
            for k, (dev, idx) in enumerate(peers):
                pltpu.make_async_remote_copy(
                    src_ref=src_view(t, src_refs[t], me), dst_ref=dst_view(t, dst_refs[t], idx),
                    send_sem=send_sems.at[t, k], recv_sem=recv_sems.at[t, k], device_id=dev,
                    device_id_type=pl.DeviceIdType.MESH).wait_recv()
        for cp in remote:
            cp.wait_send()
        for cp in local:
            cp.wait()

    any_spec = pl.BlockSpec(memory_space=pl.ANY)
    return pl.pallas_call(
        body, in_specs=[any_spec] * n, out_specs=[any_spec] * n, out_shape=dst_shapes,
        scratch_shapes=[pltpu.SemaphoreType.DMA((n, N_DEV - 1)), pltpu.SemaphoreType.DMA((n, N_DEV - 1)),
                        pltpu.SemaphoreType.DMA((n,))],
        name=name)(*srcs)


BIG = {
    "a_w_in": (1, (2, 1024, A_IN)),
    "b_w_in": (1, (2, 1024, 1024)),
    "w_kv_shared": (0, (1024, 256)),
    "mem_w_kv": (1, (4, 1024, 512)),
    "w_o": (1, (4, 1024, 1024)),
    "mlp_w_up": (2, (4, 1024, 4096)),
    "mlp_w_down": (1, (4, 4096, 1024)),
}
BIG_NAMES = tuple(BIG)


def _gather_plan(names, shards):
    dst_shapes, axes, sizes = [], [], []
    for name, s in zip(names, shards):
        axis = BIG[name][0] - (len(BIG[name][1]) - s.ndim)
        dst_shapes.append(_sds(tuple(d * N_DEV if a == axis else d for a, d in enumerate(s.shape)), s.dtype))
        axes.append(axis)
        sizes.append(s.shape[axis])
    return dst_shapes, axes, sizes


def _gather_ride(names, shards):
    dst_shapes, axes, sizes = _gather_plan(names, shards)
    return _Ride(shards, dst_shapes, lambda t, ref, idx: ref, lambda t, ref, idx: _slot(ref, axes[t], idx, sizes[t]))


def _scatter_plan(names, grads):
    dst_shapes, axes, sizes = [], [], []
    for name, g in zip(names, grads):
        axis = BIG[name][0] - (len(BIG[name][1]) - g.ndim)
        shard = tuple(d // N_DEV if a == axis else d for a, d in enumerate(g.shape))
        axes.append(axis)
        sizes.append(shard[axis])
        dst_shapes.append(_sds((N_DEV,) + shard, g.dtype))
    return dst_shapes, axes, sizes


def _scatter_ride(names, grads):
    dst_shapes, axes, sizes = _scatter_plan(names, grads)
    return _Ride(grads, dst_shapes, lambda t, ref, idx: _slot(ref, axes[t], idx, sizes[t]),
                 lambda t, ref, idx: ref.at[idx])


def _allgather_weights(names, shards):
    dst_shapes, axes, sizes = _gather_plan(names, shards)
    n = len(shards)

    def body(*refs):
        src_refs, dst_refs = refs[:n], refs[n:2 * n]
        send_sems, recv_sems, local_sems = refs[2 * n:]
        x, y, c = lax.axis_index("x"), lax.axis_index("y"), lax.axis_index("c")
        sibling = (x, y, 1 - c)
        chips = [(1 - x, y), (x, 1 - y), (1 - x, 1 - y)]
        index = lambda px, py, pc: 4 * px + 2 * py + pc

        def copy(t, k, block, to, src=None):
            rows = _slot(dst_refs[t], axes[t], index(*block), sizes[t])
            return pltpu.make_async_remote_copy(
                src_ref=rows if src is None else src, dst_ref=rows, send_sem=send_sems.at[t, k],
                recv_sem=recv_sems.at[t, k], device_id=to, device_id_type=pl.DeviceIdType.MESH)

        started, local = [], []
        for t in range(n):
            mine = pltpu.make_async_copy(src_refs[t], _slot(dst_refs[t], axes[t], index(x, y, c), sizes[t]),
                                         local_sems.at[t])
            mine.start()
            local.append(mine)
            first = [copy(t, 0, (x, y, c), sibling, src=src_refs[t])]
            first += [copy(t, 1 + j, (x, y, c), (*chip, c), src=src_refs[t]) for j, chip in enumerate(chips)]
            for cp in first:
                cp.start()
            started += first
        for t in range(n):
            for j, chip in enumerate(chips):
                copy(t, 1 + j, (*chip, c), (x, y, c)).wait_recv()
                passed = copy(t, 4 + j, (*chip, c), sibling)
                passed.start()
                started.append(passed)
        for t in range(n):
            copy(t, 0, sibling, (x, y, c)).wait_recv()
            for j, chip in enumerate(chips):
                copy(t, 4 + j, (*chip, 1 - c), (x, y, c)).wait_recv()
        for cp in started:
            cp.wait_send()
        for cp in local:
            cp.wait()

    any_spec = pl.BlockSpec(memory_space=pl.ANY)
    return pl.pallas_call(
        body, in_specs=[any_spec] * n, out_specs=[any_spec] * n, out_shape=dst_shapes,
        scratch_shapes=[pltpu.SemaphoreType.DMA((n, N_DEV - 1)), pltpu.SemaphoreType.DMA((n, N_DEV - 1)),
                        pltpu.SemaphoreType.DMA((n,))],
        name="allgather_weights")(*shards)


def _scatter_grads(names, grads):
    dst_shapes, axes, sizes = _scatter_plan(names, grads)

    def src_view(t, ref, idx):
        return _slot(ref, axes[t], idx, sizes[t])

    def dst_view(t, ref, idx):
        return ref.at[idx]

    return _big_exchange(grads, dst_shapes, src_view, dst_view, "scatter_grads")


def _pad_row(vec, width=128):
    return jnp.pad(vec.astype(F32), (0, width - vec.shape[0])).reshape(1, width)


def _block_sizes(t_total):
    return dict(row=min(256, t_total), dn=min(512, t_total), swa=min(256, t_total), scan=min(256, t_total))


def _ln_apply(h, mix, g, b, tb):
    t_total, d = h.shape
    fwd = lambda pids, *a: _ln_fn(pids, *a) * 2
    return _block_fwd(fwd, [h, mix, g, b], [_rows(tb, d), _rows(tb, d), _whole((1, d)), _whole((1, d))],
                      [_sds((t_total, d)), _sds((t_total, d), BF16)], [_rows(tb, d), _rows(tb, d)],
                      (t_total // tb,), "ln_fwd")


def _ln_grad(h, mix, g, b, dy, tb):
    t_total, d = h.shape
    return _block_bwd(_ln_fn, [h, mix, g, b], [_rows(tb, d), _rows(tb, d), _whole((1, d)), _whole((1, d))],
                      [dy], [_rows(tb, d)], ["s", "s", "a", "a"],
                      [_sds((t_total, d)), _sds((t_total, d), BF16), _sds((1, d)), _sds((1, d))],
                      [_rows(tb, d), _rows(tb, d), _whole((1, d)), _whole((1, d))], (t_total // tb,), "ln_bwd")


def _memattn_specs(tb, qcol):
    return [pl.BlockSpec((tb, MEM_W), lambda i: (i, qcol)), _whole((MEM_W, 2 * MEM_W))]


def _act_epilogue(acc):
    r = jnp.maximum(acc, 0.0)
    return (r * r,)


def _dact_epilogue(acc, act):
    return (acc * (2.0 * jnp.sqrt(act.astype(F32))),)


def _add_epilogue(acc, other):
    return (acc + other,)


def _key(name, layer):
    if name == "w_kv_shared":
        return (name, None)
    return (name, layer - N_A if name == "b_w_in" else layer)


_PER_LAYER = ("mem_w_kv", "w_o", "mlp_w_up", "mlp_w_down")
GATHER_FIRST = [_key("a_w_in", 0)]
GATHER_ON_ROWA0 = [_key(n, 0) for n in _PER_LAYER]
GATHER_ON_DN1_0 = [_key("a_w_in", 1), _key("w_kv_shared", 1)] + [_key(n, 1) for n in _PER_LAYER]
GATHER_ON_DN1_1 = [_key("b_w_in", 2), _key("b_w_in", 3)] + [_key(n, l) for l in (2, 3) for n in _PER_LAYER]
SCATTER_ON_DN1_BWD_1 = GATHER_ON_DN1_1
SCATTER_ON_DN1_BWD_0 = GATHER_ON_DN1_0 + GATHER_ON_ROWA0
SCATTER_LAST = GATHER_FIRST
ALL_KEYS = GATHER_FIRST + GATHER_ON_ROWA0 + GATHER_ON_DN1_0 + GATHER_ON_DN1_1


def _local_step(x, mem, positions, target, ready, shards, small):
    t_total = x.shape[0]
    bs = _block_sizes(t_total)
    tb, tdn, tsw = bs["row"], bs["dn"], bs["swa"]
    nb = t_total // tb
    nbs = t_total // tsw

    inv_freq = ROPE_THETA ** (-jnp.arange(0, SWA_DH, 2, dtype=F32) / SWA_DH)
    ang = positions.astype(F32)[:, None] * inv_freq
    cos = jnp.tile(jnp.cos(ang), (1, 4))
    sin = jnp.tile(jnp.sin(ang), (1, 4))

    mem_b = mem.astype(BF16)
    ready = dict(ready)
    derived = {}

    def gather_ride(keys):
        names = [k[0] for k in keys]
        return names, (_gather_ride(names, [shards[k] for k in keys]) if shards is not None else None)

    def weight(name, l):
        key = _key(name, l)
        if key not in derived:
            w = ready[key]
            if name == "a_w_in":
                w = jnp.concatenate([w[0][:, :3072], w[0][:, 3084:], w[0][:, 3072:3084],
                                     jnp.zeros((D_MODEL, A_IN_PAD - A_IN), BF16)], axis=1)
            elif name == "w_kv_shared":
                w = jnp.concatenate([w[:, 64 * (i // 2):64 * (i // 2 + 1)] for i in range(8)], axis=1)
            else:
                w = w[0]
            derived[key] = w
        return derived[key]

    saved = []
    h, hb = x, x.astype(BF16)
    kr = vd_src = None
    for l in range(DEPTH):
        sv = dict(h=h, hb=hb)
        if l < N_A:
            proj = _matmul(hb, weight("a_w_in", l), "nn", [F32], "mm_proj_a", tn=1152)
            conv_w = small["a_conv_w"][l]
            c = _conv_fwd(proj, conv_w, tb)
            alog, dtb = _pad_row(small["a_A_log"][l]), _pad_row(small["a_dt_bias"][l])
            rowa_in = [c, proj, alog, dtb]
            rowa_specs = [_rows(tb, 3 * DN_W), _rows(tb, 128, 26), _whole((1, 128)), _whole((1, 128))]
            rowa_args = (_rowa_fn, rowa_in, rowa_specs, [_sds((t_total, DN_W))] * 5, [_rows(tb, DN_W)] * 5, (nb,))
            if shards is not None and l == 0:
                names, ride = gather_ride(GATHER_ON_ROWA0)
                (q, k, v, gcb, betab), got = _block_fwd(*rowa_args, "rowa_fwd_gather", ride=ride)
                ready.update(zip(GATHER_ON_ROWA0, got))
            else:
                q, k, v, gcb, betab = _block_fwd(*rowa_args, "rowa_fwd")
            hs = _head_spec(tdn)
            dn_grid = (DN_HEADS, t_total // tdn)
            full = _sds((t_total, DN_W))
            full_b = _sds((t_total, DN_W), BF16)
            dn1_out_shapes = [full, full_b, _sds((DN_HEADS, t_total, PAIR), BF16), full_b, full_b, full,
                              _sds((DN_HEADS, t_total, PAIR))]
            dn1_out_specs = [hs, hs, _intra_spec(tdn), hs, hs, hs, _intra_spec(tdn)]
            if shards is not None:
                keys = GATHER_ON_DN1_0 if l == 0 else GATHER_ON_DN1_1
                names, ride = gather_ride(keys)
                (u, w, intra, qd, kd, cd, tinv), got = _block_fwd(
                    _dn1_fn, [q, k, v, gcb, betab], [hs] * 5, dn1_out_shapes, dn1_out_specs, dn_grid,
                    "dn1_fwd_gather%d" % l, ride=ride)
                ready.update(zip(keys, got))
            else:
                u, w, intra, qd, kd, cd, tinv = _block_fwd(_dn1_fn, [q, k, v, gcb, betab], [hs] * 5, dn1_out_shapes,
                                                           dn1_out_specs, dn_grid, "dn1_fwd")
            o, states = _dn2_fwd(qd, kd, u, w, intra, cd, bs["scan"])
            nw = small["a_norm_w"][l].reshape(1, DN_D)
            post_in = [o, proj, nw]
            post_specs = [_rows(tb, DN_W), _rows(tb, DN_W, 3), _whole((1, DN_D))]
            (og,) = _block_fwd(_post_fn, post_in, post_specs, [_sds((t_total, DN_W), BF16)], [_rows(tb, DN_W)],
                               (nb,), "post_fwd")
            qm_col = 12
            sv.update(proj=proj, c=c, rowa_in=rowa_in, rowa_specs=rowa_specs, dn1_in=[q, k, v, gcb, betab, tinv],
                      dn2_in=[qd, kd, u, w, intra, cd], states=states, post_in=post_in, post_specs=post_specs,
                      conv_w=conv_w)
        else:
            jb = l - N_A
            proj = _matmul(hb, weight("b_w_in", l), "nn", [F32], "mm_proj_b")
            sinks = _pad_row(small["b_sinks"][jb])
            swa_in = [proj, cos, sin, kr, kr, vd_src, vd_src, sinks]
            swa_specs = [_rows(tsw, DN_W), _rows(tsw, 128), _rows(tsw, 128),
                         pl.BlockSpec((WINDOW, 256), lambda i: (jnp.maximum(i * (tsw // WINDOW) - 1, 0), 0)),
                         _rows(tsw, 256),
                         pl.BlockSpec((WINDOW, 256), lambda i: (jnp.maximum(i * (tsw // WINDOW) - 1, 0), 1)),
                         _rows(tsw, 256, 1), _whole((1, 128))]
            (og,) = _block_fwd(_swa_fn, swa_in, swa_specs, [_sds((t_total, DN_W), BF16)], [_rows(tsw, DN_W)],
                               (nbs,), "swa_fwd")
            qm_col = 3
            sv.update(proj=proj, swa_in=swa_in, swa_specs=swa_specs)
        kvm = _matmul(mem_b, weight("mem_w_kv", l), "nn", [F32], "mm_memkv", tm=256)
        mem_in = [proj, kvm]
        (mo,) = _block_fwd(_memattn_fn, mem_in, _memattn_specs(tb, qm_col), [_sds((t_total, MEM_W), BF16)],
                           [_rows(tb, MEM_W)], (nb,), "memattn_fwd")
        mixin = jnp.concatenate([og, mo], axis=1)
        g0, b0 = small["ln_g"][l, 0].reshape(1, -1), small["ln_b"][l, 0].reshape(1, -1)
        mix, h1, h1b = _matmul_ln(mixin, weight("w_o", l), h, g0, b0, "mm_wo_ln")
        act = _matmul(h1b, weight("mlp_w_up", l), "nn", [BF16], "mm_up", epi=_act_epilogue, tm=2048)
        g1, b1 = small["ln_g"][l, 1].reshape(1, -1), small["ln_b"][l, 1].reshape(1, -1)
        mlp, h2, h2b = _matmul_ln(act, weight("mlp_w_down", l), h1, g1, b1, "mm_down_ln")
        sv.update(kvm=kvm, mem_in=mem_in, qm_col=qm_col, mixin=mixin, mix=mix, ln0=(g0, b0), h1=h1, h1b=h1b,
                  act=act, mlp=mlp, ln1=(g1, b1))
        saved.append(sv)
        h, hb = h2, h2b
        if l == N_A - 1:
            kvd = _matmul(hb, weight("w_kv_shared", l), "nn", [F32], "mm_kvd")
            krope_in = [kvd, cos, sin]
            krope_specs = [_rows(tb, 256), _rows(tb, 128), _rows(tb, 128)]
            (kr,) = _block_fwd(_krope_fn, krope_in, krope_specs, [_sds((t_total, 256))], [_rows(tb, 256)], (nb,),
                               "krope_fwd")
            vd_src = kvd

    loss, dh = _loss_and_grad(h, target, tb)

    grads = {}

    def scatter_ride(keys):
        return _scatter_ride([k[0] for k in keys], [grads[k] for k in keys]) if shards is not None else None

    sg = dict(a_conv_w=[None] * N_A, a_A_log=[None] * N_A, a_dt_bias=[None] * N_A, a_norm_w=[None] * N_A,
              b_sinks=[None] * (DEPTH - N_A), ln_g=[[None, None] for _ in range(DEPTH)],
              ln_b=[[None, None] for _ in range(DEPTH)])
    dk_parts, dv_parts = [], []
    for l in reversed(range(DEPTH)):
        sv = saved[l]
        if l == N_A - 1:
            dkr = _halo_sum([p[0] for p in dk_parts], [p[1] for p in dk_parts], tsw)
            dvv = _halo_sum([p[0] for p in dv_parts], [p[1] for p in dv_parts], tsw)
            (dkraw,) = _block_bwd(_krope_fn, krope_in, krope_specs, [dkr], [_rows(tb, 256)], ["s", None, None],
                                  [_sds((t_total, 256), BF16)], [_rows(tb, 256)], (nb,), "krope_bwd")
            dkvd = jnp.concatenate([dkraw, dvv.astype(BF16)], axis=1)
            g_kvd = _matmul(saved[l + 1]["hb"], dkvd, "tn", [F32], "mm_dw_kvd", tm=1024, tn=512)
            dh = _matmul(dkvd, weight("w_kv_shared", l), "nt", [F32], "mm_dx_kvd", epi=_add_epilogue, extras=[dh],
                         tn=1024, tk=512)
            grads[_key("w_kv_shared", l)] = jnp.concatenate(
                [g_kvd[:, 128 * i:128 * i + 64] + g_kvd[:, 128 * i + 64:128 * (i + 1)] for i in range(4)],
                axis=1).astype(BF16)
        g1, b1 = sv["ln1"]
        dh1a, dmlp, dg1, db1 = _ln_grad(sv["h1"], sv["mlp"], g1, b1, dh, tb)
        dup = _matmul(dmlp, weight("mlp_w_down", l), "nt", [BF16], "mm_dact", epi=_dact_epilogue, extras=[sv["act"]],
                      tm=2048)
        grads[_key("mlp_w_down", l)] = _matmul(sv["act"], dmlp, "tn", [BF16], "mm_dw_down", tk=4096)[None]
        grads[_key("mlp_w_up", l)] = _matmul(sv["h1b"], dup, "tn", [BF16], "mm_dw_up", tk=4096)[None]
        dh1 = _matmul(dup, weight("mlp_w_up", l), "nt", [F32], "mm_dx_up", epi=_add_epilogue, extras=[dh1a], tk=2048)
        g0, b0 = sv["ln0"]
        dha, dmix, dg0, db0 = _ln_grad(sv["h"], sv["mix"], g0, b0, dh1, tb)
        sg["ln_g"][l] = [dg0, dg1]
        sg["ln_b"][l] = [db0, db1]
        grads[_key("w_o", l)] = _matmul(sv["mixin"], dmix, "tn", [BF16], "mm_dw_o", tk=4096)[None]
        dmixin = _matmul(dmix, weight("w_o", l), "nt", [F32], "mm_dx_o", tn=1024)
        dqm, dkvm = _block_bwd(_memattn_fn, sv["mem_in"], _memattn_specs(tb, sv["qm_col"]), [dmixin],
                               [_rows(tb, MEM_W, 3)], ["s", "a"],
                               [_sds((t_total, MEM_W), BF16), _sds((MEM_W, 2 * MEM_W))],
                               [_rows(tb, MEM_W), _whole((MEM_W, 2 * MEM_W))], (nb,), "memattn_bwd")
        grads[_key("mem_w_kv", l)] = _matmul(mem_b, dkvm.astype(BF16), "tn", [BF16], "mm_dw_memkv", tm=1024,
                                             tn=512)[None]
        if l < N_A:
            d_o, dz, dnw = _block_bwd(_post_fn, sv["post_in"], sv["post_specs"], [dmixin], [_rows(tb, DN_W)],
                                      ["s", "s", "a"],
                                      [_sds((t_total, DN_W)), _sds((t_total, DN_W), BF16), _sds((1, DN_D))],
                                      [_rows(tb, DN_W), _rows(tb, DN_W), _whole((1, DN_D))], (nb,), "post_bwd")
            sg["a_norm_w"][l] = dnw
            dqd, dkd, du, dw, da, dcd = _dn2_bwd(*sv["dn2_in"], sv["states"], d_o, bs["scan"])
            hs = _head_spec(tdn)
            full = _sds((t_total, DN_W))
            dn1_bwd_args = (_dn1_fn_known, sv["dn1_in"], [hs] * 5 + [_intra_spec(tdn)], [du, dw, da, dqd, dkd, dcd],
                            [hs, hs, _intra_spec(tdn), hs, hs, hs], ["s"] * 5 + [None], [full] * 5, [hs] * 5,
                            (DN_HEADS, t_total // tdn))
            if shards is not None:
                keys = SCATTER_ON_DN1_BWD_1 if l == N_A - 1 else SCATTER_ON_DN1_BWD_0
                (dq, dk, dv, dgc, dbeta), got = _block_bwd(*dn1_bwd_args, "dn1_bwd_scatter%d" % l,
                                                           ride=scatter_ride(keys))
                grads.update(zip(keys, got))
            else:
                dq, dk, dv, dgc, dbeta = _block_bwd(*dn1_bwd_args, "dn1_bwd")
            dc, dab, dalog, ddtb = _block_bwd(
                _rowa_fn, sv["rowa_in"], sv["rowa_specs"], [dq, dk, dv, dgc, dbeta], [_rows(tb, DN_W)] * 5,
                ["s", "s", "a", "a"],
                [_sds((t_total, 3 * DN_W)), _sds((t_total, 128), BF16), _sds((1, 128)), _sds((1, 128))],
                [_rows(tb, 3 * DN_W), _rows(tb, 128), _whole((1, 128)), _whole((1, 128))], (nb,), "rowa_bwd")
            sg["a_A_log"][l] = dalog[0, :DN_HEADS]
            sg["a_dt_bias"][l] = ddtb[0, :DN_HEADS]
            dx, dconv = _conv_bwd(dc, sv["proj"], sv["conv_w"], tb)
            sg["a_conv_w"][l] = dconv
            dproj = jnp.concatenate([dx, dz, dqm, dab], axis=1)
            g_in = _matmul(sv["hb"], dproj, "tn", [BF16], "mm_dw_a", tm=1024, tn=1152, tk=2048)
            grads[_key("a_w_in", l)] = jnp.concatenate([g_in[:, :3072], g_in[:, 3328:3340], g_in[:, 3072:3328]],
                                                       axis=1)[None]
            dh = _matmul(dproj, weight("a_w_in", l), "nt", [F32], "mm_dx_a", epi=_add_epilogue, extras=[dha], tk=1152)
        else:
            jb = l - N_A
            swa_kinds = ["s", None, None, "s", "s", "s", "s", "a"]
            halo_spec = pl.BlockSpec((None, WINDOW, 256), lambda i: (i, 0, 0))
            dq, dkh, dkc, dvh, dvc, dsink = _block_bwd(
                _swa_fn, sv["swa_in"], sv["swa_specs"], [dmixin], [_rows(tsw, DN_W)], swa_kinds,
                [_sds((t_total, DN_W), BF16), _sds((nbs, WINDOW, 256)), _sds((t_total, 256)),
                 _sds((nbs, WINDOW, 256)), _sds((t_total, 256)), _sds((1, 128))],
                [_rows(tsw, DN_W), halo_spec, _rows(tsw, 256), halo_spec, _rows(tsw, 256), _whole((1, 128))],
                (nbs,), "swa_bwd")
            sg["b_sinks"][jb] = dsink[0, :SWA_HEADS]
            dk_parts.append((dkc, dkh))
            dv_parts.append((dvc, dvh))
            dproj = jnp.concatenate([dq, dqm], axis=1)
            grads[_key("b_w_in", l)] = _matmul(sv["hb"], dproj, "tn", [BF16], "mm_dw_b", tk=4096)[None]
            dh = _matmul(dproj, weight("b_w_in", l), "nt", [F32], "mm_dx_b", epi=_add_epilogue, extras=[dha], tn=1024)

    small_grads = dict(
        a_conv_w=jnp.stack(sg["a_conv_w"]), a_A_log=jnp.stack(sg["a_A_log"]), a_dt_bias=jnp.stack(sg["a_dt_bias"]),
        a_norm_w=jnp.concatenate(sg["a_norm_w"], axis=0), b_sinks=jnp.stack(sg["b_sinks"]),
        ln_g=jnp.stack([jnp.concatenate(p, axis=0) for p in sg["ln_g"]]),
        ln_b=jnp.stack([jnp.concatenate(p, axis=0) for p in sg["ln_b"]]))
    return loss, dh, grads, small_grads


def _pack(arrays, rows):
    flat = []
    for a in arrays:
        v = a.astype(F32).reshape(-1)
        flat.append(jnp.pad(v, (0, (-v.shape[0]) % 128)))
    flat = jnp.concatenate(flat)
    return jnp.pad(flat, (0, rows * 128 - flat.shape[0])).reshape(rows, 128)


def _unpack(slab, shapes):
    flat = slab.reshape(slab.shape[:-2] + (-1,))
    out, off = [], 0
    for s in shapes:
        n = math.prod(s)
        out.append(flat[..., off:off + n].reshape(slab.shape[:-2] + tuple(s)))
        off += n + (-n) % 128
    return out


def _rows_for(shapes):
    rows = sum((math.prod(s) + 127) // 128 for s in shapes)
    return rows + (-rows) % 8


SMALL_NAMES = ("a_conv_w", "a_A_log", "a_dt_bias", "a_norm_w", "b_sinks", "ln_g", "ln_b")
SMALL_SHARDED = {"a_conv_w": 2, "ln_g": 2, "ln_b": 2}
SMALL_FULL = {"a_conv_w": (2, 4, 2304), "a_A_log": (2, 6), "a_dt_bias": (2, 6), "a_norm_w": (2, 128),
              "b_sinks": (2, 12), "ln_g": (4, 2, 1024), "ln_b": (4, 2, 1024)}


def kernel(x, mem, positions, a_w_in, a_conv_w, a_A_log, a_dt_bias, a_norm_w, b_w_in, b_sinks, w_kv_shared, mem_w_kv, w_o, mlp_w_up, mlp_w_down, ln_g, ln_b, loss_target, m_a_w_in, m_a_conv_w, m_a_A_log, m_a_dt_bias, m_a_norm_w, m_b_w_in, m_b_sinks, m_w_kv_shared, m_mem_w_kv, m_w_o, m_mlp_w_up, m_mlp_w_down, m_ln_g, m_ln_b, v_a_w_in, v_a_conv_w, v_a_A_log, v_a_dt_bias, v_a_norm_w, v_b_w_in, v_b_sinks, v_w_kv_shared, v_mem_w_kv, v_w_o, v_mlp_w_up, v_mlp_w_down, v_ln_g, v_ln_b):
    params = dict(a_w_in=a_w_in, a_conv_w=a_conv_w, a_A_log=a_A_log, a_dt_bias=a_dt_bias, a_norm_w=a_norm_w,
                  b_w_in=b_w_in, b_sinks=b_sinks, w_kv_shared=w_kv_shared, mem_w_kv=mem_w_kv, w_o=w_o,
                  mlp_w_up=mlp_w_up, mlp_w_down=mlp_w_down, ln_g=ln_g, ln_b=ln_b)
    mom = dict(a_w_in=m_a_w_in, a_conv_w=m_a_conv_w, a_A_log=m_a_A_log, a_dt_bias=m_a_dt_bias, a_norm_w=m_a_norm_w,
               b_w_in=m_b_w_in, b_sinks=m_b_sinks, w_kv_shared=m_w_kv_shared, mem_w_kv=m_mem_w_kv, w_o=m_w_o,
               mlp_w_up=m_mlp_w_up, mlp_w_down=m_mlp_w_down, ln_g=m_ln_g, ln_b=m_ln_b)
    var = dict(a_w_in=v_a_w_in, a_conv_w=v_a_conv_w, a_A_log=v_a_A_log, a_dt_bias=v_a_dt_bias, a_norm_w=v_a_norm_w,
               b_w_in=v_b_w_in, b_sinks=v_b_sinks, w_kv_shared=v_w_kv_shared, mem_w_kv=v_mem_w_kv, w_o=v_w_o,
               mlp_w_up=v_mlp_w_up, mlp_w_down=v_mlp_w_down, ln_g=v_ln_g, ln_b=v_ln_b)
    me = 4 * lax.axis_index("x") + 2 * lax.axis_index("y") + lax.axis_index("c")

    shards = {(n, i): (params[n] if i is None else params[n][i:i + 1]).astype(BF16) for n, i in ALL_KEYS}
    first = [k[0] for k in GATHER_FIRST]
    ready = dict(zip(GATHER_FIRST, _allgather_weights(first, [shards[k] for k in GATHER_FIRST])))
    sharded_names = [n for n in SMALL_NAMES if n in SMALL_SHARDED]
    shard_shapes = [params[n].shape for n in sharded_names]
    gathered = _small_exchange(_pack([params[n] for n in sharded_names], _rows_for(shard_shapes)), reduce=False)
    small = {n: params[n] for n in SMALL_NAMES if n not in SMALL_SHARDED}
    for n, g in zip(sharded_names, _unpack(gathered, shard_shapes)):
        small[n] = jnp.moveaxis(g, 0, 2).reshape(SMALL_FULL[n])

    loss, dx, recv, small_grads = _local_step(x[0], mem[0], positions[0], loss_target[0], ready, shards, small)
    loss = lax.psum(loss, ("x", "y", "c"))
    last = [k[0] for k in SCATTER_LAST]
    recv.update(zip(SCATTER_LAST, _scatter_grads(last, [recv[k] for k in SCATTER_LAST])))
    out = {}
    for n in BIG_NAMES:
        shp = params[n].shape
        rows = math.prod(shp[:-1])
        recvs = [recv[k].reshape(N_DEV, -1, shp[-1]) for k in sorted(k for k in ALL_KEYS if k[0] == n)]
        res = _adamw(recvs, params[n].reshape(rows, shp[-1]), mom[n].reshape(rows, shp[-1]),
                     var[n].reshape(rows, shp[-1]), 32, "adamw_" + n)
        out[n] = [t.reshape(shp) for t in res]
    full_shapes = [SMALL_FULL[n] for n in SMALL_NAMES]
    summed = _small_exchange(_pack([small_grads[n] for n in SMALL_NAMES], _rows_for(full_shapes)), reduce=True)
    local_g = []
    for n, g in zip(SMALL_NAMES, _unpack(summed, full_shapes)):
        if n in SMALL_SHARDED:
            size = params[n].shape[2]
            g = lax.dynamic_slice_in_dim(g, me * size, size, axis=2)
        local_g.append(g)
    local_shapes = [params[n].shape for n in SMALL_NAMES]
    rows = _rows_for(local_shapes)
    res = _adamw([_pack(local_g, rows)[None]], _pack([params[n] for n in SMALL_NAMES], rows),
                 _pack([mom[n] for n in SMALL_NAMES], rows), _pack([var[n] for n in SMALL_NAMES], rows), rows,
                 "adamw_small")
    unpacked = [_unpack(t, local_shapes) for t in res]
    for i, n in enumerate(SMALL_NAMES):
        out[n] = [unpacked[k][i] for k in range(4)]

    order = ("a_w_in", "a_conv_w", "a_A_log", "a_dt_bias", "a_norm_w", "b_w_in", "b_sinks", "w_kv_shared",
             "mem_w_kv", "w_o", "mlp_w_up", "mlp_w_down", "ln_g", "ln_b")
    return (loss, dx[None], *[out[n][0] for n in order], *[out[n][1] for n in order],
            *[out[n][2] for n in order], *[out[n][3] for n in order])
```

```python
import functools
import math

import jax
import jax.numpy as jnp
from jax import lax
from jax.experimental import pallas as pl
from jax.experimental.pallas import tpu as pltpu

F32 = jnp.float32
BF16 = jnp.bfloat16

D_MODEL = 1024
DEPTH = 4
N_A = 2
MEM_HEADS = 4
MEM_DH = 64
MEM_W = 256
DN_HEADS = 6
DN_D = 128
DN_W = 768
CHUNK = 64
SWA_DH = 64
SWA_HEADS = 12
WINDOW = 128
ROPE_THETA = 10000.0
LN_EPS = 1e-5
NORM_EPS = 1e-6
DN_ALPHA = (2.0 * DEPTH) ** 0.25
A_IN = 3340
A_IN_PAD = 3456
N_DEV = 8

ADAM_LR = 0.001
ADAM_B1 = 0.9
ADAM_B2 = 0.999
ADAM_EPS = 1e-08
ADAM_WD = 0.01
ADAM_STEP = 10

VMEM_LIMIT = 52 * 1024 * 1024
NEG_BIG = -1e30


def _cparams(sem):
    return pltpu.CompilerParams(dimension_semantics=sem, vmem_limit_bytes=VMEM_LIMIT)


_CONTRACT = {"nn": (1, 0), "nt": (1, 1), "tn": (0, 0)}


def _raw_mm(a, b, mode, prec):
    ca, cb = _CONTRACT[mode]
    dims = (((ca,), (cb,)), ((), ()))
    dot = lambda p, q: lax.dot_general(p, q, dims, preferred_element_type=F32)
    if prec == "bf16":
        return dot(a.astype(BF16), b.astype(BF16))
    a, b = a.astype(F32), b.astype(F32)
    a_hi, b_hi = a.astype(BF16), b.astype(BF16)
    if prec == "sela":
        return dot(a_hi, b_hi) + dot(a_hi, (b - b_hi.astype(F32)).astype(BF16))
    a_lo = (a - a_hi.astype(F32)).astype(BF16)
    if prec == "selb":
        return dot(a_hi, b_hi) + dot(a_lo, b_hi)
    b_lo = (b - b_hi.astype(F32)).astype(BF16)
    return dot(a_hi, b_hi) + (dot(a_hi, b_lo) + dot(a_lo, b_hi))


@functools.partial(jax.custom_vjp, nondiff_argnums=(2, 3))
def mm(a, b, mode, prec):
    return _raw_mm(a, b, mode, prec)


def _mm_fwd(a, b, mode, prec):
    return _raw_mm(a, b, mode, prec), (a, b)


def _mm_bwd(mode, prec, res, ct):
    a, b = res
    if prec == "sela":
        pa, pb = "f32", {"nn": "sela", "nt": "selb", "tn": "sela"}[mode]
    elif prec == "selb":
        pa, pb = {"nn": "selb", "nt": "selb", "tn": "sela"}[mode], "f32"
    else:
        pa = pb = prec
    if mode == "nn":
        return mm(ct, b, "nt", pa), mm(a, ct, "tn", pb)
    if mode == "nt":
        return mm(ct, b, "nn", pa), mm(ct, a, "tn", pb)
    return mm(b, ct, "nt", pa), mm(a, ct, "nn", pb)


mm.defvjp(_mm_fwd, _mm_bwd)


@jax.custom_vjp
def _softplus(x):
    y = jnp.exp(-jnp.abs(x))
    log1p_y = jnp.where(y < 1e-2, y * (1.0 - y * (0.5 - y * (1.0 / 3.0))), jnp.log(1.0 + y))
    return jnp.maximum(x, 0.0) + log1p_y


def _softplus_fwd(x):
    return _softplus(x), x


def _softplus_bwd(x, ct):
    return (ct * jax.nn.sigmoid(x),)


_softplus.defvjp(_softplus_fwd, _softplus_bwd)


def _iota(shape, dim):
    return lax.broadcasted_iota(jnp.int32, shape, dim)


class _Ride:
    def __init__(self, srcs, dst_shapes, src_view, dst_view):
        self.srcs, self.dst_shapes, self.src_view, self.dst_view = list(srcs), list(dst_shapes), src_view, dst_view
        self.n = len(self.srcs)
        self.any_specs = [pl.BlockSpec(memory_space=pl.ANY)] * self.n
        self.scratch = [pltpu.SemaphoreType.DMA((self.n, N_DEV - 1)), pltpu.SemaphoreType.DMA((self.n, N_DEV - 1)),
                        pltpu.SemaphoreType.DMA((self.n,))]

    def copies(self, src_refs, dst_refs, sems):
        send_sems, recv_sems, local_sems = sems
        me, peers = _me_and_peers()
        local, out, inc = [], [], []
        for t in range(self.n):
            local.append(pltpu.make_async_copy(self.src_view(t, src_refs[t], me), self.dst_view(t, dst_refs[t], me),
                                               local_sems.at[t]))
            for k, (dev, idx) in enumerate(peers):
                mk = lambda s, d: pltpu.make_async_remote_copy(
                    src_ref=s, dst_ref=d, send_sem=send_sems.at[t, k], recv_sem=recv_sems.at[t, k], device_id=dev,
                    device_id_type=pl.DeviceIdType.MESH)
                out.append(mk(self.src_view(t, src_refs[t], idx), self.dst_view(t, dst_refs[t], me)))
                inc.append(mk(self.src_view(t, src_refs[t], me), self.dst_view(t, dst_refs[t], idx)))
        return local, out, inc

    def start(self, grid, src_refs, dst_refs, sems):
        first = pl.program_id(0) == 0
        for a in range(1, len(grid)):
            first = jnp.logical_and(first, pl.program_id(a) == 0)

        @pl.when(first)
        def _():
            local, out, _ = self.copies(src_refs, dst_refs, sems)
            for cp in local + out:
                cp.start()

    def finish(self, grid, src_refs, dst_refs, sems):
        last = pl.program_id(0) == grid[0] - 1
        for a in range(1, len(grid)):
            last = jnp.logical_and(last, pl.program_id(a) == grid[a] - 1)

        @pl.when(last)
        def _():
            local, out, inc = self.copies(src_refs, dst_refs, sems)
            for cp in inc:
                cp.wait_recv()
            for cp in out:
                cp.wait_send()
            for cp in local:
                cp.wait()


def _block_fwd(fn, ins, in_specs, out_shapes, out_specs, grid, name, ride=None):
    n_in, n_out = len(ins), len(out_shapes)
    n_ride = ride.n if ride else 0

    def body(*refs):
        pids = tuple(pl.program_id(a) for a in range(len(grid)))
        ride_refs = (refs[n_in:n_in + n_ride], refs[n_in + n_ride + n_out:n_in + 2 * n_ride + n_out],
                     refs[n_in + 2 * n_ride + n_out:])
        if ride:
            ride.start(grid, *ride_refs)
        vals = [r[...].astype(F32) for r in refs[:n_in]]
        outs = fn(pids, *vals)
        for r, o in zip(refs[n_in + n_ride:n_in + n_ride + n_out], outs):
            r[...] = o.astype(r.dtype)
        if ride:
            ride.finish(grid, *ride_refs)

    if not ride:
        return pl.pallas_call(
            body, grid=grid, in_specs=in_specs, out_specs=out_specs, out_shape=out_shapes, name=name,
            compiler_params=_cparams(("parallel",) * len(grid)))(*ins)
    res = pl.pallas_call(
        body, grid=grid, in_specs=list(in_specs) + ride.any_specs, out_specs=list(out_specs) + ride.any_specs,
        out_shape=list(out_shapes) + ride.dst_shapes, scratch_shapes=ride.scratch, name=name,
        compiler_params=_cparams(("arbitrary",) * len(grid)))(*ins, *ride.srcs)
    return res[:n_out], res[n_out:]


def _block_bwd(fn, ins, in_specs, cts, ct_specs, kinds, g_shapes, g_specs, grid, name, ride=None):
    n_in, n_ct, n_g = len(ins), len(cts), len(g_shapes)
    n_ride = ride.n if ride else 0
    didx = [i for i, k in enumerate(kinds) if k]

    def body(*refs):
        in_refs, ct_refs = refs[:n_in], refs[n_in:n_in + n_ct]
        base = n_in + n_ct
        g_refs = refs[base + n_ride:base + n_ride + n_g]
        ride_refs = (refs[base:base + n_ride], refs[base + n_ride + n_g:base + 2 * n_ride + n_g],
                     refs[base + 2 * n_ride + n_g:])
        if ride:
            ride.start(grid, *ride_refs)
        pids = tuple(pl.program_id(a) for a in range(len(grid)))
        vals = [r[...].astype(F32) for r in in_refs]

        def f(*dvals):
            full = list(vals)
            for i, v in zip(didx, dvals):
                full[i] = v
            return tuple(fn(pids, *full))

        _, vjp = jax.vjp(f, *[vals[i] for i in didx])
        gs = vjp(tuple(r[...].astype(F32) for r in ct_refs))
        first = pids[0] == 0
        for p in pids[1:]:
            first = jnp.logical_and(first, p == 0)
        for i, g, r in zip(didx, gs, g_refs):
            if kinds[i] == "s":
                r[...] = g.astype(r.dtype)
            else:
                @pl.when(first)
                def _(r=r):
                    r[...] = jnp.zeros(r.shape, r.dtype)

                r[...] += g.astype(r.dtype)
        if ride:
            ride.finish(grid, *ride_refs)

    sem = ("arbitrary",) * len(grid) if "a" in kinds or ride else ("parallel",) * len(grid)
    if not ride:
        return pl.pallas_call(
            body, grid=grid, in_specs=list(in_specs) + list(ct_specs), out_specs=g_specs, out_shape=g_shapes,
            name=name, compiler_params=_cparams(sem))(*ins, *cts)
    res = pl.pallas_call(
        body, grid=grid, in_specs=list(in_specs) + list(ct_specs) + ride.any_specs,
        out_specs=list(g_specs) + ride.any_specs, out_shape=list(g_shapes) + ride.dst_shapes,
        scratch_shapes=ride.scratch, name=name, compiler_params=_cparams(sem))(*ins, *cts, *ride.srcs)
    return res[:n_g], res[n_g:]


def _rows(tb, width, col=0):
    return pl.BlockSpec((tb, width), lambda i, col=col: (i, col))


def _whole(shape):
    return pl.BlockSpec(shape, lambda *_: (0,) * len(shape))


def _sds(shape, dtype=F32):
    return jax.ShapeDtypeStruct(shape, dtype)


def _matmul(a, b, mode, out_dtypes, name, epi=None, extras=(), tm=1024, tn=1024, tk=1024,
            b_spec=None, n_total=None, out_specs=None, out_shapes=None):
    if mode == "tn":
        k_total, m_total = a.shape
    else:
        m_total, k_total = a.shape
    if n_total is None:
        n_total = b.shape[0] if mode == "nt" else b.shape[1]
    tm, tn, tk = min(tm, m_total), min(tn, n_total), min(tk, k_total)
    assert m_total % tm == 0 and n_total % tn == 0 and k_total % tk == 0, (name, a.shape, b.shape)
    grid = (m_total // tm, n_total // tn, k_total // tk)
    nk = grid[2]
    if mode == "tn":
        a_spec = pl.BlockSpec((tk, tm), lambda i, j, k: (k, i))
    else:
        a_spec = pl.BlockSpec((tm, tk), lambda i, j, k: (i, k))
    if b_spec is None:
        if mode == "nt":
            b_spec = pl.BlockSpec((tn, tk), lambda i, j, k: (j, k))
        else:
            b_spec = pl.BlockSpec((tk, tn), lambda i, j, k: (k, j))
    tile = pl.BlockSpec((tm, tn), lambda i, j, k: (i, j))
    n_ex, n_out = len(extras), len(out_dtypes)
    ca, cb = _CONTRACT[mode]
    dims = (((ca,), (cb,)), ((), ()))

    chunk = 256
    n_chunks = tm // chunk if (epi is not None and mode != "tn" and tm % chunk == 0) else 1

    def body(*refs):
        a_ref, b_ref = refs[:2]
        ex_refs = refs[2:2 + n_ex]
        out_refs = refs[2 + n_ex:2 + n_ex + n_out]
        acc = refs[-1] if nk > 1 else None
        full = lambda: lax.dot_general(a_ref[...], b_ref[...], dims, preferred_element_type=F32)

        def last_step():
            for c in range(n_chunks):
                rows = pl.ds(c * chunk, chunk) if n_chunks > 1 else slice(None)
                val = (lax.dot_general(a_ref[rows, :], b_ref[...], dims, preferred_element_type=F32)
                       if n_chunks > 1 else full())
                if acc is not None:
                    val = val + acc[rows, :]
                res = epi(val, *[e[rows, :] for e in ex_refs]) if epi is not None else (val,)
                for r, o in zip(out_refs, res):
                    r[rows, :] = o.astype(r.dtype)

        if nk == 1:
            last_step()
        else:
            k = pl.program_id(2)

            @pl.when(k == 0)
            def _():
                acc[...] = full()

            if nk > 2:
                @pl.when(jnp.logical_and(k > 0, k < nk - 1))
                def _():
                    acc[...] += full()

            @pl.when(k == nk - 1)
            def _():
                last_step()

    if out_shapes is None:
        out_shapes = [_sds((m_total, n_total), d) for d in out_dtypes]
        out_specs = [tile] * n_out
    outs = pl.pallas_call(
        body, grid=grid, in_specs=[a_spec, b_spec] + [tile] * n_ex, out_specs=out_specs, out_shape=out_shapes,
        scratch_shapes=[pltpu.VMEM((tm, tn), F32)] if nk > 1 else [], name=name,
        compiler_params=_cparams(("parallel", "parallel", "arbitrary")))(a, b, *extras)
    return outs if n_out > 1 else outs[0]


def _matmul_ln(a, b, h, g, beta, name, tm=1024, tk=2048, chunk=256):
    m_total, k_total = a.shape
    n = b.shape[1]
    tm, tk = min(tm, m_total), min(tk, k_total)
    chunk = min(chunk, tm)
    assert m_total % tm == 0 and k_total % tk == 0 and tm % chunk == 0, (name, a.shape, b.shape)
    nk = k_total // tk
    dims = (((1,), (0,)), ((), ()))

    def body(a_ref, b_ref, h_ref, g_ref, beta_ref, mix_ref, y_ref, yb_ref, *acc):
        k = pl.program_id(1)

        def finish():
            for c in range(tm // chunk):
                rows = pl.ds(chunk * c, chunk)
                val = lax.dot_general(a_ref[rows, :], b_ref[...], dims, preferred_element_type=F32)
                if nk > 1:
                    val = val + acc[0][rows, :]
                mix_ref[rows, :] = val
                (y,) = _ln_fn(None, h_ref[rows, :], val, g_ref[...], beta_ref[...])
                y_ref[rows, :] = y
                yb_ref[rows, :] = y.astype(BF16)

        if nk == 1:
            finish()
        else:
            part = lambda: lax.dot_general(a_ref[...], b_ref[...], dims, preferred_element_type=F32)

            @pl.when(k == 0)
            def _():
                acc[0][...] = part()

            if nk > 2:
                @pl.when(jnp.logical_and(k > 0, k < nk - 1))
                def _():
                    acc[0][...] += part()

            @pl.when(k == nk - 1)
            def _():
                finish()

    tile = pl.BlockSpec((tm, n), lambda i, k: (i, 0))
    row = pl.BlockSpec((1, n), lambda i, k: (0, 0))
    return pl.pallas_call(
        body, grid=(m_total // tm, nk),
        in_specs=[pl.BlockSpec((tm, tk), lambda i, k: (i, k)), pl.BlockSpec((tk, n), lambda i, k: (k, 0)), tile, row,
                  row],
        out_specs=[tile, tile, tile],
        out_shape=[_sds((m_total, n)), _sds((m_total, n)), _sds((m_total, n), BF16)],
        scratch_shapes=[pltpu.VMEM((tm, n), F32)] if nk > 1 else [], name=name,
        compiler_params=_cparams(("parallel", "arbitrary")))(a, b, h, g, beta)


def _silu(x):
    return x * jax.nn.sigmoid(x)


def _rowa_fn(pids, c, ab, alog, dtb):
    tb = c.shape[0]
    s = _silu(c)
    qs, ks = [], []
    for h in range(DN_HEADS):
        qh = s[:, DN_D * h:DN_D * (h + 1)]
        qs.append(qh * lax.rsqrt(jnp.sum(qh * qh, axis=-1, keepdims=True) + NORM_EPS) * (DN_D ** -0.5))
        kh = s[:, DN_W + DN_D * h:DN_W + DN_D * (h + 1)]
        ks.append(kh * lax.rsqrt(jnp.sum(kh * kh, axis=-1, keepdims=True) + NORM_EPS))
    q = jnp.concatenate(qs, axis=1)
    k = jnp.concatenate(ks, axis=1)
    v = s[:, 2 * DN_W:3 * DN_W]
    g128 = -jnp.exp(alog) * _softplus(ab + dtb)
    b128 = jax.nn.sigmoid(ab)
    r, cc = _iota((tb, tb), 0), _iota((tb, tb), 1)
    tri = jnp.where(((r >> 6) == (cc >> 6)) & (r >= cc), 1.0, 0.0)
    gc128 = mm(tri, g128, "nn", "sela")
    lane, col = _iota((128, DN_W), 0), _iota((128, DN_W), 1)
    exp_a = jnp.where(lane == (col >> 7), 1.0, 0.0)
    exp_b = jnp.where(lane == (col >> 7) + DN_HEADS, 1.0, 0.0)
    return q, k, v, mm(gc128, exp_a, "nn", "selb"), mm(b128, exp_b, "nn", "selb")


def _tri_inv_raw(lows, block):
    n = lows[0].shape[0]
    r, c = _iota((n, n), 0), _iota((n, n), 1)
    lg = 0
    xs = None
    while (1 << lg) < block:
        off = ((r >> (lg + 1)) == (c >> (lg + 1))) & (((r >> lg) & 1) == 1) & (((c >> lg) & 1) == 0)
        cblks = [jnp.where(off, low, 0.0) for low in lows]
        if xs is None:
            xs = [jnp.where(r == c, 1.0, 0.0) - cb for cb in cblks]
        else:
            ys = [mm(cb, x, "nn", "f32") for cb, x in zip(cblks, xs)]
            xs = [x - mm(x, y, "nn", "f32") for x, y in zip(xs, ys)]
        lg += 1
    return tuple(xs)


def _tri_inv_cotangent(block, xs, cts):
    n = xs[0].shape[0]
    r, c = _iota((n, n), 0), _iota((n, n), 1)
    shift = block.bit_length() - 1
    keep = ((r >> shift) == (c >> shift)) & (r > c)
    gs = [mm(x, ct, "tn", "f32") for x, ct in zip(xs, cts)]
    gs = [mm(g, x, "nt", "f32") for g, x in zip(gs, xs)]
    return tuple(jnp.where(keep, -g, 0.0) for g in gs)


@functools.partial(jax.custom_vjp, nondiff_argnums=(1,))
def _tri_inv(lows, block):
    return _tri_inv_raw(lows, block)


def _tri_inv_fwd(lows, block):
    xs = _tri_inv_raw(lows, block)
    return xs, xs


def _tri_inv_bwd(block, xs, cts):
    return (_tri_inv_cotangent(block, xs, cts),)


_tri_inv.defvjp(_tri_inv_fwd, _tri_inv_bwd)


@functools.partial(jax.custom_vjp, nondiff_argnums=(2,))
def _tri_inv_known(lows, known, block):
    return known


def _tri_inv_known_fwd(lows, known, block):
    return known, known


def _tri_inv_known_bwd(block, xs, cts):
    return _tri_inv_cotangent(block, xs, cts), tuple(jnp.zeros_like(x) for x in xs)


_tri_inv_known.defvjp(_tri_inv_known_fwd, _tri_inv_known_bwd)


PAIR = 2 * CHUNK


def _dn1_pairs(q, k, v, gc, beta, tinv_known=None):
    assert PAIR == DN_D
    n = PAIR
    pairs = range(q.shape[0] // n)
    cut = lambda t: [t[n * j:n * (j + 1)] for j in pairs]
    q, k, v, gc, beta = cut(q), cut(k), cut(v), cut(gc), cut(beta)
    onehot = jnp.where(_iota((n, DN_D), 1) == 0, 1.0, 0.0)
    r, c = _iota((n, n), 0), _iota((n, n), 1)
    same = (r >> 6) == (c >> 6)
    incl, strict = same & (r >= c), same & (r > c)
    row = _iota((n, DN_D), 0)
    eg = [jnp.exp(g) for g in gc]
    kb = [k[j] * beta[j] for j in pairs]
    g_row = [mm(onehot, g, "nt", "sela") for g in gc]
    kk = [mm(kb[j], k[j], "nt", "bf16") for j in pairs]
    qk = [mm(q[j], k[j], "nt", "bf16") for j in pairs]
    decay = [jnp.exp(jnp.where(incl, gc[j] - g_row[j], NEG_BIG)) for j in pairs]
    low = tuple(jnp.where(strict, kk[j] * decay[j], 0.0) for j in pairs)
    if tinv_known is None:
        tinv = _tri_inv(low, CHUNK)
    else:
        tinv = _tri_inv_known(low, tuple(cut(tinv_known)), CHUNK)
    uw = [mm(tinv[j], jnp.concatenate([v[j] * beta[j], kb[j] * eg[j]], axis=1), "nn", "f32") for j in pairs]
    intra = [jnp.where(incl, qk[j] * decay[j], 0.0) for j in pairs]
    g_last = []
    for g in gc:
        last0 = jnp.sum(jnp.where(row == CHUNK - 1, g, 0.0), axis=0, keepdims=True)
        last1 = jnp.sum(jnp.where(row == PAIR - 1, g, 0.0), axis=0, keepdims=True)
        g_last.append(jnp.where(row < CHUNK, last0, last1))
    join = lambda parts: jnp.concatenate(parts, axis=0)
    return (join([t[:, :DN_D] for t in uw]), join([t[:, DN_D:] for t in uw]), join(intra),
            join([q[j] * eg[j] for j in pairs]), join([k[j] * jnp.exp(g_last[j] - gc[j]) for j in pairs]),
            join([jnp.exp(g) for g in g_last]), join(list(tinv)))


def _dn1_fn(pids, q, k, v, gc, beta):
    return _dn1_pairs(q, k, v, gc, beta)


def _dn1_fn_known(pids, q, k, v, gc, beta, tinv):
    return _dn1_pairs(q, k, v, gc, beta, tinv)[:6]


def _dn2_step(half, state, qd, kd, u, w, intra, cd_row):
    heads = range(len(state))
    v_new = [u[h] - mm(w[h], state[h], "nn", "bf16") for h in heads]
    zeros = jnp.zeros_like(v_new[0])
    v_pair = [jnp.concatenate([v, zeros] if half == 0 else [zeros, v], axis=0) for v in v_new]
    from_state = [mm(qd[h], state[h], "nn", "bf16") for h in heads]
    out = tuple(from_state[h] + mm(intra[h], v_pair[h], "nn", "bf16") for h in heads)
    return out, tuple(state[h] * cd_row[h] + mm(kd[h], v_new[h], "tn", "bf16") for h in heads)


def _post_fn(pids, o, z, nw):
    outs = []
    for h in range(DN_HEADS):
        oh = o[:, DN_D * h:DN_D * (h + 1)]
        zh = z[:, DN_D * h:DN_D * (h + 1)]
        y = oh * lax.rsqrt(jnp.mean(oh * oh, axis=-1, keepdims=True) + NORM_EPS) * nw
        outs.append(y * _silu(zh))
    return (jnp.concatenate(outs, axis=1),)


def _memattn_fn(pids, qm, kvm):
    kmem, vmem = kvm[:, :MEM_W], kvm[:, MEM_W:]
    lane = _iota((1, MEM_W), 1)
    heads = range(MEM_HEADS)
    hm = [jnp.where((lane >> 6) == h, 1.0, 0.0) for h in heads]
    s = [mm(qm * (hm[h] * MEM_DH ** -0.5), kmem, "nt", "bf16") for h in heads]
    e = [jnp.exp(t - lax.stop_gradient(jnp.max(t, axis=-1, keepdims=True))) for t in s]
    o = [mm(e[h], vmem, "nn", "bf16") * (hm[h] / jnp.sum(e[h], axis=-1, keepdims=True)) for h in heads]
    return ((o[0] + o[1]) + (o[2] + o[3]),)


def _ln_fn(pids, h, mix, g, b):
    x = DN_ALPHA * h + mix
    mu = jnp.mean(x, axis=-1, keepdims=True)
    xc = x - mu
    var = jnp.mean(xc * xc, axis=-1, keepdims=True)
    return (xc * lax.rsqrt(var + LN_EPS) * g + b,)


def _rope_matrix():
    i, j = _iota((128, 128), 0), _iota((128, 128), 1)
    jj = j & 63
    return jnp.where((jj < 32) & (i == j + 32), -1.0, 0.0) + jnp.where((jj >= 32) & (i == j - 32), 1.0, 0.0)


def _rope128(x, cos, sin, rot):
    return x * cos + mm(x, rot, "nn", "selb") * sin


def _krope_fn(pids, kraw, cos, sin):
    rot = _rope_matrix()
    return (jnp.concatenate([_rope128(kraw[:, 128 * g:128 * (g + 1)], cos, sin, rot) for g in range(2)], axis=1),)


def _swa_fn(pids, qraw, cos, sin, k_halo, k_cur, v_halo, v_cur, sinks):
    tb = qraw.shape[0]
    nwin = tb // WINDOW
    rot = _rope_matrix()
    kcat = jnp.concatenate([k_halo, k_cur], axis=0)
    vcat = jnp.concatenate([v_halo, v_cur], axis=0)
    lane = _iota((1, 128), 1)
    halves = (jnp.where(lane < 64, 1.0, 0.0), jnp.where(lane >= 64, 1.0, 0.0))
    group = SWA_HEADS // 2
    rows = group * WINDOW
    in_cur = _iota((rows, WINDOW), 1) <= (_iota((rows, WINDOW), 0) & (WINDOW - 1))
    qg = [_rope128(qraw[:, 128 * p:128 * (p + 1)], cos, sin, rot) for p in range(group)]
    sink = []
    for kv in range(2):
        cols = [jnp.sum(jnp.where(lane == group * kv + i, sinks, 0.0), axis=-1, keepdims=True)
                + jnp.zeros((WINDOW, 1), F32) for i in range(group)]
        sink.append(jnp.concatenate(cols, axis=0))
    units = [(w, kv) for w in range(nwin) for kv in range(2)]
    n_units = range(len(units))
    q6 = [jnp.concatenate([qg[3 * kv + i // 2][WINDOW * w:WINDOW * (w + 1)] * (halves[i % 2] * SWA_DH ** -0.5)
                           for i in range(group)], axis=0) for w, kv in units]
    blk = lambda cat, w, kv: cat[WINDOW * w:WINDOW * (w + 1), 128 * kv:128 * (kv + 1)]
    s_prev = [mm(q6[u], blk(kcat, w, kv), "nt", "bf16") for u, (w, kv) in enumerate(units)]
    s_cur = [mm(q6[u], blk(kcat, w + 1, kv), "nt", "bf16") for u, (w, kv) in enumerate(units)]
    s = [jnp.where(in_cur, s_cur[u], jnp.where(pids[0] * nwin + w > 0, s_prev[u], NEG_BIG))
         for u, (w, kv) in enumerate(units)]
    m = [lax.stop_gradient(jnp.maximum(jnp.max(s[u], axis=-1, keepdims=True), sink[kv]))
         for u, (w, kv) in enumerate(units)]
    e = [jnp.exp(s[u] - m[u]) for u in n_units]
    denom = [jnp.sum(e[u], axis=-1, keepdims=True) + jnp.exp(sink[kv] - m[u]) for u, (w, kv) in enumerate(units)]
    o = [(mm(jnp.where(in_cur, e[u], 0.0), blk(vcat, w + 1, kv), "nn", "bf16")
          + mm(jnp.where(in_cur, 0.0, e[u]), blk(vcat, w, kv), "nn", "bf16")) / denom[u]
         for u, (w, kv) in enumerate(units)]
    out_rows = []
    for w in range(nwin):
        lanes = []
        for p in range(group):
            ou = o[units.index((w, p // 3))]
            i = 2 * (p % 3)
            lanes.append(ou[WINDOW * i:WINDOW * (i + 1)] * halves[0] + ou[WINDOW * (i + 1):WINDOW * (i + 2)] * halves[1])
        out_rows.append(jnp.concatenate(lanes, axis=1))
    return (jnp.concatenate(out_rows, axis=0),)


def _conv_fwd(proj, conv_w, tb):
    t_total = proj.shape[0]
    width = conv_w.shape[1]
    nb = t_total // tb

    def body(cur_ref, prev_ref, w_ref, out_ref):
        i = pl.program_id(0)
        prev = jnp.where(i > 0, prev_ref[...], 0.0)
        xcat = jnp.concatenate([prev, cur_ref[...]], axis=0)
        acc = xcat[8:] * w_ref[3:4, :]
        for j in range(3):
            acc = acc + pltpu.roll(xcat, 3 - j, 0)[8:] * w_ref[j:j + 1, :]
        out_ref[...] = acc

    return pl.pallas_call(
        body, grid=(nb,),
        in_specs=[pl.BlockSpec((tb, width), lambda i: (i, 0)),
                  pl.BlockSpec((8, width), lambda i: (jnp.maximum(i * (tb // 8) - 1, 0), 0)),
                  _whole((4, width))],
        out_specs=pl.BlockSpec((tb, width), lambda i: (i, 0)), out_shape=_sds((t_total, width)),
        name="conv_fwd", compiler_params=_cparams(("parallel",)))(proj, proj, conv_w)


def _conv_bwd(dc, proj, conv_w, tb):
    t_total, width = dc.shape
    nb = t_total // tb

    def body(dcur_ref, dnext_ref, cur_ref, prev_ref, w_ref, dx_ref, dw_ref):
        i = pl.program_id(0)
        dnext = jnp.where(i < nb - 1, dnext_ref[...], 0.0)
        dcur = dcur_ref[...]
        dcat = jnp.concatenate([dcur, dnext], axis=0)
        prev = jnp.where(i > 0, prev_ref[...], 0.0)
        xcat = jnp.concatenate([prev, cur_ref[...]], axis=0)

        @pl.when(i == 0)
        def _():
            dw_ref[...] = jnp.zeros(dw_ref.shape, F32)

        dx = dcur * w_ref[3:4, :]
        dw_ref[3:4, :] += jnp.sum(dcur * xcat[8:], axis=0, keepdims=True)
        for j in range(3):
            dx = dx + pltpu.roll(dcat, 8 - (3 - j), 0)[8:] * w_ref[j:j + 1, :]
            dw_ref[j:j + 1, :] += jnp.sum(dcur * pltpu.roll(xcat, 3 - j, 0)[8:], axis=0, keepdims=True)
        dx_ref[...] = dx.astype(dx_ref.dtype)

    return pl.pallas_call(
        body, grid=(nb,),
        in_specs=[pl.BlockSpec((tb, width), lambda i: (i, 0)),
                  pl.BlockSpec((8, width), lambda i: (jnp.minimum((i + 1) * (tb // 8), t_total // 8 - 1), 0)),
                  pl.BlockSpec((tb, width), lambda i: (i, 0)),
                  pl.BlockSpec((8, width), lambda i: (jnp.maximum(i * (tb // 8) - 1, 0), 0)),
                  _whole((4, width))],
        out_specs=[pl.BlockSpec((tb, width), lambda i: (i, 0)), _whole((4, width))],
        out_shape=[_sds((t_total, width), BF16), _sds((4, width))],
        name="conv_bwd", compiler_params=_cparams(("arbitrary",)))(dc, dc, proj, proj, conv_w)


def _head_spec(tb, nb=None):
    if nb is None:
        return pl.BlockSpec((tb, DN_D), lambda h, i: (i, h))
    return pl.BlockSpec((tb, DN_D), lambda h, i: (nb - 1 - i, h))


def _intra_spec(tb, nb=None):
    if nb is None:
        return pl.BlockSpec((None, tb, PAIR), lambda h, i: (h, i, 0))
    return pl.BlockSpec((None, tb, PAIR), lambda h, i: (h, nb - 1 - i, 0))


def _state_spec(tb, nb=None):
    if nb is None:
        return pl.BlockSpec((None, tb // CHUNK, DN_D, DN_D), lambda h, i: (h, i, 0, 0))
    return pl.BlockSpec((None, tb // CHUNK, DN_D, DN_D), lambda h, i: (h, nb - 1 - i, 0, 0))


def _scan_specs(tb, nb=None):
    blk = (lambda i: i) if nb is None else (lambda i: nb - 1 - i)
    rows = pl.BlockSpec((tb, DN_W), lambda i: (blk(i), 0))
    pair = pl.BlockSpec((DN_HEADS, tb, PAIR), lambda i: (0, blk(i), 0))
    states = pl.BlockSpec((DN_HEADS, tb // CHUNK, DN_D, DN_D), lambda i: (0, blk(i), 0, 0))
    return rows, pair, states


def _dn2_fwd(qd, kd, u, w, intra, cd, tb):
    t_total = qd.shape[0]
    rows, pair, states = _scan_specs(tb)

    def body(qd_ref, kd_ref, u_ref, w_ref, a_ref, cd_ref, o_ref, save_ref, state):
        @pl.when(pl.program_id(0) == 0)
        def _():
            state[...] = jnp.zeros(state.shape, F32)

        heads = range(DN_HEADS)
        lanes = [pl.ds(DN_D * h, DN_D) for h in heads]
        for j in range(tb // CHUNK):
            sl = pl.ds(CHUNK * j, CHUNK)
            s0 = tuple(state[h] for h in heads)
            for h in heads:
                save_ref[h, j] = s0[h]
            per_head = lambda ref: tuple(ref[sl, lanes[h]] for h in heads)
            out, s1 = _dn2_step(j % 2, s0, per_head(qd_ref), per_head(kd_ref), per_head(u_ref), per_head(w_ref),
                                tuple(a_ref[h, sl, :] for h in heads),
                                tuple(cd_ref[pl.ds(CHUNK * j, 1), lanes[h]] for h in heads))
            for h in heads:
                o_ref[sl, lanes[h]] = out[h]
                state[h] = s1[h]

    return pl.pallas_call(
        body, grid=(t_total // tb,), in_specs=[rows, rows, rows, rows, pair, rows],
        out_specs=[rows, states],
        out_shape=[_sds((t_total, DN_W)), _sds((DN_HEADS, t_total // CHUNK, DN_D, DN_D))],
        scratch_shapes=[pltpu.VMEM((DN_HEADS, DN_D, DN_D), F32)], name="dn2_fwd",
        compiler_params=_cparams(("arbitrary",)))(qd, kd, u, w, intra, cd)


def _dn2_bwd(qd, kd, u, w, intra, cd, saved, d_o, tb):
    t_total = qd.shape[0]
    nb = t_total // tb
    rows, pair, states = _scan_specs(tb, nb)

    def body(qd_ref, kd_ref, u_ref, w_ref, a_ref, cd_ref, save_ref, do_ref,
             dqd_ref, dkd_ref, du_ref, dw_ref, da_ref, dcd_ref, dstate):
        @pl.when(pl.program_id(0) == 0)
        def _():
            dstate[...] = jnp.zeros(dstate.shape, F32)

        first_row = _iota((CHUNK, DN_D), 0) == 0
        heads = range(DN_HEADS)
        lanes = [pl.ds(DN_D * h, DN_D) for h in heads]
        for j in reversed(range(tb // CHUNK)):
            sl = pl.ds(CHUNK * j, CHUNK)
            per_head = lambda ref: tuple(ref[sl, lanes[h]] for h in heads)
            _, vjp = jax.vjp(functools.partial(_dn2_step, j % 2), tuple(save_ref[h, j] for h in heads),
                             per_head(qd_ref), per_head(kd_ref), per_head(u_ref), per_head(w_ref),
                             tuple(a_ref[h, sl, :] for h in heads),
                             tuple(cd_ref[pl.ds(CHUNK * j, 1), lanes[h]] for h in heads))
            ds0, dqd, dkd, du, dw, da, dcd = vjp((per_head(do_ref), tuple(dstate[h] for h in heads)))
            for h in heads:
                dqd_ref[sl, lanes[h]] = dqd[h]
                dkd_ref[sl, lanes[h]] = dkd[h]
                du_ref[sl, lanes[h]] = du[h]
                dw_ref[sl, lanes[h]] = dw[h]
                da_ref[h, sl, :] = da[h]
                dcd_ref[sl, lanes[h]] = jnp.where(first_row, dcd[h], 0.0)
                dstate[h] = ds0[h]

    full = _sds((t_total, DN_W))
    return pl.pallas_call(
        body, grid=(nb,),
        in_specs=[rows, rows, rows, rows, pair, rows, states, rows],
        out_specs=[rows, rows, rows, rows, pair, rows],
        out_shape=[full, full, full, full, _sds((DN_HEADS, t_total, PAIR)), full],
        scratch_shapes=[pltpu.VMEM((DN_HEADS, DN_D, DN_D), F32)], name="dn2_bwd",
        compiler_params=_cparams(("arbitrary",)))(qd, kd, u, w, intra, cd, saved, d_o)


def _loss_and_grad(y, target, tb):
    t_total, d = y.shape

    def body(y_ref, t_ref, dy_ref, acc_ref):
        @pl.when(pl.program_id(0) == 0)
        def _():
            acc_ref[...] = jnp.zeros(acc_ref.shape, F32)

        err = y_ref[...] - t_ref[...]
        dy_ref[...] = err * (1.0 / d)
        acc_ref[...] += jnp.sum(err * err, axis=0, keepdims=True)

    dy, acc = pl.pallas_call(
        body, grid=(t_total // tb,), in_specs=[_rows(tb, d), _rows(tb, d)],
        out_specs=[_rows(tb, d), _whole((1, d))], out_shape=[_sds((t_total, d)), _sds((1, d))],
        name="loss", compiler_params=_cparams(("arbitrary",)))(y, target)
    return 0.5 * jnp.sum(acc) / d, dy


def _halo_sum(mains, halos, tb):
    t_total, width = mains[0].shape
    nb = t_total // tb
    n = len(mains)

    def body(*refs):
        out_ref = refs[-1]
        i = pl.program_id(0)
        tot = refs[0][...]
        for r in refs[1:n]:
            tot = tot + r[...]
        hal = refs[n][...]
        for r in refs[n + 1:2 * n]:
            hal = hal + r[...]
        hal = jnp.where(i < nb - 1, hal, 0.0)
        out_ref[...] = tot + jnp.concatenate([jnp.zeros((tb - WINDOW, width), F32), hal], axis=0)

    return pl.pallas_call(
        body, grid=(nb,),
        in_specs=[_rows(tb, width)] * n
        + [pl.BlockSpec((None, WINDOW, width), lambda i: (jnp.minimum(i + 1, nb - 1), 0, 0))] * n,
        out_specs=_rows(tb, width), out_shape=_sds((t_total, width)), name="halo_sum",
        compiler_params=_cparams(("parallel",)))(*mains, *halos)


def _adamw(recvs, w, m, v, tr, name):
    slots, _, c_total = recvs[0].shape
    r_total = w.shape[0]
    assert sum(r.shape[1] for r in recvs) == r_total
    tr = min([tr] + [r.shape[1] for r in recvs])
    assert all(r.shape[1] % tr == 0 for r in recvs)
    starts = [sum(r.shape[1] for r in recvs[:i]) // tr for i in range(len(recvs))]
    counts = [r.shape[1] // tr for r in recvs]
    c1 = 1.0 / (1.0 - ADAM_B1 ** ADAM_STEP)
    c2 = 1.0 / (1.0 - ADAM_B2 ** ADAM_STEP)

    def body(*refs):
        recv_refs = refs[:len(recvs)]
        w_ref, m_ref, v_ref, g_ref, d_ref, nm_ref, nv_ref = refs[len(recvs):]
        g = None
        for recv_ref, start in zip(recv_refs, starts):
            part = recv_ref[0].astype(F32)
            for s in range(1, slots):
                part = part + recv_ref[s].astype(F32)
            g = part if g is None else jnp.where(pl.program_id(0) >= start, part, g)
        nm = ADAM_B1 * m_ref[...] + (1.0 - ADAM_B1) * g
        nv = ADAM_B2 * v_ref[...] + (1.0 - ADAM_B2) * (g * g)
        g_ref[...] = g
        nm_ref[...] = nm
        nv_ref[...] = nv
        d_ref[...] = -ADAM_LR * ((nm * c1) / (jnp.sqrt(nv * c2) + ADAM_EPS) + ADAM_WD * w_ref[...])

    blk = pl.BlockSpec((tr, c_total), lambda i: (i, 0))
    recv_specs = [pl.BlockSpec((slots, tr, c_total), lambda i, s=s, n=n: (0, jnp.clip(i - s, 0, n - 1), 0))
                  for s, n in zip(starts, counts)]
    return pl.pallas_call(
        body, grid=(r_total // tr,), in_specs=recv_specs + [blk, blk, blk],
        out_specs=[blk] * 4, out_shape=[_sds((r_total, c_total))] * 4, name=name,
        compiler_params=_cparams(("parallel",)))(*recvs, w, m, v)


def _me_and_peers():
    x, y, c = lax.axis_index("x"), lax.axis_index("y"), lax.axis_index("c")
    me = 4 * x + 2 * y + c
    peers = []
    for k in range(1, N_DEV):
        px = 1 - x if (k >> 2) & 1 else x
        py = 1 - y if (k >> 1) & 1 else y
        pc = 1 - c if k & 1 else c
        peers.append(((px, py, pc), 4 * px + 2 * py + pc))
    return me, peers


def _small_exchange(packed, reduce):
    r_total = packed.shape[0]

    def body(p_ref, out_ref, gath_ref, send_sems, recv_sems):
        me, peers = _me_and_peers()
        gath_ref[me] = p_ref[...]
        copies = []
        for k, (dev, _) in enumerate(peers):
            cp = pltpu.make_async_remote_copy(src_ref=p_ref, dst_ref=gath_ref.at[me], send_sem=send_sems.at[k],
                                              recv_sem=recv_sems.at[k], device_id=dev,
                                              device_id_type=pl.DeviceIdType.MESH)
            cp.start()
            copies.append(cp)
        for k, (dev, idx) in enumerate(peers):
            pltpu.make_async_remote_copy(src_ref=p_ref, dst_ref=gath_ref.at[idx], send_sem=send_sems.at[k],
                                         recv_sem=recv_sems.at[k], device_id=dev,
                                         device_id_type=pl.DeviceIdType.MESH).wait_recv()
        for cp in copies:
            cp.wait_send()
        if reduce:
            tot = gath_ref[0]
            for d in range(1, N_DEV):
                tot = tot + gath_ref[d]
            out_ref[...] = tot
        else:
            out_ref[...] = gath_ref[...]

    out_shape = _sds((r_total, 128)) if reduce else _sds((N_DEV, r_total, 128))
    return pl.pallas_call(
        body, in_specs=[pl.BlockSpec(memory_space=pltpu.VMEM)], out_specs=pl.BlockSpec(memory_space=pltpu.VMEM),
        out_shape=out_shape,
        scratch_shapes=[pltpu.VMEM((N_DEV, r_total, 128), F32), pltpu.SemaphoreType.DMA((N_DEV - 1,)),
                        pltpu.SemaphoreType.DMA((N_DEV - 1,))],
        name="small_allreduce" if reduce else "small_allgather")(packed)


def _slot(ref, axis, idx, size):
    sel = [slice(None)] * len(ref.shape)
    sel[axis] = idx if size is None else pl.ds(pl.multiple_of(idx * size, size), size)
    return ref.at[tuple(sel)]


def _big_exchange(srcs, dst_shapes, src_view, dst_view, name):
    n = len(srcs)

    def body(*refs):
        src_refs, dst_refs = refs[:n], refs[n:2 * n]
        send_sems, recv_sems, local_sems = refs[2 * n:]
        me, peers = _me_and_peers()
        local, remote = [], []
        for t in range(n):
            loc = pltpu.make_async_copy(src_view(t, src_refs[t], me), dst_view(t, dst_refs[t], me), local_sems.at[t])
            loc.start()
            local.append(loc)
            for k, (dev, idx) in enumerate(peers):
                cp = pltpu.make_async_remote_copy(
                    src_ref=src_view(t, src_refs[t], idx), dst_ref=dst_view(t, dst_refs[t], me),
                    send_sem=send_sems.at[t, k], recv_sem=recv_sems.at[t, k], device_id=dev,
                    device_id_type=pl.DeviceIdType.MESH)
                cp.start()
                remote.append(cp)
        for t in range(n):
            for k, (dev, idx) in enumerate(peers):
                pltpu.make_async_remote_copy(
                    src_ref=src_view(t, src_refs[t], me), dst_ref=dst_view(t, dst_refs[t], idx),
                    send_sem=send_sems.at[t, k], recv_sem=recv_sems.at[t, k], device_id=dev,
                    device_id_type=pl.DeviceIdType.MESH).wait_recv()
        for cp in remote:
            cp.wait_send()
        for cp in local:
            cp.wait()

    any_spec = pl.BlockSpec(memory_space=pl.ANY)
    return pl.pallas_call(
        body, in_specs=[any_spec] * n, out_specs=[any_spec] * n, out_shape=dst_shapes,
        scratch_shapes=[pltpu.SemaphoreType.DMA((n, N_DEV - 1)), pltpu.SemaphoreType.DMA((n, N_DEV - 1)),
                        pltpu.SemaphoreType.DMA((n,))],
        name=name)(*srcs)


BIG = {
    "a_w_in": (1, (2, 1024, A_IN)),
    "b_w_in": (1, (2, 1024, 1024)),
    "w_kv_shared": (0, (1024, 256)),
    "mem_w_kv": (1, (4, 1024, 512)),
    "w_o": (1, (4, 1024, 1024)),
    "mlp_w_up": (2, (4, 1024, 4096)),
    "mlp_w_down": (1, (4, 4096, 1024)),
}
BIG_NAMES = tuple(BIG)


def _gather_plan(names, shards):
    dst_shapes, axes, sizes = [], [], []
    for name, s in zip(names, shards):
        axis = BIG[name][0] - (len(BIG[name][1]) - s.ndim)
        dst_shapes.append(_sds(tuple(d * N_DEV if a == axis else d for a, d in enumerate(s.shape)), s.dtype))
        axes.append(axis)
        sizes.append(s.shape[axis])
    return dst_shapes, axes, sizes


def _gather_ride(names, shards):
    dst_shapes, axes, sizes = _gather_plan(names, shards)
    return _Ride(shards, dst_shapes, lambda t, ref, idx: ref, lambda t, ref, idx: _slot(ref, axes[t], idx, sizes[t]))


def _scatter_plan(names, grads):
    dst_shapes, axes, sizes = [], [], []
    for name, g in zip(names, grads):
        axis = BIG[name][0] - (len(BIG[name][1]) - g.ndim)
        shard = tuple(d // N_DEV if a == axis else d for a, d in enumerate(g.shape))
        axes.append(axis)
        sizes.append(shard[axis])
        dst_shapes.append(_sds((N_DEV,) + shard, g.dtype))
    return dst_shapes, axes, sizes


def _scatter_ride(names, grads):
    dst_shapes, axes, sizes = _scatter_plan(names, grads)
    return _Ride(grads, dst_shapes, lambda t, ref, idx: _slot(ref, axes[t], idx, sizes[t]),
                 lambda t, ref, idx: ref.at[idx])


def _allgather_weights(names, shards):
    dst_shapes, axes, sizes = _gather_plan(names, shards)
    n = len(shards)

    def body(*refs):
        src_refs, dst_refs = refs[:n], refs[n:2 * n]
        send_sems, recv_sems, local_sems = refs[2 * n:]
        x, y, c = lax.axis_index("x"), lax.axis_index("y"), lax.axis_index("c")
        sibling = (x, y, 1 - c)
        chips = [(1 - x, y), (x, 1 - y), (1 - x, 1 - y)]
        index = lambda px, py, pc: 4 * px + 2 * py + pc

        def copy(t, k, block, to, src=None):
            rows = _slot(dst_refs[t], axes[t], index(*block), sizes[t])
            return pltpu.make_async_remote_copy(
                src_ref=rows if src is None else src, dst_ref=rows, send_sem=send_sems.at[t, k],
                recv_sem=recv_sems.at[t, k], device_id=to, device_id_type=pl.DeviceIdType.MESH)

        started, local = [], []
        for t in range(n):
            mine = pltpu.make_async_copy(src_refs[t], _slot(dst_refs[t], axes[t], index(x, y, c), sizes[t]),
                                         local_sems.at[t])
            mine.start()
            local.append(mine)
            first = [copy(t, 0, (x, y, c), sibling, src=src_refs[t])]
            first += [copy(t, 1 + j, (x, y, c), (*chip, c), src=src_refs[t]) for j, chip in enumerate(chips)]
            for cp in first:
                cp.start()
            started += first
        for t in range(n):
            for j, chip in enumerate(chips):
                copy(t, 1 + j, (*chip, c), (x, y, c)).wait_recv()
                passed = copy(t, 4 + j, (*chip, c), sibling)
                passed.start()
                started.append(passed)
        for t in range(n):
            copy(t, 0, sibling, (x, y, c)).wait_recv()
            for j, chip in enumerate(chips):
                copy(t, 4 + j, (*chip, 1 - c), (x, y, c)).wait_recv()
        for cp in started:
            cp.wait_send()
        for cp in local:
            cp.wait()

    any_spec = pl.BlockSpec(memory_space=pl.ANY)
    return pl.pallas_call(
        body, in_specs=[any_spec] * n, out_specs=[any_spec] * n, out_shape=dst_shapes,
        scratch_shapes=[pltpu.SemaphoreType.DMA((n, N_DEV - 1)), pltpu.SemaphoreType.DMA((n, N_DEV - 1)),
                        pltpu.SemaphoreType.DMA((n,))],
        name="allgather_weights")(*shards)


def _scatter_grads(names, grads):
    dst_shapes, axes, sizes = _scatter_plan(names, grads)

    def src_view(t, ref, idx):
        return _slot(ref, axes[t], idx, sizes[t])

    def dst_view(t, ref, idx):
        return ref.at[idx]

    return _big_exchange(grads, dst_shapes, src_view, dst_view, "scatter_grads")


def _pad_row(vec, width=128):
    return jnp.pad(vec.astype(F32), (0, width - vec.shape[0])).reshape(1, width)


def _block_sizes(t_total):
    return dict(row=min(256, t_total), dn=min(512, t_total), swa=min(256, t_total), scan=min(256, t_total))


def _ln_apply(h, mix, g, b, tb):
    t_total, d = h.shape
    fwd = lambda pids, *a: _ln_fn(pids, *a) * 2
    return _block_fwd(fwd, [h, mix, g, b], [_rows(tb, d), _rows(tb, d), _whole((1, d)), _whole((1, d))],
                      [_sds((t_total, d)), _sds((t_total, d), BF16)], [_rows(tb, d), _rows(tb, d)],
                      (t_total // tb,), "ln_fwd")


def _ln_grad(h, mix, g, b, dy, tb):
    t_total, d = h.shape

    def body(h_ref, mix_ref, g_ref, dy_ref, dh_ref, dmix_ref, dg_ref, db_ref):
        @pl.when(pl.program_id(0) == 0)
        def _():
            dg_ref[...] = jnp.zeros(dg_ref.shape, F32)
            db_ref[...] = jnp.zeros(db_ref.shape, F32)

        x = DN_ALPHA * h_ref[...] + mix_ref[...]
        xc = x - jnp.mean(x, axis=-1, keepdims=True)
        rstd = lax.rsqrt(jnp.mean(xc * xc, axis=-1, keepdims=True) + LN_EPS)
        xhat = xc * rstd
        dy_val = dy_ref[...]
        dxh = dy_val * g_ref[...]
        m1 = jnp.mean(dxh, axis=-1, keepdims=True)
        m2 = jnp.mean(dxh * xhat, axis=-1, keepdims=True)
        dx = (dxh - m1 - xhat * m2) * rstd
        dmix_ref[...] = dx.astype(dmix_ref.dtype)
        dh_ref[...] = DN_ALPHA * dx
        dg_ref[...] += jnp.sum(dy_val * xhat, axis=0, keepdims=True)
        db_ref[...] += jnp.sum(dy_val, axis=0, keepdims=True)

    return pl.pallas_call(
        body, grid=(t_total // tb,),
        in_specs=[_rows(tb, d), _rows(tb, d), _whole((1, d)), _rows(tb, d)],
        out_specs=[_rows(tb, d), _rows(tb, d), _whole((1, d)), _whole((1, d))],
        out_shape=[_sds((t_total, d)), _sds((t_total, d), BF16), _sds((1, d)), _sds((1, d))],
        name="ln_bwd", compiler_params=_cparams(("arbitrary",)))(h, mix, g, dy)


def _memattn_specs(tb, qcol):
    return [pl.BlockSpec((tb, MEM_W), lambda i: (i, qcol)), _whole((MEM_W, 2 * MEM_W))]


def _act_epilogue(acc):
    r = jnp.maximum(acc, 0.0)
    return (r * r,)


def _dact_epilogue(acc, act):
    return (acc * (2.0 * jnp.sqrt(act.astype(F32))),)


def _add_epilogue(acc, other):
    return (acc + other,)


def _key(name, layer):
    if name == "w_kv_shared":
        return (name, None)
    return (name, layer - N_A if name == "b_w_in" else layer)


_PER_LAYER = ("mem_w_kv", "w_o", "mlp_w_up", "mlp_w_down")
GATHER_FIRST = [_key("a_w_in", 0)]
GATHER_ON_ROWA0 = [_key(n, 0) for n in _PER_LAYER]
GATHER_ON_DN1_0 = [_key("a_w_in", 1), _key("w_kv_shared", 1)] + [_key(n, 1) for n in _PER_LAYER]
GATHER_ON_DN1_1 = [_key("b_w_in", 2), _key("b_w_in", 3)] + [_key(n, l) for l in (2, 3) for n in _PER_LAYER]
SCATTER_ON_DN1_BWD_1 = GATHER_ON_DN1_1
SCATTER_ON_DN1_BWD_0 = GATHER_ON_DN1_0 + GATHER_ON_ROWA0
SCATTER_LAST = GATHER_FIRST
ALL_KEYS = GATHER_FIRST + GATHER_ON_ROWA0 + GATHER_ON_DN1_0 + GATHER_ON_DN1_1


def _local_step(x, mem, positions, target, ready, shards, small):
    t_total = x.shape[0]
    bs = _block_sizes(t_total)
    tb, tdn, tsw = bs["row"], bs["dn"], bs["swa"]
    nb = t_total // tb
    nbs = t_total // tsw

    inv_freq = ROPE_THETA ** (-jnp.arange(0, SWA_DH, 2, dtype=F32) / SWA_DH)
    ang = positions.astype(F32)[:, None] * inv_freq
    cos = jnp.tile(jnp.cos(ang), (1, 4))
    sin = jnp.tile(jnp.sin(ang), (1, 4))

    mem_b = mem.astype(BF16)
    ready = dict(ready)
    derived = {}

    def gather_ride(keys):
        names = [k[0] for k in keys]
        return names, (_gather_ride(names, [shards[k] for k in keys]) if shards is not None else None)

    def weight(name, l):
        key = _key(name, l)
        if key not in derived:
            w = ready[key]
            if name == "a_w_in":
                w = jnp.concatenate([w[0][:, :3072], w[0][:, 3084:], w[0][:, 3072:3084],
                                     jnp.zeros((D_MODEL, A_IN_PAD - A_IN), BF16)], axis=1)
            elif name == "w_kv_shared":
                w = jnp.concatenate([w[:, 64 * (i // 2):64 * (i // 2 + 1)] for i in range(8)], axis=1)
            else:
                w = w[0]
            derived[key] = w
        return derived[key]

    saved = []
    h, hb = x, x.astype(BF16)
    kr = vd_src = None
    for l in range(DEPTH):
        sv = dict(h=h, hb=hb)
        if l < N_A:
            proj = _matmul(hb, weight("a_w_in", l), "nn", [F32], "mm_proj_a", tn=1152)
            conv_w = small["a_conv_w"][l]
            c = _conv_fwd(proj, conv_w, tb)
            alog, dtb = _pad_row(small["a_A_log"][l]), _pad_row(small["a_dt_bias"][l])
            rowa_in = [c, proj, alog, dtb]
            rowa_specs = [_rows(tb, 3 * DN_W), _rows(tb, 128, 26), _whole((1, 128)), _whole((1, 128))]
            rowa_args = (_rowa_fn, rowa_in, rowa_specs, [_sds((t_total, DN_W))] * 5, [_rows(tb, DN_W)] * 5, (nb,))
            if shards is not None and l == 0:
                names, ride = gather_ride(GATHER_ON_ROWA0)
                (q, k, v, gcb, betab), got = _block_fwd(*rowa_args, "rowa_fwd_gather", ride=ride)
                ready.update(zip(GATHER_ON_ROWA0, got))
            else:
                q, k, v, gcb, betab = _block_fwd(*rowa_args, "rowa_fwd")
            hs = _head_spec(tdn)
            dn_grid = (DN_HEADS, t_total // tdn)
            full = _sds((t_total, DN_W))
            full_b = _sds((t_total, DN_W), BF16)
            dn1_out_shapes = [full, full_b, _sds((DN_HEADS, t_total, PAIR), BF16), full_b, full_b, full,
                              _sds((DN_HEADS, t_total, PAIR))]
            dn1_out_specs = [hs, hs, _intra_spec(tdn), hs, hs, hs, _intra_spec(tdn)]
            if shards is not None:
                keys = GATHER_ON_DN1_0 if l == 0 else GATHER_ON_DN1_1
                names, ride = gather_ride(keys)
                (u, w, intra, qd, kd, cd, tinv), got = _block_fwd(
                    _dn1_fn, [q, k, v, gcb, betab], [hs] * 5, dn1_out_shapes, dn1_out_specs, dn_grid,
                    "dn1_fwd_gather%d" % l, ride=ride)
                ready.update(zip(keys, got))
            else:
                u, w, intra, qd, kd, cd, tinv = _block_fwd(_dn1_fn, [q, k, v, gcb, betab], [hs] * 5, dn1_out_shapes,
                                                           dn1_out_specs, dn_grid, "dn1_fwd")
            o, states = _dn2_fwd(qd, kd, u, w, intra, cd, bs["scan"])
            nw = small["a_norm_w"][l].reshape(1, DN_D)
            post_in = [o, proj, nw]
            post_specs = [_rows(tb, DN_W), _rows(tb, DN_W, 3), _whole((1, DN_D))]
            (og,) = _block_fwd(_post_fn, post_in, post_specs, [_sds((t_total, DN_W), BF16)], [_rows(tb, DN_W)],
                               (nb,), "post_fwd")
            qm_col = 12
            sv.update(proj=proj, c=c, rowa_in=rowa_in, rowa_specs=rowa_specs, dn1_in=[q, k, v, gcb, betab, tinv],
                      dn2_in=[qd, kd, u, w, intra, cd], states=states, post_in=post_in, post_specs=post_specs,
                      conv_w=conv_w)
        else:
            jb = l - N_A
            proj = _matmul(hb, weight("b_w_in", l), "nn", [F32], "mm_proj_b")
            sinks = _pad_row(small["b_sinks"][jb])
            swa_in = [proj, cos, sin, kr, kr, vd_src, vd_src, sinks]
            swa_specs = [_rows(tsw, DN_W), _rows(tsw, 128), _rows(tsw, 128),
                         pl.BlockSpec((WINDOW, 256), lambda i: (jnp.maximum(i * (tsw // WINDOW) - 1, 0), 0)),
                         _rows(tsw, 256),
                         pl.BlockSpec((WINDOW, 256), lambda i: (jnp.maximum(i * (tsw // WINDOW) - 1, 0), 1)),
                         _rows(tsw, 256, 1), _whole((1, 128))]
            (og,) = _block_fwd(_swa_fn, swa_in, swa_specs, [_sds((t_total, DN_W), BF16)], [_rows(tsw, DN_W)],
                               (nbs,), "swa_fwd")
            qm_col = 3
            sv.update(proj=proj, swa_in=swa_in, swa_specs=swa_specs)
        kvm = _matmul(mem_b, weight("mem_w_kv", l), "nn", [F32], "mm_memkv", tm=256)
        mem_in = [proj, kvm]
        (mo,) = _block_fwd(_memattn_fn, mem_in, _memattn_specs(tb, qm_col), [_sds((t_total, MEM_W), BF16)],
                           [_rows(tb, MEM_W)], (nb,), "memattn_fwd")
        mixin = jnp.concatenate([og, mo], axis=1)
        g0, b0 = small["ln_g"][l, 0].reshape(1, -1), small["ln_b"][l, 0].reshape(1, -1)
        mix, h1, h1b = _matmul_ln(mixin, weight("w_o", l), h, g0, b0, "mm_wo_ln")
        act = _matmul(h1b, weight("mlp_w_up", l), "nn", [BF16], "mm_up", epi=_act_epilogue, tm=2048)
        g1, b1 = small["ln_g"][l, 1].reshape(1, -1), small["ln_b"][l, 1].reshape(1, -1)
        mlp, h2, h2b = _matmul_ln(act, weight("mlp_w_down", l), h1, g1, b1, "mm_down_ln")
        sv.update(kvm=kvm, mem_in=mem_in, qm_col=qm_col, mixin=mixin, mix=mix, ln0=(g0, b0), h1=h1, h1b=h1b,
                  act=act, mlp=mlp, ln1=(g1, b1))
        saved.append(sv)
        h, hb = h2, h2b
        if l == N_A - 1:
            kvd = _matmul(hb, weight("w_kv_shared", l), "nn", [F32], "mm_kvd")
            krope_in = [kvd, cos, sin]
            krope_specs = [_rows(tb, 256), _rows(tb, 128), _rows(tb, 128)]
            (kr,) = _block_fwd(_krope_fn, krope_in, krope_specs, [_sds((t_total, 256))], [_rows(tb, 256)], (nb,),
                               "krope_fwd")
            vd_src = kvd

    loss, dh = _loss_and_grad(h, target, tb)

    grads = {}

    def scatter_ride(keys):
        return _scatter_ride([k[0] for k in keys], [grads[k] for k in keys]) if shards is not None else None

    sg = dict(a_conv_w=[None] * N_A, a_A_log=[None] * N_A, a_dt_bias=[None] * N_A, a_norm_w=[None] * N_A,
              b_sinks=[None] * (DEPTH - N_A), ln_g=[[None, None] for _ in range(DEPTH)],
              ln_b=[[None, None] for _ in range(DEPTH)])
    dk_parts, dv_parts = [], []
    for l in reversed(range(DEPTH)):
        sv = saved[l]
        if l == N_A - 1:
            dkr = _halo_sum([p[0] for p in dk_parts], [p[1] for p in dk_parts], tsw)
            dvv = _halo_sum([p[0] for p in dv_parts], [p[1] for p in dv_parts], tsw)
            (dkraw,) = _block_bwd(_krope_fn, krope_in, krope_specs, [dkr], [_rows(tb, 256)], ["s", None, None],
                                  [_sds((t_total, 256), BF16)], [_rows(tb, 256)], (nb,), "krope_bwd")
            dkvd = jnp.concatenate([dkraw, dvv.astype(BF16)], axis=1)
            g_kvd = _matmul(saved[l + 1]["hb"], dkvd, "tn", [F32], "mm_dw_kvd", tm=1024, tn=512)
            dh = _matmul(dkvd, weight("w_kv_shared", l), "nt", [F32], "mm_dx_kvd", epi=_add_epilogue, extras=[dh],
                         tn=1024, tk=512)
            grads[_key("w_kv_shared", l)] = jnp.concatenate(
                [g_kvd[:, 128 * i:128 * i + 64] + g_kvd[:, 128 * i + 64:128 * (i + 1)] for i in range(4)],
                axis=1).astype(BF16)
        g1, b1 = sv["ln1"]
        dh1a, dmlp, dg1, db1 = _ln_grad(sv["h1"], sv["mlp"], g1, b1, dh, tb)
        dup = _matmul(dmlp, weight("mlp_w_down", l), "nt", [BF16], "mm_dact", epi=_dact_epilogue, extras=[sv["act"]],
                      tm=2048)
        grads[_key("mlp_w_down", l)] = _matmul(sv["act"], dmlp, "tn", [BF16], "mm_dw_down", tk=4096)[None]
        grads[_key("mlp_w_up", l)] = _matmul(sv["h1b"], dup, "tn", [BF16], "mm_dw_up", tk=4096)[None]
        dh1 = _matmul(dup, weight("mlp_w_up", l), "nt", [F32], "mm_dx_up", epi=_add_epilogue, extras=[dh1a], tk=2048)
        g0, b0 = sv["ln0"]
        dha, dmix, dg0, db0 = _ln_grad(sv["h"], sv["mix"], g0, b0, dh1, tb)
        sg["ln_g"][l] = [dg0, dg1]
        sg["ln_b"][l] = [db0, db1]
        grads[_key("w_o", l)] = _matmul(sv["mixin"], dmix, "tn", [BF16], "mm_dw_o", tk=4096)[None]
        dmixin = _matmul(dmix, weight("w_o", l), "nt", [F32], "mm_dx_o", tn=1024)
        dqm, dkvm = _block_bwd(_memattn_fn, sv["mem_in"], _memattn_specs(tb, sv["qm_col"]), [dmixin],
                               [_rows(tb, MEM_W, 3)], ["s", "a"],
                               [_sds((t_total, MEM_W), BF16), _sds((MEM_W, 2 * MEM_W))],
                               [_rows(tb, MEM_W), _whole((MEM_W, 2 * MEM_W))], (nb,), "memattn_bwd")
        grads[_key("mem_w_kv", l)] = _matmul(mem_b, dkvm.astype(BF16), "tn", [BF16], "mm_dw_memkv", tm=1024,
                                             tn=512)[None]
        if l < N_A:
            d_o, dz, dnw = _block_bwd(_post_fn, sv["post_in"], sv["post_specs"], [dmixin], [_rows(tb, DN_W)],
                                      ["s", "s", "a"],
                                      [_sds((t_total, DN_W)), _sds((t_total, DN_W), BF16), _sds((1, DN_D))],
                                      [_rows(tb, DN_W), _rows(tb, DN_W), _whole((1, DN_D))], (nb,), "post_bwd")
            sg["a_norm_w"][l] = dnw
            dqd, dkd, du, dw, da, dcd = _dn2_bwd(*sv["dn2_in"], sv["states"], d_o, bs["scan"])
            hs = _head_spec(tdn)
            full = _sds((t_total, DN_W))
            dn1_bwd_args = (_dn1_fn_known, sv["dn1_in"], [hs] * 5 + [_intra_spec(tdn)], [du, dw, da, dqd, dkd, dcd],
                            [hs, hs, _intra_spec(tdn), hs, hs, hs], ["s"] * 5 + [None], [full] * 5, [hs] * 5,
                            (DN_HEADS, t_total // tdn))
            if shards is not None:
                keys = SCATTER_ON_DN1_BWD_1 if l == N_A - 1 else SCATTER_ON_DN1_BWD_0
                (dq, dk, dv, dgc, dbeta), got = _block_bwd(*dn1_bwd_args, "dn1_bwd_scatter%d" % l,
                                                           ride=scatter_ride(keys))
                grads.update(zip(keys, got))
            else:
                dq, dk, dv, dgc, dbeta = _block_bwd(*dn1_bwd_args, "dn1_bwd")
            dc, dab, dalog, ddtb = _block_bwd(
                _rowa_fn, sv["rowa_in"], sv["rowa_specs"], [dq, dk, dv, dgc, dbeta], [_rows(tb, DN_W)] * 5,
                ["s", "s", "a", "a"],
                [_sds((t_total, 3 * DN_W)), _sds((t_total, 128), BF16), _sds((1, 128)), _sds((1, 128))],
                [_rows(tb, 3 * DN_W), _rows(tb, 128), _whole((1, 128)), _whole((1, 128))], (nb,), "rowa_bwd")
            sg["a_A_log"][l] = dalog[0, :DN_HEADS]
            sg["a_dt_bias"][l] = ddtb[0, :DN_HEADS]
            dx, dconv = _conv_bwd(dc, sv["proj"], sv["conv_w"], tb)
            sg["a_conv_w"][l] = dconv
            dproj = jnp.concatenate([dx, dz, dqm, dab], axis=1)
            g_in = _matmul(sv["hb"], dproj, "tn", [BF16], "mm_dw_a", tm=1024, tn=1152, tk=2048)
            grads[_key("a_w_in", l)] = jnp.concatenate([g_in[:, :3072], g_in[:, 3328:3340], g_in[:, 3072:3328]],
                                                       axis=1)[None]
            dh = _matmul(dproj, weight("a_w_in", l), "nt", [F32], "mm_dx_a", epi=_add_epilogue, extras=[dha], tk=1152)
        else:
            jb = l - N_A
            swa_kinds = ["s", None, None, "s", "s", "s", "s", "a"]
            halo_spec = pl.BlockSpec((None, WINDOW, 256), lambda i: (i, 0, 0))
            dq, dkh, dkc, dvh, dvc, dsink = _block_bwd(
                _swa_fn, sv["swa_in"], sv["swa_specs"], [dmixin], [_rows(tsw, DN_W)], swa_kinds,
                [_sds((t_total, DN_W), BF16), _sds((nbs, WINDOW, 256)), _sds((t_total, 256)),
                 _sds((nbs, WINDOW, 256)), _sds((t_total, 256)), _sds((1, 128))],
                [_rows(tsw, DN_W), halo_spec, _rows(tsw, 256), halo_spec, _rows(tsw, 256), _whole((1, 128))],
                (nbs,), "swa_bwd")
            sg["b_sinks"][jb] = dsink[0, :SWA_HEADS]
            dk_parts.append((dkc, dkh))
            dv_parts.append((dvc, dvh))
            dproj = jnp.concatenate([dq, dqm], axis=1)
            grads[_key("b_w_in", l)] = _matmul(sv["hb"], dproj, "tn", [BF16], "mm_dw_b", tk=4096)[None]
            dh = _matmul(dproj, weight("b_w_in", l), "nt", [F32], "mm_dx_b", epi=_add_epilogue, extras=[dha], tn=1024)

    small_grads = dict(
        a_conv_w=jnp.stack(sg["a_conv_w"]), a_A_log=jnp.stack(sg["a_A_log"]), a_dt_bias=jnp.stack(sg["a_dt_bias"]),
        a_norm_w=jnp.concatenate(sg["a_norm_w"], axis=0), b_sinks=jnp.stack(sg["b_sinks"]),
        ln_g=jnp.stack([jnp.concatenate(p, axis=0) for p in sg["ln_g"]]),
        ln_b=jnp.stack([jnp.concatenate(p, axis=0) for p in sg["ln_b"]]))
    return loss, dh, grads, small_grads


def _pack(arrays, rows):
    flat = []
    for a in arrays:
        v = a.astype(F32).reshape(-1)
        flat.append(jnp.pad(v, (0, (-v.shape[0]) % 128)))
    flat = jnp.concatenate(flat)
    return jnp.pad(flat, (0, rows * 128 - flat.shape[0])).reshape(rows, 128)


def _unpack(slab, shapes):
    flat = slab.reshape(slab.shape[:-2] + (-1,))
    out, off = [], 0
    for s in shapes:
        n = math.prod(s)
        out.append(flat[..., off:off + n].reshape(slab.shape[:-2] + tuple(s)))
        off += n + (-n) % 128
    return out


def _rows_for(shapes):
    rows = sum((math.prod(s) + 127) // 128 for s in shapes)
    return rows + (-rows) % 8


SMALL_NAMES = ("a_conv_w", "a_A_log", "a_dt_bias", "a_norm_w", "b_sinks", "ln_g", "ln_b")
SMALL_SHARDED = {"a_conv_w": 2, "ln_g": 2, "ln_b": 2}
SMALL_FULL = {"a_conv_w": (2, 4, 2304), "a_A_log": (2, 6), "a_dt_bias": (2, 6), "a_norm_w": (2, 128),
              "b_sinks": (2, 12), "ln_g": (4, 2, 1024), "ln_b": (4, 2, 1024)}


def kernel(x, mem, positions, a_w_in, a_conv_w, a_A_log, a_dt_bias, a_norm_w, b_w_in, b_sinks, w_kv_shared, mem_w_kv, w_o, mlp_w_up, mlp_w_down, ln_g, ln_b, loss_target, m_a_w_in, m_a_conv_w, m_a_A_log, m_a_dt_bias, m_a_norm_w, m_b_w_in, m_b_sinks, m_w_kv_shared, m_mem_w_kv, m_w_o, m_mlp_w_up, m_mlp_w_down, m_ln_g, m_ln_b, v_a_w_in, v_a_conv_w, v_a_A_log, v_a_dt_bias, v_a_norm_w, v_b_w_in, v_b_sinks, v_w_kv_shared, v_mem_w_kv, v_w_o, v_mlp_w_up, v_mlp_w_down, v_ln_g, v_ln_b):
    params = dict(a_w_in=a_w_in, a_conv_w=a_conv_w, a_A_log=a_A_log, a_dt_bias=a_dt_bias, a_norm_w=a_norm_w,
                  b_w_in=b_w_in, b_sinks=b_sinks, w_kv_shared=w_kv_shared, mem_w_kv=mem_w_kv, w_o=w_o,
                  mlp_w_up=mlp_w_up, mlp_w_down=mlp_w_down, ln_g=ln_g, ln_b=ln_b)
    mom = dict(a_w_in=m_a_w_in, a_conv_w=m_a_conv_w, a_A_log=m_a_A_log, a_dt_bias=m_a_dt_bias, a_norm_w=m_a_norm_w,
               b_w_in=m_b_w_in, b_sinks=m_b_sinks, w_kv_shared=m_w_kv_shared, mem_w_kv=m_mem_w_kv, w_o=m_w_o,
               mlp_w_up=m_mlp_w_up, mlp_w_down=m_mlp_w_down, ln_g=m_ln_g, ln_b=m_ln_b)
    var = dict(a_w_in=v_a_w_in, a_conv_w=v_a_conv_w, a_A_log=v_a_A_log, a_dt_bias=v_a_dt_bias, a_norm_w=v_a_norm_w,
               b_w_in=v_b_w_in, b_sinks=v_b_sinks, w_kv_shared=v_w_kv_shared, mem_w_kv=v_mem_w_kv, w_o=v_w_o,
               mlp_w_up=v_mlp_w_up, mlp_w_down=v_mlp_w_down, ln_g=v_ln_g, ln_b=v_ln_b)
    me = 4 * lax.axis_index("x") + 2 * lax.axis_index("y") + lax.axis_index("c")

    shards = {(n, i): (params[n] if i is None else params[n][i:i + 1]).astype(BF16) for n, i in ALL_KEYS}
    first = [k[0] for k in GATHER_FIRST]
    ready = dict(zip(GATHER_FIRST, _allgather_weights(first, [shards[k] for k in GATHER_FIRST])))
    sharded_names = [n for n in SMALL_NAMES if n in SMALL_SHARDED]
    shard_shapes = [params[n].shape for n in sharded_names]
    gathered = _small_exchange(_pack([params[n] for n in sharded_names], _rows_for(shard_shapes)), reduce=False)
    small = {n: params[n] for n in SMALL_NAMES if n not in SMALL_SHARDED}
    for n, g in zip(sharded_names, _unpack(gathered, shard_shapes)):
        small[n] = jnp.moveaxis(g, 0, 2).reshape(SMALL_FULL[n])

    loss, dx, recv, small_grads = _local_step(x[0], mem[0], positions[0], loss_target[0], ready, shards, small)
    loss = lax.psum(loss, ("x", "y", "c"))
    last = [k[0] for k in SCATTER_LAST]
    recv.update(zip(SCATTER_LAST, _scatter_grads(last, [recv[k] for k in SCATTER_LAST])))
    out = {}
    for n in BIG_NAMES:
        shp = params[n].shape
        rows = math.prod(shp[:-1])
        recvs = [recv[k].reshape(N_DEV, -1, shp[-1]) for k in sorted(k for k in ALL_KEYS if k[0] == n)]
        res = _adamw(recvs, params[n].reshape(rows, shp[-1]), mom[n].reshape(rows, shp[-1]),
                     var[n].reshape(rows, shp[-1]), 32, "adamw_" + n)
        out[n] = [t.reshape(shp) for t in res]
    full_shapes = [SMALL_FULL[n] for n in SMALL_NAMES]
    summed = _small_exchange(_pack([small_grads[n] for n in SMALL_NAMES], _rows_for(full_shapes)), reduce=True)
    local_g = []
    for n, g in zip(SMALL_NAMES, _unpack(summed, full_shapes)):
        if n in SMALL_SHARDED:
            size = params[n].shape[2]
            g = lax.dynamic_slice_in_dim(g, me * size, size, axis=2)
        local_g.append(g)
    local_shapes = [params[n].shape for n in SMALL_NAMES]
    rows = _rows_for(local_shapes)
    res = _adamw([_pack(local_g, rows)[None]], _pack([params[n] for n in SMALL_NAMES], rows),
                 _pack([mom[n] for n in SMALL_NAMES], rows), _pack([var[n] for n in SMALL_NAMES], rows), rows,
                 "adamw_small")
    unpacked = [_unpack(t, local_shapes) for t in res]
    for i, n in enumerate(SMALL_NAMES):
        out[n] = [unpacked[k][i] for k in range(4)]

    order = ("a_w_in", "a_conv_w", "a_A_log", "a_dt_bias", "a_norm_w", "b_w_in", "b_sinks", "w_kv_shared",
             "mem_w_kv", "w_o", "mlp_w_up", "mlp_w_down", "ln_g", "ln_b")
    return (loss, dx[None], *[out[n][0] for n in order], *[out[n][1] for n in order],
            *[out[n][2] for n in order], *[out[n][3] for n in order])
```

```python
import functools
import math

import jax
import jax.numpy as jnp
from jax import lax
from jax.experimental import pallas as pl
from jax.experimental.pallas import tpu as pltpu

F32 = jnp.float32
BF16 = jnp.bfloat16

D_MODEL = 1024
DEPTH = 4
N_A = 2
MEM_HEADS = 4
MEM_DH = 64
MEM_W = 256
DN_HEADS = 6
DN_D = 128
DN_W = 768
CHUNK = 64
SWA_DH = 64
SWA_HEADS = 12
WINDOW = 128
ROPE_THETA = 10000.0
LN_EPS = 1e-5
NORM_EPS = 1e-6
DN_ALPHA = (2.0 * DEPTH) ** 0.25
A_IN = 3340
A_IN_PAD = 3456
N_DEV = 8

ADAM_LR = 0.001
ADAM_B1 = 0.9
ADAM_B2 = 0.999
ADAM_EPS = 1e-08
ADAM_WD = 0.01
ADAM_STEP = 10

VMEM_LIMIT = 52 * 1024 * 1024
NEG_BIG = -1e30


def _cparams(sem):
    return pltpu.CompilerParams(dimension_semantics=sem, vmem_limit_bytes=VMEM_LIMIT)


_CONTRACT = {"nn": (1, 0), "nt": (1, 1), "tn": (0, 0)}


def _raw_mm(a, b, mode, prec):
    ca, cb = _CONTRACT[mode]
    dims = (((ca,), (cb,)), ((), ()))
    dot = lambda p, q: lax.dot_general(p, q, dims, preferred_element_type=F32)
    if prec == "bf16":
        return dot(a.astype(BF16), b.astype(BF16))
    a, b = a.astype(F32), b.astype(F32)
    a_hi, b_hi = a.astype(BF16), b.astype(BF16)
    if prec == "sela":
        return dot(a_hi, b_hi) + dot(a_hi, (b - b_hi.astype(F32)).astype(BF16))
    a_lo = (a - a_hi.astype(F32)).astype(BF16)
    if prec == "selb":
        return dot(a_hi, b_hi) + dot(a_lo, b_hi)
    b_lo = (b - b_hi.astype(F32)).astype(BF16)
    return dot(a_hi, b_hi) + (dot(a_hi, b_lo) + dot(a_lo, b_hi))


@functools.partial(jax.custom_vjp, nondiff_argnums=(2, 3))
def mm(a, b, mode, prec):
    return _raw_mm(a, b, mode, prec)


def _mm_fwd(a, b, mode, prec):
    return _raw_mm(a, b, mode, prec), (a, b)


def _mm_bwd(mode, prec, res, ct):
    a, b = res
    if prec == "sela":
        pa, pb = "f32", {"nn": "sela", "nt": "selb", "tn": "sela"}[mode]
    elif prec == "selb":
        pa, pb = {"nn": "selb", "nt": "selb", "tn": "sela"}[mode], "f32"
    else:
        pa = pb = prec
    if mode == "nn":
        return mm(ct, b, "nt", pa), mm(a, ct, "tn", pb)
    if mode == "nt":
        return mm(ct, b, "nn", pa), mm(ct, a, "tn", pb)
    return mm(b, ct, "nt", pa), mm(a, ct, "nn", pb)


mm.defvjp(_mm_fwd, _mm_bwd)


@jax.custom_vjp
def _softplus(x):
    y = jnp.exp(-jnp.abs(x))
    log1p_y = jnp.where(y < 1e-2, y * (1.0 - y * (0.5 - y * (1.0 / 3.0))), jnp.log(1.0 + y))
    return jnp.maximum(x, 0.0) + log1p_y


def _softplus_fwd(x):
    return _softplus(x), x


def _softplus_bwd(x, ct):
    return (ct * jax.nn.sigmoid(x),)


_softplus.defvjp(_softplus_fwd, _softplus_bwd)


def _iota(shape, dim):
    return lax.broadcasted_iota(jnp.int32, shape, dim)


class _Ride:
    def __init__(self, srcs, dst_shapes, src_view, dst_view):
        self.srcs, self.dst_shapes, self.src_view, self.dst_view = list(srcs), list(dst_shapes), src_view, dst_view
        self.n = len(self.srcs)
        self.any_specs = [pl.BlockSpec(memory_space=pl.ANY)] * self.n
        self.scratch = [pltpu.SemaphoreType.DMA((self.n, N_DEV - 1)), pltpu.SemaphoreType.DMA((self.n, N_DEV - 1)),
                        pltpu.SemaphoreType.DMA((self.n,))]

    def copies(self, src_refs, dst_refs, sems):
        send_sems, recv_sems, local_sems = sems
        me, peers = _me_and_peers()
        local, out, inc = [], [], []
        for t in range(self.n):
            local.append(pltpu.make_async_copy(self.src_view(t, src_refs[t], me), self.dst_view(t, dst_refs[t], me),
                                               local_sems.at[t]))
            for k, (dev, idx) in enumerate(peers):
                mk = lambda s, d: pltpu.make_async_remote_copy(
                    src_ref=s, dst_ref=d, send_sem=send_sems.at[t, k], recv_sem=recv_sems.at[t, k], device_id=dev,
                    device_id_type=pl.DeviceIdType.MESH)
                out.append(mk(self.src_view(t, src_refs[t], idx), self.dst_view(t, dst_refs[t], me)))
                inc.append(mk(self.src_view(t, src_refs[t], me), self.dst_view(t, dst_refs[t], idx)))
        return local, out, inc

    def start(self, grid, src_refs, dst_refs, sems):
        first = pl.program_id(0) == 0
        for a in range(1, len(grid)):
            first = jnp.logical_and(first, pl.program_id(a) == 0)

        @pl.when(first)
        def _():
            local, out, _ = self.copies(src_refs, dst_refs, sems)
            for cp in local + out:
                cp.start()

    def finish(self, grid, src_refs, dst_refs, sems):
        last = pl.program_id(0) == grid[0] - 1
        for a in range(1, len(grid)):
            last = jnp.logical_and(last, pl.program_id(a) == grid[a] - 1)

        @pl.when(last)
        def _():
            local, out, inc = self.copies(src_refs, dst_refs, sems)
            for cp in inc:
                cp.wait_recv()
            for cp in out:
                cp.wait_send()
            for cp in local:
                cp.wait()


def _block_fwd(fn, ins, in_specs, out_shapes, out_specs, grid, name, ride=None):
    n_in, n_out = len(ins), len(out_shapes)
    n_ride = ride.n if ride else 0

    def body(*refs):
        pids = tuple(pl.program_id(a) for a in range(len(grid)))
        ride_refs = (refs[n_in:n_in + n_ride], refs[n_in + n_ride + n_out:n_in + 2 * n_ride + n_out],
                     refs[n_in + 2 * n_ride + n_out:])
        if ride:
            ride.start(grid, *ride_refs)
        vals = [r[...].astype(F32) for r in refs[:n_in]]
        outs = fn(pids, *vals)
        for r, o in zip(refs[n_in + n_ride:n_in + n_ride + n_out], outs):
            r[...] = o.astype(r.dtype)
        if ride:
            ride.finish(grid, *ride_refs)

    if not ride:
        return pl.pallas_call(
            body, grid=grid, in_specs=in_specs, out_specs=out_specs, out_shape=out_shapes, name=name,
            compiler_params=_cparams(("parallel",) * len(grid)))(*ins)
    res = pl.pallas_call(
        body, grid=grid, in_specs=list(in_specs) + ride.any_specs, out_specs=list(out_specs) + ride.any_specs,
        out_shape=list(out_shapes) + ride.dst_shapes, scratch_shapes=ride.scratch, name=name,
        compiler_params=_cparams(("arbitrary",) * len(grid)))(*ins, *ride.srcs)
    return res[:n_out], res[n_out:]


def _block_bwd(fn, ins, in_specs, cts, ct_specs, kinds, g_shapes, g_specs, grid, name, ride=None):
    n_in, n_ct, n_g = len(ins), len(cts), len(g_shapes)
    n_ride = ride.n if ride else 0
    didx = [i for i, k in enumerate(kinds) if k]

    def body(*refs):
        in_refs, ct_refs = refs[:n_in], refs[n_in:n_in + n_ct]
        base = n_in + n_ct
        g_refs = refs[base + n_ride:base + n_ride + n_g]
        ride_refs = (refs[base:base + n_ride], refs[base + n_ride + n_g:base + 2 * n_ride + n_g],
                     refs[base + 2 * n_ride + n_g:])
        if ride:
            ride.start(grid, *ride_refs)
        pids = tuple(pl.program_id(a) for a in range(len(grid)))
        vals = [r[...].astype(F32) for r in in_refs]

        def f(*dvals):
            full = list(vals)
            for i, v in zip(didx, dvals):
                full[i] = v
            return tuple(fn(pids, *full))

        _, vjp = jax.vjp(f, *[vals[i] for i in didx])
        gs = vjp(tuple(r[...].astype(F32) for r in ct_refs))
        first = pids[0] == 0
        for p in pids[1:]:
            first = jnp.logical_and(first, p == 0)
        for i, g, r in zip(didx, gs, g_refs):
            if kinds[i] == "s":
                r[...] = g.astype(r.dtype)
            else:
                @pl.when(first)
                def _(r=r):
                    r[...] = jnp.zeros(r.shape, r.dtype)

                r[...] += g.astype(r.dtype)
        if ride:
            ride.finish(grid, *ride_refs)

    sem = ("arbitrary",) * len(grid) if "a" in kinds or ride else ("parallel",) * len(grid)
    if not ride:
        return pl.pallas_call(
            body, grid=grid, in_specs=list(in_specs) + list(ct_specs), out_specs=g_specs, out_shape=g_shapes,
            name=name, compiler_params=_cparams(sem))(*ins, *cts)
    res = pl.pallas_call(
        body, grid=grid, in_specs=list(in_specs) + list(ct_specs) + ride.any_specs,
        out_specs=list(g_specs) + ride.any_specs, out_shape=list(g_shapes) + ride.dst_shapes,
        scratch_shapes=ride.scratch, name=name, compiler_params=_cparams(sem))(*ins, *cts, *ride.srcs)
    return res[:n_g], res[n_g:]


def _rows(tb, width, col=0):
    return pl.BlockSpec((tb, width), lambda i, col=col: (i, col))


def _whole(shape):
    return pl.BlockSpec(shape, lambda *_: (0,) * len(shape))


def _sds(shape, dtype=F32):
    return jax.ShapeDtypeStruct(shape, dtype)


def _matmul(a, b, mode, out_dtypes, name, epi=None, extras=(), tm=1024, tn=1024, tk=1024,
            b_spec=None, n_total=None, out_specs=None, out_shapes=None):
    if mode == "tn":
        k_total, m_total = a.shape
    else:
        m_total, k_total = a.shape
    if n_total is None:
        n_total = b.shape[0] if mode == "nt" else b.shape[1]
    tm, tn, tk = min(tm, m_total), min(tn, n_total), min(tk, k_total)
    assert m_total % tm == 0 and n_total % tn == 0 and k_total % tk == 0, (name, a.shape, b.shape)
    grid = (m_total // tm, n_total // tn, k_total // tk)
    nk = grid[2]
    if mode == "tn":
        a_spec = pl.BlockSpec((tk, tm), lambda i, j, k: (k, i))
    else:
        a_spec = pl.BlockSpec((tm, tk), lambda i, j, k: (i, k))
    if b_spec is None:
        if mode == "nt":
            b_spec = pl.BlockSpec((tn, tk), lambda i, j, k: (j, k))
        else:
            b_spec = pl.BlockSpec((tk, tn), lambda i, j, k: (k, j))
    tile = pl.BlockSpec((tm, tn), lambda i, j, k: (i, j))
    n_ex, n_out = len(extras), len(out_dtypes)
    ca, cb = _CONTRACT[mode]
    dims = (((ca,), (cb,)), ((), ()))

    chunk = 256
    n_chunks = tm // chunk if (epi is not None and mode != "tn" and tm % chunk == 0) else 1

    def body(*refs):
        a_ref, b_ref = refs[:2]
        ex_refs = refs[2:2 + n_ex]
        out_refs = refs[2 + n_ex:2 + n_ex + n_out]
        acc = refs[-1] if nk > 1 else None
        full = lambda: lax.dot_general(a_ref[...], b_ref[...], dims, preferred_element_type=F32)

        def last_step():
            for c in range(n_chunks):
                rows = pl.ds(c * chunk, chunk) if n_chunks > 1 else slice(None)
                val = (lax.dot_general(a_ref[rows, :], b_ref[...], dims, preferred_element_type=F32)
                       if n_chunks > 1 else full())
                if acc is not None:
                    val = val + acc[rows, :]
                res = epi(val, *[e[rows, :] for e in ex_refs]) if epi is not None else (val,)
                for r, o in zip(out_refs, res):
                    r[rows, :] = o.astype(r.dtype)

        if nk == 1:
            last_step()
        else:
            k = pl.program_id(2)

            @pl.when(k == 0)
            def _():
                acc[...] = full()

            if nk > 2:
                @pl.when(jnp.logical_and(k > 0, k < nk - 1))
                def _():
                    acc[...] += full()

            @pl.when(k == nk - 1)
            def _():
                last_step()

    if out_shapes is None:
        out_shapes = [_sds((m_total, n_total), d) for d in out_dtypes]
        out_specs = [tile] * n_out
    outs = pl.pallas_call(
        body, grid=grid, in_specs=[a_spec, b_spec] + [tile] * n_ex, out_specs=out_specs, out_shape=out_shapes,
        scratch_shapes=[pltpu.VMEM((tm, tn), F32)] if nk > 1 else [], name=name,
        compiler_params=_cparams(("parallel", "parallel", "arbitrary")))(a, b, *extras)
    return outs if n_out > 1 else outs[0]


def _matmul_ln(a, b, h, g, beta, name, tm=1024, tk=2048, chunk=256):
    m_total, k_total = a.shape
    n = b.shape[1]
    tm, tk = min(tm, m_total), min(tk, k_total)
    chunk = min(chunk, tm)
    assert m_total % tm == 0 and k_total % tk == 0 and tm % chunk == 0, (name, a.shape, b.shape)
    nk = k_total // tk
    dims = (((1,), (0,)), ((), ()))

    def body(a_ref, b_ref, h_ref, g_ref, beta_ref, mix_ref, y_ref, yb_ref, *acc):
        k = pl.program_id(1)

        def finish():
            for c in range(tm // chunk):
                rows = pl.ds(chunk * c, chunk)
                val = lax.dot_general(a_ref[rows, :], b_ref[...], dims, preferred_element_type=F32)
                if nk > 1:
                    val = val + acc[0][rows, :]
                mix_ref[rows, :] = val
                (y,) = _ln_fn(None, h_ref[rows, :], val, g_ref[...], beta_ref[...])
                y_ref[rows, :] = y
                yb_ref[rows, :] = y.astype(BF16)

        if nk == 1:
            finish()
        else:
            part = lambda: lax.dot_general(a_ref[...], b_ref[...], dims, preferred_element_type=F32)

            @pl.when(k == 0)
            def _():
                acc[0][...] = part()

            if nk > 2:
                @pl.when(jnp.logical_and(k > 0, k < nk - 1))
                def _():
                    acc[0][...] += part()

            @pl.when(k == nk - 1)
            def _():
                finish()

    tile = pl.BlockSpec((tm, n), lambda i, k: (i, 0))
    row = pl.BlockSpec((1, n), lambda i, k: (0, 0))
    return pl.pallas_call(
        body, grid=(m_total // tm, nk),
        in_specs=[pl.BlockSpec((tm, tk), lambda i, k: (i, k)), pl.BlockSpec((tk, n), lambda i, k: (k, 0)), tile, row,
                  row],
        out_specs=[tile, tile, tile],
        out_shape=[_sds((m_total, n)), _sds((m_total, n)), _sds((m_total, n), BF16)],
        scratch_shapes=[pltpu.VMEM((tm, n), F32)] if nk > 1 else [], name=name,
        compiler_params=_cparams(("parallel", "arbitrary")))(a, b, h, g, beta)


def _silu(x):
    return x * jax.nn.sigmoid(x)


def _rowa_fn(pids, c, ab, alog, dtb):
    tb = c.shape[0]
    s = _silu(c)
    qs, ks = [], []
    for h in range(DN_HEADS):
        qh = s[:, DN_D * h:DN_D * (h + 1)]
        qs.append(qh * lax.rsqrt(jnp.sum(qh * qh, axis=-1, keepdims=True) + NORM_EPS) * (DN_D ** -0.5))
        kh = s[:, DN_W + DN_D * h:DN_W + DN_D * (h + 1)]
        ks.append(kh * lax.rsqrt(jnp.sum(kh * kh, axis=-1, keepdims=True) + NORM_EPS))
    q = jnp.concatenate(qs, axis=1)
    k = jnp.concatenate(ks, axis=1)
    v = s[:, 2 * DN_W:3 * DN_W]
    g128 = -jnp.exp(alog) * _softplus(ab + dtb)
    b128 = jax.nn.sigmoid(ab)
    r, cc = _iota((tb, tb), 0), _iota((tb, tb), 1)
    tri = jnp.where(((r >> 6) == (cc >> 6)) & (r >= cc), 1.0, 0.0)
    gc128 = mm(tri, g128, "nn", "sela")
    lane, col = _iota((128, DN_W), 0), _iota((128, DN_W), 1)
    exp_a = jnp.where(lane == (col >> 7), 1.0, 0.0)
    exp_b = jnp.where(lane == (col >> 7) + DN_HEADS, 1.0, 0.0)
    return q, k, v, mm(gc128, exp_a, "nn", "selb"), mm(b128, exp_b, "nn", "selb")


def _tri_inv_raw(lows, block):
    n = lows[0].shape[0]
    r, c = _iota((n, n), 0), _iota((n, n), 1)
    lg = 0
    xs = None
    while (1 << lg) < block:
        off = ((r >> (lg + 1)) == (c >> (lg + 1))) & (((r >> lg) & 1) == 1) & (((c >> lg) & 1) == 0)
        cblks = [jnp.where(off, low, 0.0) for low in lows]
        if xs is None:
            xs = [jnp.where(r == c, 1.0, 0.0) - cb for cb in cblks]
        else:
            ys = [mm(cb, x, "nn", "f32") for cb, x in zip(cblks, xs)]
            xs = [x - mm(x, y, "nn", "f32") for x, y in zip(xs, ys)]
        lg += 1
    return tuple(xs)


def _tri_inv_cotangent(block, xs, cts):
    n = xs[0].shape[0]
    r, c = _iota((n, n), 0), _iota((n, n), 1)
    shift = block.bit_length() - 1
    keep = ((r >> shift) == (c >> shift)) & (r > c)
    gs = [mm(x, ct, "tn", "f32") for x, ct in zip(xs, cts)]
    gs = [mm(g, x, "nt", "f32") for g, x in zip(gs, xs)]
    return tuple(jnp.where(keep, -g, 0.0) for g in gs)


@functools.partial(jax.custom_vjp, nondiff_argnums=(1,))
def _tri_inv(lows, block):
    return _tri_inv_raw(lows, block)


def _tri_inv_fwd(lows, block):
    xs = _tri_inv_raw(lows, block)
    return xs, xs


def _tri_inv_bwd(block, xs, cts):
    return (_tri_inv_cotangent(block, xs, cts),)


_tri_inv.defvjp(_tri_inv_fwd, _tri_inv_bwd)


@functools.partial(jax.custom_vjp, nondiff_argnums=(2,))
def _tri_inv_known(lows, known, block):
    return known


def _tri_inv_known_fwd(lows, known, block):
    return known, known


def _tri_inv_known_bwd(block, xs, cts):
    return _tri_inv_cotangent(block, xs, cts), tuple(jnp.zeros_like(x) for x in xs)


_tri_inv_known.defvjp(_tri_inv_known_fwd, _tri_inv_known_bwd)


PAIR = 2 * CHUNK


def _dn1_pairs(q, k, v, gc, beta, tinv_known=None):
    assert PAIR == DN_D
    n = PAIR
    pairs = range(q.shape[0] // n)
    cut = lambda t: [t[n * j:n * (j + 1)] for j in pairs]
    q, k, v, gc, beta = cut(q), cut(k), cut(v), cut(gc), cut(beta)
    onehot = jnp.where(_iota((n, DN_D), 1) == 0, 1.0, 0.0)
    r, c = _iota((n, n), 0), _iota((n, n), 1)
    same = (r >> 6) == (c >> 6)
    incl, strict = same & (r >= c), same & (r > c)
    row = _iota((n, DN_D), 0)
    eg = [jnp.exp(g) for g in gc]
    kb = [k[j] * beta[j] for j in pairs]
    g_row = [mm(onehot, g, "nt", "sela") for g in gc]
    kk = [mm(kb[j], k[j], "nt", "bf16") for j in pairs]
    qk = [mm(q[j], k[j], "nt", "bf16") for j in pairs]
    decay = [jnp.exp(jnp.where(incl, gc[j] - g_row[j], NEG_BIG)) for j in pairs]
    low = tuple(jnp.where(strict, kk[j] * decay[j], 0.0) for j in pairs)
    if tinv_known is None:
        tinv = _tri_inv(low, CHUNK)
    else:
        tinv = _tri_inv_known(low, tuple(cut(tinv_known)), CHUNK)
    uw = [mm(tinv[j], jnp.concatenate([v[j] * beta[j], kb[j] * eg[j]], axis=1), "nn", "f32") for j in pairs]
    intra = [jnp.where(incl, qk[j] * decay[j], 0.0) for j in pairs]
    g_last = []
    for g in gc:
        last0 = jnp.sum(jnp.where(row == CHUNK - 1, g, 0.0), axis=0, keepdims=True)
        last1 = jnp.sum(jnp.where(row == PAIR - 1, g, 0.0), axis=0, keepdims=True)
        g_last.append(jnp.where(row < CHUNK, last0, last1))
    join = lambda parts: jnp.concatenate(parts, axis=0)
    return (join([t[:, :DN_D] for t in uw]), join([t[:, DN_D:] for t in uw]), join(intra),
            join([q[j] * eg[j] for j in pairs]), join([k[j] * jnp.exp(g_last[j] - gc[j]) for j in pairs]),
            join([jnp.exp(g) for g in g_last]), join(list(tinv)))


def _dn1_fn(pids, q, k, v, gc, beta):
    return _dn1_pairs(q, k, v, gc, beta)


def _dn1_fn_known(pids, q, k, v, gc, beta, tinv):
    return _dn1_pairs(q, k, v, gc, beta, tinv)[:6]


def _dn2_step(half, state, qd, kd, u, w, intra, cd_row):
    heads = range(len(state))
    v_new = [u[h] - mm(w[h], state[h], "nn", "bf16") for h in heads]
    zeros = jnp.zeros_like(v_new[0])
    v_pair = [jnp.concatenate([v, zeros] if half == 0 else [zeros, v], axis=0) for v in v_new]
    from_state = [mm(qd[h], state[h], "nn", "bf16") for h in heads]
    out = tuple(from_state[h] + mm(intra[h], v_pair[h], "nn", "bf16") for h in heads)
    return out, tuple(state[h] * cd_row[h] + mm(kd[h], v_new[h], "tn", "bf16") for h in heads)


def _post_fn(pids, o, z, nw):
    outs = []
    for h in range(DN_HEADS):
        oh = o[:, DN_D * h:DN_D * (h + 1)]
        zh = z[:, DN_D * h:DN_D * (h + 1)]
        y = oh * lax.rsqrt(jnp.mean(oh * oh, axis=-1, keepdims=True) + NORM_EPS) * nw
        outs.append(y * _silu(zh))
    return (jnp.concatenate(outs, axis=1),)


def _memattn_fn(pids, qm, kvm):
    kmem, vmem = kvm[:, :MEM_W], kvm[:, MEM_W:]
    lane = _iota((1, MEM_W), 1)
    heads = range(MEM_HEADS)
    hm = [jnp.where((lane >> 6) == h, 1.0, 0.0) for h in heads]
    s = [mm(qm * (hm[h] * MEM_DH ** -0.5), kmem, "nt", "bf16") for h in heads]
    e = [jnp.exp(t - lax.stop_gradient(jnp.max(t, axis=-1, keepdims=True))) for t in s]
    o = [mm(e[h], vmem, "nn", "bf16") * (hm[h] / jnp.sum(e[h], axis=-1, keepdims=True)) for h in heads]
    return ((o[0] + o[1]) + (o[2] + o[3]),)


def _ln_fn(pids, h, mix, g, b):
    x = DN_ALPHA * h + mix
    mu = jnp.mean(x, axis=-1, keepdims=True)
    xc = x - mu
    var = jnp.mean(xc * xc, axis=-1, keepdims=True)
    return (xc * lax.rsqrt(var + LN_EPS) * g + b,)


def _rope_matrix():
    i, j = _iota((128, 128), 0), _iota((128, 128), 1)
    jj = j & 63
    return jnp.where((jj < 32) & (i == j + 32), -1.0, 0.0) + jnp.where((jj >= 32) & (i == j - 32), 1.0, 0.0)


def _rope128(x, cos, sin, rot):
    return x * cos + mm(x, rot, "nn", "selb") * sin


def _krope_fn(pids, kraw, cos, sin):
    rot = _rope_matrix()
    return (jnp.concatenate([_rope128(kraw[:, 128 * g:128 * (g + 1)], cos, sin, rot) for g in range(2)], axis=1),)


def _swa_fn(pids, qraw, cos, sin, k_halo, k_cur, v_halo, v_cur, sinks):
    tb = qraw.shape[0]
    nwin = tb // WINDOW
    rot = _rope_matrix()
    kcat = jnp.concatenate([k_halo, k_cur], axis=0)
    vcat = jnp.concatenate([v_halo, v_cur], axis=0)
    lane = _iota((1, 128), 1)
    halves = (jnp.where(lane < 64, 1.0, 0.0), jnp.where(lane >= 64, 1.0, 0.0))
    group = SWA_HEADS // 2
    rows = group * WINDOW
    in_cur = _iota((rows, WINDOW), 1) <= (_iota((rows, WINDOW), 0) & (WINDOW - 1))
    qg = [_rope128(qraw[:, 128 * p:128 * (p + 1)], cos, sin, rot) for p in range(group)]
    sink = []
    for kv in range(2):
        cols = [jnp.sum(jnp.where(lane == group * kv + i, sinks, 0.0), axis=-1, keepdims=True)
                + jnp.zeros((WINDOW, 1), F32) for i in range(group)]
        sink.append(jnp.concatenate(cols, axis=0))
    units = [(w, kv) for w in range(nwin) for kv in range(2)]
    n_units = range(len(units))
    q6 = [jnp.concatenate([qg[3 * kv + i // 2][WINDOW * w:WINDOW * (w + 1)] * (halves[i % 2] * SWA_DH ** -0.5)
                           for i in range(group)], axis=0) for w, kv in units]
    blk = lambda cat, w, kv: cat[WINDOW * w:WINDOW * (w + 1), 128 * kv:128 * (kv + 1)]
    s_prev = [mm(q6[u], blk(kcat, w, kv), "nt", "bf16") for u, (w, kv) in enumerate(units)]
    s_cur = [mm(q6[u], blk(kcat, w + 1, kv), "nt", "bf16") for u, (w, kv) in enumerate(units)]
    s = [jnp.where(in_cur, s_cur[u], jnp.where(pids[0] * nwin + w > 0, s_prev[u], NEG_BIG))
         for u, (w, kv) in enumerate(units)]
    m = [lax.stop_gradient(jnp.maximum(jnp.max(s[u], axis=-1, keepdims=True), sink[kv]))
         for u, (w, kv) in enumerate(units)]
    e = [jnp.exp(s[u] - m[u]) for u in n_units]
    denom = [jnp.sum(e[u], axis=-1, keepdims=True) + jnp.exp(sink[kv] - m[u]) for u, (w, kv) in enumerate(units)]
    o = [(mm(jnp.where(in_cur, e[u], 0.0), blk(vcat, w + 1, kv), "nn", "bf16")
          + mm(jnp.where(in_cur, 0.0, e[u]), blk(vcat, w, kv), "nn", "bf16")) / denom[u]
         for u, (w, kv) in enumerate(units)]
    out_rows = []
    for w in range(nwin):
        lanes = []
        for p in range(group):
            ou = o[units.index((w, p // 3))]
            i = 2 * (p % 3)
            lanes.append(ou[WINDOW * i:WINDOW * (i + 1)] * halves[0] + ou[WINDOW * (i + 1):WINDOW * (i + 2)] * halves[1])
        out_rows.append(jnp.concatenate(lanes, axis=1))
    return (jnp.concatenate(out_rows, axis=0),)


def _conv_fwd(proj, conv_w, tb):
    t_total = proj.shape[0]
    width = conv_w.shape[1]
    nb = t_total // tb

    def body(cur_ref, prev_ref, w_ref, out_ref):
        i = pl.program_id(0)
        prev = jnp.where(i > 0, prev_ref[...], 0.0)
        xcat = jnp.concatenate([prev, cur_ref[...]], axis=0)
        acc = xcat[8:] * w_ref[3:4, :]
        for j in range(3):
            acc = acc + pltpu.roll(xcat, 3 - j, 0)[8:] * w_ref[j:j + 1, :]
        out_ref[...] = acc

    return pl.pallas_call(
        body, grid=(nb,),
        in_specs=[pl.BlockSpec((tb, width), lambda i: (i, 0)),
                  pl.BlockSpec((8, width), lambda i: (jnp.maximum(i * (tb // 8) - 1, 0), 0)),
                  _whole((4, width))],
        out_specs=pl.BlockSpec((tb, width), lambda i: (i, 0)), out_shape=_sds((t_total, width)),
        name="conv_fwd", compiler_params=_cparams(("parallel",)))(proj, proj, conv_w)


def _conv_bwd(dc, proj, conv_w, tb):
    t_total, width = dc.shape
    nb = t_total // tb

    def body(dcur_ref, dnext_ref, cur_ref, prev_ref, w_ref, dx_ref, dw_ref):
        i = pl.program_id(0)
        dnext = jnp.where(i < nb - 1, dnext_ref[...], 0.0)
        dcur = dcur_ref[...]
        dcat = jnp.concatenate([dcur, dnext], axis=0)
        prev = jnp.where(i > 0, prev_ref[...], 0.0)
        xcat = jnp.concatenate([prev, cur_ref[...]], axis=0)

        @pl.when(i == 0)
        def _():
            dw_ref[...] = jnp.zeros(dw_ref.shape, F32)

        dx = dcur * w_ref[3:4, :]
        dw_ref[3:4, :] += jnp.sum(dcur * xcat[8:], axis=0, keepdims=True)
        for j in range(3):
            dx = dx + pltpu.roll(dcat, 8 - (3 - j), 0)[8:] * w_ref[j:j + 1, :]
            dw_ref[j:j + 1, :] += jnp.sum(dcur * pltpu.roll(xcat, 3 - j, 0)[8:], axis=0, keepdims=True)
        dx_ref[...] = dx.astype(dx_ref.dtype)

    return pl.pallas_call(
        body, grid=(nb,),
        in_specs=[pl.BlockSpec((tb, width), lambda i: (i, 0)),
                  pl.BlockSpec((8, width), lambda i: (jnp.minimum((i + 1) * (tb // 8), t_total // 8 - 1), 0)),
                  pl.BlockSpec((tb, width), lambda i: (i, 0)),
                  pl.BlockSpec((8, width), lambda i: (jnp.maximum(i * (tb // 8) - 1, 0), 0)),
                  _whole((4, width))],
        out_specs=[pl.BlockSpec((tb, width), lambda i: (i, 0)), _whole((4, width))],
        out_shape=[_sds((t_total, width), BF16), _sds((4, width))],
        name="conv_bwd", compiler_params=_cparams(("arbitrary",)))(dc, dc, proj, proj, conv_w)


def _head_spec(tb, nb=None):
    if nb is None:
        return pl.BlockSpec((tb, DN_D), lambda h, i: (i, h))
    return pl.BlockSpec((tb, DN_D), lambda h, i: (nb - 1 - i, h))


def _intra_spec(tb, nb=None):
    if nb is None:
        return pl.BlockSpec((None, tb, PAIR), lambda h, i: (h, i, 0))
    return pl.BlockSpec((None, tb, PAIR), lambda h, i: (h, nb - 1 - i, 0))


def _scan_specs(tb, nb=None):
    blk = (lambda i: i) if nb is None else (lambda i: nb - 1 - i)
    rows = pl.BlockSpec((tb, DN_W), lambda i: (blk(i), 0))
    pair = pl.BlockSpec((DN_HEADS, tb, PAIR), lambda i: (0, blk(i), 0))
    states = pl.BlockSpec((DN_HEADS, tb // CHUNK, DN_D, DN_D), lambda i: (0, blk(i), 0, 0))
    return rows, pair, states


def _dn2_fwd(qd, kd, u, w, intra, cd, tb):
    t_total = qd.shape[0]
    rows, pair, states = _scan_specs(tb)

    def body(qd_ref, kd_ref, u_ref, w_ref, a_ref, cd_ref, o_ref, save_ref, state):
        @pl.when(pl.program_id(0) == 0)
        def _():
            state[...] = jnp.zeros(state.shape, F32)

        heads = range(DN_HEADS)
        lanes = [pl.ds(DN_D * h, DN_D) for h in heads]
        for j in range(tb // CHUNK):
            sl = pl.ds(CHUNK * j, CHUNK)
            s0 = tuple(state[h] for h in heads)
            for h in heads:
                save_ref[h, j] = s0[h]
            per_head = lambda ref: tuple(ref[sl, lanes[h]] for h in heads)
            out, s1 = _dn2_step(j % 2, s0, per_head(qd_ref), per_head(kd_ref), per_head(u_ref), per_head(w_ref),
                                tuple(a_ref[h, sl, :] for h in heads),
                                tuple(cd_ref[pl.ds(CHUNK * j, 1), lanes[h]] for h in heads))
            for h in heads:
                o_ref[sl, lanes[h]] = out[h]
                state[h] = s1[h]

    return pl.pallas_call(
        body, grid=(t_total // tb,), in_specs=[rows, rows, rows, rows, pair, rows],
        out_specs=[rows, states],
        out_shape=[_sds((t_total, DN_W)), _sds((DN_HEADS, t_total // CHUNK, DN_D, DN_D))],
        scratch_shapes=[pltpu.VMEM((DN_HEADS, DN_D, DN_D), F32)], name="dn2_fwd",
        compiler_params=_cparams(("arbitrary",)))(qd, kd, u, w, intra, cd)


def _dn2_bwd(qd, kd, u, w, intra, cd, saved, d_o, tb):
    t_total = qd.shape[0]
    nb = t_total // tb
    rows, pair, states = _scan_specs(tb, nb)

    def body(qd_ref, kd_ref, u_ref, w_ref, a_ref, cd_ref, save_ref, do_ref,
             dqd_ref, dkd_ref, du_ref, dw_ref, da_ref, dcd_ref, dstate):
        @pl.when(pl.program_id(0) == 0)
        def _():
            dstate[...] = jnp.zeros(dstate.shape, F32)

        first_row = _iota((CHUNK, DN_D), 0) == 0
        heads = range(DN_HEADS)
        lanes = [pl.ds(DN_D * h, DN_D) for h in heads]
        for j in reversed(range(tb // CHUNK)):
            sl = pl.ds(CHUNK * j, CHUNK)
            per_head = lambda ref: tuple(ref[sl, lanes[h]] for h in heads)
            _, vjp = jax.vjp(functools.partial(_dn2_step, j % 2), tuple(save_ref[h, j] for h in heads),
                             per_head(qd_ref), per_head(kd_ref), per_head(u_ref), per_head(w_ref),
                             tuple(a_ref[h, sl, :] for h in heads),
                             tuple(cd_ref[pl.ds(CHUNK * j, 1), lanes[h]] for h in heads))
            ds0, dqd, dkd, du, dw, da, dcd = vjp((per_head(do_ref), tuple(dstate[h] for h in heads)))
            for h in heads:
                dqd_ref[sl, lanes[h]] = dqd[h]
                dkd_ref[sl, lanes[h]] = dkd[h]
                du_ref[sl, lanes[h]] = du[h]
                dw_ref[sl, lanes[h]] = dw[h]
                da_ref[h, sl, :] = da[h]
                dcd_ref[sl, lanes[h]] = jnp.where(first_row, dcd[h], 0.0)
                dstate[h] = ds0[h]

    full = _sds((t_total, DN_W))
    return pl.pallas_call(
        body, grid=(nb,),
        in_specs=[rows, rows, rows, rows, pair, rows, states, rows],
        out_specs=[rows, rows, rows, rows, pair, rows],
        out_shape=[full, full, full, full, _sds((DN_HEADS, t_total, PAIR)), full],
        scratch_shapes=[pltpu.VMEM((DN_HEADS, DN_D, DN_D), F32)], name="dn2_bwd",
        compiler_params=_cparams(("arbitrary",)))(qd, kd, u, w, intra, cd, saved, d_o)


def _loss_and_grad(y, target, tb):
    t_total, d = y.shape

    def body(y_ref, t_ref, dy_ref, acc_ref):
        @pl.when(pl.program_id(0) == 0)
        def _():
            acc_ref[...] = jnp.zeros(acc_ref.shape, F32)

        err = y_ref[...] - t_ref[...]
        dy_ref[...] = err * (1.0 / d)
        acc_ref[...] += jnp.sum(err * err, axis=0, keepdims=True)

    dy, acc = pl.pallas_call(
        body, grid=(t_total // tb,), in_specs=[_rows(tb, d), _rows(tb, d)],
        out_specs=[_rows(tb, d), _whole((1, d))], out_shape=[_sds((t_total, d)), _sds((1, d))],
        name="loss", compiler_params=_cparams(("arbitrary",)))(y, target)
    return 0.5 * jnp.sum(acc) / d, dy


def _halo_sum(mains, halos, tb):
    t_total, width = mains[0].shape
    nb = t_total // tb
    n = len(mains)

    def body(*refs):
        out_ref = refs[-1]
        i = pl.program_id(0)
        tot = refs[0][...]
        for r in refs[1:n]:
            tot = tot + r[...]
        hal = refs[n][...]
        for r in refs[n + 1:2 * n]:
            hal = hal + r[...]
        hal = jnp.where(i < nb - 1, hal, 0.0)
        out_ref[...] = tot + jnp.concatenate([jnp.zeros((tb - WINDOW, width), F32), hal], axis=0)

    return pl.pallas_call(
        body, grid=(nb,),
        in_specs=[_rows(tb, width)] * n
        + [pl.BlockSpec((None, WINDOW, width), lambda i: (jnp.minimum(i + 1, nb - 1), 0, 0))] * n,
        out_specs=_rows(tb, width), out_shape=_sds((t_total, width)), name="halo_sum",
        compiler_params=_cparams(("parallel",)))(*mains, *halos)


def _adamw(recvs, w, m, v, tr, name):
    slots, _, c_total = recvs[0].shape
    r_total = w.shape[0]
    assert sum(r.shape[1] for r in recvs) == r_total
    tr = min([tr] + [r.shape[1] for r in recvs])
    assert all(r.shape[1] % tr == 0 for r in recvs)
    starts = [sum(r.shape[1] for r in recvs[:i]) // tr for i in range(len(recvs))]
    counts = [r.shape[1] // tr for r in recvs]
    c1 = 1.0 / (1.0 - ADAM_B1 ** ADAM_STEP)
    c2 = 1.0 / (1.0 - ADAM_B2 ** ADAM_STEP)

    def body(*refs):
        recv_refs = refs[:len(recvs)]
        w_ref, m_ref, v_ref, g_ref, d_ref, nm_ref, nv_ref = refs[len(recvs):]
        g = None
        for recv_ref, start in zip(recv_refs, starts):
            part = recv_ref[0].astype(F32)
            for s in range(1, slots):
                part = part + recv_ref[s].astype(F32)
            g = part if g is None else jnp.where(pl.program_id(0) >= start, part, g)
        nm = ADAM_B1 * m_ref[...] + (1.0 - ADAM_B1) * g
        nv = ADAM_B2 * v_ref[...] + (1.0 - ADAM_B2) * (g * g)
        g_ref[...] = g
        nm_ref[...] = nm
        nv_ref[...] = nv
        d_ref[...] = -ADAM_LR * ((nm * c1) / (jnp.sqrt(nv * c2) + ADAM_EPS) + ADAM_WD * w_ref[...])

    blk = pl.BlockSpec((tr, c_total), lambda i: (i, 0))
    recv_specs = [pl.BlockSpec((slots, tr, c_total), lambda i, s=s, n=n: (0, jnp.clip(i - s, 0, n - 1), 0))
                  for s, n in zip(starts, counts)]
    return pl.pallas_call(
        body, grid=(r_total // tr,), in_specs=recv_specs + [blk, blk, blk],
        out_specs=[blk] * 4, out_shape=[_sds((r_total, c_total))] * 4, name=name,
        compiler_params=_cparams(("parallel",)))(*recvs, w, m, v)


def _me_and_peers():
    x, y, c = lax.axis_index("x"), lax.axis_index("y"), lax.axis_index("c")
    me = 4 * x + 2 * y + c
    peers = []
    for k in range(1, N_DEV):
        px = 1 - x if (k >> 2) & 1 else x
        py = 1 - y if (k >> 1) & 1 else y
        pc = 1 - c if k & 1 else c
        peers.append(((px, py, pc), 4 * px + 2 * py + pc))
    return me, peers


def _small_exchange(packed, reduce):
    r_total = packed.shape[0]

    def body(p_ref, out_ref, gath_ref, send_sems, recv_sems):
        me, peers = _me_and_peers()
        gath_ref[me] = p_ref[...]
        copies = []
        for k, (dev, _) in enumerate(peers):
            cp = pltpu.make_async_remote_copy(src_ref=p_ref, dst_ref=gath_ref.at[me], send_sem=send_sems.at[k],
                                              recv_sem=recv_sems.at[k], device_id=dev,
                                              device_id_type=pl.DeviceIdType.MESH)
            cp.start()
            copies.append(cp)
        for k, (dev, idx) in enumerate(peers):
            pltpu.make_async_remote_copy(src_ref=p_ref, dst_ref=gath_ref.at[idx], send_sem=send_sems.at[k],
                                         recv_sem=recv_sems.at[k], device_id=dev,
                                         device_id_type=pl.DeviceIdType.MESH).wait_recv()
        for cp in copies:
            cp.wait_send()
        if reduce:
            tot = gath_ref[0]
            for d in range(1, N_DEV):
                tot = tot + gath_ref[d]
            out_ref[...] = tot
        else:
            out_ref[...] = gath_ref[...]

    out_shape = _sds((r_total, 128)) if reduce else _sds((N_DEV, r_total, 128))
    return pl.pallas_call(
        body, in_specs=[pl.BlockSpec(memory_space=pltpu.VMEM)], out_specs=pl.BlockSpec(memory_space=pltpu.VMEM),
        out_shape=out_shape,
        scratch_shapes=[pltpu.VMEM((N_DEV, r_total, 128), F32), pltpu.SemaphoreType.DMA((N_DEV - 1,)),
                        pltpu.SemaphoreType.DMA((N_DEV - 1,))],
        name="small_allreduce" if reduce else "small_allgather")(packed)


def _slot(ref, axis, idx, size):
    sel = [slice(None)] * len(ref.shape)
    sel[axis] = idx if size is None else pl.ds(pl.multiple_of(idx * size, size), size)
    return ref.at[tuple(sel)]


def _big_exchange(srcs, dst_shapes, src_view, dst_view, name):
    n = len(srcs)

    def body(*refs):
        src_refs, dst_refs = refs[:n], refs[n:2 * n]
        send_sems, recv_sems, local_sems = refs[2 * n:]
        me, peers = _me_and_peers()
        local, remote = [], []
        for t in range(n):
            loc = pltpu.make_async_copy(src_view(t, src_refs[t], me), dst_view(t, dst_refs[t], me), local_sems.at[t])
            loc.start()
            local.append(loc)
            for k, (dev, idx) in enumerate(peers):
                cp = pltpu.make_async_remote_copy(
                    src_ref=src_view(t, src_refs[t], idx), dst_ref=dst_view(t, dst_refs[t], me),
                    send_sem=send_sems.at[t, k], recv_sem=recv_sems.at[t, k], device_id=dev,
                    device_id_type=pl.DeviceIdType.MESH)
                cp.start()
                remote.append(cp)
        for t in range(n):
            for k, (dev, idx) in enumerate(peers):
                pltpu.make_async_remote_copy(
                    src_ref=src_view(t, src_refs[t], me), dst_ref=dst_view(t, dst_refs[t], idx),
                    send_sem=send_sems.at[t, k], recv_sem=recv_sems.at[t, k], device_id=dev,
                    device_id_type=pl.DeviceIdType.MESH).wait_recv()
        for cp in remote:
            cp.wait_send()
        for cp in local:
            cp.wait()

    any_spec = pl.BlockSpec(memory_space=pl.ANY)
    return pl.pallas_call(
        body, in_specs=[any_spec] * n, out_specs=[any_spec] * n, out_shape=dst_shapes,
        scratch_shapes=[pltpu.SemaphoreType.DMA((n, N_DEV - 1)), pltpu.SemaphoreType.DMA((n, N_DEV - 1)),
                        pltpu.SemaphoreType.DMA((n,))],
        name=name)(*srcs)


BIG = {
    "a_w_in": (1, (2, 1024, A_IN)),
    "b_w_in": (1, (2, 1024, 1024)),
    "w_kv_shared": (0, (1024, 256)),
    "mem_w_kv": (1, (4, 1024, 512)),
    "w_o": (1, (4, 1024, 1024)),
    "mlp_w_up": (2, (4, 1024, 4096)),
    "mlp_w_down": (1, (4, 4096, 1024)),
}
BIG_NAMES = tuple(BIG)


def _gather_plan(names, shards):
    dst_shapes, axes, sizes = [], [], []
    for name, s in zip(names, shards):
        axis = BIG[name][0] - (len(BIG[name][1]) - s.ndim)
        dst_shapes.append(_sds(tuple(d * N_DEV if a == axis else d for a, d in enumerate(s.shape)), s.dtype))
        axes.append(axis)
        sizes.append(s.shape[axis])
    return dst_shapes, axes, sizes


def _gather_ride(names, shards):
    dst_shapes, axes, sizes = _gather_plan(names, shards)
    return _Ride(shards, dst_shapes, lambda t, ref, idx: ref, lambda t, ref, idx: _slot(ref, axes[t], idx, sizes[t]))


def _scatter_plan(names, grads):
    dst_shapes, axes, sizes = [], [], []
    for name, g in zip(names, grads):
        axis = BIG[name][0] - (len(BIG[name][1]) - g.ndim)
        shard = tuple(d // N_DEV if a == axis else d for a, d in enumerate(g.shape))
        axes.append(axis)
        sizes.append(shard[axis])
        dst_shapes.append(_sds((N_DEV,) + shard, g.dtype))
    return dst_shapes, axes, sizes


def _scatter_ride(names, grads):
    dst_shapes, axes, sizes = _scatter_plan(names, grads)
    return _Ride(grads, dst_shapes, lambda t, ref, idx: _slot(ref, axes[t], idx, sizes[t]),
                 lambda t, ref, idx: ref.at[idx])


def _allgather_weights(names, shards):
    dst_shapes, axes, sizes = _gather_plan(names, shards)
    n = len(shards)

    def body(*refs):
        src_refs, dst_refs = refs[:n], refs[n:2 * n]
        send_sems, recv_sems, local_sems = refs[2 * n:]
        x, y, c = lax.axis_index("x"), lax.axis_index("y"), lax.axis_index("c")
        sibling = (x, y, 1 - c)
        chips = [(1 - x, y), (x, 1 - y), (1 - x, 1 - y)]
        index = lambda px, py, pc: 4 * px + 2 * py + pc

        def copy(t, k, block, to, src=None):
            rows = _slot(dst_refs[t], axes[t], index(*block), sizes[t])
            return pltpu.make_async_remote_copy(
                src_ref=rows if src is None else src, dst_ref=rows, send_sem=send_sems.at[t, k],
                recv_sem=recv_sems.at[t, k], device_id=to, device_id_type=pl.DeviceIdType.MESH)

        started, local = [], []
        for t in range(n):
            mine = pltpu.make_async_copy(src_refs[t], _slot(dst_refs[t], axes[t], index(x, y, c), sizes[t]),
                                         local_sems.at[t])
            mine.start()
            local.append(mine)
            first = [copy(t, 0, (x, y, c), sibling, src=src_refs[t])]
            first += [copy(t, 1 + j, (x, y, c), (*chip, c), src=src_refs[t]) for j, chip in enumerate(chips)]
            for cp in first:
                cp.start()
            started += first
        for t in range(n):
            for j, chip in enumerate(chips):
                copy(t, 1 + j, (*chip, c), (x, y, c)).wait_recv()
                passed = copy(t, 4 + j, (*chip, c), sibling)
                passed.start()
                started.append(passed)
        for t in range(n):
            copy(t, 0, sibling, (x, y, c)).wait_recv()
            for j, chip in enumerate(chips):
                copy(t, 4 + j, (*chip, 1 - c), (x, y, c)).wait_recv()
        for cp in started:
            cp.wait_send()
        for cp in local:
            cp.wait()

    any_spec = pl.BlockSpec(memory_space=pl.ANY)
    return pl.pallas_call(
        body, in_specs=[any_spec] * n, out_specs=[any_spec] * n, out_shape=dst_shapes,
        scratch_shapes=[pltpu.SemaphoreType.DMA((n, N_DEV - 1)), pltpu.SemaphoreType.DMA((n, N_DEV - 1)),
                        pltpu.SemaphoreType.DMA((n,))],
        name="allgather_weights")(*shards)


def _scatter_grads(names, grads):
    dst_shapes, axes, sizes = _scatter_plan(names, grads)

    def src_view(t, ref, idx):
        return _slot(ref, axes[t], idx, sizes[t])

    def dst_view(t, ref, idx):
        return ref.at[idx]

    return _big_exchange(grads, dst_shapes, src_view, dst_view, "scatter_grads")


def _pad_row(vec, width=128):
    return jnp.pad(vec.astype(F32), (0, width - vec.shape[0])).reshape(1, width)


def _block_sizes(t_total):
    return dict(row=min(256, t_total), dn=min(512, t_total), swa=min(256, t_total), scan=min(256, t_total),
                ln=min(512, t_total))


def _ln_grad(h, mix, g, b, dy, tb):
    t_total, d = h.shape

    def body(h_ref, mix_ref, g_ref, dy_ref, dh_ref, dmix_ref, dg_ref, db_ref):
        @pl.when(pl.program_id(0) == 0)
        def _():
            dg_ref[...] = jnp.zeros(dg_ref.shape, F32)
            db_ref[...] = jnp.zeros(db_ref.shape, F32)

        x = DN_ALPHA * h_ref[...] + mix_ref[...]
        xc = x - jnp.mean(x, axis=-1, keepdims=True)
        rstd = lax.rsqrt(jnp.mean(xc * xc, axis=-1, keepdims=True) + LN_EPS)
        xhat = xc * rstd
        dy_val = dy_ref[...]
        dxh = dy_val * g_ref[...]
        m1 = jnp.mean(dxh, axis=-1, keepdims=True)
        m2 = jnp.mean(dxh * xhat, axis=-1, keepdims=True)
        dx = (dxh - m1 - xhat * m2) * rstd
        dmix_ref[...] = dx.astype(dmix_ref.dtype)
        dh_ref[...] = DN_ALPHA * dx
        dg_ref[...] += jnp.sum(dy_val * xhat, axis=0, keepdims=True)
        db_ref[...] += jnp.sum(dy_val, axis=0, keepdims=True)

    return pl.pallas_call(
        body, grid=(t_total // tb,),
        in_specs=[_rows(tb, d), _rows(tb, d), _whole((1, d)), _rows(tb, d)],
        out_specs=[_rows(tb, d), _rows(tb, d), _whole((1, d)), _whole((1, d))],
        out_shape=[_sds((t_total, d)), _sds((t_total, d), BF16), _sds((1, d)), _sds((1, d))],
        name="ln_bwd", compiler_params=_cparams(("arbitrary",)))(h, mix, g, dy)


def _memattn_specs(tb, qcol):
    return [pl.BlockSpec((tb, MEM_W), lambda i: (i, qcol)), _whole((MEM_W, 2 * MEM_W))]


def _act_epilogue(acc):
    r = jnp.maximum(acc, 0.0)
    return (r * r,)


def _dact_epilogue(acc, act):
    return (acc * (2.0 * jnp.sqrt(act.astype(F32))),)


def _add_epilogue(acc, other):
    return (acc + other,)


def _key(name, layer):
    if name == "w_kv_shared":
        return (name, None)
    return (name, layer - N_A if name == "b_w_in" else layer)


_PER_LAYER = ("mem_w_kv", "w_o", "mlp_w_up", "mlp_w_down")
GATHER_FIRST = [_key("a_w_in", 0)]
GATHER_ON_ROWA0 = [_key(n, 0) for n in _PER_LAYER]
GATHER_ON_DN1_0 = [_key("a_w_in", 1), _key("w_kv_shared", 1)] + [_key(n, 1) for n in _PER_LAYER]
GATHER_ON_DN1_1 = [_key("b_w_in", 2), _key("b_w_in", 3)] + [_key(n, l) for l in (2, 3) for n in _PER_LAYER]
SCATTER_ON_DN1_BWD_1 = GATHER_ON_DN1_1
SCATTER_ON_DN1_BWD_0 = GATHER_ON_DN1_0 + GATHER_ON_ROWA0
SCATTER_LAST = GATHER_FIRST
ALL_KEYS = GATHER_FIRST + GATHER_ON_ROWA0 + GATHER_ON_DN1_0 + GATHER_ON_DN1_1


def _local_step(x, mem, positions, target, ready, shards, small):
    t_total = x.shape[0]
    bs = _block_sizes(t_total)
    tb, tdn, tsw = bs["row"], bs["dn"], bs["swa"]
    nb = t_total // tb
    nbs = t_total // tsw

    inv_freq = ROPE_THETA ** (-jnp.arange(0, SWA_DH, 2, dtype=F32) / SWA_DH)
    ang = positions.astype(F32)[:, None] * inv_freq
    cos = jnp.tile(jnp.cos(ang), (1, 4))
    sin = jnp.tile(jnp.sin(ang), (1, 4))

    mem_b = mem.astype(BF16)
    ready = dict(ready)
    derived = {}

    def gather_ride(keys):
        names = [k[0] for k in keys]
        return names, (_gather_ride(names, [shards[k] for k in keys]) if shards is not None else None)

    def weight(name, l):
        key = _key(name, l)
        if key not in derived:
            w = ready[key]
            if name == "a_w_in":
                w = jnp.concatenate([w[0][:, :3072], w[0][:, 3084:], w[0][:, 3072:3084],
                                     jnp.zeros((D_MODEL, A_IN_PAD - A_IN), BF16)], axis=1)
            elif name == "w_kv_shared":
                w = jnp.concatenate([w[:, 64 * (i // 2):64 * (i // 2 + 1)] for i in range(8)], axis=1)
            else:
                w = w[0]
            derived[key] = w
        return derived[key]

    saved = []
    h, hb = x, x.astype(BF16)
    kr = vd_src = None
    for l in range(DEPTH):
        sv = dict(h=h, hb=hb)
        if l < N_A:
            proj = _matmul(hb, weight("a_w_in", l), "nn", [F32], "mm_proj_a", tn=1152)
            conv_w = small["a_conv_w"][l]
            c = _conv_fwd(proj, conv_w, tb)
            alog, dtb = _pad_row(small["a_A_log"][l]), _pad_row(small["a_dt_bias"][l])
            rowa_in = [c, proj, alog, dtb]
            rowa_specs = [_rows(tb, 3 * DN_W), _rows(tb, 128, 26), _whole((1, 128)), _whole((1, 128))]
            rowa_args = (_rowa_fn, rowa_in, rowa_specs, [_sds((t_total, DN_W))] * 5, [_rows(tb, DN_W)] * 5, (nb,))
            if shards is not None and l == 0:
                names, ride = gather_ride(GATHER_ON_ROWA0)
                (q, k, v, gcb, betab), got = _block_fwd(*rowa_args, "rowa_fwd_gather", ride=ride)
                ready.update(zip(GATHER_ON_ROWA0, got))
            else:
                q, k, v, gcb, betab = _block_fwd(*rowa_args, "rowa_fwd")
            hs = _head_spec(tdn)
            dn_grid = (DN_HEADS, t_total // tdn)
            full = _sds((t_total, DN_W))
            full_b = _sds((t_total, DN_W), BF16)
            dn1_out_shapes = [full, full_b, _sds((DN_HEADS, t_total, PAIR), BF16), full_b, full_b, full,
                              _sds((DN_HEADS, t_total, PAIR))]
            dn1_out_specs = [hs, hs, _intra_spec(tdn), hs, hs, hs, _intra_spec(tdn)]
            if shards is not None:
                keys = GATHER_ON_DN1_0 if l == 0 else GATHER_ON_DN1_1
                names, ride = gather_ride(keys)
                (u, w, intra, qd, kd, cd, tinv), got = _block_fwd(
                    _dn1_fn, [q, k, v, gcb, betab], [hs] * 5, dn1_out_shapes, dn1_out_specs, dn_grid,
                    "dn1_fwd_gather%d" % l, ride=ride)
                ready.update(zip(keys, got))
            else:
                u, w, intra, qd, kd, cd, tinv = _block_fwd(_dn1_fn, [q, k, v, gcb, betab], [hs] * 5, dn1_out_shapes,
                                                           dn1_out_specs, dn_grid, "dn1_fwd")
            o, states = _dn2_fwd(qd, kd, u, w, intra, cd, bs["scan"])
            nw = small["a_norm_w"][l].reshape(1, DN_D)
            post_in = [o, proj, nw]
            post_specs = [_rows(tb, DN_W), _rows(tb, DN_W, 3), _whole((1, DN_D))]
            (og,) = _block_fwd(_post_fn, post_in, post_specs, [_sds((t_total, DN_W), BF16)], [_rows(tb, DN_W)],
                               (nb,), "post_fwd")
            qm_col = 12
            sv.update(proj=proj, c=c, rowa_in=rowa_in, rowa_specs=rowa_specs, dn1_in=[q, k, v, gcb, betab, tinv],
                      dn2_in=[qd, kd, u, w, intra, cd], states=states, post_in=post_in, post_specs=post_specs,
                      conv_w=conv_w)
        else:
            jb = l - N_A
            proj = _matmul(hb, weight("b_w_in", l), "nn", [F32], "mm_proj_b")
            sinks = _pad_row(small["b_sinks"][jb])
            swa_in = [proj, cos, sin, kr, kr, vd_src, vd_src, sinks]
            swa_specs = [_rows(tsw, DN_W), _rows(tsw, 128), _rows(tsw, 128),
                         pl.BlockSpec((WINDOW, 256), lambda i: (jnp.maximum(i * (tsw // WINDOW) - 1, 0), 0)),
                         _rows(tsw, 256),
                         pl.BlockSpec((WINDOW, 256), lambda i: (jnp.maximum(i * (tsw // WINDOW) - 1, 0), 1)),
                         _rows(tsw, 256, 1), _whole((1, 128))]
            (og,) = _block_fwd(_swa_fn, swa_in, swa_specs, [_sds((t_total, DN_W), BF16)], [_rows(tsw, DN_W)],
                               (nbs,), "swa_fwd")
            qm_col = 3
            sv.update(proj=proj, swa_in=swa_in, swa_specs=swa_specs)
        kvm = _matmul(mem_b, weight("mem_w_kv", l), "nn", [F32], "mm_memkv", tm=256)
        mem_in = [proj, kvm]
        (mo,) = _block_fwd(_memattn_fn, mem_in, _memattn_specs(tb, qm_col), [_sds((t_total, MEM_W), BF16)],
                           [_rows(tb, MEM_W)], (nb,), "memattn_fwd")
        mixin = jnp.concatenate([og, mo], axis=1)
        g0, b0 = small["ln_g"][l, 0].reshape(1, -1), small["ln_b"][l, 0].reshape(1, -1)
        mix, h1, h1b = _matmul_ln(mixin, weight("w_o", l), h, g0, b0, "mm_wo_ln")
        act = _matmul(h1b, weight("mlp_w_up", l), "nn", [BF16], "mm_up", epi=_act_epilogue, tm=2048)
        g1, b1 = small["ln_g"][l, 1].reshape(1, -1), small["ln_b"][l, 1].reshape(1, -1)
        mlp, h2, h2b = _matmul_ln(act, weight("mlp_w_down", l), h1, g1, b1, "mm_down_ln")
        sv.update(kvm=kvm, mem_in=mem_in, qm_col=qm_col, mixin=mixin, mix=mix, ln0=(g0, b0), h1=h1, h1b=h1b,
                  act=act, mlp=mlp, ln1=(g1, b1))
        saved.append(sv)
        h, hb = h2, h2b
        if l == N_A - 1:
            kvd = _matmul(hb, weight("w_kv_shared", l), "nn", [F32], "mm_kvd")
            krope_in = [kvd, cos, sin]
            krope_specs = [_rows(tb, 256), _rows(tb, 128), _rows(tb, 128)]
            (kr,) = _block_fwd(_krope_fn, krope_in, krope_specs, [_sds((t_total, 256))], [_rows(tb, 256)], (nb,),
                               "krope_fwd")
            vd_src = kvd

    loss, dh = _loss_and_grad(h, target, tb)

    grads = {}

    def scatter_ride(keys):
        return _scatter_ride([k[0] for k in keys], [grads[k] for k in keys]) if shards is not None else None

    sg = dict(a_conv_w=[None] * N_A, a_A_log=[None] * N_A, a_dt_bias=[None] * N_A, a_norm_w=[None] * N_A,
              b_sinks=[None] * (DEPTH - N_A), ln_g=[[None, None] for _ in range(DEPTH)],
              ln_b=[[None, None] for _ in range(DEPTH)])
    dk_parts, dv_parts = [], []
    for l in reversed(range(DEPTH)):
        sv = saved[l]
        if l == N_A - 1:
            dkr = _halo_sum([p[0] for p in dk_parts], [p[1] for p in dk_parts], tsw)
            dvv = _halo_sum([p[0] for p in dv_parts], [p[1] for p in dv_parts], tsw)
            (dkraw,) = _block_bwd(_krope_fn, krope_in, krope_specs, [dkr], [_rows(tb, 256)], ["s", None, None],
                                  [_sds((t_total, 256), BF16)], [_rows(tb, 256)], (nb,), "krope_bwd")
            dkvd = jnp.concatenate([dkraw, dvv.astype(BF16)], axis=1)
            g_kvd = _matmul(saved[l + 1]["hb"], dkvd, "tn", [F32], "mm_dw_kvd", tm=1024, tn=512)
            dh = _matmul(dkvd, weight("w_kv_shared", l), "nt", [F32], "mm_dx_kvd", epi=_add_epilogue, extras=[dh],
                         tn=1024, tk=512)
            grads[_key("w_kv_shared", l)] = jnp.concatenate(
                [g_kvd[:, 128 * i:128 * i + 64] + g_kvd[:, 128 * i + 64:128 * (i + 1)] for i in range(4)],
                axis=1).astype(BF16)
        g1, b1 = sv["ln1"]
        dh1a, dmlp, dg1, db1 = _ln_grad(sv["h1"], sv["mlp"], g1, b1, dh, bs["ln"])
        dup = _matmul(dmlp, weight("mlp_w_down", l), "nt", [BF16], "mm_dact", epi=_dact_epilogue, extras=[sv["act"]],
                      tm=2048)
        grads[_key("mlp_w_down", l)] = _matmul(sv["act"], dmlp, "tn", [BF16], "mm_dw_down", tk=4096)[None]
        grads[_key("mlp_w_up", l)] = _matmul(sv["h1b"], dup, "tn", [BF16], "mm_dw_up", tk=4096)[None]
        dh1 = _matmul(dup, weight("mlp_w_up", l), "nt", [F32], "mm_dx_up", epi=_add_epilogue, extras=[dh1a], tk=2048)
        g0, b0 = sv["ln0"]
        dha, dmix, dg0, db0 = _ln_grad(sv["h"], sv["mix"], g0, b0, dh1, bs["ln"])
        sg["ln_g"][l] = [dg0, dg1]
        sg["ln_b"][l] = [db0, db1]
        grads[_key("w_o", l)] = _matmul(sv["mixin"], dmix, "tn", [BF16], "mm_dw_o", tk=4096)[None]
        dmixin = _matmul(dmix, weight("w_o", l), "nt", [F32], "mm_dx_o", tn=1024)
        dqm, dkvm = _block_bwd(_memattn_fn, sv["mem_in"], _memattn_specs(tb, sv["qm_col"]), [dmixin],
                               [_rows(tb, MEM_W, 3)], ["s", "a"],
                               [_sds((t_total, MEM_W), BF16), _sds((MEM_W, 2 * MEM_W))],
                               [_rows(tb, MEM_W), _whole((MEM_W, 2 * MEM_W))], (nb,), "memattn_bwd")
        grads[_key("mem_w_kv", l)] = _matmul(mem_b, dkvm.astype(BF16), "tn", [BF16], "mm_dw_memkv", tm=1024,
                                             tn=512)[None]
        if l < N_A:
            d_o, dz, dnw = _block_bwd(_post_fn, sv["post_in"], sv["post_specs"], [dmixin], [_rows(tb, DN_W)],
                                      ["s", "s", "a"],
                                      [_sds((t_total, DN_W)), _sds((t_total, DN_W), BF16), _sds((1, DN_D))],
                                      [_rows(tb, DN_W), _rows(tb, DN_W), _whole((1, DN_D))], (nb,), "post_bwd")
            sg["a_norm_w"][l] = dnw
            dqd, dkd, du, dw, da, dcd = _dn2_bwd(*sv["dn2_in"], sv["states"], d_o, bs["scan"])
            hs = _head_spec(tdn)
            full = _sds((t_total, DN_W))
            dn1_bwd_args = (_dn1_fn_known, sv["dn1_in"], [hs] * 5 + [_intra_spec(tdn)], [du, dw, da, dqd, dkd, dcd],
                            [hs, hs, _intra_spec(tdn), hs, hs, hs], ["s"] * 5 + [None], [full] * 5, [hs] * 5,
                            (DN_HEADS, t_total // tdn))
            if shards is not None:
                keys = SCATTER_ON_DN1_BWD_1 if l == N_A - 1 else SCATTER_ON_DN1_BWD_0
                (dq, dk, dv, dgc, dbeta), got = _block_bwd(*dn1_bwd_args, "dn1_bwd_scatter%d" % l,
                                                           ride=scatter_ride(keys))
                grads.update(zip(keys, got))
            else:
                dq, dk, dv, dgc, dbeta = _block_bwd(*dn1_bwd_args, "dn1_bwd")
            dc, dab, dalog, ddtb = _block_bwd(
                _rowa_fn, sv["rowa_in"], sv["rowa_specs"], [dq, dk, dv, dgc, dbeta], [_rows(tb, DN_W)] * 5,
                ["s", "s", "a", "a"],
                [_sds((t_total, 3 * DN_W)), _sds((t_total, 128), BF16), _sds((1, 128)), _sds((1, 128))],
                [_rows(tb, 3 * DN_W), _rows(tb, 128), _whole((1, 128)), _whole((1, 128))], (nb,), "rowa_bwd")
            sg["a_A_log"][l] = dalog[0, :DN_HEADS]
            sg["a_dt_bias"][l] = ddtb[0, :DN_HEADS]
            dx, dconv = _conv_bwd(dc, sv["proj"], sv["conv_w"], tb)
            sg["a_conv_w"][l] = dconv
            dproj = jnp.concatenate([dx, dz, dqm, dab], axis=1)
            g_in = _matmul(sv["hb"], dproj, "tn", [BF16], "mm_dw_a", tm=1024, tn=1152, tk=2048)
            grads[_key("a_w_in", l)] = jnp.concatenate([g_in[:, :3072], g_in[:, 3328:3340], g_in[:, 3072:3328]],
                                                       axis=1)[None]
            dh = _matmul(dproj, weight("a_w_in", l), "nt", [F32], "mm_dx_a", epi=_add_epilogue, extras=[dha], tk=1152)
        else:
            jb = l - N_A
            swa_kinds = ["s", None, None, "s", "s", "s", "s", "a"]
            halo_spec = pl.BlockSpec((None, WINDOW, 256), lambda i: (i, 0, 0))
            dq, dkh, dkc, dvh, dvc, dsink = _block_bwd(
                _swa_fn, sv["swa_in"], sv["swa_specs"], [dmixin], [_rows(tsw, DN_W)], swa_kinds,
                [_sds((t_total, DN_W), BF16), _sds((nbs, WINDOW, 256)), _sds((t_total, 256)),
                 _sds((nbs, WINDOW, 256)), _sds((t_total, 256)), _sds((1, 128))],
                [_rows(tsw, DN_W), halo_spec, _rows(tsw, 256), halo_spec, _rows(tsw, 256), _whole((1, 128))],
                (nbs,), "swa_bwd")
            sg["b_sinks"][jb] = dsink[0, :SWA_HEADS]
            dk_parts.append((dkc, dkh))
            dv_parts.append((dvc, dvh))
            dproj = jnp.concatenate([dq, dqm], axis=1)
            grads[_key("b_w_in", l)] = _matmul(sv["hb"], dproj, "tn", [BF16], "mm_dw_b", tk=4096)[None]
            dh = _matmul(dproj, weight("b_w_in", l), "nt", [F32], "mm_dx_b", epi=_add_epilogue, extras=[dha], tn=1024)

    small_grads = dict(
        a_conv_w=jnp.stack(sg["a_conv_w"]), a_A_log=jnp.stack(sg["a_A_log"]), a_dt_bias=jnp.stack(sg["a_dt_bias"]),
        a_norm_w=jnp.concatenate(sg["a_norm_w"], axis=0), b_sinks=jnp.stack(sg["b_sinks"]),
        ln_g=jnp.stack([jnp.concatenate(p, axis=0) for p in sg["ln_g"]]),
        ln_b=jnp.stack([jnp.concatenate(p, axis=0) for p in sg["ln_b"]]))
    return loss, dh, grads, small_grads


def _pack(arrays, rows):
    flat = []
    for a in arrays:
        v = a.astype(F32).reshape(-1)
        flat.append(jnp.pad(v, (0, (-v.shape[0]) % 128)))
    flat = jnp.concatenate(flat)
    return jnp.pad(flat, (0, rows * 128 - flat.shape[0])).reshape(rows, 128)


def _unpack(slab, shapes):
    flat = slab.reshape(slab.shape[:-2] + (-1,))
    out, off = [], 0
    for s in shapes:
        n = math.prod(s)
        out.append(flat[..., off:off + n].reshape(slab.shape[:-2] + tuple(s)))
        off += n + (-n) % 128
    return out


def _rows_for(shapes):
    rows = sum((math.prod(s) + 127) // 128 for s in shapes)
    return rows + (-rows) % 8


SMALL_NAMES = ("a_conv_w", "a_A_log", "a_dt_bias", "a_norm_w", "b_sinks", "ln_g", "ln_b")
SMALL_SHARDED = {"a_conv_w": 2, "ln_g": 2, "ln_b": 2}
SMALL_FULL = {"a_conv_w": (2, 4, 2304), "a_A_log": (2, 6), "a_dt_bias": (2, 6), "a_norm_w": (2, 128),
              "b_sinks": (2, 12), "ln_g": (4, 2, 1024), "ln_b": (4, 2, 1024)}


def kernel(x, mem, positions, a_w_in, a_conv_w, a_A_log, a_dt_bias, a_norm_w, b_w_in, b_sinks, w_kv_shared, mem_w_kv, w_o, mlp_w_up, mlp_w_down, ln_g, ln_b, loss_target, m_a_w_in, m_a_conv_w, m_a_A_log, m_a_dt_bias, m_a_norm_w, m_b_w_in, m_b_sinks, m_w_kv_shared, m_mem_w_kv, m_w_o, m_mlp_w_up, m_mlp_w_down, m_ln_g, m_ln_b, v_a_w_in, v_a_conv_w, v_a_A_log, v_a_dt_bias, v_a_norm_w, v_b_w_in, v_b_sinks, v_w_kv_shared, v_mem_w_kv, v_w_o, v_mlp_w_up, v_mlp_w_down, v_ln_g, v_ln_b):
    params = dict(a_w_in=a_w_in, a_conv_w=a_conv_w, a_A_log=a_A_log, a_dt_bias=a_dt_bias, a_norm_w=a_norm_w,
                  b_w_in=b_w_in, b_sinks=b_sinks, w_kv_shared=w_kv_shared, mem_w_kv=mem_w_kv, w_o=w_o,
                  mlp_w_up=mlp_w_up, mlp_w_down=mlp_w_down, ln_g=ln_g, ln_b=ln_b)
    mom = dict(a_w_in=m_a_w_in, a_conv_w=m_a_conv_w, a_A_log=m_a_A_log, a_dt_bias=m_a_dt_bias, a_norm_w=m_a_norm_w,
               b_w_in=m_b_w_in, b_sinks=m_b_sinks, w_kv_shared=m_w_kv_shared, mem_w_kv=m_mem_w_kv, w_o=m_w_o,
               mlp_w_up=m_mlp_w_up, mlp_w_down=m_mlp_w_down, ln_g=m_ln_g, ln_b=m_ln_b)
    var = dict(a_w_in=v_a_w_in, a_conv_w=v_a_conv_w, a_A_log=v_a_A_log, a_dt_bias=v_a_dt_bias, a_norm_w=v_a_norm_w,
               b_w_in=v_b_w_in, b_sinks=v_b_sinks, w_kv_shared=v_w_kv_shared, mem_w_kv=v_mem_w_kv, w_o=v_w_o,
               mlp_w_up=v_mlp_w_up, mlp_w_down=v_mlp_w_down, ln_g=v_ln_g, ln_b=v_ln_b)
    me = 4 * lax.axis_index("x") + 2 * lax.axis_index("y") + lax.axis_index("c")

    shards = {(n, i): (params[n] if i is None else params[n][i:i + 1]).astype(BF16) for n, i in ALL_KEYS}
    first = [k[0] for k in GATHER_FIRST]
    ready = dict(zip(GATHER_FIRST, _allgather_weights(first, [shards[k] for k in GATHER_FIRST])))
    sharded_names = [n for n in SMALL_NAMES if n in SMALL_SHARDED]
    shard_shapes = [params[n].shape for n in sharded_names]
    gathered = _small_exchange(_pack([params[n] for n in sharded_names], _rows_for(shard_shapes)), reduce=False)
    small = {n: params[n] for n in SMALL_NAMES if n not in SMALL_SHARDED}
    for n, g in zip(sharded_names, _unpack(gathered, shard_shapes)):
        small[n] = jnp.moveaxis(g, 0, 2).reshape(SMALL_FULL[n])

    loss, dx, recv, small_grads = _local_step(x[0], mem[0], positions[0], loss_target[0], ready, shards, small)
    loss = lax.psum(loss, ("x", "y", "c"))
    last = [k[0] for k in SCATTER_LAST]
    recv.update(zip(SCATTER_LAST, _scatter_grads(last, [recv[k] for k in SCATTER_LAST])))
    out = {}
    for n in BIG_NAMES:
        shp = params[n].shape
        rows = math.prod(shp[:-1])
        recvs = [recv[k].reshape(N_DEV, -1, shp[-1]) for k in sorted(k for k in ALL_KEYS if k[0] == n)]
        res = _adamw(recvs, params[n].reshape(rows, shp[-1]), mom[n].reshape(rows, shp[-1]),
                     var[n].reshape(rows, shp[-1]), 32, "adamw_" + n)
        out[n] = [t.reshape(shp) for t in res]
    full_shapes = [SMALL_FULL[n] for n in SMALL_NAMES]
    summed = _small_exchange(_pack([small_grads[n] for n in SMALL_NAMES], _rows_for(full_shapes)), reduce=True)
    local_g = []
    for n, g in zip(SMALL_NAMES, _unpack(summed, full_shapes)):
        if n in SMALL_SHARDED:
            size = params[n].shape[2]
            g = lax.dynamic_slice_in_dim(g, me * size, size, axis=2)
        local_g.append(g)
    local_shapes = [params[n].shape for n in SMALL_NAMES]
    rows = _rows_for(local_shapes)
    res = _adamw([_pack(local_g, rows)[None]], _pack([params[n] for n in SMALL_NAMES], rows),
                 _pack([mom[n] for n in SMALL_NAMES], rows), _pack([var[n] for n in SMALL_NAMES], rows), rows,
                 "adamw_small")
    unpacked = [_unpack(t, local_shapes) for t in res]
    for i, n in enumerate(SMALL_NAMES):
        out[n] = [unpacked[k][i] for k in range(4)]

    order = ("a_w_in", "a_conv_w", "a_A_log", "a_dt_bias", "a_norm_w", "b_w_in", "b_sinks", "w_kv_shared",
             "mem_w_kv", "w_o", "mlp_w_up", "mlp_w_down", "ln_g", "ln_b")
    return (loss, dx[None], *[out[n][0] for n in order], *[out[n][1] for n in order],
            *[out[n][2] for n in order], *[out[n][3] for n in order])
```

```python
import functools
import math

import jax
import jax.numpy as jnp
from jax import lax
from jax.experimental import pallas as pl
from jax.experimental.pallas import tpu as pltpu

F32 = jnp.float32
BF16 = jnp.bfloat16

D_MODEL = 1024
DEPTH = 4
N_A = 2
MEM_HEADS = 4
MEM_DH = 64
MEM_W = 256
DN_HEADS = 6
DN_D = 128
DN_W = 768
CHUNK = 64
SWA_DH = 64
SWA_HEADS = 12
WINDOW = 128
ROPE_THETA = 10000.0
LN_EPS = 1e-5
NORM_EPS = 1e-6
DN_ALPHA = (2.0 * DEPTH) ** 0.25
A_IN = 3340
A_IN_PAD = 3456
N_DEV = 8

ADAM_LR = 0.001
ADAM_B1 = 0.9
ADAM_B2 = 0.999
ADAM_EPS = 1e-08
ADAM_WD = 0.01
ADAM_STEP = 10

VMEM_LIMIT = 52 * 1024 * 1024
NEG_BIG = -1e30


def _cparams(sem):
    return pltpu.CompilerParams(dimension_semantics=sem, vmem_limit_bytes=VMEM_LIMIT)


_CONTRACT = {"nn": (1, 0), "nt": (1, 1), "tn": (0, 0)}


def _raw_mm(a, b, mode, prec):
    ca, cb = _CONTRACT[mode]
    dims = (((ca,), (cb,)), ((), ()))
    dot = lambda p, q: lax.dot_general(p, q, dims, preferred_element_type=F32)
    if prec == "bf16":
        return dot(a.astype(BF16), b.astype(BF16))
    a, b = a.astype(F32), b.astype(F32)
    a_hi, b_hi = a.astype(BF16), b.astype(BF16)
    if prec == "sela":
        return dot(a_hi, b_hi) + dot(a_hi, (b - b_hi.astype(F32)).astype(BF16))
    a_lo = (a - a_hi.astype(F32)).astype(BF16)
    if prec == "selb":
        return dot(a_hi, b_hi) + dot(a_lo, b_hi)
    b_lo = (b - b_hi.astype(F32)).astype(BF16)
    return dot(a_hi, b_hi) + (dot(a_hi, b_lo) + dot(a_lo, b_hi))


@functools.partial(jax.custom_vjp, nondiff_argnums=(2, 3))
def mm(a, b, mode, prec):
    return _raw_mm(a, b, mode, prec)


def _mm_fwd(a, b, mode, prec):
    return _raw_mm(a, b, mode, prec), (a, b)


def _mm_bwd(mode, prec, res, ct):
    a, b = res
    if prec == "sela":
        pa, pb = "f32", {"nn": "sela", "nt": "selb", "tn": "sela"}[mode]
    elif prec == "selb":
        pa, pb = {"nn": "selb", "nt": "selb", "tn": "sela"}[mode], "f32"
    else:
        pa = pb = prec
    if mode == "nn":
        return mm(ct, b, "nt", pa), mm(a, ct, "tn", pb)
    if mode == "nt":
        return mm(ct, b, "nn", pa), mm(ct, a, "tn", pb)
    return mm(b, ct, "nt", pa), mm(a, ct, "nn", pb)


mm.defvjp(_mm_fwd, _mm_bwd)


@jax.custom_vjp
def _softplus(x):
    y = jnp.exp(-jnp.abs(x))
    log1p_y = jnp.where(y < 1e-2, y * (1.0 - y * (0.5 - y * (1.0 / 3.0))), jnp.log(1.0 + y))
    return jnp.maximum(x, 0.0) + log1p_y


def _softplus_fwd(x):
    return _softplus(x), x


def _softplus_bwd(x, ct):
    return (ct * jax.nn.sigmoid(x),)


_softplus.defvjp(_softplus_fwd, _softplus_bwd)


def _iota(shape, dim):
    return lax.broadcasted_iota(jnp.int32, shape, dim)


class _Ride:
    def __init__(self, srcs, dst_shapes, src_view, dst_view):
        self.srcs, self.dst_shapes, self.src_view, self.dst_view = list(srcs), list(dst_shapes), src_view, dst_view
        self.n = len(self.srcs)
        self.any_specs = [pl.BlockSpec(memory_space=pl.ANY)] * self.n
        self.scratch = [pltpu.SemaphoreType.DMA((self.n, N_DEV - 1)), pltpu.SemaphoreType.DMA((self.n, N_DEV - 1)),
                        pltpu.SemaphoreType.DMA((self.n,))]

    def copies(self, src_refs, dst_refs, sems):
        send_sems, recv_sems, local_sems = sems
        me, peers = _me_and_peers()
        local, out, inc = [], [], []
        for t in range(self.n):
            local.append(pltpu.make_async_copy(self.src_view(t, src_refs[t], me), self.dst_view(t, dst_refs[t], me),
                                               local_sems.at[t]))
            for k, (dev, idx) in enumerate(peers):
                mk = lambda s, d: pltpu.make_async_remote_copy(
                    src_ref=s, dst_ref=d, send_sem=send_sems.at[t, k], recv_sem=recv_sems.at[t, k], device_id=dev,
                    device_id_type=pl.DeviceIdType.MESH)
                out.append(mk(self.src_view(t, src_refs[t], idx), self.dst_view(t, dst_refs[t], me)))
                inc.append(mk(self.src_view(t, src_refs[t], me), self.dst_view(t, dst_refs[t], idx)))
        return local, out, inc

    def start(self, grid, src_refs, dst_refs, sems):
        first = pl.program_id(0) == 0
        for a in range(1, len(grid)):
            first = jnp.logical_and(first, pl.program_id(a) == 0)

        @pl.when(first)
        def _():
            local, out, _ = self.copies(src_refs, dst_refs, sems)
            for cp in local + out:
                cp.start()

    def finish(self, grid, src_refs, dst_refs, sems):
        last = pl.program_id(0) == grid[0] - 1
        for a in range(1, len(grid)):
            last = jnp.logical_and(last, pl.program_id(a) == grid[a] - 1)

        @pl.when(last)
        def _():
            local, out, inc = self.copies(src_refs, dst_refs, sems)
            for cp in inc:
                cp.wait_recv()
            for cp in out:
                cp.wait_send()
            for cp in local:
                cp.wait()


def _block_fwd(fn, ins, in_specs, out_shapes, out_specs, grid, name, ride=None):
    n_in, n_out = len(ins), len(out_shapes)
    n_ride = ride.n if ride else 0

    def body(*refs):
        pids = tuple(pl.program_id(a) for a in range(len(grid)))
        ride_refs = (refs[n_in:n_in + n_ride], refs[n_in + n_ride + n_out:n_in + 2 * n_ride + n_out],
                     refs[n_in + 2 * n_ride + n_out:])
        if ride:
            ride.start(grid, *ride_refs)
        vals = [r[...].astype(F32) for r in refs[:n_in]]
        outs = fn(pids, *vals)
        for r, o in zip(refs[n_in + n_ride:n_in + n_ride + n_out], outs):
            r[...] = o.astype(r.dtype)
        if ride:
            ride.finish(grid, *ride_refs)

    if not ride:
        return pl.pallas_call(
            body, grid=grid, in_specs=in_specs, out_specs=out_specs, out_shape=out_shapes, name=name,
            compiler_params=_cparams(("parallel",) * len(grid)))(*ins)
    res = pl.pallas_call(
        body, grid=grid, in_specs=list(in_specs) + ride.any_specs, out_specs=list(out_specs) + ride.any_specs,
        out_shape=list(out_shapes) + ride.dst_shapes, scratch_shapes=ride.scratch, name=name,
        compiler_params=_cparams(("arbitrary",) * len(grid)))(*ins, *ride.srcs)
    return res[:n_out], res[n_out:]


def _block_bwd(fn, ins, in_specs, cts, ct_specs, kinds, g_shapes, g_specs, grid, name, ride=None):
    n_in, n_ct, n_g = len(ins), len(cts), len(g_shapes)
    n_ride = ride.n if ride else 0
    didx = [i for i, k in enumerate(kinds) if k]

    def body(*refs):
        in_refs, ct_refs = refs[:n_in], refs[n_in:n_in + n_ct]
        base = n_in + n_ct
        g_refs = refs[base + n_ride:base + n_ride + n_g]
        ride_refs = (refs[base:base + n_ride], refs[base + n_ride + n_g:base + 2 * n_ride + n_g],
                     refs[base + 2 * n_ride + n_g:])
        if ride:
            ride.start(grid, *ride_refs)
        pids = tuple(pl.program_id(a) for a in range(len(grid)))
        vals = [r[...].astype(F32) for r in in_refs]

        def f(*dvals):
            full = list(vals)
            for i, v in zip(didx, dvals):
                full[i] = v
            return tuple(fn(pids, *full))

        _, vjp = jax.vjp(f, *[vals[i] for i in didx])
        gs = vjp(tuple(r[...].astype(F32) for r in ct_refs))
        first = pids[0] == 0
        for p in pids[1:]:
            first = jnp.logical_and(first, p == 0)
        for i, g, r in zip(didx, gs, g_refs):
            if kinds[i] == "s":
                r[...] = g.astype(r.dtype)
            else:
                @pl.when(first)
                def _(r=r):
                    r[...] = jnp.zeros(r.shape, r.dtype)

                r[...] += g.astype(r.dtype)
        if ride:
            ride.finish(grid, *ride_refs)

    sem = ("arbitrary",) * len(grid) if "a" in kinds or ride else ("parallel",) * len(grid)
    if not ride:
        return pl.pallas_call(
            body, grid=grid, in_specs=list(in_specs) + list(ct_specs), out_specs=g_specs, out_shape=g_shapes,
            name=name, compiler_params=_cparams(sem))(*ins, *cts)
    res = pl.pallas_call(
        body, grid=grid, in_specs=list(in_specs) + list(ct_specs) + ride.any_specs,
        out_specs=list(g_specs) + ride.any_specs, out_shape=list(g_shapes) + ride.dst_shapes,
        scratch_shapes=ride.scratch, name=name, compiler_params=_cparams(sem))(*ins, *cts, *ride.srcs)
    return res[:n_g], res[n_g:]


def _rows(tb, width, col=0):
    return pl.BlockSpec((tb, width), lambda i, col=col: (i, col))


def _whole(shape):
    return pl.BlockSpec(shape, lambda *_: (0,) * len(shape))


def _sds(shape, dtype=F32):
    return jax.ShapeDtypeStruct(shape, dtype)


def _matmul(a, b, mode, out_dtypes, name, epi=None, extras=(), tm=1024, tn=1024, tk=1024,
            b_spec=None, n_total=None, out_specs=None, out_shapes=None):
    if mode == "tn":
        k_total, m_total = a.shape
    else:
        m_total, k_total = a.shape
    if n_total is None:
        n_total = b.shape[0] if mode == "nt" else b.shape[1]
    tm, tn, tk = min(tm, m_total), min(tn, n_total), min(tk, k_total)
    assert m_total % tm == 0 and n_total % tn == 0 and k_total % tk == 0, (name, a.shape, b.shape)
    grid = (m_total // tm, n_total // tn, k_total // tk)
    nk = grid[2]
    if mode == "tn":
        a_spec = pl.BlockSpec((tk, tm), lambda i, j, k: (k, i))
    else:
        a_spec = pl.BlockSpec((tm, tk), lambda i, j, k: (i, k))
    if b_spec is None:
        if mode == "nt":
            b_spec = pl.BlockSpec((tn, tk), lambda i, j, k: (j, k))
        else:
            b_spec = pl.BlockSpec((tk, tn), lambda i, j, k: (k, j))
    tile = pl.BlockSpec((tm, tn), lambda i, j, k: (i, j))
    n_ex, n_out = len(extras), len(out_dtypes)
    ca, cb = _CONTRACT[mode]
    dims = (((ca,), (cb,)), ((), ()))

    chunk = 256
    n_chunks = tm // chunk if (epi is not None and mode != "tn" and tm % chunk == 0) else 1

    def body(*refs):
        a_ref, b_ref = refs[:2]
        ex_refs = refs[2:2 + n_ex]
        out_refs = refs[2 + n_ex:2 + n_ex + n_out]
        acc = refs[-1] if nk > 1 else None
        full = lambda: lax.dot_general(a_ref[...], b_ref[...], dims, preferred_element_type=F32)

        def last_step():
            for c in range(n_chunks):
                rows = pl.ds(c * chunk, chunk) if n_chunks > 1 else slice(None)
                val = (lax.dot_general(a_ref[rows, :], b_ref[...], dims, preferred_element_type=F32)
                       if n_chunks > 1 else full())
                if acc is not None:
                    val = val + acc[rows, :]
                res = epi(val, *[e[rows, :] for e in ex_refs]) if epi is not None else (val,)
                for r, o in zip(out_refs, res):
                    r[rows, :] = o.astype(r.dtype)

        if nk == 1:
            last_step()
        else:
            k = pl.program_id(2)

            @pl.when(k == 0)
            def _():
                acc[...] = full()

            if nk > 2:
                @pl.when(jnp.logical_and(k > 0, k < nk - 1))
                def _():
                    acc[...] += full()

            @pl.when(k == nk - 1)
            def _():
                last_step()

    if out_shapes is None:
        out_shapes = [_sds((m_total, n_total), d) for d in out_dtypes]
        out_specs = [tile] * n_out
    outs = pl.pallas_call(
        body, grid=grid, in_specs=[a_spec, b_spec] + [tile] * n_ex, out_specs=out_specs, out_shape=out_shapes,
        scratch_shapes=[pltpu.VMEM((tm, tn), F32)] if nk > 1 else [], name=name,
        compiler_params=_cparams(("parallel", "parallel", "arbitrary")))(a, b, *extras)
    return outs if n_out > 1 else outs[0]


def _matmul_ln(a, b, h, g, beta, name, tm=1024, tk=2048, chunk=256):
    m_total, k_total = a.shape
    n = b.shape[1]
    tm, tk = min(tm, m_total), min(tk, k_total)
    chunk = min(chunk, tm)
    assert m_total % tm == 0 and k_total % tk == 0 and tm % chunk == 0, (name, a.shape, b.shape)
    nk = k_total // tk
    dims = (((1,), (0,)), ((), ()))

    def body(a_ref, b_ref, h_ref, g_ref, beta_ref, mix_ref, y_ref, yb_ref, *acc):
        k = pl.program_id(1)

        def finish():
            for c in range(tm // chunk):
                rows = pl.ds(chunk * c, chunk)
                val = lax.dot_general(a_ref[rows, :], b_ref[...], dims, preferred_element_type=F32)
                if nk > 1:
                    val = val + acc[0][rows, :]
                mix_ref[rows, :] = val
                (y,) = _ln_fn(None, h_ref[rows, :], val, g_ref[...], beta_ref[...])
                y_ref[rows, :] = y
                yb_ref[rows, :] = y.astype(BF16)

        if nk == 1:
            finish()
        else:
            part = lambda: lax.dot_general(a_ref[...], b_ref[...], dims, preferred_element_type=F32)

            @pl.when(k == 0)
            def _():
                acc[0][...] = part()

            if nk > 2:
                @pl.when(jnp.logical_and(k > 0, k < nk - 1))
                def _():
                    acc[0][...] += part()

            @pl.when(k == nk - 1)
            def _():
                finish()

    tile = pl.BlockSpec((tm, n), lambda i, k: (i, 0))
    row = pl.BlockSpec((1, n), lambda i, k: (0, 0))
    return pl.pallas_call(
        body, grid=(m_total // tm, nk),
        in_specs=[pl.BlockSpec((tm, tk), lambda i, k: (i, k)), pl.BlockSpec((tk, n), lambda i, k: (k, 0)), tile, row,
                  row],
        out_specs=[tile, tile, tile],
        out_shape=[_sds((m_total, n)), _sds((m_total, n)), _sds((m_total, n), BF16)],
        scratch_shapes=[pltpu.VMEM((tm, n), F32)] if nk > 1 else [], name=name,
        compiler_params=_cparams(("parallel", "arbitrary")))(a, b, h, g, beta)


def _silu(x):
    return x * jax.nn.sigmoid(x)


def _rowa_fn(pids, c, ab, alog, dtb):
    tb = c.shape[0]
    s = _silu(c)
    qs, ks = [], []
    for h in range(DN_HEADS):
        qh = s[:, DN_D * h:DN_D * (h + 1)]
        qs.append(qh * lax.rsqrt(jnp.sum(qh * qh, axis=-1, keepdims=True) + NORM_EPS) * (DN_D ** -0.5))
        kh = s[:, DN_W + DN_D * h:DN_W + DN_D * (h + 1)]
        ks.append(kh * lax.rsqrt(jnp.sum(kh * kh, axis=-1, keepdims=True) + NORM_EPS))
    q = jnp.concatenate(qs, axis=1)
    k = jnp.concatenate(ks, axis=1)
    v = s[:, 2 * DN_W:3 * DN_W]
    g128 = -jnp.exp(alog) * _softplus(ab + dtb)
    b128 = jax.nn.sigmoid(ab)
    r, cc = _iota((tb, tb), 0), _iota((tb, tb), 1)
    tri = jnp.where(((r >> 6) == (cc >> 6)) & (r >= cc), 1.0, 0.0)
    gc128 = mm(tri, g128, "nn", "sela")
    lane, col = _iota((128, DN_W), 0), _iota((128, DN_W), 1)
    exp_a = jnp.where(lane == (col >> 7), 1.0, 0.0)
    exp_b = jnp.where(lane == (col >> 7) + DN_HEADS, 1.0, 0.0)
    return q, k, v, mm(gc128, exp_a, "nn", "selb"), mm(b128, exp_b, "nn", "selb")


def _tri_inv_raw(lows, block):
    n = lows[0].shape[0]
    r, c = _iota((n, n), 0), _iota((n, n), 1)
    lg = 0
    xs = None
    while (1 << lg) < block:
        off = ((r >> (lg + 1)) == (c >> (lg + 1))) & (((r >> lg) & 1) == 1) & (((c >> lg) & 1) == 0)
        cblks = [jnp.where(off, low, 0.0) for low in lows]
        if xs is None:
            xs = [jnp.where(r == c, 1.0, 0.0) - cb for cb in cblks]
        else:
            ys = [mm(cb, x, "nn", "f32") for cb, x in zip(cblks, xs)]
            xs = [x - mm(x, y, "nn", "f32") for x, y in zip(xs, ys)]
        lg += 1
    return tuple(xs)


def _tri_inv_cotangent(block, xs, cts):
    n = xs[0].shape[0]
    r, c = _iota((n, n), 0), _iota((n, n), 1)
    shift = block.bit_length() - 1
    keep = ((r >> shift) == (c >> shift)) & (r > c)
    gs = [mm(x, ct, "tn", "f32") for x, ct in zip(xs, cts)]
    gs = [mm(g, x, "nt", "f32") for g, x in zip(gs, xs)]
    return tuple(jnp.where(keep, -g, 0.0) for g in gs)


@functools.partial(jax.custom_vjp, nondiff_argnums=(1,))
def _tri_inv(lows, block):
    return _tri_inv_raw(lows, block)


def _tri_inv_fwd(lows, block):
    xs = _tri_inv_raw(lows, block)
    return xs, xs


def _tri_inv_bwd(block, xs, cts):
    return (_tri_inv_cotangent(block, xs, cts),)


_tri_inv.defvjp(_tri_inv_fwd, _tri_inv_bwd)


@functools.partial(jax.custom_vjp, nondiff_argnums=(2,))
def _tri_inv_known(lows, known, block):
    return known


def _tri_inv_known_fwd(lows, known, block):
    return known, known


def _tri_inv_known_bwd(block, xs, cts):
    return _tri_inv_cotangent(block, xs, cts), tuple(jnp.zeros_like(x) for x in xs)


_tri_inv_known.defvjp(_tri_inv_known_fwd, _tri_inv_known_bwd)


PAIR = 2 * CHUNK


def _dn1_pairs(q, k, v, gc, beta, tinv_known=None):
    assert PAIR == DN_D
    n = PAIR
    pairs = range(q.shape[0] // n)
    cut = lambda t: [t[n * j:n * (j + 1)] for j in pairs]
    q, k, v, gc, beta = cut(q), cut(k), cut(v), cut(gc), cut(beta)
    onehot = jnp.where(_iota((n, DN_D), 1) == 0, 1.0, 0.0)
    r, c = _iota((n, n), 0), _iota((n, n), 1)
    same = (r >> 6) == (c >> 6)
    incl, strict = same & (r >= c), same & (r > c)
    row = _iota((n, DN_D), 0)
    eg = [jnp.exp(g) for g in gc]
    kb = [k[j] * beta[j] for j in pairs]
    g_row = [mm(onehot, g, "nt", "sela") for g in gc]
    kk = [mm(kb[j], k[j], "nt", "bf16") for j in pairs]
    qk = [mm(q[j], k[j], "nt", "bf16") for j in pairs]
    decay = [jnp.exp(jnp.where(incl, gc[j] - g_row[j], NEG_BIG)) for j in pairs]
    low = tuple(jnp.where(strict, kk[j] * decay[j], 0.0) for j in pairs)
    if tinv_known is None:
        tinv = _tri_inv(low, CHUNK)
    else:
        tinv = _tri_inv_known(low, tuple(cut(tinv_known)), CHUNK)
    uw = [mm(tinv[j], jnp.concatenate([v[j] * beta[j], kb[j] * eg[j]], axis=1), "nn", "f32") for j in pairs]
    intra = [jnp.where(incl, qk[j] * decay[j], 0.0) for j in pairs]
    g_last = []
    for g in gc:
        last0 = jnp.sum(jnp.where(row == CHUNK - 1, g, 0.0), axis=0, keepdims=True)
        last1 = jnp.sum(jnp.where(row == PAIR - 1, g, 0.0), axis=0, keepdims=True)
        g_last.append(jnp.where(row < CHUNK, last0, last1))
    join = lambda parts: jnp.concatenate(parts, axis=0)
    return (join([t[:, :DN_D] for t in uw]), join([t[:, DN_D:] for t in uw]), join(intra),
            join([q[j] * eg[j] for j in pairs]), join([k[j] * jnp.exp(g_last[j] - gc[j]) for j in pairs]),
            join([jnp.exp(g) for g in g_last]), join(list(tinv)))


def _dn1_fn(pids, q, k, v, gc, beta):
    return _dn1_pairs(q, k, v, gc, beta)


def _dn1_fn_known(pids, q, k, v, gc, beta, tinv):
    return _dn1_pairs(q, k, v, gc, beta, tinv)[:6]


def _dn2_step(half, state, qd, kd, u, w, intra, cd_row):
    heads = range(len(state))
    v_new = [u[h] - mm(w[h], state[h], "nn", "bf16") for h in heads]
    zeros = jnp.zeros_like(v_new[0])
    v_pair = [jnp.concatenate([v, zeros] if half == 0 else [zeros, v], axis=0) for v in v_new]
    from_state = [mm(qd[h], state[h], "nn", "bf16") for h in heads]
    out = tuple(from_state[h] + mm(intra[h], v_pair[h], "nn", "bf16") for h in heads)
    return out, tuple(state[h] * cd_row[h] + mm(kd[h], v_new[h], "tn", "bf16") for h in heads)


def _post_fn(pids, o, z, nw):
    outs = []
    for h in range(DN_HEADS):
        oh = o[:, DN_D * h:DN_D * (h + 1)]
        zh = z[:, DN_D * h:DN_D * (h + 1)]
        y = oh * lax.rsqrt(jnp.mean(oh * oh, axis=-1, keepdims=True) + NORM_EPS) * nw
        outs.append(y * _silu(zh))
    return (jnp.concatenate(outs, axis=1),)


def _memattn_fn(pids, qm, kvm):
    kmem, vmem = kvm[:, :MEM_W], kvm[:, MEM_W:]
    lane = _iota((1, MEM_W), 1)
    heads = range(MEM_HEADS)
    hm = [jnp.where((lane >> 6) == h, 1.0, 0.0) for h in heads]
    s = [mm(qm * (hm[h] * MEM_DH ** -0.5), kmem, "nt", "bf16") for h in heads]
    e = [jnp.exp(t - lax.stop_gradient(jnp.max(t, axis=-1, keepdims=True))) for t in s]
    o = [mm(e[h], vmem, "nn", "bf16") * (hm[h] / jnp.sum(e[h], axis=-1, keepdims=True)) for h in heads]
    return ((o[0] + o[1]) + (o[2] + o[3]),)


def _ln_fn(pids, h, mix, g, b):
    x = DN_ALPHA * h + mix
    mu = jnp.mean(x, axis=-1, keepdims=True)
    xc = x - mu
    var = jnp.mean(xc * xc, axis=-1, keepdims=True)
    return (xc * lax.rsqrt(var + LN_EPS) * g + b,)


def _rope_matrix():
    i, j = _iota((128, 128), 0), _iota((128, 128), 1)
    jj = j & 63
    return jnp.where((jj < 32) & (i == j + 32), -1.0, 0.0) + jnp.where((jj >= 32) & (i == j - 32), 1.0, 0.0)


def _rope128(x, cos, sin, rot):
    return x * cos + mm(x, rot, "nn", "selb") * sin


def _krope_fn(pids, kraw, cos, sin):
    rot = _rope_matrix()
    return (jnp.concatenate([_rope128(kraw[:, 128 * g:128 * (g + 1)], cos, sin, rot) for g in range(2)], axis=1),)


def _swa_fn(pids, qraw, cos, sin, k_halo, k_cur, v_halo, v_cur, sinks):
    tb = qraw.shape[0]
    nwin = tb // WINDOW
    rot = _rope_matrix()
    kcat = jnp.concatenate([k_halo, k_cur], axis=0)
    vcat = jnp.concatenate([v_halo, v_cur], axis=0)
    lane = _iota((1, 128), 1)
    halves = (jnp.where(lane < 64, 1.0, 0.0), jnp.where(lane >= 64, 1.0, 0.0))
    group = SWA_HEADS // 2
    rows = group * WINDOW
    in_cur = _iota((rows, WINDOW), 1) <= (_iota((rows, WINDOW), 0) & (WINDOW - 1))
    qg = [_rope128(qraw[:, 128 * p:128 * (p + 1)], cos, sin, rot) for p in range(group)]
    sink = []
    for kv in range(2):
        cols = [jnp.sum(jnp.where(lane == group * kv + i, sinks, 0.0), axis=-1, keepdims=True)
                + jnp.zeros((WINDOW, 1), F32) for i in range(group)]
        sink.append(jnp.concatenate(cols, axis=0))
    units = [(w, kv) for w in range(nwin) for kv in range(2)]
    n_units = range(len(units))
    q6 = [jnp.concatenate([qg[3 * kv + i // 2][WINDOW * w:WINDOW * (w + 1)] * (halves[i % 2] * SWA_DH ** -0.5)
                           for i in range(group)], axis=0) for w, kv in units]
    blk = lambda cat, w, kv: cat[WINDOW * w:WINDOW * (w + 1), 128 * kv:128 * (kv + 1)]
    s_prev = [mm(q6[u], blk(kcat, w, kv), "nt", "bf16") for u, (w, kv) in enumerate(units)]
    s_cur = [mm(q6[u], blk(kcat, w + 1, kv), "nt", "bf16") for u, (w, kv) in enumerate(units)]
    s = [jnp.where(in_cur, s_cur[u], jnp.where(pids[0] * nwin + w > 0, s_prev[u], NEG_BIG))
         for u, (w, kv) in enumerate(units)]
    m = [lax.stop_gradient(jnp.maximum(jnp.max(s[u], axis=-1, keepdims=True), sink[kv]))
         for u, (w, kv) in enumerate(units)]
    e = [jnp.exp(s[u] - m[u]) for u in n_units]
    denom = [jnp.sum(e[u], axis=-1, keepdims=True) + jnp.exp(sink[kv] - m[u]) for u, (w, kv) in enumerate(units)]
    o = [(mm(jnp.where(in_cur, e[u], 0.0), blk(vcat, w + 1, kv), "nn", "bf16")
          + mm(jnp.where(in_cur, 0.0, e[u]), blk(vcat, w, kv), "nn", "bf16")) / denom[u]
         for u, (w, kv) in enumerate(units)]
    out_rows = []
    for w in range(nwin):
        lanes = []
        for p in range(group):
            ou = o[units.index((w, p // 3))]
            i = 2 * (p % 3)
            lanes.append(ou[WINDOW * i:WINDOW * (i + 1)] * halves[0] + ou[WINDOW * (i + 1):WINDOW * (i + 2)] * halves[1])
        out_rows.append(jnp.concatenate(lanes, axis=1))
    return (jnp.concatenate(out_rows, axis=0),)


def _conv_fwd(proj, conv_w, tb):
    t_total = proj.shape[0]
    width = conv_w.shape[1]
    nb = t_total // tb

    def body(cur_ref, prev_ref, w_ref, out_ref):
        i = pl.program_id(0)
        prev = jnp.where(i > 0, prev_ref[...], 0.0)
        xcat = jnp.concatenate([prev, cur_ref[...]], axis=0)
        acc = xcat[8:] * w_ref[3:4, :]
        for j in range(3):
            acc = acc + pltpu.roll(xcat, 3 - j, 0)[8:] * w_ref[j:j + 1, :]
        out_ref[...] = acc

    return pl.pallas_call(
        body, grid=(nb,),
        in_specs=[pl.BlockSpec((tb, width), lambda i: (i, 0)),
                  pl.BlockSpec((8, width), lambda i: (jnp.maximum(i * (tb // 8) - 1, 0), 0)),
                  _whole((4, width))],
        out_specs=pl.BlockSpec((tb, width), lambda i: (i, 0)), out_shape=_sds((t_total, width)),
        name="conv_fwd", compiler_params=_cparams(("parallel",)))(proj, proj, conv_w)


def _conv_bwd(dc, proj, conv_w, tb):
    t_total, width = dc.shape
    nb = t_total // tb

    def body(dcur_ref, dnext_ref, cur_ref, prev_ref, w_ref, dx_ref, dw_ref):
        i = pl.program_id(0)
        dnext = jnp.where(i < nb - 1, dnext_ref[...], 0.0)
        dcur = dcur_ref[...]
        dcat = jnp.concatenate([dcur, dnext], axis=0)
        prev = jnp.where(i > 0, prev_ref[...], 0.0)
        xcat = jnp.concatenate([prev, cur_ref[...]], axis=0)

        @pl.when(i == 0)
        def _():
            dw_ref[...] = jnp.zeros(dw_ref.shape, F32)

        dx = dcur * w_ref[3:4, :]
        dw_ref[3:4, :] += jnp.sum(dcur * xcat[8:], axis=0, keepdims=True)
        for j in range(3):
            dx = dx + pltpu.roll(dcat, 8 - (3 - j), 0)[8:] * w_ref[j:j + 1, :]
            dw_ref[j:j + 1, :] += jnp.sum(dcur * pltpu.roll(xcat, 3 - j, 0)[8:], axis=0, keepdims=True)
        dx_ref[...] = dx.astype(dx_ref.dtype)

    return pl.pallas_call(
        body, grid=(nb,),
        in_specs=[pl.BlockSpec((tb, width), lambda i: (i, 0)),
                  pl.BlockSpec((8, width), lambda i: (jnp.minimum((i + 1) * (tb // 8), t_total // 8 - 1), 0)),
                  pl.BlockSpec((tb, width), lambda i: (i, 0)),
                  pl.BlockSpec((8, width), lambda i: (jnp.maximum(i * (tb // 8) - 1, 0), 0)),
                  _whole((4, width))],
        out_specs=[pl.BlockSpec((tb, width), lambda i: (i, 0)), _whole((4, width))],
        out_shape=[_sds((t_total, width), BF16), _sds((4, width))],
        name="conv_bwd", compiler_params=_cparams(("arbitrary",)))(dc, dc, proj, proj, conv_w)


def _head_spec(tb, nb=None):
    if nb is None:
        return pl.BlockSpec((tb, DN_D), lambda h, i: (i, h))
    return pl.BlockSpec((tb, DN_D), lambda h, i: (nb - 1 - i, h))


def _intra_spec(tb, nb=None):
    if nb is None:
        return pl.BlockSpec((None, tb, PAIR), lambda h, i: (h, i, 0))
    return pl.BlockSpec((None, tb, PAIR), lambda h, i: (h, nb - 1 - i, 0))


def _scan_specs(tb, nb=None):
    blk = (lambda i: i) if nb is None else (lambda i: nb - 1 - i)
    rows = pl.BlockSpec((tb, DN_W), lambda i: (blk(i), 0))
    pair = pl.BlockSpec((DN_HEADS, tb, PAIR), lambda i: (0, blk(i), 0))
    states = pl.BlockSpec((DN_HEADS, tb // CHUNK, DN_D, DN_D), lambda i: (0, blk(i), 0, 0))
    return rows, pair, states


def _dn2_fwd(qd, kd, u, w, intra, cd, tb):
    t_total = qd.shape[0]
    rows, pair, states = _scan_specs(tb)

    def body(qd_ref, kd_ref, u_ref, w_ref, a_ref, cd_ref, o_ref, save_ref, state):
        @pl.when(pl.program_id(0) == 0)
        def _():
            state[...] = jnp.zeros(state.shape, F32)

        heads = range(DN_HEADS)
        lanes = [pl.ds(DN_D * h, DN_D) for h in heads]
        for j in range(tb // CHUNK):
            sl = pl.ds(CHUNK * j, CHUNK)
            s0 = tuple(state[h] for h in heads)
            for h in heads:
                save_ref[h, j] = s0[h]
            per_head = lambda ref: tuple(ref[sl, lanes[h]] for h in heads)
            out, s1 = _dn2_step(j % 2, s0, per_head(qd_ref), per_head(kd_ref), per_head(u_ref), per_head(w_ref),
                                tuple(a_ref[h, sl, :] for h in heads),
                                tuple(cd_ref[pl.ds(CHUNK * j, 1), lanes[h]] for h in heads))
            for h in heads:
                o_ref[sl, lanes[h]] = out[h]
                state[h] = s1[h]

    return pl.pallas_call(
        body, grid=(t_total // tb,), in_specs=[rows, rows, rows, rows, pair, rows],
        out_specs=[rows, states],
        out_shape=[_sds((t_total, DN_W)), _sds((DN_HEADS, t_total // CHUNK, DN_D, DN_D))],
        scratch_shapes=[pltpu.VMEM((DN_HEADS, DN_D, DN_D), F32)], name="dn2_fwd",
        compiler_params=_cparams(("arbitrary",)))(qd, kd, u, w, intra, cd)


def _dn2_bwd(qd, kd, u, w, intra, cd, saved, d_o, tb):
    t_total = qd.shape[0]
    nb = t_total // tb
    rows, pair, states = _scan_specs(tb, nb)

    def body(qd_ref, kd_ref, u_ref, w_ref, a_ref, cd_ref, save_ref, do_ref,
             dqd_ref, dkd_ref, du_ref, dw_ref, da_ref, dcd_ref, dstate):
        @pl.when(pl.program_id(0) == 0)
        def _():
            dstate[...] = jnp.zeros(dstate.shape, F32)

        first_row = _iota((CHUNK, DN_D), 0) == 0
        heads = range(DN_HEADS)
        lanes = [pl.ds(DN_D * h, DN_D) for h in heads]
        for j in reversed(range(tb // CHUNK)):
            sl = pl.ds(CHUNK * j, CHUNK)
            per_head = lambda ref: tuple(ref[sl, lanes[h]] for h in heads)
            _, vjp = jax.vjp(functools.partial(_dn2_step, j % 2), tuple(save_ref[h, j] for h in heads),
                             per_head(qd_ref), per_head(kd_ref), per_head(u_ref), per_head(w_ref),
                             tuple(a_ref[h, sl, :] for h in heads),
                             tuple(cd_ref[pl.ds(CHUNK * j, 1), lanes[h]] for h in heads))
            ds0, dqd, dkd, du, dw, da, dcd = vjp((per_head(do_ref), tuple(dstate[h] for h in heads)))
            for h in heads:
                dqd_ref[sl, lanes[h]] = dqd[h]
                dkd_ref[sl, lanes[h]] = dkd[h]
                du_ref[sl, lanes[h]] = du[h]
                dw_ref[sl, lanes[h]] = dw[h]
                da_ref[h, sl, :] = da[h]
                dcd_ref[sl, lanes[h]] = jnp.where(first_row, dcd[h], 0.0)
                dstate[h] = ds0[h]

    full = _sds((t_total, DN_W))
    return pl.pallas_call(
        body, grid=(nb,),
        in_specs=[rows, rows, rows, rows, pair, rows, states, rows],
        out_specs=[rows, rows, rows, rows, pair, rows],
        out_shape=[full, full, full, full, _sds((DN_HEADS, t_total, PAIR)), full],
        scratch_shapes=[pltpu.VMEM((DN_HEADS, DN_D, DN_D), F32)], name="dn2_bwd",
        compiler_params=_cparams(("arbitrary",)))(qd, kd, u, w, intra, cd, saved, d_o)


def _loss_and_grad(y, target, tb):
    t_total, d = y.shape

    def body(y_ref, t_ref, dy_ref, acc_ref):
        @pl.when(pl.program_id(0) == 0)
        def _():
            acc_ref[...] = jnp.zeros(acc_ref.shape, F32)

        err = y_ref[...] - t_ref[...]
        dy_ref[...] = err * (1.0 / d)
        acc_ref[...] += jnp.sum(err * err, axis=0, keepdims=True)

    dy, acc = pl.pallas_call(
        body, grid=(t_total // tb,), in_specs=[_rows(tb, d), _rows(tb, d)],
        out_specs=[_rows(tb, d), _whole((1, d))], out_shape=[_sds((t_total, d)), _sds((1, d))],
        name="loss", compiler_params=_cparams(("arbitrary",)))(y, target)
    return 0.5 * jnp.sum(acc) / d, dy


def _halo_sum(mains, halos, tb):
    t_total, width = mains[0].shape
    nb = t_total // tb
    n = len(mains)

    def body(*refs):
        out_ref = refs[-1]
        i = pl.program_id(0)
        tot = refs[0][...]
        for r in refs[1:n]:
            tot = tot + r[...]
        hal = refs[n][...]
        for r in refs[n + 1:2 * n]:
            hal = hal + r[...]
        hal = jnp.where(i < nb - 1, hal, 0.0)
        out_ref[...] = tot + jnp.concatenate([jnp.zeros((tb - WINDOW, width), F32), hal], axis=0)

    return pl.pallas_call(
        body, grid=(nb,),
        in_specs=[_rows(tb, width)] * n
        + [pl.BlockSpec((None, WINDOW, width), lambda i: (jnp.minimum(i + 1, nb - 1), 0, 0))] * n,
        out_specs=_rows(tb, width), out_shape=_sds((t_total, width)), name="halo_sum",
        compiler_params=_cparams(("parallel",)))(*mains, *halos)


def _adamw(recvs, w, m, v, tr, name):
    slots, _, c_total = recvs[0].shape
    r_total = w.shape[0]
    assert sum(r.shape[1] for r in recvs) == r_total
    tr = min([tr] + [r.shape[1] for r in recvs])
    assert all(r.shape[1] % tr == 0 for r in recvs)
    starts = [sum(r.shape[1] for r in recvs[:i]) // tr for i in range(len(recvs))]
    counts = [r.shape[1] // tr for r in recvs]
    c1 = 1.0 / (1.0 - ADAM_B1 ** ADAM_STEP)
    c2 = 1.0 / (1.0 - ADAM_B2 ** ADAM_STEP)

    def body(*refs):
        recv_refs = refs[:len(recvs)]
        w_ref, m_ref, v_ref, g_ref, d_ref, nm_ref, nv_ref = refs[len(recvs):]
        g = None
        for recv_ref, start in zip(recv_refs, starts):
            part = recv_ref[0].astype(F32)
            for s in range(1, slots):
                part = part + recv_ref[s].astype(F32)
            g = part if g is None else jnp.where(pl.program_id(0) >= start, part, g)
        nm = ADAM_B1 * m_ref[...] + (1.0 - ADAM_B1) * g
        nv = ADAM_B2 * v_ref[...] + (1.0 - ADAM_B2) * (g * g)
        g_ref[...] = g
        nm_ref[...] = nm
        nv_ref[...] = nv
        d_ref[...] = -ADAM_LR * ((nm * c1) / (jnp.sqrt(nv * c2) + ADAM_EPS) + ADAM_WD * w_ref[...])

    blk = pl.BlockSpec((tr, c_total), lambda i: (i, 0))
    recv_specs = [pl.BlockSpec((slots, tr, c_total), lambda i, s=s, n=n: (0, jnp.clip(i - s, 0, n - 1), 0))
                  for s, n in zip(starts, counts)]
    return pl.pallas_call(
        body, grid=(r_total // tr,), in_specs=recv_specs + [blk, blk, blk],
        out_specs=[blk] * 4, out_shape=[_sds((r_total, c_total))] * 4, name=name,
        compiler_params=_cparams(("parallel",)))(*recvs, w, m, v)


def _me_and_peers():
    x, y, c = lax.axis_index("x"), lax.axis_index("y"), lax.axis_index("c")
    me = 4 * x + 2 * y + c
    peers = []
    for k in range(1, N_DEV):
        px = 1 - x if (k >> 2) & 1 else x
        py = 1 - y if (k >> 1) & 1 else y
        pc = 1 - c if k & 1 else c
        peers.append(((px, py, pc), 4 * px + 2 * py + pc))
    return me, peers


def _small_exchange(packed, reduce):
    r_total = packed.shape[0]

    def body(p_ref, out_ref, gath_ref, send_sems, recv_sems):
        me, peers = _me_and_peers()
        gath_ref[me] = p_ref[...]
        copies = []
        for k, (dev, _) in enumerate(peers):
            cp = pltpu.make_async_remote_copy(src_ref=p_ref, dst_ref=gath_ref.at[me], send_sem=send_sems.at[k],
                                              recv_sem=recv_sems.at[k], device_id=dev,
                                              device_id_type=pl.DeviceIdType.MESH)
            cp.start()
            copies.append(cp)
        for k, (dev, idx) in enumerate(peers):
            pltpu.make_async_remote_copy(src_ref=p_ref, dst_ref=gath_ref.at[idx], send_sem=send_sems.at[k],
                                         recv_sem=recv_sems.at[k], device_id=dev,
                                         device_id_type=pl.DeviceIdType.MESH).wait_recv()
        for cp in copies:
            cp.wait_send()
        if reduce:
            tot = gath_ref[0]
            for d in range(1, N_DEV):
                tot = tot + gath_ref[d]
            out_ref[...] = tot
        else:
            out_ref[...] = gath_ref[...]

    out_shape = _sds((r_total, 128)) if reduce else _sds((N_DEV, r_total, 128))
    return pl.pallas_call(
        body, in_specs=[pl.BlockSpec(memory_space=pltpu.VMEM)], out_specs=pl.BlockSpec(memory_space=pltpu.VMEM),
        out_shape=out_shape,
        scratch_shapes=[pltpu.VMEM((N_DEV, r_total, 128), F32), pltpu.SemaphoreType.DMA((N_DEV - 1,)),
                        pltpu.SemaphoreType.DMA((N_DEV - 1,))],
        name="small_allreduce" if reduce else "small_allgather")(packed)


def _slot(ref, axis, idx, size):
    sel = [slice(None)] * len(ref.shape)
    sel[axis] = idx if size is None else pl.ds(pl.multiple_of(idx * size, size), size)
    return ref.at[tuple(sel)]


def _big_exchange(srcs, dst_shapes, src_view, dst_view, name):
    n = len(srcs)

    def body(*refs):
        src_refs, dst_refs = refs[:n], refs[n:2 * n]
        send_sems, recv_sems, local_sems = refs[2 * n:]
        me, peers = _me_and_peers()
        local, remote = [], []
        for t in range(n):
            loc = pltpu.make_async_copy(src_view(t, src_refs[t], me), dst_view(t, dst_refs[t], me), local_sems.at[t])
            loc.start()
            local.append(loc)
            for k, (dev, idx) in enumerate(peers):
                cp = pltpu.make_async_remote_copy(
                    src_ref=src_view(t, src_refs[t], idx), dst_ref=dst_view(t, dst_refs[t], me),
                    send_sem=send_sems.at[t, k], recv_sem=recv_sems.at[t, k], device_id=dev,
                    device_id_type=pl.DeviceIdType.MESH)
                cp.start()
                remote.append(cp)
        for t in range(n):
            for k, (dev, idx) in enumerate(peers):
                pltpu.make_async_remote_copy(
                    src_ref=src_view(t, src_refs[t], me), dst_ref=dst_view(t, dst_refs[t], idx),
                    send_sem=send_sems.at[t, k], recv_sem=recv_sems.at[t, k], device_id=dev,
                    device_id_type=pl.DeviceIdType.MESH).wait_recv()
        for cp in remote:
            cp.wait_send()
        for cp in local:
            cp.wait()

    any_spec = pl.BlockSpec(memory_space=pl.ANY)
    return pl.pallas_call(
        body, in_specs=[any_spec] * n, out_specs=[any_spec] * n, out_shape=dst_shapes,
        scratch_shapes=[pltpu.SemaphoreType.DMA((n, N_DEV - 1)), pltpu.SemaphoreType.DMA((n, N_DEV - 1)),
                        pltpu.SemaphoreType.DMA((n,))],
        name=name)(*srcs)


BIG = {
    "a_w_in": (1, (2, 1024, A_IN)),
    "b_w_in": (1, (2, 1024, 1024)),
    "w_kv_shared": (0, (1024, 256)),
    "mem_w_kv": (1, (4, 1024, 512)),
    "w_o": (1, (4, 1024, 1024)),
    "mlp_w_up": (2, (4, 1024, 4096)),
    "mlp_w_down": (1, (4, 4096, 1024)),
}
BIG_NAMES = tuple(BIG)


def _gather_plan(names, shards):
    dst_shapes, axes, sizes = [], [], []
    for name, s in zip(names, shards):
        axis = BIG[name][0] - (len(BIG[name][1]) - s.ndim)
        dst_shapes.append(_sds(tuple(d * N_DEV if a == axis else d for a, d in enumerate(s.shape)), s.dtype))
        axes.append(axis)
        sizes.append(s.shape[axis])
    return dst_shapes, axes, sizes


def _gather_ride(names, shards):
    dst_shapes, axes, sizes = _gather_plan(names, shards)
    return _Ride(shards, dst_shapes, lambda t, ref, idx: ref, lambda t, ref, idx: _slot(ref, axes[t], idx, sizes[t]))


def _scatter_plan(names, grads):
    dst_shapes, axes, sizes = [], [], []
    for name, g in zip(names, grads):
        axis = BIG[name][0] - (len(BIG[name][1]) - g.ndim)
        shard = tuple(d // N_DEV if a == axis else d for a, d in enumerate(g.shape))
        axes.append(axis)
        sizes.append(shard[axis])
        dst_shapes.append(_sds((N_DEV,) + shard, g.dtype))
    return dst_shapes, axes, sizes


def _scatter_ride(names, grads):
    dst_shapes, axes, sizes = _scatter_plan(names, grads)
    return _Ride(grads, dst_shapes, lambda t, ref, idx: _slot(ref, axes[t], idx, sizes[t]),
                 lambda t, ref, idx: ref.at[idx])


def _allgather_weights(names, shards):
    dst_shapes, axes, sizes = _gather_plan(names, shards)
    n = len(shards)

    def body(*refs):
        src_refs, dst_refs = refs[:n], refs[n:2 * n]
        send_sems, recv_sems, local_sems = refs[2 * n:]
        x, y, c = lax.axis_index("x"), lax.axis_index("y"), lax.axis_index("c")
        sibling = (x, y, 1 - c)
        chips = [(1 - x, y), (x, 1 - y), (1 - x, 1 - y)]
        index = lambda px, py, pc: 4 * px + 2 * py + pc

        def copy(t, k, block, to, src=None):
            rows = _slot(dst_refs[t], axes[t], index(*block), sizes[t])
            return pltpu.make_async_remote_copy(
                src_ref=rows if src is None else src, dst_ref=rows, send_sem=send_sems.at[t, k],
                recv_sem=recv_sems.at[t, k], device_id=to, device_id_type=pl.DeviceIdType.MESH)

        started, local = [], []
        for t in range(n):
            mine = pltpu.make_async_copy(src_refs[t], _slot(dst_refs[t], axes[t], index(x, y, c), sizes[t]),
                                         local_sems.at[t])
            mine.start()
            local.append(mine)
            first = [copy(t, 0, (x, y, c), sibling, src=src_refs[t])]
            first += [copy(t, 1 + j, (x, y, c), (*chip, c), src=src_refs[t]) for j, chip in enumerate(chips)]
            for cp in first:
                cp.start()
            started += first
        for t in range(n):
            for j, chip in enumerate(chips):
                copy(t, 1 + j, (*chip, c), (x, y, c)).wait_recv()
                passed = copy(t, 4 + j, (*chip, c), sibling)
                passed.start()
                started.append(passed)
        for t in range(n):
            copy(t, 0, sibling, (x, y, c)).wait_recv()
            for j, chip in enumerate(chips):
                copy(t, 4 + j, (*chip, 1 - c), (x, y, c)).wait_recv()
        for cp in started:
            cp.wait_send()
        for cp in local:
            cp.wait()

    any_spec = pl.BlockSpec(memory_space=pl.ANY)
    return pl.pallas_call(
        body, in_specs=[any_spec] * n, out_specs=[any_spec] * n, out_shape=dst_shapes,
        scratch_shapes=[pltpu.SemaphoreType.DMA((n, N_DEV - 1)), pltpu.SemaphoreType.DMA((n, N_DEV - 1)),
                        pltpu.SemaphoreType.DMA((n,))],
        name="allgather_weights")(*shards)


def _scatter_grads(names, grads):
    dst_shapes, axes, sizes = _scatter_plan(names, grads)

    def src_view(t, ref, idx):
        return _slot(ref, axes[t], idx, sizes[t])

    def dst_view(t, ref, idx):
        return ref.at[idx]

    return _big_exchange(grads, dst_shapes, src_view, dst_view, "scatter_grads")


def _pad_row(vec, width=128):
    return jnp.pad(vec.astype(F32), (0, width - vec.shape[0])).reshape(1, width)


def _block_sizes(t_total):
    return dict(row=min(256, t_total), dn=min(512, t_total), swa=min(256, t_total), scan=min(256, t_total),
                ln=min(512, t_total))


def _ln_grad(h, mix, g, b, dy, tb):
    t_total, d = h.shape

    def body(h_ref, mix_ref, g_ref, dy_ref, dh_ref, dmix_ref, dg_ref, db_ref):
        @pl.when(pl.program_id(0) == 0)
        def _():
            dg_ref[...] = jnp.zeros(dg_ref.shape, F32)
            db_ref[...] = jnp.zeros(db_ref.shape, F32)

        x = DN_ALPHA * h_ref[...] + mix_ref[...]
        xc = x - jnp.mean(x, axis=-1, keepdims=True)
        rstd = lax.rsqrt(jnp.mean(xc * xc, axis=-1, keepdims=True) + LN_EPS)
        xhat = xc * rstd
        dy_val = dy_ref[...]
        dxh = dy_val * g_ref[...]
        m1 = jnp.mean(dxh, axis=-1, keepdims=True)
        m2 = jnp.mean(dxh * xhat, axis=-1, keepdims=True)
        dx = (dxh - m1 - xhat * m2) * rstd
        dmix_ref[...] = dx.astype(dmix_ref.dtype)
        dh_ref[...] = DN_ALPHA * dx
        dg_ref[...] += jnp.sum(dy_val * xhat, axis=0, keepdims=True)
        db_ref[...] += jnp.sum(dy_val, axis=0, keepdims=True)

    return pl.pallas_call(
        body, grid=(t_total // tb,),
        in_specs=[_rows(tb, d), _rows(tb, d), _whole((1, d)), _rows(tb, d)],
        out_specs=[_rows(tb, d), _rows(tb, d), _whole((1, d)), _whole((1, d))],
        out_shape=[_sds((t_total, d)), _sds((t_total, d), BF16), _sds((1, d)), _sds((1, d))],
        name="ln_bwd", compiler_params=_cparams(("arbitrary",)))(h, mix, g, dy)


def _memattn_specs(tb, qcol):
    return [pl.BlockSpec((tb, MEM_W), lambda i: (i, qcol)), _whole((MEM_W, 2 * MEM_W))]


def _act_epilogue(acc):
    r = jnp.maximum(acc, 0.0)
    return (r * r,)


def _dact_epilogue(acc, act):
    return (acc * (2.0 * jnp.sqrt(act.astype(F32))),)


def _add_epilogue(acc, other):
    return (acc + other,)


def _key(name, layer):
    if name == "w_kv_shared":
        return (name, None)
    return (name, layer - N_A if name == "b_w_in" else layer)


_PER_LAYER = ("mem_w_kv", "w_o", "mlp_w_up", "mlp_w_down")
GATHER_FIRST = [_key("a_w_in", 0)]
GATHER_ON_DN1_0 = ([_key(n, 0) for n in _PER_LAYER] + [_key("a_w_in", 1), _key("w_kv_shared", 1)]
                   + [_key(n, 1) for n in _PER_LAYER])
GATHER_ON_DN1_1 = [_key("b_w_in", 2), _key("b_w_in", 3)] + [_key(n, l) for l in (2, 3) for n in _PER_LAYER]
SCATTER_ON_DN1_BWD_1 = GATHER_ON_DN1_1
SCATTER_ON_DN1_BWD_0 = GATHER_ON_DN1_0
SCATTER_LAST = GATHER_FIRST
ALL_KEYS = GATHER_FIRST + GATHER_ON_DN1_0 + GATHER_ON_DN1_1


def _local_step(x, mem, positions, target, ready, shards, small):
    t_total = x.shape[0]
    bs = _block_sizes(t_total)
    tb, tdn, tsw = bs["row"], bs["dn"], bs["swa"]
    nb = t_total // tb
    nbs = t_total // tsw

    inv_freq = ROPE_THETA ** (-jnp.arange(0, SWA_DH, 2, dtype=F32) / SWA_DH)
    ang = positions.astype(F32)[:, None] * inv_freq
    cos = jnp.tile(jnp.cos(ang), (1, 4))
    sin = jnp.tile(jnp.sin(ang), (1, 4))

    mem_b = mem.astype(BF16)
    ready = dict(ready)
    derived = {}

    def gather_ride(keys):
        names = [k[0] for k in keys]
        return names, (_gather_ride(names, [shards[k] for k in keys]) if shards is not None else None)

    def weight(name, l):
        key = _key(name, l)
        if key not in derived:
            w = ready[key]
            if name == "a_w_in":
                w = jnp.concatenate([w[0][:, :3072], w[0][:, 3084:], w[0][:, 3072:3084],
                                     jnp.zeros((D_MODEL, A_IN_PAD - A_IN), BF16)], axis=1)
            elif name == "w_kv_shared":
                w = jnp.concatenate([w[:, 64 * (i // 2):64 * (i // 2 + 1)] for i in range(8)], axis=1)
            else:
                w = w[0]
            derived[key] = w
        return derived[key]

    saved = []
    h, hb = x, x.astype(BF16)
    kr = vd_src = None
    for l in range(DEPTH):
        sv = dict(h=h, hb=hb)
        if l < N_A:
            proj = _matmul(hb, weight("a_w_in", l), "nn", [F32], "mm_proj_a", tn=1152)
            conv_w = small["a_conv_w"][l]
            c = _conv_fwd(proj, conv_w, tb)
            alog, dtb = _pad_row(small["a_A_log"][l]), _pad_row(small["a_dt_bias"][l])
            rowa_in = [c, proj, alog, dtb]
            rowa_specs = [_rows(tb, 3 * DN_W), _rows(tb, 128, 26), _whole((1, 128)), _whole((1, 128))]
            rowa_args = (_rowa_fn, rowa_in, rowa_specs, [_sds((t_total, DN_W))] * 5, [_rows(tb, DN_W)] * 5, (nb,))
            q, k, v, gcb, betab = _block_fwd(*rowa_args, "rowa_fwd")
            hs = _head_spec(tdn)
            dn_grid = (DN_HEADS, t_total // tdn)
            full = _sds((t_total, DN_W))
            full_b = _sds((t_total, DN_W), BF16)
            dn1_out_shapes = [full, full_b, _sds((DN_HEADS, t_total, PAIR), BF16), full_b, full_b, full,
                              _sds((DN_HEADS, t_total, PAIR))]
            dn1_out_specs = [hs, hs, _intra_spec(tdn), hs, hs, hs, _intra_spec(tdn)]
            if shards is not None:
                keys = GATHER_ON_DN1_0 if l == 0 else GATHER_ON_DN1_1
                names, ride = gather_ride(keys)
                (u, w, intra, qd, kd, cd, tinv), got = _block_fwd(
                    _dn1_fn, [q, k, v, gcb, betab], [hs] * 5, dn1_out_shapes, dn1_out_specs, dn_grid,
                    "dn1_fwd_gather%d" % l, ride=ride)
                ready.update(zip(keys, got))
            else:
                u, w, intra, qd, kd, cd, tinv = _block_fwd(_dn1_fn, [q, k, v, gcb, betab], [hs] * 5, dn1_out_shapes,
                                                           dn1_out_specs, dn_grid, "dn1_fwd")
            o, states = _dn2_fwd(qd, kd, u, w, intra, cd, bs["scan"])
            nw = small["a_norm_w"][l].reshape(1, DN_D)
            post_in = [o, proj, nw]
            post_specs = [_rows(tb, DN_W), _rows(tb, DN_W, 3), _whole((1, DN_D))]
            (og,) = _block_fwd(_post_fn, post_in, post_specs, [_sds((t_total, DN_W), BF16)], [_rows(tb, DN_W)],
                               (nb,), "post_fwd")
            qm_col = 12
            sv.update(proj=proj, c=c, rowa_in=rowa_in, rowa_specs=rowa_specs, dn1_in=[q, k, v, gcb, betab, tinv],
                      dn2_in=[qd, kd, u, w, intra, cd], states=states, post_in=post_in, post_specs=post_specs,
                      conv_w=conv_w)
        else:
            jb = l - N_A
            proj = _matmul(hb, weight("b_w_in", l), "nn", [F32], "mm_proj_b")
            sinks = _pad_row(small["b_sinks"][jb])
            swa_in = [proj, cos, sin, kr, kr, vd_src, vd_src, sinks]
            swa_specs = [_rows(tsw, DN_W), _rows(tsw, 128), _rows(tsw, 128),
                         pl.BlockSpec((WINDOW, 256), lambda i: (jnp.maximum(i * (tsw // WINDOW) - 1, 0), 0)),
                         _rows(tsw, 256),
                         pl.BlockSpec((WINDOW, 256), lambda i: (jnp.maximum(i * (tsw // WINDOW) - 1, 0), 1)),
                         _rows(tsw, 256, 1), _whole((1, 128))]
            (og,) = _block_fwd(_swa_fn, swa_in, swa_specs, [_sds((t_total, DN_W), BF16)], [_rows(tsw, DN_W)],
                               (nbs,), "swa_fwd")
            qm_col = 3
            sv.update(proj=proj, swa_in=swa_in, swa_specs=swa_specs)
        kvm = _matmul(mem_b, weight("mem_w_kv", l), "nn", [F32], "mm_memkv", tm=256)
        mem_in = [proj, kvm]
        (mo,) = _block_fwd(_memattn_fn, mem_in, _memattn_specs(tb, qm_col), [_sds((t_total, MEM_W), BF16)],
                           [_rows(tb, MEM_W)], (nb,), "memattn_fwd")
        mixin = jnp.concatenate([og, mo], axis=1)
        g0, b0 = small["ln_g"][l, 0].reshape(1, -1), small["ln_b"][l, 0].reshape(1, -1)
        mix, h1, h1b = _matmul_ln(mixin, weight("w_o", l), h, g0, b0, "mm_wo_ln")
        act = _matmul(h1b, weight("mlp_w_up", l), "nn", [BF16], "mm_up", epi=_act_epilogue, tm=2048)
        g1, b1 = small["ln_g"][l, 1].reshape(1, -1), small["ln_b"][l, 1].reshape(1, -1)
        mlp, h2, h2b = _matmul_ln(act, weight("mlp_w_down", l), h1, g1, b1, "mm_down_ln")
        sv.update(kvm=kvm, mem_in=mem_in, qm_col=qm_col, mixin=mixin, mix=mix, ln0=(g0, b0), h1=h1, h1b=h1b,
                  act=act, mlp=mlp, ln1=(g1, b1))
        saved.append(sv)
        h, hb = h2, h2b
        if l == N_A - 1:
            kvd = _matmul(hb, weight("w_kv_shared", l), "nn", [F32], "mm_kvd")
            krope_in = [kvd, cos, sin]
            krope_specs = [_rows(tb, 256), _rows(tb, 128), _rows(tb, 128)]
            (kr,) = _block_fwd(_krope_fn, krope_in, krope_specs, [_sds((t_total, 256))], [_rows(tb, 256)], (nb,),
                               "krope_fwd")
            vd_src = kvd

    loss, dh = _loss_and_grad(h, target, tb)

    grads = {}

    def scatter_ride(keys):
        return _scatter_ride([k[0] for k in keys], [grads[k] for k in keys]) if shards is not None else None

    sg = dict(a_conv_w=[None] * N_A, a_A_log=[None] * N_A, a_dt_bias=[None] * N_A, a_norm_w=[None] * N_A,
              b_sinks=[None] * (DEPTH - N_A), ln_g=[[None, None] for _ in range(DEPTH)],
              ln_b=[[None, None] for _ in range(DEPTH)])
    dk_parts, dv_parts = [], []
    for l in reversed(range(DEPTH)):
        sv = saved[l]
        if l == N_A - 1:
            dkr = _halo_sum([p[0] for p in dk_parts], [p[1] for p in dk_parts], tsw)
            dvv = _halo_sum([p[0] for p in dv_parts], [p[1] for p in dv_parts], tsw)
            (dkraw,) = _block_bwd(_krope_fn, krope_in, krope_specs, [dkr], [_rows(tb, 256)], ["s", None, None],
                                  [_sds((t_total, 256), BF16)], [_rows(tb, 256)], (nb,), "krope_bwd")
            dkvd = jnp.concatenate([dkraw, dvv.astype(BF16)], axis=1)
            g_kvd = _matmul(saved[l + 1]["hb"], dkvd, "tn", [F32], "mm_dw_kvd", tm=1024, tn=512)
            dh = _matmul(dkvd, weight("w_kv_shared", l), "nt", [F32], "mm_dx_kvd", epi=_add_epilogue, extras=[dh],
                         tn=1024, tk=512)
            grads[_key("w_kv_shared", l)] = jnp.concatenate(
                [g_kvd[:, 128 * i:128 * i + 64] + g_kvd[:, 128 * i + 64:128 * (i + 1)] for i in range(4)],
                axis=1).astype(BF16)
        g1, b1 = sv["ln1"]
        dh1a, dmlp, dg1, db1 = _ln_grad(sv["h1"], sv["mlp"], g1, b1, dh, bs["ln"])
        dup = _matmul(dmlp, weight("mlp_w_down", l), "nt", [BF16], "mm_dact", epi=_dact_epilogue, extras=[sv["act"]],
                      tm=2048)
        grads[_key("mlp_w_down", l)] = _matmul(sv["act"], dmlp, "tn", [BF16], "mm_dw_down", tk=4096)[None]
        grads[_key("mlp_w_up", l)] = _matmul(sv["h1b"], dup, "tn", [BF16], "mm_dw_up", tk=4096)[None]
        dh1 = _matmul(dup, weight("mlp_w_up", l), "nt", [F32], "mm_dx_up", epi=_add_epilogue, extras=[dh1a], tk=2048)
        g0, b0 = sv["ln0"]
        dha, dmix, dg0, db0 = _ln_grad(sv["h"], sv["mix"], g0, b0, dh1, bs["ln"])
        sg["ln_g"][l] = [dg0, dg1]
        sg["ln_b"][l] = [db0, db1]
        grads[_key("w_o", l)] = _matmul(sv["mixin"], dmix, "tn", [BF16], "mm_dw_o", tk=4096)[None]
        dmixin = _matmul(dmix, weight("w_o", l), "nt", [F32], "mm_dx_o", tn=1024)
        dqm, dkvm = _block_bwd(_memattn_fn, sv["mem_in"], _memattn_specs(tb, sv["qm_col"]), [dmixin],
                               [_rows(tb, MEM_W, 3)], ["s", "a"],
                               [_sds((t_total, MEM_W), BF16), _sds((MEM_W, 2 * MEM_W))],
                               [_rows(tb, MEM_W), _whole((MEM_W, 2 * MEM_W))], (nb,), "memattn_bwd")
        grads[_key("mem_w_kv", l)] = _matmul(mem_b, dkvm.astype(BF16), "tn", [BF16], "mm_dw_memkv", tm=1024,
                                             tn=512)[None]
        if l < N_A:
            d_o, dz, dnw = _block_bwd(_post_fn, sv["post_in"], sv["post_specs"], [dmixin], [_rows(tb, DN_W)],
                                      ["s", "s", "a"],
                                      [_sds((t_total, DN_W)), _sds((t_total, DN_W), BF16), _sds((1, DN_D))],
                                      [_rows(tb, DN_W), _rows(tb, DN_W), _whole((1, DN_D))], (nb,), "post_bwd")
            sg["a_norm_w"][l] = dnw
            dqd, dkd, du, dw, da, dcd = _dn2_bwd(*sv["dn2_in"], sv["states"], d_o, bs["scan"])
            hs = _head_spec(tdn)
            full = _sds((t_total, DN_W))
            dn1_bwd_args = (_dn1_fn_known, sv["dn1_in"], [hs] * 5 + [_intra_spec(tdn)], [du, dw, da, dqd, dkd, dcd],
                            [hs, hs, _intra_spec(tdn), hs, hs, hs], ["s"] * 5 + [None], [full] * 5, [hs] * 5,
                            (DN_HEADS, t_total // tdn))
            if shards is not None:
                keys = SCATTER_ON_DN1_BWD_1 if l == N_A - 1 else SCATTER_ON_DN1_BWD_0
                (dq, dk, dv, dgc, dbeta), got = _block_bwd(*dn1_bwd_args, "dn1_bwd_scatter%d" % l,
                                                           ride=scatter_ride(keys))
                grads.update(zip(keys, got))
            else:
                dq, dk, dv, dgc, dbeta = _block_bwd(*dn1_bwd_args, "dn1_bwd")
            dc, dab, dalog, ddtb = _block_bwd(
                _rowa_fn, sv["rowa_in"], sv["rowa_specs"], [dq, dk, dv, dgc, dbeta], [_rows(tb, DN_W)] * 5,
                ["s", "s", "a", "a"],
                [_sds((t_total, 3 * DN_W)), _sds((t_total, 128), BF16), _sds((1, 128)), _sds((1, 128))],
                [_rows(tb, 3 * DN_W), _rows(tb, 128), _whole((1, 128)), _whole((1, 128))], (nb,), "rowa_bwd")
            sg["a_A_log"][l] = dalog[0, :DN_HEADS]
            sg["a_dt_bias"][l] = ddtb[0, :DN_HEADS]
            dx, dconv = _conv_bwd(dc, sv["proj"], sv["conv_w"], tb)
            sg["a_conv_w"][l] = dconv
            dproj = jnp.concatenate([dx, dz, dqm, dab], axis=1)
            g_in = _matmul(sv["hb"], dproj, "tn", [BF16], "mm_dw_a", tm=1024, tn=1152, tk=2048)
            grads[_key("a_w_in", l)] = jnp.concatenate([g_in[:, :3072], g_in[:, 3328:3340], g_in[:, 3072:3328]],
                                                       axis=1)[None]
            dh = _matmul(dproj, weight("a_w_in", l), "nt", [F32], "mm_dx_a", epi=_add_epilogue, extras=[dha], tk=1152)
        else:
            jb = l - N_A
            swa_kinds = ["s", None, None, "s", "s", "s", "s", "a"]
            halo_spec = pl.BlockSpec((None, WINDOW, 256), lambda i: (i, 0, 0))
            dq, dkh, dkc, dvh, dvc, dsink = _block_bwd(
                _swa_fn, sv["swa_in"], sv["swa_specs"], [dmixin], [_rows(tsw, DN_W)], swa_kinds,
                [_sds((t_total, DN_W), BF16), _sds((nbs, WINDOW, 256)), _sds((t_total, 256)),
                 _sds((nbs, WINDOW, 256)), _sds((t_total, 256)), _sds((1, 128))],
                [_rows(tsw, DN_W), halo_spec, _rows(tsw, 256), halo_spec, _rows(tsw, 256), _whole((1, 128))],
                (nbs,), "swa_bwd")
            sg["b_sinks"][jb] = dsink[0, :SWA_HEADS]
            dk_parts.append((dkc, dkh))
            dv_parts.append((dvc, dvh))
            dproj = jnp.concatenate([dq, dqm], axis=1)
            grads[_key("b_w_in", l)] = _matmul(sv["hb"], dproj, "tn", [BF16], "mm_dw_b", tk=4096)[None]
            dh = _matmul(dproj, weight("b_w_in", l), "nt", [F32], "mm_dx_b", epi=_add_epilogue, extras=[dha], tn=1024)

    small_grads = dict(
        a_conv_w=jnp.stack(sg["a_conv_w"]), a_A_log=jnp.stack(sg["a_A_log"]), a_dt_bias=jnp.stack(sg["a_dt_bias"]),
        a_norm_w=jnp.concatenate(sg["a_norm_w"], axis=0), b_sinks=jnp.stack(sg["b_sinks"]),
        ln_g=jnp.stack([jnp.concatenate(p, axis=0) for p in sg["ln_g"]]),
        ln_b=jnp.stack([jnp.concatenate(p, axis=0) for p in sg["ln_b"]]))
    return loss, dh, grads, small_grads


def _pack(arrays, rows):
    flat = []
    for a in arrays:
        v = a.astype(F32).reshape(-1)
        flat.append(jnp.pad(v, (0, (-v.shape[0]) % 128)))
    flat = jnp.concatenate(flat)
    return jnp.pad(flat, (0, rows * 128 - flat.shape[0])).reshape(rows, 128)


def _unpack(slab, shapes):
    flat = slab.reshape(slab.shape[:-2] + (-1,))
    out, off = [], 0
    for s in shapes:
        n = math.prod(s)
        out.append(flat[..., off:off + n].reshape(slab.shape[:-2] + tuple(s)))
        off += n + (-n) % 128
    return out


def _rows_for(shapes):
    rows = sum((math.prod(s) + 127) // 128 for s in shapes)
    return rows + (-rows) % 8


SMALL_NAMES = ("a_conv_w", "a_A_log", "a_dt_bias", "a_norm_w", "b_sinks", "ln_g", "ln_b")
SMALL_SHARDED = {"a_conv_w": 2, "ln_g": 2, "ln_b": 2}
SMALL_FULL = {"a_conv_w": (2, 4, 2304), "a_A_log": (2, 6), "a_dt_bias": (2, 6), "a_norm_w": (2, 128),
              "b_sinks": (2, 12), "ln_g": (4, 2, 1024), "ln_b": (4, 2, 1024)}


def kernel(x, mem, positions, a_w_in, a_conv_w, a_A_log, a_dt_bias, a_norm_w, b_w_in, b_sinks, w_kv_shared, mem_w_kv, w_o, mlp_w_up, mlp_w_down, ln_g, ln_b, loss_target, m_a_w_in, m_a_conv_w, m_a_A_log, m_a_dt_bias, m_a_norm_w, m_b_w_in, m_b_sinks, m_w_kv_shared, m_mem_w_kv, m_w_o, m_mlp_w_up, m_mlp_w_down, m_ln_g, m_ln_b, v_a_w_in, v_a_conv_w, v_a_A_log, v_a_dt_bias, v_a_norm_w, v_b_w_in, v_b_sinks, v_w_kv_shared, v_mem_w_kv, v_w_o, v_mlp_w_up, v_mlp_w_down, v_ln_g, v_ln_b):
    params = dict(a_w_in=a_w_in, a_conv_w=a_conv_w, a_A_log=a_A_log, a_dt_bias=a_dt_bias, a_norm_w=a_norm_w,
                  b_w_in=b_w_in, b_sinks=b_sinks, w_kv_shared=w_kv_shared, mem_w_kv=mem_w_kv, w_o=w_o,
                  mlp_w_up=mlp_w_up, mlp_w_down=mlp_w_down, ln_g=ln_g, ln_b=ln_b)
    mom = dict(a_w_in=m_a_w_in, a_conv_w=m_a_conv_w, a_A_log=m_a_A_log, a_dt_bias=m_a_dt_bias, a_norm_w=m_a_norm_w,
               b_w_in=m_b_w_in, b_sinks=m_b_sinks, w_kv_shared=m_w_kv_shared, mem_w_kv=m_mem_w_kv, w_o=m_w_o,
               mlp_w_up=m_mlp_w_up, mlp_w_down=m_mlp_w_down, ln_g=m_ln_g, ln_b=m_ln_b)
    var = dict(a_w_in=v_a_w_in, a_conv_w=v_a_conv_w, a_A_log=v_a_A_log, a_dt_bias=v_a_dt_bias, a_norm_w=v_a_norm_w,
               b_w_in=v_b_w_in, b_sinks=v_b_sinks, w_kv_shared=v_w_kv_shared, mem_w_kv=v_mem_w_kv, w_o=v_w_o,
               mlp_w_up=v_mlp_w_up, mlp_w_down=v_mlp_w_down, ln_g=v_ln_g, ln_b=v_ln_b)
    me = 4 * lax.axis_index("x") + 2 * lax.axis_index("y") + lax.axis_index("c")

    shards = {(n, i): (params[n] if i is None else params[n][i:i + 1]).astype(BF16) for n, i in ALL_KEYS}
    first = [k[0] for k in GATHER_FIRST]
    ready = dict(zip(GATHER_FIRST, _allgather_weights(first, [shards[k] for k in GATHER_FIRST])))
    sharded_names = [n for n in SMALL_NAMES if n in SMALL_SHARDED]
    shard_shapes = [params[n].shape for n in sharded_names]
    gathered = _small_exchange(_pack([params[n] for n in sharded_names], _rows_for(shard_shapes)), reduce=False)
    small = {n: params[n] for n in SMALL_NAMES if n not in SMALL_SHARDED}
    for n, g in zip(sharded_names, _unpack(gathered, shard_shapes)):
        small[n] = jnp.moveaxis(g, 0, 2).reshape(SMALL_FULL[n])

    loss, dx, recv, small_grads = _local_step(x[0], mem[0], positions[0], loss_target[0], ready, shards, small)
    loss = lax.psum(loss, ("x", "y", "c"))
    last = [k[0] for k in SCATTER_LAST]
    recv.update(zip(SCATTER_LAST, _scatter_grads(last, [recv[k] for k in SCATTER_LAST])))
    out = {}
    for n in BIG_NAMES:
        shp = params[n].shape
        rows = math.prod(shp[:-1])
        recvs = [recv[k].reshape(N_DEV, -1, shp[-1]) for k in sorted(k for k in ALL_KEYS if k[0] == n)]
        res = _adamw(recvs, params[n].reshape(rows, shp[-1]), mom[n].reshape(rows, shp[-1]),
                     var[n].reshape(rows, shp[-1]), 32, "adamw_" + n)
        out[n] = [t.reshape(shp) for t in res]
    full_shapes = [SMALL_FULL[n] for n in SMALL_NAMES]
    summed = _small_exchange(_pack([small_grads[n] for n in SMALL_NAMES], _rows_for(full_shapes)), reduce=True)
    local_g = []
    for n, g in zip(SMALL_NAMES, _unpack(summed, full_shapes)):
        if n in SMALL_SHARDED:
            size = params[n].shape[2]
            g = lax.dynamic_slice_in_dim(g, me * size, size, axis=2)
        local_g.append(g)
    local_shapes = [params[n].shape for n in SMALL_NAMES]
    rows = _rows_for(local_shapes)
    res = _adamw([_pack(local_g, rows)[None]], _pack([params[n] for n in SMALL_NAMES], rows),
                 _pack([mom[n] for n in SMALL_NAMES], rows), _pack([var[n] for n in SMALL_NAMES], rows), rows,
                 "adamw_small")
    unpacked = [_unpack(t, local_shapes) for t in res]
    for i, n in enumerate(SMALL_NAMES):
        out[n] = [unpacked[k][i] for k in range(4)]

    order = ("a_w_in", "a_conv_w", "a_A_log", "a_dt_bias", "a_norm_w", "b_w_in", "b_sinks", "w_kv_shared",
             "mem_w_kv", "w_o", "mlp_w_up", "mlp_w_down", "ln_g", "ln_b")
    return (loss, dx[None], *[out[n][0] for n in order], *[out[n][1] for n in order],
            *[out[n][2] for n in order], *[out[n][3] for n in order])
```

```python
import functools
import math

import jax
import jax.numpy as jnp
from jax import lax
from jax.experimental import pallas as pl
from jax.experimental.pallas import tpu as pltpu

F32 = jnp.float32
BF16 = jnp.bfloat16

D_MODEL = 1024
DEPTH = 4
N_A = 2
MEM_HEADS = 4
MEM_DH = 64
MEM_W = 256
DN_HEADS = 6
DN_D = 128
DN_W = 768
CHUNK = 64
SWA_DH = 64
SWA_HEADS = 12
WINDOW = 128
ROPE_THETA = 10000.0
LN_EPS = 1e-5
NORM_EPS = 1e-6
DN_ALPHA = (2.0 * DEPTH) ** 0.25
A_IN = 3340
A_IN_PAD = 3456
N_DEV = 8

ADAM_LR = 0.001
ADAM_B1 = 0.9
ADAM_B2 = 0.999
ADAM_EPS = 1e-08
ADAM_WD = 0.01
ADAM_STEP = 10

VMEM_LIMIT = 52 * 1024 * 1024
NEG_BIG = -1e30


def _cparams(sem):
    return pltpu.CompilerParams(dimension_semantics=sem, vmem_limit_bytes=VMEM_LIMIT)


_CONTRACT = {"nn": (1, 0), "nt": (1, 1), "tn": (0, 0)}


def _raw_mm(a, b, mode, prec):
    ca, cb = _CONTRACT[mode]
    dims = (((ca,), (cb,)), ((), ()))
    dot = lambda p, q: lax.dot_general(p, q, dims, preferred_element_type=F32)
    if prec == "bf16":
        return dot(a.astype(BF16), b.astype(BF16))
    a, b = a.astype(F32), b.astype(F32)
    a_hi, b_hi = a.astype(BF16), b.astype(BF16)
    if prec == "sela":
        return dot(a_hi, b_hi) + dot(a_hi, (b - b_hi.astype(F32)).astype(BF16))
    a_lo = (a - a_hi.astype(F32)).astype(BF16)
    if prec == "selb":
        return dot(a_hi, b_hi) + dot(a_lo, b_hi)
    b_lo = (b - b_hi.astype(F32)).astype(BF16)
    return dot(a_hi, b_hi) + (dot(a_hi, b_lo) + dot(a_lo, b_hi))


@functools.partial(jax.custom_vjp, nondiff_argnums=(2, 3))
def mm(a, b, mode, prec):
    return _raw_mm(a, b, mode, prec)


def _mm_fwd(a, b, mode, prec):
    return _raw_mm(a, b, mode, prec), (a, b)


def _mm_bwd(mode, prec, res, ct):
    a, b = res
    if prec == "sela":
        pa, pb = "f32", {"nn": "sela", "nt": "selb", "tn": "sela"}[mode]
    elif prec == "selb":
        pa, pb = {"nn": "selb", "nt": "selb", "tn": "sela"}[mode], "f32"
    else:
        pa = pb = prec
    if mode == "nn":
        return mm(ct, b, "nt", pa), mm(a, ct, "tn", pb)
    if mode == "nt":
        return mm(ct, b, "nn", pa), mm(ct, a, "tn", pb)
    return mm(b, ct, "nt", pa), mm(a, ct, "nn", pb)


mm.defvjp(_mm_fwd, _mm_bwd)


@jax.custom_vjp
def _softplus(x):
    y = jnp.exp(-jnp.abs(x))
    log1p_y = jnp.where(y < 1e-2, y * (1.0 - y * (0.5 - y * (1.0 / 3.0))), jnp.log(1.0 + y))
    return jnp.maximum(x, 0.0) + log1p_y


def _softplus_fwd(x):
    return _softplus(x), x


def _softplus_bwd(x, ct):
    return (ct * jax.nn.sigmoid(x),)


_softplus.defvjp(_softplus_fwd, _softplus_bwd)


def _iota(shape, dim):
    return lax.broadcasted_iota(jnp.int32, shape, dim)


class _Ride:
    def __init__(self, srcs, dst_shapes, src_view, dst_view):
        self.srcs, self.dst_shapes, self.src_view, self.dst_view = list(srcs), list(dst_shapes), src_view, dst_view
        self.n = len(self.srcs)
        self.any_specs = [pl.BlockSpec(memory_space=pl.ANY)] * self.n
        self.scratch = [pltpu.SemaphoreType.DMA((self.n, N_DEV - 1)), pltpu.SemaphoreType.DMA((self.n, N_DEV - 1)),
                        pltpu.SemaphoreType.DMA((self.n,))]

    def copies(self, src_refs, dst_refs, sems):
        send_sems, recv_sems, local_sems = sems
        me, peers = _me_and_peers()
        local, out, inc = [], [], []
        for t in range(self.n):
            local.append(pltpu.make_async_copy(self.src_view(t, src_refs[t], me), self.dst_view(t, dst_refs[t], me),
                                               local_sems.at[t]))
            for k, (dev, idx) in enumerate(peers):
                mk = lambda s, d: pltpu.make_async_remote_copy(
                    src_ref=s, dst_ref=d, send_sem=send_sems.at[t, k], recv_sem=recv_sems.at[t, k], device_id=dev,
                    device_id_type=pl.DeviceIdType.MESH)
                out.append(mk(self.src_view(t, src_refs[t], idx), self.dst_view(t, dst_refs[t], me)))
                inc.append(mk(self.src_view(t, src_refs[t], me), self.dst_view(t, dst_refs[t], idx)))
        return local, out, inc

    def start(self, grid, src_refs, dst_refs, sems):
        first = pl.program_id(0) == 0
        for a in range(1, len(grid)):
            first = jnp.logical_and(first, pl.program_id(a) == 0)

        @pl.when(first)
        def _():
            local, out, _ = self.copies(src_refs, dst_refs, sems)
            for cp in local + out:
                cp.start()

    def finish(self, grid, src_refs, dst_refs, sems):
        last = pl.program_id(0) == grid[0] - 1
        for a in range(1, len(grid)):
            last = jnp.logical_and(last, pl.program_id(a) == grid[a] - 1)

        @pl.when(last)
        def _():
            local, out, inc = self.copies(src_refs, dst_refs, sems)
            for cp in inc:
                cp.wait_recv()
            for cp in out:
                cp.wait_send()
            for cp in local:
                cp.wait()


def _block_fwd(fn, ins, in_specs, out_shapes, out_specs, grid, name, ride=None):
    n_in, n_out = len(ins), len(out_shapes)
    n_ride = ride.n if ride else 0

    def body(*refs):
        pids = tuple(pl.program_id(a) for a in range(len(grid)))
        ride_refs = (refs[n_in:n_in + n_ride], refs[n_in + n_ride + n_out:n_in + 2 * n_ride + n_out],
                     refs[n_in + 2 * n_ride + n_out:])
        if ride:
            ride.start(grid, *ride_refs)
        vals = [r[...].astype(F32) for r in refs[:n_in]]
        outs = fn(pids, *vals)
        for r, o in zip(refs[n_in + n_ride:n_in + n_ride + n_out], outs):
            r[...] = o.astype(r.dtype)
        if ride:
            ride.finish(grid, *ride_refs)

    if not ride:
        return pl.pallas_call(
            body, grid=grid, in_specs=in_specs, out_specs=out_specs, out_shape=out_shapes, name=name,
            compiler_params=_cparams(("parallel",) * len(grid)))(*ins)
    res = pl.pallas_call(
        body, grid=grid, in_specs=list(in_specs) + ride.any_specs, out_specs=list(out_specs) + ride.any_specs,
        out_shape=list(out_shapes) + ride.dst_shapes, scratch_shapes=ride.scratch, name=name,
        compiler_params=_cparams(("arbitrary",) * len(grid)))(*ins, *ride.srcs)
    return res[:n_out], res[n_out:]


def _block_bwd(fn, ins, in_specs, cts, ct_specs, kinds, g_shapes, g_specs, grid, name, ride=None):
    n_in, n_ct, n_g = len(ins), len(cts), len(g_shapes)
    n_ride = ride.n if ride else 0
    didx = [i for i, k in enumerate(kinds) if k]

    def body(*refs):
        in_refs, ct_refs = refs[:n_in], refs[n_in:n_in + n_ct]
        base = n_in + n_ct
        g_refs = refs[base + n_ride:base + n_ride + n_g]
        ride_refs = (refs[base:base + n_ride], refs[base + n_ride + n_g:base + 2 * n_ride + n_g],
                     refs[base + 2 * n_ride + n_g:])
        if ride:
            ride.start(grid, *ride_refs)
        pids = tuple(pl.program_id(a) for a in range(len(grid)))
        vals = [r[...].astype(F32) for r in in_refs]

        def f(*dvals):
            full = list(vals)
            for i, v in zip(didx, dvals):
                full[i] = v
            return tuple(fn(pids, *full))

        _, vjp = jax.vjp(f, *[vals[i] for i in didx])
        gs = vjp(tuple(r[...].astype(F32) for r in ct_refs))
        first = pids[0] == 0
        for p in pids[1:]:
            first = jnp.logical_and(first, p == 0)
        for i, g, r in zip(didx, gs, g_refs):
            if kinds[i] == "s":
                r[...] = g.astype(r.dtype)
            else:
                @pl.when(first)
                def _(r=r):
                    r[...] = jnp.zeros(r.shape, r.dtype)

                r[...] += g.astype(r.dtype)
        if ride:
            ride.finish(grid, *ride_refs)

    sem = ("arbitrary",) * len(grid) if "a" in kinds or ride else ("parallel",) * len(grid)
    if not ride:
        return pl.pallas_call(
            body, grid=grid, in_specs=list(in_specs) + list(ct_specs), out_specs=g_specs, out_shape=g_shapes,
            name=name, compiler_params=_cparams(sem))(*ins, *cts)
    res = pl.pallas_call(
        body, grid=grid, in_specs=list(in_specs) + list(ct_specs) + ride.any_specs,
        out_specs=list(g_specs) + ride.any_specs, out_shape=list(g_shapes) + ride.dst_shapes,
        scratch_shapes=ride.scratch, name=name, compiler_params=_cparams(sem))(*ins, *cts, *ride.srcs)
    return res[:n_g], res[n_g:]


def _rows(tb, width, col=0):
    return pl.BlockSpec((tb, width), lambda i, col=col: (i, col))


def _whole(shape):
    return pl.BlockSpec(shape, lambda *_: (0,) * len(shape))


def _sds(shape, dtype=F32):
    return jax.ShapeDtypeStruct(shape, dtype)


def _matmul(a, b, mode, out_dtypes, name, epi=None, extras=(), tm=1024, tn=1024, tk=1024,
            b_spec=None, n_total=None, out_specs=None, out_shapes=None):
    if mode == "tn":
        k_total, m_total = a.shape
    else:
        m_total, k_total = a.shape
    if n_total is None:
        n_total = b.shape[0] if mode == "nt" else b.shape[1]
    tm, tn, tk = min(tm, m_total), min(tn, n_total), min(tk, k_total)
    assert m_total % tm == 0 and n_total % tn == 0 and k_total % tk == 0, (name, a.shape, b.shape)
    grid = (m_total // tm, n_total // tn, k_total // tk)
    nk = grid[2]
    if mode == "tn":
        a_spec = pl.BlockSpec((tk, tm), lambda i, j, k: (k, i))
    else:
        a_spec = pl.BlockSpec((tm, tk), lambda i, j, k: (i, k))
    if b_spec is None:
        if mode == "nt":
            b_spec = pl.BlockSpec((tn, tk), lambda i, j, k: (j, k))
        else:
            b_spec = pl.BlockSpec((tk, tn), lambda i, j, k: (k, j))
    tile = pl.BlockSpec((tm, tn), lambda i, j, k: (i, j))
    n_ex, n_out = len(extras), len(out_dtypes)
    ca, cb = _CONTRACT[mode]
    dims = (((ca,), (cb,)), ((), ()))

    chunk = 256
    n_chunks = tm // chunk if (epi is not None and mode != "tn" and tm % chunk == 0) else 1

    def body(*refs):
        a_ref, b_ref = refs[:2]
        ex_refs = refs[2:2 + n_ex]
        out_refs = refs[2 + n_ex:2 + n_ex + n_out]
        acc = refs[-1] if nk > 1 else None
        full = lambda: lax.dot_general(a_ref[...], b_ref[...], dims, preferred_element_type=F32)

        def last_step():
            for c in range(n_chunks):
                rows = pl.ds(c * chunk, chunk) if n_chunks > 1 else slice(None)
                val = (lax.dot_general(a_ref[rows, :], b_ref[...], dims, preferred_element_type=F32)
                       if n_chunks > 1 else full())
                if acc is not None:
                    val = val + acc[rows, :]
                res = epi(val, *[e[rows, :] for e in ex_refs]) if epi is not None else (val,)
                for r, o in zip(out_refs, res):
                    r[rows, :] = o.astype(r.dtype)

        if nk == 1:
            last_step()
        else:
            k = pl.program_id(2)

            @pl.when(k == 0)
            def _():
                acc[...] = full()

            if nk > 2:
                @pl.when(jnp.logical_and(k > 0, k < nk - 1))
                def _():
                    acc[...] += full()

            @pl.when(k == nk - 1)
            def _():
                last_step()

    if out_shapes is None:
        out_shapes = [_sds((m_total, n_total), d) for d in out_dtypes]
        out_specs = [tile] * n_out
    outs = pl.pallas_call(
        body, grid=grid, in_specs=[a_spec, b_spec] + [tile] * n_ex, out_specs=out_specs, out_shape=out_shapes,
        scratch_shapes=[pltpu.VMEM((tm, tn), F32)] if nk > 1 else [], name=name,
        compiler_params=_cparams(("parallel", "parallel", "arbitrary")))(a, b, *extras)
    return outs if n_out > 1 else outs[0]


def _matmul_ln(a, b, h, g, beta, name, tm=1024, tk=2048, chunk=256):
    m_total, k_total = a.shape
    n = b.shape[1]
    tm, tk = min(tm, m_total), min(tk, k_total)
    chunk = min(chunk, tm)
    assert m_total % tm == 0 and k_total % tk == 0 and tm % chunk == 0, (name, a.shape, b.shape)
    nk = k_total // tk
    dims = (((1,), (0,)), ((), ()))

    def body(a_ref, b_ref, h_ref, g_ref, beta_ref, mix_ref, y_ref, yb_ref, *acc):
        k = pl.program_id(1)

        def finish():
            for c in range(tm // chunk):
                rows = pl.ds(chunk * c, chunk)
                val = lax.dot_general(a_ref[rows, :], b_ref[...], dims, preferred_element_type=F32)
                if nk > 1:
                    val = val + acc[0][rows, :]
                mix_ref[rows, :] = val
                (y,) = _ln_fn(None, h_ref[rows, :], val, g_ref[...], beta_ref[...])
                y_ref[rows, :] = y
                yb_ref[rows, :] = y.astype(BF16)

        if nk == 1:
            finish()
        else:
            part = lambda: lax.dot_general(a_ref[...], b_ref[...], dims, preferred_element_type=F32)

            @pl.when(k == 0)
            def _():
                acc[0][...] = part()

            if nk > 2:
                @pl.when(jnp.logical_and(k > 0, k < nk - 1))
                def _():
                    acc[0][...] += part()

            @pl.when(k == nk - 1)
            def _():
                finish()

    tile = pl.BlockSpec((tm, n), lambda i, k: (i, 0))
    row = pl.BlockSpec((1, n), lambda i, k: (0, 0))
    return pl.pallas_call(
        body, grid=(m_total // tm, nk),
        in_specs=[pl.BlockSpec((tm, tk), lambda i, k: (i, k)), pl.BlockSpec((tk, n), lambda i, k: (k, 0)), tile, row,
                  row],
        out_specs=[tile, tile, tile],
        out_shape=[_sds((m_total, n)), _sds((m_total, n)), _sds((m_total, n), BF16)],
        scratch_shapes=[pltpu.VMEM((tm, n), F32)] if nk > 1 else [], name=name,
        compiler_params=_cparams(("parallel", "arbitrary")))(a, b, h, g, beta)


def _silu(x):
    return x * jax.nn.sigmoid(x)


def _rowa_fn(pids, c, ab, alog, dtb):
    tb = c.shape[0]
    s = _silu(c)
    qs, ks = [], []
    for h in range(DN_HEADS):
        qh = s[:, DN_D * h:DN_D * (h + 1)]
        qs.append(qh * lax.rsqrt(jnp.sum(qh * qh, axis=-1, keepdims=True) + NORM_EPS) * (DN_D ** -0.5))
        kh = s[:, DN_W + DN_D * h:DN_W + DN_D * (h + 1)]
        ks.append(kh * lax.rsqrt(jnp.sum(kh * kh, axis=-1, keepdims=True) + NORM_EPS))
    q = jnp.concatenate(qs, axis=1)
    k = jnp.concatenate(ks, axis=1)
    v = s[:, 2 * DN_W:3 * DN_W]
    g128 = -jnp.exp(alog) * _softplus(ab + dtb)
    b128 = jax.nn.sigmoid(ab)
    r, cc = _iota((tb, tb), 0), _iota((tb, tb), 1)
    tri = jnp.where(((r >> 6) == (cc >> 6)) & (r >= cc), 1.0, 0.0)
    gc128 = mm(tri, g128, "nn", "sela")
    lane, col = _iota((128, DN_W), 0), _iota((128, DN_W), 1)
    exp_a = jnp.where(lane == (col >> 7), 1.0, 0.0)
    exp_b = jnp.where(lane == (col >> 7) + DN_HEADS, 1.0, 0.0)
    return q, k, v, mm(gc128, exp_a, "nn", "selb"), mm(b128, exp_b, "nn", "selb")


def _tri_inv_raw(lows, block):
    n = lows[0].shape[0]
    r, c = _iota((n, n), 0), _iota((n, n), 1)
    lg = 0
    xs = None
    while (1 << lg) < block:
        off = ((r >> (lg + 1)) == (c >> (lg + 1))) & (((r >> lg) & 1) == 1) & (((c >> lg) & 1) == 0)
        cblks = [jnp.where(off, low, 0.0) for low in lows]
        if xs is None:
            xs = [jnp.where(r == c, 1.0, 0.0) - cb for cb in cblks]
        else:
            ys = [mm(cb, x, "nn", "f32") for cb, x in zip(cblks, xs)]
            xs = [x - mm(x, y, "nn", "f32") for x, y in zip(xs, ys)]
        lg += 1
    return tuple(xs)


def _tri_inv_cotangent(block, xs, cts):
    n = xs[0].shape[0]
    r, c = _iota((n, n), 0), _iota((n, n), 1)
    shift = block.bit_length() - 1
    keep = ((r >> shift) == (c >> shift)) & (r > c)
    gs = [mm(x, ct, "tn", "f32") for x, ct in zip(xs, cts)]
    gs = [mm(g, x, "nt", "f32") for g, x in zip(gs, xs)]
    return tuple(jnp.where(keep, -g, 0.0) for g in gs)


@functools.partial(jax.custom_vjp, nondiff_argnums=(1,))
def _tri_inv(lows, block):
    return _tri_inv_raw(lows, block)


def _tri_inv_fwd(lows, block):
    xs = _tri_inv_raw(lows, block)
    return xs, xs


def _tri_inv_bwd(block, xs, cts):
    return (_tri_inv_cotangent(block, xs, cts),)


_tri_inv.defvjp(_tri_inv_fwd, _tri_inv_bwd)


@functools.partial(jax.custom_vjp, nondiff_argnums=(2,))
def _tri_inv_known(lows, known, block):
    return known


def _tri_inv_known_fwd(lows, known, block):
    return known, known


def _tri_inv_known_bwd(block, xs, cts):
    return _tri_inv_cotangent(block, xs, cts), tuple(jnp.zeros_like(x) for x in xs)


_tri_inv_known.defvjp(_tri_inv_known_fwd, _tri_inv_known_bwd)


PAIR = 2 * CHUNK


def _dn1_pairs(q, k, v, gc, beta, tinv_known=None):
    assert PAIR == DN_D
    n = PAIR
    pairs = range(q.shape[0] // n)
    cut = lambda t: [t[n * j:n * (j + 1)] for j in pairs]
    q, k, v, gc, beta = cut(q), cut(k), cut(v), cut(gc), cut(beta)
    onehot = jnp.where(_iota((n, DN_D), 1) == 0, 1.0, 0.0)
    r, c = _iota((n, n), 0), _iota((n, n), 1)
    same = (r >> 6) == (c >> 6)
    incl, strict = same & (r >= c), same & (r > c)
    row = _iota((n, DN_D), 0)
    eg = [jnp.exp(g) for g in gc]
    kb = [k[j] * beta[j] for j in pairs]
    g_row = [mm(onehot, g, "nt", "sela") for g in gc]
    kk = [mm(kb[j], k[j], "nt", "bf16") for j in pairs]
    qk = [mm(q[j], k[j], "nt", "bf16") for j in pairs]
    decay = [jnp.exp(jnp.where(incl, gc[j] - g_row[j], NEG_BIG)) for j in pairs]
    low = tuple(jnp.where(strict, kk[j] * decay[j], 0.0) for j in pairs)
    if tinv_known is None:
        tinv = _tri_inv(low, CHUNK)
    else:
        tinv = _tri_inv_known(low, tuple(cut(tinv_known)), CHUNK)
    uw = [mm(tinv[j], jnp.concatenate([v[j] * beta[j], kb[j] * eg[j]], axis=1), "nn", "f32") for j in pairs]
    intra = [jnp.where(incl, qk[j] * decay[j], 0.0) for j in pairs]
    g_last = []
    for g in gc:
        last0 = jnp.sum(jnp.where(row == CHUNK - 1, g, 0.0), axis=0, keepdims=True)
        last1 = jnp.sum(jnp.where(row == PAIR - 1, g, 0.0), axis=0, keepdims=True)
        g_last.append(jnp.where(row < CHUNK, last0, last1))
    join = lambda parts: jnp.concatenate(parts, axis=0)
    return (join([t[:, :DN_D] for t in uw]), join([t[:, DN_D:] for t in uw]), join(intra),
            join([q[j] * eg[j] for j in pairs]), join([k[j] * jnp.exp(g_last[j] - gc[j]) for j in pairs]),
            join([jnp.exp(g) for g in g_last]), join(list(tinv)))


def _dn1_fn(pids, q, k, v, gc, beta):
    return _dn1_pairs(q, k, v, gc, beta)


def _dn1_fn_known(pids, q, k, v, gc, beta, tinv):
    return _dn1_pairs(q, k, v, gc, beta, tinv)[:6]


def _dn2_step(half, state, qd, kd, u, w, intra, cd_row):
    heads = range(len(state))
    v_new = [u[h] - mm(w[h], state[h], "nn", "bf16") for h in heads]
    zeros = jnp.zeros_like(v_new[0])
    v_pair = [jnp.concatenate([v, zeros] if half == 0 else [zeros, v], axis=0) for v in v_new]
    from_state = [mm(qd[h], state[h], "nn", "bf16") for h in heads]
    out = tuple(from_state[h] + mm(intra[h], v_pair[h], "nn", "bf16") for h in heads)
    return out, tuple(state[h] * cd_row[h] + mm(kd[h], v_new[h], "tn", "bf16") for h in heads)


def _post_fn(pids, o, z, nw):
    outs = []
    for h in range(DN_HEADS):
        oh = o[:, DN_D * h:DN_D * (h + 1)]
        zh = z[:, DN_D * h:DN_D * (h + 1)]
        y = oh * lax.rsqrt(jnp.mean(oh * oh, axis=-1, keepdims=True) + NORM_EPS) * nw
        outs.append(y * _silu(zh))
    return (jnp.concatenate(outs, axis=1),)


def _memattn_fn(pids, qm, kvm):
    kmem, vmem = kvm[:, :MEM_W], kvm[:, MEM_W:]
    lane = _iota((1, MEM_W), 1)
    heads = range(MEM_HEADS)
    hm = [jnp.where((lane >> 6) == h, 1.0, 0.0) for h in heads]
    s = [mm(qm * (hm[h] * MEM_DH ** -0.5), kmem, "nt", "bf16") for h in heads]
    e = [jnp.exp(t - lax.stop_gradient(jnp.max(t, axis=-1, keepdims=True))) for t in s]
    o = [mm(e[h], vmem, "nn", "bf16") * (hm[h] / jnp.sum(e[h], axis=-1, keepdims=True)) for h in heads]
    return ((o[0] + o[1]) + (o[2] + o[3]),)


def _ln_fn(pids, h, mix, g, b):
    x = DN_ALPHA * h + mix
    mu = jnp.mean(x, axis=-1, keepdims=True)
    xc = x - mu
    var = jnp.mean(xc * xc, axis=-1, keepdims=True)
    return (xc * lax.rsqrt(var + LN_EPS) * g + b,)


def _rope_matrix():
    i, j = _iota((128, 128), 0), _iota((128, 128), 1)
    jj = j & 63
    return jnp.where((jj < 32) & (i == j + 32), -1.0, 0.0) + jnp.where((jj >= 32) & (i == j - 32), 1.0, 0.0)


def _rope128(x, cos, sin, rot):
    return x * cos + mm(x, rot, "nn", "selb") * sin


def _krope_fn(pids, kraw, cos, sin):
    rot = _rope_matrix()
    return (jnp.concatenate([_rope128(kraw[:, 128 * g:128 * (g + 1)], cos, sin, rot) for g in range(2)], axis=1),)


def _swa_fn(pids, qraw, cos, sin, k_halo, k_cur, v_halo, v_cur, sinks):
    tb = qraw.shape[0]
    nwin = tb // WINDOW
    rot = _rope_matrix()
    kcat = jnp.concatenate([k_halo, k_cur], axis=0)
    vcat = jnp.concatenate([v_halo, v_cur], axis=0)
    lane = _iota((1, 128), 1)
    halves = (jnp.where(lane < 64, 1.0, 0.0), jnp.where(lane >= 64, 1.0, 0.0))
    group = SWA_HEADS // 2
    rows = group * WINDOW
    in_cur = _iota((rows, WINDOW), 1) <= (_iota((rows, WINDOW), 0) & (WINDOW - 1))
    qg = [_rope128(qraw[:, 128 * p:128 * (p + 1)], cos, sin, rot) for p in range(group)]
    sink = []
    for kv in range(2):
        cols = [jnp.sum(jnp.where(lane == group * kv + i, sinks, 0.0), axis=-1, keepdims=True)
                + jnp.zeros((WINDOW, 1), F32) for i in range(group)]
        sink.append(jnp.concatenate(cols, axis=0))
    units = [(w, kv) for w in range(nwin) for kv in range(2)]
    n_units = range(len(units))
    q6 = [jnp.concatenate([qg[3 * kv + i // 2][WINDOW * w:WINDOW * (w + 1)] * (halves[i % 2] * SWA_DH ** -0.5)
                           for i in range(group)], axis=0) for w, kv in units]
    blk = lambda cat, w, kv: cat[WINDOW * w:WINDOW * (w + 1), 128 * kv:128 * (kv + 1)]
    s_prev = [mm(q6[u], blk(kcat, w, kv), "nt", "bf16") for u, (w, kv) in enumerate(units)]
    s_cur = [mm(q6[u], blk(kcat, w + 1, kv), "nt", "bf16") for u, (w, kv) in enumerate(units)]
    s = [jnp.where(in_cur, s_cur[u], jnp.where(pids[0] * nwin + w > 0, s_prev[u], NEG_BIG))
         for u, (w, kv) in enumerate(units)]
    m = [lax.stop_gradient(jnp.maximum(jnp.max(s[u], axis=-1, keepdims=True), sink[kv]))
         for u, (w, kv) in enumerate(units)]
    e = [jnp.exp(s[u] - m[u]) for u in n_units]
    denom = [jnp.sum(e[u], axis=-1, keepdims=True) + jnp.exp(sink[kv] - m[u]) for u, (w, kv) in enumerate(units)]
    o = [(mm(jnp.where(in_cur, e[u], 0.0), blk(vcat, w + 1, kv), "nn", "bf16")
          + mm(jnp.where(in_cur, 0.0, e[u]), blk(vcat, w, kv), "nn", "bf16")) / denom[u]
         for u, (w, kv) in enumerate(units)]
    out_rows = []
    for w in range(nwin):
        lanes = []
        for p in range(group):
            ou = o[units.index((w, p // 3))]
            i = 2 * (p % 3)
            lanes.append(ou[WINDOW * i:WINDOW * (i + 1)] * halves[0] + ou[WINDOW * (i + 1):WINDOW * (i + 2)] * halves[1])
        out_rows.append(jnp.concatenate(lanes, axis=1))
    return (jnp.concatenate(out_rows, axis=0),)


def _conv_fwd(proj, conv_w, tb):
    t_total = proj.shape[0]
    width = conv_w.shape[1]
    nb = t_total // tb

    def body(cur_ref, prev_ref, w_ref, out_ref):
        i = pl.program_id(0)
        prev = jnp.where(i > 0, prev_ref[...], 0.0)
        xcat = jnp.concatenate([prev, cur_ref[...]], axis=0)
        acc = xcat[8:] * w_ref[3:4, :]
        for j in range(3):
            acc = acc + pltpu.roll(xcat, 3 - j, 0)[8:] * w_ref[j:j + 1, :]
        out_ref[...] = acc

    return pl.pallas_call(
        body, grid=(nb,),
        in_specs=[pl.BlockSpec((tb, width), lambda i: (i, 0)),
                  pl.BlockSpec((8, width), lambda i: (jnp.maximum(i * (tb // 8) - 1, 0), 0)),
                  _whole((4, width))],
        out_specs=pl.BlockSpec((tb, width), lambda i: (i, 0)), out_shape=_sds((t_total, width)),
        name="conv_fwd", compiler_params=_cparams(("parallel",)))(proj, proj, conv_w)


def _conv_bwd(dc, proj, conv_w, tb):
    t_total, width = dc.shape
    nb = t_total // tb

    def body(dcur_ref, dnext_ref, cur_ref, prev_ref, w_ref, dx_ref, dw_ref):
        i = pl.program_id(0)
        dnext = jnp.where(i < nb - 1, dnext_ref[...], 0.0)
        dcur = dcur_ref[...]
        dcat = jnp.concatenate([dcur, dnext], axis=0)
        prev = jnp.where(i > 0, prev_ref[...], 0.0)
        xcat = jnp.concatenate([prev, cur_ref[...]], axis=0)

        @pl.when(i == 0)
        def _():
            dw_ref[...] = jnp.zeros(dw_ref.shape, F32)

        dx = dcur * w_ref[3:4, :]
        dw_ref[3:4, :] += jnp.sum(dcur * xcat[8:], axis=0, keepdims=True)
        for j in range(3):
            dx = dx + pltpu.roll(dcat, 8 - (3 - j), 0)[8:] * w_ref[j:j + 1, :]
            dw_ref[j:j + 1, :] += jnp.sum(dcur * pltpu.roll(xcat, 3 - j, 0)[8:], axis=0, keepdims=True)
        dx_ref[...] = dx.astype(dx_ref.dtype)

    return pl.pallas_call(
        body, grid=(nb,),
        in_specs=[pl.BlockSpec((tb, width), lambda i: (i, 0)),
                  pl.BlockSpec((8, width), lambda i: (jnp.minimum((i + 1) * (tb // 8), t_total // 8 - 1), 0)),
                  pl.BlockSpec((tb, width), lambda i: (i, 0)),
                  pl.BlockSpec((8, width), lambda i: (jnp.maximum(i * (tb // 8) - 1, 0), 0)),
                  _whole((4, width))],
        out_specs=[pl.BlockSpec((tb, width), lambda i: (i, 0)), _whole((4, width))],
        out_shape=[_sds((t_total, width), BF16), _sds((4, width))],
        name="conv_bwd", compiler_params=_cparams(("arbitrary",)))(dc, dc, proj, proj, conv_w)


def _head_spec(tb, nb=None):
    if nb is None:
        return pl.BlockSpec((tb, DN_D), lambda h, i: (i, h))
    return pl.BlockSpec((tb, DN_D), lambda h, i: (nb - 1 - i, h))


def _intra_spec(tb, nb=None):
    if nb is None:
        return pl.BlockSpec((None, tb, PAIR), lambda h, i: (h, i, 0))
    return pl.BlockSpec((None, tb, PAIR), lambda h, i: (h, nb - 1 - i, 0))


def _scan_specs(tb, nb=None):
    blk = (lambda i: i) if nb is None else (lambda i: nb - 1 - i)
    rows = pl.BlockSpec((tb, DN_W), lambda i: (blk(i), 0))
    pair = pl.BlockSpec((DN_HEADS, tb, PAIR), lambda i: (0, blk(i), 0))
    states = pl.BlockSpec((DN_HEADS, tb // CHUNK, DN_D, DN_D), lambda i: (0, blk(i), 0, 0))
    return rows, pair, states


def _dn2_fwd(qd, kd, u, w, intra, cd, tb):
    t_total = qd.shape[0]
    rows, pair, states = _scan_specs(tb)

    def body(qd_ref, kd_ref, u_ref, w_ref, a_ref, cd_ref, o_ref, save_ref, state):
        @pl.when(pl.program_id(0) == 0)
        def _():
            state[...] = jnp.zeros(state.shape, F32)

        heads = range(DN_HEADS)
        lanes = [pl.ds(DN_D * h, DN_D) for h in heads]
        for j in range(tb // CHUNK):
            sl = pl.ds(CHUNK * j, CHUNK)
            s0 = tuple(state[h] for h in heads)
            for h in heads:
                save_ref[h, j] = s0[h]
            per_head = lambda ref: tuple(ref[sl, lanes[h]] for h in heads)
            out, s1 = _dn2_step(j % 2, s0, per_head(qd_ref), per_head(kd_ref), per_head(u_ref), per_head(w_ref),
                                tuple(a_ref[h, sl, :] for h in heads),
                                tuple(cd_ref[pl.ds(CHUNK * j, 1), lanes[h]] for h in heads))
            for h in heads:
                o_ref[sl, lanes[h]] = out[h]
                state[h] = s1[h]

    return pl.pallas_call(
        body, grid=(t_total // tb,), in_specs=[rows, rows, rows, rows, pair, rows],
        out_specs=[rows, states],
        out_shape=[_sds((t_total, DN_W)), _sds((DN_HEADS, t_total // CHUNK, DN_D, DN_D))],
        scratch_shapes=[pltpu.VMEM((DN_HEADS, DN_D, DN_D), F32)], name="dn2_fwd",
        compiler_params=_cparams(("arbitrary",)))(qd, kd, u, w, intra, cd)


def _dn2_bwd(qd, kd, u, w, intra, cd, saved, d_o, tb):
    t_total = qd.shape[0]
    nb = t_total // tb
    rows, pair, states = _scan_specs(tb, nb)

    def body(qd_ref, kd_ref, u_ref, w_ref, a_ref, cd_ref, save_ref, do_ref,
             dqd_ref, dkd_ref, du_ref, dw_ref, da_ref, dcd_ref, dstate):
        @pl.when(pl.program_id(0) == 0)
        def _():
            dstate[...] = jnp.zeros(dstate.shape, F32)

        first_row = _iota((CHUNK, DN_D), 0) == 0
        heads = range(DN_HEADS)
        lanes = [pl.ds(DN_D * h, DN_D) for h in heads]
        for j in reversed(range(tb // CHUNK)):
            sl = pl.ds(CHUNK * j, CHUNK)
            per_head = lambda ref: tuple(ref[sl, lanes[h]] for h in heads)
            _, vjp = jax.vjp(functools.partial(_dn2_step, j % 2), tuple(save_ref[h, j] for h in heads),
                             per_head(qd_ref), per_head(kd_ref), per_head(u_ref), per_head(w_ref),
                             tuple(a_ref[h, sl, :] for h in heads),
                             tuple(cd_ref[pl.ds(CHUNK * j, 1), lanes[h]] for h in heads))
            ds0, dqd, dkd, du, dw, da, dcd = vjp((per_head(do_ref), tuple(dstate[h] for h in heads)))
            for h in heads:
                dqd_ref[sl, lanes[h]] = dqd[h]
                dkd_ref[sl, lanes[h]] = dkd[h]
                du_ref[sl, lanes[h]] = du[h]
                dw_ref[sl, lanes[h]] = dw[h]
                da_ref[h, sl, :] = da[h]
                dcd_ref[sl, lanes[h]] = jnp.where(first_row, dcd[h], 0.0)
                dstate[h] = ds0[h]

    full = _sds((t_total, DN_W))
    return pl.pallas_call(
        body, grid=(nb,),
        in_specs=[rows, rows, rows, rows, pair, rows, states, rows],
        out_specs=[rows, rows, rows, rows, pair, rows],
        out_shape=[full, full, full, full, _sds((DN_HEADS, t_total, PAIR)), full],
        scratch_shapes=[pltpu.VMEM((DN_HEADS, DN_D, DN_D), F32)], name="dn2_bwd",
        compiler_params=_cparams(("arbitrary",)))(qd, kd, u, w, intra, cd, saved, d_o)


def _loss_and_grad(y, target, tb):
    t_total, d = y.shape

    def body(y_ref, t_ref, dy_ref, acc_ref):
        @pl.when(pl.program_id(0) == 0)
        def _():
            acc_ref[...] = jnp.zeros(acc_ref.shape, F32)

        err = y_ref[...] - t_ref[...]
        dy_ref[...] = err * (1.0 / d)
        acc_ref[...] += jnp.sum(err * err, axis=0, keepdims=True)

    dy, acc = pl.pallas_call(
        body, grid=(t_total // tb,), in_specs=[_rows(tb, d), _rows(tb, d)],
        out_specs=[_rows(tb, d), _whole((1, d))], out_shape=[_sds((t_total, d)), _sds((1, d))],
        name="loss", compiler_params=_cparams(("arbitrary",)))(y, target)
    return 0.5 * jnp.sum(acc) / d, dy


def _halo_sum(mains, halos, tb):
    t_total, width = mains[0].shape
    nb = t_total // tb
    n = len(mains)

    def body(*refs):
        out_ref = refs[-1]
        i = pl.program_id(0)
        tot = refs[0][...]
        for r in refs[1:n]:
            tot = tot + r[...]
        hal = refs[n][...]
        for r in refs[n + 1:2 * n]:
            hal = hal + r[...]
        hal = jnp.where(i < nb - 1, hal, 0.0)
        out_ref[...] = tot + jnp.concatenate([jnp.zeros((tb - WINDOW, width), F32), hal], axis=0)

    return pl.pallas_call(
        body, grid=(nb,),
        in_specs=[_rows(tb, width)] * n
        + [pl.BlockSpec((None, WINDOW, width), lambda i: (jnp.minimum(i + 1, nb - 1), 0, 0))] * n,
        out_specs=_rows(tb, width), out_shape=_sds((t_total, width)), name="halo_sum",
        compiler_params=_cparams(("parallel",)))(*mains, *halos)


def _adamw(recvs, w, m, v, tr, name):
    slots, _, c_total = recvs[0].shape
    r_total = w.shape[0]
    assert sum(r.shape[1] for r in recvs) == r_total
    tr = min([tr] + [r.shape[1] for r in recvs])
    assert all(r.shape[1] % tr == 0 for r in recvs)
    starts = [sum(r.shape[1] for r in recvs[:i]) // tr for i in range(len(recvs))]
    counts = [r.shape[1] // tr for r in recvs]
    c1 = 1.0 / (1.0 - ADAM_B1 ** ADAM_STEP)
    c2 = 1.0 / (1.0 - ADAM_B2 ** ADAM_STEP)

    def body(*refs):
        recv_refs = refs[:len(recvs)]
        w_ref, m_ref, v_ref, g_ref, d_ref, nm_ref, nv_ref = refs[len(recvs):]
        g = None
        for recv_ref, start in zip(recv_refs, starts):
            part = recv_ref[0].astype(F32)
            for s in range(1, slots):
                part = part + recv_ref[s].astype(F32)
            g = part if g is None else jnp.where(pl.program_id(0) >= start, part, g)
        nm = ADAM_B1 * m_ref[...] + (1.0 - ADAM_B1) * g
        nv = ADAM_B2 * v_ref[...] + (1.0 - ADAM_B2) * (g * g)
        g_ref[...] = g
        nm_ref[...] = nm
        nv_ref[...] = nv
        d_ref[...] = -ADAM_LR * ((nm * c1) / (jnp.sqrt(nv * c2) + ADAM_EPS) + ADAM_WD * w_ref[...])

    blk = pl.BlockSpec((tr, c_total), lambda i: (i, 0))
    recv_specs = [pl.BlockSpec((slots, tr, c_total), lambda i, s=s, n=n: (0, jnp.clip(i - s, 0, n - 1), 0))
                  for s, n in zip(starts, counts)]
    return pl.pallas_call(
        body, grid=(r_total // tr,), in_specs=recv_specs + [blk, blk, blk],
        out_specs=[blk] * 4, out_shape=[_sds((r_total, c_total))] * 4, name=name,
        compiler_params=_cparams(("parallel",)))(*recvs, w, m, v)


def _me_and_peers():
    x, y, c = lax.axis_index("x"), lax.axis_index("y"), lax.axis_index("c")
    me = 4 * x + 2 * y + c
    peers = []
    for k in range(1, N_DEV):
        px = 1 - x if (k >> 2) & 1 else x
        py = 1 - y if (k >> 1) & 1 else y
        pc = 1 - c if k & 1 else c
        peers.append(((px, py, pc), 4 * px + 2 * py + pc))
    return me, peers


def _small_exchange(packed, reduce):
    r_total = packed.shape[0]

    def body(p_ref, out_ref, gath_ref, send_sems, recv_sems):
        me, peers = _me_and_peers()
        gath_ref[me] = p_ref[...]
        copies = []
        for k, (dev, _) in enumerate(peers):
            cp = pltpu.make_async_remote_copy(src_ref=p_ref, dst_ref=gath_ref.at[me], send_sem=send_sems.at[k],
                                              recv_sem=recv_sems.at[k], device_id=dev,
                                              device_id_type=pl.DeviceIdType.MESH)
            cp.start()
            copies.append(cp)
        for k, (dev, idx) in enumerate(peers):
            pltpu.make_async_remote_copy(src_ref=p_ref, dst_ref=gath_ref.at[idx], send_sem=send_sems.at[k],
                                         recv_sem=recv_sems.at[k], device_id=dev,
                                         device_id_type=pl.DeviceIdType.MESH).wait_recv()
        for cp in copies:
            cp.wait_send()
        if reduce:
            tot = gath_ref[0]
            for d in range(1, N_DEV):
                tot = tot + gath_ref[d]
            out_ref[...] = tot
        else:
            out_ref[...] = gath_ref[...]

    out_shape = _sds((r_total, 128)) if reduce else _sds((N_DEV, r_total, 128))
    return pl.pallas_call(
        body, in_specs=[pl.BlockSpec(memory_space=pltpu.VMEM)], out_specs=pl.BlockSpec(memory_space=pltpu.VMEM),
        out_shape=out_shape,
        scratch_shapes=[pltpu.VMEM((N_DEV, r_total, 128), F32), pltpu.SemaphoreType.DMA((N_DEV - 1,)),
                        pltpu.SemaphoreType.DMA((N_DEV - 1,))],
        name="small_allreduce" if reduce else "small_allgather")(packed)


def _slot(ref, axis, idx, size):
    sel = [slice(None)] * len(ref.shape)
    sel[axis] = idx if size is None else pl.ds(pl.multiple_of(idx * size, size), size)
    return ref.at[tuple(sel)]


def _big_exchange(srcs, dst_shapes, src_view, dst_view, name):
    n = len(srcs)

    def body(*refs):
        src_refs, dst_refs = refs[:n], refs[n:2 * n]
        send_sems, recv_sems, local_sems = refs[2 * n:]
        me, peers = _me_and_peers()
        local, remote = [], []
        for t in range(n):
            loc = pltpu.make_async_copy(src_view(t, src_refs[t], me), dst_view(t, dst_refs[t], me), local_sems.at[t])
            loc.start()
            local.append(loc)
            for k, (dev, idx) in enumerate(peers):
                cp = pltpu.make_async_remote_copy(
                    src_ref=src_view(t, src_refs[t], idx), dst_ref=dst_view(t, dst_refs[t], me),
                    send_sem=send_sems.at[t, k], recv_sem=recv_sems.at[t, k], device_id=dev,
                    device_id_type=pl.DeviceIdType.MESH)
                cp.start()
                remote.append(cp)
        for t in range(n):
            for k, (dev, idx) in enumerate(peers):
                pltpu.make_async_remote_copy(
                    src_ref=src_view(t, src_refs[t], me), dst_ref=dst_view(t, dst_refs[t], idx),
                    send_sem=send_sems.at[t, k], recv_sem=recv_sems.at[t, k], device_id=dev,
                    device_id_type=pl.DeviceIdType.MESH).wait_recv()
        for cp in remote:
            cp.wait_send()
        for cp in local:
            cp.wait()

    any_spec = pl.BlockSpec(memory_space=pl.ANY)
    return pl.pallas_call(
        body, in_specs=[any_spec] * n, out_specs=[any_spec] * n, out_shape=dst_shapes,
        scratch_shapes=[pltpu.SemaphoreType.DMA((n, N_DEV - 1)), pltpu.SemaphoreType.DMA((n, N_DEV - 1)),
                        pltpu.SemaphoreType.DMA((n,))],
        name=name)(*srcs)


BIG = {
    "a_w_in": (1, (2, 1024, A_IN)),
    "b_w_in": (1, (2, 1024, 1024)),
    "w_kv_shared": (0, (1024, 256)),
    "mem_w_kv": (1, (4, 1024, 512)),
    "w_o": (1, (4, 1024, 1024)),
    "mlp_w_up": (2, (4, 1024, 4096)),
    "mlp_w_down": (1, (4, 4096, 1024)),
}
BIG_NAMES = tuple(BIG)


def _gather_plan(names, shards):
    dst_shapes, axes, sizes = [], [], []
    for name, s in zip(names, shards):
        axis = BIG[name][0] - (len(BIG[name][1]) - s.ndim)
        dst_shapes.append(_sds(tuple(d * N_DEV if a == axis else d for a, d in enumerate(s.shape)), s.dtype))
        axes.append(axis)
        sizes.append(s.shape[axis])
    return dst_shapes, axes, sizes


def _gather_ride(names, shards):
    dst_shapes, axes, sizes = _gather_plan(names, shards)
    return _Ride(shards, dst_shapes, lambda t, ref, idx: ref, lambda t, ref, idx: _slot(ref, axes[t], idx, sizes[t]))


def _scatter_plan(names, grads):
    dst_shapes, axes, sizes = [], [], []
    for name, g in zip(names, grads):
        axis = BIG[name][0] - (len(BIG[name][1]) - g.ndim)
        shard = tuple(d // N_DEV if a == axis else d for a, d in enumerate(g.shape))
        axes.append(axis)
        sizes.append(shard[axis])
        dst_shapes.append(_sds((N_DEV,) + shard, g.dtype))
    return dst_shapes, axes, sizes


def _scatter_ride(names, grads):
    dst_shapes, axes, sizes = _scatter_plan(names, grads)
    return _Ride(grads, dst_shapes, lambda t, ref, idx: _slot(ref, axes[t], idx, sizes[t]),
                 lambda t, ref, idx: ref.at[idx])


def _allgather_weights(names, shards):
    dst_shapes, axes, sizes = _gather_plan(names, shards)
    n = len(shards)

    def body(*refs):
        src_refs, dst_refs = refs[:n], refs[n:2 * n]
        send_sems, recv_sems, local_sems = refs[2 * n:]
        x, y, c = lax.axis_index("x"), lax.axis_index("y"), lax.axis_index("c")
        sibling = (x, y, 1 - c)
        chips = [(1 - x, y), (x, 1 - y), (1 - x, 1 - y)]
        index = lambda px, py, pc: 4 * px + 2 * py + pc

        def copy(t, k, block, to, src=None):
            rows = _slot(dst_refs[t], axes[t], index(*block), sizes[t])
            return pltpu.make_async_remote_copy(
                src_ref=rows if src is None else src, dst_ref=rows, send_sem=send_sems.at[t, k],
                recv_sem=recv_sems.at[t, k], device_id=to, device_id_type=pl.DeviceIdType.MESH)

        started, local = [], []
        for t in range(n):
            mine = pltpu.make_async_copy(src_refs[t], _slot(dst_refs[t], axes[t], index(x, y, c), sizes[t]),
                                         local_sems.at[t])
            mine.start()
            local.append(mine)
            first = [copy(t, 0, (x, y, c), sibling, src=src_refs[t])]
            first += [copy(t, 1 + j, (x, y, c), (*chip, c), src=src_refs[t]) for j, chip in enumerate(chips)]
            for cp in first:
                cp.start()
            started += first
        for t in range(n):
            for j, chip in enumerate(chips):
                copy(t, 1 + j, (*chip, c), (x, y, c)).wait_recv()
                passed = copy(t, 4 + j, (*chip, c), sibling)
                passed.start()
                started.append(passed)
        for t in range(n):
            copy(t, 0, sibling, (x, y, c)).wait_recv()
            for j, chip in enumerate(chips):
                copy(t, 4 + j, (*chip, 1 - c), (x, y, c)).wait_recv()
        for cp in started:
            cp.wait_send()
        for cp in local:
            cp.wait()

    any_spec = pl.BlockSpec(memory_space=pl.ANY)
    return pl.pallas_call(
        body, in_specs=[any_spec] * n, out_specs=[any_spec] * n, out_shape=dst_shapes,
        scratch_shapes=[pltpu.SemaphoreType.DMA((n, N_DEV - 1)), pltpu.SemaphoreType.DMA((n, N_DEV - 1)),
                        pltpu.SemaphoreType.DMA((n,))],
        name="allgather_weights")(*shards)


def _scatter_grads(names, grads):
    dst_shapes, axes, sizes = _scatter_plan(names, grads)

    def src_view(t, ref, idx):
        return _slot(ref, axes[t], idx, sizes[t])

    def dst_view(t, ref, idx):
        return ref.at[idx]

    return _big_exchange(grads, dst_shapes, src_view, dst_view, "scatter_grads")


def _pad_row(vec, width=128):
    return jnp.pad(vec.astype(F32), (0, width - vec.shape[0])).reshape(1, width)


def _block_sizes(t_total):
    return dict(row=min(256, t_total), dn=min(512, t_total), swa=min(256, t_total), scan=min(256, t_total),
                ln=min(512, t_total))


def _ln_grad(h, mix, g, b, dy, tb):
    t_total, d = h.shape

    def body(h_ref, mix_ref, g_ref, dy_ref, dh_ref, dmix_ref, dg_ref, db_ref):
        @pl.when(pl.program_id(0) == 0)
        def _():
            dg_ref[...] = jnp.zeros(dg_ref.shape, F32)
            db_ref[...] = jnp.zeros(db_ref.shape, F32)

        x = DN_ALPHA * h_ref[...] + mix_ref[...]
        xc = x - jnp.mean(x, axis=-1, keepdims=True)
        rstd = lax.rsqrt(jnp.mean(xc * xc, axis=-1, keepdims=True) + LN_EPS)
        xhat = xc * rstd
        dy_val = dy_ref[...]
        dxh = dy_val * g_ref[...]
        m1 = jnp.mean(dxh, axis=-1, keepdims=True)
        m2 = jnp.mean(dxh * xhat, axis=-1, keepdims=True)
        dx = (dxh - m1 - xhat * m2) * rstd
        dmix_ref[...] = dx.astype(dmix_ref.dtype)
        dh_ref[...] = DN_ALPHA * dx
        dg_ref[...] += jnp.sum(dy_val * xhat, axis=0, keepdims=True)
        db_ref[...] += jnp.sum(dy_val, axis=0, keepdims=True)

    return pl.pallas_call(
        body, grid=(t_total // tb,),
        in_specs=[_rows(tb, d), _rows(tb, d), _whole((1, d)), _rows(tb, d)],
        out_specs=[_rows(tb, d), _rows(tb, d), _whole((1, d)), _whole((1, d))],
        out_shape=[_sds((t_total, d)), _sds((t_total, d), BF16), _sds((1, d)), _sds((1, d))],
        name="ln_bwd", compiler_params=_cparams(("arbitrary",)))(h, mix, g, dy)


def _memattn_specs(tb, qcol):
    return [pl.BlockSpec((tb, MEM_W), lambda i: (i, qcol)), _whole((MEM_W, 2 * MEM_W))]


def _act_epilogue(acc):
    r = jnp.maximum(acc, 0.0)
    return (r * r,)


def _dact_epilogue(acc, act):
    return (acc * (2.0 * jnp.sqrt(act.astype(F32))),)


def _add_epilogue(acc, other):
    return (acc + other,)


def _key(name, layer):
    if name == "w_kv_shared":
        return (name, None)
    return (name, layer - N_A if name == "b_w_in" else layer)


_PER_LAYER = ("mem_w_kv", "w_o", "mlp_w_up", "mlp_w_down")
GATHER_FIRST = [_key("a_w_in", 0)]
GATHER_ON_DN1_0 = ([_key(n, 0) for n in _PER_LAYER] + [_key("a_w_in", 1), _key("w_kv_shared", 1)]
                   + [_key(n, 1) for n in _PER_LAYER])
GATHER_ON_DN1_1 = [_key("b_w_in", 2), _key("b_w_in", 3)] + [_key(n, l) for l in (2, 3) for n in _PER_LAYER]
SCATTER_ON_DN1_BWD_1 = GATHER_ON_DN1_1
SCATTER_ON_ROWA_BWD_0 = [_key(n, 0) for n in _PER_LAYER]
SCATTER_ON_DN1_BWD_0 = [k for k in GATHER_ON_DN1_0 if k not in SCATTER_ON_ROWA_BWD_0]
SCATTER_LAST = GATHER_FIRST
ALL_KEYS = GATHER_FIRST + GATHER_ON_DN1_0 + GATHER_ON_DN1_1


def _local_step(x, mem, positions, target, ready, shards, small):
    t_total = x.shape[0]
    bs = _block_sizes(t_total)
    tb, tdn, tsw = bs["row"], bs["dn"], bs["swa"]
    nb = t_total // tb
    nbs = t_total // tsw

    inv_freq = ROPE_THETA ** (-jnp.arange(0, SWA_DH, 2, dtype=F32) / SWA_DH)
    ang = positions.astype(F32)[:, None] * inv_freq
    cos = jnp.tile(jnp.cos(ang), (1, 4))
    sin = jnp.tile(jnp.sin(ang), (1, 4))

    mem_b = mem.astype(BF16)
    ready = dict(ready)
    derived = {}

    def gather_ride(keys):
        names = [k[0] for k in keys]
        return names, (_gather_ride(names, [shards[k] for k in keys]) if shards is not None else None)

    def weight(name, l):
        key = _key(name, l)
        if key not in derived:
            w = ready[key]
            if name == "a_w_in":
                w = jnp.concatenate([w[0][:, :3072], w[0][:, 3084:], w[0][:, 3072:3084],
                                     jnp.zeros((D_MODEL, A_IN_PAD - A_IN), BF16)], axis=1)
            elif name == "w_kv_shared":
                w = jnp.concatenate([w[:, 64 * (i // 2):64 * (i // 2 + 1)] for i in range(8)], axis=1)
            else:
                w = w[0]
            derived[key] = w
        return derived[key]

    saved = []
    h, hb = x, x.astype(BF16)
    kr = vd_src = None
    for l in range(DEPTH):
        sv = dict(h=h, hb=hb)
        if l < N_A:
            proj = _matmul(hb, weight("a_w_in", l), "nn", [F32], "mm_proj_a", tn=1152)
            conv_w = small["a_conv_w"][l]
            c = _conv_fwd(proj, conv_w, tb)
            alog, dtb = _pad_row(small["a_A_log"][l]), _pad_row(small["a_dt_bias"][l])
            rowa_in = [c, proj, alog, dtb]
            rowa_specs = [_rows(tb, 3 * DN_W), _rows(tb, 128, 26), _whole((1, 128)), _whole((1, 128))]
            rowa_args = (_rowa_fn, rowa_in, rowa_specs, [_sds((t_total, DN_W))] * 5, [_rows(tb, DN_W)] * 5, (nb,))
            q, k, v, gcb, betab = _block_fwd(*rowa_args, "rowa_fwd")
            hs = _head_spec(tdn)
            dn_grid = (DN_HEADS, t_total // tdn)
            full = _sds((t_total, DN_W))
            full_b = _sds((t_total, DN_W), BF16)
            dn1_out_shapes = [full, full_b, _sds((DN_HEADS, t_total, PAIR), BF16), full_b, full_b, full,
                              _sds((DN_HEADS, t_total, PAIR))]
            dn1_out_specs = [hs, hs, _intra_spec(tdn), hs, hs, hs, _intra_spec(tdn)]
            if shards is not None:
                keys = GATHER_ON_DN1_0 if l == 0 else GATHER_ON_DN1_1
                names, ride = gather_ride(keys)
                (u, w, intra, qd, kd, cd, tinv), got = _block_fwd(
                    _dn1_fn, [q, k, v, gcb, betab], [hs] * 5, dn1_out_shapes, dn1_out_specs, dn_grid,
                    "dn1_fwd_gather%d" % l, ride=ride)
                ready.update(zip(keys, got))
            else:
                u, w, intra, qd, kd, cd, tinv = _block_fwd(_dn1_fn, [q, k, v, gcb, betab], [hs] * 5, dn1_out_shapes,
                                                           dn1_out_specs, dn_grid, "dn1_fwd")
            o, states = _dn2_fwd(qd, kd, u, w, intra, cd, bs["scan"])
            nw = small["a_norm_w"][l].reshape(1, DN_D)
            post_in = [o, proj, nw]
            post_specs = [_rows(tb, DN_W), _rows(tb, DN_W, 3), _whole((1, DN_D))]
            (og,) = _block_fwd(_post_fn, post_in, post_specs, [_sds((t_total, DN_W), BF16)], [_rows(tb, DN_W)],
                               (nb,), "post_fwd")
            qm_col = 12
            sv.update(proj=proj, c=c, rowa_in=rowa_in, rowa_specs=rowa_specs, dn1_in=[q, k, v, gcb, betab, tinv],
                      dn2_in=[qd, kd, u, w, intra, cd], states=states, post_in=post_in, post_specs=post_specs,
                      conv_w=conv_w)
        else:
            jb = l - N_A
            proj = _matmul(hb, weight("b_w_in", l), "nn", [F32], "mm_proj_b")
            sinks = _pad_row(small["b_sinks"][jb])
            swa_in = [proj, cos, sin, kr, kr, vd_src, vd_src, sinks]
            swa_specs = [_rows(tsw, DN_W), _rows(tsw, 128), _rows(tsw, 128),
                         pl.BlockSpec((WINDOW, 256), lambda i: (jnp.maximum(i * (tsw // WINDOW) - 1, 0), 0)),
                         _rows(tsw, 256),
                         pl.BlockSpec((WINDOW, 256), lambda i: (jnp.maximum(i * (tsw // WINDOW) - 1, 0), 1)),
                         _rows(tsw, 256, 1), _whole((1, 128))]
            (og,) = _block_fwd(_swa_fn, swa_in, swa_specs, [_sds((t_total, DN_W), BF16)], [_rows(tsw, DN_W)],
                               (nbs,), "swa_fwd")
            qm_col = 3
            sv.update(proj=proj, swa_in=swa_in, swa_specs=swa_specs)
        kvm = _matmul(mem_b, weight("mem_w_kv", l), "nn", [F32], "mm_memkv", tm=256)
        mem_in = [proj, kvm]
        (mo,) = _block_fwd(_memattn_fn, mem_in, _memattn_specs(tb, qm_col), [_sds((t_total, MEM_W), BF16)],
                           [_rows(tb, MEM_W)], (nb,), "memattn_fwd")
        mixin = jnp.concatenate([og, mo], axis=1)
        g0, b0 = small["ln_g"][l, 0].reshape(1, -1), small["ln_b"][l, 0].reshape(1, -1)
        mix, h1, h1b = _matmul_ln(mixin, weight("w_o", l), h, g0, b0, "mm_wo_ln")
        act = _matmul(h1b, weight("mlp_w_up", l), "nn", [BF16], "mm_up", epi=_act_epilogue, tm=2048)
        g1, b1 = small["ln_g"][l, 1].reshape(1, -1), small["ln_b"][l, 1].reshape(1, -1)
        mlp, h2, h2b = _matmul_ln(act, weight("mlp_w_down", l), h1, g1, b1, "mm_down_ln")
        sv.update(kvm=kvm, mem_in=mem_in, qm_col=qm_col, mixin=mixin, mix=mix, ln0=(g0, b0), h1=h1, h1b=h1b,
                  act=act, mlp=mlp, ln1=(g1, b1))
        saved.append(sv)
        h, hb = h2, h2b
        if l == N_A - 1:
            kvd = _matmul(hb, weight("w_kv_shared", l), "nn", [F32], "mm_kvd")
            krope_in = [kvd, cos, sin]
            krope_specs = [_rows(tb, 256), _rows(tb, 128), _rows(tb, 128)]
            (kr,) = _block_fwd(_krope_fn, krope_in, krope_specs, [_sds((t_total, 256))], [_rows(tb, 256)], (nb,),
                               "krope_fwd")
            vd_src = kvd

    loss, dh = _loss_and_grad(h, target, tb)

    grads = {}

    def scatter_ride(keys):
        return _scatter_ride([k[0] for k in keys], [grads[k] for k in keys]) if shards is not None else None

    sg = dict(a_conv_w=[None] * N_A, a_A_log=[None] * N_A, a_dt_bias=[None] * N_A, a_norm_w=[None] * N_A,
              b_sinks=[None] * (DEPTH - N_A), ln_g=[[None, None] for _ in range(DEPTH)],
              ln_b=[[None, None] for _ in range(DEPTH)])
    dk_parts, dv_parts = [], []
    for l in reversed(range(DEPTH)):
        sv = saved[l]
        if l == N_A - 1:
            dkr = _halo_sum([p[0] for p in dk_parts], [p[1] for p in dk_parts], tsw)
            dvv = _halo_sum([p[0] for p in dv_parts], [p[1] for p in dv_parts], tsw)
            (dkraw,) = _block_bwd(_krope_fn, krope_in, krope_specs, [dkr], [_rows(tb, 256)], ["s", None, None],
                                  [_sds((t_total, 256), BF16)], [_rows(tb, 256)], (nb,), "krope_bwd")
            dkvd = jnp.concatenate([dkraw, dvv.astype(BF16)], axis=1)
            g_kvd = _matmul(saved[l + 1]["hb"], dkvd, "tn", [F32], "mm_dw_kvd", tm=1024, tn=512)
            dh = _matmul(dkvd, weight("w_kv_shared", l), "nt", [F32], "mm_dx_kvd", epi=_add_epilogue, extras=[dh],
                         tn=1024, tk=512)
            grads[_key("w_kv_shared", l)] = jnp.concatenate(
                [g_kvd[:, 128 * i:128 * i + 64] + g_kvd[:, 128 * i + 64:128 * (i + 1)] for i in range(4)],
                axis=1).astype(BF16)
        g1, b1 = sv["ln1"]
        dh1a, dmlp, dg1, db1 = _ln_grad(sv["h1"], sv["mlp"], g1, b1, dh, bs["ln"])
        dup = _matmul(dmlp, weight("mlp_w_down", l), "nt", [BF16], "mm_dact", epi=_dact_epilogue, extras=[sv["act"]],
                      tm=2048)
        grads[_key("mlp_w_down", l)] = _matmul(sv["act"], dmlp, "tn", [BF16], "mm_dw_down", tk=4096)[None]
        grads[_key("mlp_w_up", l)] = _matmul(sv["h1b"], dup, "tn", [BF16], "mm_dw_up", tk=4096)[None]
        dh1 = _matmul(dup, weight("mlp_w_up", l), "nt", [F32], "mm_dx_up", epi=_add_epilogue, extras=[dh1a], tk=2048)
        g0, b0 = sv["ln0"]
        dha, dmix, dg0, db0 = _ln_grad(sv["h"], sv["mix"], g0, b0, dh1, bs["ln"])
        sg["ln_g"][l] = [dg0, dg1]
        sg["ln_b"][l] = [db0, db1]
        grads[_key("w_o", l)] = _matmul(sv["mixin"], dmix, "tn", [BF16], "mm_dw_o", tk=4096)[None]
        dmixin = _matmul(dmix, weight("w_o", l), "nt", [F32], "mm_dx_o", tn=1024)
        dqm, dkvm = _block_bwd(_memattn_fn, sv["mem_in"], _memattn_specs(tb, sv["qm_col"]), [dmixin],
                               [_rows(tb, MEM_W, 3)], ["s", "a"],
                               [_sds((t_total, MEM_W), BF16), _sds((MEM_W, 2 * MEM_W))],
                               [_rows(tb, MEM_W), _whole((MEM_W, 2 * MEM_W))], (nb,), "memattn_bwd")
        grads[_key("mem_w_kv", l)] = _matmul(mem_b, dkvm.astype(BF16), "tn", [BF16], "mm_dw_memkv", tm=1024,
                                             tn=512)[None]
        if l < N_A:
            d_o, dz, dnw = _block_bwd(_post_fn, sv["post_in"], sv["post_specs"], [dmixin], [_rows(tb, DN_W)],
                                      ["s", "s", "a"],
                                      [_sds((t_total, DN_W)), _sds((t_total, DN_W), BF16), _sds((1, DN_D))],
                                      [_rows(tb, DN_W), _rows(tb, DN_W), _whole((1, DN_D))], (nb,), "post_bwd")
            sg["a_norm_w"][l] = dnw
            dqd, dkd, du, dw, da, dcd = _dn2_bwd(*sv["dn2_in"], sv["states"], d_o, bs["scan"])
            hs = _head_spec(tdn)
            full = _sds((t_total, DN_W))
            dn1_bwd_args = (_dn1_fn_known, sv["dn1_in"], [hs] * 5 + [_intra_spec(tdn)], [du, dw, da, dqd, dkd, dcd],
                            [hs, hs, _intra_spec(tdn), hs, hs, hs], ["s"] * 5 + [None], [full] * 5, [hs] * 5,
                            (DN_HEADS, t_total // tdn))
            if shards is not None:
                keys = SCATTER_ON_DN1_BWD_1 if l == N_A - 1 else SCATTER_ON_DN1_BWD_0
                (dq, dk, dv, dgc, dbeta), got = _block_bwd(*dn1_bwd_args, "dn1_bwd_scatter%d" % l,
                                                           ride=scatter_ride(keys))
                grads.update(zip(keys, got))
            else:
                dq, dk, dv, dgc, dbeta = _block_bwd(*dn1_bwd_args, "dn1_bwd")
            rowa_bwd_args = (
                _rowa_fn, sv["rowa_in"], sv["rowa_specs"], [dq, dk, dv, dgc, dbeta], [_rows(tb, DN_W)] * 5,
                ["s", "s", "a", "a"],
                [_sds((t_total, 3 * DN_W)), _sds((t_total, 128), BF16), _sds((1, 128)), _sds((1, 128))],
                [_rows(tb, 3 * DN_W), _rows(tb, 128), _whole((1, 128)), _whole((1, 128))], (nb,))
            if shards is not None and l == 0:
                (dc, dab, dalog, ddtb), got = _block_bwd(*rowa_bwd_args, "rowa_bwd_scatter",
                                                         ride=scatter_ride(SCATTER_ON_ROWA_BWD_0))
                grads.update(zip(SCATTER_ON_ROWA_BWD_0, got))
            else:
                dc, dab, dalog, ddtb = _block_bwd(*rowa_bwd_args, "rowa_bwd")
            sg["a_A_log"][l] = dalog[0, :DN_HEADS]
            sg["a_dt_bias"][l] = ddtb[0, :DN_HEADS]
            dx, dconv = _conv_bwd(dc, sv["proj"], sv["conv_w"], tb)
            sg["a_conv_w"][l] = dconv
            dproj = jnp.concatenate([dx, dz, dqm, dab], axis=1)
            g_in = _matmul(sv["hb"], dproj, "tn", [BF16], "mm_dw_a", tm=1024, tn=1152, tk=2048)
            grads[_key("a_w_in", l)] = jnp.concatenate([g_in[:, :3072], g_in[:, 3328:3340], g_in[:, 3072:3328]],
                                                       axis=1)[None]
            dh = _matmul(dproj, weight("a_w_in", l), "nt", [F32], "mm_dx_a", epi=_add_epilogue, extras=[dha], tk=1152)
        else:
            jb = l - N_A
            swa_kinds = ["s", None, None, "s", "s", "s", "s", "a"]
            halo_spec = pl.BlockSpec((None, WINDOW, 256), lambda i: (i, 0, 0))
            dq, dkh, dkc, dvh, dvc, dsink = _block_bwd(
                _swa_fn, sv["swa_in"], sv["swa_specs"], [dmixin], [_rows(tsw, DN_W)], swa_kinds,
                [_sds((t_total, DN_W), BF16), _sds((nbs, WINDOW, 256)), _sds((t_total, 256)),
                 _sds((nbs, WINDOW, 256)), _sds((t_total, 256)), _sds((1, 128))],
                [_rows(tsw, DN_W), halo_spec, _rows(tsw, 256), halo_spec, _rows(tsw, 256), _whole((1, 128))],
                (nbs,), "swa_bwd")
            sg["b_sinks"][jb] = dsink[0, :SWA_HEADS]
            dk_parts.append((dkc, dkh))
            dv_parts.append((dvc, dvh))
            dproj = jnp.concatenate([dq, dqm], axis=1)
            grads[_key("b_w_in", l)] = _matmul(sv["hb"], dproj, "tn", [BF16], "mm_dw_b", tk=4096)[None]
            dh = _matmul(dproj, weight("b_w_in", l), "nt", [F32], "mm_dx_b", epi=_add_epilogue, extras=[dha], tn=1024)

    small_grads = dict(
        a_conv_w=jnp.stack(sg["a_conv_w"]), a_A_log=jnp.stack(sg["a_A_log"]), a_dt_bias=jnp.stack(sg["a_dt_bias"]),
        a_norm_w=jnp.concatenate(sg["a_norm_w"], axis=0), b_sinks=jnp.stack(sg["b_sinks"]),
        ln_g=jnp.stack([jnp.concatenate(p, axis=0) for p in sg["ln_g"]]),
        ln_b=jnp.stack([jnp.concatenate(p, axis=0) for p in sg["ln_b"]]))
    return loss, dh, grads, small_grads


def _pack(arrays, rows):
    flat = []
    for a in arrays:
        v = a.astype(F32).reshape(-1)
        flat.append(jnp.pad(v, (0, (-v.shape[0]) % 128)))
    flat = jnp.concatenate(flat)
    return jnp.pad(flat, (0, rows * 128 - flat.shape[0])).reshape(rows, 128)


def _unpack(slab, shapes):
    flat = slab.reshape(slab.shape[:-2] + (-1,))
    out, off = [], 0
    for s in shapes:
        n = math.prod(s)
        out.append(flat[..., off:off + n].reshape(slab.shape[:-2] + tuple(s)))
        off += n + (-n) % 128
    return out


def _rows_for(shapes):
    rows = sum((math.prod(s) + 127) // 128 for s in shapes)
    return rows + (-rows) % 8


SMALL_NAMES = ("a_conv_w", "a_A_log", "a_dt_bias", "a_norm_w", "b_sinks", "ln_g", "ln_b")
SMALL_SHARDED = {"a_conv_w": 2, "ln_g": 2, "ln_b": 2}
SMALL_FULL = {"a_conv_w": (2, 4, 2304), "a_A_log": (2, 6), "a_dt_bias": (2, 6), "a_norm_w": (2, 128),
              "b_sinks": (2, 12), "ln_g": (4, 2, 1024), "ln_b": (4, 2, 1024)}


def kernel(x, mem, positions, a_w_in, a_conv_w, a_A_log, a_dt_bias, a_norm_w, b_w_in, b_sinks, w_kv_shared, mem_w_kv, w_o, mlp_w_up, mlp_w_down, ln_g, ln_b, loss_target, m_a_w_in, m_a_conv_w, m_a_A_log, m_a_dt_bias, m_a_norm_w, m_b_w_in, m_b_sinks, m_w_kv_shared, m_mem_w_kv, m_w_o, m_mlp_w_up, m_mlp_w_down, m_ln_g, m_ln_b, v_a_w_in, v_a_conv_w, v_a_A_log, v_a_dt_bias, v_a_norm_w, v_b_w_in, v_b_sinks, v_w_kv_shared, v_mem_w_kv, v_w_o, v_mlp_w_up, v_mlp_w_down, v_ln_g, v_ln_b):
    params = dict(a_w_in=a_w_in, a_conv_w=a_conv_w, a_A_log=a_A_log, a_dt_bias=a_dt_bias, a_norm_w=a_norm_w,
                  b_w_in=b_w_in, b_sinks=b_sinks, w_kv_shared=w_kv_shared, mem_w_kv=mem_w_kv, w_o=w_o,
                  mlp_w_up=mlp_w_up, mlp_w_down=mlp_w_down, ln_g=ln_g, ln_b=ln_b)
    mom = dict(a_w_in=m_a_w_in, a_conv_w=m_a_conv_w, a_A_log=m_a_A_log, a_dt_bias=m_a_dt_bias, a_norm_w=m_a_norm_w,
               b_w_in=m_b_w_in, b_sinks=m_b_sinks, w_kv_shared=m_w_kv_shared, mem_w_kv=m_mem_w_kv, w_o=m_w_o,
               mlp_w_up=m_mlp_w_up, mlp_w_down=m_mlp_w_down, ln_g=m_ln_g, ln_b=m_ln_b)
    var = dict(a_w_in=v_a_w_in, a_conv_w=v_a_conv_w, a_A_log=v_a_A_log, a_dt_bias=v_a_dt_bias, a_norm_w=v_a_norm_w,
               b_w_in=v_b_w_in, b_sinks=v_b_sinks, w_kv_shared=v_w_kv_shared, mem_w_kv=v_mem_w_kv, w_o=v_w_o,
               mlp_w_up=v_mlp_w_up, mlp_w_down=v_mlp_w_down, ln_g=v_ln_g, ln_b=v_ln_b)
    me = 4 * lax.axis_index("x") + 2 * lax.axis_index("y") + lax.axis_index("c")

    shards = {(n, i): (params[n] if i is None else params[n][i:i + 1]).astype(BF16) for n, i in ALL_KEYS}
    first = [k[0] for k in GATHER_FIRST]
    ready = dict(zip(GATHER_FIRST, _allgather_weights(first, [shards[k] for k in GATHER_FIRST])))
    sharded_names = [n for n in SMALL_NAMES if n in SMALL_SHARDED]
    shard_shapes = [params[n].shape for n in sharded_names]
    gathered = _small_exchange(_pack([params[n] for n in sharded_names], _rows_for(shard_shapes)), reduce=False)
    small = {n: params[n] for n in SMALL_NAMES if n not in SMALL_SHARDED}
    for n, g in zip(sharded_names, _unpack(gathered, shard_shapes)):
        small[n] = jnp.moveaxis(g, 0, 2).reshape(SMALL_FULL[n])

    loss, dx, recv, small_grads = _local_step(x[0], mem[0], positions[0], loss_target[0], ready, shards, small)
    loss = lax.psum(loss, ("x", "y", "c"))
    last = [k[0] for k in SCATTER_LAST]
    recv.update(zip(SCATTER_LAST, _scatter_grads(last, [recv[k] for k in SCATTER_LAST])))
    out = {}
    for n in BIG_NAMES:
        shp = params[n].shape
        rows = math.prod(shp[:-1])
        recvs = [recv[k].reshape(N_DEV, -1, shp[-1]) for k in sorted(k for k in ALL_KEYS if k[0] == n)]
        res = _adamw(recvs, params[n].reshape(rows, shp[-1]), mom[n].reshape(rows, shp[-1]),
                     var[n].reshape(rows, shp[-1]), 32, "adamw_" + n)
        out[n] = [t.reshape(shp) for t in res]
    full_shapes = [SMALL_FULL[n] for n in SMALL_NAMES]
    summed = _small_exchange(_pack([small_grads[n] for n in SMALL_NAMES], _rows_for(full_shapes)), reduce=True)
    local_g = []
    for n, g in zip(SMALL_NAMES, _unpack(summed, full_shapes)):
        if n in SMALL_SHARDED:
            size = params[n].shape[2]
            g = lax.dynamic_slice_in_dim(g, me * size, size, axis=2)
        local_g.append(g)
    local_shapes = [params[n].shape for n in SMALL_NAMES]
    rows = _rows_for(local_shapes)
    res = _adamw([_pack(local_g, rows)[None]], _pack([params[n] for n in SMALL_NAMES], rows),
                 _pack([mom[n] for n in SMALL_NAMES], rows), _pack([var[n] for n in SMALL_NAMES], rows), rows,
                 "adamw_small")
    unpacked = [_unpack(t, local_shapes) for t in res]
    for i, n in enumerate(SMALL_NAMES):
        out[n] = [unpacked[k][i] for k in range(4)]

    order = ("a_w_in", "a_conv_w", "a_A_log", "a_dt_bias", "a_norm_w", "b_w_in", "b_sinks", "w_kv_shared",
             "mem_w_kv", "w_o", "mlp_w_up", "mlp_w_down", "ln_g", "ln_b")
    return (loss, dx[None], *[out[n][0] for n in order], *[out[n][1] for n in order],
            *[out[n][2] for n in order], *[out[n][3] for n in order])
```
